```python
import math
import jax, jax.numpy as jnp
from jax import lax
import numpy as np

D_MODEL = 1024
BATCH = 16
SEQ = 256
DEPTH = 4
DEC_BATCH = 2
DEC_SEQ = 1024
PAST_LEN = 256

GRID_W = 64
ATT_HEADS = 8
ATT_KV_HEADS = 2
ATT_GROUP = ATT_HEADS // ATT_KV_HEADS
HEAD_DIM = 64
WINDOW = 128
ATT_BLOCK = 128
ROPE_BASE = 10000.0
GLA_HEADS = 4
GLA_DK = 64
GLA_DV = 128
GLA_RANK = 16
GLA_TAU = 16.0
GLA_CHUNK = 64
POOL_GROUPS = 4
POOL_GROUP_DIM = 128
POOL_WINDOWS = (2, 4, 8, 16)
D_FF = -(-8 * D_MODEL // (3 * 256)) * 256
ATT_Q = ATT_HEADS * HEAD_DIM
ATT_KV = ATT_KV_HEADS * HEAD_DIM
GLA_QK = GLA_HEADS * GLA_DK
GLA_VW = GLA_HEADS * GLA_DV
POOL_W = POOL_GROUPS * POOL_GROUP_DIM
IN_SPLITS = (ATT_Q, ATT_KV, ATT_KV, GLA_QK, GLA_QK, GLA_VW, GLA_VW, GLA_RANK, GLA_RANK, POOL_W, D_MODEL, D_MODEL, D_MODEL)
IN_WIDTH = sum(IN_SPLITS)
EPS = 1e-6
NEG = -1e30

kernel_name = 'hybrid_dit_prefix_ctx_step'


def _rmsnorm(x, g):
    xf = x.astype(jnp.float32)
    y = xf * lax.rsqrt(jnp.mean(xf * xf, axis=-1, keepdims=True) + EPS)
    return (y * g.astype(jnp.float32)).astype(x.dtype)


def _rope_1d(x, pos):
    half = x.shape[-1] // 2
    inv_freq = ROPE_BASE ** (-jnp.arange(half, dtype=jnp.float32) / half)
    ang = pos[:, None] * inv_freq[None, :]
    cos = jnp.cos(ang)[None, :, None, :]
    sin = jnp.sin(ang)[None, :, None, :]
    xf = x.astype(jnp.float32)
    x1, x2 = xf[..., :half], xf[..., half:]
    return jnp.concatenate([x1 * cos - x2 * sin, x2 * cos + x1 * sin], axis=-1).astype(x.dtype)


def _rope_2d(x):
    T = x.shape[1]
    rows = T // GRID_W
    row = jnp.repeat(jnp.arange(rows, dtype=jnp.float32), GRID_W)
    col = jnp.tile(jnp.arange(GRID_W, dtype=jnp.float32), rows)
    h = HEAD_DIM // 2
    return jnp.concatenate([_rope_1d(x[..., :h], row), _rope_1d(x[..., h:], col)], axis=-1)


def _sink_softmax_av(s, v, sink):
    sk = sink.astype(jnp.float32)[None, :, :, None, None]
    m = jnp.maximum(jnp.max(s, axis=-1, keepdims=True), sk)
    p = jnp.exp(s - m)
    p = p / (jnp.sum(p, axis=-1, keepdims=True) + jnp.exp(sk - m))
    return jnp.einsum('bkgqs,bskd->bqkgd', p, v.astype(jnp.float32))


def _ctx_attention(q, k, v, sink):
    B, T = q.shape[0], q.shape[1]
    nb = T // ATT_BLOCK
    scale = HEAD_DIM ** -0.5
    qb = q.astype(jnp.float32).reshape(B, nb, ATT_BLOCK, ATT_KV_HEADS, ATT_GROUP, HEAD_DIM).swapaxes(0, 1)
    kf = k.astype(jnp.float32)

    def one(qblk):
        s = jnp.einsum('bqkgd,bskd->bkgqs', qblk, kf) * scale
        return _sink_softmax_av(s, v, sink)

    o = lax.map(one, qb)
    return o.swapaxes(0, 1).reshape(B, T, ATT_Q).astype(q.dtype)


def _latent_attention(q, k, v, kc, vc, sink):
    B, T = q.shape[0], q.shape[1]
    nb = T // ATT_BLOCK
    scale = HEAD_DIM ** -0.5
    pad = ((0, 0), (ATT_BLOCK, ATT_BLOCK), (0, 0), (0, 0))
    kp = jnp.pad(k.astype(jnp.float32), pad)
    vp = jnp.pad(v.astype(jnp.float32), pad)
    qf = q.astype(jnp.float32).reshape(B, T, ATT_KV_HEADS, ATT_GROUP, HEAD_DIM)
    kcf = kc.astype(jnp.float32)
    vcf = vc.astype(jnp.float32)
    a = jnp.arange(ATT_BLOCK, dtype=jnp.int32)[:, None]
    j = jnp.arange(3 * ATT_BLOCK, dtype=jnp.int32)[None, :]
    band = jnp.abs(j - ATT_BLOCK - a) <= WINDOW

    def one(i):
        start = i * ATT_BLOCK
        qblk = lax.dynamic_slice_in_dim(qf, start, ATT_BLOCK, axis=1)
        kblk = lax.dynamic_slice_in_dim(kp, start, 3 * ATT_BLOCK, axis=1)
        vblk = lax.dynamic_slice_in_dim(vp, start, 3 * ATT_BLOCK, axis=1)
        kpos = start - ATT_BLOCK + j
        valid = band & (kpos >= 0) & (kpos < T)
        s_loc = jnp.einsum('bqkgd,bskd->bkgqs', qblk, kblk) * scale
        s_loc = jnp.where(valid, s_loc, NEG)
        s_ctx = jnp.einsum('bqkgd,bskd->bkgqs', qblk, kcf) * scale
        s = jnp.concatenate([s_loc, s_ctx], axis=-1)
        vall = jnp.concatenate([vblk, vcf], axis=1)
        return _sink_softmax_av(s, vall, sink)

    o = lax.map(one, jnp.arange(nb, dtype=jnp.int32))
    return o.swapaxes(0, 1).reshape(B, T, ATT_Q).astype(q.dtype)


def _gla_chunked(q, k, v, log_a, s0):
    B, T = q.shape[0], q.shape[1]
    n = T // GLA_CHUNK

    def chunks(t):
        return t.reshape(B, n, GLA_CHUNK, GLA_HEADS, t.shape[-1]).transpose(1, 0, 3, 2, 4)

    qc, kc, vc = chunks(q), chunks(k), chunks(v)
    bc = jnp.cumsum(chunks(log_a), axis=3)
    causal = jnp.tril(jnp.ones((GLA_CHUNK, GLA_CHUNK), dtype=bool))[:, :, None]

    def step(S, inp):
        q_, k_, v_, b_ = inp
        inter = jnp.einsum('bhtd,bhde->bhte', q_ * jnp.exp(b_), S)
        diff = b_[:, :, :, None, :] - b_[:, :, None, :, :]
        decay = jnp.exp(jnp.where(causal, diff, -jnp.inf))
        A = jnp.einsum('bhtd,bhsd,bhtsd->bhts', q_, k_, decay)
        intra = jnp.einsum('bhts,bhse->bhte', A, v_)
        b_last = b_[:, :, -1:, :]
        S_new = jnp.exp(b_[:, :, -1, :])[..., None] * S + jnp.einsum('bhsd,bhse->bhde', k_ * jnp.exp(b_last - b_), v_)
        return S_new, inter + intra

    S_fin, o = lax.scan(step, s0, (qc, kc, vc, bc))
    o = o.transpose(1, 0, 3, 2, 4).reshape(B, T, GLA_HEADS, GLA_DV)
    return o, S_fin


def _gla_bidir(q, k, v, la_f, la_b, s0_f, s0_b):
    o_f, S_f = _gla_chunked(q, k, v, la_f, s0_f)
    fl = lambda t: jnp.flip(t, axis=1)
    o_b, S_b = _gla_chunked(fl(q), fl(k), fl(v), fl(la_b), s0_b)
    return o_f + fl(o_b), S_f, S_b


def _pool_mix(u, w_pool, scale):
    B, T = u.shape[0], u.shape[1]
    uf = u.astype(jnp.float32).reshape(B, T, POOL_GROUPS, POOL_GROUP_DIM)
    cs = jnp.concatenate([jnp.zeros_like(uf[:, :1]), jnp.cumsum(uf, axis=1)], axis=1)
    w = jnp.array(POOL_WINDOWS, dtype=jnp.int32)[None, :]
    t = jnp.arange(T, dtype=jnp.int32)[:, None]
    lo = jnp.clip(t - w // 2, 0, T)
    hi = jnp.clip(t - w // 2 + w, 0, T)
    gi = jnp.arange(POOL_GROUPS, dtype=jnp.int32)[None, :]
    win_sum = cs[:, hi, gi] - cs[:, lo, gi]
    cnt = (hi - lo).astype(jnp.float32)[None, :, :, None]
    pooled = win_sum / cnt - uf
    y = jnp.einsum('btgi,gio->btgo', pooled, w_pool.astype(jnp.float32)).reshape(B, T, POOL_W)
    return (y * scale.astype(jnp.float32)).astype(u.dtype)


def _mixer(hn, p, ctx):
    B, T = hn.shape[0], hn.shape[1]
    z = hn @ p['w_in']
    qa, ka, va, qb, kb, vb, rb, glf, glb, uc, ga, gb, gc = jnp.split(z, [int(s) for s in np.cumsum(IN_SPLITS)[:-1]], axis=-1)
    qa = _rmsnorm(qa.reshape(B, T, ATT_HEADS, HEAD_DIM), p['g_qn'])
    ka = _rmsnorm(ka.reshape(B, T, ATT_KV_HEADS, HEAD_DIM), p['g_kn'])
    va = va.reshape(B, T, ATT_KV_HEADS, HEAD_DIM)
    sink = p['att_sink'].reshape(ATT_KV_HEADS, ATT_GROUP)
    qb = qb.astype(jnp.float32).reshape(B, T, GLA_HEADS, GLA_DK) * (GLA_DK ** -0.5)
    kb = kb.astype(jnp.float32).reshape(B, T, GLA_HEADS, GLA_DK)
    vb = vb.astype(jnp.float32).reshape(B, T, GLA_HEADS, GLA_DV)
    la_f = (jax.nn.log_sigmoid((glf @ p['w_gate2'][0] + p['b_gate2'][0]).astype(jnp.float32)) / GLA_TAU).reshape(B, T, GLA_HEADS, GLA_DK)
    la_b = (jax.nn.log_sigmoid((glb @ p['w_gate2'][1] + p['b_gate2'][1]).astype(jnp.float32)) / GLA_TAU).reshape(B, T, GLA_HEADS, GLA_DK)
    if ctx is None:
        o_a = _ctx_attention(qa, ka, va, sink)
        s0 = jnp.zeros((B, GLA_HEADS, GLA_DK, GLA_DV), jnp.float32)
        o_b, S_f, S_b = _gla_bidir(qb, kb, vb, la_f, la_b, s0, s0)
        new = (ka, va, jnp.stack([S_f, S_b], axis=1))
    else:
        kc, vc, st = ctx
        o_a = _latent_attention(_rope_2d(qa), _rope_2d(ka), va, kc, vc, sink)
        o_b, _, _ = _gla_bidir(qb, kb, vb, la_f, la_b, st[:, 0].astype(jnp.float32), st[:, 1].astype(jnp.float32))
        new = None
    o_b = _rmsnorm(o_b, p['g_gla_out']).reshape(B, T, GLA_VW).astype(hn.dtype) * jax.nn.silu(rb)
    o_c = _pool_mix(uc, p['w_pool'], p['pool_scale'])
    mixed = (jax.nn.sigmoid(ga) * (o_a @ p['w_br_a'])
             + jax.nn.sigmoid(gb) * (o_b @ p['w_br_b'])
             + jax.nn.sigmoid(gc) * (o_c @ p['w_br_c']))
    return mixed @ p['w_out'], new


def _layer(x, cvec, p, ctx):
    mod = (jax.nn.silu(cvec) @ p['w_mod'] + p['b_mod']).reshape(-1, 1, 6 * D_MODEL)
    sh1, sc1, g1, sh2, sc2, g2 = jnp.split(mod, 6, axis=-1)
    hn = _rmsnorm(x, p['g_norm1']) * (1 + sc1) + sh1
    mix, new = _mixer(hn, p, ctx)
    x = x + g1 * mix
    hn = _rmsnorm(x, p['g_norm2']) * (1 + sc2) + sh2
    x = x + g2 * ((jax.nn.silu(hn @ p['w_ff_gate']) * (hn @ p['w_ff_up'])) @ p['w_ff_down'])
    return x, new


def setup_inputs(seed: int = 0) -> dict:
    key = jax.random.key(seed)
    ks = jax.random.split(key, 32)
    f32 = jnp.float32
    nrm = lambda k, shape, s: jax.random.normal(k, shape, f32) * s
    gain = lambda k, shape: 1.0 + 0.02 * jax.random.normal(k, shape, f32)
    D = D_MODEL
    return {
        'x_prompt': nrm(ks[0], (BATCH, SEQ, D), 1.0),
        'x_sample': nrm(ks[1], (DEC_BATCH, DEC_SEQ, D), 1.0),
        'c': nrm(ks[2], (DEC_BATCH, D), 1.0),
        'cache_k': nrm(ks[3], (DEC_BATCH, DEPTH, PAST_LEN, ATT_KV_HEADS, HEAD_DIM), 1.0),
        'cache_v': nrm(ks[4], (DEC_BATCH, DEPTH, PAST_LEN, ATT_KV_HEADS, HEAD_DIM), 1.0),
        'state_gla': nrm(ks[5], (DEC_BATCH, DEPTH, 2, GLA_HEADS, GLA_DK, GLA_DV), 1.0),
        'c_ctx': nrm(ks[6], (D,), 1.0),
        'w_in': nrm(ks[7], (DEPTH, D, IN_WIDTH), D ** -0.5),
        'g_qn': gain(ks[8], (DEPTH, HEAD_DIM)),
        'g_kn': gain(ks[9], (DEPTH, HEAD_DIM)),
        'att_sink': nrm(ks[10], (DEPTH, ATT_HEADS), 0.5),
        'w_gate2': nrm(ks[11], (DEPTH, 2, GLA_RANK, GLA_QK), GLA_RANK ** -0.5),
        'b_gate2': nrm(ks[12], (DEPTH, 2, GLA_QK), 0.1),
        'g_gla_out': gain(ks[13], (DEPTH, GLA_DV)),
        'w_pool': nrm(ks[14], (DEPTH, POOL_GROUPS, POOL_GROUP_DIM, POOL_GROUP_DIM), POOL_GROUP_DIM ** -0.5),
        'pool_scale': 1.0 + 0.1 * jax.random.normal(ks[15], (DEPTH, POOL_W), f32),
        'w_br_a': nrm(ks[16], (DEPTH, ATT_Q, D), ATT_Q ** -0.5),
        'w_br_b': nrm(ks[17], (DEPTH, GLA_VW, D), GLA_VW ** -0.5),
        'w_br_c': nrm(ks[18], (DEPTH, POOL_W, D), POOL_W ** -0.5),
        'w_out': nrm(ks[19], (DEPTH, D, D), D ** -0.5),
        'g_norm1': gain(ks[20], (DEPTH, D)),
        'g_norm2': gain(ks[21], (DEPTH, D)),
        'w_mod': nrm(ks[22], (DEPTH, D, 6 * D), D ** -0.5),
        'b_mod': nrm(ks[23], (DEPTH, 6 * D), 0.02),
        'w_ff_gate': nrm(ks[24], (DEPTH, D, D_FF), D ** -0.5),
        'w_ff_up': nrm(ks[25], (DEPTH, D, D_FF), D ** -0.5),
        'w_ff_down': nrm(ks[26], (DEPTH, D_FF, D), D_FF ** -0.5),
    }


def reference(x_prompt, x_sample, c, cache_k, cache_v, state_gla, c_ctx, w_in, g_qn, g_kn, att_sink,
              w_gate2, b_gate2, g_gla_out, w_pool, pool_scale, w_br_a, w_br_b, w_br_c, w_out,
              g_norm1, g_norm2, w_mod, b_mod, w_ff_gate, w_ff_up, w_ff_down):
    yp = x_prompt
    ys = x_sample
    new_ks, new_vs, new_sts = [], [], []
    for l in range(DEPTH):
        p = {
            'w_in': w_in[l], 'g_qn': g_qn[l], 'g_kn': g_kn[l], 'att_sink': att_sink[l],
            'w_gate2': w_gate2[l], 'b_gate2': b_gate2[l], 'g_gla_out': g_gla_out[l],
            'w_pool': w_pool[l], 'pool_scale': pool_scale[l],
            'w_br_a': w_br_a[l], 'w_br_b': w_br_b[l], 'w_br_c': w_br_c[l], 'w_out': w_out[l],
            'g_norm1': g_norm1[l], 'g_norm2': g_norm2[l], 'w_mod': w_mod[l], 'b_mod': b_mod[l],
            'w_ff_gate': w_ff_gate[l], 'w_ff_up': w_ff_up[l], 'w_ff_down': w_ff_down[l],
        }
        yp, (k_l, v_l, s_l) = _layer(yp, c_ctx, p, None)
        new_ks.append(k_l)
        new_vs.append(v_l)
        new_sts.append(s_l)
        ys, _ = _layer(ys, c, p, (cache_k[:, l], cache_v[:, l], state_gla[:, l]))
    new_k = jnp.stack(new_ks, axis=1)
    new_v = jnp.stack(new_vs, axis=1)
    new_state_gla = jnp.stack(new_sts, axis=1)
    return (yp, ys, new_k, new_v, new_state_gla)
```

```python
import functools

import jax
import jax.numpy as jnp
import numpy as np
from jax import lax
from jax.experimental import pallas as pl
from jax.experimental.pallas import tpu as pltpu

D_MODEL = 1024
DEPTH = 4
GRID_W = 64
ATT_HEADS = 8
ATT_KV_HEADS = 2
ATT_GROUP = ATT_HEADS // ATT_KV_HEADS
HEAD_DIM = 64
WINDOW = 128
ATT_BLOCK = 128
ROPE_BASE = 10000.0
GLA_HEADS = 4
GLA_DK = 64
GLA_DV = 128
GLA_RANK = 16
GLA_TAU = 16.0
GLA_CHUNK = 64
POOL_GROUPS = 4
POOL_GROUP_DIM = 128
POOL_WINDOWS = (2, 4, 8, 16)
D_FF = 2816
ATT_Q = ATT_HEADS * HEAD_DIM
ATT_KV = ATT_KV_HEADS * HEAD_DIM
GLA_QK = GLA_HEADS * GLA_DK
GLA_VW = GLA_HEADS * GLA_DV
POOL_W = POOL_GROUPS * POOL_GROUP_DIM
EPS = 1e-6
NEG = -1e30

C_QA = 0
C_KA = C_QA + ATT_Q
C_VA = C_KA + ATT_KV
C_QB = C_VA + ATT_KV
C_KB = C_QB + GLA_QK
C_VB = C_KB + GLA_QK
C_RB = C_VB + GLA_VW
C_UC = C_RB + GLA_VW
C_GL = C_UC + POOL_W
GL_PAD = 128
MIX_W = C_GL + GL_PAD
GATE_W = 3 * D_MODEL

POST_TILE = 256
PROJ_TILE = 256
POOL_TILE = 256
POOL_HALO = 128
MOD_ROWS = 8
MOD_TILE = 1024
GLA_FAST_MIN_LOG_DECAY = -2.0
VMEM_LIMIT = 56 * 1024 * 1024

_F32 = jnp.float32
_BF16 = jnp.bfloat16


def _dot(a, b):
    return jnp.dot(a, b, preferred_element_type=_F32)


def _dot_nt(a, b):
    return lax.dot_general(a, b, (((1,), (1,)), ((), ())), preferred_element_type=_F32)


def _dot_tn(a, b):
    return lax.dot_general(a, b, (((0,), (0,)), ((), ())), preferred_element_type=_F32)


def _rms_scale(x):
    return lax.rsqrt(jnp.mean(x * x, axis=-1, keepdims=True) + EPS)


def _head_norm(x, n_heads, width):
    parts = []
    for h in range(n_heads):
        xh = x[:, h * width:(h + 1) * width]
        parts.append(xh * _rms_scale(xh))
    return jnp.concatenate(parts, axis=-1)


def _log_sigmoid(x):
    return jnp.minimum(x, 0.0) - jnp.log1p(jnp.exp(-jnp.abs(x)))


def _silu(x):
    return x * jax.nn.sigmoid(x)


def _rope(x, cos, sin_signed):
    n = x.shape[-1]
    lane = lax.broadcasted_iota(jnp.int32, x.shape, 1)
    up = pltpu.roll(x, n - HEAD_DIM // 4, axis=1)
    down = pltpu.roll(x, HEAD_DIM // 4, axis=1)
    partner = jnp.where((lane & (HEAD_DIM // 2 - 1)) < HEAD_DIM // 4, up, down)
    return x * cos + partner * sin_signed


def _sink_softmax_av(scores, values, sink_col):
    m = sink_col
    for s in scores:
        m = jnp.maximum(m, jnp.max(s, axis=-1, keepdims=True))
    den = jnp.exp(sink_col - m)
    acc = None
    for s, v in zip(scores, values):
        p = jnp.exp(s - m)
        den = den + jnp.sum(p, axis=-1, keepdims=True)
        pv = _dot(p.astype(_BF16), v)
        acc = pv if acc is None else acc + pv
    return acc / den


def _mod_kernel(cv_ref, w_ref, b_ref, out_ref):
    s = _silu(cv_ref[...]).astype(_BF16)
    out_ref[0] = _dot(s, w_ref[0].astype(_BF16)) + b_ref[0]


def _modulation(cv, w_mod, b_mod):
    n_col = (6 * D_MODEL) // MOD_TILE
    return pl.pallas_call(
        _mod_kernel,
        grid=(DEPTH, n_col),
        in_specs=[
            pl.BlockSpec((MOD_ROWS, D_MODEL), lambda l, j: (0, 0)),
            pl.BlockSpec((1, D_MODEL, MOD_TILE), lambda l, j: (l, 0, j)),
            pl.BlockSpec((1, 1, MOD_TILE), lambda l, j: (l, 0, j)),
        ],
        out_specs=pl.BlockSpec((1, MOD_ROWS, MOD_TILE), lambda l, j: (l, 0, j)),
        out_shape=jax.ShapeDtypeStruct((DEPTH, MOD_ROWS, 6 * D_MODEL), _F32),
        name="modulation",
    )(cv, w_mod, b_mod.reshape(DEPTH, 1, 6 * D_MODEL))


def _stacked_queries(qr_ref, rows, kv):
    return jnp.concatenate(
        [qr_ref[rows, (kv * ATT_GROUP + g) * HEAD_DIM:(kv * ATT_GROUP + g + 1) * HEAD_DIM]
         for g in range(ATT_GROUP)], axis=0)


def _stacked_sink(sink_ref, layer, kv, n):
    return jnp.concatenate(
        [jnp.full((n, 1), sink_ref[layer, kv * ATT_GROUP + g], _F32) for g in range(ATT_GROUP)], axis=0)


def _attention_ctx(T, layer, qr_ref, kr_ref, vr_ref, sink_ref, oa_ref):
    heads = [None] * ATT_HEADS
    for kv in range(ATT_KV_HEADS):
        cols = slice(kv * HEAD_DIM, (kv + 1) * HEAD_DIM)
        qs = _stacked_queries(qr_ref, slice(0, T), kv)
        o = _sink_softmax_av([_dot_nt(qs, kr_ref[0:T, cols])], [vr_ref[0:T, cols]],
                             _stacked_sink(sink_ref, layer, kv, T))
        for g in range(ATT_GROUP):
            heads[kv * ATT_GROUP + g] = o[g * T:(g + 1) * T]
    oa_ref[0] = jnp.concatenate(heads, axis=-1).astype(_BF16)


def _attention_latent(T, layer, qr_ref, kr_ref, vr_ref, kc_ref, vc_ref, sink_ref, oa_ref):
    rows = ATT_GROUP * ATT_BLOCK
    span = 3 * ATT_BLOCK
    kc = kc_ref[0, 0].astype(_BF16)
    vc = vc_ref[0, 0].astype(_BF16)

    def block(i, carry):
        q_rows = pl.ds(pl.multiple_of(i * ATT_BLOCK, ATT_BLOCK), ATT_BLOCK)
        k_rows = pl.ds(pl.multiple_of(i * ATT_BLOCK, ATT_BLOCK), span)
        q_pos = i * ATT_BLOCK + (lax.broadcasted_iota(jnp.int32, (rows, span), 0) & (ATT_BLOCK - 1))
        k_pos = (i - 1) * ATT_BLOCK + lax.broadcasted_iota(jnp.int32, (rows, span), 1)
        valid = (jnp.abs(k_pos - q_pos) <= WINDOW) & (k_pos >= 0) & (k_pos < T)
        heads = [None] * ATT_HEADS
        for kv in range(ATT_KV_HEADS):
            cols = slice(kv * HEAD_DIM, (kv + 1) * HEAD_DIM)
            qs = _stacked_queries(qr_ref, q_rows, kv)
            s_loc = jnp.where(valid, _dot_nt(qs, kr_ref[k_rows, cols]), NEG)
            s_ctx = _dot_nt(qs, kc[:, cols])
            o = _sink_softmax_av([s_loc, s_ctx], [vr_ref[k_rows, cols], vc[:, cols]],
                                 _stacked_sink(sink_ref, layer, kv, ATT_BLOCK))
            for g in range(ATT_GROUP):
                heads[kv * ATT_GROUP + g] = o[g * ATT_BLOCK:(g + 1) * ATT_BLOCK]
        oa_ref[0, q_rows, :] = jnp.concatenate(heads, axis=-1).astype(_BF16)
        return carry

    lax.fori_loop(0, T // ATT_BLOCK, block, 0)


def _gla_chunk(z_ref, la_ref, o_ref, st_ref, a_ref, direction, start, fast):
    C = GLA_CHUNK
    rows = pl.ds(pl.multiple_of(start, C), C)
    q = z_ref[rows, C_QB:C_QB + GLA_QK] * (GLA_DK ** -0.5)
    k = z_ref[rows, C_KB:C_KB + GLA_QK]
    v = z_ref[rows, C_VB:C_VB + GLA_VW].astype(_BF16)
    la = la_ref[rows, direction * GLA_QK:(direction + 1) * GLA_QK]
    la_hi = la.astype(_BF16)
    la_lo = (la - la_hi.astype(_F32)).astype(_BF16)
    ti = lax.broadcasted_iota(jnp.int32, (C, C), 0)
    si = lax.broadcasted_iota(jnp.int32, (C, C), 1)
    causal = (si <= ti) if direction == 0 else (si >= ti)
    tri = jnp.where(causal, 1.0, 0.0).astype(_BF16)
    b = _dot(tri, la_hi) + _dot(tri, la_lo)
    end, mid = (C - 1, C // 2 - 1) if direction == 0 else (0, C // 2)
    b_end = b[end:end + 1]
    q_in = (q * jnp.exp(b)).astype(_BF16)
    k_out = (k * jnp.exp(b_end - b)).astype(_BF16)
    e_end = jnp.exp(b_end)
    if fast:
        c = b - b[mid:mid + 1]
        q_c = (q * jnp.exp(c)).astype(_BF16)
        k_c = (k * jnp.exp(-c)).astype(_BF16)
    else:
        ones = jnp.ones((8, GLA_DK), _BF16)

        def row_group(g, carry):
            base = pl.multiple_of(g * 8, 8)
            b8 = a_ref[GLA_HEADS, pl.ds(base, 8), :]
            q8 = a_ref[GLA_HEADS + 1, pl.ds(base, 8), :]
            s_idx = lax.broadcasted_iota(jnp.int32, (C, 1), 0)
            rows_h = [[] for _ in range(GLA_HEADS)]
            for j in range(8):
                ok = (s_idx <= base + j) if direction == 0 else (s_idx >= base + j)
                decay = jnp.exp(jnp.where(ok, b8[j:j + 1] - b, NEG))
                p = (q8[j:j + 1] * k * decay).astype(_BF16)
                for h in range(GLA_HEADS):
                    rows_h[h].append(_dot_nt(ones, p[:, h * GLA_DK:(h + 1) * GLA_DK])[0:1])
            for h in range(GLA_HEADS):
                a_ref[h, pl.ds(base, 8), 0:C] = jnp.concatenate(rows_h[h], axis=0)
            return carry

        a_ref[GLA_HEADS] = b
        a_ref[GLA_HEADS + 1] = q
        lax.fori_loop(0, C // 8, row_group, 0)
    outs = []
    for h in range(GLA_HEADS):
        kc = slice(h * GLA_DK, (h + 1) * GLA_DK)
        vc = slice(h * GLA_DV, (h + 1) * GLA_DV)
        if fast:
            a_h = jnp.where(causal, _dot_nt(q_c[:, kc], k_c[:, kc]), 0.0)
        else:
            a_h = a_ref[h, :, 0:C]
        s_t = st_ref[direction, h]
        o_h = _dot_nt(q_in[:, kc], s_t.astype(_BF16)) + _dot(a_h.astype(_BF16), v[:, vc])
        st_ref[direction, h] = s_t * e_end[:, kc] + _dot_tn(v[:, vc], k_out[:, kc])
        outs.append(o_h)
    o_ref[rows, :] += jnp.concatenate(outs, axis=-1)


def _gla(T, z_ref, la_ref, o_ref, st_ref, a_ref, wg2_ref, bg2_ref):
    n_chunks = T // GLA_CHUNK
    la_min = None
    for r0 in range(0, T, PROJ_TILE):
        rows = slice(r0, r0 + PROJ_TILE)
        pre = lax.dot_general(z_ref[rows, C_GL:C_GL + GL_PAD], wg2_ref[...], (((1,), (0,)), ((), ())),
                              precision=lax.Precision.HIGHEST, preferred_element_type=_F32) + bg2_ref[...]
        la = _log_sigmoid(pre) * (1.0 / GLA_TAU)
        la_ref[rows, :] = la
        la_min = jnp.min(la) if la_min is None else jnp.minimum(la_min, jnp.min(la))
    fast_ok = la_min >= GLA_FAST_MIN_LOG_DECAY
    o_ref[...] = jnp.zeros(o_ref.shape, _F32)

    def run(fast):
        def body(i, carry):
            _gla_chunk(z_ref, la_ref, o_ref, st_ref, a_ref, 0, i * GLA_CHUNK, fast)
            _gla_chunk(z_ref, la_ref, o_ref, st_ref, a_ref, 1, (n_chunks - 1 - i) * GLA_CHUNK, fast)
            return carry
        lax.fori_loop(0, n_chunks, body, 0)

    @pl.when(fast_ok)
    def _():
        run(True)

    @pl.when(jnp.logical_not(fast_ok))
    def _():
        run(False)


def _gla_finish(T, z_ref, o_ref, ggla_ref, ob_ref):
    for r0 in range(0, T, PROJ_TILE):
        rows = slice(r0, r0 + PROJ_TILE)
        parts = []
        for h in range(GLA_HEADS):
            oh = o_ref[rows, h * GLA_DV:(h + 1) * GLA_DV]
            parts.append(oh * _rms_scale(oh) * ggla_ref[...])
        y = jnp.concatenate(parts, axis=-1) * _silu(z_ref[rows, C_RB:C_RB + GLA_VW])
        ob_ref[0, rows, :] = y.astype(_BF16)


def _pool(T, z_ref, upad_ref, wpool_ref, pscale_ref, oc_ref):
    u = z_ref[:, C_UC:C_UC + POOL_W]
    u_hi = u.astype(_BF16)
    u_lo = (u - u_hi.astype(_F32)).astype(_BF16)
    zeros = jnp.zeros((POOL_HALO, POOL_W), _BF16)
    for part, val in ((0, u_hi), (1, u_lo)):
        upad_ref[part, 0:POOL_HALO, :] = zeros
        upad_ref[part, POOL_HALO + T:POOL_HALO + T + POOL_HALO, :] = zeros
        upad_ref[part, POOL_HALO:POOL_HALO + T, :] = val
    span = POOL_TILE + 2 * POOL_HALO
    r = lax.broadcasted_iota(jnp.int32, (POOL_TILE, span), 0)
    c = lax.broadcasted_iota(jnp.int32, (POOL_TILE, span), 1)
    off = c - POOL_HALO - r
    for jb in range(T // POOL_TILE):
        t = jb * POOL_TILE + lax.broadcasted_iota(jnp.int32, (POOL_TILE, 1), 0)
        parts = []
        for g, w in enumerate(POOL_WINDOWS):
            cols = slice(g * POOL_GROUP_DIM, (g + 1) * POOL_GROUP_DIM)
            band = jnp.where((off >= -(w // 2)) & (off < w - w // 2), 1.0, 0.0).astype(_BF16)
            win = slice(jb * POOL_TILE, jb * POOL_TILE + span)
            total = _dot(band, upad_ref[0, win, cols]) + _dot(band, upad_ref[1, win, cols])
            cnt = (jnp.minimum(t - w // 2 + w, T) - jnp.maximum(t - w // 2, 0)).astype(_F32)
            pooled = total / cnt - z_ref[jb * POOL_TILE:(jb + 1) * POOL_TILE, C_UC + g * POOL_GROUP_DIM:
                                         C_UC + (g + 1) * POOL_GROUP_DIM]
            parts.append(_dot(pooled.astype(_BF16), wpool_ref[g]))
        y = jnp.concatenate(parts, axis=-1) * pscale_ref[...]
        oc_ref[0, jb * POOL_TILE:(jb + 1) * POOL_TILE, :] = y.astype(_BF16)


def _mix_kernel(latent, T, layer_ref, *refs):
    (x_ref, mod_ref, gn1_ref, wmix_ref, gqn_ref, gkn_ref, sink_ref, wg2_ref, bg2_ref, ggla_ref,
     wpool_ref, pscale_ref) = refs[:12]
    refs = refs[12:]
    if latent:
        kc_ref, vc_ref, st0_ref, cos_ref, sin_ref, oa_ref, ob_ref, oc_ref = refs[:8]
        refs = refs[8:]
    else:
        oa_ref, ob_ref, oc_ref, kout_ref, vout_ref, stout_ref = refs[:6]
        refs = refs[6:]
    z_ref, qr_ref, kr_ref, vr_ref, la_ref, o_ref, st_ref, a_ref, upad_ref = refs
    layer = layer_ref[0]
    pad = ATT_BLOCK if latent else 0

    if latent:
        zeros = jnp.zeros((pad, ATT_KV), _BF16)
        for ref in (kr_ref, vr_ref):
            ref[0:pad, :] = zeros
            ref[pad + T:pad + T + pad, :] = zeros
    shift = mod_ref[0, :, 0:D_MODEL]
    scale = mod_ref[0, :, D_MODEL:2 * D_MODEL]
    for r0 in range(0, T, PROJ_TILE):
        rows = slice(r0, r0 + PROJ_TILE)
        x = x_ref[0, rows, :]
        hn = (x * _rms_scale(x) * gn1_ref[...]) * (1.0 + scale) + shift
        z_ref[rows, :] = _dot(hn.astype(_BF16), wmix_ref[...])
        q = _head_norm(z_ref[rows, C_QA:C_QA + ATT_Q], ATT_HEADS, HEAD_DIM) * gqn_ref[...]
        k = _head_norm(z_ref[rows, C_KA:C_KA + ATT_KV], ATT_KV_HEADS, HEAD_DIM) * gkn_ref[...]
        v = z_ref[rows, C_VA:C_VA + ATT_KV]
        if latent:
            cos = jnp.concatenate([cos_ref[rows, :]] * (ATT_Q // ATT_KV), axis=-1)
            sin = jnp.concatenate([sin_ref[rows, :]] * (ATT_Q // ATT_KV), axis=-1)
            q = _rope(q, cos, sin)
            k = _rope(k, cos_ref[rows, :], sin_ref[rows, :])
        else:
            kout_ref[0, rows, :] = k
            vout_ref[0, rows, :] = v
        qr_ref[rows, :] = (q * (HEAD_DIM ** -0.5)).astype(_BF16)
        kr_ref[pad + r0:pad + r0 + PROJ_TILE, :] = k.astype(_BF16)
        vr_ref[pad + r0:pad + r0 + PROJ_TILE, :] = v.astype(_BF16)
    if latent:
        _attention_latent(T, layer, qr_ref, kr_ref, vr_ref, kc_ref, vc_ref, sink_ref, oa_ref)
    else:
        _attention_ctx(T, layer, qr_ref, kr_ref, vr_ref, sink_ref, oa_ref)

    if latent:
        st_ref[...] = st0_ref[0]
    else:
        st_ref[...] = jnp.zeros(st_ref.shape, _F32)
    _gla(T, z_ref, la_ref, o_ref, st_ref, a_ref, wg2_ref, bg2_ref)
    _gla_finish(T, z_ref, o_ref, ggla_ref, ob_ref)
    if not latent:
        for d in range(2):
            for h in range(GLA_HEADS):
                stout_ref[0, d, h] = st_ref[d, h].T

    _pool(T, z_ref, upad_ref, wpool_ref, pscale_ref, oc_ref)


def _layer_spec(shape):
    zeros = (0,) * len(shape)
    return pl.BlockSpec((None,) + tuple(shape), lambda i, layer: (layer[0],) + zeros,
                        pipeline_mode=pl.Buffered(1))


def _mix_call(latent, layer, x, mod_all, pw, extra):
    B, T, _ = x.shape
    per_seq = lambda blk: pl.BlockSpec(blk, lambda b, layer: (b,) + (0,) * (len(blk) - 1))
    if latent:
        mod_spec = pl.BlockSpec((None, 1, 1, 6 * D_MODEL), lambda b, layer: (layer[0], b + 1, 0, 0))
    else:
        mod_spec = pl.BlockSpec((None, 1, 1, 6 * D_MODEL), lambda b, layer: (layer[0], 0, 0, 0))
    x_spec = pl.BlockSpec((1, T, D_MODEL), lambda b, layer: (b, 0, 0),
                          pipeline_mode=pl.Buffered(1) if latent else None)
    in_specs = [
        x_spec, mod_spec,
        _layer_spec((1, D_MODEL)), _layer_spec((D_MODEL, MIX_W)),
        _layer_spec((1, ATT_Q)), _layer_spec((1, ATT_KV)),
        pl.BlockSpec(memory_space=pltpu.SMEM),
        _layer_spec((GL_PAD, 2 * GLA_QK)), _layer_spec((1, 2 * GLA_QK)), _layer_spec((1, GLA_DV)),
        _layer_spec((POOL_GROUPS, POOL_GROUP_DIM, POOL_GROUP_DIM)), _layer_spec((1, POOL_W)),
    ]
    args = [x, mod_all, pw["g_norm1"], pw["w_mix"], pw["g_qn"], pw["g_kn"], pw["att_sink"], pw["w_gate2"],
            pw["b_gate2"], pw["g_gla_out"], pw["w_pool"], pw["pool_scale"]]
    act = lambda width: jax.ShapeDtypeStruct((B, T, width), _BF16)
    out_shape = [act(ATT_Q), act(GLA_VW), act(POOL_W)]
    out_specs = [per_seq((1, T, ATT_Q)), per_seq((1, T, GLA_VW)), per_seq((1, T, POOL_W))]
    if latent:
        cache_k, cache_v, st0, cos, sin = extra
        P = cache_k.shape[2]
        cache_spec = pl.BlockSpec((1, 1, P, ATT_KV), lambda b, layer: (b, layer[0], 0, 0))
        table_spec = pl.BlockSpec((T, ATT_KV), lambda b, layer: (0, 0), pipeline_mode=pl.Buffered(1))
        in_specs += [cache_spec, cache_spec,
                     pl.BlockSpec((1, None, 2, GLA_HEADS, GLA_DV, GLA_DK),
                                  lambda b, layer: (b, layer[0], 0, 0, 0, 0)),
                     table_spec, table_spec]
        args += [cache_k, cache_v, st0, cos, sin]
    else:
        out_shape += [jax.ShapeDtypeStruct((B, T, ATT_KV), _F32), jax.ShapeDtypeStruct((B, T, ATT_KV), _F32),
                      jax.ShapeDtypeStruct((B, 2, GLA_HEADS, GLA_DK, GLA_DV), _F32)]
        out_specs += [per_seq((1, T, ATT_KV)), per_seq((1, T, ATT_KV)),
                      per_seq((1, 2, GLA_HEADS, GLA_DK, GLA_DV))]
    kv_rows = T + 2 * ATT_BLOCK if latent else T
    scratch = [
        pltpu.VMEM((T, MIX_W), _F32),
        pltpu.VMEM((T, ATT_Q), _BF16), pltpu.VMEM((kv_rows, ATT_KV), _BF16), pltpu.VMEM((kv_rows, ATT_KV), _BF16),
        pltpu.VMEM((T, 2 * GLA_QK), _F32),
        pltpu.VMEM((T, GLA_VW), _F32),
        pltpu.VMEM((2, GLA_HEADS, GLA_DV, GLA_DK), _F32),
        pltpu.VMEM((GLA_HEADS + 2, GLA_CHUNK, GLA_QK), _F32),
        pltpu.VMEM((2, T + 2 * POOL_HALO, POOL_W), _BF16),
    ]
    return pl.pallas_call(
        functools.partial(_mix_kernel, latent, T),
        grid_spec=pltpu.PrefetchScalarGridSpec(
            num_scalar_prefetch=1, grid=(B,), in_specs=in_specs, out_specs=out_specs, scratch_shapes=scratch),
        out_shape=out_shape,
        compiler_params=pltpu.CompilerParams(dimension_semantics=("arbitrary",), vmem_limit_bytes=VMEM_LIMIT),
        name="mix_latent" if latent else "mix_ctx",
    )(layer, *args)


def _post_kernel(layer_ref, x_ref, mod_ref, oa_ref, ob_ref, oc_ref, gn1_ref, gn2_ref, wgate_ref, wa_ref, wb_ref,
                 wc_ref, wout_ref, wfg_ref, wfu_ref, wfd_ref, out_ref):
    x = x_ref[...]
    mod = lambda i: mod_ref[0, :, i * D_MODEL:(i + 1) * D_MODEL]
    hn = (x * _rms_scale(x) * gn1_ref[...]) * (1.0 + mod(1)) + mod(0)
    gates = jax.nn.sigmoid(_dot(hn.astype(_BF16), wgate_ref[...]))
    mixed = (gates[:, 0:D_MODEL] * _dot(oa_ref[...], wa_ref[...])
             + gates[:, D_MODEL:2 * D_MODEL] * _dot(ob_ref[...], wb_ref[...])
             + gates[:, 2 * D_MODEL:3 * D_MODEL] * _dot(oc_ref[...], wc_ref[...]))
    x = x + mod(2) * _dot(mixed.astype(_BF16), wout_ref[...])
    hn = ((x * _rms_scale(x) * gn2_ref[...]) * (1.0 + mod(4)) + mod(3)).astype(_BF16)
    h = _silu(_dot(hn, wfg_ref[...])) * _dot(hn, wfu_ref[...])
    out_ref[...] = x + mod(5) * _dot(h.astype(_BF16), wfd_ref[...])


def _post_call(layer, x2d, mod_all, oa, ob, oc, pw, tiles_per_seq, shared_mod):
    n = x2d.shape[0]
    row = lambda w: pl.BlockSpec((POST_TILE, w), lambda i, layer: (i, 0))
    if shared_mod:
        mod_spec = pl.BlockSpec((None, 1, 1, 6 * D_MODEL), lambda i, layer: (layer[0], 0, 0, 0))
    else:
        mod_spec = pl.BlockSpec((None, 1, 1, 6 * D_MODEL),
                                lambda i, layer: (layer[0], 1 + i // tiles_per_seq, 0, 0))
    in_specs = [row(D_MODEL), mod_spec, row(ATT_Q), row(GLA_VW), row(POOL_W),
                _layer_spec((1, D_MODEL)), _layer_spec((1, D_MODEL)), _layer_spec((D_MODEL, GATE_W)),
                _layer_spec((ATT_Q, D_MODEL)), _layer_spec((GLA_VW, D_MODEL)), _layer_spec((POOL_W, D_MODEL)),
                _layer_spec((D_MODEL, D_MODEL)), _layer_spec((D_MODEL, D_FF)), _layer_spec((D_MODEL, D_FF)),
                _layer_spec((D_FF, D_MODEL))]
    return pl.pallas_call(
        _post_kernel,
        grid_spec=pltpu.PrefetchScalarGridSpec(
            num_scalar_prefetch=1, grid=(n // POST_TILE,), in_specs=in_specs, out_specs=row(D_MODEL)),
        out_shape=jax.ShapeDtypeStruct(x2d.shape, _F32),
        compiler_params=pltpu.CompilerParams(dimension_semantics=("arbitrary",), vmem_limit_bytes=VMEM_LIMIT),
        name="post",
    )(layer, x2d, mod_all, oa, ob, oc, pw["g_norm1"], pw["g_norm2"], pw["w_gates"], pw["w_br_a"], pw["w_br_b"],
      pw["w_br_c"], pw["w_out"], pw["w_ff_gate"], pw["w_ff_up"], pw["w_ff_down"])


def _rope_tables(T):
    quarter = HEAD_DIM // 4
    inv_freq = ROPE_BASE ** (-np.arange(quarter, dtype=np.float32) / quarter)
    pos = np.arange(T)
    ang_row = (pos // GRID_W).astype(np.float32)[:, None] * inv_freq[None, :]
    ang_col = (pos % GRID_W).astype(np.float32)[:, None] * inv_freq[None, :]
    cos = np.concatenate([np.cos(ang_row)] * 2 + [np.cos(ang_col)] * 2, axis=-1)
    sin = np.concatenate([-np.sin(ang_row), np.sin(ang_row), -np.sin(ang_col), np.sin(ang_col)], axis=-1)
    return (jnp.asarray(np.tile(cos, (1, ATT_KV_HEADS)), _F32), jnp.asarray(np.tile(sin, (1, ATT_KV_HEADS)), _F32))


def _prepare_weights(w_in, g_qn, g_kn, att_sink, w_gate2, b_gate2, g_gla_out, w_pool, pool_scale, w_br_a,
                     w_br_b, w_br_c, w_out, g_norm1, g_norm2, w_ff_gate, w_ff_up, w_ff_down):
    o_gl = ATT_Q + 2 * ATT_KV + 2 * GLA_QK + 2 * GLA_VW
    o_uc = o_gl + 2 * GLA_RANK
    o_gate = o_uc + POOL_W
    w_mix = jnp.concatenate(
        [w_in[:, :, :o_gl], w_in[:, :, o_uc:o_gate], w_in[:, :, o_gl:o_uc],
         jnp.zeros((DEPTH, D_MODEL, GL_PAD - 2 * GLA_RANK), w_in.dtype)], axis=2).astype(_BF16)
    wg2 = jnp.zeros((DEPTH, GL_PAD, 2 * GLA_QK), _F32)
    wg2 = wg2.at[:, 0:GLA_RANK, 0:GLA_QK].set(w_gate2[:, 0])
    wg2 = wg2.at[:, GLA_RANK:2 * GLA_RANK, GLA_QK:].set(w_gate2[:, 1])
    vec = lambda a: a.reshape(DEPTH, 1, -1)
    return {
        "w_mix": w_mix,
        "w_gates": w_in[:, :, o_gate:].astype(_BF16),
        "g_qn": vec(jnp.tile(g_qn, (1, ATT_HEADS))),
        "g_kn": vec(jnp.tile(g_kn, (1, ATT_KV_HEADS))),
        "att_sink": att_sink,
        "w_gate2": wg2,
        "b_gate2": vec(b_gate2),
        "g_gla_out": vec(g_gla_out),
        "w_pool": w_pool.astype(_BF16),
        "pool_scale": vec(pool_scale),
        "w_br_a": w_br_a.astype(_BF16),
        "w_br_b": w_br_b.astype(_BF16),
        "w_br_c": w_br_c.astype(_BF16),
        "w_out": w_out.astype(_BF16),
        "g_norm1": vec(g_norm1),
        "g_norm2": vec(g_norm2),
        "w_ff_gate": w_ff_gate.astype(_BF16),
        "w_ff_up": w_ff_up.astype(_BF16),
        "w_ff_down": w_ff_down.astype(_BF16),
    }


def kernel(x_prompt, x_sample, c, cache_k, cache_v, state_gla, c_ctx, w_in, g_qn, g_kn, att_sink, w_gate2,
           b_gate2, g_gla_out, w_pool, pool_scale, w_br_a, w_br_b, w_br_c, w_out, g_norm1, g_norm2, w_mod,
           b_mod, w_ff_gate, w_ff_up, w_ff_down):
    B, T, _ = x_prompt.shape
    BL, TL, _ = x_sample.shape
    assert T % POST_TILE == 0 and TL % POST_TILE == 0 and BL + 1 <= MOD_ROWS
    cv = jnp.concatenate([c_ctx[None, :], c, jnp.zeros((MOD_ROWS - 1 - BL, D_MODEL), _F32)], axis=0)
    mod_all = _modulation(cv, w_mod, b_mod).reshape(DEPTH, MOD_ROWS, 1, 6 * D_MODEL)
    pw = _prepare_weights(w_in, g_qn, g_kn, att_sink, w_gate2, b_gate2, g_gla_out, w_pool, pool_scale, w_br_a,
                          w_br_b, w_br_c, w_out, g_norm1, g_norm2, w_ff_gate, w_ff_up, w_ff_down)
    cos, sin = _rope_tables(TL)
    P = cache_k.shape[2]
    latent_ctx = (cache_k.reshape(BL, DEPTH, P, ATT_KV), cache_v.reshape(BL, DEPTH, P, ATT_KV),
                  jnp.swapaxes(state_gla, -1, -2), cos, sin)

    def layer_step(l, carry):
        yp, ys, new_k, new_v, new_st = carry
        layer = jnp.full((1,), l, jnp.int32)
        oa, ob, oc, k_l, v_l, s_l = _mix_call(False, layer, yp, mod_all, pw, None)
        yp = _post_call(layer, yp.reshape(B * T, D_MODEL), mod_all, oa.reshape(B * T, -1), ob.reshape(B * T, -1),
                        oc.reshape(B * T, -1), pw, T // POST_TILE, True).reshape(B, T, D_MODEL)
        new_k = lax.dynamic_update_slice_in_dim(new_k, k_l[:, None], l, axis=1)
        new_v = lax.dynamic_update_slice_in_dim(new_v, v_l[:, None], l, axis=1)
        new_st = lax.dynamic_update_slice_in_dim(new_st, s_l[:, None], l, axis=1)
        oa, ob, oc = _mix_call(True, layer, ys, mod_all, pw, latent_ctx)
        ys = _post_call(layer, ys.reshape(BL * TL, D_MODEL), mod_all, oa.reshape(BL * TL, -1),
                        ob.reshape(BL * TL, -1), oc.reshape(BL * TL, -1), pw, TL // POST_TILE,
                        False).reshape(BL, TL, D_MODEL)
        return yp, ys, new_k, new_v, new_st

    init = (x_prompt, x_sample,
            jnp.zeros((B, DEPTH, T, ATT_KV), _F32), jnp.zeros((B, DEPTH, T, ATT_KV), _F32),
            jnp.zeros((B, DEPTH, 2, GLA_HEADS, GLA_DK, GLA_DV), _F32))
    yp, ys, new_k, new_v, new_st = lax.fori_loop(0, DEPTH, layer_step, init)
    return (yp, ys, new_k.reshape(B, DEPTH, T, ATT_KV_HEADS, HEAD_DIM),
            new_v.reshape(B, DEPTH, T, ATT_KV_HEADS, HEAD_DIM), new_st)
```

```python
import functools

import jax
import jax.numpy as jnp
import numpy as np
from jax import lax
from jax.experimental import pallas as pl
from jax.experimental.pallas import tpu as pltpu

D_MODEL = 1024
DEPTH = 4
GRID_W = 64
ATT_HEADS = 8
ATT_KV_HEADS = 2
ATT_GROUP = ATT_HEADS // ATT_KV_HEADS
HEAD_DIM = 64
WINDOW = 128
ATT_BLOCK = 128
ROPE_BASE = 10000.0
GLA_HEADS = 4
GLA_DK = 64
GLA_DV = 128
GLA_RANK = 16
GLA_TAU = 16.0
GLA_CHUNK = 64
POOL_GROUPS = 4
POOL_GROUP_DIM = 128
POOL_WINDOWS = (2, 4, 8, 16)
D_FF = 2816
ATT_Q = ATT_HEADS * HEAD_DIM
ATT_KV = ATT_KV_HEADS * HEAD_DIM
GLA_QK = GLA_HEADS * GLA_DK
GLA_VW = GLA_HEADS * GLA_DV
POOL_W = POOL_GROUPS * POOL_GROUP_DIM
EPS = 1e-6
NEG = -1e30

C_QA = 0
C_KA = C_QA + ATT_Q
C_VA = C_KA + ATT_KV
C_QB = C_VA + ATT_KV
C_KB = C_QB + GLA_QK
C_VB = C_KB + GLA_QK
C_RB = C_VB + GLA_VW
C_UC = C_RB + GLA_VW
C_GL = C_UC + POOL_W
GL_PAD = 128
MIX_W = C_GL + GL_PAD
GATE_W = 3 * D_MODEL

POST_TILE = 256
PROJ_TILE = 256
POOL_TILE = 256
POOL_HALO = 128
MOD_ROWS = 8
MOD_TILE = 1024
GLA_BLOCK = 256
GLA_MAX_EXPONENT = 80.0
GLA_FAST_MIN_LOG_DECAY = -GLA_MAX_EXPONENT / GLA_CHUNK
VMEM_LIMIT = 56 * 1024 * 1024

_F32 = jnp.float32
_BF16 = jnp.bfloat16


def _dot(a, b):
    return jnp.dot(a, b, preferred_element_type=_F32)


def _dot_nt(a, b):
    return lax.dot_general(a, b, (((1,), (1,)), ((), ())), preferred_element_type=_F32)


def _dot_tn(a, b):
    return lax.dot_general(a, b, (((0,), (0,)), ((), ())), preferred_element_type=_F32)


def _rms_scale(x):
    return lax.rsqrt(jnp.mean(x * x, axis=-1, keepdims=True) + EPS)


def _head_norm(x, n_heads, width):
    parts = []
    for h in range(n_heads):
        xh = x[:, h * width:(h + 1) * width]
        parts.append(xh * _rms_scale(xh))
    return jnp.concatenate(parts, axis=-1)


def _log_sigmoid(x):
    return jnp.minimum(x, 0.0) - jnp.log1p(jnp.exp(-jnp.abs(x)))


def _silu(x):
    return x * jax.nn.sigmoid(x)


def _rope(x, cos, sin_signed):
    n = x.shape[-1]
    lane = lax.broadcasted_iota(jnp.int32, x.shape, 1)
    up = pltpu.roll(x, n - HEAD_DIM // 4, axis=1)
    down = pltpu.roll(x, HEAD_DIM // 4, axis=1)
    partner = jnp.where((lane & (HEAD_DIM // 2 - 1)) < HEAD_DIM // 4, up, down)
    return x * cos + partner * sin_signed


def _sink_softmax_av(scores, values, sink_col):
    m = sink_col
    for s in scores:
        m = jnp.maximum(m, jnp.max(s, axis=-1, keepdims=True))
    den = jnp.exp(sink_col - m)
    acc = None
    for s, v in zip(scores, values):
        p = jnp.exp(s - m).astype(_BF16)
        den = den + _dot(p, jnp.ones_like(v))
        pv = _dot(p, v)
        acc = pv if acc is None else acc + pv
    return acc / den


def _mod_kernel(cv_ref, w_ref, b_ref, out_ref):
    s = _silu(cv_ref[...]).astype(_BF16)
    out_ref[0] = _dot(s, w_ref[0].astype(_BF16)) + b_ref[0]


def _modulation(cv, w_mod, b_mod):
    n_col = (6 * D_MODEL) // MOD_TILE
    return pl.pallas_call(
        _mod_kernel,
        grid=(DEPTH, n_col),
        in_specs=[
            pl.BlockSpec((MOD_ROWS, D_MODEL), lambda l, j: (0, 0)),
            pl.BlockSpec((1, D_MODEL, MOD_TILE), lambda l, j: (l, 0, j)),
            pl.BlockSpec((1, 1, MOD_TILE), lambda l, j: (l, 0, j)),
        ],
        out_specs=pl.BlockSpec((1, MOD_ROWS, MOD_TILE), lambda l, j: (l, 0, j)),
        out_shape=jax.ShapeDtypeStruct((DEPTH, MOD_ROWS, 6 * D_MODEL), _F32),
        name="modulation",
    )(cv, w_mod, b_mod.reshape(DEPTH, 1, 6 * D_MODEL))


def _stacked_queries(qr_ref, rows, kv):
    return jnp.concatenate(
        [qr_ref[rows, (kv * ATT_GROUP + g) * HEAD_DIM:(kv * ATT_GROUP + g + 1) * HEAD_DIM]
         for g in range(ATT_GROUP)], axis=0)


def _stacked_sink(sink_ref, layer, kv, n):
    return jnp.concatenate(
        [jnp.full((n, 1), sink_ref[layer, kv * ATT_GROUP + g], _F32) for g in range(ATT_GROUP)], axis=0)


def _attention_ctx(T, layer, qr_ref, kr_ref, vr_ref, sink_ref, oa_ref):
    heads = [None] * ATT_HEADS
    for kv in range(ATT_KV_HEADS):
        cols = slice(kv * HEAD_DIM, (kv + 1) * HEAD_DIM)
        qs = _stacked_queries(qr_ref, slice(0, T), kv)
        o = _sink_softmax_av([_dot_nt(qs, kr_ref[0:T, cols])], [vr_ref[0:T, cols]],
                             _stacked_sink(sink_ref, layer, kv, T))
        for g in range(ATT_GROUP):
            heads[kv * ATT_GROUP + g] = o[g * T:(g + 1) * T]
    oa_ref[0] = jnp.concatenate(heads, axis=-1).astype(_BF16)


def _attention_latent(T, layer, qr_ref, kr_ref, vr_ref, kc_ref, vc_ref, sink_ref, oa_ref):
    rows = ATT_GROUP * ATT_BLOCK
    span = 3 * ATT_BLOCK
    kc = kc_ref[0, 0].astype(_BF16)
    vc = vc_ref[0, 0].astype(_BF16)

    def block(i, carry):
        q_rows = pl.ds(pl.multiple_of(i * ATT_BLOCK, ATT_BLOCK), ATT_BLOCK)
        k_rows = pl.ds(pl.multiple_of(i * ATT_BLOCK, ATT_BLOCK), span)
        q_pos = i * ATT_BLOCK + (lax.broadcasted_iota(jnp.int32, (rows, span), 0) & (ATT_BLOCK - 1))
        k_pos = (i - 1) * ATT_BLOCK + lax.broadcasted_iota(jnp.int32, (rows, span), 1)
        valid = (jnp.abs(k_pos - q_pos) <= WINDOW) & (k_pos >= 0) & (k_pos < T)
        heads = [None] * ATT_HEADS
        for kv in range(ATT_KV_HEADS):
            cols = slice(kv * HEAD_DIM, (kv + 1) * HEAD_DIM)
            qs = _stacked_queries(qr_ref, q_rows, kv)
            s_loc = jnp.where(valid, _dot_nt(qs, kr_ref[k_rows, cols]), NEG)
            s_ctx = _dot_nt(qs, kc[:, cols])
            o = _sink_softmax_av([s_loc, s_ctx], [vr_ref[k_rows, cols], vc[:, cols]],
                                 _stacked_sink(sink_ref, layer, kv, ATT_BLOCK))
            for g in range(ATT_GROUP):
                heads[kv * ATT_GROUP + g] = o[g * ATT_BLOCK:(g + 1) * ATT_BLOCK]
        oa_ref[0, q_rows, :] = jnp.concatenate(heads, axis=-1).astype(_BF16)
        return carry

    lax.fori_loop(0, T // ATT_BLOCK, block, 0)


def _gla_chunk(z_ref, la_ref, o_ref, st_ref, a_ref, direction, start):
    C = GLA_CHUNK
    rows = pl.ds(pl.multiple_of(start, C), C)
    q = z_ref[rows, C_QB:C_QB + GLA_QK] * (GLA_DK ** -0.5)
    k = z_ref[rows, C_KB:C_KB + GLA_QK]
    v = z_ref[rows, C_VB:C_VB + GLA_VW].astype(_BF16)
    la = la_ref[rows, direction * GLA_QK:(direction + 1) * GLA_QK]
    la_hi = la.astype(_BF16)
    la_lo = (la - la_hi.astype(_F32)).astype(_BF16)
    ti = lax.broadcasted_iota(jnp.int32, (C, C), 0)
    si = lax.broadcasted_iota(jnp.int32, (C, C), 1)
    causal = (si <= ti) if direction == 0 else (si >= ti)
    tri = jnp.where(causal, 1.0, 0.0).astype(_BF16)
    b = _dot(tri, la_hi) + _dot(tri, la_lo)
    end = C - 1 if direction == 0 else 0
    b_end = b[end:end + 1]
    q_in = (q * jnp.exp(b)).astype(_BF16)
    k_out = (k * jnp.exp(b_end - b)).astype(_BF16)
    e_end = jnp.exp(b_end)
    ones = jnp.ones((8, GLA_DK), _BF16)

    def row_group(g, carry):
        base = pl.multiple_of(g * 8, 8)
        b8 = a_ref[GLA_HEADS, pl.ds(base, 8), :]
        q8 = a_ref[GLA_HEADS + 1, pl.ds(base, 8), :]
        s_idx = lax.broadcasted_iota(jnp.int32, (C, 1), 0)
        rows_h = [[] for _ in range(GLA_HEADS)]
        for j in range(8):
            ok = (s_idx <= base + j) if direction == 0 else (s_idx >= base + j)
            decay = jnp.exp(jnp.where(ok, b8[j:j + 1] - b, NEG))
            p = (q8[j:j + 1] * k * decay).astype(_BF16)
            for h in range(GLA_HEADS):
                rows_h[h].append(_dot_nt(ones, p[:, h * GLA_DK:(h + 1) * GLA_DK])[0:1])
        for h in range(GLA_HEADS):
            a_ref[h, pl.ds(base, 8), 0:C] = jnp.concatenate(rows_h[h], axis=0)
        return carry

    a_ref[GLA_HEADS] = b
    a_ref[GLA_HEADS + 1] = q
    lax.fori_loop(0, C // 8, row_group, 0)
    outs = []
    for h in range(GLA_HEADS):
        kc = slice(h * GLA_DK, (h + 1) * GLA_DK)
        vc = slice(h * GLA_DV, (h + 1) * GLA_DV)
        s_t = st_ref[direction, h]
        o_h = _dot_nt(q_in[:, kc], s_t.astype(_BF16)) + _dot(a_ref[h, :, 0:C].astype(_BF16), v[:, vc])
        st_ref[direction, h] = s_t * e_end[:, kc] + _dot_tn(v[:, vc], k_out[:, kc])
        outs.append(o_h)
    o_ref[rows, :] += jnp.concatenate(outs, axis=-1)


def _gla_block(z_ref, la_ref, o_ref, st_ref, direction, start, use_state):
    C, NB = GLA_CHUNK, GLA_BLOCK
    n = NB // C
    fwd = direction == 0
    rows = slice(start, start + NB) if isinstance(start, int) else pl.ds(pl.multiple_of(start, NB), NB)
    q = z_ref[rows, C_QB:C_QB + GLA_QK] * (GLA_DK ** -0.5)
    k = z_ref[rows, C_KB:C_KB + GLA_QK]
    v = z_ref[rows, C_VB:C_VB + GLA_VW].astype(_BF16)
    la = la_ref[rows, direction * GLA_QK:(direction + 1) * GLA_QK]
    la_hi = la.astype(_BF16)
    la_lo = (la - la_hi.astype(_F32)).astype(_BF16)
    ti = lax.broadcasted_iota(jnp.int32, (NB, NB), 0)
    si = lax.broadcasted_iota(jnp.int32, (NB, NB), 1)
    shift = C.bit_length() - 1
    lag = ((ti >> shift) - (si >> shift)) if fwd else ((si >> shift) - (ti >> shift))
    causal = (si <= ti) if fwd else (si >= ti)
    tri = jnp.where(lag == 0, jnp.where(causal, 1.0, 0.0), 0.0).astype(_BF16)
    b = _dot(tri, la_hi) + _dot(tri, la_lo)
    b3 = b.reshape(n, C, GLA_QK)
    end = C - 1 if fwd else 0
    b_end3 = b3[:, end:end + 1, :]
    r = (b_end3 - b3).reshape(NB, GLA_QK)
    e = [jnp.exp(b_end3[m]) for m in range(n)]
    ones = jnp.ones((1, GLA_QK), _F32)
    chunks = list(range(n)) if fwd else list(range(n - 1, -1, -1))
    before, after = [None] * n, [None] * n
    acc = ones
    for m in chunks:
        before[m] = acc
        acc = acc * e[m]
    e_all = acc
    acc = ones
    for m in reversed(chunks):
        after[m] = acc
        acc = acc * e[m]
    prev = lambda m, d: chunks[chunks.index(m) - d] if chunks.index(m) - d >= 0 else None
    rowwise = lambda per_chunk: jnp.concatenate(
        [jnp.broadcast_to(f, (C, GLA_QK)) for f in per_chunk], axis=0)
    f1 = rowwise([e[prev(m, 1)] if prev(m, 1) is not None else ones for m in range(n)])
    f2 = rowwise([e[prev(m, 2)] if prev(m, 2) is not None else ones for m in range(n)])
    k_out = k * jnp.exp(r)
    q_lag = [q * jnp.exp(-r)]
    q_lag.append(q * jnp.exp(b))
    q_lag.append(q_lag[1] * f1)
    q_lag.append(q_lag[2] * f2)
    assert n == len(q_lag)
    k_fin = (k_out * rowwise(after)).astype(_BF16)
    k_out = k_out.astype(_BF16)
    if use_state:
        q_state = (q_lag[1] * rowwise(before)).astype(_BF16)
    outs = []
    for h in range(GLA_HEADS):
        kc = slice(h * GLA_DK, (h + 1) * GLA_DK)
        vc = slice(h * GLA_DV, (h + 1) * GLA_DV)
        qs = jnp.concatenate([ql[:, kc] for ql in q_lag], axis=0).astype(_BF16)
        p = _dot_nt(qs, k_out[:, kc])
        a = jnp.where(lag == 0, jnp.where(causal, p[0:NB], 0.0), 0.0)
        for d in range(1, n):
            a = jnp.where(lag == d, p[d * NB:(d + 1) * NB], a)
        o_h = _dot(a.astype(_BF16), v[:, vc])
        s_new = _dot_tn(v[:, vc], k_fin[:, kc])
        if use_state:
            s_t = st_ref[direction, h]
            o_h = o_h + _dot_nt(q_state[:, kc], s_t.astype(_BF16))
            s_new = s_new + s_t * e_all[:, kc]
        st_ref[direction, h] = s_new
        outs.append(o_h)
    o_ref[rows, :] += jnp.concatenate(outs, axis=-1)


def _gla(T, z_ref, la_ref, o_ref, st_ref, a_ref, wg2_ref, bg2_ref, use_state):
    n_chunks = T // GLA_CHUNK
    n_blocks = T // GLA_BLOCK
    la_min = None
    for r0 in range(0, T, PROJ_TILE):
        rows = slice(r0, r0 + PROJ_TILE)
        pre = lax.dot_general(z_ref[rows, C_GL:C_GL + GL_PAD], wg2_ref[...], (((1,), (0,)), ((), ())),
                              precision=lax.Precision.HIGHEST, preferred_element_type=_F32) + bg2_ref[...]
        la = _log_sigmoid(pre) * (1.0 / GLA_TAU)
        la_ref[rows, :] = la
        la_min = jnp.min(la) if la_min is None else jnp.minimum(la_min, jnp.min(la))
    fast_ok = la_min >= GLA_FAST_MIN_LOG_DECAY
    o_ref[...] = jnp.zeros(o_ref.shape, _F32)

    @pl.when(fast_ok)
    def _():
        if n_blocks == 1:
            for direction in range(2):
                _gla_block(z_ref, la_ref, o_ref, st_ref, direction, 0, use_state)
        else:
            def body(i, carry):
                _gla_block(z_ref, la_ref, o_ref, st_ref, 0, i * GLA_BLOCK, True)
                _gla_block(z_ref, la_ref, o_ref, st_ref, 1, (n_blocks - 1 - i) * GLA_BLOCK, True)
                return carry
            lax.fori_loop(0, n_blocks, body, 0)

    @pl.when(jnp.logical_not(fast_ok))
    def _():
        def body(i, carry):
            _gla_chunk(z_ref, la_ref, o_ref, st_ref, a_ref, 0, i * GLA_CHUNK)
            _gla_chunk(z_ref, la_ref, o_ref, st_ref, a_ref, 1, (n_chunks - 1 - i) * GLA_CHUNK)
            return carry
        lax.fori_loop(0, n_chunks, body, 0)


def _gla_finish(T, z_ref, o_ref, ggla_ref, ob_ref):
    for r0 in range(0, T, PROJ_TILE):
        rows = slice(r0, r0 + PROJ_TILE)
        parts = []
        for h in range(GLA_HEADS):
            oh = o_ref[rows, h * GLA_DV:(h + 1) * GLA_DV]
            parts.append(oh * _rms_scale(oh) * ggla_ref[...])
        y = jnp.concatenate(parts, axis=-1) * _silu(z_ref[rows, C_RB:C_RB + GLA_VW])
        ob_ref[0, rows, :] = y.astype(_BF16)


def _pool(T, z_ref, upad_ref, wpool_ref, pscale_ref, oc_ref):
    u = z_ref[:, C_UC:C_UC + POOL_W]
    u_hi = u.astype(_BF16)
    u_lo = (u - u_hi.astype(_F32)).astype(_BF16)
    zeros = jnp.zeros((POOL_HALO, POOL_W), _BF16)
    for part, val in ((0, u_hi), (1, u_lo)):
        upad_ref[part, 0:POOL_HALO, :] = zeros
        upad_ref[part, POOL_HALO + T:POOL_HALO + T + POOL_HALO, :] = zeros
        upad_ref[part, POOL_HALO:POOL_HALO + T, :] = val
    span = POOL_TILE + 2 * POOL_HALO
    r = lax.broadcasted_iota(jnp.int32, (POOL_TILE, span), 0)
    c = lax.broadcasted_iota(jnp.int32, (POOL_TILE, span), 1)
    off = c - POOL_HALO - r
    for jb in range(T // POOL_TILE):
        t = jb * POOL_TILE + lax.broadcasted_iota(jnp.int32, (POOL_TILE, 1), 0)
        parts = []
        for g, w in enumerate(POOL_WINDOWS):
            cols = slice(g * POOL_GROUP_DIM, (g + 1) * POOL_GROUP_DIM)
            band = jnp.where((off >= -(w // 2)) & (off < w - w // 2), 1.0, 0.0).astype(_BF16)
            win = slice(jb * POOL_TILE, jb * POOL_TILE + span)
            total = _dot(band, upad_ref[0, win, cols]) + _dot(band, upad_ref[1, win, cols])
            cnt = (jnp.minimum(t - w // 2 + w, T) - jnp.maximum(t - w // 2, 0)).astype(_F32)
            pooled = total / cnt - z_ref[jb * POOL_TILE:(jb + 1) * POOL_TILE, C_UC + g * POOL_GROUP_DIM:
                                         C_UC + (g + 1) * POOL_GROUP_DIM]
            parts.append(_dot(pooled.astype(_BF16), wpool_ref[g]))
        y = jnp.concatenate(parts, axis=-1) * pscale_ref[...]
        oc_ref[0, jb * POOL_TILE:(jb + 1) * POOL_TILE, :] = y.astype(_BF16)


def _mix_kernel(latent, T, layer_ref, *refs):
    (x_ref, mod_ref, gn1_ref, wmix_ref, gqn_ref, gkn_ref, sink_ref, wg2_ref, bg2_ref, ggla_ref,
     wpool_ref, pscale_ref) = refs[:12]
    refs = refs[12:]
    if latent:
        kc_ref, vc_ref, st0_ref, cos_ref, sin_ref, oa_ref, ob_ref, oc_ref = refs[:8]
        refs = refs[8:]
    else:
        oa_ref, ob_ref, oc_ref, kout_ref, vout_ref, stout_ref = refs[:6]
        refs = refs[6:]
    z_ref, qr_ref, kr_ref, vr_ref, la_ref, o_ref, st_ref, a_ref, upad_ref = refs
    layer = layer_ref[0]
    pad = ATT_BLOCK if latent else 0

    if latent:
        zeros = jnp.zeros((pad, ATT_KV), _BF16)
        for ref in (kr_ref, vr_ref):
            ref[0:pad, :] = zeros
            ref[pad + T:pad + T + pad, :] = zeros
    shift = mod_ref[0, :, 0:D_MODEL]
    scale = mod_ref[0, :, D_MODEL:2 * D_MODEL]
    for r0 in range(0, T, PROJ_TILE):
        rows = slice(r0, r0 + PROJ_TILE)
        x = x_ref[0, rows, :]
        hn = (x * _rms_scale(x) * gn1_ref[...]) * (1.0 + scale) + shift
        z_ref[rows, :] = _dot(hn.astype(_BF16), wmix_ref[...])
        q = _head_norm(z_ref[rows, C_QA:C_QA + ATT_Q], ATT_HEADS, HEAD_DIM) * gqn_ref[...]
        k = _head_norm(z_ref[rows, C_KA:C_KA + ATT_KV], ATT_KV_HEADS, HEAD_DIM) * gkn_ref[...]
        v = z_ref[rows, C_VA:C_VA + ATT_KV]
        if latent:
            cos = jnp.concatenate([cos_ref[rows, :]] * (ATT_Q // ATT_KV), axis=-1)
            sin = jnp.concatenate([sin_ref[rows, :]] * (ATT_Q // ATT_KV), axis=-1)
            q = _rope(q, cos, sin)
            k = _rope(k, cos_ref[rows, :], sin_ref[rows, :])
        else:
            kout_ref[0, rows, :] = k
            vout_ref[0, rows, :] = v
        qr_ref[rows, :] = (q * (HEAD_DIM ** -0.5)).astype(_BF16)
        kr_ref[pad + r0:pad + r0 + PROJ_TILE, :] = k.astype(_BF16)
        vr_ref[pad + r0:pad + r0 + PROJ_TILE, :] = v.astype(_BF16)
    if latent:
        _attention_latent(T, layer, qr_ref, kr_ref, vr_ref, kc_ref, vc_ref, sink_ref, oa_ref)
    else:
        _attention_ctx(T, layer, qr_ref, kr_ref, vr_ref, sink_ref, oa_ref)

    if latent:
        st_ref[...] = st0_ref[0]
    else:
        st_ref[...] = jnp.zeros(st_ref.shape, _F32)
    _gla(T, z_ref, la_ref, o_ref, st_ref, a_ref, wg2_ref, bg2_ref, latent)
    _gla_finish(T, z_ref, o_ref, ggla_ref, ob_ref)
    if not latent:
        for d in range(2):
            for h in range(GLA_HEADS):
                stout_ref[0, d, h] = st_ref[d, h].T

    _pool(T, z_ref, upad_ref, wpool_ref, pscale_ref, oc_ref)


def _layer_spec(shape):
    zeros = (0,) * len(shape)
    return pl.BlockSpec((None,) + tuple(shape), lambda i, layer: (layer[0],) + zeros,
                        pipeline_mode=pl.Buffered(1))


def _mix_call(latent, layer, x, mod_all, pw, extra):
    B, T, _ = x.shape
    per_seq = lambda blk: pl.BlockSpec(blk, lambda b, layer: (b,) + (0,) * (len(blk) - 1))
    if latent:
        mod_spec = pl.BlockSpec((None, 1, 1, 6 * D_MODEL), lambda b, layer: (layer[0], b + 1, 0, 0))
    else:
        mod_spec = pl.BlockSpec((None, 1, 1, 6 * D_MODEL), lambda b, layer: (layer[0], 0, 0, 0))
    x_spec = pl.BlockSpec((1, T, D_MODEL), lambda b, layer: (b, 0, 0),
                          pipeline_mode=pl.Buffered(1) if latent else None)
    in_specs = [
        x_spec, mod_spec,
        _layer_spec((1, D_MODEL)), _layer_spec((D_MODEL, MIX_W)),
        _layer_spec((1, ATT_Q)), _layer_spec((1, ATT_KV)),
        pl.BlockSpec(memory_space=pltpu.SMEM),
        _layer_spec((GL_PAD, 2 * GLA_QK)), _layer_spec((1, 2 * GLA_QK)), _layer_spec((1, GLA_DV)),
        _layer_spec((POOL_GROUPS, POOL_GROUP_DIM, POOL_GROUP_DIM)), _layer_spec((1, POOL_W)),
    ]
    args = [x, mod_all, pw["g_norm1"], pw["w_mix"], pw["g_qn"], pw["g_kn"], pw["att_sink"], pw["w_gate2"],
            pw["b_gate2"], pw["g_gla_out"], pw["w_pool"], pw["pool_scale"]]
    act = lambda width: jax.ShapeDtypeStruct((B, T, width), _BF16)
    out_shape = [act(ATT_Q), act(GLA_VW), act(POOL_W)]
    out_specs = [per_seq((1, T, ATT_Q)), per_seq((1, T, GLA_VW)), per_seq((1, T, POOL_W))]
    if latent:
        cache_k, cache_v, st0, cos, sin = extra
        P = cache_k.shape[2]
        cache_spec = pl.BlockSpec((1, 1, P, ATT_KV), lambda b, layer: (b, layer[0], 0, 0))
        table_spec = pl.BlockSpec((T, ATT_KV), lambda b, layer: (0, 0), pipeline_mode=pl.Buffered(1))
        in_specs += [cache_spec, cache_spec,
                     pl.BlockSpec((1, None, 2, GLA_HEADS, GLA_DV, GLA_DK),
                                  lambda b, layer: (b, layer[0], 0, 0, 0, 0)),
                     table_spec, table_spec]
        args += [cache_k, cache_v, st0, cos, sin]
    else:
        out_shape += [jax.ShapeDtypeStruct((B, T, ATT_KV), _F32), jax.ShapeDtypeStruct((B, T, ATT_KV), _F32),
                      jax.ShapeDtypeStruct((B, 2, GLA_HEADS, GLA_DK, GLA_DV), _F32)]
        out_specs += [per_seq((1, T, ATT_KV)), per_seq((1, T, ATT_KV)),
                      per_seq((1, 2, GLA_HEADS, GLA_DK, GLA_DV))]
    kv_rows = T + 2 * ATT_BLOCK if latent else T
    scratch = [
        pltpu.VMEM((T, MIX_W), _F32),
        pltpu.VMEM((T, ATT_Q), _BF16), pltpu.VMEM((kv_rows, ATT_KV), _BF16), pltpu.VMEM((kv_rows, ATT_KV), _BF16),
        pltpu.VMEM((T, 2 * GLA_QK), _F32),
        pltpu.VMEM((T, GLA_VW), _F32),
        pltpu.VMEM((2, GLA_HEADS, GLA_DV, GLA_DK), _F32),
        pltpu.VMEM((GLA_HEADS + 2, GLA_CHUNK, GLA_QK), _F32),
        pltpu.VMEM((2, T + 2 * POOL_HALO, POOL_W), _BF16),
    ]
    return pl.pallas_call(
        functools.partial(_mix_kernel, latent, T),
        grid_spec=pltpu.PrefetchScalarGridSpec(
            num_scalar_prefetch=1, grid=(B,), in_specs=in_specs, out_specs=out_specs, scratch_shapes=scratch),
        out_shape=out_shape,
        compiler_params=pltpu.CompilerParams(dimension_semantics=("arbitrary",), vmem_limit_bytes=VMEM_LIMIT),
        name="mix_latent" if latent else "mix_ctx",
    )(layer, *args)


def _post_kernel(layer_ref, x_ref, mod_ref, oa_ref, ob_ref, oc_ref, gn1_ref, gn2_ref, wgate_ref, wa_ref, wb_ref,
                 wc_ref, wout_ref, wfg_ref, wfu_ref, wfd_ref, out_ref):
    x = x_ref[...]
    mod = lambda i: mod_ref[0, :, i * D_MODEL:(i + 1) * D_MODEL]
    hn = (x * _rms_scale(x) * gn1_ref[...]) * (1.0 + mod(1)) + mod(0)
    gates = jax.nn.sigmoid(_dot(hn.astype(_BF16), wgate_ref[...]))
    mixed = (gates[:, 0:D_MODEL] * _dot(oa_ref[...], wa_ref[...])
             + gates[:, D_MODEL:2 * D_MODEL] * _dot(ob_ref[...], wb_ref[...])
             + gates[:, 2 * D_MODEL:3 * D_MODEL] * _dot(oc_ref[...], wc_ref[...]))
    x = x + mod(2) * _dot(mixed.astype(_BF16), wout_ref[...])
    hn = ((x * _rms_scale(x) * gn2_ref[...]) * (1.0 + mod(4)) + mod(3)).astype(_BF16)
    h = _silu(_dot(hn, wfg_ref[...])) * _dot(hn, wfu_ref[...])
    out_ref[...] = x + mod(5) * _dot(h.astype(_BF16), wfd_ref[...])


def _post_call(layer, x2d, mod_all, oa, ob, oc, pw, tiles_per_seq, shared_mod):
    n = x2d.shape[0]
    row = lambda w: pl.BlockSpec((POST_TILE, w), lambda i, layer: (i, 0))
    if shared_mod:
        mod_spec = pl.BlockSpec((None, 1, 1, 6 * D_MODEL), lambda i, layer: (layer[0], 0, 0, 0))
    else:
        mod_spec = pl.BlockSpec((None, 1, 1, 6 * D_MODEL),
                                lambda i, layer: (layer[0], 1 + i // tiles_per_seq, 0, 0))
    in_specs = [row(D_MODEL), mod_spec, row(ATT_Q), row(GLA_VW), row(POOL_W),
                _layer_spec((1, D_MODEL)), _layer_spec((1, D_MODEL)), _layer_spec((D_MODEL, GATE_W)),
                _layer_spec((ATT_Q, D_MODEL)), _layer_spec((GLA_VW, D_MODEL)), _layer_spec((POOL_W, D_MODEL)),
                _layer_spec((D_MODEL, D_MODEL)), _layer_spec((D_MODEL, D_FF)), _layer_spec((D_MODEL, D_FF)),
                _layer_spec((D_FF, D_MODEL))]
    return pl.pallas_call(
        _post_kernel,
        grid_spec=pltpu.PrefetchScalarGridSpec(
            num_scalar_prefetch=1, grid=(n // POST_TILE,), in_specs=in_specs, out_specs=row(D_MODEL)),
        out_shape=jax.ShapeDtypeStruct(x2d.shape, _F32),
        compiler_params=pltpu.CompilerParams(dimension_semantics=("arbitrary",), vmem_limit_bytes=VMEM_LIMIT),
        name="post",
    )(layer, x2d, mod_all, oa, ob, oc, pw["g_norm1"], pw["g_norm2"], pw["w_gates"], pw["w_br_a"], pw["w_br_b"],
      pw["w_br_c"], pw["w_out"], pw["w_ff_gate"], pw["w_ff_up"], pw["w_ff_down"])


def _rope_tables(T):
    quarter = HEAD_DIM // 4
    inv_freq = ROPE_BASE ** (-np.arange(quarter, dtype=np.float32) / quarter)
    pos = np.arange(T)
    ang_row = (pos // GRID_W).astype(np.float32)[:, None] * inv_freq[None, :]
    ang_col = (pos % GRID_W).astype(np.float32)[:, None] * inv_freq[None, :]
    cos = np.concatenate([np.cos(ang_row)] * 2 + [np.cos(ang_col)] * 2, axis=-1)
    sin = np.concatenate([-np.sin(ang_row), np.sin(ang_row), -np.sin(ang_col), np.sin(ang_col)], axis=-1)
    return (jnp.asarray(np.tile(cos, (1, ATT_KV_HEADS)), _F32), jnp.asarray(np.tile(sin, (1, ATT_KV_HEADS)), _F32))


def _prepare_weights(w_in, g_qn, g_kn, att_sink, w_gate2, b_gate2, g_gla_out, w_pool, pool_scale, w_br_a,
                     w_br_b, w_br_c, w_out, g_norm1, g_norm2, w_ff_gate, w_ff_up, w_ff_down):
    o_gl = ATT_Q + 2 * ATT_KV + 2 * GLA_QK + 2 * GLA_VW
    o_uc = o_gl + 2 * GLA_RANK
    o_gate = o_uc + POOL_W
    w_mix = jnp.concatenate(
        [w_in[:, :, :o_gl], w_in[:, :, o_uc:o_gate], w_in[:, :, o_gl:o_uc],
         jnp.zeros((DEPTH, D_MODEL, GL_PAD - 2 * GLA_RANK), w_in.dtype)], axis=2).astype(_BF16)
    wg2 = jnp.zeros((DEPTH, GL_PAD, 2 * GLA_QK), _F32)
    wg2 = wg2.at[:, 0:GLA_RANK, 0:GLA_QK].set(w_gate2[:, 0])
    wg2 = wg2.at[:, GLA_RANK:2 * GLA_RANK, GLA_QK:].set(w_gate2[:, 1])
    vec = lambda a: a.reshape(DEPTH, 1, -1)
    return {
        "w_mix": w_mix,
        "w_gates": w_in[:, :, o_gate:].astype(_BF16),
        "g_qn": vec(jnp.tile(g_qn, (1, ATT_HEADS))),
        "g_kn": vec(jnp.tile(g_kn, (1, ATT_KV_HEADS))),
        "att_sink": att_sink,
        "w_gate2": wg2,
        "b_gate2": vec(b_gate2),
        "g_gla_out": vec(g_gla_out),
        "w_pool": w_pool.astype(_BF16),
        "pool_scale": vec(pool_scale),
        "w_br_a": w_br_a.astype(_BF16),
        "w_br_b": w_br_b.astype(_BF16),
        "w_br_c": w_br_c.astype(_BF16),
        "w_out": w_out.astype(_BF16),
        "g_norm1": vec(g_norm1),
        "g_norm2": vec(g_norm2),
        "w_ff_gate": w_ff_gate.astype(_BF16),
        "w_ff_up": w_ff_up.astype(_BF16),
        "w_ff_down": w_ff_down.astype(_BF16),
    }


def kernel(x_prompt, x_sample, c, cache_k, cache_v, state_gla, c_ctx, w_in, g_qn, g_kn, att_sink, w_gate2,
           b_gate2, g_gla_out, w_pool, pool_scale, w_br_a, w_br_b, w_br_c, w_out, g_norm1, g_norm2, w_mod,
           b_mod, w_ff_gate, w_ff_up, w_ff_down):
    B, T, _ = x_prompt.shape
    BL, TL, _ = x_sample.shape
    assert T % POST_TILE == 0 and TL % POST_TILE == 0 and BL + 1 <= MOD_ROWS
    cv = jnp.concatenate([c_ctx[None, :], c, jnp.zeros((MOD_ROWS - 1 - BL, D_MODEL), _F32)], axis=0)
    mod_all = _modulation(cv, w_mod, b_mod).reshape(DEPTH, MOD_ROWS, 1, 6 * D_MODEL)
    pw = _prepare_weights(w_in, g_qn, g_kn, att_sink, w_gate2, b_gate2, g_gla_out, w_pool, pool_scale, w_br_a,
                          w_br_b, w_br_c, w_out, g_norm1, g_norm2, w_ff_gate, w_ff_up, w_ff_down)
    cos, sin = _rope_tables(TL)
    P = cache_k.shape[2]
    latent_ctx = (cache_k.reshape(BL, DEPTH, P, ATT_KV), cache_v.reshape(BL, DEPTH, P, ATT_KV),
                  jnp.swapaxes(state_gla, -1, -2), cos, sin)

    def layer_step(l, carry):
        yp, ys, new_k, new_v, new_st = carry
        layer = jnp.full((1,), l, jnp.int32)
        oa, ob, oc, k_l, v_l, s_l = _mix_call(False, layer, yp, mod_all, pw, None)
        yp = _post_call(layer, yp.reshape(B * T, D_MODEL), mod_all, oa.reshape(B * T, -1), ob.reshape(B * T, -1),
                        oc.reshape(B * T, -1), pw, T // POST_TILE, True).reshape(B, T, D_MODEL)
        new_k = lax.dynamic_update_slice_in_dim(new_k, k_l[:, None], l, axis=1)
        new_v = lax.dynamic_update_slice_in_dim(new_v, v_l[:, None], l, axis=1)
        new_st = lax.dynamic_update_slice_in_dim(new_st, s_l[:, None], l, axis=1)
        oa, ob, oc = _mix_call(True, layer, ys, mod_all, pw, latent_ctx)
        ys = _post_call(layer, ys.reshape(BL * TL, D_MODEL), mod_all, oa.reshape(BL * TL, -1),
                        ob.reshape(BL * TL, -1), oc.reshape(BL * TL, -1), pw, TL // POST_TILE,
                        False).reshape(BL, TL, D_MODEL)
        return yp, ys, new_k, new_v, new_st

    init = (x_prompt, x_sample,
            jnp.zeros((B, DEPTH, T, ATT_KV), _F32), jnp.zeros((B, DEPTH, T, ATT_KV), _F32),
            jnp.zeros((B, DEPTH, 2, GLA_HEADS, GLA_DK, GLA_DV), _F32))
    yp, ys, new_k, new_v, new_st = lax.fori_loop(0, DEPTH, layer_step, init)
    return (yp, ys, new_k.reshape(B, DEPTH, T, ATT_KV_HEADS, HEAD_DIM),
            new_v.reshape(B, DEPTH, T, ATT_KV_HEADS, HEAD_DIM), new_st)
```

```python
import functools

import jax
import jax.numpy as jnp
import numpy as np
from jax import lax
from jax.experimental import pallas as pl
from jax.experimental.pallas import tpu as pltpu

D_MODEL = 1024
DEPTH = 4
GRID_W = 64
ATT_HEADS = 8
ATT_KV_HEADS = 2
ATT_GROUP = ATT_HEADS // ATT_KV_HEADS
HEAD_DIM = 64
WINDOW = 128
ATT_BLOCK = 128
ROPE_BASE = 10000.0
GLA_HEADS = 4
GLA_DK = 64
GLA_DV = 128
GLA_RANK = 16
GLA_TAU = 16.0
GLA_CHUNK = 64
POOL_GROUPS = 4
POOL_GROUP_DIM = 128
POOL_WINDOWS = (2, 4, 8, 16)
D_FF = 2816
ATT_Q = ATT_HEADS * HEAD_DIM
ATT_KV = ATT_KV_HEADS * HEAD_DIM
GLA_QK = GLA_HEADS * GLA_DK
GLA_VW = GLA_HEADS * GLA_DV
POOL_W = POOL_GROUPS * POOL_GROUP_DIM
EPS = 1e-6
NEG = -1e30

C_QA = 0
C_KA = C_QA + ATT_Q
C_VA = C_KA + ATT_KV
C_QB = C_VA + ATT_KV
C_KB = C_QB + GLA_QK
C_VB = C_KB + GLA_QK
C_RB = C_VB + GLA_VW
C_UC = C_RB + GLA_VW
C_GL = C_UC + POOL_W
GL_PAD = 128
MIX_W = C_GL + GL_PAD
GATE_W = 3 * D_MODEL

POST_TILE = 256
PROJ_TILE = 256
POOL_TILE = 256
POOL_HALO = 128
MOD_ROWS = 8
MOD_TILE = 1024
GLA_BLOCK = 256
GLA_MAX_EXPONENT = 80.0
VMEM_LIMIT = 56 * 1024 * 1024

_F32 = jnp.float32
_BF16 = jnp.bfloat16


def _dot(a, b):
    return jnp.dot(a, b, preferred_element_type=_F32)


def _dot_nt(a, b):
    return lax.dot_general(a, b, (((1,), (1,)), ((), ())), preferred_element_type=_F32)


def _dot_tn(a, b):
    return lax.dot_general(a, b, (((0,), (0,)), ((), ())), preferred_element_type=_F32)


def _rms_scale(x):
    return lax.rsqrt(jnp.mean(x * x, axis=-1, keepdims=True) + EPS)


def _head_norm(x, n_heads, width):
    parts = []
    for h in range(n_heads):
        xh = x[:, h * width:(h + 1) * width]
        parts.append(xh * _rms_scale(xh))
    return jnp.concatenate(parts, axis=-1)


def _log_sigmoid(x):
    return jnp.minimum(x, 0.0) - jnp.log1p(jnp.exp(-jnp.abs(x)))


def _silu(x):
    return x * jax.nn.sigmoid(x)


def _rope(x, cos, sin_signed):
    n = x.shape[-1]
    lane = lax.broadcasted_iota(jnp.int32, x.shape, 1)
    up = pltpu.roll(x, n - HEAD_DIM // 4, axis=1)
    down = pltpu.roll(x, HEAD_DIM // 4, axis=1)
    partner = jnp.where((lane & (HEAD_DIM // 2 - 1)) < HEAD_DIM // 4, up, down)
    return x * cos + partner * sin_signed


def _sink_softmax_av(scores, values, sink_col):
    m = sink_col
    for s in scores:
        m = jnp.maximum(m, jnp.max(s, axis=-1, keepdims=True))
    den = jnp.exp(sink_col - m)
    acc = None
    for s, v in zip(scores, values):
        p = jnp.exp(s - m).astype(_BF16)
        den = den + _dot(p, jnp.ones_like(v))
        pv = _dot(p, v)
        acc = pv if acc is None else acc + pv
    return acc / den


def _mod_kernel(cv_ref, w_ref, b_ref, out_ref):
    s = _silu(cv_ref[...]).astype(_BF16)
    out_ref[0] = _dot(s, w_ref[0].astype(_BF16)) + b_ref[0]


def _modulation(cv, w_mod, b_mod):
    n_col = (6 * D_MODEL) // MOD_TILE
    return pl.pallas_call(
        _mod_kernel,
        grid=(DEPTH, n_col),
        in_specs=[
            pl.BlockSpec((MOD_ROWS, D_MODEL), lambda l, j: (0, 0)),
            pl.BlockSpec((1, D_MODEL, MOD_TILE), lambda l, j: (l, 0, j)),
            pl.BlockSpec((1, 1, MOD_TILE), lambda l, j: (l, 0, j)),
        ],
        out_specs=pl.BlockSpec((1, MOD_ROWS, MOD_TILE), lambda l, j: (l, 0, j)),
        out_shape=jax.ShapeDtypeStruct((DEPTH, MOD_ROWS, 6 * D_MODEL), _F32),
        name="modulation",
    )(cv, w_mod, b_mod.reshape(DEPTH, 1, 6 * D_MODEL))


def _stacked_queries(qr_ref, rows, kv):
    return jnp.concatenate(
        [qr_ref[rows, (kv * ATT_GROUP + g) * HEAD_DIM:(kv * ATT_GROUP + g + 1) * HEAD_DIM]
         for g in range(ATT_GROUP)], axis=0)


def _stacked_sink(sink_ref, layer, kv, n):
    return jnp.concatenate(
        [jnp.full((n, 1), sink_ref[layer, kv * ATT_GROUP + g], _F32) for g in range(ATT_GROUP)], axis=0)


def _attention_ctx(T, layer, qr_ref, kr_ref, vr_ref, sink_ref, oa_ref):
    heads = [None] * ATT_HEADS
    for kv in range(ATT_KV_HEADS):
        cols = slice(kv * HEAD_DIM, (kv + 1) * HEAD_DIM)
        qs = _stacked_queries(qr_ref, slice(0, T), kv)
        o = _sink_softmax_av([_dot_nt(qs, kr_ref[0:T, cols])], [vr_ref[0:T, cols]],
                             _stacked_sink(sink_ref, layer, kv, T))
        for g in range(ATT_GROUP):
            heads[kv * ATT_GROUP + g] = o[g * T:(g + 1) * T]
    oa_ref[0] = jnp.concatenate(heads, axis=-1).astype(_BF16)


def _attention_latent(T, layer, qr_ref, kr_ref, vr_ref, kc_ref, vc_ref, sink_ref, oa_ref):
    rows = ATT_GROUP * ATT_BLOCK
    span = 3 * ATT_BLOCK
    kc = kc_ref[0, 0].astype(_BF16)
    vc = vc_ref[0, 0].astype(_BF16)

    def block(i, carry):
        q_rows = pl.ds(pl.multiple_of(i * ATT_BLOCK, ATT_BLOCK), ATT_BLOCK)
        k_rows = pl.ds(pl.multiple_of(i * ATT_BLOCK, ATT_BLOCK), span)
        q_pos = i * ATT_BLOCK + (lax.broadcasted_iota(jnp.int32, (rows, span), 0) & (ATT_BLOCK - 1))
        k_pos = (i - 1) * ATT_BLOCK + lax.broadcasted_iota(jnp.int32, (rows, span), 1)
        valid = (jnp.abs(k_pos - q_pos) <= WINDOW) & (k_pos >= 0) & (k_pos < T)
        heads = [None] * ATT_HEADS
        for kv in range(ATT_KV_HEADS):
            cols = slice(kv * HEAD_DIM, (kv + 1) * HEAD_DIM)
            qs = _stacked_queries(qr_ref, q_rows, kv)
            s_loc = jnp.where(valid, _dot_nt(qs, kr_ref[k_rows, cols]), NEG)
            s_ctx = _dot_nt(qs, kc[:, cols])
            o = _sink_softmax_av([s_loc, s_ctx], [vr_ref[k_rows, cols], vc[:, cols]],
                                 _stacked_sink(sink_ref, layer, kv, ATT_BLOCK))
            for g in range(ATT_GROUP):
                heads[kv * ATT_GROUP + g] = o[g * ATT_BLOCK:(g + 1) * ATT_BLOCK]
        oa_ref[0, q_rows, :] = jnp.concatenate(heads, axis=-1).astype(_BF16)
        return carry

    lax.fori_loop(0, T // ATT_BLOCK, block, 0)


def _gla_chunk(z_ref, la_ref, o_ref, st_ref, a_ref, direction, start):
    C = GLA_CHUNK
    rows = pl.ds(pl.multiple_of(start, C), C)
    q = z_ref[rows, C_QB:C_QB + GLA_QK] * (GLA_DK ** -0.5)
    k = z_ref[rows, C_KB:C_KB + GLA_QK]
    v = z_ref[rows, C_VB:C_VB + GLA_VW].astype(_BF16)
    la = la_ref[rows, direction * GLA_QK:(direction + 1) * GLA_QK]
    la_hi = la.astype(_BF16)
    la_lo = (la - la_hi.astype(_F32)).astype(_BF16)
    ti = lax.broadcasted_iota(jnp.int32, (C, C), 0)
    si = lax.broadcasted_iota(jnp.int32, (C, C), 1)
    causal = (si <= ti) if direction == 0 else (si >= ti)
    tri = jnp.where(causal, 1.0, 0.0).astype(_BF16)
    b = _dot(tri, la_hi) + _dot(tri, la_lo)
    end = C - 1 if direction == 0 else 0
    b_end = b[end:end + 1]
    q_in = (q * jnp.exp(b)).astype(_BF16)
    k_out = (k * jnp.exp(b_end - b)).astype(_BF16)
    e_end = jnp.exp(b_end)
    ones = jnp.ones((8, GLA_DK), _BF16)

    def row_group(g, carry):
        base = pl.multiple_of(g * 8, 8)
        b8 = a_ref[GLA_HEADS, pl.ds(base, 8), :]
        q8 = a_ref[GLA_HEADS + 1, pl.ds(base, 8), :]
        s_idx = lax.broadcasted_iota(jnp.int32, (C, 1), 0)
        rows_h = [[] for _ in range(GLA_HEADS)]
        for j in range(8):
            ok = (s_idx <= base + j) if direction == 0 else (s_idx >= base + j)
            decay = jnp.exp(jnp.where(ok, b8[j:j + 1] - b, NEG))
            p = (q8[j:j + 1] * k * decay).astype(_BF16)
            for h in range(GLA_HEADS):
                rows_h[h].append(_dot_nt(ones, p[:, h * GLA_DK:(h + 1) * GLA_DK])[0:1])
        for h in range(GLA_HEADS):
            a_ref[h, pl.ds(base, 8), 0:C] = jnp.concatenate(rows_h[h], axis=0)
        return carry

    a_ref[GLA_HEADS] = b
    a_ref[GLA_HEADS + 1] = q
    lax.fori_loop(0, C // 8, row_group, 0)
    outs = []
    for h in range(GLA_HEADS):
        kc = slice(h * GLA_DK, (h + 1) * GLA_DK)
        vc = slice(h * GLA_DV, (h + 1) * GLA_DV)
        s_t = st_ref[direction, h]
        o_h = _dot_nt(q_in[:, kc], s_t.astype(_BF16)) + _dot(a_ref[h, :, 0:C].astype(_BF16), v[:, vc])
        st_ref[direction, h] = s_t * e_end[:, kc] + _dot_tn(v[:, vc], k_out[:, kc])
        outs.append(o_h)
    o_ref[rows, :] += jnp.concatenate(outs, axis=-1)


def _gla_block(z_ref, la_ref, o_ref, st_ref, direction, start, use_state):
    NB = GLA_BLOCK
    fwd = direction == 0
    rows = slice(start, start + NB) if isinstance(start, int) else pl.ds(pl.multiple_of(start, NB), NB)
    q = z_ref[rows, C_QB:C_QB + GLA_QK] * (GLA_DK ** -0.5)
    k = z_ref[rows, C_KB:C_KB + GLA_QK]
    v = z_ref[rows, C_VB:C_VB + GLA_VW].astype(_BF16)
    b = la_ref[rows, direction * GLA_QK:(direction + 1) * GLA_QK]
    end, mid = (NB - 1, NB // 2 - 1) if fwd else (0, NB // 2)
    b_end = b[end:end + 1]
    c = b - b[mid:mid + 1]
    q_c = (q * jnp.exp(c)).astype(_BF16)
    k_c = (k * jnp.exp(-c)).astype(_BF16)
    k_fin = (k * jnp.exp(b_end - b)).astype(_BF16)
    if use_state:
        q_in = (q * jnp.exp(b)).astype(_BF16)
        e_all = jnp.exp(b_end)
    ti = lax.broadcasted_iota(jnp.int32, (NB, NB), 0)
    si = lax.broadcasted_iota(jnp.int32, (NB, NB), 1)
    causal = (si <= ti) if fwd else (si >= ti)
    outs = []
    for h in range(GLA_HEADS):
        kc = slice(h * GLA_DK, (h + 1) * GLA_DK)
        vc = slice(h * GLA_DV, (h + 1) * GLA_DV)
        a = jnp.where(causal, _dot_nt(q_c[:, kc], k_c[:, kc]), 0.0)
        o_h = _dot(a.astype(_BF16), v[:, vc])
        s_new = _dot_tn(v[:, vc], k_fin[:, kc])
        if use_state:
            s_t = st_ref[direction, h]
            o_h = o_h + _dot_nt(q_in[:, kc], s_t.astype(_BF16))
            s_new = s_new + s_t * e_all[:, kc]
        st_ref[direction, h] = s_new
        outs.append(o_h)
    o_ref[rows, :] += jnp.concatenate(outs, axis=-1)


def _log_decay(z_ref, rows, wg2_ref, bg2_ref):
    pre = lax.dot_general(z_ref[rows, C_GL:C_GL + GL_PAD], wg2_ref[...], (((1,), (0,)), ((), ())),
                          precision=lax.Precision.HIGHEST, preferred_element_type=_F32) + bg2_ref[...]
    return _log_sigmoid(pre) * (1.0 / GLA_TAU)


def _gla(T, z_ref, la_ref, o_ref, st_ref, a_ref, wg2_ref, bg2_ref, use_state):
    NB = GLA_BLOCK
    n_chunks = T // GLA_CHUNK
    n_blocks = T // NB
    ti = lax.broadcasted_iota(jnp.int32, (NB, NB), 0)
    si = lax.broadcasted_iota(jnp.int32, (NB, NB), 1)
    tri = [jnp.where(si <= ti, 1.0, 0.0).astype(_BF16), jnp.where(si >= ti, 1.0, 0.0).astype(_BF16)]
    worst = None
    for r0 in range(0, T, NB):
        rows = slice(r0, r0 + NB)
        la = _log_decay(z_ref, rows, wg2_ref, bg2_ref)
        la_hi = la.astype(_BF16)
        la_lo = (la - la_hi.astype(_F32)).astype(_BF16)
        for d in range(2):
            cols = slice(d * GLA_QK, (d + 1) * GLA_QK)
            b = _dot(tri[d], la_hi[:, cols]) + _dot(tri[d], la_lo[:, cols])
            la_ref[rows, cols] = b
            first, mid, last = (0, NB // 2 - 1, NB - 1) if d == 0 else (NB - 1, NB // 2, 0)
            span = jnp.max(jnp.maximum(b[first:first + 1] - b[mid:mid + 1], b[mid:mid + 1] - b[last:last + 1]))
            worst = span if worst is None else jnp.maximum(worst, span)
    fast_ok = worst <= GLA_MAX_EXPONENT
    o_ref[...] = jnp.zeros(o_ref.shape, _F32)

    @pl.when(fast_ok)
    def _():
        if n_blocks == 1:
            for direction in range(2):
                _gla_block(z_ref, la_ref, o_ref, st_ref, direction, 0, use_state)
        else:
            def body(i, carry):
                _gla_block(z_ref, la_ref, o_ref, st_ref, 0, i * NB, True)
                _gla_block(z_ref, la_ref, o_ref, st_ref, 1, (n_blocks - 1 - i) * NB, True)
                return carry
            lax.fori_loop(0, n_blocks, body, 0)

    @pl.when(jnp.logical_not(fast_ok))
    def _():
        for r0 in range(0, T, NB):
            la_ref[r0:r0 + NB, :] = _log_decay(z_ref, slice(r0, r0 + NB), wg2_ref, bg2_ref)

        def body(i, carry):
            _gla_chunk(z_ref, la_ref, o_ref, st_ref, a_ref, 0, i * GLA_CHUNK)
            _gla_chunk(z_ref, la_ref, o_ref, st_ref, a_ref, 1, (n_chunks - 1 - i) * GLA_CHUNK)
            return carry
        lax.fori_loop(0, n_chunks, body, 0)


def _gla_finish(T, z_ref, o_ref, ggla_ref, ob_ref):
    for r0 in range(0, T, PROJ_TILE):
        rows = slice(r0, r0 + PROJ_TILE)
        parts = []
        for h in range(GLA_HEADS):
            oh = o_ref[rows, h * GLA_DV:(h + 1) * GLA_DV]
            parts.append(oh * _rms_scale(oh) * ggla_ref[...])
        y = jnp.concatenate(parts, axis=-1) * _silu(z_ref[rows, C_RB:C_RB + GLA_VW])
        ob_ref[0, rows, :] = y.astype(_BF16)


def _pool(T, z_ref, upad_ref, wpool_ref, pscale_ref, oc_ref):
    u = z_ref[:, C_UC:C_UC + POOL_W]
    u_hi = u.astype(_BF16)
    u_lo = (u - u_hi.astype(_F32)).astype(_BF16)
    zeros = jnp.zeros((POOL_HALO, POOL_W), _BF16)
    for part, val in ((0, u_hi), (1, u_lo)):
        upad_ref[part, 0:POOL_HALO, :] = zeros
        upad_ref[part, POOL_HALO + T:POOL_HALO + T + POOL_HALO, :] = zeros
        upad_ref[part, POOL_HALO:POOL_HALO + T, :] = val
    span = POOL_TILE + 2 * POOL_HALO
    r = lax.broadcasted_iota(jnp.int32, (POOL_TILE, span), 0)
    c = lax.broadcasted_iota(jnp.int32, (POOL_TILE, span), 1)
    off = c - POOL_HALO - r
    for jb in range(T // POOL_TILE):
        t = jb * POOL_TILE + lax.broadcasted_iota(jnp.int32, (POOL_TILE, 1), 0)
        parts = []
        for g, w in enumerate(POOL_WINDOWS):
            cols = slice(g * POOL_GROUP_DIM, (g + 1) * POOL_GROUP_DIM)
            band = jnp.where((off >= -(w // 2)) & (off < w - w // 2), 1.0, 0.0).astype(_BF16)
            win = slice(jb * POOL_TILE, jb * POOL_TILE + span)
            total = _dot(band, upad_ref[0, win, cols]) + _dot(band, upad_ref[1, win, cols])
            cnt = (jnp.minimum(t - w // 2 + w, T) - jnp.maximum(t - w // 2, 0)).astype(_F32)
            pooled = total / cnt - z_ref[jb * POOL_TILE:(jb + 1) * POOL_TILE, C_UC + g * POOL_GROUP_DIM:
                                         C_UC + (g + 1) * POOL_GROUP_DIM]
            parts.append(_dot(pooled.astype(_BF16), wpool_ref[g]))
        y = jnp.concatenate(parts, axis=-1) * pscale_ref[...]
        oc_ref[0, jb * POOL_TILE:(jb + 1) * POOL_TILE, :] = y.astype(_BF16)


def _mix_kernel(latent, T, layer_ref, *refs):
    (x_ref, mod_ref, gn1_ref, wmix_ref, gqn_ref, gkn_ref, sink_ref, wg2_ref, bg2_ref, ggla_ref,
     wpool_ref, pscale_ref) = refs[:12]
    refs = refs[12:]
    if latent:
        kc_ref, vc_ref, st0_ref, cos_ref, sin_ref, oa_ref, ob_ref, oc_ref = refs[:8]
        refs = refs[8:]
    else:
        oa_ref, ob_ref, oc_ref, kout_ref, vout_ref, stout_ref = refs[:6]
        refs = refs[6:]
    z_ref, qr_ref, kr_ref, vr_ref, la_ref, o_ref, st_ref, a_ref, upad_ref = refs
    layer = layer_ref[0]
    pad = ATT_BLOCK if latent else 0

    if latent:
        zeros = jnp.zeros((pad, ATT_KV), _BF16)
        for ref in (kr_ref, vr_ref):
            ref[0:pad, :] = zeros
            ref[pad + T:pad + T + pad, :] = zeros
    shift = mod_ref[0, :, 0:D_MODEL]
    scale = mod_ref[0, :, D_MODEL:2 * D_MODEL]
    for r0 in range(0, T, PROJ_TILE):
        rows = slice(r0, r0 + PROJ_TILE)
        x = x_ref[0, rows, :]
        hn = (x * _rms_scale(x) * gn1_ref[...]) * (1.0 + scale) + shift
        z_ref[rows, :] = _dot(hn.astype(_BF16), wmix_ref[...])
        q = _head_norm(z_ref[rows, C_QA:C_QA + ATT_Q], ATT_HEADS, HEAD_DIM) * gqn_ref[...]
        k = _head_norm(z_ref[rows, C_KA:C_KA + ATT_KV], ATT_KV_HEADS, HEAD_DIM) * gkn_ref[...]
        v = z_ref[rows, C_VA:C_VA + ATT_KV]
        if latent:
            cos = jnp.concatenate([cos_ref[rows, :]] * (ATT_Q // ATT_KV), axis=-1)
            sin = jnp.concatenate([sin_ref[rows, :]] * (ATT_Q // ATT_KV), axis=-1)
            q = _rope(q, cos, sin)
            k = _rope(k, cos_ref[rows, :], sin_ref[rows, :])
        else:
            kout_ref[0, rows, :] = k
            vout_ref[0, rows, :] = v
        qr_ref[rows, :] = (q * (HEAD_DIM ** -0.5)).astype(_BF16)
        kr_ref[pad + r0:pad + r0 + PROJ_TILE, :] = k.astype(_BF16)
        vr_ref[pad + r0:pad + r0 + PROJ_TILE, :] = v.astype(_BF16)
    if latent:
        _attention_latent(T, layer, qr_ref, kr_ref, vr_ref, kc_ref, vc_ref, sink_ref, oa_ref)
    else:
        _attention_ctx(T, layer, qr_ref, kr_ref, vr_ref, sink_ref, oa_ref)

    if latent:
        st_ref[...] = st0_ref[0]
    else:
        st_ref[...] = jnp.zeros(st_ref.shape, _F32)
    _gla(T, z_ref, la_ref, o_ref, st_ref, a_ref, wg2_ref, bg2_ref, latent)
    _gla_finish(T, z_ref, o_ref, ggla_ref, ob_ref)
    if not latent:
        for d in range(2):
            for h in range(GLA_HEADS):
                stout_ref[0, d, h] = st_ref[d, h].T

    _pool(T, z_ref, upad_ref, wpool_ref, pscale_ref, oc_ref)


def _layer_spec(shape):
    zeros = (0,) * len(shape)
    return pl.BlockSpec((None,) + tuple(shape), lambda i, layer: (layer[0],) + zeros,
                        pipeline_mode=pl.Buffered(1))


def _mix_call(latent, layer, x, mod_all, pw, extra):
    B, T, _ = x.shape
    per_seq = lambda blk: pl.BlockSpec(blk, lambda b, layer: (b,) + (0,) * (len(blk) - 1))
    if latent:
        mod_spec = pl.BlockSpec((None, 1, 1, 6 * D_MODEL), lambda b, layer: (layer[0], b + 1, 0, 0))
    else:
        mod_spec = pl.BlockSpec((None, 1, 1, 6 * D_MODEL), lambda b, layer: (layer[0], 0, 0, 0))
    x_spec = pl.BlockSpec((1, T, D_MODEL), lambda b, layer: (b, 0, 0),
                          pipeline_mode=pl.Buffered(1) if latent else None)
    in_specs = [
        x_spec, mod_spec,
        _layer_spec((1, D_MODEL)), _layer_spec((D_MODEL, MIX_W)),
        _layer_spec((1, ATT_Q)), _layer_spec((1, ATT_KV)),
        pl.BlockSpec(memory_space=pltpu.SMEM),
        _layer_spec((GL_PAD, 2 * GLA_QK)), _layer_spec((1, 2 * GLA_QK)), _layer_spec((1, GLA_DV)),
        _layer_spec((POOL_GROUPS, POOL_GROUP_DIM, POOL_GROUP_DIM)), _layer_spec((1, POOL_W)),
    ]
    args = [x, mod_all, pw["g_norm1"], pw["w_mix"], pw["g_qn"], pw["g_kn"], pw["att_sink"], pw["w_gate2"],
            pw["b_gate2"], pw["g_gla_out"], pw["w_pool"], pw["pool_scale"]]
    act = lambda width: jax.ShapeDtypeStruct((B, T, width), _BF16)
    out_shape = [act(ATT_Q), act(GLA_VW), act(POOL_W)]
    out_specs = [per_seq((1, T, ATT_Q)), per_seq((1, T, GLA_VW)), per_seq((1, T, POOL_W))]
    if latent:
        cache_k, cache_v, st0, cos, sin = extra
        P = cache_k.shape[2]
        cache_spec = pl.BlockSpec((1, 1, P, ATT_KV), lambda b, layer: (b, layer[0], 0, 0))
        table_spec = pl.BlockSpec((T, ATT_KV), lambda b, layer: (0, 0), pipeline_mode=pl.Buffered(1))
        in_specs += [cache_spec, cache_spec,
                     pl.BlockSpec((1, None, 2, GLA_HEADS, GLA_DV, GLA_DK),
                                  lambda b, layer: (b, layer[0], 0, 0, 0, 0)),
                     table_spec, table_spec]
        args += [cache_k, cache_v, st0, cos, sin]
    else:
        out_shape += [jax.ShapeDtypeStruct((B, T, ATT_KV), _F32), jax.ShapeDtypeStruct((B, T, ATT_KV), _F32),
                      jax.ShapeDtypeStruct((B, 2, GLA_HEADS, GLA_DK, GLA_DV), _F32)]
        out_specs += [per_seq((1, T, ATT_KV)), per_seq((1, T, ATT_KV)),
                      per_seq((1, 2, GLA_HEADS, GLA_DK, GLA_DV))]
    kv_rows = T + 2 * ATT_BLOCK if latent else T
    scratch = [
        pltpu.VMEM((T, MIX_W), _F32),
        pltpu.VMEM((T, ATT_Q), _BF16), pltpu.VMEM((kv_rows, ATT_KV), _BF16), pltpu.VMEM((kv_rows, ATT_KV), _BF16),
        pltpu.VMEM((T, 2 * GLA_QK), _F32),
        pltpu.VMEM((T, GLA_VW), _F32),
        pltpu.VMEM((2, GLA_HEADS, GLA_DV, GLA_DK), _F32),
        pltpu.VMEM((GLA_HEADS + 2, GLA_CHUNK, GLA_QK), _F32),
        pltpu.VMEM((2, T + 2 * POOL_HALO, POOL_W), _BF16),
    ]
    return pl.pallas_call(
        functools.partial(_mix_kernel, latent, T),
        grid_spec=pltpu.PrefetchScalarGridSpec(
            num_scalar_prefetch=1, grid=(B,), in_specs=in_specs, out_specs=out_specs, scratch_shapes=scratch),
        out_shape=out_shape,
        compiler_params=pltpu.CompilerParams(dimension_semantics=("arbitrary",), vmem_limit_bytes=VMEM_LIMIT),
        name="mix_latent" if latent else "mix_ctx",
    )(layer, *args)


def _post_kernel(layer_ref, x_ref, mod_ref, oa_ref, ob_ref, oc_ref, gn1_ref, gn2_ref, wgate_ref, wa_ref, wb_ref,
                 wc_ref, wout_ref, wfg_ref, wfu_ref, wfd_ref, out_ref):
    x = x_ref[...]
    mod = lambda i: mod_ref[0, :, i * D_MODEL:(i + 1) * D_MODEL]
    hn = (x * _rms_scale(x) * gn1_ref[...]) * (1.0 + mod(1)) + mod(0)
    gates = jax.nn.sigmoid(_dot(hn.astype(_BF16), wgate_ref[...]))
    mixed = (gates[:, 0:D_MODEL] * _dot(oa_ref[...], wa_ref[...])
             + gates[:, D_MODEL:2 * D_MODEL] * _dot(ob_ref[...], wb_ref[...])
             + gates[:, 2 * D_MODEL:3 * D_MODEL] * _dot(oc_ref[...], wc_ref[...]))
    x = x + mod(2) * _dot(mixed.astype(_BF16), wout_ref[...])
    hn = ((x * _rms_scale(x) * gn2_ref[...]) * (1.0 + mod(4)) + mod(3)).astype(_BF16)
    h = _silu(_dot(hn, wfg_ref[...])) * _dot(hn, wfu_ref[...])
    out_ref[...] = x + mod(5) * _dot(h.astype(_BF16), wfd_ref[...])


def _post_call(layer, x2d, mod_all, oa, ob, oc, pw, tiles_per_seq, shared_mod):
    n = x2d.shape[0]
    row = lambda w: pl.BlockSpec((POST_TILE, w), lambda i, layer: (i, 0))
    if shared_mod:
        mod_spec = pl.BlockSpec((None, 1, 1, 6 * D_MODEL), lambda i, layer: (layer[0], 0, 0, 0))
    else:
        mod_spec = pl.BlockSpec((None, 1, 1, 6 * D_MODEL),
                                lambda i, layer: (layer[0], 1 + i // tiles_per_seq, 0, 0))
    in_specs = [row(D_MODEL), mod_spec, row(ATT_Q), row(GLA_VW), row(POOL_W),
                _layer_spec((1, D_MODEL)), _layer_spec((1, D_MODEL)), _layer_spec((D_MODEL, GATE_W)),
                _layer_spec((ATT_Q, D_MODEL)), _layer_spec((GLA_VW, D_MODEL)), _layer_spec((POOL_W, D_MODEL)),
                _layer_spec((D_MODEL, D_MODEL)), _layer_spec((D_MODEL, D_FF)), _layer_spec((D_MODEL, D_FF)),
                _layer_spec((D_FF, D_MODEL))]
    return pl.pallas_call(
        _post_kernel,
        grid_spec=pltpu.PrefetchScalarGridSpec(
            num_scalar_prefetch=1, grid=(n // POST_TILE,), in_specs=in_specs, out_specs=row(D_MODEL)),
        out_shape=jax.ShapeDtypeStruct(x2d.shape, _F32),
        compiler_params=pltpu.CompilerParams(dimension_semantics=("arbitrary",), vmem_limit_bytes=VMEM_LIMIT),
        name="post",
    )(layer, x2d, mod_all, oa, ob, oc, pw["g_norm1"], pw["g_norm2"], pw["w_gates"], pw["w_br_a"], pw["w_br_b"],
      pw["w_br_c"], pw["w_out"], pw["w_ff_gate"], pw["w_ff_up"], pw["w_ff_down"])


def _rope_tables(T):
    quarter = HEAD_DIM // 4
    inv_freq = ROPE_BASE ** (-np.arange(quarter, dtype=np.float32) / quarter)
    pos = np.arange(T)
    ang_row = (pos // GRID_W).astype(np.float32)[:, None] * inv_freq[None, :]
    ang_col = (pos % GRID_W).astype(np.float32)[:, None] * inv_freq[None, :]
    cos = np.concatenate([np.cos(ang_row)] * 2 + [np.cos(ang_col)] * 2, axis=-1)
    sin = np.concatenate([-np.sin(ang_row), np.sin(ang_row), -np.sin(ang_col), np.sin(ang_col)], axis=-1)
    return (jnp.asarray(np.tile(cos, (1, ATT_KV_HEADS)), _F32), jnp.asarray(np.tile(sin, (1, ATT_KV_HEADS)), _F32))


def _prepare_weights(w_in, g_qn, g_kn, att_sink, w_gate2, b_gate2, g_gla_out, w_pool, pool_scale, w_br_a,
                     w_br_b, w_br_c, w_out, g_norm1, g_norm2, w_ff_gate, w_ff_up, w_ff_down):
    o_gl = ATT_Q + 2 * ATT_KV + 2 * GLA_QK + 2 * GLA_VW
    o_uc = o_gl + 2 * GLA_RANK
    o_gate = o_uc + POOL_W
    w_mix = jnp.concatenate(
        [w_in[:, :, :o_gl], w_in[:, :, o_uc:o_gate], w_in[:, :, o_gl:o_uc],
         jnp.zeros((DEPTH, D_MODEL, GL_PAD - 2 * GLA_RANK), w_in.dtype)], axis=2).astype(_BF16)
    wg2 = jnp.zeros((DEPTH, GL_PAD, 2 * GLA_QK), _F32)
    wg2 = wg2.at[:, 0:GLA_RANK, 0:GLA_QK].set(w_gate2[:, 0])
    wg2 = wg2.at[:, GLA_RANK:2 * GLA_RANK, GLA_QK:].set(w_gate2[:, 1])
    vec = lambda a: a.reshape(DEPTH, 1, -1)
    return {
        "w_mix": w_mix,
        "w_gates": w_in[:, :, o_gate:].astype(_BF16),
        "g_qn": vec(jnp.tile(g_qn, (1, ATT_HEADS))),
        "g_kn": vec(jnp.tile(g_kn, (1, ATT_KV_HEADS))),
        "att_sink": att_sink,
        "w_gate2": wg2,
        "b_gate2": vec(b_gate2),
        "g_gla_out": vec(g_gla_out),
        "w_pool": w_pool.astype(_BF16),
        "pool_scale": vec(pool_scale),
        "w_br_a": w_br_a.astype(_BF16),
        "w_br_b": w_br_b.astype(_BF16),
        "w_br_c": w_br_c.astype(_BF16),
        "w_out": w_out.astype(_BF16),
        "g_norm1": vec(g_norm1),
        "g_norm2": vec(g_norm2),
        "w_ff_gate": w_ff_gate.astype(_BF16),
        "w_ff_up": w_ff_up.astype(_BF16),
        "w_ff_down": w_ff_down.astype(_BF16),
    }


def kernel(x_prompt, x_sample, c, cache_k, cache_v, state_gla, c_ctx, w_in, g_qn, g_kn, att_sink, w_gate2,
           b_gate2, g_gla_out, w_pool, pool_scale, w_br_a, w_br_b, w_br_c, w_out, g_norm1, g_norm2, w_mod,
           b_mod, w_ff_gate, w_ff_up, w_ff_down):
    B, T, _ = x_prompt.shape
    BL, TL, _ = x_sample.shape
    assert T % POST_TILE == 0 and TL % POST_TILE == 0 and BL + 1 <= MOD_ROWS
    cv = jnp.concatenate([c_ctx[None, :], c, jnp.zeros((MOD_ROWS - 1 - BL, D_MODEL), _F32)], axis=0)
    mod_all = _modulation(cv, w_mod, b_mod).reshape(DEPTH, MOD_ROWS, 1, 6 * D_MODEL)
    pw = _prepare_weights(w_in, g_qn, g_kn, att_sink, w_gate2, b_gate2, g_gla_out, w_pool, pool_scale, w_br_a,
                          w_br_b, w_br_c, w_out, g_norm1, g_norm2, w_ff_gate, w_ff_up, w_ff_down)
    cos, sin = _rope_tables(TL)
    P = cache_k.shape[2]
    latent_ctx = (cache_k.reshape(BL, DEPTH, P, ATT_KV), cache_v.reshape(BL, DEPTH, P, ATT_KV),
                  jnp.swapaxes(state_gla, -1, -2), cos, sin)

    def layer_step(l, carry):
        yp, ys, new_k, new_v, new_st = carry
        layer = jnp.full((1,), l, jnp.int32)
        oa, ob, oc, k_l, v_l, s_l = _mix_call(False, layer, yp, mod_all, pw, None)
        yp = _post_call(layer, yp.reshape(B * T, D_MODEL), mod_all, oa.reshape(B * T, -1), ob.reshape(B * T, -1),
                        oc.reshape(B * T, -1), pw, T // POST_TILE, True).reshape(B, T, D_MODEL)
        new_k = lax.dynamic_update_slice_in_dim(new_k, k_l[:, None], l, axis=1)
        new_v = lax.dynamic_update_slice_in_dim(new_v, v_l[:, None], l, axis=1)
        new_st = lax.dynamic_update_slice_in_dim(new_st, s_l[:, None], l, axis=1)
        oa, ob, oc = _mix_call(True, layer, ys, mod_all, pw, latent_ctx)
        ys = _post_call(layer, ys.reshape(BL * TL, D_MODEL), mod_all, oa.reshape(BL * TL, -1),
                        ob.reshape(BL * TL, -1), oc.reshape(BL * TL, -1), pw, TL // POST_TILE,
                        False).reshape(BL, TL, D_MODEL)
        return yp, ys, new_k, new_v, new_st

    init = (x_prompt, x_sample,
            jnp.zeros((B, DEPTH, T, ATT_KV), _F32), jnp.zeros((B, DEPTH, T, ATT_KV), _F32),
            jnp.zeros((B, DEPTH, 2, GLA_HEADS, GLA_DK, GLA_DV), _F32))
    yp, ys, new_k, new_v, new_st = lax.fori_loop(0, DEPTH, layer_step, init)
    return (yp, ys, new_k.reshape(B, DEPTH, T, ATT_KV_HEADS, HEAD_DIM),
            new_v.reshape(B, DEPTH, T, ATT_KV_HEADS, HEAD_DIM), new_st)
```

```python
import functools

import jax
import jax.numpy as jnp
import numpy as np
from jax import lax
from jax.experimental import pallas as pl
from jax.experimental.pallas import tpu as pltpu

D_MODEL = 1024
DEPTH = 4
GRID_W = 64
ATT_HEADS = 8
ATT_KV_HEADS = 2
ATT_GROUP = ATT_HEADS // ATT_KV_HEADS
HEAD_DIM = 64
WINDOW = 128
ATT_BLOCK = 128
ROPE_BASE = 10000.0
GLA_HEADS = 4
GLA_DK = 64
GLA_DV = 128
GLA_RANK = 16
GLA_TAU = 16.0
GLA_CHUNK = 64
POOL_GROUPS = 4
POOL_GROUP_DIM = 128
POOL_WINDOWS = (2, 4, 8, 16)
D_FF = 2816
ATT_Q = ATT_HEADS * HEAD_DIM
ATT_KV = ATT_KV_HEADS * HEAD_DIM
GLA_QK = GLA_HEADS * GLA_DK
GLA_VW = GLA_HEADS * GLA_DV
POOL_W = POOL_GROUPS * POOL_GROUP_DIM
EPS = 1e-6
NEG = -1e30

C_QA = 0
C_KA = C_QA + ATT_Q
C_VA = C_KA + ATT_KV
C_QB = C_VA + ATT_KV
C_KB = C_QB + GLA_QK
C_VB = C_KB + GLA_QK
C_RB = C_VB + GLA_VW
C_UC = C_RB + GLA_VW
C_GL = C_UC + POOL_W
GL_PAD = 128
MIX_W = C_GL + GL_PAD
GATE_W = 3 * D_MODEL

POST_TILE = 256
PROJ_TILE = 256
POOL_TILE = 256
POOL_HALO = 128
MOD_ROWS = 8
MOD_TILE = 1024
GLA_BLOCK = 256
GLA_MAX_EXPONENT = 80.0
VMEM_LIMIT = 56 * 1024 * 1024

_F32 = jnp.float32
_BF16 = jnp.bfloat16


def _dot(a, b):
    return jnp.dot(a, b, preferred_element_type=_F32)


def _dot_nt(a, b):
    return lax.dot_general(a, b, (((1,), (1,)), ((), ())), preferred_element_type=_F32)


def _dot_tn(a, b):
    return lax.dot_general(a, b, (((0,), (0,)), ((), ())), preferred_element_type=_F32)


def _rms_scale(x):
    return lax.rsqrt(jnp.mean(x * x, axis=-1, keepdims=True) + EPS)


def _group_rms_scale(x, group_ones, width):
    sq = x * x
    hi = sq.astype(_BF16)
    lo = (sq - hi.astype(_F32)).astype(_BF16)
    return lax.rsqrt((_dot(hi, group_ones) + _dot(lo, group_ones)) * (1.0 / width) + EPS)


def _log_sigmoid(x):
    return jnp.minimum(x, 0.0) - jnp.log1p(jnp.exp(-jnp.abs(x)))


def _silu(x):
    return x * jax.nn.sigmoid(x)


def _rope(x, cos, sin_signed):
    n = x.shape[-1]
    lane = lax.broadcasted_iota(jnp.int32, x.shape, 1)
    up = pltpu.roll(x, n - HEAD_DIM // 4, axis=1)
    down = pltpu.roll(x, HEAD_DIM // 4, axis=1)
    partner = jnp.where((lane & (HEAD_DIM // 2 - 1)) < HEAD_DIM // 4, up, down)
    return x * cos + partner * sin_signed


def _mod_kernel(cv_ref, w_ref, b_ref, out_ref):
    s = _silu(cv_ref[...]).astype(_BF16)
    out_ref[0] = _dot(s, w_ref[0].astype(_BF16)) + b_ref[0]


def _modulation(cv, w_mod, b_mod):
    n_col = (6 * D_MODEL) // MOD_TILE
    return pl.pallas_call(
        _mod_kernel,
        grid=(DEPTH, n_col),
        in_specs=[
            pl.BlockSpec((MOD_ROWS, D_MODEL), lambda l, j: (0, 0)),
            pl.BlockSpec((1, D_MODEL, MOD_TILE), lambda l, j: (l, 0, j)),
            pl.BlockSpec((1, 1, MOD_TILE), lambda l, j: (l, 0, j)),
        ],
        out_specs=pl.BlockSpec((1, MOD_ROWS, MOD_TILE), lambda l, j: (l, 0, j)),
        out_shape=jax.ShapeDtypeStruct((DEPTH, MOD_ROWS, 6 * D_MODEL), _F32),
        name="modulation",
    )(cv, w_mod, b_mod.reshape(DEPTH, 1, 6 * D_MODEL))


def _split_heads(x):
    low = lax.broadcasted_iota(jnp.int32, x.shape, 1) < HEAD_DIM
    swapped = pltpu.roll(x, HEAD_DIM, axis=1)
    zero = jnp.zeros_like(x)
    return ((jnp.where(low, x, zero), jnp.where(low, zero, swapped)),
            (jnp.where(low, swapped, zero), jnp.where(low, zero, x)))


def _store_split_kv(k, v, ks_ref, vs_ref, rows):
    ones = jnp.ones_like(v)
    for kv, (k_sides, v_sides, one_sides) in enumerate(zip(_split_heads(k), _split_heads(v), _split_heads(ones))):
        for side in range(2):
            ks_ref[kv, side, rows, :] = k_sides[side].astype(_BF16)
            vs_ref[kv, side, rows, :] = jnp.concatenate([v_sides[side], one_sides[side]], axis=-1).astype(_BF16)


def _pair_softmax_av(qp, keys, values, masks, sink_even, sink_odd):
    m = qp.shape[0]
    scores = []
    for (k_left, k_right), mask in zip(keys, masks):
        s_even, s_odd = _dot_nt(qp, k_left), _dot_nt(qp, k_right)
        if mask is not None:
            s_even, s_odd = jnp.where(mask, s_even, NEG), jnp.where(mask, s_odd, NEG)
        scores.append((s_even, s_odd))
    m_even = jnp.full((m, 1), sink_even, _F32)
    m_odd = jnp.full((m, 1), sink_odd, _F32)
    for s_even, s_odd in scores:
        m_even = jnp.maximum(m_even, jnp.max(s_even, axis=-1, keepdims=True))
        m_odd = jnp.maximum(m_odd, jnp.max(s_odd, axis=-1, keepdims=True))
    res = None
    for (s_even, s_odd), (w_left, w_right) in zip(scores, values):
        r = (_dot(jnp.exp(s_even - m_even).astype(_BF16), w_left)
             + _dot(jnp.exp(s_odd - m_odd).astype(_BF16), w_right))
        res = r if res is None else res + r
    pair = 2 * HEAD_DIM
    low = lax.broadcasted_iota(jnp.int32, (m, pair), 1) < HEAD_DIM
    den = res[:, pair:] + jnp.where(low, jnp.exp(sink_even - m_even), jnp.exp(sink_odd - m_odd))
    return res[:, :pair] / den


def _attention_ctx(T, layer, qr_ref, ks_ref, vs_ref, sink_ref, oa_ref):
    pair = 2 * HEAD_DIM
    for kv in range(ATT_KV_HEADS):
        keys = [(ks_ref[kv, 0], ks_ref[kv, 1])]
        values = [(vs_ref[kv, 0], vs_ref[kv, 1])]
        for j in range(ATT_GROUP // 2):
            head = kv * ATT_GROUP + 2 * j
            cols = slice(head * HEAD_DIM, head * HEAD_DIM + pair)
            o = _pair_softmax_av(qr_ref[:, cols], keys, values, [None], sink_ref[layer, head],
                                 sink_ref[layer, head + 1])
            oa_ref[0, :, cols] = o.astype(_BF16)


def _attention_latent(T, layer, qr_ref, ks_ref, vs_ref, kc_ref, vc_ref, sink_ref, oa_ref):
    pair = 2 * HEAD_DIM
    span = 3 * ATT_BLOCK
    kc, vc = kc_ref[0, 0], vc_ref[0, 0]
    ones = jnp.ones_like(vc)
    ctx_keys = [tuple(side.astype(_BF16) for side in sides) for sides in _split_heads(kc)]
    ctx_values = [tuple(jnp.concatenate([v_side, one_side], axis=-1).astype(_BF16)
                        for v_side, one_side in zip(v_sides, one_sides))
                  for v_sides, one_sides in zip(_split_heads(vc), _split_heads(ones))]

    def block(i, carry):
        q_rows = pl.ds(pl.multiple_of(i * ATT_BLOCK, ATT_BLOCK), ATT_BLOCK)
        k_rows = pl.ds(pl.multiple_of(i * ATT_BLOCK, ATT_BLOCK), span)
        q_pos = i * ATT_BLOCK + lax.broadcasted_iota(jnp.int32, (ATT_BLOCK, span), 0)
        k_pos = (i - 1) * ATT_BLOCK + lax.broadcasted_iota(jnp.int32, (ATT_BLOCK, span), 1)
        valid = (jnp.abs(k_pos - q_pos) <= WINDOW) & (k_pos >= 0) & (k_pos < T)
        for kv in range(ATT_KV_HEADS):
            keys = [(ks_ref[kv, 0, k_rows, :], ks_ref[kv, 1, k_rows, :]), ctx_keys[kv]]
            values = [(vs_ref[kv, 0, k_rows, :], vs_ref[kv, 1, k_rows, :]), ctx_values[kv]]
            for j in range(ATT_GROUP // 2):
                head = kv * ATT_GROUP + 2 * j
                cols = slice(head * HEAD_DIM, head * HEAD_DIM + pair)
                o = _pair_softmax_av(qr_ref[q_rows, cols], keys, values, [valid, None], sink_ref[layer, head],
                                     sink_ref[layer, head + 1])
                oa_ref[0, q_rows, cols] = o.astype(_BF16)
        return carry

    lax.fori_loop(0, T // ATT_BLOCK, block, 0)


def _gla_chunk(z_ref, la_ref, o_ref, st_ref, a_ref, direction, start):
    C = GLA_CHUNK
    rows = pl.ds(pl.multiple_of(start, C), C)
    q = z_ref[rows, C_QB:C_QB + GLA_QK] * (GLA_DK ** -0.5)
    k = z_ref[rows, C_KB:C_KB + GLA_QK]
    v = z_ref[rows, C_VB:C_VB + GLA_VW].astype(_BF16)
    la = la_ref[rows, direction * GLA_QK:(direction + 1) * GLA_QK]
    la_hi = la.astype(_BF16)
    la_lo = (la - la_hi.astype(_F32)).astype(_BF16)
    ti = lax.broadcasted_iota(jnp.int32, (C, C), 0)
    si = lax.broadcasted_iota(jnp.int32, (C, C), 1)
    causal = (si <= ti) if direction == 0 else (si >= ti)
    tri = jnp.where(causal, 1.0, 0.0).astype(_BF16)
    b = _dot(tri, la_hi) + _dot(tri, la_lo)
    end = C - 1 if direction == 0 else 0
    b_end = b[end:end + 1]
    q_in = (q * jnp.exp(b)).astype(_BF16)
    k_out = (k * jnp.exp(b_end - b)).astype(_BF16)
    e_end = jnp.exp(b_end)
    ones = jnp.ones((8, GLA_DK), _BF16)

    def row_group(g, carry):
        base = pl.multiple_of(g * 8, 8)
        b8 = a_ref[GLA_HEADS, pl.ds(base, 8), :]
        q8 = a_ref[GLA_HEADS + 1, pl.ds(base, 8), :]
        s_idx = lax.broadcasted_iota(jnp.int32, (C, 1), 0)
        rows_h = [[] for _ in range(GLA_HEADS)]
        for j in range(8):
            ok = (s_idx <= base + j) if direction == 0 else (s_idx >= base + j)
            decay = jnp.exp(jnp.where(ok, b8[j:j + 1] - b, NEG))
            p = (q8[j:j + 1] * k * decay).astype(_BF16)
            for h in range(GLA_HEADS):
                rows_h[h].append(_dot_nt(ones, p[:, h * GLA_DK:(h + 1) * GLA_DK])[0:1])
        for h in range(GLA_HEADS):
            a_ref[h, pl.ds(base, 8), 0:C] = jnp.concatenate(rows_h[h], axis=0)
        return carry

    a_ref[GLA_HEADS] = b
    a_ref[GLA_HEADS + 1] = q
    lax.fori_loop(0, C // 8, row_group, 0)
    outs = []
    for h in range(GLA_HEADS):
        kc = slice(h * GLA_DK, (h + 1) * GLA_DK)
        vc = slice(h * GLA_DV, (h + 1) * GLA_DV)
        s_t = st_ref[direction, h]
        o_h = _dot_nt(q_in[:, kc], s_t.astype(_BF16)) + _dot(a_ref[h, :, 0:C].astype(_BF16), v[:, vc])
        st_ref[direction, h] = s_t * e_end[:, kc] + _dot_tn(v[:, vc], k_out[:, kc])
        outs.append(o_h)
    o_ref[rows, :] += jnp.concatenate(outs, axis=-1)


def _gla_block(z_ref, la_ref, o_ref, st_ref, direction, start, use_state):
    NB = GLA_BLOCK
    fwd = direction == 0
    rows = slice(start, start + NB) if isinstance(start, int) else pl.ds(pl.multiple_of(start, NB), NB)
    q = z_ref[rows, C_QB:C_QB + GLA_QK] * (GLA_DK ** -0.5)
    k = z_ref[rows, C_KB:C_KB + GLA_QK]
    v = z_ref[rows, C_VB:C_VB + GLA_VW].astype(_BF16)
    b = la_ref[rows, direction * GLA_QK:(direction + 1) * GLA_QK]
    end, mid = (NB - 1, NB // 2 - 1) if fwd else (0, NB // 2)
    b_end = b[end:end + 1]
    c = b - b[mid:mid + 1]
    q_c = (q * jnp.exp(c)).astype(_BF16)
    k_c = (k * jnp.exp(-c)).astype(_BF16)
    k_fin = (k * jnp.exp(b_end - b)).astype(_BF16)
    if use_state:
        q_in = (q * jnp.exp(b)).astype(_BF16)
        e_all = jnp.exp(b_end)
    ti = lax.broadcasted_iota(jnp.int32, (NB, NB), 0)
    si = lax.broadcasted_iota(jnp.int32, (NB, NB), 1)
    causal = (si <= ti) if fwd else (si >= ti)
    outs = []
    for h in range(GLA_HEADS):
        kc = slice(h * GLA_DK, (h + 1) * GLA_DK)
        vc = slice(h * GLA_DV, (h + 1) * GLA_DV)
        a = jnp.where(causal, _dot_nt(q_c[:, kc], k_c[:, kc]), 0.0)
        o_h = _dot(a.astype(_BF16), v[:, vc])
        s_new = _dot_tn(v[:, vc], k_fin[:, kc])
        if use_state:
            s_t = st_ref[direction, h]
            o_h = o_h + _dot_nt(q_in[:, kc], s_t.astype(_BF16))
            s_new = s_new + s_t * e_all[:, kc]
        st_ref[direction, h] = s_new
        outs.append(o_h)
    o_ref[rows, :] += jnp.concatenate(outs, axis=-1)


def _log_decay(z_ref, rows, wg2_ref, bg2_ref):
    pre = lax.dot_general(z_ref[rows, C_GL:C_GL + GL_PAD], wg2_ref[...], (((1,), (0,)), ((), ())),
                          precision=lax.Precision.HIGHEST, preferred_element_type=_F32) + bg2_ref[...]
    return _log_sigmoid(pre) * (1.0 / GLA_TAU)


def _gla(T, z_ref, la_ref, o_ref, st_ref, a_ref, wg2_ref, bg2_ref, use_state):
    NB = GLA_BLOCK
    n_chunks = T // GLA_CHUNK
    n_blocks = T // NB
    ti = lax.broadcasted_iota(jnp.int32, (NB, NB), 0)
    si = lax.broadcasted_iota(jnp.int32, (NB, NB), 1)
    tri = [jnp.where(si <= ti, 1.0, 0.0).astype(_BF16), jnp.where(si >= ti, 1.0, 0.0).astype(_BF16)]
    worst = None
    for r0 in range(0, T, NB):
        rows = slice(r0, r0 + NB)
        la = _log_decay(z_ref, rows, wg2_ref, bg2_ref)
        la_hi = la.astype(_BF16)
        la_lo = (la - la_hi.astype(_F32)).astype(_BF16)
        for d in range(2):
            cols = slice(d * GLA_QK, (d + 1) * GLA_QK)
            b = _dot(tri[d], la_hi[:, cols]) + _dot(tri[d], la_lo[:, cols])
            la_ref[rows, cols] = b
            first, mid, last = (0, NB // 2 - 1, NB - 1) if d == 0 else (NB - 1, NB // 2, 0)
            span = jnp.max(jnp.maximum(b[first:first + 1] - b[mid:mid + 1], b[mid:mid + 1] - b[last:last + 1]))
            worst = span if worst is None else jnp.maximum(worst, span)
    fast_ok = worst <= GLA_MAX_EXPONENT
    o_ref[...] = jnp.zeros(o_ref.shape, _F32)

    @pl.when(fast_ok)
    def _():
        if n_blocks == 1:
            for direction in range(2):
                _gla_block(z_ref, la_ref, o_ref, st_ref, direction, 0, use_state)
        else:
            def body(i, carry):
                _gla_block(z_ref, la_ref, o_ref, st_ref, 0, i * NB, True)
                _gla_block(z_ref, la_ref, o_ref, st_ref, 1, (n_blocks - 1 - i) * NB, True)
                return carry
            lax.fori_loop(0, n_blocks, body, 0)

    @pl.when(jnp.logical_not(fast_ok))
    def _():
        for r0 in range(0, T, NB):
            la_ref[r0:r0 + NB, :] = _log_decay(z_ref, slice(r0, r0 + NB), wg2_ref, bg2_ref)

        def body(i, carry):
            _gla_chunk(z_ref, la_ref, o_ref, st_ref, a_ref, 0, i * GLA_CHUNK)
            _gla_chunk(z_ref, la_ref, o_ref, st_ref, a_ref, 1, (n_chunks - 1 - i) * GLA_CHUNK)
            return carry
        lax.fori_loop(0, n_chunks, body, 0)


def _gla_finish(T, z_ref, o_ref, ggla_ref, vones_ref, ob_ref):
    for r0 in range(0, T, PROJ_TILE):
        rows = slice(r0, r0 + PROJ_TILE)
        o = o_ref[rows, :]
        y = o * _group_rms_scale(o, vones_ref[...], GLA_DV) * ggla_ref[...]
        ob_ref[0, rows, :] = (y * _silu(z_ref[rows, C_RB:C_RB + GLA_VW])).astype(_BF16)


def _pool(T, z_ref, upad_ref, wpool_ref, pscale_ref, oc_ref):
    u = z_ref[:, C_UC:C_UC + POOL_W]
    u_hi = u.astype(_BF16)
    u_lo = (u - u_hi.astype(_F32)).astype(_BF16)
    zeros = jnp.zeros((POOL_HALO, POOL_W), _BF16)
    for part, val in ((0, u_hi), (1, u_lo)):
        upad_ref[part, 0:POOL_HALO, :] = zeros
        upad_ref[part, POOL_HALO + T:POOL_HALO + T + POOL_HALO, :] = zeros
        upad_ref[part, POOL_HALO:POOL_HALO + T, :] = val
    span = POOL_TILE + 2 * POOL_HALO
    r = lax.broadcasted_iota(jnp.int32, (POOL_TILE, span), 0)
    c = lax.broadcasted_iota(jnp.int32, (POOL_TILE, span), 1)
    off = c - POOL_HALO - r
    for jb in range(T // POOL_TILE):
        t = jb * POOL_TILE + lax.broadcasted_iota(jnp.int32, (POOL_TILE, 1), 0)
        parts = []
        for g, w in enumerate(POOL_WINDOWS):
            cols = slice(g * POOL_GROUP_DIM, (g + 1) * POOL_GROUP_DIM)
            band = jnp.where((off >= -(w // 2)) & (off < w - w // 2), 1.0, 0.0).astype(_BF16)
            win = slice(jb * POOL_TILE, jb * POOL_TILE + span)
            total = _dot(band, upad_ref[0, win, cols]) + _dot(band, upad_ref[1, win, cols])
            cnt = (jnp.minimum(t - w // 2 + w, T) - jnp.maximum(t - w // 2, 0)).astype(_F32)
            pooled = total / cnt - z_ref[jb * POOL_TILE:(jb + 1) * POOL_TILE, C_UC + g * POOL_GROUP_DIM:
                                         C_UC + (g + 1) * POOL_GROUP_DIM]
            parts.append(_dot(pooled.astype(_BF16), wpool_ref[g]))
        y = jnp.concatenate(parts, axis=-1) * pscale_ref[...]
        oc_ref[0, jb * POOL_TILE:(jb + 1) * POOL_TILE, :] = y.astype(_BF16)


def _mix_kernel(latent, T, layer_ref, *refs):
    (x_ref, mod_ref, gn1_ref, wmix_ref, gqn_ref, gkn_ref, sink_ref, wg2_ref, bg2_ref, ggla_ref,
     wpool_ref, pscale_ref, hones_ref, vones_ref) = refs[:14]
    refs = refs[14:]
    if latent:
        kc_ref, vc_ref, st0_ref, cos_ref, sin_ref, oa_ref, ob_ref, oc_ref = refs[:8]
        refs = refs[8:]
    else:
        oa_ref, ob_ref, oc_ref, kout_ref, vout_ref, stout_ref = refs[:6]
        refs = refs[6:]
    z_ref, qr_ref, ks_ref, vs_ref, la_ref, o_ref, st_ref, a_ref, upad_ref = refs
    layer = layer_ref[0]
    pad = ATT_BLOCK if latent else 0

    if latent:
        for ref in (ks_ref, vs_ref):
            zeros = jnp.zeros(ref.shape[:2] + (pad, ref.shape[3]), _BF16)
            ref[:, :, 0:pad, :] = zeros
            ref[:, :, pad + T:pad + T + pad, :] = zeros
    shift = mod_ref[0, :, 0:D_MODEL]
    scale = mod_ref[0, :, D_MODEL:2 * D_MODEL]
    for r0 in range(0, T, PROJ_TILE):
        rows = slice(r0, r0 + PROJ_TILE)
        x = x_ref[0, rows, :]
        hn = (x * _rms_scale(x) * gn1_ref[...]) * (1.0 + scale) + shift
        z_ref[rows, :] = _dot(hn.astype(_BF16), wmix_ref[...])
        q = z_ref[rows, C_QA:C_QA + ATT_Q]
        k = z_ref[rows, C_KA:C_KA + ATT_KV]
        q = q * _group_rms_scale(q, hones_ref[...], HEAD_DIM) * gqn_ref[...]
        k = k * _group_rms_scale(k, hones_ref[0:ATT_KV, 0:ATT_KV], HEAD_DIM) * gkn_ref[...]
        v = z_ref[rows, C_VA:C_VA + ATT_KV]
        if latent:
            cos = jnp.concatenate([cos_ref[rows, :]] * (ATT_Q // ATT_KV), axis=-1)
            sin = jnp.concatenate([sin_ref[rows, :]] * (ATT_Q // ATT_KV), axis=-1)
            q = _rope(q, cos, sin)
            k = _rope(k, cos_ref[rows, :], sin_ref[rows, :])
        else:
            kout_ref[0, rows, :] = k
            vout_ref[0, rows, :] = v
        qr_ref[rows, :] = (q * (HEAD_DIM ** -0.5)).astype(_BF16)
        _store_split_kv(k, v, ks_ref, vs_ref, slice(pad + r0, pad + r0 + PROJ_TILE))
    if latent:
        _attention_latent(T, layer, qr_ref, ks_ref, vs_ref, kc_ref, vc_ref, sink_ref, oa_ref)
    else:
        _attention_ctx(T, layer, qr_ref, ks_ref, vs_ref, sink_ref, oa_ref)

    if latent:
        st_ref[...] = st0_ref[0]
    else:
        st_ref[...] = jnp.zeros(st_ref.shape, _F32)
    _gla(T, z_ref, la_ref, o_ref, st_ref, a_ref, wg2_ref, bg2_ref, latent)
    _gla_finish(T, z_ref, o_ref, ggla_ref, vones_ref, ob_ref)
    if not latent:
        for d in range(2):
            for h in range(GLA_HEADS):
                stout_ref[0, d, h] = st_ref[d, h].T

    _pool(T, z_ref, upad_ref, wpool_ref, pscale_ref, oc_ref)


def _layer_spec(shape):
    zeros = (0,) * len(shape)
    return pl.BlockSpec((None,) + tuple(shape), lambda i, layer: (layer[0],) + zeros,
                        pipeline_mode=pl.Buffered(1))


def _mix_call(latent, layer, x, mod_all, pw, extra):
    B, T, _ = x.shape
    per_seq = lambda blk: pl.BlockSpec(blk, lambda b, layer: (b,) + (0,) * (len(blk) - 1))
    const_spec = lambda blk: pl.BlockSpec(blk, lambda b, layer: (0,) * len(blk), pipeline_mode=pl.Buffered(1))
    if latent:
        mod_spec = pl.BlockSpec((None, 1, 1, 6 * D_MODEL), lambda b, layer: (layer[0], b + 1, 0, 0))
    else:
        mod_spec = pl.BlockSpec((None, 1, 1, 6 * D_MODEL), lambda b, layer: (layer[0], 0, 0, 0))
    x_spec = pl.BlockSpec((1, T, D_MODEL), lambda b, layer: (b, 0, 0),
                          pipeline_mode=pl.Buffered(1) if latent else None)
    in_specs = [
        x_spec, mod_spec,
        _layer_spec((1, D_MODEL)), _layer_spec((D_MODEL, MIX_W)),
        _layer_spec((1, ATT_Q)), _layer_spec((1, ATT_KV)),
        pl.BlockSpec(memory_space=pltpu.SMEM),
        _layer_spec((GL_PAD, 2 * GLA_QK)), _layer_spec((1, 2 * GLA_QK)), _layer_spec((1, GLA_VW)),
        _layer_spec((POOL_GROUPS, POOL_GROUP_DIM, POOL_GROUP_DIM)), _layer_spec((1, POOL_W)),
        const_spec((ATT_Q, ATT_Q)), const_spec((GLA_VW, GLA_VW)),
    ]
    args = [x, mod_all, pw["g_norm1"], pw["w_mix"], pw["g_qn"], pw["g_kn"], pw["att_sink"], pw["w_gate2"],
            pw["b_gate2"], pw["g_gla_out"], pw["w_pool"], pw["pool_scale"], pw["head_ones"], pw["gla_ones"]]
    act = lambda width: jax.ShapeDtypeStruct((B, T, width), _BF16)
    out_shape = [act(ATT_Q), act(GLA_VW), act(POOL_W)]
    out_specs = [per_seq((1, T, ATT_Q)), per_seq((1, T, GLA_VW)), per_seq((1, T, POOL_W))]
    if latent:
        cache_k, cache_v, st0, cos, sin = extra
        P = cache_k.shape[2]
        cache_spec = pl.BlockSpec((1, 1, P, ATT_KV), lambda b, layer: (b, layer[0], 0, 0))
        table_spec = const_spec((T, ATT_KV))
        in_specs += [cache_spec, cache_spec,
                     pl.BlockSpec((1, None, 2, GLA_HEADS, GLA_DV, GLA_DK),
                                  lambda b, layer: (b, layer[0], 0, 0, 0, 0)),
                     table_spec, table_spec]
        args += [cache_k, cache_v, st0, cos, sin]
    else:
        out_shape += [jax.ShapeDtypeStruct((B, T, ATT_KV), _F32), jax.ShapeDtypeStruct((B, T, ATT_KV), _F32),
                      jax.ShapeDtypeStruct((B, 2, GLA_HEADS, GLA_DK, GLA_DV), _F32)]
        out_specs += [per_seq((1, T, ATT_KV)), per_seq((1, T, ATT_KV)),
                      per_seq((1, 2, GLA_HEADS, GLA_DK, GLA_DV))]
    kv_rows = T + 2 * ATT_BLOCK if latent else T
    scratch = [
        pltpu.VMEM((T, MIX_W), _F32),
        pltpu.VMEM((T, ATT_Q), _BF16),
        pltpu.VMEM((ATT_KV_HEADS, 2, kv_rows, 2 * HEAD_DIM), _BF16),
        pltpu.VMEM((ATT_KV_HEADS, 2, kv_rows, 4 * HEAD_DIM), _BF16),
        pltpu.VMEM((T, 2 * GLA_QK), _F32),
        pltpu.VMEM((T, GLA_VW), _F32),
        pltpu.VMEM((2, GLA_HEADS, GLA_DV, GLA_DK), _F32),
        pltpu.VMEM((GLA_HEADS + 2, GLA_CHUNK, GLA_QK), _F32),
        pltpu.VMEM((2, T + 2 * POOL_HALO, POOL_W), _BF16),
    ]
    return pl.pallas_call(
        functools.partial(_mix_kernel, latent, T),
        grid_spec=pltpu.PrefetchScalarGridSpec(
            num_scalar_prefetch=1, grid=(B,), in_specs=in_specs, out_specs=out_specs, scratch_shapes=scratch),
        out_shape=out_shape,
        compiler_params=pltpu.CompilerParams(dimension_semantics=("arbitrary",), vmem_limit_bytes=VMEM_LIMIT),
        name="mix_latent" if latent else "mix_ctx",
    )(layer, *args)


def _post_kernel(layer_ref, x_ref, mod_ref, oa_ref, ob_ref, oc_ref, gn1_ref, gn2_ref, wgate_ref, wa_ref, wb_ref,
                 wc_ref, wout_ref, wfg_ref, wfu_ref, wfd_ref, out_ref):
    x = x_ref[...]
    mod = lambda i: mod_ref[0, :, i * D_MODEL:(i + 1) * D_MODEL]
    hn = (x * _rms_scale(x) * gn1_ref[...]) * (1.0 + mod(1)) + mod(0)
    gates = jax.nn.sigmoid(_dot(hn.astype(_BF16), wgate_ref[...]))
    mixed = (gates[:, 0:D_MODEL] * _dot(oa_ref[...], wa_ref[...])
             + gates[:, D_MODEL:2 * D_MODEL] * _dot(ob_ref[...], wb_ref[...])
             + gates[:, 2 * D_MODEL:3 * D_MODEL] * _dot(oc_ref[...], wc_ref[...]))
    x = x + mod(2) * _dot(mixed.astype(_BF16), wout_ref[...])
    hn = ((x * _rms_scale(x) * gn2_ref[...]) * (1.0 + mod(4)) + mod(3)).astype(_BF16)
    h = _silu(_dot(hn, wfg_ref[...])) * _dot(hn, wfu_ref[...])
    out_ref[...] = x + mod(5) * _dot(h.astype(_BF16), wfd_ref[...])


def _post_call(layer, x2d, mod_all, oa, ob, oc, pw, tiles_per_seq, shared_mod):
    n = x2d.shape[0]
    row = lambda w: pl.BlockSpec((POST_TILE, w), lambda i, layer: (i, 0))
    if shared_mod:
        mod_spec = pl.BlockSpec((None, 1, 1, 6 * D_MODEL), lambda i, layer: (layer[0], 0, 0, 0))
    else:
        mod_spec = pl.BlockSpec((None, 1, 1, 6 * D_MODEL),
                                lambda i, layer: (layer[0], 1 + i // tiles_per_seq, 0, 0))
    in_specs = [row(D_MODEL), mod_spec, row(ATT_Q), row(GLA_VW), row(POOL_W),
                _layer_spec((1, D_MODEL)), _layer_spec((1, D_MODEL)), _layer_spec((D_MODEL, GATE_W)),
                _layer_spec((ATT_Q, D_MODEL)), _layer_spec((GLA_VW, D_MODEL)), _layer_spec((POOL_W, D_MODEL)),
                _layer_spec((D_MODEL, D_MODEL)), _layer_spec((D_MODEL, D_FF)), _layer_spec((D_MODEL, D_FF)),
                _layer_spec((D_FF, D_MODEL))]
    return pl.pallas_call(
        _post_kernel,
        grid_spec=pltpu.PrefetchScalarGridSpec(
            num_scalar_prefetch=1, grid=(n // POST_TILE,), in_specs=in_specs, out_specs=row(D_MODEL)),
        out_shape=jax.ShapeDtypeStruct(x2d.shape, _F32),
        compiler_params=pltpu.CompilerParams(dimension_semantics=("arbitrary",), vmem_limit_bytes=VMEM_LIMIT),
        name="post",
    )(layer, x2d, mod_all, oa, ob, oc, pw["g_norm1"], pw["g_norm2"], pw["w_gates"], pw["w_br_a"], pw["w_br_b"],
      pw["w_br_c"], pw["w_out"], pw["w_ff_gate"], pw["w_ff_up"], pw["w_ff_down"])


def _rope_tables(T):
    quarter = HEAD_DIM // 4
    inv_freq = ROPE_BASE ** (-np.arange(quarter, dtype=np.float32) / quarter)
    pos = np.arange(T)
    ang_row = (pos // GRID_W).astype(np.float32)[:, None] * inv_freq[None, :]
    ang_col = (pos % GRID_W).astype(np.float32)[:, None] * inv_freq[None, :]
    cos = np.concatenate([np.cos(ang_row)] * 2 + [np.cos(ang_col)] * 2, axis=-1)
    sin = np.concatenate([-np.sin(ang_row), np.sin(ang_row), -np.sin(ang_col), np.sin(ang_col)], axis=-1)
    return (jnp.asarray(np.tile(cos, (1, ATT_KV_HEADS)), _F32), jnp.asarray(np.tile(sin, (1, ATT_KV_HEADS)), _F32))


def _prepare_weights(w_in, g_qn, g_kn, att_sink, w_gate2, b_gate2, g_gla_out, w_pool, pool_scale, w_br_a,
                     w_br_b, w_br_c, w_out, g_norm1, g_norm2, w_ff_gate, w_ff_up, w_ff_down):
    o_gl = ATT_Q + 2 * ATT_KV + 2 * GLA_QK + 2 * GLA_VW
    o_uc = o_gl + 2 * GLA_RANK
    o_gate = o_uc + POOL_W
    w_mix = jnp.concatenate(
        [w_in[:, :, :o_gl], w_in[:, :, o_uc:o_gate], w_in[:, :, o_gl:o_uc],
         jnp.zeros((DEPTH, D_MODEL, GL_PAD - 2 * GLA_RANK), w_in.dtype)], axis=2).astype(_BF16)
    wg2 = jnp.zeros((DEPTH, GL_PAD, 2 * GLA_QK), _F32)
    wg2 = wg2.at[:, 0:GLA_RANK, 0:GLA_QK].set(w_gate2[:, 0])
    wg2 = wg2.at[:, GLA_RANK:2 * GLA_RANK, GLA_QK:].set(w_gate2[:, 1])
    vec = lambda a: a.reshape(DEPTH, 1, -1)
    group_ones = lambda n, width: jnp.asarray(
        (np.arange(n)[:, None] // width) == (np.arange(n)[None, :] // width), _BF16)
    return {
        "head_ones": group_ones(ATT_Q, HEAD_DIM),
        "gla_ones": group_ones(GLA_VW, GLA_DV),
        "w_mix": w_mix,
        "w_gates": w_in[:, :, o_gate:].astype(_BF16),
        "g_qn": vec(jnp.tile(g_qn, (1, ATT_HEADS))),
        "g_kn": vec(jnp.tile(g_kn, (1, ATT_KV_HEADS))),
        "att_sink": att_sink,
        "w_gate2": wg2,
        "b_gate2": vec(b_gate2),
        "g_gla_out": vec(jnp.tile(g_gla_out, (1, GLA_HEADS))),
        "w_pool": w_pool.astype(_BF16),
        "pool_scale": vec(pool_scale),
        "w_br_a": w_br_a.astype(_BF16),
        "w_br_b": w_br_b.astype(_BF16),
        "w_br_c": w_br_c.astype(_BF16),
        "w_out": w_out.astype(_BF16),
        "g_norm1": vec(g_norm1),
        "g_norm2": vec(g_norm2),
        "w_ff_gate": w_ff_gate.astype(_BF16),
        "w_ff_up": w_ff_up.astype(_BF16),
        "w_ff_down": w_ff_down.astype(_BF16),
    }


def kernel(x_prompt, x_sample, c, cache_k, cache_v, state_gla, c_ctx, w_in, g_qn, g_kn, att_sink, w_gate2,
           b_gate2, g_gla_out, w_pool, pool_scale, w_br_a, w_br_b, w_br_c, w_out, g_norm1, g_norm2, w_mod,
           b_mod, w_ff_gate, w_ff_up, w_ff_down):
    B, T, _ = x_prompt.shape
    BL, TL, _ = x_sample.shape
    assert T % POST_TILE == 0 and TL % POST_TILE == 0 and BL + 1 <= MOD_ROWS
    cv = jnp.concatenate([c_ctx[None, :], c, jnp.zeros((MOD_ROWS - 1 - BL, D_MODEL), _F32)], axis=0)
    mod_all = _modulation(cv, w_mod, b_mod).reshape(DEPTH, MOD_ROWS, 1, 6 * D_MODEL)
    pw = _prepare_weights(w_in, g_qn, g_kn, att_sink, w_gate2, b_gate2, g_gla_out, w_pool, pool_scale, w_br_a,
                          w_br_b, w_br_c, w_out, g_norm1, g_norm2, w_ff_gate, w_ff_up, w_ff_down)
    cos, sin = _rope_tables(TL)
    P = cache_k.shape[2]
    latent_ctx = (cache_k.reshape(BL, DEPTH, P, ATT_KV), cache_v.reshape(BL, DEPTH, P, ATT_KV),
                  jnp.swapaxes(state_gla, -1, -2), cos, sin)

    def layer_step(l, carry):
        yp, ys, new_k, new_v, new_st = carry
        layer = jnp.full((1,), l, jnp.int32)
        oa, ob, oc, k_l, v_l, s_l = _mix_call(False, layer, yp, mod_all, pw, None)
        yp = _post_call(layer, yp.reshape(B * T, D_MODEL), mod_all, oa.reshape(B * T, -1), ob.reshape(B * T, -1),
                        oc.reshape(B * T, -1), pw, T // POST_TILE, True).reshape(B, T, D_MODEL)
        new_k = lax.dynamic_update_slice_in_dim(new_k, k_l[:, None], l, axis=1)
        new_v = lax.dynamic_update_slice_in_dim(new_v, v_l[:, None], l, axis=1)
        new_st = lax.dynamic_update_slice_in_dim(new_st, s_l[:, None], l, axis=1)
        oa, ob, oc = _mix_call(True, layer, ys, mod_all, pw, latent_ctx)
        ys = _post_call(layer, ys.reshape(BL * TL, D_MODEL), mod_all, oa.reshape(BL * TL, -1),
                        ob.reshape(BL * TL, -1), oc.reshape(BL * TL, -1), pw, TL // POST_TILE,
                        False).reshape(BL, TL, D_MODEL)
        return yp, ys, new_k, new_v, new_st

    init = (x_prompt, x_sample,
            jnp.zeros((B, DEPTH, T, ATT_KV), _F32), jnp.zeros((B, DEPTH, T, ATT_KV), _F32),
            jnp.zeros((B, DEPTH, 2, GLA_HEADS, GLA_DK, GLA_DV), _F32))
    yp, ys, new_k, new_v, new_st = lax.fori_loop(0, DEPTH, layer_step, init)
    return (yp, ys, new_k.reshape(B, DEPTH, T, ATT_KV_HEADS, HEAD_DIM),
            new_v.reshape(B, DEPTH, T, ATT_KV_HEADS, HEAD_DIM), new_st)
```

```python
import functools

import jax
import jax.numpy as jnp
import numpy as np
from jax import lax
from jax.experimental import pallas as pl
from jax.experimental.pallas import tpu as pltpu

D_MODEL = 1024
DEPTH = 4
GRID_W = 64
ATT_HEADS = 8
ATT_KV_HEADS = 2
ATT_GROUP = ATT_HEADS // ATT_KV_HEADS
HEAD_DIM = 64
WINDOW = 128
ATT_BLOCK = 128
ROPE_BASE = 10000.0
GLA_HEADS = 4
GLA_DK = 64
GLA_DV = 128
GLA_RANK = 16
GLA_TAU = 16.0
GLA_CHUNK = 64
POOL_GROUPS = 4
POOL_GROUP_DIM = 128
POOL_WINDOWS = (2, 4, 8, 16)
D_FF = 2816
ATT_Q = ATT_HEADS * HEAD_DIM
ATT_KV = ATT_KV_HEADS * HEAD_DIM
GLA_QK = GLA_HEADS * GLA_DK
GLA_VW = GLA_HEADS * GLA_DV
POOL_W = POOL_GROUPS * POOL_GROUP_DIM
EPS = 1e-6
NEG = -1e30

C_QA = 0
C_KA = C_QA + ATT_Q
C_VA = C_KA + ATT_KV
C_QB = C_VA + ATT_KV
C_KB = C_QB + GLA_QK
C_VB = C_KB + GLA_QK
C_RB = C_VB + GLA_VW
C_UC = C_RB + GLA_VW
C_GL = C_UC + POOL_W
GL_PAD = 128
MIX_W = C_GL + GL_PAD
GATE_W = 3 * D_MODEL

POST_TILE = 256
PROJ_TILE = 256
POOL_TILE = 256
POOL_HALO = 128
MOD_ROWS = 8
MOD_TILE = 1024
GLA_BLOCK = 256
GLA_MAX_EXPONENT = 80.0
VMEM_LIMIT = 56 * 1024 * 1024

_F32 = jnp.float32
_BF16 = jnp.bfloat16


def _dot(a, b):
    return jnp.dot(a, b, preferred_element_type=_F32)


def _dot_nt(a, b):
    return lax.dot_general(a, b, (((1,), (1,)), ((), ())), preferred_element_type=_F32)


def _dot_tn(a, b):
    return lax.dot_general(a, b, (((0,), (0,)), ((), ())), preferred_element_type=_F32)


def _rms_scale(x):
    return lax.rsqrt(jnp.mean(x * x, axis=-1, keepdims=True) + EPS)


def _group_rms_scale(x, group_ones, width):
    sq = x * x
    hi = sq.astype(_BF16)
    lo = (sq - hi.astype(_F32)).astype(_BF16)
    return lax.rsqrt((_dot(hi, group_ones) + _dot(lo, group_ones)) * (1.0 / width) + EPS)


def _log_sigmoid(x):
    return jnp.minimum(x, 0.0) - jnp.log1p(jnp.exp(-jnp.abs(x)))


def _silu(x):
    return x * jax.nn.sigmoid(x)


def _rope(x, cos, sin_signed):
    n = x.shape[-1]
    lane = lax.broadcasted_iota(jnp.int32, x.shape, 1)
    up = pltpu.roll(x, n - HEAD_DIM // 4, axis=1)
    down = pltpu.roll(x, HEAD_DIM // 4, axis=1)
    partner = jnp.where((lane & (HEAD_DIM // 2 - 1)) < HEAD_DIM // 4, up, down)
    return x * cos + partner * sin_signed


def _mod_kernel(cv_ref, w_ref, b_ref, out_ref):
    s = _silu(cv_ref[...]).astype(_BF16)
    out_ref[0] = _dot(s, w_ref[0].astype(_BF16)) + b_ref[0]


def _modulation(cv, w_mod, b_mod):
    n_col = (6 * D_MODEL) // MOD_TILE
    return pl.pallas_call(
        _mod_kernel,
        grid=(DEPTH, n_col),
        in_specs=[
            pl.BlockSpec((MOD_ROWS, D_MODEL), lambda l, j: (0, 0)),
            pl.BlockSpec((1, D_MODEL, MOD_TILE), lambda l, j: (l, 0, j)),
            pl.BlockSpec((1, 1, MOD_TILE), lambda l, j: (l, 0, j)),
        ],
        out_specs=pl.BlockSpec((1, MOD_ROWS, MOD_TILE), lambda l, j: (l, 0, j)),
        out_shape=jax.ShapeDtypeStruct((DEPTH, MOD_ROWS, 6 * D_MODEL), _F32),
        name="modulation",
    )(cv, w_mod, b_mod.reshape(DEPTH, 1, 6 * D_MODEL))


def _split_heads(x):
    low = lax.broadcasted_iota(jnp.int32, x.shape, 1) < HEAD_DIM
    swapped = pltpu.roll(x, HEAD_DIM, axis=1)
    zero = jnp.zeros_like(x)
    return ((jnp.where(low, x, zero), jnp.where(low, zero, swapped)),
            (jnp.where(low, swapped, zero), jnp.where(low, zero, x)))


def _store_split_kv(k, v, ks_ref, vs_ref, rows):
    ones = jnp.ones_like(v)
    for kv, (k_sides, v_sides, one_sides) in enumerate(zip(_split_heads(k), _split_heads(v), _split_heads(ones))):
        for side in range(2):
            ks_ref[kv, side, rows, :] = k_sides[side].astype(_BF16)
            vs_ref[kv, side, rows, :] = jnp.concatenate([v_sides[side], one_sides[side]], axis=-1).astype(_BF16)


def _pair_softmax_av(qp, keys, values, masks, sink_even, sink_odd):
    m = qp.shape[0]
    scores = []
    for (k_left, k_right), mask in zip(keys, masks):
        s_even, s_odd = _dot_nt(qp, k_left), _dot_nt(qp, k_right)
        if mask is not None:
            s_even, s_odd = jnp.where(mask, s_even, NEG), jnp.where(mask, s_odd, NEG)
        scores.append((s_even, s_odd))
    m_even = jnp.full((m, 1), sink_even, _F32)
    m_odd = jnp.full((m, 1), sink_odd, _F32)
    for s_even, s_odd in scores:
        m_even = jnp.maximum(m_even, jnp.max(s_even, axis=-1, keepdims=True))
        m_odd = jnp.maximum(m_odd, jnp.max(s_odd, axis=-1, keepdims=True))
    res = None
    for (s_even, s_odd), (w_left, w_right) in zip(scores, values):
        r = (_dot(jnp.exp(s_even - m_even).astype(_BF16), w_left)
             + _dot(jnp.exp(s_odd - m_odd).astype(_BF16), w_right))
        res = r if res is None else res + r
    pair = 2 * HEAD_DIM
    low = lax.broadcasted_iota(jnp.int32, (m, pair), 1) < HEAD_DIM
    den = res[:, pair:] + jnp.where(low, jnp.exp(sink_even - m_even), jnp.exp(sink_odd - m_odd))
    return res[:, :pair] / den


def _attention_ctx(T, layer, qr_ref, ks_ref, vs_ref, sink_ref, oa_ref):
    pair = 2 * HEAD_DIM
    for kv in range(ATT_KV_HEADS):
        keys = [(ks_ref[kv, 0], ks_ref[kv, 1])]
        values = [(vs_ref[kv, 0], vs_ref[kv, 1])]
        for j in range(ATT_GROUP // 2):
            head = kv * ATT_GROUP + 2 * j
            cols = slice(head * HEAD_DIM, head * HEAD_DIM + pair)
            o = _pair_softmax_av(qr_ref[:, cols], keys, values, [None], sink_ref[layer, head],
                                 sink_ref[layer, head + 1])
            oa_ref[0, :, cols] = o.astype(_BF16)


def _attention_latent(T, layer, qr_ref, ks_ref, vs_ref, kc_ref, vc_ref, sink_ref, oa_ref):
    pair = 2 * HEAD_DIM
    span = 3 * ATT_BLOCK
    kc, vc = kc_ref[0, 0], vc_ref[0, 0]
    ones = jnp.ones_like(vc)
    ctx_keys = [tuple(side.astype(_BF16) for side in sides) for sides in _split_heads(kc)]
    ctx_values = [tuple(jnp.concatenate([v_side, one_side], axis=-1).astype(_BF16)
                        for v_side, one_side in zip(v_sides, one_sides))
                  for v_sides, one_sides in zip(_split_heads(vc), _split_heads(ones))]

    def block(i, carry):
        q_rows = pl.ds(pl.multiple_of(i * ATT_BLOCK, ATT_BLOCK), ATT_BLOCK)
        k_rows = pl.ds(pl.multiple_of(i * ATT_BLOCK, ATT_BLOCK), span)
        q_pos = i * ATT_BLOCK + lax.broadcasted_iota(jnp.int32, (ATT_BLOCK, span), 0)
        k_pos = (i - 1) * ATT_BLOCK + lax.broadcasted_iota(jnp.int32, (ATT_BLOCK, span), 1)
        valid = (jnp.abs(k_pos - q_pos) <= WINDOW) & (k_pos >= 0) & (k_pos < T)
        for kv in range(ATT_KV_HEADS):
            keys = [(ks_ref[kv, 0, k_rows, :], ks_ref[kv, 1, k_rows, :]), ctx_keys[kv]]
            values = [(vs_ref[kv, 0, k_rows, :], vs_ref[kv, 1, k_rows, :]), ctx_values[kv]]
            for j in range(ATT_GROUP // 2):
                head = kv * ATT_GROUP + 2 * j
                cols = slice(head * HEAD_DIM, head * HEAD_DIM + pair)
                o = _pair_softmax_av(qr_ref[q_rows, cols], keys, values, [valid, None], sink_ref[layer, head],
                                     sink_ref[layer, head + 1])
                oa_ref[0, q_rows, cols] = o.astype(_BF16)
        return carry

    lax.fori_loop(0, T // ATT_BLOCK, block, 0)


def _gla_chunk(z_ref, la_ref, o_ref, st_ref, a_ref, direction, start):
    C = GLA_CHUNK
    rows = pl.ds(pl.multiple_of(start, C), C)
    q = z_ref[rows, C_QB:C_QB + GLA_QK] * (GLA_DK ** -0.5)
    k = z_ref[rows, C_KB:C_KB + GLA_QK]
    v = z_ref[rows, C_VB:C_VB + GLA_VW].astype(_BF16)
    la = la_ref[rows, direction * GLA_QK:(direction + 1) * GLA_QK]
    la_hi = la.astype(_BF16)
    la_lo = (la - la_hi.astype(_F32)).astype(_BF16)
    ti = lax.broadcasted_iota(jnp.int32, (C, C), 0)
    si = lax.broadcasted_iota(jnp.int32, (C, C), 1)
    causal = (si <= ti) if direction == 0 else (si >= ti)
    tri = jnp.where(causal, 1.0, 0.0).astype(_BF16)
    b = _dot(tri, la_hi) + _dot(tri, la_lo)
    end = C - 1 if direction == 0 else 0
    b_end = b[end:end + 1]
    q_in = (q * jnp.exp(b)).astype(_BF16)
    k_out = (k * jnp.exp(b_end - b)).astype(_BF16)
    e_end = jnp.exp(b_end)
    ones = jnp.ones((8, GLA_DK), _BF16)

    def row_group(g, carry):
        base = pl.multiple_of(g * 8, 8)
        b8 = a_ref[GLA_HEADS, pl.ds(base, 8), :]
        q8 = a_ref[GLA_HEADS + 1, pl.ds(base, 8), :]
        s_idx = lax.broadcasted_iota(jnp.int32, (C, 1), 0)
        rows_h = [[] for _ in range(GLA_HEADS)]
        for j in range(8):
            ok = (s_idx <= base + j) if direction == 0 else (s_idx >= base + j)
            decay = jnp.exp(jnp.where(ok, b8[j:j + 1] - b, NEG))
            p = (q8[j:j + 1] * k * decay).astype(_BF16)
            for h in range(GLA_HEADS):
                rows_h[h].append(_dot_nt(ones, p[:, h * GLA_DK:(h + 1) * GLA_DK])[0:1])
        for h in range(GLA_HEADS):
            a_ref[h, pl.ds(base, 8), 0:C] = jnp.concatenate(rows_h[h], axis=0)
        return carry

    a_ref[GLA_HEADS] = b
    a_ref[GLA_HEADS + 1] = q
    lax.fori_loop(0, C // 8, row_group, 0)
    outs = []
    for pair in range(GLA_HEADS // 2):
        s_pair = st_ref[direction, pair]
        s_next = []
        for h in (2 * pair, 2 * pair + 1):
            kc = slice(h * GLA_DK, (h + 1) * GLA_DK)
            vc = slice(h * GLA_DV, (h + 1) * GLA_DV)
            s_t = s_pair[:, (h % 2) * GLA_DK:(h % 2 + 1) * GLA_DK]
            outs.append(_dot_nt(q_in[:, kc], s_t.astype(_BF16)) + _dot(a_ref[h, :, 0:C].astype(_BF16), v[:, vc]))
            s_next.append(s_t * e_end[:, kc] + _dot_tn(v[:, vc], k_out[:, kc]))
        st_ref[direction, pair] = jnp.concatenate(s_next, axis=-1)
    o_ref[rows, :] += jnp.concatenate(outs, axis=-1)


def _gla_block(z_ref, la_ref, o_ref, st_ref, direction, start, use_state):
    NB = GLA_BLOCK
    fwd = direction == 0
    rows = slice(start, start + NB) if isinstance(start, int) else pl.ds(pl.multiple_of(start, NB), NB)
    q = z_ref[rows, C_QB:C_QB + GLA_QK] * (GLA_DK ** -0.5)
    k = z_ref[rows, C_KB:C_KB + GLA_QK]
    v = z_ref[rows, C_VB:C_VB + GLA_VW].astype(_BF16)
    b = la_ref[rows, direction * GLA_QK:(direction + 1) * GLA_QK]
    end, mid = (NB - 1, NB // 2 - 1) if fwd else (0, NB // 2)
    b_end = b[end:end + 1]
    c = b - b[mid:mid + 1]
    q_c = (q * jnp.exp(c)).astype(_BF16)
    k_c = (k * jnp.exp(-c)).astype(_BF16)
    k_fin = (k * jnp.exp(b_end - b)).astype(_BF16)
    if use_state:
        q_in = (q * jnp.exp(b)).astype(_BF16)
        e_all = jnp.exp(b_end)
    ti = lax.broadcasted_iota(jnp.int32, (NB, NB), 0)
    si = lax.broadcasted_iota(jnp.int32, (NB, NB), 1)
    causal = (si <= ti) if fwd else (si >= ti)
    tile = 2 * GLA_DK
    low = lax.broadcasted_iota(jnp.int32, (NB, tile), 1) < GLA_DK
    zero = jnp.zeros((NB, tile), _BF16)
    pick = lambda x, parity: jnp.where(low, x, zero) if parity == 0 else jnp.where(low, zero, x)
    outs = []
    for pair in range(GLA_HEADS // 2):
        lanes = slice(pair * tile, (pair + 1) * tile)
        if use_state:
            s_pair = st_ref[direction, pair]
            s_bf = s_pair.astype(_BF16)
        s_new = None
        for parity in range(2):
            h = 2 * pair + parity
            vc = slice(h * GLA_DV, (h + 1) * GLA_DV)
            a = jnp.where(causal, _dot_nt(q_c[:, lanes], pick(k_c[:, lanes], parity)), 0.0)
            o_h = _dot(a.astype(_BF16), v[:, vc])
            if use_state:
                o_h = o_h + _dot_nt(pick(q_in[:, lanes], parity), s_bf)
            outs.append(o_h)
            upd = _dot_tn(v[:, vc], pick(k_fin[:, lanes], parity))
            s_new = upd if s_new is None else s_new + upd
        if use_state:
            s_new = s_new + s_pair * e_all[:, lanes]
        st_ref[direction, pair] = s_new
    o_ref[rows, :] += jnp.concatenate(outs, axis=-1)


def _log_decay(z_ref, rows, wg2_ref, bg2_ref):
    pre = lax.dot_general(z_ref[rows, C_GL:C_GL + GL_PAD], wg2_ref[...], (((1,), (0,)), ((), ())),
                          precision=lax.Precision.HIGHEST, preferred_element_type=_F32) + bg2_ref[...]
    return _log_sigmoid(pre) * (1.0 / GLA_TAU)


def _gla(T, z_ref, la_ref, o_ref, st_ref, a_ref, wg2_ref, bg2_ref, use_state):
    NB = GLA_BLOCK
    n_chunks = T // GLA_CHUNK
    n_blocks = T // NB
    ti = lax.broadcasted_iota(jnp.int32, (NB, NB), 0)
    si = lax.broadcasted_iota(jnp.int32, (NB, NB), 1)
    tri = [jnp.where(si <= ti, 1.0, 0.0).astype(_BF16), jnp.where(si >= ti, 1.0, 0.0).astype(_BF16)]
    worst = None
    for r0 in range(0, T, NB):
        rows = slice(r0, r0 + NB)
        la = _log_decay(z_ref, rows, wg2_ref, bg2_ref)
        la_hi = la.astype(_BF16)
        la_lo = (la - la_hi.astype(_F32)).astype(_BF16)
        for d in range(2):
            cols = slice(d * GLA_QK, (d + 1) * GLA_QK)
            b = _dot(tri[d], la_hi[:, cols]) + _dot(tri[d], la_lo[:, cols])
            la_ref[rows, cols] = b
            first, mid, last = (0, NB // 2 - 1, NB - 1) if d == 0 else (NB - 1, NB // 2, 0)
            span = jnp.max(jnp.maximum(b[first:first + 1] - b[mid:mid + 1], b[mid:mid + 1] - b[last:last + 1]))
            worst = span if worst is None else jnp.maximum(worst, span)
    fast_ok = worst <= GLA_MAX_EXPONENT
    o_ref[...] = jnp.zeros(o_ref.shape, _F32)

    @pl.when(fast_ok)
    def _():
        if n_blocks == 1:
            for direction in range(2):
                _gla_block(z_ref, la_ref, o_ref, st_ref, direction, 0, use_state)
        else:
            def body(i, carry):
                _gla_block(z_ref, la_ref, o_ref, st_ref, 0, i * NB, True)
                _gla_block(z_ref, la_ref, o_ref, st_ref, 1, (n_blocks - 1 - i) * NB, True)
                return carry
            lax.fori_loop(0, n_blocks, body, 0)

    @pl.when(jnp.logical_not(fast_ok))
    def _():
        for r0 in range(0, T, NB):
            la_ref[r0:r0 + NB, :] = _log_decay(z_ref, slice(r0, r0 + NB), wg2_ref, bg2_ref)

        def body(i, carry):
            _gla_chunk(z_ref, la_ref, o_ref, st_ref, a_ref, 0, i * GLA_CHUNK)
            _gla_chunk(z_ref, la_ref, o_ref, st_ref, a_ref, 1, (n_chunks - 1 - i) * GLA_CHUNK)
            return carry
        lax.fori_loop(0, n_chunks, body, 0)


def _gla_finish(T, z_ref, o_ref, ggla_ref, vones_ref, ob_ref):
    for r0 in range(0, T, PROJ_TILE):
        rows = slice(r0, r0 + PROJ_TILE)
        o = o_ref[rows, :]
        y = o * _group_rms_scale(o, vones_ref[...], GLA_DV) * ggla_ref[...]
        ob_ref[0, rows, :] = (y * _silu(z_ref[rows, C_RB:C_RB + GLA_VW])).astype(_BF16)


def _pool(T, z_ref, upad_ref, wpool_ref, pscale_ref, oc_ref):
    u = z_ref[:, C_UC:C_UC + POOL_W]
    u_hi = u.astype(_BF16)
    u_lo = (u - u_hi.astype(_F32)).astype(_BF16)
    zeros = jnp.zeros((POOL_HALO, POOL_W), _BF16)
    for part, val in ((0, u_hi), (1, u_lo)):
        upad_ref[part, 0:POOL_HALO, :] = zeros
        upad_ref[part, POOL_HALO + T:POOL_HALO + T + POOL_HALO, :] = zeros
        upad_ref[part, POOL_HALO:POOL_HALO + T, :] = val
    span = POOL_TILE + 2 * POOL_HALO
    r = lax.broadcasted_iota(jnp.int32, (POOL_TILE, span), 0)
    c = lax.broadcasted_iota(jnp.int32, (POOL_TILE, span), 1)
    off = c - POOL_HALO - r
    for jb in range(T // POOL_TILE):
        t = jb * POOL_TILE + lax.broadcasted_iota(jnp.int32, (POOL_TILE, 1), 0)
        parts = []
        for g, w in enumerate(POOL_WINDOWS):
            cols = slice(g * POOL_GROUP_DIM, (g + 1) * POOL_GROUP_DIM)
            band = jnp.where((off >= -(w // 2)) & (off < w - w // 2), 1.0, 0.0).astype(_BF16)
            win = slice(jb * POOL_TILE, jb * POOL_TILE + span)
            total = _dot(band, upad_ref[0, win, cols]) + _dot(band, upad_ref[1, win, cols])
            cnt = (jnp.minimum(t - w // 2 + w, T) - jnp.maximum(t - w // 2, 0)).astype(_F32)
            pooled = total / cnt - z_ref[jb * POOL_TILE:(jb + 1) * POOL_TILE, C_UC + g * POOL_GROUP_DIM:
                                         C_UC + (g + 1) * POOL_GROUP_DIM]
            parts.append(_dot(pooled.astype(_BF16), wpool_ref[g]))
        y = jnp.concatenate(parts, axis=-1) * pscale_ref[...]
        oc_ref[0, jb * POOL_TILE:(jb + 1) * POOL_TILE, :] = y.astype(_BF16)


def _mix_kernel(latent, T, layer_ref, *refs):
    (x_ref, mod_ref, gn1_ref, wmix_ref, gqn_ref, gkn_ref, sink_ref, wg2_ref, bg2_ref, ggla_ref,
     wpool_ref, pscale_ref, hones_ref, vones_ref) = refs[:14]
    refs = refs[14:]
    if latent:
        kc_ref, vc_ref, st0_ref, cos_ref, sin_ref, oa_ref, ob_ref, oc_ref = refs[:8]
        refs = refs[8:]
    else:
        oa_ref, ob_ref, oc_ref, kout_ref, vout_ref, stout_ref = refs[:6]
        refs = refs[6:]
    z_ref, qr_ref, ks_ref, vs_ref, la_ref, o_ref, st_ref, a_ref, upad_ref = refs
    layer = layer_ref[0]
    pad = ATT_BLOCK if latent else 0

    if latent:
        for ref in (ks_ref, vs_ref):
            zeros = jnp.zeros(ref.shape[:2] + (pad, ref.shape[3]), _BF16)
            ref[:, :, 0:pad, :] = zeros
            ref[:, :, pad + T:pad + T + pad, :] = zeros
    shift = mod_ref[0, :, 0:D_MODEL]
    scale = mod_ref[0, :, D_MODEL:2 * D_MODEL]
    for r0 in range(0, T, PROJ_TILE):
        rows = slice(r0, r0 + PROJ_TILE)
        x = x_ref[0, rows, :]
        hn = (x * _rms_scale(x) * gn1_ref[...]) * (1.0 + scale) + shift
        z_ref[rows, :] = _dot(hn.astype(_BF16), wmix_ref[...])
        q = z_ref[rows, C_QA:C_QA + ATT_Q]
        k = z_ref[rows, C_KA:C_KA + ATT_KV]
        q = q * _group_rms_scale(q, hones_ref[...], HEAD_DIM) * gqn_ref[...]
        k = k * _group_rms_scale(k, hones_ref[0:ATT_KV, 0:ATT_KV], HEAD_DIM) * gkn_ref[...]
        v = z_ref[rows, C_VA:C_VA + ATT_KV]
        if latent:
            cos = jnp.concatenate([cos_ref[rows, :]] * (ATT_Q // ATT_KV), axis=-1)
            sin = jnp.concatenate([sin_ref[rows, :]] * (ATT_Q // ATT_KV), axis=-1)
            q = _rope(q, cos, sin)
            k = _rope(k, cos_ref[rows, :], sin_ref[rows, :])
        else:
            kout_ref[0, rows, :] = k
            vout_ref[0, rows, :] = v
        qr_ref[rows, :] = (q * (HEAD_DIM ** -0.5)).astype(_BF16)
        _store_split_kv(k, v, ks_ref, vs_ref, slice(pad + r0, pad + r0 + PROJ_TILE))
    if latent:
        _attention_latent(T, layer, qr_ref, ks_ref, vs_ref, kc_ref, vc_ref, sink_ref, oa_ref)
    else:
        _attention_ctx(T, layer, qr_ref, ks_ref, vs_ref, sink_ref, oa_ref)

    if latent:
        st_ref[...] = st0_ref[0]
    else:
        st_ref[...] = jnp.zeros(st_ref.shape, _F32)
    _gla(T, z_ref, la_ref, o_ref, st_ref, a_ref, wg2_ref, bg2_ref, latent)
    _gla_finish(T, z_ref, o_ref, ggla_ref, vones_ref, ob_ref)
    if not latent:
        for d in range(2):
            for pair in range(GLA_HEADS // 2):
                stout_ref[0, d, pair] = st_ref[d, pair].T

    _pool(T, z_ref, upad_ref, wpool_ref, pscale_ref, oc_ref)


def _layer_spec(shape):
    zeros = (0,) * len(shape)
    return pl.BlockSpec((None,) + tuple(shape), lambda i, layer: (layer[0],) + zeros,
                        pipeline_mode=pl.Buffered(1))


def _mix_call(latent, layer, x, mod_all, pw, extra):
    B, T, _ = x.shape
    per_seq = lambda blk: pl.BlockSpec(blk, lambda b, layer: (b,) + (0,) * (len(blk) - 1))
    const_spec = lambda blk: pl.BlockSpec(blk, lambda b, layer: (0,) * len(blk), pipeline_mode=pl.Buffered(1))
    if latent:
        mod_spec = pl.BlockSpec((None, 1, 1, 6 * D_MODEL), lambda b, layer: (layer[0], b + 1, 0, 0))
    else:
        mod_spec = pl.BlockSpec((None, 1, 1, 6 * D_MODEL), lambda b, layer: (layer[0], 0, 0, 0))
    x_spec = pl.BlockSpec((1, T, D_MODEL), lambda b, layer: (b, 0, 0),
                          pipeline_mode=pl.Buffered(1) if latent else None)
    in_specs = [
        x_spec, mod_spec,
        _layer_spec((1, D_MODEL)), _layer_spec((D_MODEL, MIX_W)),
        _layer_spec((1, ATT_Q)), _layer_spec((1, ATT_KV)),
        pl.BlockSpec(memory_space=pltpu.SMEM),
        _layer_spec((GL_PAD, 2 * GLA_QK)), _layer_spec((1, 2 * GLA_QK)), _layer_spec((1, GLA_VW)),
        _layer_spec((POOL_GROUPS, POOL_GROUP_DIM, POOL_GROUP_DIM)), _layer_spec((1, POOL_W)),
        const_spec((ATT_Q, ATT_Q)), const_spec((GLA_VW, GLA_VW)),
    ]
    args = [x, mod_all, pw["g_norm1"], pw["w_mix"], pw["g_qn"], pw["g_kn"], pw["att_sink"], pw["w_gate2"],
            pw["b_gate2"], pw["g_gla_out"], pw["w_pool"], pw["pool_scale"], pw["head_ones"], pw["gla_ones"]]
    act = lambda width: jax.ShapeDtypeStruct((B, T, width), _BF16)
    out_shape = [act(ATT_Q), act(GLA_VW), act(POOL_W)]
    out_specs = [per_seq((1, T, ATT_Q)), per_seq((1, T, GLA_VW)), per_seq((1, T, POOL_W))]
    if latent:
        cache_k, cache_v, st0, cos, sin = extra
        P = cache_k.shape[2]
        cache_spec = pl.BlockSpec((1, 1, P, ATT_KV), lambda b, layer: (b, layer[0], 0, 0))
        table_spec = const_spec((T, ATT_KV))
        in_specs += [cache_spec, cache_spec,
                     pl.BlockSpec((1, None, 2, GLA_HEADS // 2, GLA_DV, 2 * GLA_DK),
                                  lambda b, layer: (b, layer[0], 0, 0, 0, 0)),
                     table_spec, table_spec]
        args += [cache_k, cache_v, st0, cos, sin]
    else:
        out_shape += [jax.ShapeDtypeStruct((B, T, ATT_KV), _F32), jax.ShapeDtypeStruct((B, T, ATT_KV), _F32),
                      jax.ShapeDtypeStruct((B, 2, GLA_HEADS // 2, 2 * GLA_DK, GLA_DV), _F32)]
        out_specs += [per_seq((1, T, ATT_KV)), per_seq((1, T, ATT_KV)),
                      per_seq((1, 2, GLA_HEADS // 2, 2 * GLA_DK, GLA_DV))]
    kv_rows = T + 2 * ATT_BLOCK if latent else T
    scratch = [
        pltpu.VMEM((T, MIX_W), _F32),
        pltpu.VMEM((T, ATT_Q), _BF16),
        pltpu.VMEM((ATT_KV_HEADS, 2, kv_rows, 2 * HEAD_DIM), _BF16),
        pltpu.VMEM((ATT_KV_HEADS, 2, kv_rows, 4 * HEAD_DIM), _BF16),
        pltpu.VMEM((T, 2 * GLA_QK), _F32),
        pltpu.VMEM((T, GLA_VW), _F32),
        pltpu.VMEM((2, GLA_HEADS // 2, GLA_DV, 2 * GLA_DK), _F32),
        pltpu.VMEM((GLA_HEADS + 2, GLA_CHUNK, GLA_QK), _F32),
        pltpu.VMEM((2, T + 2 * POOL_HALO, POOL_W), _BF16),
    ]
    return pl.pallas_call(
        functools.partial(_mix_kernel, latent, T),
        grid_spec=pltpu.PrefetchScalarGridSpec(
            num_scalar_prefetch=1, grid=(B,), in_specs=in_specs, out_specs=out_specs, scratch_shapes=scratch),
        out_shape=out_shape,
        compiler_params=pltpu.CompilerParams(dimension_semantics=("arbitrary",), vmem_limit_bytes=VMEM_LIMIT),
        name="mix_latent" if latent else "mix_ctx",
    )(layer, *args)


def _post_kernel(layer_ref, x_ref, mod_ref, oa_ref, ob_ref, oc_ref, gn1_ref, gn2_ref, wgate_ref, wa_ref, wb_ref,
                 wc_ref, wout_ref, wfg_ref, wfu_ref, wfd_ref, out_ref):
    x = x_ref[...]
    mod = lambda i: mod_ref[0, :, i * D_MODEL:(i + 1) * D_MODEL]
    hn = (x * _rms_scale(x) * gn1_ref[...]) * (1.0 + mod(1)) + mod(0)
    gates = jax.nn.sigmoid(_dot(hn.astype(_BF16), wgate_ref[...]))
    mixed = (gates[:, 0:D_MODEL] * _dot(oa_ref[...], wa_ref[...])
             + gates[:, D_MODEL:2 * D_MODEL] * _dot(ob_ref[...], wb_ref[...])
             + gates[:, 2 * D_MODEL:3 * D_MODEL] * _dot(oc_ref[...], wc_ref[...]))
    x = x + mod(2) * _dot(mixed.astype(_BF16), wout_ref[...])
    hn = ((x * _rms_scale(x) * gn2_ref[...]) * (1.0 + mod(4)) + mod(3)).astype(_BF16)
    h = _silu(_dot(hn, wfg_ref[...])) * _dot(hn, wfu_ref[...])
    out_ref[...] = x + mod(5) * _dot(h.astype(_BF16), wfd_ref[...])


def _post_call(layer, x2d, mod_all, oa, ob, oc, pw, tiles_per_seq, shared_mod):
    n = x2d.shape[0]
    row = lambda w: pl.BlockSpec((POST_TILE, w), lambda i, layer: (i, 0))
    if shared_mod:
        mod_spec = pl.BlockSpec((None, 1, 1, 6 * D_MODEL), lambda i, layer: (layer[0], 0, 0, 0))
    else:
        mod_spec = pl.BlockSpec((None, 1, 1, 6 * D_MODEL),
                                lambda i, layer: (layer[0], 1 + i // tiles_per_seq, 0, 0))
    in_specs = [row(D_MODEL), mod_spec, row(ATT_Q), row(GLA_VW), row(POOL_W),
                _layer_spec((1, D_MODEL)), _layer_spec((1, D_MODEL)), _layer_spec((D_MODEL, GATE_W)),
                _layer_spec((ATT_Q, D_MODEL)), _layer_spec((GLA_VW, D_MODEL)), _layer_spec((POOL_W, D_MODEL)),
                _layer_spec((D_MODEL, D_MODEL)), _layer_spec((D_MODEL, D_FF)), _layer_spec((D_MODEL, D_FF)),
                _layer_spec((D_FF, D_MODEL))]
    return pl.pallas_call(
        _post_kernel,
        grid_spec=pltpu.PrefetchScalarGridSpec(
            num_scalar_prefetch=1, grid=(n // POST_TILE,), in_specs=in_specs, out_specs=row(D_MODEL)),
        out_shape=jax.ShapeDtypeStruct(x2d.shape, _F32),
        input_output_aliases={1: 0},
        compiler_params=pltpu.CompilerParams(dimension_semantics=("arbitrary",), vmem_limit_bytes=VMEM_LIMIT),
        name="post",
    )(layer, x2d, mod_all, oa, ob, oc, pw["g_norm1"], pw["g_norm2"], pw["w_gates"], pw["w_br_a"], pw["w_br_b"],
      pw["w_br_c"], pw["w_out"], pw["w_ff_gate"], pw["w_ff_up"], pw["w_ff_down"])


def _rope_tables(T):
    quarter = HEAD_DIM // 4
    inv_freq = ROPE_BASE ** (-np.arange(quarter, dtype=np.float32) / quarter)
    pos = np.arange(T)
    ang_row = (pos // GRID_W).astype(np.float32)[:, None] * inv_freq[None, :]
    ang_col = (pos % GRID_W).astype(np.float32)[:, None] * inv_freq[None, :]
    cos = np.concatenate([np.cos(ang_row)] * 2 + [np.cos(ang_col)] * 2, axis=-1)
    sin = np.concatenate([-np.sin(ang_row), np.sin(ang_row), -np.sin(ang_col), np.sin(ang_col)], axis=-1)
    return (jnp.asarray(np.tile(cos, (1, ATT_KV_HEADS)), _F32), jnp.asarray(np.tile(sin, (1, ATT_KV_HEADS)), _F32))


def _prepare_weights(w_in, g_qn, g_kn, att_sink, w_gate2, b_gate2, g_gla_out, w_pool, pool_scale, w_br_a,
                     w_br_b, w_br_c, w_out, g_norm1, g_norm2, w_ff_gate, w_ff_up, w_ff_down):
    o_gl = ATT_Q + 2 * ATT_KV + 2 * GLA_QK + 2 * GLA_VW
    o_uc = o_gl + 2 * GLA_RANK
    o_gate = o_uc + POOL_W
    w_mix = jnp.concatenate(
        [w_in[:, :, :o_gl], w_in[:, :, o_uc:o_gate], w_in[:, :, o_gl:o_uc],
         jnp.zeros((DEPTH, D_MODEL, GL_PAD - 2 * GLA_RANK), w_in.dtype)], axis=2).astype(_BF16)
    wg2 = jnp.zeros((DEPTH, GL_PAD, 2 * GLA_QK), _F32)
    wg2 = wg2.at[:, 0:GLA_RANK, 0:GLA_QK].set(w_gate2[:, 0])
    wg2 = wg2.at[:, GLA_RANK:2 * GLA_RANK, GLA_QK:].set(w_gate2[:, 1])
    vec = lambda a: a.reshape(DEPTH, 1, -1)
    group_ones = lambda n, width: jnp.asarray(
        (np.arange(n)[:, None] // width) == (np.arange(n)[None, :] // width), _BF16)
    return {
        "head_ones": group_ones(ATT_Q, HEAD_DIM),
        "gla_ones": group_ones(GLA_VW, GLA_DV),
        "w_mix": w_mix,
        "w_gates": w_in[:, :, o_gate:].astype(_BF16),
        "g_qn": vec(jnp.tile(g_qn, (1, ATT_HEADS))),
        "g_kn": vec(jnp.tile(g_kn, (1, ATT_KV_HEADS))),
        "att_sink": att_sink,
        "w_gate2": wg2,
        "b_gate2": vec(b_gate2),
        "g_gla_out": vec(jnp.tile(g_gla_out, (1, GLA_HEADS))),
        "w_pool": w_pool.astype(_BF16),
        "pool_scale": vec(pool_scale),
        "w_br_a": w_br_a.astype(_BF16),
        "w_br_b": w_br_b.astype(_BF16),
        "w_br_c": w_br_c.astype(_BF16),
        "w_out": w_out.astype(_BF16),
        "g_norm1": vec(g_norm1),
        "g_norm2": vec(g_norm2),
        "w_ff_gate": w_ff_gate.astype(_BF16),
        "w_ff_up": w_ff_up.astype(_BF16),
        "w_ff_down": w_ff_down.astype(_BF16),
    }


def kernel(x_prompt, x_sample, c, cache_k, cache_v, state_gla, c_ctx, w_in, g_qn, g_kn, att_sink, w_gate2,
           b_gate2, g_gla_out, w_pool, pool_scale, w_br_a, w_br_b, w_br_c, w_out, g_norm1, g_norm2, w_mod,
           b_mod, w_ff_gate, w_ff_up, w_ff_down):
    B, T, _ = x_prompt.shape
    BL, TL, _ = x_sample.shape
    assert T % POST_TILE == 0 and TL % POST_TILE == 0 and BL + 1 <= MOD_ROWS
    cv = jnp.concatenate([c_ctx[None, :], c, jnp.zeros((MOD_ROWS - 1 - BL, D_MODEL), _F32)], axis=0)
    mod_all = _modulation(cv, w_mod, b_mod).reshape(DEPTH, MOD_ROWS, 1, 6 * D_MODEL)
    pw = _prepare_weights(w_in, g_qn, g_kn, att_sink, w_gate2, b_gate2, g_gla_out, w_pool, pool_scale, w_br_a,
                          w_br_b, w_br_c, w_out, g_norm1, g_norm2, w_ff_gate, w_ff_up, w_ff_down)
    cos, sin = _rope_tables(TL)
    P = cache_k.shape[2]
    latent_ctx = (cache_k.reshape(BL, DEPTH, P, ATT_KV), cache_v.reshape(BL, DEPTH, P, ATT_KV),
                  jnp.swapaxes(state_gla.reshape(BL, DEPTH, 2, GLA_HEADS // 2, 2 * GLA_DK, GLA_DV), -1, -2),
                  cos, sin)

    def layer_step(l, carry):
        yp, ys, new_k, new_v, new_st = carry
        layer = jnp.full((1,), l, jnp.int32)
        oa, ob, oc, k_l, v_l, s_l = _mix_call(False, layer, yp, mod_all, pw, None)
        yp = _post_call(layer, yp.reshape(B * T, D_MODEL), mod_all, oa.reshape(B * T, -1), ob.reshape(B * T, -1),
                        oc.reshape(B * T, -1), pw, T // POST_TILE, True).reshape(B, T, D_MODEL)
        new_k = lax.dynamic_update_slice_in_dim(new_k, k_l[:, None], l, axis=1)
        new_v = lax.dynamic_update_slice_in_dim(new_v, v_l[:, None], l, axis=1)
        new_st = lax.dynamic_update_slice_in_dim(
            new_st, s_l.reshape(B, 1, 2, GLA_HEADS, GLA_DK, GLA_DV), l, axis=1)
        oa, ob, oc = _mix_call(True, layer, ys, mod_all, pw, latent_ctx)
        ys = _post_call(layer, ys.reshape(BL * TL, D_MODEL), mod_all, oa.reshape(BL * TL, -1),
                        ob.reshape(BL * TL, -1), oc.reshape(BL * TL, -1), pw, TL // POST_TILE,
                        False).reshape(BL, TL, D_MODEL)
        return yp, ys, new_k, new_v, new_st

    init = (x_prompt, x_sample,
            jnp.zeros((B, DEPTH, T, ATT_KV), _F32), jnp.zeros((B, DEPTH, T, ATT_KV), _F32),
            jnp.zeros((B, DEPTH, 2, GLA_HEADS, GLA_DK, GLA_DV), _F32))
    yp, ys, new_k, new_v, new_st = lax.fori_loop(0, DEPTH, layer_step, init)
    return (yp, ys, new_k.reshape(B, DEPTH, T, ATT_KV_HEADS, HEAD_DIM),
            new_v.reshape(B, DEPTH, T, ATT_KV_HEADS, HEAD_DIM), new_st)
```

```python
import functools

import jax
import jax.numpy as jnp
import numpy as np
from jax import lax
from jax.experimental import pallas as pl
from jax.experimental.pallas import tpu as pltpu

D_MODEL = 1024
DEPTH = 4
GRID_W = 64
ATT_HEADS = 8
ATT_KV_HEADS = 2
ATT_GROUP = ATT_HEADS // ATT_KV_HEADS
HEAD_DIM = 64
WINDOW = 128
ATT_BLOCK = 128
ROPE_BASE = 10000.0
GLA_HEADS = 4
GLA_DK = 64
GLA_DV = 128
GLA_RANK = 16
GLA_TAU = 16.0
GLA_CHUNK = 64
POOL_GROUPS = 4
POOL_GROUP_DIM = 128
POOL_WINDOWS = (2, 4, 8, 16)
D_FF = 2816
ATT_Q = ATT_HEADS * HEAD_DIM
ATT_KV = ATT_KV_HEADS * HEAD_DIM
GLA_QK = GLA_HEADS * GLA_DK
GLA_VW = GLA_HEADS * GLA_DV
POOL_W = POOL_GROUPS * POOL_GROUP_DIM
EPS = 1e-6
NEG = -1e30

C_QA = 0
C_KA = C_QA + ATT_Q
C_VA = C_KA + ATT_KV
C_QB = C_VA + ATT_KV
C_KB = C_QB + GLA_QK
C_VB = C_KB + GLA_QK
C_RB = C_VB + GLA_VW
C_UC = C_RB + GLA_VW
C_GL = C_UC + POOL_W
GL_PAD = 128
MIX_W = C_GL + GL_PAD
GATE_W = 3 * D_MODEL

POST_TILE = 256
PROJ_TILE = 256
POOL_TILE = 256
POOL_HALO = 128
MOD_ROWS = 8
MOD_TILE = 1024
GLA_BLOCK = 256
GLA_MAX_EXPONENT = 80.0
VMEM_LIMIT = 56 * 1024 * 1024

_F32 = jnp.float32
_BF16 = jnp.bfloat16


def _dot(a, b):
    return jnp.dot(a, b, preferred_element_type=_F32)


def _dot_nt(a, b):
    return lax.dot_general(a, b, (((1,), (1,)), ((), ())), preferred_element_type=_F32)


def _dot_tn(a, b):
    return lax.dot_general(a, b, (((0,), (0,)), ((), ())), preferred_element_type=_F32)


def _rms_scale(x):
    return lax.rsqrt(jnp.mean(x * x, axis=-1, keepdims=True) + EPS)


def _group_rms_scale(x, group_ones, width):
    sq = x * x
    hi = sq.astype(_BF16)
    lo = (sq - hi.astype(_F32)).astype(_BF16)
    return lax.rsqrt((_dot(hi, group_ones) + _dot(lo, group_ones)) * (1.0 / width) + EPS)


def _log_sigmoid(x):
    return jnp.minimum(x, 0.0) - jnp.log1p(jnp.exp(-jnp.abs(x)))


def _silu(x):
    return x * jax.nn.sigmoid(x)


def _rope(x, cos, sin_signed):
    n = x.shape[-1]
    lane = lax.broadcasted_iota(jnp.int32, x.shape, 1)
    up = pltpu.roll(x, n - HEAD_DIM // 4, axis=1)
    down = pltpu.roll(x, HEAD_DIM // 4, axis=1)
    partner = jnp.where((lane & (HEAD_DIM // 2 - 1)) < HEAD_DIM // 4, up, down)
    return x * cos + partner * sin_signed


def _mod_kernel(cv_ref, w_ref, b_ref, out_ref):
    s = _silu(cv_ref[...]).astype(_BF16)
    out_ref[0] = _dot(s, w_ref[0].astype(_BF16)) + b_ref[0]


def _modulation(cv, w_mod, b_mod):
    n_col = (6 * D_MODEL) // MOD_TILE
    return pl.pallas_call(
        _mod_kernel,
        grid=(DEPTH, n_col),
        in_specs=[
            pl.BlockSpec((MOD_ROWS, D_MODEL), lambda l, j: (0, 0)),
            pl.BlockSpec((1, D_MODEL, MOD_TILE), lambda l, j: (l, 0, j)),
            pl.BlockSpec((1, 1, MOD_TILE), lambda l, j: (l, 0, j)),
        ],
        out_specs=pl.BlockSpec((1, MOD_ROWS, MOD_TILE), lambda l, j: (l, 0, j)),
        out_shape=jax.ShapeDtypeStruct((DEPTH, MOD_ROWS, 6 * D_MODEL), _F32),
        name="modulation",
    )(cv, w_mod, b_mod.reshape(DEPTH, 1, 6 * D_MODEL))


def _split_heads(x):
    low = lax.broadcasted_iota(jnp.int32, x.shape, 1) < HEAD_DIM
    swapped = pltpu.roll(x, HEAD_DIM, axis=1)
    zero = jnp.zeros_like(x)
    return ((jnp.where(low, x, zero), jnp.where(low, zero, swapped)),
            (jnp.where(low, swapped, zero), jnp.where(low, zero, x)))


def _store_split_kv(k, v, ks_ref, vs_ref, rows):
    ones = jnp.ones_like(v)
    for kv, (k_sides, v_sides, one_sides) in enumerate(zip(_split_heads(k), _split_heads(v), _split_heads(ones))):
        for side in range(2):
            ks_ref[kv, side, rows, :] = k_sides[side].astype(_BF16)
            vs_ref[kv, side, rows, :] = jnp.concatenate([v_sides[side], one_sides[side]], axis=-1).astype(_BF16)


def _pair_softmax_av(qp, keys, values, masks, sink_even, sink_odd):
    m = qp.shape[0]
    scores = []
    for (k_left, k_right), mask in zip(keys, masks):
        s_even, s_odd = _dot_nt(qp, k_left), _dot_nt(qp, k_right)
        if mask is not None:
            s_even, s_odd = jnp.where(mask, s_even, NEG), jnp.where(mask, s_odd, NEG)
        scores.append((s_even, s_odd))
    m_even = jnp.full((m, 1), sink_even, _F32)
    m_odd = jnp.full((m, 1), sink_odd, _F32)
    for s_even, s_odd in scores:
        m_even = jnp.maximum(m_even, jnp.max(s_even, axis=-1, keepdims=True))
        m_odd = jnp.maximum(m_odd, jnp.max(s_odd, axis=-1, keepdims=True))
    res = None
    for (s_even, s_odd), (w_left, w_right) in zip(scores, values):
        r = (_dot(jnp.exp(s_even - m_even).astype(_BF16), w_left)
             + _dot(jnp.exp(s_odd - m_odd).astype(_BF16), w_right))
        res = r if res is None else res + r
    pair = 2 * HEAD_DIM
    low = lax.broadcasted_iota(jnp.int32, (m, pair), 1) < HEAD_DIM
    den = res[:, pair:] + jnp.where(low, jnp.exp(sink_even - m_even), jnp.exp(sink_odd - m_odd))
    return res[:, :pair] / den


def _attention_ctx(T, layer, qr_ref, ks_ref, vs_ref, sink_ref, oa_ref):
    pair = 2 * HEAD_DIM
    for kv in range(ATT_KV_HEADS):
        keys = [(ks_ref[kv, 0], ks_ref[kv, 1])]
        values = [(vs_ref[kv, 0], vs_ref[kv, 1])]
        for j in range(ATT_GROUP // 2):
            head = kv * ATT_GROUP + 2 * j
            cols = slice(head * HEAD_DIM, head * HEAD_DIM + pair)
            o = _pair_softmax_av(qr_ref[:, cols], keys, values, [None], sink_ref[layer, head],
                                 sink_ref[layer, head + 1])
            oa_ref[0, :, cols] = o.astype(_BF16)


def _attention_latent(T, layer, qr_ref, ks_ref, vs_ref, kc_ref, vc_ref, sink_ref, oa_ref):
    pair = 2 * HEAD_DIM
    span = 3 * ATT_BLOCK
    kc, vc = kc_ref[0, 0], vc_ref[0, 0]
    ones = jnp.ones_like(vc)
    ctx_keys = [tuple(side.astype(_BF16) for side in sides) for sides in _split_heads(kc)]
    ctx_values = [tuple(jnp.concatenate([v_side, one_side], axis=-1).astype(_BF16)
                        for v_side, one_side in zip(v_sides, one_sides))
                  for v_sides, one_sides in zip(_split_heads(vc), _split_heads(ones))]

    def block(i, carry):
        q_rows = pl.ds(pl.multiple_of(i * ATT_BLOCK, ATT_BLOCK), ATT_BLOCK)
        k_rows = pl.ds(pl.multiple_of(i * ATT_BLOCK, ATT_BLOCK), span)
        q_pos = i * ATT_BLOCK + lax.broadcasted_iota(jnp.int32, (ATT_BLOCK, span), 0)
        k_pos = (i - 1) * ATT_BLOCK + lax.broadcasted_iota(jnp.int32, (ATT_BLOCK, span), 1)
        valid = (jnp.abs(k_pos - q_pos) <= WINDOW) & (k_pos >= 0) & (k_pos < T)
        for kv in range(ATT_KV_HEADS):
            keys = [(ks_ref[kv, 0, k_rows, :], ks_ref[kv, 1, k_rows, :]), ctx_keys[kv]]
            values = [(vs_ref[kv, 0, k_rows, :], vs_ref[kv, 1, k_rows, :]), ctx_values[kv]]
            for j in range(ATT_GROUP // 2):
                head = kv * ATT_GROUP + 2 * j
                cols = slice(head * HEAD_DIM, head * HEAD_DIM + pair)
                o = _pair_softmax_av(qr_ref[q_rows, cols], keys, values, [valid, None], sink_ref[layer, head],
                                     sink_ref[layer, head + 1])
                oa_ref[0, q_rows, cols] = o.astype(_BF16)
        return carry

    lax.fori_loop(0, T // ATT_BLOCK, block, 0)


def _gla_chunk(z_ref, la_ref, o_ref, st_ref, a_ref, direction, start):
    C = GLA_CHUNK
    rows = pl.ds(pl.multiple_of(start, C), C)
    q = z_ref[rows, C_QB:C_QB + GLA_QK] * (GLA_DK ** -0.5)
    k = z_ref[rows, C_KB:C_KB + GLA_QK]
    v = z_ref[rows, C_VB:C_VB + GLA_VW].astype(_BF16)
    la = la_ref[rows, direction * GLA_QK:(direction + 1) * GLA_QK]
    la_hi = la.astype(_BF16)
    la_lo = (la - la_hi.astype(_F32)).astype(_BF16)
    ti = lax.broadcasted_iota(jnp.int32, (C, C), 0)
    si = lax.broadcasted_iota(jnp.int32, (C, C), 1)
    causal = (si <= ti) if direction == 0 else (si >= ti)
    tri = jnp.where(causal, 1.0, 0.0).astype(_BF16)
    b = _dot(tri, la_hi) + _dot(tri, la_lo)
    end = C - 1 if direction == 0 else 0
    b_end = b[end:end + 1]
    q_in = (q * jnp.exp(b)).astype(_BF16)
    k_out = (k * jnp.exp(b_end - b)).astype(_BF16)
    e_end = jnp.exp(b_end)
    ones = jnp.ones((8, GLA_DK), _BF16)

    def row_group(g, carry):
        base = pl.multiple_of(g * 8, 8)
        b8 = a_ref[GLA_HEADS, pl.ds(base, 8), :]
        q8 = a_ref[GLA_HEADS + 1, pl.ds(base, 8), :]
        s_idx = lax.broadcasted_iota(jnp.int32, (C, 1), 0)
        rows_h = [[] for _ in range(GLA_HEADS)]
        for j in range(8):
            ok = (s_idx <= base + j) if direction == 0 else (s_idx >= base + j)
            decay = jnp.exp(jnp.where(ok, b8[j:j + 1] - b, NEG))
            p = (q8[j:j + 1] * k * decay).astype(_BF16)
            for h in range(GLA_HEADS):
                rows_h[h].append(_dot_nt(ones, p[:, h * GLA_DK:(h + 1) * GLA_DK])[0:1])
        for h in range(GLA_HEADS):
            a_ref[h, pl.ds(base, 8), 0:C] = jnp.concatenate(rows_h[h], axis=0)
        return carry

    a_ref[GLA_HEADS] = b
    a_ref[GLA_HEADS + 1] = q
    lax.fori_loop(0, C // 8, row_group, 0)
    outs = []
    for pair in range(GLA_HEADS // 2):
        s_pair = st_ref[direction, pair]
        s_next = []
        for h in (2 * pair, 2 * pair + 1):
            kc = slice(h * GLA_DK, (h + 1) * GLA_DK)
            vc = slice(h * GLA_DV, (h + 1) * GLA_DV)
            s_t = s_pair[:, (h % 2) * GLA_DK:(h % 2 + 1) * GLA_DK]
            outs.append(_dot_nt(q_in[:, kc], s_t.astype(_BF16)) + _dot(a_ref[h, :, 0:C].astype(_BF16), v[:, vc]))
            s_next.append(s_t * e_end[:, kc] + _dot_tn(v[:, vc], k_out[:, kc]))
        st_ref[direction, pair] = jnp.concatenate(s_next, axis=-1)
    o_ref[rows, :] += jnp.concatenate(outs, axis=-1)


def _gla_block(z_ref, la_ref, o_ref, st_ref, direction, start, use_state):
    NB = GLA_BLOCK
    fwd = direction == 0
    rows = slice(start, start + NB) if isinstance(start, int) else pl.ds(pl.multiple_of(start, NB), NB)
    q = z_ref[rows, C_QB:C_QB + GLA_QK] * (GLA_DK ** -0.5)
    k = z_ref[rows, C_KB:C_KB + GLA_QK]
    v = z_ref[rows, C_VB:C_VB + GLA_VW].astype(_BF16)
    b = la_ref[rows, direction * GLA_QK:(direction + 1) * GLA_QK]
    end, mid = (NB - 1, NB // 2 - 1) if fwd else (0, NB // 2)
    b_end = b[end:end + 1]
    c = b - b[mid:mid + 1]
    q_c = (q * jnp.exp(c)).astype(_BF16)
    k_c = (k * jnp.exp(-c)).astype(_BF16)
    k_fin = (k * jnp.exp(b_end - b)).astype(_BF16)
    if use_state:
        q_in = (q * jnp.exp(b)).astype(_BF16)
        e_all = jnp.exp(b_end)
    ti = lax.broadcasted_iota(jnp.int32, (NB, NB), 0)
    si = lax.broadcasted_iota(jnp.int32, (NB, NB), 1)
    causal = (si <= ti) if fwd else (si >= ti)
    tile = 2 * GLA_DK
    low = lax.broadcasted_iota(jnp.int32, (NB, tile), 1) < GLA_DK
    zero = jnp.zeros((NB, tile), _BF16)
    pick = lambda x, parity: jnp.where(low, x, zero) if parity == 0 else jnp.where(low, zero, x)
    outs = []
    for pair in range(GLA_HEADS // 2):
        lanes = slice(pair * tile, (pair + 1) * tile)
        if use_state:
            s_pair = st_ref[direction, pair]
            s_bf = s_pair.astype(_BF16)
        s_new = None
        for parity in range(2):
            h = 2 * pair + parity
            vc = slice(h * GLA_DV, (h + 1) * GLA_DV)
            a = jnp.where(causal, _dot_nt(q_c[:, lanes], pick(k_c[:, lanes], parity)), 0.0)
            o_h = _dot(a.astype(_BF16), v[:, vc])
            if use_state:
                o_h = o_h + _dot_nt(pick(q_in[:, lanes], parity), s_bf)
            outs.append(o_h)
            upd = _dot_tn(v[:, vc], pick(k_fin[:, lanes], parity))
            s_new = upd if s_new is None else s_new + upd
        if use_state:
            s_new = s_new + s_pair * e_all[:, lanes]
        st_ref[direction, pair] = s_new
    o_ref[rows, :] += jnp.concatenate(outs, axis=-1)


def _log_decay(z_ref, rows, wg2_ref, bg2_ref):
    pre = lax.dot_general(z_ref[rows, C_GL:C_GL + GL_PAD], wg2_ref[...], (((1,), (0,)), ((), ())),
                          precision=lax.Precision.HIGHEST, preferred_element_type=_F32) + bg2_ref[...]
    return _log_sigmoid(pre) * (1.0 / GLA_TAU)


def _gla(T, z_ref, la_ref, o_ref, st_ref, a_ref, wg2_ref, bg2_ref, use_state):
    NB = GLA_BLOCK
    n_chunks = T // GLA_CHUNK
    n_blocks = T // NB
    ti = lax.broadcasted_iota(jnp.int32, (NB, NB), 0)
    si = lax.broadcasted_iota(jnp.int32, (NB, NB), 1)
    tri = [jnp.where(si <= ti, 1.0, 0.0).astype(_BF16), jnp.where(si >= ti, 1.0, 0.0).astype(_BF16)]
    worst = None
    for r0 in range(0, T, NB):
        rows = slice(r0, r0 + NB)
        la = _log_decay(z_ref, rows, wg2_ref, bg2_ref)
        la_hi = la.astype(_BF16)
        la_lo = (la - la_hi.astype(_F32)).astype(_BF16)
        for d in range(2):
            cols = slice(d * GLA_QK, (d + 1) * GLA_QK)
            b = _dot(tri[d], la_hi[:, cols]) + _dot(tri[d], la_lo[:, cols])
            la_ref[rows, cols] = b
            first, mid, last = (0, NB // 2 - 1, NB - 1) if d == 0 else (NB - 1, NB // 2, 0)
            span = jnp.max(jnp.maximum(b[first:first + 1] - b[mid:mid + 1], b[mid:mid + 1] - b[last:last + 1]))
            worst = span if worst is None else jnp.maximum(worst, span)
    fast_ok = worst <= GLA_MAX_EXPONENT
    o_ref[...] = jnp.zeros(o_ref.shape, _F32)

    @pl.when(fast_ok)
    def _():
        if n_blocks == 1:
            for direction in range(2):
                _gla_block(z_ref, la_ref, o_ref, st_ref, direction, 0, use_state)
        else:
            def body(i, carry):
                _gla_block(z_ref, la_ref, o_ref, st_ref, 0, i * NB, True)
                _gla_block(z_ref, la_ref, o_ref, st_ref, 1, (n_blocks - 1 - i) * NB, True)
                return carry
            lax.fori_loop(0, n_blocks, body, 0)

    @pl.when(jnp.logical_not(fast_ok))
    def _():
        for r0 in range(0, T, NB):
            la_ref[r0:r0 + NB, :] = _log_decay(z_ref, slice(r0, r0 + NB), wg2_ref, bg2_ref)

        def body(i, carry):
            _gla_chunk(z_ref, la_ref, o_ref, st_ref, a_ref, 0, i * GLA_CHUNK)
            _gla_chunk(z_ref, la_ref, o_ref, st_ref, a_ref, 1, (n_chunks - 1 - i) * GLA_CHUNK)
            return carry
        lax.fori_loop(0, n_chunks, body, 0)


def _gla_finish(T, z_ref, o_ref, ggla_ref, vones_ref, ob_ref):
    for r0 in range(0, T, PROJ_TILE):
        rows = slice(r0, r0 + PROJ_TILE)
        o = o_ref[rows, :]
        y = o * _group_rms_scale(o, vones_ref[...], GLA_DV) * ggla_ref[...]
        ob_ref[0, rows, :] = (y * _silu(z_ref[rows, C_RB:C_RB + GLA_VW])).astype(_BF16)


def _pool(T, z_ref, upad_ref, wpool_ref, pscale_ref, oc_ref):
    u = z_ref[:, C_UC:C_UC + POOL_W]
    u_hi = u.astype(_BF16)
    u_lo = (u - u_hi.astype(_F32)).astype(_BF16)
    zeros = jnp.zeros((POOL_HALO, POOL_W), _BF16)
    for part, val in ((0, u_hi), (1, u_lo)):
        upad_ref[part, 0:POOL_HALO, :] = zeros
        upad_ref[part, POOL_HALO + T:POOL_HALO + T + POOL_HALO, :] = zeros
        upad_ref[part, POOL_HALO:POOL_HALO + T, :] = val
    span = POOL_TILE + 2 * POOL_HALO
    r = lax.broadcasted_iota(jnp.int32, (POOL_TILE, span), 0)
    c = lax.broadcasted_iota(jnp.int32, (POOL_TILE, span), 1)
    off = c - POOL_HALO - r
    for jb in range(T // POOL_TILE):
        t = jb * POOL_TILE + lax.broadcasted_iota(jnp.int32, (POOL_TILE, 1), 0)
        parts = []
        for g, w in enumerate(POOL_WINDOWS):
            cols = slice(g * POOL_GROUP_DIM, (g + 1) * POOL_GROUP_DIM)
            band = jnp.where((off >= -(w // 2)) & (off < w - w // 2), 1.0, 0.0).astype(_BF16)
            win = slice(jb * POOL_TILE, jb * POOL_TILE + span)
            total = _dot(band, upad_ref[0, win, cols]) + _dot(band, upad_ref[1, win, cols])
            cnt = (jnp.minimum(t - w // 2 + w, T) - jnp.maximum(t - w // 2, 0)).astype(_F32)
            pooled = total / cnt - z_ref[jb * POOL_TILE:(jb + 1) * POOL_TILE, C_UC + g * POOL_GROUP_DIM:
                                         C_UC + (g + 1) * POOL_GROUP_DIM]
            parts.append(_dot(pooled.astype(_BF16), wpool_ref[g]))
        y = jnp.concatenate(parts, axis=-1) * pscale_ref[...]
        oc_ref[0, jb * POOL_TILE:(jb + 1) * POOL_TILE, :] = y.astype(_BF16)


def _mix_kernel(latent, T, layer_ref, *refs):
    (x_ref, mod_ref, gn1_ref, wmix_ref, gqn_ref, gkn_ref, sink_ref, wg2_ref, bg2_ref, ggla_ref,
     wpool_ref, pscale_ref, hones_ref, vones_ref) = refs[:14]
    refs = refs[14:]
    if latent:
        kc_ref, vc_ref, st0_ref, cos_ref, sin_ref, oa_ref, ob_ref, oc_ref = refs[:8]
        refs = refs[8:]
    else:
        oa_ref, ob_ref, oc_ref, kout_ref, vout_ref, stout_ref = refs[3:9]
        refs = refs[9:]
    z_ref, qr_ref, ks_ref, vs_ref, la_ref, o_ref, st_ref, a_ref, upad_ref = refs
    layer = layer_ref[0]
    pad = ATT_BLOCK if latent else 0

    if latent:
        for ref in (ks_ref, vs_ref):
            zeros = jnp.zeros(ref.shape[:2] + (pad, ref.shape[3]), _BF16)
            ref[:, :, 0:pad, :] = zeros
            ref[:, :, pad + T:pad + T + pad, :] = zeros
    shift = mod_ref[0, :, 0:D_MODEL]
    scale = mod_ref[0, :, D_MODEL:2 * D_MODEL]
    for r0 in range(0, T, PROJ_TILE):
        rows = slice(r0, r0 + PROJ_TILE)
        x = x_ref[0, rows, :]
        hn = (x * _rms_scale(x) * gn1_ref[...]) * (1.0 + scale) + shift
        z_ref[rows, :] = _dot(hn.astype(_BF16), wmix_ref[...])
        q = z_ref[rows, C_QA:C_QA + ATT_Q]
        k = z_ref[rows, C_KA:C_KA + ATT_KV]
        q = q * _group_rms_scale(q, hones_ref[...], HEAD_DIM) * gqn_ref[...]
        k = k * _group_rms_scale(k, hones_ref[0:ATT_KV, 0:ATT_KV], HEAD_DIM) * gkn_ref[...]
        v = z_ref[rows, C_VA:C_VA + ATT_KV]
        if latent:
            cos = jnp.concatenate([cos_ref[rows, :]] * (ATT_Q // ATT_KV), axis=-1)
            sin = jnp.concatenate([sin_ref[rows, :]] * (ATT_Q // ATT_KV), axis=-1)
            q = _rope(q, cos, sin)
            k = _rope(k, cos_ref[rows, :], sin_ref[rows, :])
        else:
            kout_ref[0, rows, :] = k
            vout_ref[0, rows, :] = v
        qr_ref[rows, :] = (q * (HEAD_DIM ** -0.5)).astype(_BF16)
        _store_split_kv(k, v, ks_ref, vs_ref, slice(pad + r0, pad + r0 + PROJ_TILE))
    if latent:
        _attention_latent(T, layer, qr_ref, ks_ref, vs_ref, kc_ref, vc_ref, sink_ref, oa_ref)
    else:
        _attention_ctx(T, layer, qr_ref, ks_ref, vs_ref, sink_ref, oa_ref)

    if latent:
        st_ref[...] = st0_ref[0]
    else:
        st_ref[...] = jnp.zeros(st_ref.shape, _F32)
    _gla(T, z_ref, la_ref, o_ref, st_ref, a_ref, wg2_ref, bg2_ref, latent)
    _gla_finish(T, z_ref, o_ref, ggla_ref, vones_ref, ob_ref)
    if not latent:
        for d in range(2):
            for pair in range(GLA_HEADS // 2):
                stout_ref[0, d, pair] = st_ref[d, pair].T

    _pool(T, z_ref, upad_ref, wpool_ref, pscale_ref, oc_ref)


def _layer_spec(shape):
    zeros = (0,) * len(shape)
    return pl.BlockSpec((None,) + tuple(shape), lambda i, layer: (layer[0],) + zeros,
                        pipeline_mode=pl.Buffered(1))


def _mix_call(latent, layer, x, mod_all, pw, extra):
    B, T, _ = x.shape
    per_seq = lambda blk: pl.BlockSpec(blk, lambda b, layer: (b,) + (0,) * (len(blk) - 1))
    const_spec = lambda blk: pl.BlockSpec(blk, lambda b, layer: (0,) * len(blk), pipeline_mode=pl.Buffered(1))
    if latent:
        mod_spec = pl.BlockSpec((None, 1, 1, 6 * D_MODEL), lambda b, layer: (layer[0], b + 1, 0, 0))
    else:
        mod_spec = pl.BlockSpec((None, 1, 1, 6 * D_MODEL), lambda b, layer: (layer[0], 0, 0, 0))
    x_spec = pl.BlockSpec((1, T, D_MODEL), lambda b, layer: (b, 0, 0),
                          pipeline_mode=pl.Buffered(1) if latent else None)
    in_specs = [
        x_spec, mod_spec,
        _layer_spec((1, D_MODEL)), _layer_spec((D_MODEL, MIX_W)),
        _layer_spec((1, ATT_Q)), _layer_spec((1, ATT_KV)),
        pl.BlockSpec(memory_space=pltpu.SMEM),
        _layer_spec((GL_PAD, 2 * GLA_QK)), _layer_spec((1, 2 * GLA_QK)), _layer_spec((1, GLA_VW)),
        _layer_spec((POOL_GROUPS, POOL_GROUP_DIM, POOL_GROUP_DIM)), _layer_spec((1, POOL_W)),
        const_spec((ATT_Q, ATT_Q)), const_spec((GLA_VW, GLA_VW)),
    ]
    args = [x, mod_all, pw["g_norm1"], pw["w_mix"], pw["g_qn"], pw["g_kn"], pw["att_sink"], pw["w_gate2"],
            pw["b_gate2"], pw["g_gla_out"], pw["w_pool"], pw["pool_scale"], pw["head_ones"], pw["gla_ones"]]
    act = lambda width: jax.ShapeDtypeStruct((B, T, width), _BF16)
    out_shape = [act(ATT_Q), act(GLA_VW), act(POOL_W)]
    out_specs = [per_seq((1, T, ATT_Q)), per_seq((1, T, GLA_VW)), per_seq((1, T, POOL_W))]
    aliases = {}
    if latent:
        cache_k, cache_v, st0, cos, sin = extra
        P = cache_k.shape[2]
        cache_spec = pl.BlockSpec((1, 1, P, ATT_KV), lambda b, layer: (b, layer[0], 0, 0))
        table_spec = const_spec((T, ATT_KV))
        in_specs += [cache_spec, cache_spec,
                     pl.BlockSpec((1, None, 2, GLA_HEADS // 2, GLA_DV, 2 * GLA_DK),
                                  lambda b, layer: (b, layer[0], 0, 0, 0, 0)),
                     table_spec, table_spec]
        args += [cache_k, cache_v, st0, cos, sin]
    else:
        n_in = 1 + len(args)
        aliases = {n_in + i: len(out_shape) + i for i in range(len(extra))}
        in_specs += [pl.BlockSpec(memory_space=pl.ANY)] * len(extra)
        args += list(extra)
        out_shape += [jax.ShapeDtypeStruct(a.shape, a.dtype) for a in extra]
        at_layer = lambda blk: pl.BlockSpec((1, None) + blk, lambda b, layer: (b, layer[0]) + (0,) * len(blk))
        out_specs += [at_layer((T, ATT_KV)), at_layer((T, ATT_KV)),
                      at_layer((2, GLA_HEADS // 2, 2 * GLA_DK, GLA_DV))]
    kv_rows = T + 2 * ATT_BLOCK if latent else T
    scratch = [
        pltpu.VMEM((T, MIX_W), _F32),
        pltpu.VMEM((T, ATT_Q), _BF16),
        pltpu.VMEM((ATT_KV_HEADS, 2, kv_rows, 2 * HEAD_DIM), _BF16),
        pltpu.VMEM((ATT_KV_HEADS, 2, kv_rows, 4 * HEAD_DIM), _BF16),
        pltpu.VMEM((T, 2 * GLA_QK), _F32),
        pltpu.VMEM((T, GLA_VW), _F32),
        pltpu.VMEM((2, GLA_HEADS // 2, GLA_DV, 2 * GLA_DK), _F32),
        pltpu.VMEM((GLA_HEADS + 2, GLA_CHUNK, GLA_QK), _F32),
        pltpu.VMEM((2, T + 2 * POOL_HALO, POOL_W), _BF16),
    ]
    return pl.pallas_call(
        functools.partial(_mix_kernel, latent, T),
        grid_spec=pltpu.PrefetchScalarGridSpec(
            num_scalar_prefetch=1, grid=(B,), in_specs=in_specs, out_specs=out_specs, scratch_shapes=scratch),
        out_shape=out_shape,
        input_output_aliases=aliases,
        compiler_params=pltpu.CompilerParams(dimension_semantics=("arbitrary",), vmem_limit_bytes=VMEM_LIMIT),
        name="mix_latent" if latent else "mix_ctx",
    )(layer, *args)


def _post_kernel(layer_ref, x_ref, mod_ref, oa_ref, ob_ref, oc_ref, gn1_ref, gn2_ref, wgate_ref, wa_ref, wb_ref,
                 wc_ref, wout_ref, wfg_ref, wfu_ref, wfd_ref, out_ref):
    x = x_ref[...]
    mod = lambda i: mod_ref[0, :, i * D_MODEL:(i + 1) * D_MODEL]
    hn = (x * _rms_scale(x) * gn1_ref[...]) * (1.0 + mod(1)) + mod(0)
    gates = jax.nn.sigmoid(_dot(hn.astype(_BF16), wgate_ref[...]))
    mixed = (gates[:, 0:D_MODEL] * _dot(oa_ref[...], wa_ref[...])
             + gates[:, D_MODEL:2 * D_MODEL] * _dot(ob_ref[...], wb_ref[...])
             + gates[:, 2 * D_MODEL:3 * D_MODEL] * _dot(oc_ref[...], wc_ref[...]))
    x = x + mod(2) * _dot(mixed.astype(_BF16), wout_ref[...])
    hn = ((x * _rms_scale(x) * gn2_ref[...]) * (1.0 + mod(4)) + mod(3)).astype(_BF16)
    h = _silu(_dot(hn, wfg_ref[...])) * _dot(hn, wfu_ref[...])
    out_ref[...] = x + mod(5) * _dot(h.astype(_BF16), wfd_ref[...])


def _post_call(layer, x2d, mod_all, oa, ob, oc, pw, tiles_per_seq, shared_mod):
    n = x2d.shape[0]
    row = lambda w: pl.BlockSpec((POST_TILE, w), lambda i, layer: (i, 0))
    if shared_mod:
        mod_spec = pl.BlockSpec((None, 1, 1, 6 * D_MODEL), lambda i, layer: (layer[0], 0, 0, 0))
    else:
        mod_spec = pl.BlockSpec((None, 1, 1, 6 * D_MODEL),
                                lambda i, layer: (layer[0], 1 + i // tiles_per_seq, 0, 0))
    in_specs = [row(D_MODEL), mod_spec, row(ATT_Q), row(GLA_VW), row(POOL_W),
                _layer_spec((1, D_MODEL)), _layer_spec((1, D_MODEL)), _layer_spec((D_MODEL, GATE_W)),
                _layer_spec((ATT_Q, D_MODEL)), _layer_spec((GLA_VW, D_MODEL)), _layer_spec((POOL_W, D_MODEL)),
                _layer_spec((D_MODEL, D_MODEL)), _layer_spec((D_MODEL, D_FF)), _layer_spec((D_MODEL, D_FF)),
                _layer_spec((D_FF, D_MODEL))]
    return pl.pallas_call(
        _post_kernel,
        grid_spec=pltpu.PrefetchScalarGridSpec(
            num_scalar_prefetch=1, grid=(n // POST_TILE,), in_specs=in_specs, out_specs=row(D_MODEL)),
        out_shape=jax.ShapeDtypeStruct(x2d.shape, _F32),
        input_output_aliases={1: 0},
        compiler_params=pltpu.CompilerParams(dimension_semantics=("arbitrary",), vmem_limit_bytes=VMEM_LIMIT),
        name="post",
    )(layer, x2d, mod_all, oa, ob, oc, pw["g_norm1"], pw["g_norm2"], pw["w_gates"], pw["w_br_a"], pw["w_br_b"],
      pw["w_br_c"], pw["w_out"], pw["w_ff_gate"], pw["w_ff_up"], pw["w_ff_down"])


def _rope_tables(T):
    quarter = HEAD_DIM // 4
    inv_freq = ROPE_BASE ** (-np.arange(quarter, dtype=np.float32) / quarter)
    pos = np.arange(T)
    ang_row = (pos // GRID_W).astype(np.float32)[:, None] * inv_freq[None, :]
    ang_col = (pos % GRID_W).astype(np.float32)[:, None] * inv_freq[None, :]
    cos = np.concatenate([np.cos(ang_row)] * 2 + [np.cos(ang_col)] * 2, axis=-1)
    sin = np.concatenate([-np.sin(ang_row), np.sin(ang_row), -np.sin(ang_col), np.sin(ang_col)], axis=-1)
    return (jnp.asarray(np.tile(cos, (1, ATT_KV_HEADS)), _F32), jnp.asarray(np.tile(sin, (1, ATT_KV_HEADS)), _F32))


def _prepare_weights(w_in, g_qn, g_kn, att_sink, w_gate2, b_gate2, g_gla_out, w_pool, pool_scale, w_br_a,
                     w_br_b, w_br_c, w_out, g_norm1, g_norm2, w_ff_gate, w_ff_up, w_ff_down):
    o_gl = ATT_Q + 2 * ATT_KV + 2 * GLA_QK + 2 * GLA_VW
    o_uc = o_gl + 2 * GLA_RANK
    o_gate = o_uc + POOL_W
    w_in = w_in.astype(_BF16)
    w_mix = jnp.concatenate(
        [w_in[:, :, :o_gl], w_in[:, :, o_uc:o_gate], w_in[:, :, o_gl:o_uc],
         jnp.zeros((DEPTH, D_MODEL, GL_PAD - 2 * GLA_RANK), w_in.dtype)], axis=2)
    wg2 = jnp.zeros((DEPTH, GL_PAD, 2 * GLA_QK), _F32)
    wg2 = wg2.at[:, 0:GLA_RANK, 0:GLA_QK].set(w_gate2[:, 0])
    wg2 = wg2.at[:, GLA_RANK:2 * GLA_RANK, GLA_QK:].set(w_gate2[:, 1])
    vec = lambda a: a.reshape(DEPTH, 1, -1)
    group_ones = lambda n, width: jnp.asarray(
        (np.arange(n)[:, None] // width) == (np.arange(n)[None, :] // width), _BF16)
    return {
        "head_ones": group_ones(ATT_Q, HEAD_DIM),
        "gla_ones": group_ones(GLA_VW, GLA_DV),
        "w_mix": w_mix,
        "w_gates": w_in[:, :, o_gate:],
        "g_qn": vec(jnp.tile(g_qn, (1, ATT_HEADS))),
        "g_kn": vec(jnp.tile(g_kn, (1, ATT_KV_HEADS))),
        "att_sink": att_sink,
        "w_gate2": wg2,
        "b_gate2": vec(b_gate2),
        "g_gla_out": vec(jnp.tile(g_gla_out, (1, GLA_HEADS))),
        "w_pool": w_pool.astype(_BF16),
        "pool_scale": vec(pool_scale),
        "w_br_a": w_br_a.astype(_BF16),
        "w_br_b": w_br_b.astype(_BF16),
        "w_br_c": w_br_c.astype(_BF16),
        "w_out": w_out.astype(_BF16),
        "g_norm1": vec(g_norm1),
        "g_norm2": vec(g_norm2),
        "w_ff_gate": w_ff_gate.astype(_BF16),
        "w_ff_up": w_ff_up.astype(_BF16),
        "w_ff_down": w_ff_down.astype(_BF16),
    }


def kernel(x_prompt, x_sample, c, cache_k, cache_v, state_gla, c_ctx, w_in, g_qn, g_kn, att_sink, w_gate2,
           b_gate2, g_gla_out, w_pool, pool_scale, w_br_a, w_br_b, w_br_c, w_out, g_norm1, g_norm2, w_mod,
           b_mod, w_ff_gate, w_ff_up, w_ff_down):
    B, T, _ = x_prompt.shape
    BL, TL, _ = x_sample.shape
    assert T % POST_TILE == 0 and TL % POST_TILE == 0 and BL + 1 <= MOD_ROWS
    cv = jnp.concatenate([c_ctx[None, :], c, jnp.zeros((MOD_ROWS - 1 - BL, D_MODEL), _F32)], axis=0)
    mod_all = _modulation(cv, w_mod, b_mod).reshape(DEPTH, MOD_ROWS, 1, 6 * D_MODEL)
    pw = _prepare_weights(w_in, g_qn, g_kn, att_sink, w_gate2, b_gate2, g_gla_out, w_pool, pool_scale, w_br_a,
                          w_br_b, w_br_c, w_out, g_norm1, g_norm2, w_ff_gate, w_ff_up, w_ff_down)
    cos, sin = _rope_tables(TL)
    P = cache_k.shape[2]
    latent_ctx = (cache_k.reshape(BL, DEPTH, P, ATT_KV), cache_v.reshape(BL, DEPTH, P, ATT_KV),
                  jnp.swapaxes(state_gla.reshape(BL, DEPTH, 2, GLA_HEADS // 2, 2 * GLA_DK, GLA_DV), -1, -2),
                  cos, sin)

    def layer_step(l, carry):
        yp, ys, new_k, new_v, new_st = carry
        layer = jnp.full((1,), l, jnp.int32)
        oa, ob, oc, new_k, new_v, new_st = _mix_call(False, layer, yp, mod_all, pw, (new_k, new_v, new_st))
        yp = _post_call(layer, yp.reshape(B * T, D_MODEL), mod_all, oa.reshape(B * T, -1), ob.reshape(B * T, -1),
                        oc.reshape(B * T, -1), pw, T // POST_TILE, True).reshape(B, T, D_MODEL)
        oa, ob, oc = _mix_call(True, layer, ys, mod_all, pw, latent_ctx)
        ys = _post_call(layer, ys.reshape(BL * TL, D_MODEL), mod_all, oa.reshape(BL * TL, -1),
                        ob.reshape(BL * TL, -1), oc.reshape(BL * TL, -1), pw, TL // POST_TILE,
                        False).reshape(BL, TL, D_MODEL)
        return yp, ys, new_k, new_v, new_st

    init = (x_prompt, x_sample,
            jnp.zeros((B, DEPTH, T, ATT_KV), _F32), jnp.zeros((B, DEPTH, T, ATT_KV), _F32),
            jnp.zeros((B, DEPTH, 2, GLA_HEADS // 2, 2 * GLA_DK, GLA_DV), _F32))
    yp, ys, new_k, new_v, new_st = lax.fori_loop(0, DEPTH, layer_step, init)
    return (yp, ys, new_k.reshape(B, DEPTH, T, ATT_KV_HEADS, HEAD_DIM),
            new_v.reshape(B, DEPTH, T, ATT_KV_HEADS, HEAD_DIM),
            new_st.reshape(B, DEPTH, 2, GLA_HEADS, GLA_DK, GLA_DV))
```

```python
import functools

import jax
import jax.numpy as jnp
import numpy as np
from jax import lax
from jax.experimental import pallas as pl
from jax.experimental.pallas import tpu as pltpu

D_MODEL = 1024
DEPTH = 4
GRID_W = 64
ATT_HEADS = 8
ATT_KV_HEADS = 2
ATT_GROUP = ATT_HEADS // ATT_KV_HEADS
HEAD_DIM = 64
WINDOW = 128
ATT_BLOCK = 128
ROPE_BASE = 10000.0
GLA_HEADS = 4
GLA_DK = 64
GLA_DV = 128
GLA_RANK = 16
GLA_TAU = 16.0
GLA_CHUNK = 64
POOL_GROUPS = 4
POOL_GROUP_DIM = 128
POOL_WINDOWS = (2, 4, 8, 16)
D_FF = 2816
ATT_Q = ATT_HEADS * HEAD_DIM
ATT_KV = ATT_KV_HEADS * HEAD_DIM
GLA_QK = GLA_HEADS * GLA_DK
GLA_VW = GLA_HEADS * GLA_DV
POOL_W = POOL_GROUPS * POOL_GROUP_DIM
EPS = 1e-6
NEG = -1e30

C_QA = 0
C_KA = C_QA + ATT_Q
C_VA = C_KA + ATT_KV
C_QB = C_VA + ATT_KV
C_KB = C_QB + GLA_QK
C_VB = C_KB + GLA_QK
C_RB = C_VB + GLA_VW
C_UC = C_RB + GLA_VW
C_GL = C_UC + POOL_W
GL_PAD = 128
MIX_W = C_GL + GL_PAD
GATE_W = 3 * D_MODEL

POST_TILE = 256
PROJ_TILE = 256
POOL_TILE = 256
POOL_HALO = 128
MOD_ROWS = 8
MOD_TILE = 1024
GLA_BLOCK = 256
GLA_MAX_EXPONENT = 80.0
VMEM_LIMIT = 56 * 1024 * 1024

_F32 = jnp.float32
_BF16 = jnp.bfloat16


def _dot(a, b):
    return jnp.dot(a, b, preferred_element_type=_F32)


def _dot_nt(a, b):
    return lax.dot_general(a, b, (((1,), (1,)), ((), ())), preferred_element_type=_F32)


def _dot_tn(a, b):
    return lax.dot_general(a, b, (((0,), (0,)), ((), ())), preferred_element_type=_F32)


def _rms_scale(x):
    return lax.rsqrt(jnp.mean(x * x, axis=-1, keepdims=True) + EPS)


def _group_rms_scale(x, group_ones, width):
    sq = x * x
    hi = sq.astype(_BF16)
    lo = (sq - hi.astype(_F32)).astype(_BF16)
    return lax.rsqrt((_dot(hi, group_ones) + _dot(lo, group_ones)) * (1.0 / width) + EPS)


def _log_sigmoid(x):
    return jnp.minimum(x, 0.0) - jnp.log1p(jnp.exp(-jnp.abs(x)))


def _silu(x):
    return x * jax.nn.sigmoid(x)


def _rope(x, cos, sin_signed):
    n = x.shape[-1]
    lane = lax.broadcasted_iota(jnp.int32, x.shape, 1)
    up = pltpu.roll(x, n - HEAD_DIM // 4, axis=1)
    down = pltpu.roll(x, HEAD_DIM // 4, axis=1)
    partner = jnp.where((lane & (HEAD_DIM // 2 - 1)) < HEAD_DIM // 4, up, down)
    return x * cos + partner * sin_signed


def _mod_kernel(cv_ref, w_ref, b_ref, out_ref):
    s = _silu(cv_ref[...]).astype(_BF16)
    out_ref[0] = _dot(s, w_ref[0].astype(_BF16)) + b_ref[0]


def _modulation(cv, w_mod, b_mod):
    n_col = (6 * D_MODEL) // MOD_TILE
    return pl.pallas_call(
        _mod_kernel,
        grid=(DEPTH, n_col),
        in_specs=[
            pl.BlockSpec((MOD_ROWS, D_MODEL), lambda l, j: (0, 0)),
            pl.BlockSpec((1, D_MODEL, MOD_TILE), lambda l, j: (l, 0, j)),
            pl.BlockSpec((1, 1, MOD_TILE), lambda l, j: (l, 0, j)),
        ],
        out_specs=pl.BlockSpec((1, MOD_ROWS, MOD_TILE), lambda l, j: (l, 0, j)),
        out_shape=jax.ShapeDtypeStruct((DEPTH, MOD_ROWS, 6 * D_MODEL), _F32),
        name="modulation",
    )(cv, w_mod, b_mod.reshape(DEPTH, 1, 6 * D_MODEL))


def _split_heads(x):
    low = lax.broadcasted_iota(jnp.int32, x.shape, 1) < HEAD_DIM
    swapped = pltpu.roll(x, HEAD_DIM, axis=1)
    zero = jnp.zeros_like(x)
    return ((jnp.where(low, x, zero), jnp.where(low, zero, swapped)),
            (jnp.where(low, swapped, zero), jnp.where(low, zero, x)))


def _store_split_kv(k, v, ks_ref, vs_ref, rows):
    ones = jnp.ones_like(v)
    for kv, (k_sides, v_sides, one_sides) in enumerate(zip(_split_heads(k), _split_heads(v), _split_heads(ones))):
        for side in range(2):
            ks_ref[kv, side, rows, :] = k_sides[side].astype(_BF16)
            vs_ref[kv, side, rows, :] = jnp.concatenate([v_sides[side], one_sides[side]], axis=-1).astype(_BF16)


def _pair_softmax_av(qp, keys, values, masks, sink_even, sink_odd):
    m = qp.shape[0]
    scores = []
    for (k_left, k_right), mask in zip(keys, masks):
        s_even, s_odd = _dot_nt(qp, k_left), _dot_nt(qp, k_right)
        if mask is not None:
            s_even, s_odd = jnp.where(mask, s_even, NEG), jnp.where(mask, s_odd, NEG)
        scores.append((s_even, s_odd))
    m_even = jnp.full((m, 1), sink_even, _F32)
    m_odd = jnp.full((m, 1), sink_odd, _F32)
    for s_even, s_odd in scores:
        m_even = jnp.maximum(m_even, jnp.max(s_even, axis=-1, keepdims=True))
        m_odd = jnp.maximum(m_odd, jnp.max(s_odd, axis=-1, keepdims=True))
    res = None
    for (s_even, s_odd), (w_left, w_right) in zip(scores, values):
        r = (_dot(jnp.exp(s_even - m_even).astype(_BF16), w_left)
             + _dot(jnp.exp(s_odd - m_odd).astype(_BF16), w_right))
        res = r if res is None else res + r
    pair = 2 * HEAD_DIM
    low = lax.broadcasted_iota(jnp.int32, (m, pair), 1) < HEAD_DIM
    den = res[:, pair:] + jnp.where(low, jnp.exp(sink_even - m_even), jnp.exp(sink_odd - m_odd))
    return res[:, :pair] / den


def _attention_ctx(T, layer, qr_ref, ks_ref, vs_ref, sink_ref, oa_ref):
    pair = 2 * HEAD_DIM
    for kv in range(ATT_KV_HEADS):
        keys = [(ks_ref[kv, 0], ks_ref[kv, 1])]
        values = [(vs_ref[kv, 0], vs_ref[kv, 1])]
        for j in range(ATT_GROUP // 2):
            head = kv * ATT_GROUP + 2 * j
            cols = slice(head * HEAD_DIM, head * HEAD_DIM + pair)
            o = _pair_softmax_av(qr_ref[:, cols], keys, values, [None], sink_ref[layer, head],
                                 sink_ref[layer, head + 1])
            oa_ref[0, :, cols] = o.astype(_BF16)


def _attention_latent(T, layer, qr_ref, ks_ref, vs_ref, kc_ref, vc_ref, sink_ref, oa_ref):
    pair = 2 * HEAD_DIM
    span = 3 * ATT_BLOCK
    kc, vc = kc_ref[0, 0], vc_ref[0, 0]
    ones = jnp.ones_like(vc)
    ctx_keys = [tuple(side.astype(_BF16) for side in sides) for sides in _split_heads(kc)]
    ctx_values = [tuple(jnp.concatenate([v_side, one_side], axis=-1).astype(_BF16)
                        for v_side, one_side in zip(v_sides, one_sides))
                  for v_sides, one_sides in zip(_split_heads(vc), _split_heads(ones))]

    def block(i, carry):
        q_rows = pl.ds(pl.multiple_of(i * ATT_BLOCK, ATT_BLOCK), ATT_BLOCK)
        k_rows = pl.ds(pl.multiple_of(i * ATT_BLOCK, ATT_BLOCK), span)
        q_pos = i * ATT_BLOCK + lax.broadcasted_iota(jnp.int32, (ATT_BLOCK, span), 0)
        k_pos = (i - 1) * ATT_BLOCK + lax.broadcasted_iota(jnp.int32, (ATT_BLOCK, span), 1)
        valid = (jnp.abs(k_pos - q_pos) <= WINDOW) & (k_pos >= 0) & (k_pos < T)
        for kv in range(ATT_KV_HEADS):
            keys = [(ks_ref[kv, 0, k_rows, :], ks_ref[kv, 1, k_rows, :]), ctx_keys[kv]]
            values = [(vs_ref[kv, 0, k_rows, :], vs_ref[kv, 1, k_rows, :]), ctx_values[kv]]
            for j in range(ATT_GROUP // 2):
                head = kv * ATT_GROUP + 2 * j
                cols = slice(head * HEAD_DIM, head * HEAD_DIM + pair)
                o = _pair_softmax_av(qr_ref[q_rows, cols], keys, values, [valid, None], sink_ref[layer, head],
                                     sink_ref[layer, head + 1])
                oa_ref[0, q_rows, cols] = o.astype(_BF16)
        return carry

    lax.fori_loop(0, T // ATT_BLOCK, block, 0)


def _gla_chunk(z_ref, la_ref, o_ref, st_ref, a_ref, direction, start):
    C = GLA_CHUNK
    rows = pl.ds(pl.multiple_of(start, C), C)
    q = z_ref[rows, C_QB:C_QB + GLA_QK] * (GLA_DK ** -0.5)
    k = z_ref[rows, C_KB:C_KB + GLA_QK]
    v = z_ref[rows, C_VB:C_VB + GLA_VW].astype(_BF16)
    la = la_ref[rows, direction * GLA_QK:(direction + 1) * GLA_QK]
    la_hi = la.astype(_BF16)
    la_lo = (la - la_hi.astype(_F32)).astype(_BF16)
    ti = lax.broadcasted_iota(jnp.int32, (C, C), 0)
    si = lax.broadcasted_iota(jnp.int32, (C, C), 1)
    causal = (si <= ti) if direction == 0 else (si >= ti)
    tri = jnp.where(causal, 1.0, 0.0).astype(_BF16)
    b = _dot(tri, la_hi) + _dot(tri, la_lo)
    end = C - 1 if direction == 0 else 0
    b_end = b[end:end + 1]
    q_in = (q * jnp.exp(b)).astype(_BF16)
    k_out = (k * jnp.exp(b_end - b)).astype(_BF16)
    e_end = jnp.exp(b_end)
    ones = jnp.ones((8, GLA_DK), _BF16)

    def row_group(g, carry):
        base = pl.multiple_of(g * 8, 8)
        b8 = a_ref[GLA_HEADS, pl.ds(base, 8), :]
        q8 = a_ref[GLA_HEADS + 1, pl.ds(base, 8), :]
        s_idx = lax.broadcasted_iota(jnp.int32, (C, 1), 0)
        rows_h = [[] for _ in range(GLA_HEADS)]
        for j in range(8):
            ok = (s_idx <= base + j) if direction == 0 else (s_idx >= base + j)
            decay = jnp.exp(jnp.where(ok, b8[j:j + 1] - b, NEG))
            p = (q8[j:j + 1] * k * decay).astype(_BF16)
            for h in range(GLA_HEADS):
                rows_h[h].append(_dot_nt(ones, p[:, h * GLA_DK:(h + 1) * GLA_DK])[0:1])
        for h in range(GLA_HEADS):
            a_ref[h, pl.ds(base, 8), 0:C] = jnp.concatenate(rows_h[h], axis=0)
        return carry

    a_ref[GLA_HEADS] = b
    a_ref[GLA_HEADS + 1] = q
    lax.fori_loop(0, C // 8, row_group, 0)
    outs = []
    for pair in range(GLA_HEADS // 2):
        s_pair = st_ref[direction, pair]
        s_next = []
        for h in (2 * pair, 2 * pair + 1):
            kc = slice(h * GLA_DK, (h + 1) * GLA_DK)
            vc = slice(h * GLA_DV, (h + 1) * GLA_DV)
            s_t = s_pair[:, (h % 2) * GLA_DK:(h % 2 + 1) * GLA_DK]
            outs.append(_dot_nt(q_in[:, kc], s_t.astype(_BF16)) + _dot(a_ref[h, :, 0:C].astype(_BF16), v[:, vc]))
            s_next.append(s_t * e_end[:, kc] + _dot_tn(v[:, vc], k_out[:, kc]))
        st_ref[direction, pair] = jnp.concatenate(s_next, axis=-1)
    o_ref[rows, :] += jnp.concatenate(outs, axis=-1)


def _gla_block(z_ref, la_ref, o_ref, st_ref, direction, start, use_state):
    NB = GLA_BLOCK
    fwd = direction == 0
    rows = slice(start, start + NB) if isinstance(start, int) else pl.ds(pl.multiple_of(start, NB), NB)
    q = z_ref[rows, C_QB:C_QB + GLA_QK] * (GLA_DK ** -0.5)
    k = z_ref[rows, C_KB:C_KB + GLA_QK]
    v = z_ref[rows, C_VB:C_VB + GLA_VW].astype(_BF16)
    b = la_ref[rows, direction * GLA_QK:(direction + 1) * GLA_QK]
    end, mid = (NB - 1, NB // 2 - 1) if fwd else (0, NB // 2)
    b_end = b[end:end + 1]
    c = b - b[mid:mid + 1]
    q_c = (q * jnp.exp(c)).astype(_BF16)
    k_c = (k * jnp.exp(-c)).astype(_BF16)
    k_fin = (k * jnp.exp(b_end - b)).astype(_BF16)
    if use_state:
        q_in = (q * jnp.exp(b)).astype(_BF16)
        e_all = jnp.exp(b_end)
    ti = lax.broadcasted_iota(jnp.int32, (NB, NB), 0)
    si = lax.broadcasted_iota(jnp.int32, (NB, NB), 1)
    causal = (si <= ti) if fwd else (si >= ti)
    tile = 2 * GLA_DK
    low = lax.broadcasted_iota(jnp.int32, (NB, tile), 1) < GLA_DK
    zero = jnp.zeros((NB, tile), _BF16)
    pick = lambda x, parity: jnp.where(low, x, zero) if parity == 0 else jnp.where(low, zero, x)
    outs = []
    for pair in range(GLA_HEADS // 2):
        lanes = slice(pair * tile, (pair + 1) * tile)
        if use_state:
            s_pair = st_ref[direction, pair]
            s_bf = s_pair.astype(_BF16)
        s_new = None
        for parity in range(2):
            h = 2 * pair + parity
            vc = slice(h * GLA_DV, (h + 1) * GLA_DV)
            a = jnp.where(causal, _dot_nt(q_c[:, lanes], pick(k_c[:, lanes], parity)), 0.0)
            o_h = _dot(a.astype(_BF16), v[:, vc])
            if use_state:
                o_h = o_h + _dot_nt(pick(q_in[:, lanes], parity), s_bf)
            outs.append(o_h)
            upd = _dot_tn(v[:, vc], pick(k_fin[:, lanes], parity))
            s_new = upd if s_new is None else s_new + upd
        if use_state:
            s_new = s_new + s_pair * e_all[:, lanes]
        st_ref[direction, pair] = s_new
    o_ref[rows, :] += jnp.concatenate(outs, axis=-1)


def _log_decay(z_ref, rows, wg2_ref, bg2_ref):
    pre = lax.dot_general(z_ref[rows, C_GL:C_GL + GL_PAD], wg2_ref[...], (((1,), (0,)), ((), ())),
                          precision=lax.Precision.HIGHEST, preferred_element_type=_F32) + bg2_ref[...]
    return _log_sigmoid(pre) * (1.0 / GLA_TAU)


def _gla(T, z_ref, la_ref, o_ref, st_ref, a_ref, wg2_ref, bg2_ref, use_state):
    NB = GLA_BLOCK
    n_chunks = T // GLA_CHUNK
    n_blocks = T // NB
    ti = lax.broadcasted_iota(jnp.int32, (NB, NB), 0)
    si = lax.broadcasted_iota(jnp.int32, (NB, NB), 1)
    tri = [jnp.where(si <= ti, 1.0, 0.0).astype(_BF16), jnp.where(si >= ti, 1.0, 0.0).astype(_BF16)]
    worst = None
    for r0 in range(0, T, NB):
        rows = slice(r0, r0 + NB)
        la = _log_decay(z_ref, rows, wg2_ref, bg2_ref)
        la_hi = la.astype(_BF16)
        la_lo = (la - la_hi.astype(_F32)).astype(_BF16)
        for d in range(2):
            cols = slice(d * GLA_QK, (d + 1) * GLA_QK)
            b = _dot(tri[d], la_hi[:, cols]) + _dot(tri[d], la_lo[:, cols])
            la_ref[rows, cols] = b
            first, mid, last = (0, NB // 2 - 1, NB - 1) if d == 0 else (NB - 1, NB // 2, 0)
            span = jnp.max(jnp.maximum(b[first:first + 1] - b[mid:mid + 1], b[mid:mid + 1] - b[last:last + 1]))
            worst = span if worst is None else jnp.maximum(worst, span)
    fast_ok = worst <= GLA_MAX_EXPONENT
    o_ref[...] = jnp.zeros(o_ref.shape, _F32)

    @pl.when(fast_ok)
    def _():
        if n_blocks == 1:
            for direction in range(2):
                _gla_block(z_ref, la_ref, o_ref, st_ref, direction, 0, use_state)
        else:
            def body(i, carry):
                _gla_block(z_ref, la_ref, o_ref, st_ref, 0, i * NB, True)
                _gla_block(z_ref, la_ref, o_ref, st_ref, 1, (n_blocks - 1 - i) * NB, True)
                return carry
            lax.fori_loop(0, n_blocks, body, 0)

    @pl.when(jnp.logical_not(fast_ok))
    def _():
        for r0 in range(0, T, NB):
            la_ref[r0:r0 + NB, :] = _log_decay(z_ref, slice(r0, r0 + NB), wg2_ref, bg2_ref)

        def body(i, carry):
            _gla_chunk(z_ref, la_ref, o_ref, st_ref, a_ref, 0, i * GLA_CHUNK)
            _gla_chunk(z_ref, la_ref, o_ref, st_ref, a_ref, 1, (n_chunks - 1 - i) * GLA_CHUNK)
            return carry
        lax.fori_loop(0, n_chunks, body, 0)


def _gla_finish(T, z_ref, o_ref, ggla_ref, vones_ref, ob_ref):
    for r0 in range(0, T, PROJ_TILE):
        rows = slice(r0, r0 + PROJ_TILE)
        o = o_ref[rows, :]
        y = o * _group_rms_scale(o, vones_ref[...], GLA_DV) * ggla_ref[...]
        ob_ref[0, rows, :] = (y * _silu(z_ref[rows, C_RB:C_RB + GLA_VW])).astype(_BF16)


def _pool(T, z_ref, upad_ref, wpool_ref, pscale_ref, oc_ref):
    u = z_ref[:, C_UC:C_UC + POOL_W]
    u_hi = u.astype(_BF16)
    u_lo = (u - u_hi.astype(_F32)).astype(_BF16)
    zeros = jnp.zeros((POOL_HALO, POOL_W), _BF16)
    for part, val in ((0, u_hi), (1, u_lo)):
        upad_ref[part, 0:POOL_HALO, :] = zeros
        upad_ref[part, POOL_HALO + T:POOL_HALO + T + POOL_HALO, :] = zeros
        upad_ref[part, POOL_HALO:POOL_HALO + T, :] = val
    span = POOL_TILE + 2 * POOL_HALO
    r = lax.broadcasted_iota(jnp.int32, (POOL_TILE, span), 0)
    c = lax.broadcasted_iota(jnp.int32, (POOL_TILE, span), 1)
    off = c - POOL_HALO - r
    for jb in range(T // POOL_TILE):
        t = jb * POOL_TILE + lax.broadcasted_iota(jnp.int32, (POOL_TILE, 1), 0)
        parts = []
        for g, w in enumerate(POOL_WINDOWS):
            cols = slice(g * POOL_GROUP_DIM, (g + 1) * POOL_GROUP_DIM)
            band = jnp.where((off >= -(w // 2)) & (off < w - w // 2), 1.0, 0.0).astype(_BF16)
            win = slice(jb * POOL_TILE, jb * POOL_TILE + span)
            total = _dot(band, upad_ref[0, win, cols]) + _dot(band, upad_ref[1, win, cols])
            cnt = (jnp.minimum(t - w // 2 + w, T) - jnp.maximum(t - w // 2, 0)).astype(_F32)
            pooled = total / cnt - z_ref[jb * POOL_TILE:(jb + 1) * POOL_TILE, C_UC + g * POOL_GROUP_DIM:
                                         C_UC + (g + 1) * POOL_GROUP_DIM]
            parts.append(_dot(pooled.astype(_BF16), wpool_ref[g]))
        y = jnp.concatenate(parts, axis=-1) * pscale_ref[...]
        oc_ref[0, jb * POOL_TILE:(jb + 1) * POOL_TILE, :] = y.astype(_BF16)


def _mix_kernel(latent, T, layer_ref, *refs):
    (x_ref, mod_ref, gn1_ref, wmix_ref, gqn_ref, gkn_ref, sink_ref, wg2_ref, bg2_ref, ggla_ref,
     wpool_ref, pscale_ref, hones_ref, vones_ref) = refs[:14]
    refs = refs[14:]
    if latent:
        kc_ref, vc_ref, st0_ref, cos_ref, sin_ref, oa_ref, ob_ref, oc_ref = refs[:8]
        refs = refs[8:]
    else:
        oa_ref, ob_ref, oc_ref, kout_ref, vout_ref, stout_ref = refs[3:9]
        refs = refs[9:]
    z_ref, qr_ref, ks_ref, vs_ref, la_ref, o_ref, st_ref, a_ref, upad_ref = refs
    layer = layer_ref[0]
    pad = ATT_BLOCK if latent else 0

    if latent:
        for ref in (ks_ref, vs_ref):
            zeros = jnp.zeros(ref.shape[:2] + (pad, ref.shape[3]), _BF16)
            ref[:, :, 0:pad, :] = zeros
            ref[:, :, pad + T:pad + T + pad, :] = zeros
    shift = mod_ref[0, :, 0:D_MODEL]
    scale = mod_ref[0, :, D_MODEL:2 * D_MODEL]
    for r0 in range(0, T, PROJ_TILE):
        rows = slice(r0, r0 + PROJ_TILE)
        x = x_ref[0, rows, :]
        hn = (x * _rms_scale(x) * gn1_ref[...]) * (1.0 + scale) + shift
        z_ref[rows, :] = _dot(hn.astype(_BF16), wmix_ref[...])
        q = z_ref[rows, C_QA:C_QA + ATT_Q]
        k = z_ref[rows, C_KA:C_KA + ATT_KV]
        q = q * _group_rms_scale(q, hones_ref[...], HEAD_DIM) * gqn_ref[...]
        k = k * _group_rms_scale(k, hones_ref[0:ATT_KV, 0:ATT_KV], HEAD_DIM) * gkn_ref[...]
        v = z_ref[rows, C_VA:C_VA + ATT_KV]
        if latent:
            cos = jnp.concatenate([cos_ref[rows, :]] * (ATT_Q // ATT_KV), axis=-1)
            sin = jnp.concatenate([sin_ref[rows, :]] * (ATT_Q // ATT_KV), axis=-1)
            q = _rope(q, cos, sin)
            k = _rope(k, cos_ref[rows, :], sin_ref[rows, :])
        else:
            kout_ref[0, rows, :] = k
            vout_ref[0, rows, :] = v
        qr_ref[rows, :] = (q * (HEAD_DIM ** -0.5)).astype(_BF16)
        _store_split_kv(k, v, ks_ref, vs_ref, slice(pad + r0, pad + r0 + PROJ_TILE))
    if latent:
        _attention_latent(T, layer, qr_ref, ks_ref, vs_ref, kc_ref, vc_ref, sink_ref, oa_ref)
    else:
        _attention_ctx(T, layer, qr_ref, ks_ref, vs_ref, sink_ref, oa_ref)

    if latent:
        st_ref[...] = st0_ref[0]
    else:
        st_ref[...] = jnp.zeros(st_ref.shape, _F32)
    _gla(T, z_ref, la_ref, o_ref, st_ref, a_ref, wg2_ref, bg2_ref, latent)
    _gla_finish(T, z_ref, o_ref, ggla_ref, vones_ref, ob_ref)
    if not latent:
        for d in range(2):
            for pair in range(GLA_HEADS // 2):
                stout_ref[0, d, pair] = st_ref[d, pair].T

    _pool(T, z_ref, upad_ref, wpool_ref, pscale_ref, oc_ref)


def _layer_spec(shape):
    zeros = (0,) * len(shape)
    return pl.BlockSpec((None,) + tuple(shape), lambda i, layer: (layer[0],) + zeros,
                        pipeline_mode=pl.Buffered(1))


def _mix_call(latent, layer, x, mod_all, pw, extra):
    B, T, _ = x.shape
    per_seq = lambda blk: pl.BlockSpec(blk, lambda b, layer: (b,) + (0,) * (len(blk) - 1))
    const_spec = lambda blk: pl.BlockSpec(blk, lambda b, layer: (0,) * len(blk), pipeline_mode=pl.Buffered(1))
    if latent:
        mod_spec = pl.BlockSpec((None, 1, 1, 6 * D_MODEL), lambda b, layer: (layer[0], b + 1, 0, 0))
    else:
        mod_spec = pl.BlockSpec((None, 1, 1, 6 * D_MODEL), lambda b, layer: (layer[0], 0, 0, 0))
    x_spec = pl.BlockSpec((1, T, D_MODEL), lambda b, layer: (b, 0, 0),
                          pipeline_mode=pl.Buffered(1) if latent else None)
    in_specs = [
        x_spec, mod_spec,
        _layer_spec((1, D_MODEL)), _layer_spec((D_MODEL, MIX_W)),
        _layer_spec((1, ATT_Q)), _layer_spec((1, ATT_KV)),
        pl.BlockSpec(memory_space=pltpu.SMEM),
        _layer_spec((GL_PAD, 2 * GLA_QK)), _layer_spec((1, 2 * GLA_QK)), _layer_spec((1, GLA_VW)),
        _layer_spec((POOL_GROUPS, POOL_GROUP_DIM, POOL_GROUP_DIM)), _layer_spec((1, POOL_W)),
        const_spec((ATT_Q, ATT_Q)), const_spec((GLA_VW, GLA_VW)),
    ]
    args = [x, mod_all, pw["g_norm1"], pw["w_mix"], pw["g_qn"], pw["g_kn"], pw["att_sink"], pw["w_gate2"],
            pw["b_gate2"], pw["g_gla_out"], pw["w_pool"], pw["pool_scale"], pw["head_ones"], pw["gla_ones"]]
    act = lambda width: jax.ShapeDtypeStruct((B, T, width), _BF16)
    out_shape = [act(ATT_Q), act(GLA_VW), act(POOL_W)]
    out_specs = [per_seq((1, T, ATT_Q)), per_seq((1, T, GLA_VW)), per_seq((1, T, POOL_W))]
    aliases = {}
    if latent:
        cache_k, cache_v, st0, cos, sin = extra
        P = cache_k.shape[2]
        cache_spec = pl.BlockSpec((1, 1, P, ATT_KV), lambda b, layer: (b, layer[0], 0, 0))
        table_spec = const_spec((T, ATT_KV))
        in_specs += [cache_spec, cache_spec,
                     pl.BlockSpec((1, None, 2, GLA_HEADS // 2, GLA_DV, 2 * GLA_DK),
                                  lambda b, layer: (b, layer[0], 0, 0, 0, 0)),
                     table_spec, table_spec]
        args += [cache_k, cache_v, st0, cos, sin]
    else:
        n_in = 1 + len(args)
        aliases = {n_in + i: len(out_shape) + i for i in range(len(extra))}
        in_specs += [pl.BlockSpec(memory_space=pl.ANY)] * len(extra)
        args += list(extra)
        out_shape += [jax.ShapeDtypeStruct(a.shape, a.dtype) for a in extra]
        at_layer = lambda blk: pl.BlockSpec((1, None) + blk, lambda b, layer: (b, layer[0]) + (0,) * len(blk))
        out_specs += [at_layer((T, ATT_KV)), at_layer((T, ATT_KV)),
                      at_layer((2, GLA_HEADS // 2, 2 * GLA_DK, GLA_DV))]
    kv_rows = T + 2 * ATT_BLOCK if latent else T
    scratch = [
        pltpu.VMEM((T, MIX_W), _F32),
        pltpu.VMEM((T, ATT_Q), _BF16),
        pltpu.VMEM((ATT_KV_HEADS, 2, kv_rows, 2 * HEAD_DIM), _BF16),
        pltpu.VMEM((ATT_KV_HEADS, 2, kv_rows, 4 * HEAD_DIM), _BF16),
        pltpu.VMEM((T, 2 * GLA_QK), _F32),
        pltpu.VMEM((T, GLA_VW), _F32),
        pltpu.VMEM((2, GLA_HEADS // 2, GLA_DV, 2 * GLA_DK), _F32),
        pltpu.VMEM((GLA_HEADS + 2, GLA_CHUNK, GLA_QK), _F32),
        pltpu.VMEM((2, T + 2 * POOL_HALO, POOL_W), _BF16),
    ]
    return pl.pallas_call(
        functools.partial(_mix_kernel, latent, T),
        grid_spec=pltpu.PrefetchScalarGridSpec(
            num_scalar_prefetch=1, grid=(B,), in_specs=in_specs, out_specs=out_specs, scratch_shapes=scratch),
        out_shape=out_shape,
        input_output_aliases=aliases,
        compiler_params=pltpu.CompilerParams(dimension_semantics=("arbitrary",), vmem_limit_bytes=VMEM_LIMIT),
        name="mix_latent" if latent else "mix_ctx",
    )(layer, *args)


def _post_kernel(layer_ref, x_ref, mod_ref, oa_ref, ob_ref, oc_ref, gn1_ref, gn2_ref, wgate_ref, wa_ref, wb_ref,
                 wc_ref, wout_ref, wfg_ref, wfu_ref, wfd_ref, out_ref):
    x = x_ref[...]
    mod = lambda i: mod_ref[0, :, i * D_MODEL:(i + 1) * D_MODEL]
    hn = (x * _rms_scale(x) * gn1_ref[...]) * (1.0 + mod(1)) + mod(0)
    gates = jax.nn.sigmoid(_dot(hn.astype(_BF16), wgate_ref[...]))
    mixed = (gates[:, 0:D_MODEL] * _dot(oa_ref[...], wa_ref[...])
             + gates[:, D_MODEL:2 * D_MODEL] * _dot(ob_ref[...], wb_ref[...])
             + gates[:, 2 * D_MODEL:3 * D_MODEL] * _dot(oc_ref[...], wc_ref[...]))
    x = x + mod(2) * _dot(mixed.astype(_BF16), wout_ref[...])
    hn = ((x * _rms_scale(x) * gn2_ref[...]) * (1.0 + mod(4)) + mod(3)).astype(_BF16)
    h = _silu(_dot(hn, wfg_ref[...])) * _dot(hn, wfu_ref[...])
    out_ref[...] = x + mod(5) * _dot(h.astype(_BF16), wfd_ref[...])


def _post_call(layer, x2d, mod_all, oa, ob, oc, pw, tiles_per_seq, shared_mod):
    n = x2d.shape[0]
    row = lambda w: pl.BlockSpec((POST_TILE, w), lambda i, layer: (i, 0))
    if shared_mod:
        mod_spec = pl.BlockSpec((None, 1, 1, 6 * D_MODEL), lambda i, layer: (layer[0], 0, 0, 0))
    else:
        mod_spec = pl.BlockSpec((None, 1, 1, 6 * D_MODEL),
                                lambda i, layer: (layer[0], 1 + i // tiles_per_seq, 0, 0))
    in_specs = [row(D_MODEL), mod_spec, row(ATT_Q), row(GLA_VW), row(POOL_W),
                _layer_spec((1, D_MODEL)), _layer_spec((1, D_MODEL)), _layer_spec((D_MODEL, GATE_W)),
                _layer_spec((ATT_Q, D_MODEL)), _layer_spec((GLA_VW, D_MODEL)), _layer_spec((POOL_W, D_MODEL)),
                _layer_spec((D_MODEL, D_MODEL)), _layer_spec((D_MODEL, D_FF)), _layer_spec((D_MODEL, D_FF)),
                _layer_spec((D_FF, D_MODEL))]
    return pl.pallas_call(
        _post_kernel,
        grid_spec=pltpu.PrefetchScalarGridSpec(
            num_scalar_prefetch=1, grid=(n // POST_TILE,), in_specs=in_specs, out_specs=row(D_MODEL)),
        out_shape=jax.ShapeDtypeStruct(x2d.shape, _F32),
        input_output_aliases={1: 0},
        compiler_params=pltpu.CompilerParams(dimension_semantics=("arbitrary",), vmem_limit_bytes=VMEM_LIMIT),
        name="post",
    )(layer, x2d, mod_all, oa, ob, oc, pw["g_norm1"], pw["g_norm2"], pw["w_gates"], pw["w_br_a"], pw["w_br_b"],
      pw["w_br_c"], pw["w_out"], pw["w_ff_gate"], pw["w_ff_up"], pw["w_ff_down"])


def _rope_tables(T):
    quarter = HEAD_DIM // 4
    inv_freq = ROPE_BASE ** (-np.arange(quarter, dtype=np.float32) / quarter)
    pos = np.arange(T)
    ang_row = (pos // GRID_W).astype(np.float32)[:, None] * inv_freq[None, :]
    ang_col = (pos % GRID_W).astype(np.float32)[:, None] * inv_freq[None, :]
    cos = np.concatenate([np.cos(ang_row)] * 2 + [np.cos(ang_col)] * 2, axis=-1)
    sin = np.concatenate([-np.sin(ang_row), np.sin(ang_row), -np.sin(ang_col), np.sin(ang_col)], axis=-1)
    return (jnp.asarray(np.tile(cos, (1, ATT_KV_HEADS)), _F32), jnp.asarray(np.tile(sin, (1, ATT_KV_HEADS)), _F32))


W_IN_GL = ATT_Q + 2 * ATT_KV + 2 * GLA_QK + 2 * GLA_VW
W_IN_UC = W_IN_GL + 2 * GLA_RANK
W_IN_GATE = W_IN_UC + POOL_W
W_IN_WIDTH = W_IN_GATE + GATE_W
W_IN_ROWS = 256


def _split_w_in_kernel(w_ref, mix_ref, gate_ref):
    mix_ref[0, :, 0:W_IN_GL] = w_ref[0, :, 0:W_IN_GL].astype(_BF16)
    tail = w_ref[0, :, W_IN_GL:W_IN_WIDTH]
    rank2 = 2 * GLA_RANK
    mix_ref[0, :, C_UC:C_UC + POOL_W] = tail[:, rank2:rank2 + POOL_W].astype(_BF16)
    mix_ref[0, :, C_GL:MIX_W] = jnp.concatenate(
        [tail[:, 0:rank2], jnp.zeros((W_IN_ROWS, GL_PAD - rank2), _F32)], axis=-1).astype(_BF16)
    gate_ref[0] = tail[:, rank2 + POOL_W:].astype(_BF16)


def _split_w_in(w_in):
    return pl.pallas_call(
        _split_w_in_kernel,
        grid=(DEPTH, D_MODEL // W_IN_ROWS),
        in_specs=[pl.BlockSpec((1, W_IN_ROWS, W_IN_WIDTH), lambda l, i: (l, i, 0))],
        out_specs=[pl.BlockSpec((1, W_IN_ROWS, MIX_W), lambda l, i: (l, i, 0)),
                   pl.BlockSpec((1, W_IN_ROWS, GATE_W), lambda l, i: (l, i, 0))],
        out_shape=[jax.ShapeDtypeStruct((DEPTH, D_MODEL, MIX_W), _BF16),
                   jax.ShapeDtypeStruct((DEPTH, D_MODEL, GATE_W), _BF16)],
        name="split_w_in",
    )(w_in)


def _prepare_weights(w_in, g_qn, g_kn, att_sink, w_gate2, b_gate2, g_gla_out, w_pool, pool_scale, w_br_a,
                     w_br_b, w_br_c, w_out, g_norm1, g_norm2, w_ff_gate, w_ff_up, w_ff_down):
    assert w_in.shape == (DEPTH, D_MODEL, W_IN_WIDTH) and C_GL == W_IN_GATE - 2 * GLA_RANK
    w_mix, w_gates = _split_w_in(w_in)
    wg2 = jnp.zeros((DEPTH, GL_PAD, 2 * GLA_QK), _F32)
    wg2 = wg2.at[:, 0:GLA_RANK, 0:GLA_QK].set(w_gate2[:, 0])
    wg2 = wg2.at[:, GLA_RANK:2 * GLA_RANK, GLA_QK:].set(w_gate2[:, 1])
    vec = lambda a: a.reshape(DEPTH, 1, -1)
    group_ones = lambda n, width: jnp.asarray(
        (np.arange(n)[:, None] // width) == (np.arange(n)[None, :] // width), _BF16)
    return {
        "head_ones": group_ones(ATT_Q, HEAD_DIM),
        "gla_ones": group_ones(GLA_VW, GLA_DV),
        "w_mix": w_mix,
        "w_gates": w_gates,
        "g_qn": vec(jnp.tile(g_qn, (1, ATT_HEADS))),
        "g_kn": vec(jnp.tile(g_kn, (1, ATT_KV_HEADS))),
        "att_sink": att_sink,
        "w_gate2": wg2,
        "b_gate2": vec(b_gate2),
        "g_gla_out": vec(jnp.tile(g_gla_out, (1, GLA_HEADS))),
        "w_pool": w_pool.astype(_BF16),
        "pool_scale": vec(pool_scale),
        "w_br_a": w_br_a.astype(_BF16),
        "w_br_b": w_br_b.astype(_BF16),
        "w_br_c": w_br_c.astype(_BF16),
        "w_out": w_out.astype(_BF16),
        "g_norm1": vec(g_norm1),
        "g_norm2": vec(g_norm2),
        "w_ff_gate": w_ff_gate.astype(_BF16),
        "w_ff_up": w_ff_up.astype(_BF16),
        "w_ff_down": w_ff_down.astype(_BF16),
    }


def kernel(x_prompt, x_sample, c, cache_k, cache_v, state_gla, c_ctx, w_in, g_qn, g_kn, att_sink, w_gate2,
           b_gate2, g_gla_out, w_pool, pool_scale, w_br_a, w_br_b, w_br_c, w_out, g_norm1, g_norm2, w_mod,
           b_mod, w_ff_gate, w_ff_up, w_ff_down):
    B, T, _ = x_prompt.shape
    BL, TL, _ = x_sample.shape
    assert T % POST_TILE == 0 and TL % POST_TILE == 0 and BL + 1 <= MOD_ROWS
    cv = jnp.concatenate([c_ctx[None, :], c, jnp.zeros((MOD_ROWS - 1 - BL, D_MODEL), _F32)], axis=0)
    mod_all = _modulation(cv, w_mod, b_mod).reshape(DEPTH, MOD_ROWS, 1, 6 * D_MODEL)
    pw = _prepare_weights(w_in, g_qn, g_kn, att_sink, w_gate2, b_gate2, g_gla_out, w_pool, pool_scale, w_br_a,
                          w_br_b, w_br_c, w_out, g_norm1, g_norm2, w_ff_gate, w_ff_up, w_ff_down)
    cos, sin = _rope_tables(TL)
    P = cache_k.shape[2]
    latent_ctx = (cache_k.reshape(BL, DEPTH, P, ATT_KV), cache_v.reshape(BL, DEPTH, P, ATT_KV),
                  jnp.swapaxes(state_gla.reshape(BL, DEPTH, 2, GLA_HEADS // 2, 2 * GLA_DK, GLA_DV), -1, -2),
                  cos, sin)

    def layer_step(l, carry):
        yp, ys, new_k, new_v, new_st = carry
        layer = jnp.full((1,), l, jnp.int32)
        oa, ob, oc, new_k, new_v, new_st = _mix_call(False, layer, yp, mod_all, pw, (new_k, new_v, new_st))
        yp = _post_call(layer, yp.reshape(B * T, D_MODEL), mod_all, oa.reshape(B * T, -1), ob.reshape(B * T, -1),
                        oc.reshape(B * T, -1), pw, T // POST_TILE, True).reshape(B, T, D_MODEL)
        oa, ob, oc = _mix_call(True, layer, ys, mod_all, pw, latent_ctx)
        ys = _post_call(layer, ys.reshape(BL * TL, D_MODEL), mod_all, oa.reshape(BL * TL, -1),
                        ob.reshape(BL * TL, -1), oc.reshape(BL * TL, -1), pw, TL // POST_TILE,
                        False).reshape(BL, TL, D_MODEL)
        return yp, ys, new_k, new_v, new_st

    init = (x_prompt, x_sample,
            jnp.zeros((B, DEPTH, T, ATT_KV), _F32), jnp.zeros((B, DEPTH, T, ATT_KV), _F32),
            jnp.zeros((B, DEPTH, 2, GLA_HEADS // 2, 2 * GLA_DK, GLA_DV), _F32))
    yp, ys, new_k, new_v, new_st = lax.fori_loop(0, DEPTH, layer_step, init)
    return (yp, ys, new_k.reshape(B, DEPTH, T, ATT_KV_HEADS, HEAD_DIM),
            new_v.reshape(B, DEPTH, T, ATT_KV_HEADS, HEAD_DIM),
            new_st.reshape(B, DEPTH, 2, GLA_HEADS, GLA_DK, GLA_DV))
```

```python
import functools

import jax
import jax.numpy as jnp
import numpy as np
from jax import lax
from jax.experimental import pallas as pl
from jax.experimental.pallas import tpu as pltpu

D_MODEL = 1024
DEPTH = 4
GRID_W = 64
ATT_HEADS = 8
ATT_KV_HEADS = 2
ATT_GROUP = ATT_HEADS // ATT_KV_HEADS
HEAD_DIM = 64
WINDOW = 128
ATT_BLOCK = 128
ROPE_BASE = 10000.0
GLA_HEADS = 4
GLA_DK = 64
GLA_DV = 128
GLA_RANK = 16
GLA_TAU = 16.0
GLA_CHUNK = 64
POOL_GROUPS = 4
POOL_GROUP_DIM = 128
POOL_WINDOWS = (2, 4, 8, 16)
D_FF = 2816
ATT_Q = ATT_HEADS * HEAD_DIM
ATT_KV = ATT_KV_HEADS * HEAD_DIM
GLA_QK = GLA_HEADS * GLA_DK
GLA_VW = GLA_HEADS * GLA_DV
POOL_W = POOL_GROUPS * POOL_GROUP_DIM
EPS = 1e-6
NEG = -1e30

C_QA = 0
C_KA = C_QA + ATT_Q
C_VA = C_KA + ATT_KV
C_QB = C_VA + ATT_KV
C_KB = C_QB + GLA_QK
C_VB = C_KB + GLA_QK
C_RB = C_VB + GLA_VW
C_UC = C_RB + GLA_VW
C_GL = C_UC + POOL_W
GL_PAD = 128
MIX_W = C_GL + GL_PAD
GATE_W = 3 * D_MODEL

POST_TILE = 256
PROJ_TILE = 256
POOL_TILE = 256
POOL_HALO = 128
MOD_ROWS = 8
MOD_TILE = 1024
GLA_BLOCK = 256
GLA_MAX_EXPONENT = 80.0
VMEM_LIMIT = 56 * 1024 * 1024

_F32 = jnp.float32
_BF16 = jnp.bfloat16


def _dot(a, b):
    return jnp.dot(a, b, preferred_element_type=_F32)


def _dot_nt(a, b):
    return lax.dot_general(a, b, (((1,), (1,)), ((), ())), preferred_element_type=_F32)


def _dot_tn(a, b):
    return lax.dot_general(a, b, (((0,), (0,)), ((), ())), preferred_element_type=_F32)


def _rms_scale(x):
    return lax.rsqrt(jnp.mean(x * x, axis=-1, keepdims=True) + EPS)


def _group_rms_scale(x, group_ones, width):
    return lax.rsqrt(_dot((x * x).astype(_BF16), group_ones) * (1.0 / width) + EPS)


def _log_sigmoid(x):
    return jnp.minimum(x, 0.0) - jnp.log1p(jnp.exp(-jnp.abs(x)))


def _silu(x):
    return x * jax.nn.sigmoid(x)


def _rope(x, cos, sin_signed):
    n = x.shape[-1]
    lane = lax.broadcasted_iota(jnp.int32, x.shape, 1)
    up = pltpu.roll(x, n - HEAD_DIM // 4, axis=1)
    down = pltpu.roll(x, HEAD_DIM // 4, axis=1)
    partner = jnp.where((lane & (HEAD_DIM // 2 - 1)) < HEAD_DIM // 4, up, down)
    return x * cos + partner * sin_signed


def _mod_kernel(cv_ref, w_ref, b_ref, out_ref):
    s = _silu(cv_ref[...]).astype(_BF16)
    out_ref[0] = _dot(s, w_ref[0].astype(_BF16)) + b_ref[0]


def _modulation(cv, w_mod, b_mod):
    n_col = (6 * D_MODEL) // MOD_TILE
    return pl.pallas_call(
        _mod_kernel,
        grid=(DEPTH, n_col),
        in_specs=[
            pl.BlockSpec((MOD_ROWS, D_MODEL), lambda l, j: (0, 0)),
            pl.BlockSpec((1, D_MODEL, MOD_TILE), lambda l, j: (l, 0, j)),
            pl.BlockSpec((1, 1, MOD_TILE), lambda l, j: (l, 0, j)),
        ],
        out_specs=pl.BlockSpec((1, MOD_ROWS, MOD_TILE), lambda l, j: (l, 0, j)),
        out_shape=jax.ShapeDtypeStruct((DEPTH, MOD_ROWS, 6 * D_MODEL), _F32),
        name="modulation",
    )(cv, w_mod, b_mod.reshape(DEPTH, 1, 6 * D_MODEL))


def _split_heads(x):
    low = lax.broadcasted_iota(jnp.int32, x.shape, 1) < HEAD_DIM
    swapped = pltpu.roll(x, HEAD_DIM, axis=1)
    zero = jnp.zeros_like(x)
    return ((jnp.where(low, x, zero), jnp.where(low, zero, swapped)),
            (jnp.where(low, swapped, zero), jnp.where(low, zero, x)))


def _store_split_kv(k, v, ks_ref, vs_ref, rows):
    ones = jnp.ones_like(v)
    for kv, (k_sides, v_sides, one_sides) in enumerate(zip(_split_heads(k), _split_heads(v), _split_heads(ones))):
        for side in range(2):
            ks_ref[kv, side, rows, :] = k_sides[side].astype(_BF16)
            vs_ref[kv, side, rows, :] = jnp.concatenate([v_sides[side], one_sides[side]], axis=-1).astype(_BF16)


def _pair_softmax_av(qp, keys, values, masks, sink_even, sink_odd):
    m = qp.shape[0]
    scores = []
    for (k_left, k_right), mask in zip(keys, masks):
        s_even, s_odd = _dot_nt(qp, k_left), _dot_nt(qp, k_right)
        if mask is not None:
            s_even, s_odd = jnp.where(mask, s_even, NEG), jnp.where(mask, s_odd, NEG)
        scores.append((s_even, s_odd))
    m_even = jnp.full((m, 1), sink_even, _F32)
    m_odd = jnp.full((m, 1), sink_odd, _F32)
    for s_even, s_odd in scores:
        m_even = jnp.maximum(m_even, jnp.max(s_even, axis=-1, keepdims=True))
        m_odd = jnp.maximum(m_odd, jnp.max(s_odd, axis=-1, keepdims=True))
    res = None
    for (s_even, s_odd), (w_left, w_right) in zip(scores, values):
        r = (_dot(jnp.exp(s_even - m_even).astype(_BF16), w_left)
             + _dot(jnp.exp(s_odd - m_odd).astype(_BF16), w_right))
        res = r if res is None else res + r
    pair = 2 * HEAD_DIM
    low = lax.broadcasted_iota(jnp.int32, (m, pair), 1) < HEAD_DIM
    den = res[:, pair:] + jnp.where(low, jnp.exp(sink_even - m_even), jnp.exp(sink_odd - m_odd))
    return res[:, :pair] / den


def _attention_ctx(T, layer, qr_ref, ks_ref, vs_ref, sink_ref, oa_ref):
    pair = 2 * HEAD_DIM
    for kv in range(ATT_KV_HEADS):
        keys = [(ks_ref[kv, 0], ks_ref[kv, 1])]
        values = [(vs_ref[kv, 0], vs_ref[kv, 1])]
        for j in range(ATT_GROUP // 2):
            head = kv * ATT_GROUP + 2 * j
            cols = slice(head * HEAD_DIM, head * HEAD_DIM + pair)
            o = _pair_softmax_av(qr_ref[:, cols], keys, values, [None], sink_ref[layer, head],
                                 sink_ref[layer, head + 1])
            oa_ref[0, :, cols] = o.astype(_BF16)


def _attention_latent(T, layer, qr_ref, ks_ref, vs_ref, kc_ref, vc_ref, sink_ref, oa_ref):
    pair = 2 * HEAD_DIM
    span = 3 * ATT_BLOCK
    kc, vc = kc_ref[0, 0], vc_ref[0, 0]
    ones = jnp.ones_like(vc)
    ctx_keys = [tuple(side.astype(_BF16) for side in sides) for sides in _split_heads(kc)]
    ctx_values = [tuple(jnp.concatenate([v_side, one_side], axis=-1).astype(_BF16)
                        for v_side, one_side in zip(v_sides, one_sides))
                  for v_sides, one_sides in zip(_split_heads(vc), _split_heads(ones))]

    def block(i, carry):
        q_rows = pl.ds(pl.multiple_of(i * ATT_BLOCK, ATT_BLOCK), ATT_BLOCK)
        k_rows = pl.ds(pl.multiple_of(i * ATT_BLOCK, ATT_BLOCK), span)
        q_pos = i * ATT_BLOCK + lax.broadcasted_iota(jnp.int32, (ATT_BLOCK, span), 0)
        k_pos = (i - 1) * ATT_BLOCK + lax.broadcasted_iota(jnp.int32, (ATT_BLOCK, span), 1)
        valid = (jnp.abs(k_pos - q_pos) <= WINDOW) & (k_pos >= 0) & (k_pos < T)
        for kv in range(ATT_KV_HEADS):
            keys = [(ks_ref[kv, 0, k_rows, :], ks_ref[kv, 1, k_rows, :]), ctx_keys[kv]]
            values = [(vs_ref[kv, 0, k_rows, :], vs_ref[kv, 1, k_rows, :]), ctx_values[kv]]
            for j in range(ATT_GROUP // 2):
                head = kv * ATT_GROUP + 2 * j
                cols = slice(head * HEAD_DIM, head * HEAD_DIM + pair)
                o = _pair_softmax_av(qr_ref[q_rows, cols], keys, values, [valid, None], sink_ref[layer, head],
                                     sink_ref[layer, head + 1])
                oa_ref[0, q_rows, cols] = o.astype(_BF16)
        return carry

    lax.fori_loop(0, T // ATT_BLOCK, block, 0)


def _gla_chunk(z_ref, la_ref, o_ref, st_ref, a_ref, direction, start):
    C = GLA_CHUNK
    rows = pl.ds(pl.multiple_of(start, C), C)
    q = z_ref[rows, C_QB:C_QB + GLA_QK] * (GLA_DK ** -0.5)
    k = z_ref[rows, C_KB:C_KB + GLA_QK]
    v = z_ref[rows, C_VB:C_VB + GLA_VW].astype(_BF16)
    la = la_ref[rows, direction * GLA_QK:(direction + 1) * GLA_QK]
    la_hi = la.astype(_BF16)
    la_lo = (la - la_hi.astype(_F32)).astype(_BF16)
    ti = lax.broadcasted_iota(jnp.int32, (C, C), 0)
    si = lax.broadcasted_iota(jnp.int32, (C, C), 1)
    causal = (si <= ti) if direction == 0 else (si >= ti)
    tri = jnp.where(causal, 1.0, 0.0).astype(_BF16)
    b = _dot(tri, la_hi) + _dot(tri, la_lo)
    end = C - 1 if direction == 0 else 0
    b_end = b[end:end + 1]
    q_in = (q * jnp.exp(b)).astype(_BF16)
    k_out = (k * jnp.exp(b_end - b)).astype(_BF16)
    e_end = jnp.exp(b_end)
    ones = jnp.ones((8, GLA_DK), _BF16)

    def row_group(g, carry):
        base = pl.multiple_of(g * 8, 8)
        b8 = a_ref[GLA_HEADS, pl.ds(base, 8), :]
        q8 = a_ref[GLA_HEADS + 1, pl.ds(base, 8), :]
        s_idx = lax.broadcasted_iota(jnp.int32, (C, 1), 0)
        rows_h = [[] for _ in range(GLA_HEADS)]
        for j in range(8):
            ok = (s_idx <= base + j) if direction == 0 else (s_idx >= base + j)
            decay = jnp.exp(jnp.where(ok, b8[j:j + 1] - b, NEG))
            p = (q8[j:j + 1] * k * decay).astype(_BF16)
            for h in range(GLA_HEADS):
                rows_h[h].append(_dot_nt(ones, p[:, h * GLA_DK:(h + 1) * GLA_DK])[0:1])
        for h in range(GLA_HEADS):
            a_ref[h, pl.ds(base, 8), 0:C] = jnp.concatenate(rows_h[h], axis=0)
        return carry

    a_ref[GLA_HEADS] = b
    a_ref[GLA_HEADS + 1] = q
    lax.fori_loop(0, C // 8, row_group, 0)
    outs = []
    for pair in range(GLA_HEADS // 2):
        s_pair = st_ref[direction, pair]
        s_next = []
        for h in (2 * pair, 2 * pair + 1):
            kc = slice(h * GLA_DK, (h + 1) * GLA_DK)
            vc = slice(h * GLA_DV, (h + 1) * GLA_DV)
            s_t = s_pair[:, (h % 2) * GLA_DK:(h % 2 + 1) * GLA_DK]
            outs.append(_dot_nt(q_in[:, kc], s_t.astype(_BF16)) + _dot(a_ref[h, :, 0:C].astype(_BF16), v[:, vc]))
            s_next.append(s_t * e_end[:, kc] + _dot_tn(v[:, vc], k_out[:, kc]))
        st_ref[direction, pair] = jnp.concatenate(s_next, axis=-1)
    o_ref[rows, :] += jnp.concatenate(outs, axis=-1)


def _gla_block(z_ref, la_ref, o_ref, st_ref, direction, start, use_state):
    NB = GLA_BLOCK
    fwd = direction == 0
    rows = slice(start, start + NB) if isinstance(start, int) else pl.ds(pl.multiple_of(start, NB), NB)
    q = z_ref[rows, C_QB:C_QB + GLA_QK] * (GLA_DK ** -0.5)
    k = z_ref[rows, C_KB:C_KB + GLA_QK]
    v = z_ref[rows, C_VB:C_VB + GLA_VW].astype(_BF16)
    b = la_ref[rows, direction * GLA_QK:(direction + 1) * GLA_QK]
    end, mid = (NB - 1, NB // 2 - 1) if fwd else (0, NB // 2)
    b_end = b[end:end + 1]
    c = b - b[mid:mid + 1]
    q_c = (q * jnp.exp(c)).astype(_BF16)
    k_c = (k * jnp.exp(-c)).astype(_BF16)
    k_fin = (k * jnp.exp(b_end - b)).astype(_BF16)
    if use_state:
        q_in = (q * jnp.exp(b)).astype(_BF16)
        e_all = jnp.exp(b_end)
    ti = lax.broadcasted_iota(jnp.int32, (NB, NB), 0)
    si = lax.broadcasted_iota(jnp.int32, (NB, NB), 1)
    causal = (si <= ti) if fwd else (si >= ti)
    tile = 2 * GLA_DK
    low = lax.broadcasted_iota(jnp.int32, (NB, tile), 1) < GLA_DK
    zero = jnp.zeros((NB, tile), _BF16)
    pick = lambda x, parity: jnp.where(low, x, zero) if parity == 0 else jnp.where(low, zero, x)
    outs = []
    for pair in range(GLA_HEADS // 2):
        lanes = slice(pair * tile, (pair + 1) * tile)
        if use_state:
            s_pair = st_ref[direction, pair]
            s_bf = s_pair.astype(_BF16)
        s_new = None
        for parity in range(2):
            h = 2 * pair + parity
            vc = slice(h * GLA_DV, (h + 1) * GLA_DV)
            a = jnp.where(causal, _dot_nt(q_c[:, lanes], pick(k_c[:, lanes], parity)), 0.0)
            o_h = _dot(a.astype(_BF16), v[:, vc])
            if use_state:
                o_h = o_h + _dot_nt(pick(q_in[:, lanes], parity), s_bf)
            outs.append(o_h)
            upd = _dot_tn(v[:, vc], pick(k_fin[:, lanes], parity))
            s_new = upd if s_new is None else s_new + upd
        if use_state:
            s_new = s_new + s_pair * e_all[:, lanes]
        st_ref[direction, pair] = s_new
    o_ref[rows, :] += jnp.concatenate(outs, axis=-1)


def _log_decay(z_ref, rows, wg2_ref, bg2_ref):
    x = z_ref[rows, C_GL:C_GL + GL_PAD]
    x_hi = x.astype(_BF16)
    x_lo = (x - x_hi.astype(_F32)).astype(_BF16)
    w_hi, w_lo = wg2_ref[0], wg2_ref[1]
    pre = _dot(x_hi, w_hi) + (_dot(x_lo, w_hi) + _dot(x_hi, w_lo)) + bg2_ref[...]
    return _log_sigmoid(pre) * (1.0 / GLA_TAU)


def _gla(T, z_ref, la_ref, o_ref, st_ref, a_ref, wg2_ref, bg2_ref, use_state):
    NB = GLA_BLOCK
    n_chunks = T // GLA_CHUNK
    n_blocks = T // NB
    ti = lax.broadcasted_iota(jnp.int32, (NB, NB), 0)
    si = lax.broadcasted_iota(jnp.int32, (NB, NB), 1)
    tri = [jnp.where(si <= ti, 1.0, 0.0).astype(_BF16), jnp.where(si >= ti, 1.0, 0.0).astype(_BF16)]
    worst = None
    for r0 in range(0, T, NB):
        rows = slice(r0, r0 + NB)
        la = _log_decay(z_ref, rows, wg2_ref, bg2_ref)
        la_hi = la.astype(_BF16)
        la_lo = (la - la_hi.astype(_F32)).astype(_BF16)
        for d in range(2):
            cols = slice(d * GLA_QK, (d + 1) * GLA_QK)
            b = _dot(tri[d], la_hi[:, cols]) + _dot(tri[d], la_lo[:, cols])
            la_ref[rows, cols] = b
            first, mid, last = (0, NB // 2 - 1, NB - 1) if d == 0 else (NB - 1, NB // 2, 0)
            span = jnp.max(jnp.maximum(b[first:first + 1] - b[mid:mid + 1], b[mid:mid + 1] - b[last:last + 1]))
            worst = span if worst is None else jnp.maximum(worst, span)
    fast_ok = worst <= GLA_MAX_EXPONENT
    o_ref[...] = jnp.zeros(o_ref.shape, _F32)

    @pl.when(fast_ok)
    def _():
        if n_blocks == 1:
            for direction in range(2):
                _gla_block(z_ref, la_ref, o_ref, st_ref, direction, 0, use_state)
        else:
            def body(i, carry):
                _gla_block(z_ref, la_ref, o_ref, st_ref, 0, i * NB, True)
                _gla_block(z_ref, la_ref, o_ref, st_ref, 1, (n_blocks - 1 - i) * NB, True)
                return carry
            lax.fori_loop(0, n_blocks, body, 0)

    @pl.when(jnp.logical_not(fast_ok))
    def _():
        for r0 in range(0, T, NB):
            la_ref[r0:r0 + NB, :] = _log_decay(z_ref, slice(r0, r0 + NB), wg2_ref, bg2_ref)

        def body(i, carry):
            _gla_chunk(z_ref, la_ref, o_ref, st_ref, a_ref, 0, i * GLA_CHUNK)
            _gla_chunk(z_ref, la_ref, o_ref, st_ref, a_ref, 1, (n_chunks - 1 - i) * GLA_CHUNK)
            return carry
        lax.fori_loop(0, n_chunks, body, 0)


def _gla_finish(T, z_ref, o_ref, ggla_ref, vones_ref, ob_ref):
    for r0 in range(0, T, PROJ_TILE):
        rows = slice(r0, r0 + PROJ_TILE)
        o = o_ref[rows, :]
        y = o * _group_rms_scale(o, vones_ref[...], GLA_DV) * ggla_ref[...]
        ob_ref[0, rows, :] = (y * _silu(z_ref[rows, C_RB:C_RB + GLA_VW])).astype(_BF16)


def _pool_halo(T):
    return POOL_HALO if T > POOL_TILE else 0


def _pool(T, z_ref, upad_ref, wpool_ref, pscale_ref, oc_ref):
    halo = _pool_halo(T)
    if halo:
        zeros = jnp.zeros((halo, POOL_W), _BF16)
        upad_ref[0:halo, :] = zeros
        upad_ref[halo + T:halo + T + halo, :] = zeros
    upad_ref[halo:halo + T, :] = z_ref[:, C_UC:C_UC + POOL_W].astype(_BF16)
    span = POOL_TILE + 2 * halo
    r = lax.broadcasted_iota(jnp.int32, (POOL_TILE, span), 0)
    c = lax.broadcasted_iota(jnp.int32, (POOL_TILE, span), 1)
    off = c - halo - r
    for jb in range(T // POOL_TILE):
        t = jb * POOL_TILE + lax.broadcasted_iota(jnp.int32, (POOL_TILE, 1), 0)
        parts = []
        for g, w in enumerate(POOL_WINDOWS):
            cols = slice(g * POOL_GROUP_DIM, (g + 1) * POOL_GROUP_DIM)
            band = jnp.where((off >= -(w // 2)) & (off < w - w // 2), 1.0, 0.0).astype(_BF16)
            win = slice(jb * POOL_TILE, jb * POOL_TILE + span)
            total = _dot(band, upad_ref[win, cols])
            cnt = (jnp.minimum(t - w // 2 + w, T) - jnp.maximum(t - w // 2, 0)).astype(_F32)
            pooled = total / cnt - z_ref[jb * POOL_TILE:(jb + 1) * POOL_TILE, C_UC + g * POOL_GROUP_DIM:
                                         C_UC + (g + 1) * POOL_GROUP_DIM]
            parts.append(_dot(pooled.astype(_BF16), wpool_ref[g]))
        y = jnp.concatenate(parts, axis=-1) * pscale_ref[...]
        oc_ref[0, jb * POOL_TILE:(jb + 1) * POOL_TILE, :] = y.astype(_BF16)


N_MIX_PARAMS = 15
N_MIX_SCRATCH = 9


def _mix_body(latent, T, layer, params, latent_refs, out_refs, scratch):
    (x_ref, mod_ref, gn1_ref, wmain_ref, wtail_ref, gqn_ref, gkn_ref, sink_ref, wg2_ref, bg2_ref, ggla_ref,
     wpool_ref, pscale_ref, hones_ref, vones_ref) = params
    if latent:
        kc_ref, vc_ref, st0_ref, cos_ref, sin_ref = latent_refs
        oa_ref, ob_ref, oc_ref = out_refs
    else:
        oa_ref, ob_ref, oc_ref, kout_ref, vout_ref, stout_ref = out_refs
    z_ref, qr_ref, ks_ref, vs_ref, la_ref, o_ref, st_ref, a_ref, upad_ref = scratch
    pad = ATT_BLOCK if latent else 0

    if latent:
        for ref in (ks_ref, vs_ref):
            zeros = jnp.zeros(ref.shape[:2] + (pad, ref.shape[3]), _BF16)
            ref[:, :, 0:pad, :] = zeros
            ref[:, :, pad + T:pad + T + pad, :] = zeros
    shift = mod_ref[0, :, 0:D_MODEL]
    scale = mod_ref[0, :, D_MODEL:2 * D_MODEL]
    for r0 in range(0, T, PROJ_TILE):
        rows = slice(r0, r0 + PROJ_TILE)
        x = x_ref[0, rows, :]
        hn = (x * _rms_scale(x) * gn1_ref[...]) * (1.0 + scale) + shift
        hn = hn.astype(_BF16)
        z_ref[rows, 0:C_UC] = _dot(hn, wmain_ref[...])
        z_ref[rows, C_UC:MIX_W] = _dot(hn, wtail_ref[...])
        q = z_ref[rows, C_QA:C_QA + ATT_Q]
        k = z_ref[rows, C_KA:C_KA + ATT_KV]
        q = q * _group_rms_scale(q, hones_ref[...], HEAD_DIM) * gqn_ref[...]
        k = k * _group_rms_scale(k, hones_ref[0:ATT_KV, 0:ATT_KV], HEAD_DIM) * gkn_ref[...]
        v = z_ref[rows, C_VA:C_VA + ATT_KV]
        if latent:
            cos = jnp.concatenate([cos_ref[rows, :]] * (ATT_Q // ATT_KV), axis=-1)
            sin = jnp.concatenate([sin_ref[rows, :]] * (ATT_Q // ATT_KV), axis=-1)
            q = _rope(q, cos, sin)
            k = _rope(k, cos_ref[rows, :], sin_ref[rows, :])
        else:
            kout_ref[0, rows, :] = k
            vout_ref[0, rows, :] = v
        qr_ref[rows, :] = (q * (HEAD_DIM ** -0.5)).astype(_BF16)
        _store_split_kv(k, v, ks_ref, vs_ref, slice(pad + r0, pad + r0 + PROJ_TILE))
    if latent:
        _attention_latent(T, layer, qr_ref, ks_ref, vs_ref, kc_ref, vc_ref, sink_ref, oa_ref)
    else:
        _attention_ctx(T, layer, qr_ref, ks_ref, vs_ref, sink_ref, oa_ref)

    if latent:
        st_ref[...] = st0_ref[0]
    else:
        st_ref[...] = jnp.zeros(st_ref.shape, _F32)
    _gla(T, z_ref, la_ref, o_ref, st_ref, a_ref, wg2_ref, bg2_ref, latent)
    _gla_finish(T, z_ref, o_ref, ggla_ref, vones_ref, ob_ref)
    if not latent:
        for d in range(2):
            for pair in range(GLA_HEADS // 2):
                stout_ref[0, d, pair] = st_ref[d, pair].T

    _pool(T, z_ref, upad_ref, wpool_ref, pscale_ref, oc_ref)


def _merge(x, mod_ref, oa, ob, oc, gn1_ref, wgate_ref, wa_ref, wb_ref, wc_ref, wout_ref):
    mod = lambda i: mod_ref[0, :, i * D_MODEL:(i + 1) * D_MODEL]
    hn = (x * _rms_scale(x) * gn1_ref[...]) * (1.0 + mod(1)) + mod(0)
    gates = jax.nn.sigmoid(_dot(hn.astype(_BF16), wgate_ref[...]))
    mixed = (gates[:, 0:D_MODEL] * _dot(oa, wa_ref[...])
             + gates[:, D_MODEL:2 * D_MODEL] * _dot(ob, wb_ref[...])
             + gates[:, 2 * D_MODEL:3 * D_MODEL] * _dot(oc, wc_ref[...]))
    return x + mod(2) * _dot(mixed.astype(_BF16), wout_ref[...])


def _ffn(x, mod_ref, gn2_ref, wfg_ref, wfu_ref, wfd_ref):
    mod = lambda i: mod_ref[0, :, i * D_MODEL:(i + 1) * D_MODEL]
    hn = ((x * _rms_scale(x) * gn2_ref[...]) * (1.0 + mod(4)) + mod(3)).astype(_BF16)
    h = _silu(_dot(hn, wfg_ref[...])) * _dot(hn, wfu_ref[...])
    return x + mod(5) * _dot(h.astype(_BF16), wfd_ref[...])


def _mix_latent_kernel(T, layer_ref, *refs):
    params, refs = refs[:N_MIX_PARAMS], refs[N_MIX_PARAMS:]
    _mix_body(True, T, layer_ref[0], params, refs[:5], refs[5:8], refs[8:])


def _mix_ctx_kernel(T, layer_ref, *refs):
    params, refs = refs[:N_MIX_PARAMS], refs[N_MIX_PARAMS + 3:]
    _mix_body(False, T, layer_ref[0], params, None, refs[:6], refs[6:])


def _layer_spec(shape):
    zeros = (0,) * len(shape)
    return pl.BlockSpec((None,) + tuple(shape), lambda i, layer: (layer[0],) + zeros,
                        pipeline_mode=pl.Buffered(1))


def _const_spec(blk):
    return pl.BlockSpec(blk, lambda i, layer: (0,) * len(blk), pipeline_mode=pl.Buffered(1))


def _mix_params(x, x_spec, mod_spec, mod_all, pw):
    specs = [
        x_spec, mod_spec,
        _layer_spec((1, D_MODEL)),
        _layer_spec((D_MODEL, C_UC)),
        _layer_spec((D_MODEL, MIX_W - C_UC)),
        _layer_spec((1, ATT_Q)), _layer_spec((1, ATT_KV)),
        pl.BlockSpec(memory_space=pltpu.SMEM),
        _layer_spec((2, GL_PAD, 2 * GLA_QK)), _layer_spec((1, 2 * GLA_QK)), _layer_spec((1, GLA_VW)),
        _layer_spec((POOL_GROUPS, POOL_GROUP_DIM, POOL_GROUP_DIM)), _layer_spec((1, POOL_W)),
        _const_spec((ATT_Q, ATT_Q)), _const_spec((GLA_VW, GLA_VW)),
    ]
    args = [x, mod_all, pw["g_norm1"], pw["w_in"], pw["w_tail"], pw["g_qn"], pw["g_kn"], pw["att_sink"], pw["w_gate2"],
            pw["b_gate2"], pw["g_gla_out"], pw["w_pool"], pw["pool_scale"], pw["head_ones"], pw["gla_ones"]]
    assert len(specs) == len(args) == N_MIX_PARAMS
    return specs, args


def _mix_scratch(T, kv_rows):
    scratch = [
        pltpu.VMEM((T, MIX_W), _F32),
        pltpu.VMEM((T, ATT_Q), _BF16),
        pltpu.VMEM((ATT_KV_HEADS, 2, kv_rows, 2 * HEAD_DIM), _BF16),
        pltpu.VMEM((ATT_KV_HEADS, 2, kv_rows, 4 * HEAD_DIM), _BF16),
        pltpu.VMEM((T, 2 * GLA_QK), _F32),
        pltpu.VMEM((T, GLA_VW), _F32),
        pltpu.VMEM((2, GLA_HEADS // 2, GLA_DV, 2 * GLA_DK), _F32),
        pltpu.VMEM((GLA_HEADS + 2, GLA_CHUNK, GLA_QK), _F32),
        pltpu.VMEM((T + 2 * _pool_halo(T), POOL_W), _BF16),
    ]
    assert len(scratch) == N_MIX_SCRATCH
    return scratch


def _mix_latent_call(layer, x, mod_all, pw, cache_k, cache_v, st0, cos, sin):
    B, T, _ = x.shape
    per_seq = lambda blk: pl.BlockSpec(blk, lambda b, layer: (b,) + (0,) * (len(blk) - 1))
    mod_spec = pl.BlockSpec((None, 1, 1, 6 * D_MODEL), lambda b, layer: (layer[0], b + 1, 0, 0))
    x_spec = pl.BlockSpec((1, T, D_MODEL), lambda b, layer: (b, 0, 0), pipeline_mode=pl.Buffered(1))
    in_specs, args = _mix_params(x, x_spec, mod_spec, mod_all, pw)
    P = cache_k.shape[2]
    cache_spec = pl.BlockSpec((1, 1, P, ATT_KV), lambda b, layer: (b, layer[0], 0, 0))
    in_specs += [cache_spec, cache_spec,
                 pl.BlockSpec((1, None, 2, GLA_HEADS // 2, GLA_DV, 2 * GLA_DK),
                              lambda b, layer: (b, layer[0], 0, 0, 0, 0)),
                 _const_spec((T, ATT_KV)), _const_spec((T, ATT_KV))]
    args += [cache_k, cache_v, st0, cos, sin]
    widths = (ATT_Q, GLA_VW, POOL_W)
    return pl.pallas_call(
        functools.partial(_mix_latent_kernel, T),
        grid_spec=pltpu.PrefetchScalarGridSpec(
            num_scalar_prefetch=1, grid=(B,), in_specs=in_specs,
            out_specs=[per_seq((1, T, w)) for w in widths],
            scratch_shapes=_mix_scratch(T, T + 2 * ATT_BLOCK)),
        out_shape=[jax.ShapeDtypeStruct((B, T, w), _BF16) for w in widths],
        compiler_params=pltpu.CompilerParams(dimension_semantics=("arbitrary",), vmem_limit_bytes=VMEM_LIMIT),
        name="mix_latent",
    )(layer, *args)


def _mix_ctx_call(layer, x, mod_all, pw, stacked):
    B, T, _ = x.shape
    per_seq = lambda blk: pl.BlockSpec(blk, lambda b, layer: (b,) + (0,) * (len(blk) - 1))
    mod_spec = pl.BlockSpec((None, 1, 1, 6 * D_MODEL), lambda b, layer: (layer[0], 0, 0, 0))
    in_specs, args = _mix_params(x, per_seq((1, T, D_MODEL)), mod_spec, mod_all, pw)
    widths = (ATT_Q, GLA_VW, POOL_W)
    n_in = 1 + len(args)
    aliases = {n_in + j: len(widths) + j for j in range(len(stacked))}
    in_specs += [pl.BlockSpec(memory_space=pl.ANY)] * len(stacked)
    args += list(stacked)
    at_layer = lambda blk: pl.BlockSpec((1, None) + blk, lambda b, layer: (b, layer[0]) + (0,) * len(blk))
    out_specs = [per_seq((1, T, w)) for w in widths] + [
        at_layer((T, ATT_KV)), at_layer((T, ATT_KV)), at_layer((2, GLA_HEADS // 2, 2 * GLA_DK, GLA_DV))]
    out_shape = ([jax.ShapeDtypeStruct((B, T, w), _BF16) for w in widths]
                 + [jax.ShapeDtypeStruct(a.shape, a.dtype) for a in stacked])
    return pl.pallas_call(
        functools.partial(_mix_ctx_kernel, T),
        grid_spec=pltpu.PrefetchScalarGridSpec(
            num_scalar_prefetch=1, grid=(B,), in_specs=in_specs, out_specs=out_specs,
            scratch_shapes=_mix_scratch(T, T)),
        out_shape=out_shape,
        input_output_aliases=aliases,
        compiler_params=pltpu.CompilerParams(dimension_semantics=("arbitrary",), vmem_limit_bytes=VMEM_LIMIT),
        name="mix_ctx",
    )(layer, *args)


def _post_kernel(layer_ref, x_ref, mod_ref, oa_ref, ob_ref, oc_ref, gn1_ref, gn2_ref, wgate_ref, wa_ref, wb_ref,
                 wc_ref, wout_ref, wfg_ref, wfu_ref, wfd_ref, out_ref):
    x = _merge(x_ref[...], mod_ref, oa_ref[...], ob_ref[...], oc_ref[...], gn1_ref, wgate_ref, wa_ref, wb_ref,
               wc_ref, wout_ref)
    out_ref[...] = _ffn(x, mod_ref, gn2_ref, wfg_ref, wfu_ref, wfd_ref)


def _post_call(layer, x2d, mod_all, oa, ob, oc, pw, tiles_per_seq):
    row = lambda w: pl.BlockSpec((POST_TILE, w), lambda i, layer: (i, 0))
    if tiles_per_seq is None:
        mod_spec = pl.BlockSpec((None, 1, 1, 6 * D_MODEL), lambda i, layer: (layer[0], 0, 0, 0))
    else:
        mod_spec = pl.BlockSpec((None, 1, 1, 6 * D_MODEL),
                                lambda i, layer: (layer[0], 1 + i // tiles_per_seq, 0, 0))
    weights = [(pw["g_norm1"], (1, D_MODEL)), (pw["g_norm2"], (1, D_MODEL)), (pw["w_gates"], (D_MODEL, GATE_W)),
               (pw["w_br_a"], (ATT_Q, D_MODEL)), (pw["w_br_b"], (GLA_VW, D_MODEL)),
               (pw["w_br_c"], (POOL_W, D_MODEL)), (pw["w_out"], (D_MODEL, D_MODEL)),
               (pw["w_ff_gate"], (D_MODEL, D_FF)), (pw["w_ff_up"], (D_MODEL, D_FF)),
               (pw["w_ff_down"], (D_FF, D_MODEL))]
    in_specs = ([row(D_MODEL), mod_spec, row(ATT_Q), row(GLA_VW), row(POOL_W)]
                + [_layer_spec(shape) for _, shape in weights])
    return pl.pallas_call(
        _post_kernel,
        grid_spec=pltpu.PrefetchScalarGridSpec(
            num_scalar_prefetch=1, grid=(x2d.shape[0] // POST_TILE,), in_specs=in_specs, out_specs=row(D_MODEL)),
        out_shape=jax.ShapeDtypeStruct(x2d.shape, _F32),
        input_output_aliases={1: 0},
        compiler_params=pltpu.CompilerParams(dimension_semantics=("arbitrary",), vmem_limit_bytes=VMEM_LIMIT),
        name="post",
    )(layer, x2d, mod_all, oa, ob, oc, *[a for a, _ in weights])


def _rope_tables(T):
    quarter = HEAD_DIM // 4
    inv_freq = ROPE_BASE ** (-np.arange(quarter, dtype=np.float32) / quarter)
    pos = np.arange(T)
    ang_row = (pos // GRID_W).astype(np.float32)[:, None] * inv_freq[None, :]
    ang_col = (pos % GRID_W).astype(np.float32)[:, None] * inv_freq[None, :]
    cos = np.concatenate([np.cos(ang_row)] * 2 + [np.cos(ang_col)] * 2, axis=-1)
    sin = np.concatenate([-np.sin(ang_row), np.sin(ang_row), -np.sin(ang_col), np.sin(ang_col)], axis=-1)
    return (jnp.asarray(np.tile(cos, (1, ATT_KV_HEADS)), _F32), jnp.asarray(np.tile(sin, (1, ATT_KV_HEADS)), _F32))


def _prepare_weights(w_in, g_qn, g_kn, att_sink, w_gate2, b_gate2, g_gla_out, w_pool, pool_scale, w_br_a,
                     w_br_b, w_br_c, w_out, g_norm1, g_norm2, w_ff_gate, w_ff_up, w_ff_down):
    o_gl = ATT_Q + 2 * ATT_KV + 2 * GLA_QK + 2 * GLA_VW
    o_uc = o_gl + 2 * GLA_RANK
    o_gate = o_uc + POOL_W
    assert o_gl == C_UC
    w_in = w_in.astype(_BF16)
    w_tail = jnp.concatenate(
        [w_in[:, :, o_uc:o_gate], w_in[:, :, o_gl:o_uc],
         jnp.zeros((DEPTH, D_MODEL, GL_PAD - 2 * GLA_RANK), w_in.dtype)], axis=2)
    w_gates = w_in[:, :, o_gate:]
    wg2 = jnp.zeros((DEPTH, GL_PAD, 2 * GLA_QK), _F32)
    wg2 = wg2.at[:, 0:GLA_RANK, 0:GLA_QK].set(w_gate2[:, 0])
    wg2 = wg2.at[:, GLA_RANK:2 * GLA_RANK, GLA_QK:].set(w_gate2[:, 1])
    wg2_hi = wg2.astype(_BF16)
    wg2 = jnp.stack([wg2_hi, (wg2 - wg2_hi.astype(_F32)).astype(_BF16)], axis=1)
    vec = lambda a: a.reshape(DEPTH, 1, -1)
    group_ones = lambda n, width: jnp.asarray(
        (np.arange(n)[:, None] // width) == (np.arange(n)[None, :] // width), _BF16)
    return {
        "head_ones": group_ones(ATT_Q, HEAD_DIM),
        "gla_ones": group_ones(GLA_VW, GLA_DV),
        "w_in": w_in,
        "w_tail": w_tail,
        "w_gates": w_gates,
        "g_qn": vec(jnp.tile(g_qn, (1, ATT_HEADS))),
        "g_kn": vec(jnp.tile(g_kn, (1, ATT_KV_HEADS))),
        "att_sink": att_sink,
        "w_gate2": wg2,
        "b_gate2": vec(b_gate2),
        "g_gla_out": vec(jnp.tile(g_gla_out, (1, GLA_HEADS))),
        "w_pool": w_pool.astype(_BF16),
        "pool_scale": vec(pool_scale),
        "w_br_a": w_br_a.astype(_BF16),
        "w_br_b": w_br_b.astype(_BF16),
        "w_br_c": w_br_c.astype(_BF16),
        "w_out": w_out.astype(_BF16),
        "g_norm1": vec(g_norm1),
        "g_norm2": vec(g_norm2),
        "w_ff_gate": w_ff_gate.astype(_BF16),
        "w_ff_up": w_ff_up.astype(_BF16),
        "w_ff_down": w_ff_down.astype(_BF16),
    }


def kernel(x_prompt, x_sample, c, cache_k, cache_v, state_gla, c_ctx, w_in, g_qn, g_kn, att_sink, w_gate2,
           b_gate2, g_gla_out, w_pool, pool_scale, w_br_a, w_br_b, w_br_c, w_out, g_norm1, g_norm2, w_mod,
           b_mod, w_ff_gate, w_ff_up, w_ff_down):
    B, T, _ = x_prompt.shape
    BL, TL, _ = x_sample.shape
    assert T % POST_TILE == 0 and TL % POST_TILE == 0 and BL + 1 <= MOD_ROWS
    cv = jnp.concatenate([c_ctx[None, :], c, jnp.zeros((MOD_ROWS - 1 - BL, D_MODEL), _F32)], axis=0)
    mod_all = _modulation(cv, w_mod, b_mod).reshape(DEPTH, MOD_ROWS, 1, 6 * D_MODEL)
    pw = _prepare_weights(w_in, g_qn, g_kn, att_sink, w_gate2, b_gate2, g_gla_out, w_pool, pool_scale, w_br_a,
                          w_br_b, w_br_c, w_out, g_norm1, g_norm2, w_ff_gate, w_ff_up, w_ff_down)
    cos, sin = _rope_tables(TL)
    P = cache_k.shape[2]
    latent_ctx = (cache_k.reshape(BL, DEPTH, P, ATT_KV), cache_v.reshape(BL, DEPTH, P, ATT_KV),
                  jnp.swapaxes(state_gla.reshape(BL, DEPTH, 2, GLA_HEADS // 2, 2 * GLA_DK, GLA_DV), -1, -2),
                  cos, sin)

    def layer_step(l, carry):
        yp, ys, new_k, new_v, new_st = carry
        layer = jnp.full((1,), l, jnp.int32)
        oa, ob, oc, new_k, new_v, new_st = _mix_ctx_call(layer, yp, mod_all, pw, (new_k, new_v, new_st))
        yp = _post_call(layer, yp.reshape(B * T, D_MODEL), mod_all, oa.reshape(B * T, -1), ob.reshape(B * T, -1),
                        oc.reshape(B * T, -1), pw, None).reshape(B, T, D_MODEL)
        oa, ob, oc = _mix_latent_call(layer, ys, mod_all, pw, *latent_ctx)
        ys = _post_call(layer, ys.reshape(BL * TL, D_MODEL), mod_all, oa.reshape(BL * TL, -1),
                        ob.reshape(BL * TL, -1), oc.reshape(BL * TL, -1), pw,
                        TL // POST_TILE).reshape(BL, TL, D_MODEL)
        return yp, ys, new_k, new_v, new_st

    init = (x_prompt, x_sample,
            jnp.zeros((B, DEPTH, T, ATT_KV), _F32), jnp.zeros((B, DEPTH, T, ATT_KV), _F32),
            jnp.zeros((B, DEPTH, 2, GLA_HEADS // 2, 2 * GLA_DK, GLA_DV), _F32))
    yp, ys, new_k, new_v, new_st = lax.fori_loop(0, DEPTH, layer_step, init)
    return (yp, ys, new_k.reshape(B, DEPTH, T, ATT_KV_HEADS, HEAD_DIM),
            new_v.reshape(B, DEPTH, T, ATT_KV_HEADS, HEAD_DIM),
            new_st.reshape(B, DEPTH, 2, GLA_HEADS, GLA_DK, GLA_DV))
```

```python
import functools

import jax
import jax.numpy as jnp
import numpy as np
from jax import lax
from jax.experimental import pallas as pl
from jax.experimental.pallas import tpu as pltpu

D_MODEL = 1024
DEPTH = 4
GRID_W = 64
ATT_HEADS = 8
ATT_KV_HEADS = 2
ATT_GROUP = ATT_HEADS // ATT_KV_HEADS
HEAD_DIM = 64
WINDOW = 128
ATT_BLOCK = 128
ROPE_BASE = 10000.0
GLA_HEADS = 4
GLA_DK = 64
GLA_DV = 128
GLA_RANK = 16
GLA_TAU = 16.0
GLA_CHUNK = 64
POOL_GROUPS = 4
POOL_GROUP_DIM = 128
POOL_WINDOWS = (2, 4, 8, 16)
D_FF = 2816
ATT_Q = ATT_HEADS * HEAD_DIM
ATT_KV = ATT_KV_HEADS * HEAD_DIM
GLA_QK = GLA_HEADS * GLA_DK
GLA_VW = GLA_HEADS * GLA_DV
POOL_W = POOL_GROUPS * POOL_GROUP_DIM
EPS = 1e-6
NEG = -1e30

C_QA = 0
C_KA = C_QA + ATT_Q
C_VA = C_KA + ATT_KV
C_QB = C_VA + ATT_KV
C_KB = C_QB + GLA_QK
C_VB = C_KB + GLA_QK
C_RB = C_VB + GLA_VW
C_UC = C_RB + GLA_VW
C_GL = C_UC + POOL_W
GL_PAD = 128
MIX_W = C_GL + GL_PAD
GATE_W = 3 * D_MODEL

POST_TILE = 512
PROJ_TILE = 256
POOL_TILE = 256
POOL_HALO = 128
MOD_ROWS = 8
MOD_TILE = 1024
GLA_BLOCK = 256
GLA_MAX_EXPONENT = 80.0
VMEM_LIMIT = 56 * 1024 * 1024

_F32 = jnp.float32
_BF16 = jnp.bfloat16


def _dot(a, b):
    return jnp.dot(a, b, preferred_element_type=_F32)


def _dot_nt(a, b):
    return lax.dot_general(a, b, (((1,), (1,)), ((), ())), preferred_element_type=_F32)


def _dot_tn(a, b):
    return lax.dot_general(a, b, (((0,), (0,)), ((), ())), preferred_element_type=_F32)


def _rms_scale(x):
    return lax.rsqrt(jnp.mean(x * x, axis=-1, keepdims=True) + EPS)


def _group_rms_scale(x, group_ones, width):
    return lax.rsqrt(_dot((x * x).astype(_BF16), group_ones) * (1.0 / width) + EPS)


def _log_sigmoid(x):
    return jnp.minimum(x, 0.0) - jnp.log1p(jnp.exp(-jnp.abs(x)))


def _silu(x):
    return x * jax.nn.sigmoid(x)


def _rope(x, cos, sin_signed):
    n = x.shape[-1]
    lane = lax.broadcasted_iota(jnp.int32, x.shape, 1)
    up = pltpu.roll(x, n - HEAD_DIM // 4, axis=1)
    down = pltpu.roll(x, HEAD_DIM // 4, axis=1)
    partner = jnp.where((lane & (HEAD_DIM // 2 - 1)) < HEAD_DIM // 4, up, down)
    return x * cos + partner * sin_signed


def _mod_kernel(cv_ref, w_ref, b_ref, out_ref):
    s = _silu(cv_ref[...]).astype(_BF16)
    out_ref[0] = _dot(s, w_ref[0].astype(_BF16)) + b_ref[0]


def _modulation(cv, w_mod, b_mod):
    n_col = (6 * D_MODEL) // MOD_TILE
    return pl.pallas_call(
        _mod_kernel,
        grid=(DEPTH, n_col),
        in_specs=[
            pl.BlockSpec((MOD_ROWS, D_MODEL), lambda l, j: (0, 0)),
            pl.BlockSpec((1, D_MODEL, MOD_TILE), lambda l, j: (l, 0, j)),
            pl.BlockSpec((1, 1, MOD_TILE), lambda l, j: (l, 0, j)),
        ],
        out_specs=pl.BlockSpec((1, MOD_ROWS, MOD_TILE), lambda l, j: (l, 0, j)),
        out_shape=jax.ShapeDtypeStruct((DEPTH, MOD_ROWS, 6 * D_MODEL), _F32),
        name="modulation",
    )(cv, w_mod, b_mod.reshape(DEPTH, 1, 6 * D_MODEL))


def _split_heads(x):
    low = lax.broadcasted_iota(jnp.int32, x.shape, 1) < HEAD_DIM
    swapped = pltpu.roll(x, HEAD_DIM, axis=1)
    zero = jnp.zeros_like(x)
    return ((jnp.where(low, x, zero), jnp.where(low, zero, swapped)),
            (jnp.where(low, swapped, zero), jnp.where(low, zero, x)))


def _store_split_kv(k, v, ks_ref, vs_ref, rows):
    ones = jnp.ones_like(v)
    for kv, (k_sides, v_sides, one_sides) in enumerate(zip(_split_heads(k), _split_heads(v), _split_heads(ones))):
        for side in range(2):
            ks_ref[kv, side, rows, :] = k_sides[side].astype(_BF16)
            vs_ref[kv, side, rows, :] = jnp.concatenate([v_sides[side], one_sides[side]], axis=-1).astype(_BF16)


def _pair_softmax_av(qp, keys, values, masks, sink_even, sink_odd):
    m = qp.shape[0]
    scores = []
    for (k_left, k_right), mask in zip(keys, masks):
        s_even, s_odd = _dot_nt(qp, k_left), _dot_nt(qp, k_right)
        if mask is not None:
            s_even, s_odd = jnp.where(mask, s_even, NEG), jnp.where(mask, s_odd, NEG)
        scores.append((s_even, s_odd))
    m_even = jnp.full((m, 1), sink_even, _F32)
    m_odd = jnp.full((m, 1), sink_odd, _F32)
    for s_even, s_odd in scores:
        m_even = jnp.maximum(m_even, jnp.max(s_even, axis=-1, keepdims=True))
        m_odd = jnp.maximum(m_odd, jnp.max(s_odd, axis=-1, keepdims=True))
    res = None
    for (s_even, s_odd), (w_left, w_right) in zip(scores, values):
        r = (_dot(jnp.exp(s_even - m_even).astype(_BF16), w_left)
             + _dot(jnp.exp(s_odd - m_odd).astype(_BF16), w_right))
        res = r if res is None else res + r
    pair = 2 * HEAD_DIM
    low = lax.broadcasted_iota(jnp.int32, (m, pair), 1) < HEAD_DIM
    den = res[:, pair:] + jnp.where(low, jnp.exp(sink_even - m_even), jnp.exp(sink_odd - m_odd))
    return res[:, :pair] / den


def _attention_ctx(T, layer, qr_ref, ks_ref, vs_ref, sink_ref, oa_ref):
    pair = 2 * HEAD_DIM
    for kv in range(ATT_KV_HEADS):
        keys = [(ks_ref[kv, 0], ks_ref[kv, 1])]
        values = [(vs_ref[kv, 0], vs_ref[kv, 1])]
        for j in range(ATT_GROUP // 2):
            head = kv * ATT_GROUP + 2 * j
            cols = slice(head * HEAD_DIM, head * HEAD_DIM + pair)
            o = _pair_softmax_av(qr_ref[:, cols], keys, values, [None], sink_ref[layer, head],
                                 sink_ref[layer, head + 1])
            oa_ref[0, :, cols] = o.astype(_BF16)


def _attention_latent(T, layer, qr_ref, ks_ref, vs_ref, kc_ref, vc_ref, sink_ref, oa_ref):
    pair = 2 * HEAD_DIM
    span = 3 * ATT_BLOCK
    kc, vc = kc_ref[0, 0], vc_ref[0, 0]
    ones = jnp.ones_like(vc)
    ctx_keys = [tuple(side.astype(_BF16) for side in sides) for sides in _split_heads(kc)]
    ctx_values = [tuple(jnp.concatenate([v_side, one_side], axis=-1).astype(_BF16)
                        for v_side, one_side in zip(v_sides, one_sides))
                  for v_sides, one_sides in zip(_split_heads(vc), _split_heads(ones))]

    def block(i, carry):
        q_rows = pl.ds(pl.multiple_of(i * ATT_BLOCK, ATT_BLOCK), ATT_BLOCK)
        k_rows = pl.ds(pl.multiple_of(i * ATT_BLOCK, ATT_BLOCK), span)
        q_pos = i * ATT_BLOCK + lax.broadcasted_iota(jnp.int32, (ATT_BLOCK, span), 0)
        k_pos = (i - 1) * ATT_BLOCK + lax.broadcasted_iota(jnp.int32, (ATT_BLOCK, span), 1)
        valid = (jnp.abs(k_pos - q_pos) <= WINDOW) & (k_pos >= 0) & (k_pos < T)
        for kv in range(ATT_KV_HEADS):
            keys = [(ks_ref[kv, 0, k_rows, :], ks_ref[kv, 1, k_rows, :]), ctx_keys[kv]]
            values = [(vs_ref[kv, 0, k_rows, :], vs_ref[kv, 1, k_rows, :]), ctx_values[kv]]
            for j in range(ATT_GROUP // 2):
                head = kv * ATT_GROUP + 2 * j
                cols = slice(head * HEAD_DIM, head * HEAD_DIM + pair)
                o = _pair_softmax_av(qr_ref[q_rows, cols], keys, values, [valid, None], sink_ref[layer, head],
                                     sink_ref[layer, head + 1])
                oa_ref[0, q_rows, cols] = o.astype(_BF16)
        return carry

    lax.fori_loop(0, T // ATT_BLOCK, block, 0)


def _gla_chunk(z_ref, la_ref, o_ref, st_ref, a_ref, direction, start):
    C = GLA_CHUNK
    rows = pl.ds(pl.multiple_of(start, C), C)
    q = z_ref[rows, C_QB:C_QB + GLA_QK] * (GLA_DK ** -0.5)
    k = z_ref[rows, C_KB:C_KB + GLA_QK]
    v = z_ref[rows, C_VB:C_VB + GLA_VW].astype(_BF16)
    la = la_ref[rows, direction * GLA_QK:(direction + 1) * GLA_QK]
    la_hi = la.astype(_BF16)
    la_lo = (la - la_hi.astype(_F32)).astype(_BF16)
    ti = lax.broadcasted_iota(jnp.int32, (C, C), 0)
    si = lax.broadcasted_iota(jnp.int32, (C, C), 1)
    causal = (si <= ti) if direction == 0 else (si >= ti)
    tri = jnp.where(causal, 1.0, 0.0).astype(_BF16)
    b = _dot(tri, la_hi) + _dot(tri, la_lo)
    end = C - 1 if direction == 0 else 0
    b_end = b[end:end + 1]
    q_in = (q * jnp.exp(b)).astype(_BF16)
    k_out = (k * jnp.exp(b_end - b)).astype(_BF16)
    e_end = jnp.exp(b_end)
    ones = jnp.ones((8, GLA_DK), _BF16)

    def row_group(g, carry):
        base = pl.multiple_of(g * 8, 8)
        b8 = a_ref[GLA_HEADS, pl.ds(base, 8), :]
        q8 = a_ref[GLA_HEADS + 1, pl.ds(base, 8), :]
        s_idx = lax.broadcasted_iota(jnp.int32, (C, 1), 0)
        rows_h = [[] for _ in range(GLA_HEADS)]
        for j in range(8):
            ok = (s_idx <= base + j) if direction == 0 else (s_idx >= base + j)
            decay = jnp.exp(jnp.where(ok, b8[j:j + 1] - b, NEG))
            p = (q8[j:j + 1] * k * decay).astype(_BF16)
            for h in range(GLA_HEADS):
                rows_h[h].append(_dot_nt(ones, p[:, h * GLA_DK:(h + 1) * GLA_DK])[0:1])
        for h in range(GLA_HEADS):
            a_ref[h, pl.ds(base, 8), 0:C] = jnp.concatenate(rows_h[h], axis=0)
        return carry

    a_ref[GLA_HEADS] = b
    a_ref[GLA_HEADS + 1] = q
    lax.fori_loop(0, C // 8, row_group, 0)
    outs = []
    for pair in range(GLA_HEADS // 2):
        s_pair = st_ref[direction, pair]
        s_next = []
        for h in (2 * pair, 2 * pair + 1):
            kc = slice(h * GLA_DK, (h + 1) * GLA_DK)
            vc = slice(h * GLA_DV, (h + 1) * GLA_DV)
            s_t = s_pair[:, (h % 2) * GLA_DK:(h % 2 + 1) * GLA_DK]
            outs.append(_dot_nt(q_in[:, kc], s_t.astype(_BF16)) + _dot(a_ref[h, :, 0:C].astype(_BF16), v[:, vc]))
            s_next.append(s_t * e_end[:, kc] + _dot_tn(v[:, vc], k_out[:, kc]))
        st_ref[direction, pair] = jnp.concatenate(s_next, axis=-1)
    o_ref[rows, :] += jnp.concatenate(outs, axis=-1)


def _gla_block(z_ref, la_ref, o_ref, st_ref, direction, start, use_state):
    NB = GLA_BLOCK
    fwd = direction == 0
    rows = slice(start, start + NB) if isinstance(start, int) else pl.ds(pl.multiple_of(start, NB), NB)
    q = z_ref[rows, C_QB:C_QB + GLA_QK] * (GLA_DK ** -0.5)
    k = z_ref[rows, C_KB:C_KB + GLA_QK]
    v = z_ref[rows, C_VB:C_VB + GLA_VW].astype(_BF16)
    b = la_ref[rows, direction * GLA_QK:(direction + 1) * GLA_QK]
    end, mid = (NB - 1, NB // 2 - 1) if fwd else (0, NB // 2)
    b_end = b[end:end + 1]
    c = b - b[mid:mid + 1]
    q_c = (q * jnp.exp(c)).astype(_BF16)
    k_c = (k * jnp.exp(-c)).astype(_BF16)
    k_fin = (k * jnp.exp(b_end - b)).astype(_BF16)
    if use_state:
        q_in = (q * jnp.exp(b)).astype(_BF16)
        e_all = jnp.exp(b_end)
    ti = lax.broadcasted_iota(jnp.int32, (NB, NB), 0)
    si = lax.broadcasted_iota(jnp.int32, (NB, NB), 1)
    causal = (si <= ti) if fwd else (si >= ti)
    tile = 2 * GLA_DK
    low = lax.broadcasted_iota(jnp.int32, (NB, tile), 1) < GLA_DK
    zero = jnp.zeros((NB, tile), _BF16)
    pick = lambda x, parity: jnp.where(low, x, zero) if parity == 0 else jnp.where(low, zero, x)
    outs = []
    for pair in range(GLA_HEADS // 2):
        lanes = slice(pair * tile, (pair + 1) * tile)
        if use_state:
            s_pair = st_ref[direction, pair]
            s_bf = s_pair.astype(_BF16)
        s_new = None
        for parity in range(2):
            h = 2 * pair + parity
            vc = slice(h * GLA_DV, (h + 1) * GLA_DV)
            a = jnp.where(causal, _dot_nt(q_c[:, lanes], pick(k_c[:, lanes], parity)), 0.0)
            o_h = _dot(a.astype(_BF16), v[:, vc])
            if use_state:
                o_h = o_h + _dot_nt(pick(q_in[:, lanes], parity), s_bf)
            outs.append(o_h)
            upd = _dot_tn(v[:, vc], pick(k_fin[:, lanes], parity))
            s_new = upd if s_new is None else s_new + upd
        if use_state:
            s_new = s_new + s_pair * e_all[:, lanes]
        st_ref[direction, pair] = s_new
    o_ref[rows, :] += jnp.concatenate(outs, axis=-1)


def _log_decay(z_ref, rows, wg2_ref, bg2_ref):
    x = z_ref[rows, C_GL:C_GL + GL_PAD]
    x_hi = x.astype(_BF16)
    x_lo = (x - x_hi.astype(_F32)).astype(_BF16)
    w_hi, w_lo = wg2_ref[0], wg2_ref[1]
    pre = _dot(x_hi, w_hi) + (_dot(x_lo, w_hi) + _dot(x_hi, w_lo)) + bg2_ref[...]
    return _log_sigmoid(pre) * (1.0 / GLA_TAU)


def _gla(T, z_ref, la_ref, o_ref, st_ref, a_ref, wg2_ref, bg2_ref, use_state):
    NB = GLA_BLOCK
    n_chunks = T // GLA_CHUNK
    n_blocks = T // NB
    ti = lax.broadcasted_iota(jnp.int32, (NB, NB), 0)
    si = lax.broadcasted_iota(jnp.int32, (NB, NB), 1)
    tri = [jnp.where(si <= ti, 1.0, 0.0).astype(_BF16), jnp.where(si >= ti, 1.0, 0.0).astype(_BF16)]
    worst = None
    for r0 in range(0, T, NB):
        rows = slice(r0, r0 + NB)
        la = _log_decay(z_ref, rows, wg2_ref, bg2_ref)
        la_hi = la.astype(_BF16)
        la_lo = (la - la_hi.astype(_F32)).astype(_BF16)
        for d in range(2):
            cols = slice(d * GLA_QK, (d + 1) * GLA_QK)
            b = _dot(tri[d], la_hi[:, cols]) + _dot(tri[d], la_lo[:, cols])
            la_ref[rows, cols] = b
            first, mid, last = (0, NB // 2 - 1, NB - 1) if d == 0 else (NB - 1, NB // 2, 0)
            span = jnp.max(jnp.maximum(b[first:first + 1] - b[mid:mid + 1], b[mid:mid + 1] - b[last:last + 1]))
            worst = span if worst is None else jnp.maximum(worst, span)
    fast_ok = worst <= GLA_MAX_EXPONENT
    o_ref[...] = jnp.zeros(o_ref.shape, _F32)

    @pl.when(fast_ok)
    def _():
        if n_blocks == 1:
            for direction in range(2):
                _gla_block(z_ref, la_ref, o_ref, st_ref, direction, 0, use_state)
        else:
            def body(i, carry):
                _gla_block(z_ref, la_ref, o_ref, st_ref, 0, i * NB, True)
                _gla_block(z_ref, la_ref, o_ref, st_ref, 1, (n_blocks - 1 - i) * NB, True)
                return carry
            lax.fori_loop(0, n_blocks, body, 0)

    @pl.when(jnp.logical_not(fast_ok))
    def _():
        for r0 in range(0, T, NB):
            la_ref[r0:r0 + NB, :] = _log_decay(z_ref, slice(r0, r0 + NB), wg2_ref, bg2_ref)

        def body(i, carry):
            _gla_chunk(z_ref, la_ref, o_ref, st_ref, a_ref, 0, i * GLA_CHUNK)
            _gla_chunk(z_ref, la_ref, o_ref, st_ref, a_ref, 1, (n_chunks - 1 - i) * GLA_CHUNK)
            return carry
        lax.fori_loop(0, n_chunks, body, 0)


def _gla_finish(T, z_ref, o_ref, ggla_ref, vones_ref, ob_ref):
    for r0 in range(0, T, PROJ_TILE):
        rows = slice(r0, r0 + PROJ_TILE)
        o = o_ref[rows, :]
        y = o * _group_rms_scale(o, vones_ref[...], GLA_DV) * ggla_ref[...]
        ob_ref[0, rows, :] = (y * _silu(z_ref[rows, C_RB:C_RB + GLA_VW])).astype(_BF16)


def _pool_halo(T):
    return POOL_HALO if T > POOL_TILE else 0


def _pool(T, z_ref, upad_ref, wpool_ref, pscale_ref, oc_ref):
    halo = _pool_halo(T)
    if halo:
        zeros = jnp.zeros((halo, POOL_W), _BF16)
        upad_ref[0:halo, :] = zeros
        upad_ref[halo + T:halo + T + halo, :] = zeros
    upad_ref[halo:halo + T, :] = z_ref[:, C_UC:C_UC + POOL_W].astype(_BF16)
    span = POOL_TILE + 2 * halo
    r = lax.broadcasted_iota(jnp.int32, (POOL_TILE, span), 0)
    c = lax.broadcasted_iota(jnp.int32, (POOL_TILE, span), 1)
    off = c - halo - r
    for jb in range(T // POOL_TILE):
        t = jb * POOL_TILE + lax.broadcasted_iota(jnp.int32, (POOL_TILE, 1), 0)
        parts = []
        for g, w in enumerate(POOL_WINDOWS):
            cols = slice(g * POOL_GROUP_DIM, (g + 1) * POOL_GROUP_DIM)
            band = jnp.where((off >= -(w // 2)) & (off < w - w // 2), 1.0, 0.0).astype(_BF16)
            win = slice(jb * POOL_TILE, jb * POOL_TILE + span)
            total = _dot(band, upad_ref[win, cols])
            cnt = (jnp.minimum(t - w // 2 + w, T) - jnp.maximum(t - w // 2, 0)).astype(_F32)
            pooled = total / cnt - z_ref[jb * POOL_TILE:(jb + 1) * POOL_TILE, C_UC + g * POOL_GROUP_DIM:
                                         C_UC + (g + 1) * POOL_GROUP_DIM]
            parts.append(_dot(pooled.astype(_BF16), wpool_ref[g]))
        y = jnp.concatenate(parts, axis=-1) * pscale_ref[...]
        oc_ref[0, jb * POOL_TILE:(jb + 1) * POOL_TILE, :] = y.astype(_BF16)


N_MIX_PARAMS = 15
N_MIX_SCRATCH = 9


def _mix_body(latent, T, layer, params, latent_refs, out_refs, scratch):
    (x_ref, mod_ref, gn1_ref, wmain_ref, wtail_ref, gqn_ref, gkn_ref, sink_ref, wg2_ref, bg2_ref, ggla_ref,
     wpool_ref, pscale_ref, hones_ref, vones_ref) = params
    if latent:
        kc_ref, vc_ref, st0_ref, cos_ref, sin_ref = latent_refs
        oa_ref, ob_ref, oc_ref = out_refs
    else:
        oa_ref, ob_ref, oc_ref, kout_ref, vout_ref, stout_ref = out_refs
    z_ref, qr_ref, ks_ref, vs_ref, la_ref, o_ref, st_ref, a_ref, upad_ref = scratch
    pad = ATT_BLOCK if latent else 0

    if latent:
        for ref in (ks_ref, vs_ref):
            zeros = jnp.zeros(ref.shape[:2] + (pad, ref.shape[3]), _BF16)
            ref[:, :, 0:pad, :] = zeros
            ref[:, :, pad + T:pad + T + pad, :] = zeros
    shift = mod_ref[0, :, 0:D_MODEL]
    scale = mod_ref[0, :, D_MODEL:2 * D_MODEL]
    for r0 in range(0, T, PROJ_TILE):
        rows = slice(r0, r0 + PROJ_TILE)
        x = x_ref[0, rows, :]
        hn = (x * _rms_scale(x) * gn1_ref[...]) * (1.0 + scale) + shift
        hn = hn.astype(_BF16)
        z_ref[rows, 0:C_UC] = _dot(hn, wmain_ref[...])
        z_ref[rows, C_UC:MIX_W] = _dot(hn, wtail_ref[...])
        q = z_ref[rows, C_QA:C_QA + ATT_Q]
        k = z_ref[rows, C_KA:C_KA + ATT_KV]
        q = q * _group_rms_scale(q, hones_ref[...], HEAD_DIM) * gqn_ref[...]
        k = k * _group_rms_scale(k, hones_ref[0:ATT_KV, 0:ATT_KV], HEAD_DIM) * gkn_ref[...]
        v = z_ref[rows, C_VA:C_VA + ATT_KV]
        if latent:
            cos = jnp.concatenate([cos_ref[rows, :]] * (ATT_Q // ATT_KV), axis=-1)
            sin = jnp.concatenate([sin_ref[rows, :]] * (ATT_Q // ATT_KV), axis=-1)
            q = _rope(q, cos, sin)
            k = _rope(k, cos_ref[rows, :], sin_ref[rows, :])
        else:
            kout_ref[0, rows, :] = k
            vout_ref[0, rows, :] = v
        qr_ref[rows, :] = (q * (HEAD_DIM ** -0.5)).astype(_BF16)
        _store_split_kv(k, v, ks_ref, vs_ref, slice(pad + r0, pad + r0 + PROJ_TILE))
    if latent:
        _attention_latent(T, layer, qr_ref, ks_ref, vs_ref, kc_ref, vc_ref, sink_ref, oa_ref)
    else:
        _attention_ctx(T, layer, qr_ref, ks_ref, vs_ref, sink_ref, oa_ref)

    if latent:
        st_ref[...] = st0_ref[0]
    else:
        st_ref[...] = jnp.zeros(st_ref.shape, _F32)
    _gla(T, z_ref, la_ref, o_ref, st_ref, a_ref, wg2_ref, bg2_ref, latent)
    _gla_finish(T, z_ref, o_ref, ggla_ref, vones_ref, ob_ref)
    if not latent:
        for d in range(2):
            for pair in range(GLA_HEADS // 2):
                stout_ref[0, d, pair] = st_ref[d, pair].T

    _pool(T, z_ref, upad_ref, wpool_ref, pscale_ref, oc_ref)


def _merge(x, mod_ref, oa, ob, oc, gn1_ref, wgate_ref, wa_ref, wb_ref, wc_ref, wout_ref):
    mod = lambda i: mod_ref[0, :, i * D_MODEL:(i + 1) * D_MODEL]
    hn = (x * _rms_scale(x) * gn1_ref[...]) * (1.0 + mod(1)) + mod(0)
    gates = jax.nn.sigmoid(_dot(hn.astype(_BF16), wgate_ref[...]))
    mixed = (gates[:, 0:D_MODEL] * _dot(oa, wa_ref[...])
             + gates[:, D_MODEL:2 * D_MODEL] * _dot(ob, wb_ref[...])
             + gates[:, 2 * D_MODEL:3 * D_MODEL] * _dot(oc, wc_ref[...]))
    return x + mod(2) * _dot(mixed.astype(_BF16), wout_ref[...])


def _ffn(x, mod_ref, gn2_ref, wfg_ref, wfu_ref, wfd_ref):
    mod = lambda i: mod_ref[0, :, i * D_MODEL:(i + 1) * D_MODEL]
    hn = ((x * _rms_scale(x) * gn2_ref[...]) * (1.0 + mod(4)) + mod(3)).astype(_BF16)
    h = _silu(_dot(hn, wfg_ref[...])) * _dot(hn, wfu_ref[...])
    return x + mod(5) * _dot(h.astype(_BF16), wfd_ref[...])


def _mix_latent_kernel(T, layer_ref, *refs):
    params, refs = refs[:N_MIX_PARAMS], refs[N_MIX_PARAMS:]
    _mix_body(True, T, layer_ref[0], params, refs[:5], refs[5:8], refs[8:])


def _mix_ctx_kernel(T, layer_ref, *refs):
    params, refs = refs[:N_MIX_PARAMS], refs[N_MIX_PARAMS + 3:]
    _mix_body(False, T, layer_ref[0], params, None, refs[:6], refs[6:])


def _layer_spec(shape):
    zeros = (0,) * len(shape)
    return pl.BlockSpec((None,) + tuple(shape), lambda i, layer: (layer[0],) + zeros,
                        pipeline_mode=pl.Buffered(1))


def _const_spec(blk):
    return pl.BlockSpec(blk, lambda i, layer: (0,) * len(blk), pipeline_mode=pl.Buffered(1))


def _mix_params(x, x_spec, mod_spec, mod_all, pw):
    specs = [
        x_spec, mod_spec,
        _layer_spec((1, D_MODEL)),
        _layer_spec((D_MODEL, C_UC)),
        _layer_spec((D_MODEL, MIX_W - C_UC)),
        _layer_spec((1, ATT_Q)), _layer_spec((1, ATT_KV)),
        pl.BlockSpec(memory_space=pltpu.SMEM),
        _layer_spec((2, GL_PAD, 2 * GLA_QK)), _layer_spec((1, 2 * GLA_QK)), _layer_spec((1, GLA_VW)),
        _layer_spec((POOL_GROUPS, POOL_GROUP_DIM, POOL_GROUP_DIM)), _layer_spec((1, POOL_W)),
        _const_spec((ATT_Q, ATT_Q)), _const_spec((GLA_VW, GLA_VW)),
    ]
    args = [x, mod_all, pw["g_norm1"], pw["w_in"], pw["w_tail"], pw["g_qn"], pw["g_kn"], pw["att_sink"], pw["w_gate2"],
            pw["b_gate2"], pw["g_gla_out"], pw["w_pool"], pw["pool_scale"], pw["head_ones"], pw["gla_ones"]]
    assert len(specs) == len(args) == N_MIX_PARAMS
    return specs, args


def _mix_scratch(T, kv_rows):
    scratch = [
        pltpu.VMEM((T, MIX_W), _F32),
        pltpu.VMEM((T, ATT_Q), _BF16),
        pltpu.VMEM((ATT_KV_HEADS, 2, kv_rows, 2 * HEAD_DIM), _BF16),
        pltpu.VMEM((ATT_KV_HEADS, 2, kv_rows, 4 * HEAD_DIM), _BF16),
        pltpu.VMEM((T, 2 * GLA_QK), _F32),
        pltpu.VMEM((T, GLA_VW), _F32),
        pltpu.VMEM((2, GLA_HEADS // 2, GLA_DV, 2 * GLA_DK), _F32),
        pltpu.VMEM((GLA_HEADS + 2, GLA_CHUNK, GLA_QK), _F32),
        pltpu.VMEM((T + 2 * _pool_halo(T), POOL_W), _BF16),
    ]
    assert len(scratch) == N_MIX_SCRATCH
    return scratch


def _mix_latent_call(layer, x, mod_all, pw, cache_k, cache_v, st0, cos, sin):
    B, T, _ = x.shape
    per_seq = lambda blk: pl.BlockSpec(blk, lambda b, layer: (b,) + (0,) * (len(blk) - 1))
    mod_spec = pl.BlockSpec((None, 1, 1, 6 * D_MODEL), lambda b, layer: (layer[0], b + 1, 0, 0))
    x_spec = pl.BlockSpec((1, T, D_MODEL), lambda b, layer: (b, 0, 0), pipeline_mode=pl.Buffered(1))
    in_specs, args = _mix_params(x, x_spec, mod_spec, mod_all, pw)
    P = cache_k.shape[2]
    cache_spec = pl.BlockSpec((1, 1, P, ATT_KV), lambda b, layer: (b, layer[0], 0, 0))
    in_specs += [cache_spec, cache_spec,
                 pl.BlockSpec((1, None, 2, GLA_HEADS // 2, GLA_DV, 2 * GLA_DK),
                              lambda b, layer: (b, layer[0], 0, 0, 0, 0)),
                 _const_spec((T, ATT_KV)), _const_spec((T, ATT_KV))]
    args += [cache_k, cache_v, st0, cos, sin]
    widths = (ATT_Q, GLA_VW, POOL_W)
    return pl.pallas_call(
        functools.partial(_mix_latent_kernel, T),
        grid_spec=pltpu.PrefetchScalarGridSpec(
            num_scalar_prefetch=1, grid=(B,), in_specs=in_specs,
            out_specs=[per_seq((1, T, w)) for w in widths],
            scratch_shapes=_mix_scratch(T, T + 2 * ATT_BLOCK)),
        out_shape=[jax.ShapeDtypeStruct((B, T, w), _BF16) for w in widths],
        compiler_params=pltpu.CompilerParams(dimension_semantics=("arbitrary",), vmem_limit_bytes=VMEM_LIMIT),
        name="mix_latent",
    )(layer, *args)


def _mix_ctx_call(layer, x, mod_all, pw, stacked):
    B, T, _ = x.shape
    per_seq = lambda blk: pl.BlockSpec(blk, lambda b, layer: (b,) + (0,) * (len(blk) - 1))
    mod_spec = pl.BlockSpec((None, 1, 1, 6 * D_MODEL), lambda b, layer: (layer[0], 0, 0, 0))
    in_specs, args = _mix_params(x, per_seq((1, T, D_MODEL)), mod_spec, mod_all, pw)
    widths = (ATT_Q, GLA_VW, POOL_W)
    n_in = 1 + len(args)
    aliases = {n_in + j: len(widths) + j for j in range(len(stacked))}
    in_specs += [pl.BlockSpec(memory_space=pl.ANY)] * len(stacked)
    args += list(stacked)
    at_layer = lambda blk: pl.BlockSpec((1, None) + blk, lambda b, layer: (b, layer[0]) + (0,) * len(blk))
    out_specs = [per_seq((1, T, w)) for w in widths] + [
        at_layer((T, ATT_KV)), at_layer((T, ATT_KV)), at_layer((2, GLA_HEADS // 2, 2 * GLA_DK, GLA_DV))]
    out_shape = ([jax.ShapeDtypeStruct((B, T, w), _BF16) for w in widths]
                 + [jax.ShapeDtypeStruct(a.shape, a.dtype) for a in stacked])
    return pl.pallas_call(
        functools.partial(_mix_ctx_kernel, T),
        grid_spec=pltpu.PrefetchScalarGridSpec(
            num_scalar_prefetch=1, grid=(B,), in_specs=in_specs, out_specs=out_specs,
            scratch_shapes=_mix_scratch(T, T)),
        out_shape=out_shape,
        input_output_aliases=aliases,
        compiler_params=pltpu.CompilerParams(dimension_semantics=("arbitrary",), vmem_limit_bytes=VMEM_LIMIT),
        name="mix_ctx",
    )(layer, *args)


def _post_kernel(layer_ref, x_ref, mod_ref, oa_ref, ob_ref, oc_ref, gn1_ref, gn2_ref, wgate_ref, wa_ref, wb_ref,
                 wc_ref, wout_ref, wfg_ref, wfu_ref, wfd_ref, out_ref):
    x = _merge(x_ref[...], mod_ref, oa_ref[...], ob_ref[...], oc_ref[...], gn1_ref, wgate_ref, wa_ref, wb_ref,
               wc_ref, wout_ref)
    out_ref[...] = _ffn(x, mod_ref, gn2_ref, wfg_ref, wfu_ref, wfd_ref)


def _post_call(layer, x2d, mod_all, oa, ob, oc, pw, tiles_per_seq):
    row = lambda w: pl.BlockSpec((POST_TILE, w), lambda i, layer: (i, 0))
    if tiles_per_seq is None:
        mod_spec = pl.BlockSpec((None, 1, 1, 6 * D_MODEL), lambda i, layer: (layer[0], 0, 0, 0))
    else:
        mod_spec = pl.BlockSpec((None, 1, 1, 6 * D_MODEL),
                                lambda i, layer: (layer[0], 1 + i // tiles_per_seq, 0, 0))
    weights = [(pw["g_norm1"], (1, D_MODEL)), (pw["g_norm2"], (1, D_MODEL)), (pw["w_gates"], (D_MODEL, GATE_W)),
               (pw["w_br_a"], (ATT_Q, D_MODEL)), (pw["w_br_b"], (GLA_VW, D_MODEL)),
               (pw["w_br_c"], (POOL_W, D_MODEL)), (pw["w_out"], (D_MODEL, D_MODEL)),
               (pw["w_ff_gate"], (D_MODEL, D_FF)), (pw["w_ff_up"], (D_MODEL, D_FF)),
               (pw["w_ff_down"], (D_FF, D_MODEL))]
    in_specs = ([row(D_MODEL), mod_spec, row(ATT_Q), row(GLA_VW), row(POOL_W)]
                + [_layer_spec(shape) for _, shape in weights])
    return pl.pallas_call(
        _post_kernel,
        grid_spec=pltpu.PrefetchScalarGridSpec(
            num_scalar_prefetch=1, grid=(x2d.shape[0] // POST_TILE,), in_specs=in_specs, out_specs=row(D_MODEL)),
        out_shape=jax.ShapeDtypeStruct(x2d.shape, _F32),
        input_output_aliases={1: 0},
        compiler_params=pltpu.CompilerParams(dimension_semantics=("arbitrary",), vmem_limit_bytes=VMEM_LIMIT),
        name="post",
    )(layer, x2d, mod_all, oa, ob, oc, *[a for a, _ in weights])


def _rope_tables(T):
    quarter = HEAD_DIM // 4
    inv_freq = ROPE_BASE ** (-np.arange(quarter, dtype=np.float32) / quarter)
    pos = np.arange(T)
    ang_row = (pos // GRID_W).astype(np.float32)[:, None] * inv_freq[None, :]
    ang_col = (pos % GRID_W).astype(np.float32)[:, None] * inv_freq[None, :]
    cos = np.concatenate([np.cos(ang_row)] * 2 + [np.cos(ang_col)] * 2, axis=-1)
    sin = np.concatenate([-np.sin(ang_row), np.sin(ang_row), -np.sin(ang_col), np.sin(ang_col)], axis=-1)
    return (jnp.asarray(np.tile(cos, (1, ATT_KV_HEADS)), _F32), jnp.asarray(np.tile(sin, (1, ATT_KV_HEADS)), _F32))


def _prepare_weights(w_in, g_qn, g_kn, att_sink, w_gate2, b_gate2, g_gla_out, w_pool, pool_scale, w_br_a,
                     w_br_b, w_br_c, w_out, g_norm1, g_norm2, w_ff_gate, w_ff_up, w_ff_down):
    o_gl = ATT_Q + 2 * ATT_KV + 2 * GLA_QK + 2 * GLA_VW
    o_uc = o_gl + 2 * GLA_RANK
    o_gate = o_uc + POOL_W
    assert o_gl == C_UC
    w_in = w_in.astype(_BF16)
    w_tail = jnp.concatenate(
        [w_in[:, :, o_uc:o_gate], w_in[:, :, o_gl:o_uc],
         jnp.zeros((DEPTH, D_MODEL, GL_PAD - 2 * GLA_RANK), w_in.dtype)], axis=2)
    w_gates = w_in[:, :, o_gate:]
    wg2 = jnp.zeros((DEPTH, GL_PAD, 2 * GLA_QK), _F32)
    wg2 = wg2.at[:, 0:GLA_RANK, 0:GLA_QK].set(w_gate2[:, 0])
    wg2 = wg2.at[:, GLA_RANK:2 * GLA_RANK, GLA_QK:].set(w_gate2[:, 1])
    wg2_hi = wg2.astype(_BF16)
    wg2 = jnp.stack([wg2_hi, (wg2 - wg2_hi.astype(_F32)).astype(_BF16)], axis=1)
    vec = lambda a: a.reshape(DEPTH, 1, -1)
    group_ones = lambda n, width: jnp.asarray(
        (np.arange(n)[:, None] // width) == (np.arange(n)[None, :] // width), _BF16)
    return {
        "head_ones": group_ones(ATT_Q, HEAD_DIM),
        "gla_ones": group_ones(GLA_VW, GLA_DV),
        "w_in": w_in,
        "w_tail": w_tail,
        "w_gates": w_gates,
        "g_qn": vec(jnp.tile(g_qn, (1, ATT_HEADS))),
        "g_kn": vec(jnp.tile(g_kn, (1, ATT_KV_HEADS))),
        "att_sink": att_sink,
        "w_gate2": wg2,
        "b_gate2": vec(b_gate2),
        "g_gla_out": vec(jnp.tile(g_gla_out, (1, GLA_HEADS))),
        "w_pool": w_pool.astype(_BF16),
        "pool_scale": vec(pool_scale),
        "w_br_a": w_br_a.astype(_BF16),
        "w_br_b": w_br_b.astype(_BF16),
        "w_br_c": w_br_c.astype(_BF16),
        "w_out": w_out.astype(_BF16),
        "g_norm1": vec(g_norm1),
        "g_norm2": vec(g_norm2),
        "w_ff_gate": w_ff_gate.astype(_BF16),
        "w_ff_up": w_ff_up.astype(_BF16),
        "w_ff_down": w_ff_down.astype(_BF16),
    }


def kernel(x_prompt, x_sample, c, cache_k, cache_v, state_gla, c_ctx, w_in, g_qn, g_kn, att_sink, w_gate2,
           b_gate2, g_gla_out, w_pool, pool_scale, w_br_a, w_br_b, w_br_c, w_out, g_norm1, g_norm2, w_mod,
           b_mod, w_ff_gate, w_ff_up, w_ff_down):
    B, T, _ = x_prompt.shape
    BL, TL, _ = x_sample.shape
    assert (B * T) % POST_TILE == 0 and TL % POST_TILE == 0 and BL + 1 <= MOD_ROWS
    assert T % PROJ_TILE == 0 and TL % PROJ_TILE == 0
    cv = jnp.concatenate([c_ctx[None, :], c, jnp.zeros((MOD_ROWS - 1 - BL, D_MODEL), _F32)], axis=0)
    mod_all = _modulation(cv, w_mod, b_mod).reshape(DEPTH, MOD_ROWS, 1, 6 * D_MODEL)
    pw = _prepare_weights(w_in, g_qn, g_kn, att_sink, w_gate2, b_gate2, g_gla_out, w_pool, pool_scale, w_br_a,
                          w_br_b, w_br_c, w_out, g_norm1, g_norm2, w_ff_gate, w_ff_up, w_ff_down)
    cos, sin = _rope_tables(TL)
    P = cache_k.shape[2]
    latent_ctx = (cache_k.reshape(BL, DEPTH, P, ATT_KV), cache_v.reshape(BL, DEPTH, P, ATT_KV),
                  jnp.swapaxes(state_gla.reshape(BL, DEPTH, 2, GLA_HEADS // 2, 2 * GLA_DK, GLA_DV), -1, -2),
                  cos, sin)

    def layer_step(l, carry):
        yp, ys, new_k, new_v, new_st = carry
        layer = jnp.full((1,), l, jnp.int32)
        oa, ob, oc, new_k, new_v, new_st = _mix_ctx_call(layer, yp, mod_all, pw, (new_k, new_v, new_st))
        yp = _post_call(layer, yp.reshape(B * T, D_MODEL), mod_all, oa.reshape(B * T, -1), ob.reshape(B * T, -1),
                        oc.reshape(B * T, -1), pw, None).reshape(B, T, D_MODEL)
        oa, ob, oc = _mix_latent_call(layer, ys, mod_all, pw, *latent_ctx)
        ys = _post_call(layer, ys.reshape(BL * TL, D_MODEL), mod_all, oa.reshape(BL * TL, -1),
                        ob.reshape(BL * TL, -1), oc.reshape(BL * TL, -1), pw,
                        TL // POST_TILE).reshape(BL, TL, D_MODEL)
        return yp, ys, new_k, new_v, new_st

    init = (x_prompt, x_sample,
            jnp.zeros((B, DEPTH, T, ATT_KV), _F32), jnp.zeros((B, DEPTH, T, ATT_KV), _F32),
            jnp.zeros((B, DEPTH, 2, GLA_HEADS // 2, 2 * GLA_DK, GLA_DV), _F32))
    yp, ys, new_k, new_v, new_st = lax.fori_loop(0, DEPTH, layer_step, init)
    return (yp, ys, new_k.reshape(B, DEPTH, T, ATT_KV_HEADS, HEAD_DIM),
            new_v.reshape(B, DEPTH, T, ATT_KV_HEADS, HEAD_DIM),
            new_st.reshape(B, DEPTH, 2, GLA_HEADS, GLA_DK, GLA_DV))
```

```python
import functools

import jax
import jax.numpy as jnp
import numpy as np
from jax import lax
from jax.experimental import pallas as pl
from jax.experimental.pallas import tpu as pltpu

D_MODEL = 1024
DEPTH = 4
GRID_W = 64
ATT_HEADS = 8
ATT_KV_HEADS = 2
ATT_GROUP = ATT_HEADS // ATT_KV_HEADS
HEAD_DIM = 64
WINDOW = 128
ATT_BLOCK = 128
ROPE_BASE = 10000.0
GLA_HEADS = 4
GLA_DK = 64
GLA_DV = 128
GLA_RANK = 16
GLA_TAU = 16.0
GLA_CHUNK = 64
POOL_GROUPS = 4
POOL_GROUP_DIM = 128
POOL_WINDOWS = (2, 4, 8, 16)
D_FF = 2816
ATT_Q = ATT_HEADS * HEAD_DIM
ATT_KV = ATT_KV_HEADS * HEAD_DIM
GLA_QK = GLA_HEADS * GLA_DK
GLA_VW = GLA_HEADS * GLA_DV
POOL_W = POOL_GROUPS * POOL_GROUP_DIM
EPS = 1e-6
NEG = -1e30

C_QA = 0
C_KA = C_QA + ATT_Q
C_VA = C_KA + ATT_KV
C_QB = C_VA + ATT_KV
C_KB = C_QB + GLA_QK
C_VB = C_KB + GLA_QK
C_RB = C_VB + GLA_VW
C_UC = C_RB + GLA_VW
C_GL = C_UC + POOL_W
GL_PAD = 128
MIX_W = C_GL + GL_PAD
GATE_W = 3 * D_MODEL

POST_TILE = 512
PROJ_TILE = 256
CTX_SEQS_PER_STEP = 2
POOL_TILE = 256
POOL_HALO = 128
MOD_ROWS = 8
MOD_TILE = 1024
GLA_BLOCK = 256
GLA_MAX_EXPONENT = 80.0
VMEM_LIMIT = 56 * 1024 * 1024

_F32 = jnp.float32
_BF16 = jnp.bfloat16


def _dot(a, b):
    return jnp.dot(a, b, preferred_element_type=_F32)


def _dot_nt(a, b):
    return lax.dot_general(a, b, (((1,), (1,)), ((), ())), preferred_element_type=_F32)


def _dot_tn(a, b):
    return lax.dot_general(a, b, (((0,), (0,)), ((), ())), preferred_element_type=_F32)


def _rms_scale(x):
    return lax.rsqrt(jnp.mean(x * x, axis=-1, keepdims=True) + EPS)


def _group_rms_scale(x, group_ones, width):
    return lax.rsqrt(_dot((x * x).astype(_BF16), group_ones) * (1.0 / width) + EPS)


def _log_sigmoid(x):
    return jnp.minimum(x, 0.0) - jnp.log1p(jnp.exp(-jnp.abs(x)))


def _silu(x):
    return x * jax.nn.sigmoid(x)


def _rope(x, cos, sin_signed):
    n = x.shape[-1]
    lane = lax.broadcasted_iota(jnp.int32, x.shape, 1)
    up = pltpu.roll(x, n - HEAD_DIM // 4, axis=1)
    down = pltpu.roll(x, HEAD_DIM // 4, axis=1)
    partner = jnp.where((lane & (HEAD_DIM // 2 - 1)) < HEAD_DIM // 4, up, down)
    return x * cos + partner * sin_signed


def _mod_kernel(cv_ref, w_ref, b_ref, out_ref):
    s = _silu(cv_ref[...]).astype(_BF16)
    out_ref[0] = _dot(s, w_ref[0].astype(_BF16)) + b_ref[0]


def _modulation(cv, w_mod, b_mod):
    n_col = (6 * D_MODEL) // MOD_TILE
    return pl.pallas_call(
        _mod_kernel,
        grid=(DEPTH, n_col),
        in_specs=[
            pl.BlockSpec((MOD_ROWS, D_MODEL), lambda l, j: (0, 0)),
            pl.BlockSpec((1, D_MODEL, MOD_TILE), lambda l, j: (l, 0, j)),
            pl.BlockSpec((1, 1, MOD_TILE), lambda l, j: (l, 0, j)),
        ],
        out_specs=pl.BlockSpec((1, MOD_ROWS, MOD_TILE), lambda l, j: (l, 0, j)),
        out_shape=jax.ShapeDtypeStruct((DEPTH, MOD_ROWS, 6 * D_MODEL), _F32),
        name="modulation",
    )(cv, w_mod, b_mod.reshape(DEPTH, 1, 6 * D_MODEL))


def _split_heads(x):
    low = lax.broadcasted_iota(jnp.int32, x.shape, 1) < HEAD_DIM
    swapped = pltpu.roll(x, HEAD_DIM, axis=1)
    zero = jnp.zeros_like(x)
    return ((jnp.where(low, x, zero), jnp.where(low, zero, swapped)),
            (jnp.where(low, swapped, zero), jnp.where(low, zero, x)))


def _store_split_kv(k, v, ks_ref, vs_ref, rows):
    ones = jnp.ones_like(v)
    for kv, (k_sides, v_sides, one_sides) in enumerate(zip(_split_heads(k), _split_heads(v), _split_heads(ones))):
        for side in range(2):
            ks_ref[kv, side, rows, :] = k_sides[side].astype(_BF16)
            vs_ref[kv, side, rows, :] = jnp.concatenate([v_sides[side], one_sides[side]], axis=-1).astype(_BF16)


def _pair_softmax_av(qp, keys, values, masks, sink_even, sink_odd):
    m = qp.shape[0]
    scores = []
    for (k_left, k_right), mask in zip(keys, masks):
        s_even, s_odd = _dot_nt(qp, k_left), _dot_nt(qp, k_right)
        if mask is not None:
            s_even, s_odd = jnp.where(mask, s_even, NEG), jnp.where(mask, s_odd, NEG)
        scores.append((s_even, s_odd))
    m_even = jnp.full((m, 1), sink_even, _F32)
    m_odd = jnp.full((m, 1), sink_odd, _F32)
    for s_even, s_odd in scores:
        m_even = jnp.maximum(m_even, jnp.max(s_even, axis=-1, keepdims=True))
        m_odd = jnp.maximum(m_odd, jnp.max(s_odd, axis=-1, keepdims=True))
    res = None
    for (s_even, s_odd), (w_left, w_right) in zip(scores, values):
        r = (_dot(jnp.exp(s_even - m_even).astype(_BF16), w_left)
             + _dot(jnp.exp(s_odd - m_odd).astype(_BF16), w_right))
        res = r if res is None else res + r
    pair = 2 * HEAD_DIM
    low = lax.broadcasted_iota(jnp.int32, (m, pair), 1) < HEAD_DIM
    den = res[:, pair:] + jnp.where(low, jnp.exp(sink_even - m_even), jnp.exp(sink_odd - m_odd))
    return res[:, :pair] / den


def _attention_ctx(T, layer, qr_ref, ks_ref, vs_ref, sink_ref, oa_ref):
    pair = 2 * HEAD_DIM
    for kv in range(ATT_KV_HEADS):
        keys = [(ks_ref[kv, 0], ks_ref[kv, 1])]
        values = [(vs_ref[kv, 0], vs_ref[kv, 1])]
        for j in range(ATT_GROUP // 2):
            head = kv * ATT_GROUP + 2 * j
            cols = slice(head * HEAD_DIM, head * HEAD_DIM + pair)
            o = _pair_softmax_av(qr_ref[:, cols], keys, values, [None], sink_ref[layer, head],
                                 sink_ref[layer, head + 1])
            oa_ref[0, :, cols] = o.astype(_BF16)


def _attention_latent(T, layer, qr_ref, ks_ref, vs_ref, kc_ref, vc_ref, sink_ref, oa_ref):
    pair = 2 * HEAD_DIM
    span = 3 * ATT_BLOCK
    kc, vc = kc_ref[0, 0], vc_ref[0, 0]
    ones = jnp.ones_like(vc)
    ctx_keys = [tuple(side.astype(_BF16) for side in sides) for sides in _split_heads(kc)]
    ctx_values = [tuple(jnp.concatenate([v_side, one_side], axis=-1).astype(_BF16)
                        for v_side, one_side in zip(v_sides, one_sides))
                  for v_sides, one_sides in zip(_split_heads(vc), _split_heads(ones))]

    def block(i, carry):
        q_rows = pl.ds(pl.multiple_of(i * ATT_BLOCK, ATT_BLOCK), ATT_BLOCK)
        k_rows = pl.ds(pl.multiple_of(i * ATT_BLOCK, ATT_BLOCK), span)
        q_pos = i * ATT_BLOCK + lax.broadcasted_iota(jnp.int32, (ATT_BLOCK, span), 0)
        k_pos = (i - 1) * ATT_BLOCK + lax.broadcasted_iota(jnp.int32, (ATT_BLOCK, span), 1)
        valid = (jnp.abs(k_pos - q_pos) <= WINDOW) & (k_pos >= 0) & (k_pos < T)
        for kv in range(ATT_KV_HEADS):
            keys = [(ks_ref[kv, 0, k_rows, :], ks_ref[kv, 1, k_rows, :]), ctx_keys[kv]]
            values = [(vs_ref[kv, 0, k_rows, :], vs_ref[kv, 1, k_rows, :]), ctx_values[kv]]
            for j in range(ATT_GROUP // 2):
                head = kv * ATT_GROUP + 2 * j
                cols = slice(head * HEAD_DIM, head * HEAD_DIM + pair)
                o = _pair_softmax_av(qr_ref[q_rows, cols], keys, values, [valid, None], sink_ref[layer, head],
                                     sink_ref[layer, head + 1])
                oa_ref[0, q_rows, cols] = o.astype(_BF16)
        return carry

    lax.fori_loop(0, T // ATT_BLOCK, block, 0)


def _gla_chunk(z_ref, la_ref, o_ref, st_ref, a_ref, direction, start):
    C = GLA_CHUNK
    rows = pl.ds(pl.multiple_of(start, C), C)
    q = z_ref[rows, C_QB:C_QB + GLA_QK] * (GLA_DK ** -0.5)
    k = z_ref[rows, C_KB:C_KB + GLA_QK]
    v = z_ref[rows, C_VB:C_VB + GLA_VW].astype(_BF16)
    la = la_ref[rows, direction * GLA_QK:(direction + 1) * GLA_QK]
    la_hi = la.astype(_BF16)
    la_lo = (la - la_hi.astype(_F32)).astype(_BF16)
    ti = lax.broadcasted_iota(jnp.int32, (C, C), 0)
    si = lax.broadcasted_iota(jnp.int32, (C, C), 1)
    causal = (si <= ti) if direction == 0 else (si >= ti)
    tri = jnp.where(causal, 1.0, 0.0).astype(_BF16)
    b = _dot(tri, la_hi) + _dot(tri, la_lo)
    end = C - 1 if direction == 0 else 0
    b_end = b[end:end + 1]
    q_in = (q * jnp.exp(b)).astype(_BF16)
    k_out = (k * jnp.exp(b_end - b)).astype(_BF16)
    e_end = jnp.exp(b_end)
    ones = jnp.ones((8, GLA_DK), _BF16)

    def row_group(g, carry):
        base = pl.multiple_of(g * 8, 8)
        b8 = a_ref[GLA_HEADS, pl.ds(base, 8), :]
        q8 = a_ref[GLA_HEADS + 1, pl.ds(base, 8), :]
        s_idx = lax.broadcasted_iota(jnp.int32, (C, 1), 0)
        rows_h = [[] for _ in range(GLA_HEADS)]
        for j in range(8):
            ok = (s_idx <= base + j) if direction == 0 else (s_idx >= base + j)
            decay = jnp.exp(jnp.where(ok, b8[j:j + 1] - b, NEG))
            p = (q8[j:j + 1] * k * decay).astype(_BF16)
            for h in range(GLA_HEADS):
                rows_h[h].append(_dot_nt(ones, p[:, h * GLA_DK:(h + 1) * GLA_DK])[0:1])
        for h in range(GLA_HEADS):
            a_ref[h, pl.ds(base, 8), 0:C] = jnp.concatenate(rows_h[h], axis=0)
        return carry

    a_ref[GLA_HEADS] = b
    a_ref[GLA_HEADS + 1] = q
    lax.fori_loop(0, C // 8, row_group, 0)
    outs = []
    for pair in range(GLA_HEADS // 2):
        s_pair = st_ref[direction, pair]
        s_next = []
        for h in (2 * pair, 2 * pair + 1):
            kc = slice(h * GLA_DK, (h + 1) * GLA_DK)
            vc = slice(h * GLA_DV, (h + 1) * GLA_DV)
            s_t = s_pair[:, (h % 2) * GLA_DK:(h % 2 + 1) * GLA_DK]
            outs.append(_dot_nt(q_in[:, kc], s_t.astype(_BF16)) + _dot(a_ref[h, :, 0:C].astype(_BF16), v[:, vc]))
            s_next.append(s_t * e_end[:, kc] + _dot_tn(v[:, vc], k_out[:, kc]))
        st_ref[direction, pair] = jnp.concatenate(s_next, axis=-1)
    o_ref[rows, :] += jnp.concatenate(outs, axis=-1)


def _gla_block(z_ref, la_ref, o_ref, st_ref, direction, start, use_state):
    NB = GLA_BLOCK
    fwd = direction == 0
    rows = slice(start, start + NB) if isinstance(start, int) else pl.ds(pl.multiple_of(start, NB), NB)
    q = z_ref[rows, C_QB:C_QB + GLA_QK] * (GLA_DK ** -0.5)
    k = z_ref[rows, C_KB:C_KB + GLA_QK]
    v = z_ref[rows, C_VB:C_VB + GLA_VW].astype(_BF16)
    b = la_ref[rows, direction * GLA_QK:(direction + 1) * GLA_QK]
    end, mid = (NB - 1, NB // 2 - 1) if fwd else (0, NB // 2)
    b_end = b[end:end + 1]
    c = b - b[mid:mid + 1]
    q_c = (q * jnp.exp(c)).astype(_BF16)
    k_c = (k * jnp.exp(-c)).astype(_BF16)
    k_fin = (k * jnp.exp(b_end - b)).astype(_BF16)
    if use_state:
        q_in = (q * jnp.exp(b)).astype(_BF16)
        e_all = jnp.exp(b_end)
    ti = lax.broadcasted_iota(jnp.int32, (NB, NB), 0)
    si = lax.broadcasted_iota(jnp.int32, (NB, NB), 1)
    causal = (si <= ti) if fwd else (si >= ti)
    tile = 2 * GLA_DK
    low = lax.broadcasted_iota(jnp.int32, (NB, tile), 1) < GLA_DK
    zero = jnp.zeros((NB, tile), _BF16)
    pick = lambda x, parity: jnp.where(low, x, zero) if parity == 0 else jnp.where(low, zero, x)
    outs = []
    for pair in range(GLA_HEADS // 2):
        lanes = slice(pair * tile, (pair + 1) * tile)
        if use_state:
            s_pair = st_ref[direction, pair]
            s_bf = s_pair.astype(_BF16)
        s_new = None
        for parity in range(2):
            h = 2 * pair + parity
            vc = slice(h * GLA_DV, (h + 1) * GLA_DV)
            a = jnp.where(causal, _dot_nt(q_c[:, lanes], pick(k_c[:, lanes], parity)), 0.0)
            o_h = _dot(a.astype(_BF16), v[:, vc])
            if use_state:
                o_h = o_h + _dot_nt(pick(q_in[:, lanes], parity), s_bf)
            outs.append(o_h)
            upd = _dot_tn(v[:, vc], pick(k_fin[:, lanes], parity))
            s_new = upd if s_new is None else s_new + upd
        if use_state:
            s_new = s_new + s_pair * e_all[:, lanes]
        st_ref[direction, pair] = s_new
    o_ref[rows, :] += jnp.concatenate(outs, axis=-1)


def _log_decay(z_ref, rows, wg2_ref, bg2_ref):
    x = z_ref[rows, C_GL:C_GL + GL_PAD]
    x_hi = x.astype(_BF16)
    x_lo = (x - x_hi.astype(_F32)).astype(_BF16)
    w_hi, w_lo = wg2_ref[0], wg2_ref[1]
    pre = _dot(x_hi, w_hi) + (_dot(x_lo, w_hi) + _dot(x_hi, w_lo)) + bg2_ref[...]
    return _log_sigmoid(pre) * (1.0 / GLA_TAU)


def _gla_prepare(T, z_ref, la_ref, o_ref, wg2_ref, bg2_ref):
    NB = GLA_BLOCK
    ti = lax.broadcasted_iota(jnp.int32, (NB, NB), 0)
    si = lax.broadcasted_iota(jnp.int32, (NB, NB), 1)
    tri = [jnp.where(si <= ti, 1.0, 0.0).astype(_BF16), jnp.where(si >= ti, 1.0, 0.0).astype(_BF16)]
    worst = None
    for r0 in range(0, T, NB):
        rows = slice(r0, r0 + NB)
        la = _log_decay(z_ref, rows, wg2_ref, bg2_ref)
        la_hi = la.astype(_BF16)
        la_lo = (la - la_hi.astype(_F32)).astype(_BF16)
        for d in range(2):
            cols = slice(d * GLA_QK, (d + 1) * GLA_QK)
            b = _dot(tri[d], la_hi[:, cols]) + _dot(tri[d], la_lo[:, cols])
            la_ref[rows, cols] = b
            first, mid, last = (0, NB // 2 - 1, NB - 1) if d == 0 else (NB - 1, NB // 2, 0)
            span = jnp.max(jnp.maximum(b[first:first + 1] - b[mid:mid + 1], b[mid:mid + 1] - b[last:last + 1]))
            worst = span if worst is None else jnp.maximum(worst, span)
    o_ref[...] = jnp.zeros(o_ref.shape, _F32)
    return worst


def _gla_run(T, seqs, worst, a_ref, wg2_ref, bg2_ref, use_state):
    NB = GLA_BLOCK
    n_chunks = T // GLA_CHUNK
    n_blocks = T // NB
    fast_ok = worst <= GLA_MAX_EXPONENT

    @pl.when(fast_ok)
    def _():
        if n_blocks == 1:
            for direction in range(2):
                for z_ref, la_ref, o_ref, st_ref in seqs:
                    _gla_block(z_ref, la_ref, o_ref, st_ref, direction, 0, use_state)
        else:
            def body(i, carry):
                for z_ref, la_ref, o_ref, st_ref in seqs:
                    _gla_block(z_ref, la_ref, o_ref, st_ref, 0, i * NB, True)
                    _gla_block(z_ref, la_ref, o_ref, st_ref, 1, (n_blocks - 1 - i) * NB, True)
                return carry
            lax.fori_loop(0, n_blocks, body, 0)

    @pl.when(jnp.logical_not(fast_ok))
    def _():
        for z_ref, la_ref, o_ref, st_ref in seqs:
            for r0 in range(0, T, NB):
                la_ref[r0:r0 + NB, :] = _log_decay(z_ref, slice(r0, r0 + NB), wg2_ref, bg2_ref)

            def body(i, carry):
                _gla_chunk(z_ref, la_ref, o_ref, st_ref, a_ref, 0, i * GLA_CHUNK)
                _gla_chunk(z_ref, la_ref, o_ref, st_ref, a_ref, 1, (n_chunks - 1 - i) * GLA_CHUNK)
                return carry
            lax.fori_loop(0, n_chunks, body, 0)


def _gla_finish(T, z_ref, o_ref, ggla_ref, vones_ref, ob_ref):
    for r0 in range(0, T, PROJ_TILE):
        rows = slice(r0, r0 + PROJ_TILE)
        o = o_ref[rows, :]
        y = o * _group_rms_scale(o, vones_ref[...], GLA_DV) * ggla_ref[...]
        ob_ref[0, rows, :] = (y * _silu(z_ref[rows, C_RB:C_RB + GLA_VW])).astype(_BF16)


def _pool_halo(T):
    return POOL_HALO if T > POOL_TILE else 0


def _pool(T, z_ref, upad_ref, wpool_ref, pscale_ref, oc_ref):
    halo = _pool_halo(T)
    if halo:
        zeros = jnp.zeros((halo, POOL_W), _BF16)
        upad_ref[0:halo, :] = zeros
        upad_ref[halo + T:halo + T + halo, :] = zeros
    upad_ref[halo:halo + T, :] = z_ref[:, C_UC:C_UC + POOL_W].astype(_BF16)
    span = POOL_TILE + 2 * halo
    r = lax.broadcasted_iota(jnp.int32, (POOL_TILE, span), 0)
    c = lax.broadcasted_iota(jnp.int32, (POOL_TILE, span), 1)
    off = c - halo - r
    for jb in range(T // POOL_TILE):
        t = jb * POOL_TILE + lax.broadcasted_iota(jnp.int32, (POOL_TILE, 1), 0)
        parts = []
        for g, w in enumerate(POOL_WINDOWS):
            cols = slice(g * POOL_GROUP_DIM, (g + 1) * POOL_GROUP_DIM)
            band = jnp.where((off >= -(w // 2)) & (off < w - w // 2), 1.0, 0.0).astype(_BF16)
            win = slice(jb * POOL_TILE, jb * POOL_TILE + span)
            total = _dot(band, upad_ref[win, cols])
            cnt = (jnp.minimum(t - w // 2 + w, T) - jnp.maximum(t - w // 2, 0)).astype(_F32)
            pooled = total / cnt - z_ref[jb * POOL_TILE:(jb + 1) * POOL_TILE, C_UC + g * POOL_GROUP_DIM:
                                         C_UC + (g + 1) * POOL_GROUP_DIM]
            parts.append(_dot(pooled.astype(_BF16), wpool_ref[g]))
        y = jnp.concatenate(parts, axis=-1) * pscale_ref[...]
        oc_ref[0, jb * POOL_TILE:(jb + 1) * POOL_TILE, :] = y.astype(_BF16)


N_MIX_PARAMS = 15
N_MIX_SCRATCH = 9


def _mix_body(latent, T, layer, params, latent_refs, out_refs, scratch):
    (x_ref, mod_ref, gn1_ref, wmain_ref, wtail_ref, gqn_ref, gkn_ref, sink_ref, wg2_ref, bg2_ref, ggla_ref,
     wpool_ref, pscale_ref, hones_ref, vones_ref) = params
    n_seq = x_ref.shape[0]
    one = lambda ref, s: ref.at[pl.ds(s, 1)]
    a_ref = scratch[N_MIX_SCRATCH - 2]
    per_seq = [tuple(ref.at[s] for ref in scratch[:N_MIX_SCRATCH - 2] + scratch[N_MIX_SCRATCH - 1:])
               for s in range(n_seq)]
    pad = ATT_BLOCK if latent else 0
    shift = mod_ref[0, :, 0:D_MODEL]
    scale = mod_ref[0, :, D_MODEL:2 * D_MODEL]

    for s, (z_ref, qr_ref, ks_ref, vs_ref, la_ref, o_ref, st_ref, upad_ref) in enumerate(per_seq):
        if latent:
            for ref in (ks_ref, vs_ref):
                zeros = jnp.zeros(ref.shape[:2] + (pad, ref.shape[3]), _BF16)
                ref[:, :, 0:pad, :] = zeros
                ref[:, :, pad + T:pad + T + pad, :] = zeros
        for r0 in range(0, T, PROJ_TILE):
            rows = slice(r0, r0 + PROJ_TILE)
            x = x_ref[s, rows, :]
            hn = (x * _rms_scale(x) * gn1_ref[...]) * (1.0 + scale) + shift
            hn = hn.astype(_BF16)
            z_ref[rows, 0:C_UC] = _dot(hn, wmain_ref[...])
            z_ref[rows, C_UC:MIX_W] = _dot(hn, wtail_ref[...])
            q = z_ref[rows, C_QA:C_QA + ATT_Q]
            k = z_ref[rows, C_KA:C_KA + ATT_KV]
            q = q * _group_rms_scale(q, hones_ref[...], HEAD_DIM) * gqn_ref[...]
            k = k * _group_rms_scale(k, hones_ref[0:ATT_KV, 0:ATT_KV], HEAD_DIM) * gkn_ref[...]
            v = z_ref[rows, C_VA:C_VA + ATT_KV]
            if latent:
                cos_ref, sin_ref = latent_refs[3], latent_refs[4]
                cos = jnp.concatenate([cos_ref[rows, :]] * (ATT_Q // ATT_KV), axis=-1)
                sin = jnp.concatenate([sin_ref[rows, :]] * (ATT_Q // ATT_KV), axis=-1)
                q = _rope(q, cos, sin)
                k = _rope(k, cos_ref[rows, :], sin_ref[rows, :])
            else:
                out_refs[3][s, rows, :] = k
                out_refs[4][s, rows, :] = v
            qr_ref[rows, :] = (q * (HEAD_DIM ** -0.5)).astype(_BF16)
            _store_split_kv(k, v, ks_ref, vs_ref, slice(pad + r0, pad + r0 + PROJ_TILE))

    for s, (z_ref, qr_ref, ks_ref, vs_ref, la_ref, o_ref, st_ref, upad_ref) in enumerate(per_seq):
        if latent:
            _attention_latent(T, layer, qr_ref, ks_ref, vs_ref, one(latent_refs[0], s), one(latent_refs[1], s),
                              sink_ref, one(out_refs[0], s))
        else:
            _attention_ctx(T, layer, qr_ref, ks_ref, vs_ref, sink_ref, one(out_refs[0], s))

    worst = None
    for s, (z_ref, qr_ref, ks_ref, vs_ref, la_ref, o_ref, st_ref, upad_ref) in enumerate(per_seq):
        if latent:
            st_ref[...] = latent_refs[2][s]
        else:
            st_ref[...] = jnp.zeros(st_ref.shape, _F32)
        span = _gla_prepare(T, z_ref, la_ref, o_ref, wg2_ref, bg2_ref)
        worst = span if worst is None else jnp.maximum(worst, span)
    _gla_run(T, [(z_ref, la_ref, o_ref, st_ref) for z_ref, _, _, _, la_ref, o_ref, st_ref, _ in per_seq],
             worst, a_ref, wg2_ref, bg2_ref, latent)

    for s, (z_ref, qr_ref, ks_ref, vs_ref, la_ref, o_ref, st_ref, upad_ref) in enumerate(per_seq):
        _gla_finish(T, z_ref, o_ref, ggla_ref, vones_ref, one(out_refs[1], s))
        if not latent:
            for d in range(2):
                for pair in range(GLA_HEADS // 2):
                    out_refs[5][s, d, pair] = st_ref[d, pair].T
        _pool(T, z_ref, upad_ref, wpool_ref, pscale_ref, one(out_refs[2], s))


def _merge(x, mod_ref, oa, ob, oc, gn1_ref, wgate_ref, wa_ref, wb_ref, wc_ref, wout_ref):
    mod = lambda i: mod_ref[0, :, i * D_MODEL:(i + 1) * D_MODEL]
    hn = (x * _rms_scale(x) * gn1_ref[...]) * (1.0 + mod(1)) + mod(0)
    gates = jax.nn.sigmoid(_dot(hn.astype(_BF16), wgate_ref[...]))
    mixed = (gates[:, 0:D_MODEL] * _dot(oa, wa_ref[...])
             + gates[:, D_MODEL:2 * D_MODEL] * _dot(ob, wb_ref[...])
             + gates[:, 2 * D_MODEL:3 * D_MODEL] * _dot(oc, wc_ref[...]))
    return x + mod(2) * _dot(mixed.astype(_BF16), wout_ref[...])


def _ffn(x, mod_ref, gn2_ref, wfg_ref, wfu_ref, wfd_ref):
    mod = lambda i: mod_ref[0, :, i * D_MODEL:(i + 1) * D_MODEL]
    hn = ((x * _rms_scale(x) * gn2_ref[...]) * (1.0 + mod(4)) + mod(3)).astype(_BF16)
    h = _silu(_dot(hn, wfg_ref[...])) * _dot(hn, wfu_ref[...])
    return x + mod(5) * _dot(h.astype(_BF16), wfd_ref[...])


def _mix_latent_kernel(T, layer_ref, *refs):
    params, refs = refs[:N_MIX_PARAMS], refs[N_MIX_PARAMS:]
    _mix_body(True, T, layer_ref[0], params, refs[:5], refs[5:8], refs[8:])


def _mix_ctx_kernel(T, layer_ref, *refs):
    params, refs = refs[:N_MIX_PARAMS], refs[N_MIX_PARAMS + 3:]
    _mix_body(False, T, layer_ref[0], params, None, refs[:6], refs[6:])


def _layer_spec(shape):
    zeros = (0,) * len(shape)
    return pl.BlockSpec((None,) + tuple(shape), lambda i, layer: (layer[0],) + zeros,
                        pipeline_mode=pl.Buffered(1))


def _const_spec(blk):
    return pl.BlockSpec(blk, lambda i, layer: (0,) * len(blk), pipeline_mode=pl.Buffered(1))


def _mix_params(x, x_spec, mod_spec, mod_all, pw):
    specs = [
        x_spec, mod_spec,
        _layer_spec((1, D_MODEL)),
        _layer_spec((D_MODEL, C_UC)),
        _layer_spec((D_MODEL, MIX_W - C_UC)),
        _layer_spec((1, ATT_Q)), _layer_spec((1, ATT_KV)),
        pl.BlockSpec(memory_space=pltpu.SMEM),
        _layer_spec((2, GL_PAD, 2 * GLA_QK)), _layer_spec((1, 2 * GLA_QK)), _layer_spec((1, GLA_VW)),
        _layer_spec((POOL_GROUPS, POOL_GROUP_DIM, POOL_GROUP_DIM)), _layer_spec((1, POOL_W)),
        _const_spec((ATT_Q, ATT_Q)), _const_spec((GLA_VW, GLA_VW)),
    ]
    args = [x, mod_all, pw["g_norm1"], pw["w_in"], pw["w_tail"], pw["g_qn"], pw["g_kn"], pw["att_sink"], pw["w_gate2"],
            pw["b_gate2"], pw["g_gla_out"], pw["w_pool"], pw["pool_scale"], pw["head_ones"], pw["gla_ones"]]
    assert len(specs) == len(args) == N_MIX_PARAMS
    return specs, args


def _mix_scratch(S, T, kv_rows):
    scratch = [
        pltpu.VMEM((S, T, MIX_W), _F32),
        pltpu.VMEM((S, T, ATT_Q), _BF16),
        pltpu.VMEM((S, ATT_KV_HEADS, 2, kv_rows, 2 * HEAD_DIM), _BF16),
        pltpu.VMEM((S, ATT_KV_HEADS, 2, kv_rows, 4 * HEAD_DIM), _BF16),
        pltpu.VMEM((S, T, 2 * GLA_QK), _F32),
        pltpu.VMEM((S, T, GLA_VW), _F32),
        pltpu.VMEM((S, 2, GLA_HEADS // 2, GLA_DV, 2 * GLA_DK), _F32),
        pltpu.VMEM((GLA_HEADS + 2, GLA_CHUNK, GLA_QK), _F32),
        pltpu.VMEM((S, T + 2 * _pool_halo(T), POOL_W), _BF16),
    ]
    assert len(scratch) == N_MIX_SCRATCH
    return scratch


def _mix_latent_call(layer, x, mod_all, pw, cache_k, cache_v, st0, cos, sin):
    B, T, _ = x.shape
    per_seq = lambda blk: pl.BlockSpec(blk, lambda b, layer: (b,) + (0,) * (len(blk) - 1))
    mod_spec = pl.BlockSpec((None, 1, 1, 6 * D_MODEL), lambda b, layer: (layer[0], b + 1, 0, 0))
    x_spec = pl.BlockSpec((1, T, D_MODEL), lambda b, layer: (b, 0, 0), pipeline_mode=pl.Buffered(1))
    in_specs, args = _mix_params(x, x_spec, mod_spec, mod_all, pw)
    P = cache_k.shape[2]
    cache_spec = pl.BlockSpec((1, 1, P, ATT_KV), lambda b, layer: (b, layer[0], 0, 0))
    in_specs += [cache_spec, cache_spec,
                 pl.BlockSpec((1, None, 2, GLA_HEADS // 2, GLA_DV, 2 * GLA_DK),
                              lambda b, layer: (b, layer[0], 0, 0, 0, 0)),
                 _const_spec((T, ATT_KV)), _const_spec((T, ATT_KV))]
    args += [cache_k, cache_v, st0, cos, sin]
    widths = (ATT_Q, GLA_VW, POOL_W)
    return pl.pallas_call(
        functools.partial(_mix_latent_kernel, T),
        grid_spec=pltpu.PrefetchScalarGridSpec(
            num_scalar_prefetch=1, grid=(B,), in_specs=in_specs,
            out_specs=[per_seq((1, T, w)) for w in widths],
            scratch_shapes=_mix_scratch(1, T, T + 2 * ATT_BLOCK)),
        out_shape=[jax.ShapeDtypeStruct((B, T, w), _BF16) for w in widths],
        compiler_params=pltpu.CompilerParams(dimension_semantics=("arbitrary",), vmem_limit_bytes=VMEM_LIMIT),
        name="mix_latent",
    )(layer, *args)


def _mix_ctx_call(layer, x, mod_all, pw, stacked):
    B, T, _ = x.shape
    S = CTX_SEQS_PER_STEP
    assert B % S == 0
    per_seq = lambda blk: pl.BlockSpec(blk, lambda b, layer: (b,) + (0,) * (len(blk) - 1))
    mod_spec = pl.BlockSpec((None, 1, 1, 6 * D_MODEL), lambda b, layer: (layer[0], 0, 0, 0))
    in_specs, args = _mix_params(x, per_seq((S, T, D_MODEL)), mod_spec, mod_all, pw)
    widths = (ATT_Q, GLA_VW, POOL_W)
    n_in = 1 + len(args)
    aliases = {n_in + j: len(widths) + j for j in range(len(stacked))}
    in_specs += [pl.BlockSpec(memory_space=pl.ANY)] * len(stacked)
    args += list(stacked)
    at_layer = lambda blk: pl.BlockSpec((S, None) + blk, lambda b, layer: (b, layer[0]) + (0,) * len(blk))
    out_specs = [per_seq((S, T, w)) for w in widths] + [
        at_layer((T, ATT_KV)), at_layer((T, ATT_KV)), at_layer((2, GLA_HEADS // 2, 2 * GLA_DK, GLA_DV))]
    out_shape = ([jax.ShapeDtypeStruct((B, T, w), _BF16) for w in widths]
                 + [jax.ShapeDtypeStruct(a.shape, a.dtype) for a in stacked])
    return pl.pallas_call(
        functools.partial(_mix_ctx_kernel, T),
        grid_spec=pltpu.PrefetchScalarGridSpec(
            num_scalar_prefetch=1, grid=(B // S,), in_specs=in_specs, out_specs=out_specs,
            scratch_shapes=_mix_scratch(S, T, T)),
        out_shape=out_shape,
        input_output_aliases=aliases,
        compiler_params=pltpu.CompilerParams(dimension_semantics=("arbitrary",), vmem_limit_bytes=VMEM_LIMIT),
        name="mix_ctx",
    )(layer, *args)


def _post_kernel(layer_ref, x_ref, mod_ref, oa_ref, ob_ref, oc_ref, gn1_ref, gn2_ref, wgate_ref, wa_ref, wb_ref,
                 wc_ref, wout_ref, wfg_ref, wfu_ref, wfd_ref, out_ref):
    x = _merge(x_ref[...], mod_ref, oa_ref[...], ob_ref[...], oc_ref[...], gn1_ref, wgate_ref, wa_ref, wb_ref,
               wc_ref, wout_ref)
    out_ref[...] = _ffn(x, mod_ref, gn2_ref, wfg_ref, wfu_ref, wfd_ref)


def _post_call(layer, x2d, mod_all, oa, ob, oc, pw, tiles_per_seq):
    row = lambda w: pl.BlockSpec((POST_TILE, w), lambda i, layer: (i, 0))
    if tiles_per_seq is None:
        mod_spec = pl.BlockSpec((None, 1, 1, 6 * D_MODEL), lambda i, layer: (layer[0], 0, 0, 0))
    else:
        mod_spec = pl.BlockSpec((None, 1, 1, 6 * D_MODEL),
                                lambda i, layer: (layer[0], 1 + i // tiles_per_seq, 0, 0))
    weights = [(pw["g_norm1"], (1, D_MODEL)), (pw["g_norm2"], (1, D_MODEL)), (pw["w_gates"], (D_MODEL, GATE_W)),
               (pw["w_br_a"], (ATT_Q, D_MODEL)), (pw["w_br_b"], (GLA_VW, D_MODEL)),
               (pw["w_br_c"], (POOL_W, D_MODEL)), (pw["w_out"], (D_MODEL, D_MODEL)),
               (pw["w_ff_gate"], (D_MODEL, D_FF)), (pw["w_ff_up"], (D_MODEL, D_FF)),
               (pw["w_ff_down"], (D_FF, D_MODEL))]
    in_specs = ([row(D_MODEL), mod_spec, row(ATT_Q), row(GLA_VW), row(POOL_W)]
                + [_layer_spec(shape) for _, shape in weights])
    return pl.pallas_call(
        _post_kernel,
        grid_spec=pltpu.PrefetchScalarGridSpec(
            num_scalar_prefetch=1, grid=(x2d.shape[0] // POST_TILE,), in_specs=in_specs, out_specs=row(D_MODEL)),
        out_shape=jax.ShapeDtypeStruct(x2d.shape, _F32),
        input_output_aliases={1: 0},
        compiler_params=pltpu.CompilerParams(dimension_semantics=("arbitrary",), vmem_limit_bytes=VMEM_LIMIT),
        name="post",
    )(layer, x2d, mod_all, oa, ob, oc, *[a for a, _ in weights])


def _rope_tables(T):
    quarter = HEAD_DIM // 4
    inv_freq = ROPE_BASE ** (-np.arange(quarter, dtype=np.float32) / quarter)
    pos = np.arange(T)
    ang_row = (pos // GRID_W).astype(np.float32)[:, None] * inv_freq[None, :]
    ang_col = (pos % GRID_W).astype(np.float32)[:, None] * inv_freq[None, :]
    cos = np.concatenate([np.cos(ang_row)] * 2 + [np.cos(ang_col)] * 2, axis=-1)
    sin = np.concatenate([-np.sin(ang_row), np.sin(ang_row), -np.sin(ang_col), np.sin(ang_col)], axis=-1)
    return (jnp.asarray(np.tile(cos, (1, ATT_KV_HEADS)), _F32), jnp.asarray(np.tile(sin, (1, ATT_KV_HEADS)), _F32))


def _prepare_weights(w_in, g_qn, g_kn, att_sink, w_gate2, b_gate2, g_gla_out, w_pool, pool_scale, w_br_a,
                     w_br_b, w_br_c, w_out, g_norm1, g_norm2, w_ff_gate, w_ff_up, w_ff_down):
    o_gl = ATT_Q + 2 * ATT_KV + 2 * GLA_QK + 2 * GLA_VW
    o_uc = o_gl + 2 * GLA_RANK
    o_gate = o_uc + POOL_W
    assert o_gl == C_UC
    w_in = w_in.astype(_BF16)
    w_tail = jnp.concatenate(
        [w_in[:, :, o_uc:o_gate], w_in[:, :, o_gl:o_uc],
         jnp.zeros((DEPTH, D_MODEL, GL_PAD - 2 * GLA_RANK), w_in.dtype)], axis=2)
    w_gates = w_in[:, :, o_gate:]
    wg2 = jnp.zeros((DEPTH, GL_PAD, 2 * GLA_QK), _F32)
    wg2 = wg2.at[:, 0:GLA_RANK, 0:GLA_QK].set(w_gate2[:, 0])
    wg2 = wg2.at[:, GLA_RANK:2 * GLA_RANK, GLA_QK:].set(w_gate2[:, 1])
    wg2_hi = wg2.astype(_BF16)
    wg2 = jnp.stack([wg2_hi, (wg2 - wg2_hi.astype(_F32)).astype(_BF16)], axis=1)
    vec = lambda a: a.reshape(DEPTH, 1, -1)
    group_ones = lambda n, width: jnp.asarray(
        (np.arange(n)[:, None] // width) == (np.arange(n)[None, :] // width), _BF16)
    return {
        "head_ones": group_ones(ATT_Q, HEAD_DIM),
        "gla_ones": group_ones(GLA_VW, GLA_DV),
        "w_in": w_in,
        "w_tail": w_tail,
        "w_gates": w_gates,
        "g_qn": vec(jnp.tile(g_qn, (1, ATT_HEADS))),
        "g_kn": vec(jnp.tile(g_kn, (1, ATT_KV_HEADS))),
        "att_sink": att_sink,
        "w_gate2": wg2,
        "b_gate2": vec(b_gate2),
        "g_gla_out": vec(jnp.tile(g_gla_out, (1, GLA_HEADS))),
        "w_pool": w_pool.astype(_BF16),
        "pool_scale": vec(pool_scale),
        "w_br_a": w_br_a.astype(_BF16),
        "w_br_b": w_br_b.astype(_BF16),
        "w_br_c": w_br_c.astype(_BF16),
        "w_out": w_out.astype(_BF16),
        "g_norm1": vec(g_norm1),
        "g_norm2": vec(g_norm2),
        "w_ff_gate": w_ff_gate.astype(_BF16),
        "w_ff_up": w_ff_up.astype(_BF16),
        "w_ff_down": w_ff_down.astype(_BF16),
    }


def kernel(x_prompt, x_sample, c, cache_k, cache_v, state_gla, c_ctx, w_in, g_qn, g_kn, att_sink, w_gate2,
           b_gate2, g_gla_out, w_pool, pool_scale, w_br_a, w_br_b, w_br_c, w_out, g_norm1, g_norm2, w_mod,
           b_mod, w_ff_gate, w_ff_up, w_ff_down):
    B, T, _ = x_prompt.shape
    BL, TL, _ = x_sample.shape
    assert (B * T) % POST_TILE == 0 and TL % POST_TILE == 0 and BL + 1 <= MOD_ROWS
    assert T % PROJ_TILE == 0 and TL % PROJ_TILE == 0
    cv = jnp.concatenate([c_ctx[None, :], c, jnp.zeros((MOD_ROWS - 1 - BL, D_MODEL), _F32)], axis=0)
    mod_all = _modulation(cv, w_mod, b_mod).reshape(DEPTH, MOD_ROWS, 1, 6 * D_MODEL)
    pw = _prepare_weights(w_in, g_qn, g_kn, att_sink, w_gate2, b_gate2, g_gla_out, w_pool, pool_scale, w_br_a,
                          w_br_b, w_br_c, w_out, g_norm1, g_norm2, w_ff_gate, w_ff_up, w_ff_down)
    cos, sin = _rope_tables(TL)
    P = cache_k.shape[2]
    latent_ctx = (cache_k.reshape(BL, DEPTH, P, ATT_KV), cache_v.reshape(BL, DEPTH, P, ATT_KV),
                  jnp.swapaxes(state_gla.reshape(BL, DEPTH, 2, GLA_HEADS // 2, 2 * GLA_DK, GLA_DV), -1, -2),
                  cos, sin)

    def layer_step(l, carry):
        yp, ys, new_k, new_v, new_st = carry
        layer = jnp.full((1,), l, jnp.int32)
        oa, ob, oc, new_k, new_v, new_st = _mix_ctx_call(layer, yp, mod_all, pw, (new_k, new_v, new_st))
        yp = _post_call(layer, yp.reshape(B * T, D_MODEL), mod_all, oa.reshape(B * T, -1), ob.reshape(B * T, -1),
                        oc.reshape(B * T, -1), pw, None).reshape(B, T, D_MODEL)
        oa, ob, oc = _mix_latent_call(layer, ys, mod_all, pw, *latent_ctx)
        ys = _post_call(layer, ys.reshape(BL * TL, D_MODEL), mod_all, oa.reshape(BL * TL, -1),
                        ob.reshape(BL * TL, -1), oc.reshape(BL * TL, -1), pw,
                        TL // POST_TILE).reshape(BL, TL, D_MODEL)
        return yp, ys, new_k, new_v, new_st

    init = (x_prompt, x_sample,
            jnp.zeros((B, DEPTH, T, ATT_KV), _F32), jnp.zeros((B, DEPTH, T, ATT_KV), _F32),
            jnp.zeros((B, DEPTH, 2, GLA_HEADS // 2, 2 * GLA_DK, GLA_DV), _F32))
    yp, ys, new_k, new_v, new_st = lax.fori_loop(0, DEPTH, layer_step, init)
    return (yp, ys, new_k.reshape(B, DEPTH, T, ATT_KV_HEADS, HEAD_DIM),
            new_v.reshape(B, DEPTH, T, ATT_KV_HEADS, HEAD_DIM),
            new_st.reshape(B, DEPTH, 2, GLA_HEADS, GLA_DK, GLA_DV))
```

```python
import functools

import jax
import jax.numpy as jnp
import numpy as np
from jax import lax
from jax.experimental import pallas as pl
from jax.experimental.pallas import tpu as pltpu

D_MODEL = 1024
DEPTH = 4
GRID_W = 64
ATT_HEADS = 8
ATT_KV_HEADS = 2
ATT_GROUP = ATT_HEADS // ATT_KV_HEADS
HEAD_DIM = 64
WINDOW = 128
ATT_BLOCK = 128
ROPE_BASE = 10000.0
GLA_HEADS = 4
GLA_DK = 64
GLA_DV = 128
GLA_RANK = 16
GLA_TAU = 16.0
GLA_CHUNK = 64
POOL_GROUPS = 4
POOL_GROUP_DIM = 128
POOL_WINDOWS = (2, 4, 8, 16)
D_FF = 2816
ATT_Q = ATT_HEADS * HEAD_DIM
ATT_KV = ATT_KV_HEADS * HEAD_DIM
GLA_QK = GLA_HEADS * GLA_DK
GLA_VW = GLA_HEADS * GLA_DV
POOL_W = POOL_GROUPS * POOL_GROUP_DIM
EPS = 1e-6
NEG = -1e30

C_QA = 0
C_KA = C_QA + ATT_Q
C_VA = C_KA + ATT_KV
C_QB = C_VA + ATT_KV
C_KB = C_QB + GLA_QK
C_VB = C_KB + GLA_QK
C_RB = C_VB + GLA_VW
C_UC = C_RB + GLA_VW
C_GL = C_UC + POOL_W
GL_PAD = 128
MIX_W = C_GL + GL_PAD
GATE_W = 3 * D_MODEL
W_IN_GATE = C_GL + 2 * GLA_RANK
GATE_WIN_START = (W_IN_GATE // 128) * 128
GATE_WIN_W = -(-(W_IN_GATE + GATE_W - GATE_WIN_START) // 128) * 128
W_IN_PADDED = GATE_WIN_START + GATE_WIN_W

POST_TILE = 512
PROJ_TILE = 256
CTX_SEQS_PER_STEP = 4
POOL_TILE = 256
POOL_HALO = 128
MOD_ROWS = 8
MOD_TILE = 1024
GLA_BLOCK = 256
GLA_MAX_EXPONENT = 80.0
VMEM_LIMIT = 56 * 1024 * 1024

_F32 = jnp.float32
_BF16 = jnp.bfloat16


def _dot(a, b):
    return jnp.dot(a, b, preferred_element_type=_F32)


def _dot_nt(a, b):
    return lax.dot_general(a, b, (((1,), (1,)), ((), ())), preferred_element_type=_F32)


def _dot_tn(a, b):
    return lax.dot_general(a, b, (((0,), (0,)), ((), ())), preferred_element_type=_F32)


def _rms_scale(x):
    return lax.rsqrt(jnp.mean(x * x, axis=-1, keepdims=True) + EPS)


def _group_rms_scale(x, group_ones, width):
    return lax.rsqrt(_dot((x * x).astype(_BF16), group_ones) * (1.0 / width) + EPS)


def _log_sigmoid(x):
    return jnp.minimum(x, 0.0) - jnp.log1p(jnp.exp(-jnp.abs(x)))


def _silu(x):
    return x * jax.nn.sigmoid(x)


def _rope(x, cos, sin_signed):
    n = x.shape[-1]
    lane = lax.broadcasted_iota(jnp.int32, x.shape, 1)
    up = pltpu.roll(x, n - HEAD_DIM // 4, axis=1)
    down = pltpu.roll(x, HEAD_DIM // 4, axis=1)
    partner = jnp.where((lane & (HEAD_DIM // 2 - 1)) < HEAD_DIM // 4, up, down)
    return x * cos + partner * sin_signed


def _mod_kernel(cv_ref, w_ref, b_ref, out_ref):
    s = _silu(cv_ref[...]).astype(_BF16)
    out_ref[0] = _dot(s, w_ref[0].astype(_BF16)) + b_ref[0]


def _modulation(cv, w_mod, b_mod):
    n_col = (6 * D_MODEL) // MOD_TILE
    return pl.pallas_call(
        _mod_kernel,
        grid=(DEPTH, n_col),
        in_specs=[
            pl.BlockSpec((MOD_ROWS, D_MODEL), lambda l, j: (0, 0)),
            pl.BlockSpec((1, D_MODEL, MOD_TILE), lambda l, j: (l, 0, j)),
            pl.BlockSpec((1, 1, MOD_TILE), lambda l, j: (l, 0, j)),
        ],
        out_specs=pl.BlockSpec((1, MOD_ROWS, MOD_TILE), lambda l, j: (l, 0, j)),
        out_shape=jax.ShapeDtypeStruct((DEPTH, MOD_ROWS, 6 * D_MODEL), _F32),
        name="modulation",
    )(cv, w_mod, b_mod.reshape(DEPTH, 1, 6 * D_MODEL))


def _split_heads(x):
    low = lax.broadcasted_iota(jnp.int32, x.shape, 1) < HEAD_DIM
    swapped = pltpu.roll(x, HEAD_DIM, axis=1)
    zero = jnp.zeros_like(x)
    return ((jnp.where(low, x, zero), jnp.where(low, zero, swapped)),
            (jnp.where(low, swapped, zero), jnp.where(low, zero, x)))


def _store_split_kv(k, v, ks_ref, vs_ref, rows):
    ones = jnp.ones_like(v)
    for kv, (k_sides, v_sides, one_sides) in enumerate(zip(_split_heads(k), _split_heads(v), _split_heads(ones))):
        for side in range(2):
            ks_ref[kv, side, rows, :] = k_sides[side].astype(_BF16)
            vs_ref[kv, side, rows, :] = jnp.concatenate([v_sides[side], one_sides[side]], axis=-1).astype(_BF16)


def _pair_softmax_av(qp, keys, values, masks, sink_even, sink_odd):
    m = qp.shape[0]
    scores = []
    for (k_left, k_right), mask in zip(keys, masks):
        s_even, s_odd = _dot_nt(qp, k_left), _dot_nt(qp, k_right)
        if mask is not None:
            s_even, s_odd = jnp.where(mask, s_even, NEG), jnp.where(mask, s_odd, NEG)
        scores.append((s_even, s_odd))
    m_even = jnp.full((m, 1), sink_even, _F32)
    m_odd = jnp.full((m, 1), sink_odd, _F32)
    for s_even, s_odd in scores:
        m_even = jnp.maximum(m_even, jnp.max(s_even, axis=-1, keepdims=True))
        m_odd = jnp.maximum(m_odd, jnp.max(s_odd, axis=-1, keepdims=True))
    res = None
    for (s_even, s_odd), (w_left, w_right) in zip(scores, values):
        r = (_dot(jnp.exp(s_even - m_even).astype(_BF16), w_left)
             + _dot(jnp.exp(s_odd - m_odd).astype(_BF16), w_right))
        res = r if res is None else res + r
    pair = 2 * HEAD_DIM
    low = lax.broadcasted_iota(jnp.int32, (m, pair), 1) < HEAD_DIM
    den = res[:, pair:] + jnp.where(low, jnp.exp(sink_even - m_even), jnp.exp(sink_odd - m_odd))
    return res[:, :pair] / den


def _attention_ctx(T, layer, qr_ref, ks_ref, vs_ref, sink_ref, oa_ref):
    pair = 2 * HEAD_DIM
    for kv in range(ATT_KV_HEADS):
        keys = [(ks_ref[kv, 0], ks_ref[kv, 1])]
        values = [(vs_ref[kv, 0], vs_ref[kv, 1])]
        for j in range(ATT_GROUP // 2):
            head = kv * ATT_GROUP + 2 * j
            cols = slice(head * HEAD_DIM, head * HEAD_DIM + pair)
            o = _pair_softmax_av(qr_ref[:, cols], keys, values, [None], sink_ref[layer, head],
                                 sink_ref[layer, head + 1])
            oa_ref[0, :, cols] = o.astype(_BF16)


def _attention_latent(T, layer, qr_ref, ks_ref, vs_ref, kc_ref, vc_ref, sink_ref, oa_ref):
    pair = 2 * HEAD_DIM
    span = 3 * ATT_BLOCK
    kc, vc = kc_ref[0, 0], vc_ref[0, 0]
    ones = jnp.ones_like(vc)
    ctx_keys = [tuple(side.astype(_BF16) for side in sides) for sides in _split_heads(kc)]
    ctx_values = [tuple(jnp.concatenate([v_side, one_side], axis=-1).astype(_BF16)
                        for v_side, one_side in zip(v_sides, one_sides))
                  for v_sides, one_sides in zip(_split_heads(vc), _split_heads(ones))]

    def block(i, carry):
        q_rows = pl.ds(pl.multiple_of(i * ATT_BLOCK, ATT_BLOCK), ATT_BLOCK)
        k_rows = pl.ds(pl.multiple_of(i * ATT_BLOCK, ATT_BLOCK), span)
        q_pos = i * ATT_BLOCK + lax.broadcasted_iota(jnp.int32, (ATT_BLOCK, span), 0)
        k_pos = (i - 1) * ATT_BLOCK + lax.broadcasted_iota(jnp.int32, (ATT_BLOCK, span), 1)
        valid = (jnp.abs(k_pos - q_pos) <= WINDOW) & (k_pos >= 0) & (k_pos < T)
        for kv in range(ATT_KV_HEADS):
            keys = [(ks_ref[kv, 0, k_rows, :], ks_ref[kv, 1, k_rows, :]), ctx_keys[kv]]
            values = [(vs_ref[kv, 0, k_rows, :], vs_ref[kv, 1, k_rows, :]), ctx_values[kv]]
            for j in range(ATT_GROUP // 2):
                head = kv * ATT_GROUP + 2 * j
                cols = slice(head * HEAD_DIM, head * HEAD_DIM + pair)
                o = _pair_softmax_av(qr_ref[q_rows, cols], keys, values, [valid, None], sink_ref[layer, head],
                                     sink_ref[layer, head + 1])
                oa_ref[0, q_rows, cols] = o.astype(_BF16)
        return carry

    lax.fori_loop(0, T // ATT_BLOCK, block, 0, unroll=2)


def _gla_chunk(z_ref, la_ref, o_ref, st_ref, a_ref, direction, start):
    C = GLA_CHUNK
    rows = pl.ds(pl.multiple_of(start, C), C)
    q = z_ref[rows, C_QB:C_QB + GLA_QK] * (GLA_DK ** -0.5)
    k = z_ref[rows, C_KB:C_KB + GLA_QK]
    v = z_ref[rows, C_VB:C_VB + GLA_VW].astype(_BF16)
    la = la_ref[rows, direction * GLA_QK:(direction + 1) * GLA_QK]
    la_hi = la.astype(_BF16)
    la_lo = (la - la_hi.astype(_F32)).astype(_BF16)
    ti = lax.broadcasted_iota(jnp.int32, (C, C), 0)
    si = lax.broadcasted_iota(jnp.int32, (C, C), 1)
    causal = (si <= ti) if direction == 0 else (si >= ti)
    tri = jnp.where(causal, 1.0, 0.0).astype(_BF16)
    b = _dot(tri, la_hi) + _dot(tri, la_lo)
    end = C - 1 if direction == 0 else 0
    b_end = b[end:end + 1]
    q_in = (q * jnp.exp(b)).astype(_BF16)
    k_out = (k * jnp.exp(b_end - b)).astype(_BF16)
    e_end = jnp.exp(b_end)
    ones = jnp.ones((8, GLA_DK), _BF16)

    def row_group(g, carry):
        base = pl.multiple_of(g * 8, 8)
        b8 = a_ref[GLA_HEADS, pl.ds(base, 8), :]
        q8 = a_ref[GLA_HEADS + 1, pl.ds(base, 8), :]
        s_idx = lax.broadcasted_iota(jnp.int32, (C, 1), 0)
        rows_h = [[] for _ in range(GLA_HEADS)]
        for j in range(8):
            ok = (s_idx <= base + j) if direction == 0 else (s_idx >= base + j)
            decay = jnp.exp(jnp.where(ok, b8[j:j + 1] - b, NEG))
            p = (q8[j:j + 1] * k * decay).astype(_BF16)
            for h in range(GLA_HEADS):
                rows_h[h].append(_dot_nt(ones, p[:, h * GLA_DK:(h + 1) * GLA_DK])[0:1])
        for h in range(GLA_HEADS):
            a_ref[h, pl.ds(base, 8), 0:C] = jnp.concatenate(rows_h[h], axis=0)
        return carry

    a_ref[GLA_HEADS] = b
    a_ref[GLA_HEADS + 1] = q
    lax.fori_loop(0, C // 8, row_group, 0)
    outs = []
    for pair in range(GLA_HEADS // 2):
        s_pair = st_ref[direction, pair]
        s_next = []
        for h in (2 * pair, 2 * pair + 1):
            kc = slice(h * GLA_DK, (h + 1) * GLA_DK)
            vc = slice(h * GLA_DV, (h + 1) * GLA_DV)
            s_t = s_pair[:, (h % 2) * GLA_DK:(h % 2 + 1) * GLA_DK]
            outs.append(_dot_nt(q_in[:, kc], s_t.astype(_BF16)) + _dot(a_ref[h, :, 0:C].astype(_BF16), v[:, vc]))
            s_next.append(s_t * e_end[:, kc] + _dot_tn(v[:, vc], k_out[:, kc]))
        st_ref[direction, pair] = jnp.concatenate(s_next, axis=-1)
    o_ref[rows, :] += jnp.concatenate(outs, axis=-1)


def _gla_block(z_ref, la_ref, o_ref, st_ref, direction, start, use_state):
    NB = GLA_BLOCK
    fwd = direction == 0
    rows = slice(start, start + NB) if isinstance(start, int) else pl.ds(pl.multiple_of(start, NB), NB)
    q = z_ref[rows, C_QB:C_QB + GLA_QK] * (GLA_DK ** -0.5)
    k = z_ref[rows, C_KB:C_KB + GLA_QK]
    v = z_ref[rows, C_VB:C_VB + GLA_VW].astype(_BF16)
    b = la_ref[rows, direction * GLA_QK:(direction + 1) * GLA_QK]
    end, mid = (NB - 1, NB // 2 - 1) if fwd else (0, NB // 2)
    b_end = b[end:end + 1]
    c = b - b[mid:mid + 1]
    q_c = (q * jnp.exp(c)).astype(_BF16)
    k_c = (k * jnp.exp(-c)).astype(_BF16)
    k_fin = (k * jnp.exp(b_end - b)).astype(_BF16)
    if use_state:
        q_in = (q * jnp.exp(b)).astype(_BF16)
        e_all = jnp.exp(b_end)
    ti = lax.broadcasted_iota(jnp.int32, (NB, NB), 0)
    si = lax.broadcasted_iota(jnp.int32, (NB, NB), 1)
    causal = (si <= ti) if fwd else (si >= ti)
    tile = 2 * GLA_DK
    low = lax.broadcasted_iota(jnp.int32, (NB, tile), 1) < GLA_DK
    zero = jnp.zeros((NB, tile), _BF16)
    pick = lambda x, parity: jnp.where(low, x, zero) if parity == 0 else jnp.where(low, zero, x)
    outs = []
    for pair in range(GLA_HEADS // 2):
        lanes = slice(pair * tile, (pair + 1) * tile)
        if use_state:
            s_pair = st_ref[direction, pair]
            s_bf = s_pair.astype(_BF16)
        s_new = None
        for parity in range(2):
            h = 2 * pair + parity
            vc = slice(h * GLA_DV, (h + 1) * GLA_DV)
            a = jnp.where(causal, _dot_nt(q_c[:, lanes], pick(k_c[:, lanes], parity)), 0.0)
            o_h = _dot(a.astype(_BF16), v[:, vc])
            if use_state:
                o_h = o_h + _dot_nt(pick(q_in[:, lanes], parity), s_bf)
            outs.append(o_h)
            upd = _dot_tn(v[:, vc], pick(k_fin[:, lanes], parity))
            s_new = upd if s_new is None else s_new + upd
        if use_state:
            s_new = s_new + s_pair * e_all[:, lanes]
        st_ref[direction, pair] = s_new
    o_ref[rows, :] += jnp.concatenate(outs, axis=-1)


def _log_decay(z_ref, rows, wg2_ref, bg2_ref):
    x = z_ref[rows, C_GL:C_GL + GL_PAD]
    x_hi = x.astype(_BF16)
    x_lo = (x - x_hi.astype(_F32)).astype(_BF16)
    w_hi, w_lo = wg2_ref[0], wg2_ref[1]
    pre = _dot(x_hi, w_hi) + (_dot(x_lo, w_hi) + _dot(x_hi, w_lo)) + bg2_ref[...]
    return _log_sigmoid(pre) * (1.0 / GLA_TAU)


def _gla_prepare(T, z_ref, la_ref, o_ref, wg2_ref, bg2_ref):
    NB = GLA_BLOCK
    ti = lax.broadcasted_iota(jnp.int32, (NB, NB), 0)
    si = lax.broadcasted_iota(jnp.int32, (NB, NB), 1)
    tri = [jnp.where(si <= ti, 1.0, 0.0).astype(_BF16), jnp.where(si >= ti, 1.0, 0.0).astype(_BF16)]
    worst = None
    for r0 in range(0, T, NB):
        rows = slice(r0, r0 + NB)
        la = _log_decay(z_ref, rows, wg2_ref, bg2_ref)
        la_hi = la.astype(_BF16)
        la_lo = (la - la_hi.astype(_F32)).astype(_BF16)
        for d in range(2):
            cols = slice(d * GLA_QK, (d + 1) * GLA_QK)
            b = _dot(tri[d], la_hi[:, cols]) + _dot(tri[d], la_lo[:, cols])
            la_ref[rows, cols] = b
            first, mid, last = (0, NB // 2 - 1, NB - 1) if d == 0 else (NB - 1, NB // 2, 0)
            span = jnp.max(jnp.maximum(b[first:first + 1] - b[mid:mid + 1], b[mid:mid + 1] - b[last:last + 1]))
            worst = span if worst is None else jnp.maximum(worst, span)
    o_ref[...] = jnp.zeros(o_ref.shape, _F32)
    return worst


def _gla_run(T, seqs, worst, a_ref, wg2_ref, bg2_ref, use_state):
    NB = GLA_BLOCK
    n_chunks = T // GLA_CHUNK
    n_blocks = T // NB
    fast_ok = worst <= GLA_MAX_EXPONENT

    @pl.when(fast_ok)
    def _():
        if n_blocks == 1:
            for direction in range(2):
                for z_ref, la_ref, o_ref, st_ref in seqs:
                    _gla_block(z_ref, la_ref, o_ref, st_ref, direction, 0, use_state)
        else:
            def body(i, carry):
                for z_ref, la_ref, o_ref, st_ref in seqs:
                    _gla_block(z_ref, la_ref, o_ref, st_ref, 0, i * NB, True)
                    _gla_block(z_ref, la_ref, o_ref, st_ref, 1, (n_blocks - 1 - i) * NB, True)
                return carry
            lax.fori_loop(0, n_blocks, body, 0, unroll=2)

    @pl.when(jnp.logical_not(fast_ok))
    def _():
        for z_ref, la_ref, o_ref, st_ref in seqs:
            for r0 in range(0, T, NB):
                la_ref[r0:r0 + NB, :] = _log_decay(z_ref, slice(r0, r0 + NB), wg2_ref, bg2_ref)

            def body(i, carry):
                _gla_chunk(z_ref, la_ref, o_ref, st_ref, a_ref, 0, i * GLA_CHUNK)
                _gla_chunk(z_ref, la_ref, o_ref, st_ref, a_ref, 1, (n_chunks - 1 - i) * GLA_CHUNK)
                return carry
            lax.fori_loop(0, n_chunks, body, 0)


def _gla_finish(T, z_ref, o_ref, ggla_ref, vones_ref, ob_ref):
    for r0 in range(0, T, PROJ_TILE):
        rows = slice(r0, r0 + PROJ_TILE)
        o = o_ref[rows, :]
        y = o * _group_rms_scale(o, vones_ref[...], GLA_DV) * ggla_ref[...]
        ob_ref[0, rows, :] = (y * _silu(z_ref[rows, C_RB:C_RB + GLA_VW])).astype(_BF16)


def _pool_halo(T):
    return POOL_HALO if T > POOL_TILE else 0


def _pool(T, z_ref, upad_ref, wpool_ref, pscale_ref, oc_ref):
    halo = _pool_halo(T)
    if halo:
        zeros = jnp.zeros((halo, POOL_W), _BF16)
        upad_ref[0:halo, :] = zeros
        upad_ref[halo + T:halo + T + halo, :] = zeros
    upad_ref[halo:halo + T, :] = z_ref[:, C_UC:C_UC + POOL_W].astype(_BF16)
    span = POOL_TILE + 2 * halo
    r = lax.broadcasted_iota(jnp.int32, (POOL_TILE, span), 0)
    c = lax.broadcasted_iota(jnp.int32, (POOL_TILE, span), 1)
    off = c - halo - r
    for jb in range(T // POOL_TILE):
        t = jb * POOL_TILE + lax.broadcasted_iota(jnp.int32, (POOL_TILE, 1), 0)
        parts = []
        for g, w in enumerate(POOL_WINDOWS):
            cols = slice(g * POOL_GROUP_DIM, (g + 1) * POOL_GROUP_DIM)
            band = jnp.where((off >= -(w // 2)) & (off < w - w // 2), 1.0, 0.0).astype(_BF16)
            win = slice(jb * POOL_TILE, jb * POOL_TILE + span)
            total = _dot(band, upad_ref[win, cols])
            cnt = (jnp.minimum(t - w // 2 + w, T) - jnp.maximum(t - w // 2, 0)).astype(_F32)
            pooled = total / cnt - z_ref[jb * POOL_TILE:(jb + 1) * POOL_TILE, C_UC + g * POOL_GROUP_DIM:
                                         C_UC + (g + 1) * POOL_GROUP_DIM]
            parts.append(_dot(pooled.astype(_BF16), wpool_ref[g]))
        y = jnp.concatenate(parts, axis=-1) * pscale_ref[...]
        oc_ref[0, jb * POOL_TILE:(jb + 1) * POOL_TILE, :] = y.astype(_BF16)


N_MIX_PARAMS = 15
N_MIX_SCRATCH = 9


def _mix_body(latent, T, layer, params, latent_refs, out_refs, scratch):
    (x_ref, mod_ref, gn1_ref, wmain_ref, wtail_ref, gqn_ref, gkn_ref, sink_ref, wg2_ref, bg2_ref, ggla_ref,
     wpool_ref, pscale_ref, hones_ref, vones_ref) = params
    n_seq = x_ref.shape[0]
    one = lambda ref, s: ref.at[pl.ds(s, 1)]
    a_ref = scratch[N_MIX_SCRATCH - 2]
    per_seq = [tuple(ref.at[s] for ref in scratch[:N_MIX_SCRATCH - 2] + scratch[N_MIX_SCRATCH - 1:])
               for s in range(n_seq)]
    pad = ATT_BLOCK if latent else 0
    shift = mod_ref[0, :, 0:D_MODEL]
    scale = mod_ref[0, :, D_MODEL:2 * D_MODEL]

    for s, (z_ref, qr_ref, ks_ref, vs_ref, la_ref, o_ref, st_ref, upad_ref) in enumerate(per_seq):
        if latent:
            for ref in (ks_ref, vs_ref):
                zeros = jnp.zeros(ref.shape[:2] + (pad, ref.shape[3]), _BF16)
                ref[:, :, 0:pad, :] = zeros
                ref[:, :, pad + T:pad + T + pad, :] = zeros
        for r0 in range(0, T, PROJ_TILE):
            rows = slice(r0, r0 + PROJ_TILE)
            x = x_ref[s, rows, :]
            hn = (x * _rms_scale(x) * gn1_ref[...]) * (1.0 + scale) + shift
            hn = hn.astype(_BF16)
            z_ref[rows, 0:C_UC] = _dot(hn, wmain_ref[...])
            z_ref[rows, C_UC:MIX_W] = _dot(hn, wtail_ref[...])
            q = z_ref[rows, C_QA:C_QA + ATT_Q]
            k = z_ref[rows, C_KA:C_KA + ATT_KV]
            q = q * _group_rms_scale(q, hones_ref[...], HEAD_DIM) * gqn_ref[...]
            k = k * _group_rms_scale(k, hones_ref[0:ATT_KV, 0:ATT_KV], HEAD_DIM) * gkn_ref[...]
            v = z_ref[rows, C_VA:C_VA + ATT_KV]
            if latent:
                cos_ref, sin_ref = latent_refs[3], latent_refs[4]
                cos = jnp.concatenate([cos_ref[rows, :]] * (ATT_Q // ATT_KV), axis=-1)
                sin = jnp.concatenate([sin_ref[rows, :]] * (ATT_Q // ATT_KV), axis=-1)
                q = _rope(q, cos, sin)
                k = _rope(k, cos_ref[rows, :], sin_ref[rows, :])
            else:
                out_refs[3][s, rows, :] = k
                out_refs[4][s, rows, :] = v
            qr_ref[rows, :] = (q * (HEAD_DIM ** -0.5)).astype(_BF16)
            _store_split_kv(k, v, ks_ref, vs_ref, slice(pad + r0, pad + r0 + PROJ_TILE))

    for s, (z_ref, qr_ref, ks_ref, vs_ref, la_ref, o_ref, st_ref, upad_ref) in enumerate(per_seq):
        if latent:
            _attention_latent(T, layer, qr_ref, ks_ref, vs_ref, one(latent_refs[0], s), one(latent_refs[1], s),
                              sink_ref, one(out_refs[0], s))
        else:
            _attention_ctx(T, layer, qr_ref, ks_ref, vs_ref, sink_ref, one(out_refs[0], s))

    worst = None
    for s, (z_ref, qr_ref, ks_ref, vs_ref, la_ref, o_ref, st_ref, upad_ref) in enumerate(per_seq):
        if latent:
            st_ref[...] = latent_refs[2][s]
        else:
            st_ref[...] = jnp.zeros(st_ref.shape, _F32)
        span = _gla_prepare(T, z_ref, la_ref, o_ref, wg2_ref, bg2_ref)
        worst = span if worst is None else jnp.maximum(worst, span)
    _gla_run(T, [(z_ref, la_ref, o_ref, st_ref) for z_ref, _, _, _, la_ref, o_ref, st_ref, _ in per_seq],
             worst, a_ref, wg2_ref, bg2_ref, latent)

    for s, (z_ref, qr_ref, ks_ref, vs_ref, la_ref, o_ref, st_ref, upad_ref) in enumerate(per_seq):
        _gla_finish(T, z_ref, o_ref, ggla_ref, vones_ref, one(out_refs[1], s))
        if not latent:
            for d in range(2):
                for pair in range(GLA_HEADS // 2):
                    out_refs[5][s, d, pair] = st_ref[d, pair].T
        _pool(T, z_ref, upad_ref, wpool_ref, pscale_ref, one(out_refs[2], s))


def _merge(x, mod_ref, oa, ob, oc, gn1_ref, wgate_ref, wa_ref, wb_ref, wc_ref, wout_ref):
    mod = lambda i: mod_ref[0, :, i * D_MODEL:(i + 1) * D_MODEL]
    hn = (x * _rms_scale(x) * gn1_ref[...]) * (1.0 + mod(1)) + mod(0)
    window = _dot(hn.astype(_BF16), wgate_ref[...])
    lead = W_IN_GATE - GATE_WIN_START
    gates = jax.nn.sigmoid(pltpu.roll(window, GATE_WIN_W - lead, axis=1)[:, 0:GATE_W] if lead else window)
    mixed = (gates[:, 0:D_MODEL] * _dot(oa, wa_ref[...])
             + gates[:, D_MODEL:2 * D_MODEL] * _dot(ob, wb_ref[...])
             + gates[:, 2 * D_MODEL:3 * D_MODEL] * _dot(oc, wc_ref[...]))
    return x + mod(2) * _dot(mixed.astype(_BF16), wout_ref[...])


def _ffn(x, mod_ref, gn2_ref, wfg_ref, wfu_ref, wfd_ref):
    mod = lambda i: mod_ref[0, :, i * D_MODEL:(i + 1) * D_MODEL]
    hn = ((x * _rms_scale(x) * gn2_ref[...]) * (1.0 + mod(4)) + mod(3)).astype(_BF16)
    h = _silu(_dot(hn, wfg_ref[...])) * _dot(hn, wfu_ref[...])
    return x + mod(5) * _dot(h.astype(_BF16), wfd_ref[...])


def _mix_latent_kernel(T, layer_ref, *refs):
    params, refs = refs[:N_MIX_PARAMS], refs[N_MIX_PARAMS:]
    _mix_body(True, T, layer_ref[0], params, refs[:5], refs[5:8], refs[8:])


def _mix_ctx_kernel(T, layer_ref, *refs):
    params, refs = refs[:N_MIX_PARAMS], refs[N_MIX_PARAMS + 3:]
    _mix_body(False, T, layer_ref[0], params, None, refs[:6], refs[6:])


def _layer_spec(shape):
    zeros = (0,) * len(shape)
    return pl.BlockSpec((None,) + tuple(shape), lambda i, layer: (layer[0],) + zeros,
                        pipeline_mode=pl.Buffered(1))


def _const_spec(blk):
    return pl.BlockSpec(blk, lambda i, layer: (0,) * len(blk), pipeline_mode=pl.Buffered(1))


def _mix_params(x, x_spec, mod_spec, mod_all, pw):
    specs = [
        x_spec, mod_spec,
        _layer_spec((1, D_MODEL)),
        _layer_spec((D_MODEL, C_UC)),
        _layer_spec((D_MODEL, MIX_W - C_UC)),
        _layer_spec((1, ATT_Q)), _layer_spec((1, ATT_KV)),
        pl.BlockSpec(memory_space=pltpu.SMEM),
        _layer_spec((2, GL_PAD, 2 * GLA_QK)), _layer_spec((1, 2 * GLA_QK)), _layer_spec((1, GLA_VW)),
        _layer_spec((POOL_GROUPS, POOL_GROUP_DIM, POOL_GROUP_DIM)), _layer_spec((1, POOL_W)),
        _const_spec((ATT_Q, ATT_Q)), _const_spec((GLA_VW, GLA_VW)),
    ]
    args = [x, mod_all, pw["g_norm1"], pw["w_in"], pw["w_tail"], pw["g_qn"], pw["g_kn"], pw["att_sink"], pw["w_gate2"],
            pw["b_gate2"], pw["g_gla_out"], pw["w_pool"], pw["pool_scale"], pw["head_ones"], pw["gla_ones"]]
    assert len(specs) == len(args) == N_MIX_PARAMS
    return specs, args


def _mix_scratch(S, T, kv_rows):
    scratch = [
        pltpu.VMEM((S, T, MIX_W), _F32),
        pltpu.VMEM((S, T, ATT_Q), _BF16),
        pltpu.VMEM((S, ATT_KV_HEADS, 2, kv_rows, 2 * HEAD_DIM), _BF16),
        pltpu.VMEM((S, ATT_KV_HEADS, 2, kv_rows, 4 * HEAD_DIM), _BF16),
        pltpu.VMEM((S, T, 2 * GLA_QK), _F32),
        pltpu.VMEM((S, T, GLA_VW), _F32),
        pltpu.VMEM((S, 2, GLA_HEADS // 2, GLA_DV, 2 * GLA_DK), _F32),
        pltpu.VMEM((GLA_HEADS + 2, GLA_CHUNK, GLA_QK), _F32),
        pltpu.VMEM((S, T + 2 * _pool_halo(T), POOL_W), _BF16),
    ]
    assert len(scratch) == N_MIX_SCRATCH
    return scratch


def _mix_latent_call(layer, x, mod_all, pw, cache_k, cache_v, st0, cos, sin):
    B, T, _ = x.shape
    per_seq = lambda blk: pl.BlockSpec(blk, lambda b, layer: (b,) + (0,) * (len(blk) - 1))
    mod_spec = pl.BlockSpec((None, 1, 1, 6 * D_MODEL), lambda b, layer: (layer[0], b + 1, 0, 0))
    x_spec = pl.BlockSpec((1, T, D_MODEL), lambda b, layer: (b, 0, 0), pipeline_mode=pl.Buffered(1))
    in_specs, args = _mix_params(x, x_spec, mod_spec, mod_all, pw)
    P = cache_k.shape[2]
    cache_spec = pl.BlockSpec((1, 1, P, ATT_KV), lambda b, layer: (b, layer[0], 0, 0))
    in_specs += [cache_spec, cache_spec,
                 pl.BlockSpec((1, None, 2, GLA_HEADS // 2, GLA_DV, 2 * GLA_DK),
                              lambda b, layer: (b, layer[0], 0, 0, 0, 0)),
                 _const_spec((T, ATT_KV)), _const_spec((T, ATT_KV))]
    args += [cache_k, cache_v, st0, cos, sin]
    widths = (ATT_Q, GLA_VW, POOL_W)
    return pl.pallas_call(
        functools.partial(_mix_latent_kernel, T),
        grid_spec=pltpu.PrefetchScalarGridSpec(
            num_scalar_prefetch=1, grid=(B,), in_specs=in_specs,
            out_specs=[per_seq((1, T, w)) for w in widths],
            scratch_shapes=_mix_scratch(1, T, T + 2 * ATT_BLOCK)),
        out_shape=[jax.ShapeDtypeStruct((B, T, w), _BF16) for w in widths],
        compiler_params=pltpu.CompilerParams(dimension_semantics=("arbitrary",), vmem_limit_bytes=VMEM_LIMIT),
        name="mix_latent",
    )(layer, *args)


def _mix_ctx_call(layer, x, mod_all, pw, stacked):
    B, T, _ = x.shape
    S = CTX_SEQS_PER_STEP
    assert B % S == 0
    per_seq = lambda blk: pl.BlockSpec(blk, lambda b, layer: (b,) + (0,) * (len(blk) - 1))
    mod_spec = pl.BlockSpec((None, 1, 1, 6 * D_MODEL), lambda b, layer: (layer[0], 0, 0, 0))
    in_specs, args = _mix_params(x, per_seq((S, T, D_MODEL)), mod_spec, mod_all, pw)
    widths = (ATT_Q, GLA_VW, POOL_W)
    n_in = 1 + len(args)
    aliases = {n_in + j: len(widths) + j for j in range(len(stacked))}
    in_specs += [pl.BlockSpec(memory_space=pl.ANY)] * len(stacked)
    args += list(stacked)
    at_layer = lambda blk: pl.BlockSpec((S, None) + blk, lambda b, layer: (b, layer[0]) + (0,) * len(blk))
    out_specs = [per_seq((S, T, w)) for w in widths] + [
        at_layer((T, ATT_KV)), at_layer((T, ATT_KV)), at_layer((2, GLA_HEADS // 2, 2 * GLA_DK, GLA_DV))]
    out_shape = ([jax.ShapeDtypeStruct((B, T, w), _BF16) for w in widths]
                 + [jax.ShapeDtypeStruct(a.shape, a.dtype) for a in stacked])
    return pl.pallas_call(
        functools.partial(_mix_ctx_kernel, T),
        grid_spec=pltpu.PrefetchScalarGridSpec(
            num_scalar_prefetch=1, grid=(B // S,), in_specs=in_specs, out_specs=out_specs,
            scratch_shapes=_mix_scratch(S, T, T)),
        out_shape=out_shape,
        input_output_aliases=aliases,
        compiler_params=pltpu.CompilerParams(dimension_semantics=("arbitrary",), vmem_limit_bytes=VMEM_LIMIT),
        name="mix_ctx",
    )(layer, *args)


def _post_kernel(layer_ref, x_ref, mod_ref, oa_ref, ob_ref, oc_ref, gn1_ref, gn2_ref, wgate_ref, wa_ref, wb_ref,
                 wc_ref, wout_ref, wfg_ref, wfu_ref, wfd_ref, out_ref):
    x = _merge(x_ref[...], mod_ref, oa_ref[...], ob_ref[...], oc_ref[...], gn1_ref, wgate_ref.at[0], wa_ref, wb_ref,
               wc_ref, wout_ref)
    out_ref[...] = _ffn(x, mod_ref, gn2_ref, wfg_ref, wfu_ref, wfd_ref)


def _post_call(layer, x2d, mod_all, oa, ob, oc, pw, tiles_per_seq):
    row = lambda w: pl.BlockSpec((POST_TILE, w), lambda i, layer: (i, 0))
    if tiles_per_seq is None:
        mod_spec = pl.BlockSpec((None, 1, 1, 6 * D_MODEL), lambda i, layer: (layer[0], 0, 0, 0))
    else:
        mod_spec = pl.BlockSpec((None, 1, 1, 6 * D_MODEL),
                                lambda i, layer: (layer[0], 1 + i // tiles_per_seq, 0, 0))
    gate_window = pl.BlockSpec((pl.Element(1), pl.Element(D_MODEL), pl.Element(GATE_WIN_W)),
                               lambda i, layer: (layer[0], 0, GATE_WIN_START), pipeline_mode=pl.Buffered(1))
    weights = [(pw["g_norm1"], (1, D_MODEL)), (pw["g_norm2"], (1, D_MODEL)), (pw["w_in"], gate_window),
               (pw["w_br_a"], (ATT_Q, D_MODEL)), (pw["w_br_b"], (GLA_VW, D_MODEL)),
               (pw["w_br_c"], (POOL_W, D_MODEL)), (pw["w_out"], (D_MODEL, D_MODEL)),
               (pw["w_ff_gate"], (D_MODEL, D_FF)), (pw["w_ff_up"], (D_MODEL, D_FF)),
               (pw["w_ff_down"], (D_FF, D_MODEL))]
    in_specs = ([row(D_MODEL), mod_spec, row(ATT_Q), row(GLA_VW), row(POOL_W)]
                + [shape if isinstance(shape, pl.BlockSpec) else _layer_spec(shape) for _, shape in weights])
    return pl.pallas_call(
        _post_kernel,
        grid_spec=pltpu.PrefetchScalarGridSpec(
            num_scalar_prefetch=1, grid=(x2d.shape[0] // POST_TILE,), in_specs=in_specs, out_specs=row(D_MODEL)),
        out_shape=jax.ShapeDtypeStruct(x2d.shape, _F32),
        input_output_aliases={1: 0},
        compiler_params=pltpu.CompilerParams(dimension_semantics=("arbitrary",), vmem_limit_bytes=VMEM_LIMIT),
        name="post",
    )(layer, x2d, mod_all, oa, ob, oc, *[a for a, _ in weights])


def _rope_tables(T):
    quarter = HEAD_DIM // 4
    inv_freq = ROPE_BASE ** (-np.arange(quarter, dtype=np.float32) / quarter)
    pos = np.arange(T)
    ang_row = (pos // GRID_W).astype(np.float32)[:, None] * inv_freq[None, :]
    ang_col = (pos % GRID_W).astype(np.float32)[:, None] * inv_freq[None, :]
    cos = np.concatenate([np.cos(ang_row)] * 2 + [np.cos(ang_col)] * 2, axis=-1)
    sin = np.concatenate([-np.sin(ang_row), np.sin(ang_row), -np.sin(ang_col), np.sin(ang_col)], axis=-1)
    return (jnp.asarray(np.tile(cos, (1, ATT_KV_HEADS)), _F32), jnp.asarray(np.tile(sin, (1, ATT_KV_HEADS)), _F32))


def _prepare_weights(w_in, g_qn, g_kn, att_sink, w_gate2, b_gate2, g_gla_out, w_pool, pool_scale, w_br_a,
                     w_br_b, w_br_c, w_out, g_norm1, g_norm2, w_ff_gate, w_ff_up, w_ff_down):
    o_gl = ATT_Q + 2 * ATT_KV + 2 * GLA_QK + 2 * GLA_VW
    o_uc = o_gl + 2 * GLA_RANK
    o_gate = o_uc + POOL_W
    assert o_gl == C_UC and o_gate == W_IN_GATE and w_in.shape[-1] == W_IN_GATE + GATE_W
    w_in = jnp.pad(w_in.astype(_BF16), ((0, 0), (0, 0), (0, W_IN_PADDED - w_in.shape[-1])))
    w_tail = jnp.concatenate(
        [w_in[:, :, o_uc:o_gate], w_in[:, :, o_gl:o_uc],
         jnp.zeros((DEPTH, D_MODEL, GL_PAD - 2 * GLA_RANK), w_in.dtype)], axis=2)
    wg2 = jnp.zeros((DEPTH, GL_PAD, 2 * GLA_QK), _F32)
    wg2 = wg2.at[:, 0:GLA_RANK, 0:GLA_QK].set(w_gate2[:, 0])
    wg2 = wg2.at[:, GLA_RANK:2 * GLA_RANK, GLA_QK:].set(w_gate2[:, 1])
    wg2_hi = wg2.astype(_BF16)
    wg2 = jnp.stack([wg2_hi, (wg2 - wg2_hi.astype(_F32)).astype(_BF16)], axis=1)
    vec = lambda a: a.reshape(DEPTH, 1, -1)
    group_ones = lambda n, width: jnp.asarray(
        (np.arange(n)[:, None] // width) == (np.arange(n)[None, :] // width), _BF16)
    return {
        "head_ones": group_ones(ATT_Q, HEAD_DIM),
        "gla_ones": group_ones(GLA_VW, GLA_DV),
        "w_in": w_in,
        "w_tail": w_tail,
        "g_qn": vec(jnp.tile(g_qn, (1, ATT_HEADS))),
        "g_kn": vec(jnp.tile(g_kn, (1, ATT_KV_HEADS))),
        "att_sink": att_sink,
        "w_gate2": wg2,
        "b_gate2": vec(b_gate2),
        "g_gla_out": vec(jnp.tile(g_gla_out, (1, GLA_HEADS))),
        "w_pool": w_pool.astype(_BF16),
        "pool_scale": vec(pool_scale),
        "w_br_a": w_br_a.astype(_BF16),
        "w_br_b": w_br_b.astype(_BF16),
        "w_br_c": w_br_c.astype(_BF16),
        "w_out": w_out.astype(_BF16),
        "g_norm1": vec(g_norm1),
        "g_norm2": vec(g_norm2),
        "w_ff_gate": w_ff_gate.astype(_BF16),
        "w_ff_up": w_ff_up.astype(_BF16),
        "w_ff_down": w_ff_down.astype(_BF16),
    }


def kernel(x_prompt, x_sample, c, cache_k, cache_v, state_gla, c_ctx, w_in, g_qn, g_kn, att_sink, w_gate2,
           b_gate2, g_gla_out, w_pool, pool_scale, w_br_a, w_br_b, w_br_c, w_out, g_norm1, g_norm2, w_mod,
           b_mod, w_ff_gate, w_ff_up, w_ff_down):
    B, T, _ = x_prompt.shape
    BL, TL, _ = x_sample.shape
    assert (B * T) % POST_TILE == 0 and TL % POST_TILE == 0 and BL + 1 <= MOD_ROWS
    assert T % PROJ_TILE == 0 and TL % PROJ_TILE == 0
    cv = jnp.concatenate([c_ctx[None, :], c, jnp.zeros((MOD_ROWS - 1 - BL, D_MODEL), _F32)], axis=0)
    mod_all = _modulation(cv, w_mod, b_mod).reshape(DEPTH, MOD_ROWS, 1, 6 * D_MODEL)
    pw = _prepare_weights(w_in, g_qn, g_kn, att_sink, w_gate2, b_gate2, g_gla_out, w_pool, pool_scale, w_br_a,
                          w_br_b, w_br_c, w_out, g_norm1, g_norm2, w_ff_gate, w_ff_up, w_ff_down)
    cos, sin = _rope_tables(TL)
    P = cache_k.shape[2]
    latent_ctx = (cache_k.reshape(BL, DEPTH, P, ATT_KV), cache_v.reshape(BL, DEPTH, P, ATT_KV),
                  jnp.swapaxes(state_gla.reshape(BL, DEPTH, 2, GLA_HEADS // 2, 2 * GLA_DK, GLA_DV), -1, -2),
                  cos, sin)

    def layer_step(l, carry):
        yp, ys, new_k, new_v, new_st = carry
        layer = jnp.full((1,), l, jnp.int32)
        oa, ob, oc, new_k, new_v, new_st = _mix_ctx_call(layer, yp, mod_all, pw, (new_k, new_v, new_st))
        yp = _post_call(layer, yp.reshape(B * T, D_MODEL), mod_all, oa.reshape(B * T, -1), ob.reshape(B * T, -1),
                        oc.reshape(B * T, -1), pw, None).reshape(B, T, D_MODEL)
        oa, ob, oc = _mix_latent_call(layer, ys, mod_all, pw, *latent_ctx)
        ys = _post_call(layer, ys.reshape(BL * TL, D_MODEL), mod_all, oa.reshape(BL * TL, -1),
                        ob.reshape(BL * TL, -1), oc.reshape(BL * TL, -1), pw,
                        TL // POST_TILE).reshape(BL, TL, D_MODEL)
        return yp, ys, new_k, new_v, new_st

    init = (x_prompt, x_sample,
            jnp.zeros((B, DEPTH, T, ATT_KV), _F32), jnp.zeros((B, DEPTH, T, ATT_KV), _F32),
            jnp.zeros((B, DEPTH, 2, GLA_HEADS // 2, 2 * GLA_DK, GLA_DV), _F32))
    yp, ys, new_k, new_v, new_st = lax.fori_loop(0, DEPTH, layer_step, init)
    return (yp, ys, new_k.reshape(B, DEPTH, T, ATT_KV_HEADS, HEAD_DIM),
            new_v.reshape(B, DEPTH, T, ATT_KV_HEADS, HEAD_DIM),
            new_st.reshape(B, DEPTH, 2, GLA_HEADS, GLA_DK, GLA_DV))
```

```python
import functools

import jax
import jax.numpy as jnp
import numpy as np
from jax import lax
from jax.experimental import pallas as pl
from jax.experimental.pallas import tpu as pltpu

D_MODEL = 1024
DEPTH = 4
GRID_W = 64
ATT_HEADS = 8
ATT_KV_HEADS = 2
ATT_GROUP = ATT_HEADS // ATT_KV_HEADS
HEAD_DIM = 64
WINDOW = 128
ATT_BLOCK = 128
ROPE_BASE = 10000.0
GLA_HEADS = 4
GLA_DK = 64
GLA_DV = 128
GLA_RANK = 16
GLA_TAU = 16.0
GLA_CHUNK = 64
POOL_GROUPS = 4
POOL_GROUP_DIM = 128
POOL_WINDOWS = (2, 4, 8, 16)
D_FF = 2816
ATT_Q = ATT_HEADS * HEAD_DIM
ATT_KV = ATT_KV_HEADS * HEAD_DIM
GLA_QK = GLA_HEADS * GLA_DK
GLA_VW = GLA_HEADS * GLA_DV
POOL_W = POOL_GROUPS * POOL_GROUP_DIM
EPS = 1e-6
NEG = -1e30

C_QA = 0
C_KA = C_QA + ATT_Q
C_VA = C_KA + ATT_KV
C_QB = C_VA + ATT_KV
C_KB = C_QB + GLA_QK
C_VB = C_KB + GLA_QK
C_RB = C_VB + GLA_VW
C_UC = C_RB + GLA_VW
C_GL = C_UC + POOL_W
GL_PAD = 128
MIX_W = C_GL + GL_PAD
GATE_W = 3 * D_MODEL

POST_TILE = 512
PROJ_TILE = 256
CTX_SEQS_PER_STEP = 2
POOL_TILE = 256
POOL_HALO = 128
MOD_ROWS = 8
MOD_TILE = 1024
GLA_BLOCK = 256
GLA_MAX_EXPONENT = 80.0
VMEM_LIMIT = 56 * 1024 * 1024

_F32 = jnp.float32
_BF16 = jnp.bfloat16


def _dot(a, b):
    return jnp.dot(a, b, preferred_element_type=_F32)


def _dot_nt(a, b):
    return lax.dot_general(a, b, (((1,), (1,)), ((), ())), preferred_element_type=_F32)


def _dot_tn(a, b):
    return lax.dot_general(a, b, (((0,), (0,)), ((), ())), preferred_element_type=_F32)


def _rms_scale(x):
    return lax.rsqrt(jnp.mean(x * x, axis=-1, keepdims=True) + EPS)


def _group_rms_scale(x, group_ones, width):
    return lax.rsqrt(_dot((x * x).astype(_BF16), group_ones) * (1.0 / width) + EPS)


def _log_sigmoid(x):
    return jnp.minimum(x, 0.0) - jnp.log1p(jnp.exp(-jnp.abs(x)))


def _silu(x):
    return x * jax.nn.sigmoid(x)


def _rope(x, cos, sin_signed):
    n = x.shape[-1]
    lane = lax.broadcasted_iota(jnp.int32, x.shape, 1)
    up = pltpu.roll(x, n - HEAD_DIM // 4, axis=1)
    down = pltpu.roll(x, HEAD_DIM // 4, axis=1)
    partner = jnp.where((lane & (HEAD_DIM // 2 - 1)) < HEAD_DIM // 4, up, down)
    return x * cos + partner * sin_signed


def _mod_kernel(cv_ref, w_ref, b_ref, out_ref):
    s = _silu(cv_ref[...]).astype(_BF16)
    out_ref[0] = _dot(s, w_ref[0].astype(_BF16)) + b_ref[0]


def _modulation(cv, w_mod, b_mod):
    n_col = (6 * D_MODEL) // MOD_TILE
    return pl.pallas_call(
        _mod_kernel,
        grid=(DEPTH, n_col),
        in_specs=[
            pl.BlockSpec((MOD_ROWS, D_MODEL), lambda l, j: (0, 0)),
            pl.BlockSpec((1, D_MODEL, MOD_TILE), lambda l, j: (l, 0, j)),
            pl.BlockSpec((1, 1, MOD_TILE), lambda l, j: (l, 0, j)),
        ],
        out_specs=pl.BlockSpec((1, MOD_ROWS, MOD_TILE), lambda l, j: (l, 0, j)),
        out_shape=jax.ShapeDtypeStruct((DEPTH, MOD_ROWS, 6 * D_MODEL), _F32),
        name="modulation",
    )(cv, w_mod, b_mod.reshape(DEPTH, 1, 6 * D_MODEL))


def _split_heads(x):
    low = lax.broadcasted_iota(jnp.int32, x.shape, 1) < HEAD_DIM
    swapped = pltpu.roll(x, HEAD_DIM, axis=1)
    zero = jnp.zeros_like(x)
    return ((jnp.where(low, x, zero), jnp.where(low, zero, swapped)),
            (jnp.where(low, swapped, zero), jnp.where(low, zero, x)))


def _store_split_kv(k, v, ks_ref, vs_ref, rows):
    ones = jnp.ones_like(v)
    for kv, (k_sides, v_sides, one_sides) in enumerate(zip(_split_heads(k), _split_heads(v), _split_heads(ones))):
        for side in range(2):
            ks_ref[kv, side, rows, :] = k_sides[side].astype(_BF16)
            vs_ref[kv, side, rows, :] = jnp.concatenate([v_sides[side], one_sides[side]], axis=-1).astype(_BF16)


def _pair_softmax_av(qp, keys, values, masks, sink_even, sink_odd):
    m = qp.shape[0]
    scores = []
    for (k_left, k_right), mask in zip(keys, masks):
        s_even, s_odd = _dot_nt(qp, k_left), _dot_nt(qp, k_right)
        if mask is not None:
            s_even, s_odd = jnp.where(mask, s_even, NEG), jnp.where(mask, s_odd, NEG)
        scores.append((s_even, s_odd))
    m_even = jnp.full((m, 1), sink_even, _F32)
    m_odd = jnp.full((m, 1), sink_odd, _F32)
    for s_even, s_odd in scores:
        m_even = jnp.maximum(m_even, jnp.max(s_even, axis=-1, keepdims=True))
        m_odd = jnp.maximum(m_odd, jnp.max(s_odd, axis=-1, keepdims=True))
    res = None
    for (s_even, s_odd), (w_left, w_right) in zip(scores, values):
        r = (_dot(jnp.exp(s_even - m_even).astype(_BF16), w_left)
             + _dot(jnp.exp(s_odd - m_odd).astype(_BF16), w_right))
        res = r if res is None else res + r
    pair = 2 * HEAD_DIM
    low = lax.broadcasted_iota(jnp.int32, (m, pair), 1) < HEAD_DIM
    den = res[:, pair:] + jnp.where(low, jnp.exp(sink_even - m_even), jnp.exp(sink_odd - m_odd))
    return res[:, :pair] / den


def _attention_ctx(T, layer, qr_ref, ks_ref, vs_ref, sink_ref, oa_ref):
    pair = 2 * HEAD_DIM
    for kv in range(ATT_KV_HEADS):
        keys = [(ks_ref[kv, 0], ks_ref[kv, 1])]
        values = [(vs_ref[kv, 0], vs_ref[kv, 1])]
        for j in range(ATT_GROUP // 2):
            head = kv * ATT_GROUP + 2 * j
            cols = slice(head * HEAD_DIM, head * HEAD_DIM + pair)
            o = _pair_softmax_av(qr_ref[:, cols], keys, values, [None], sink_ref[layer, head],
                                 sink_ref[layer, head + 1])
            oa_ref[0, :, cols] = o.astype(_BF16)


def _attention_latent(T, layer, qr_ref, ks_ref, vs_ref, kc_ref, vc_ref, sink_ref, oa_ref):
    pair = 2 * HEAD_DIM
    span = 3 * ATT_BLOCK
    kc, vc = kc_ref[0, 0], vc_ref[0, 0]
    ones = jnp.ones_like(vc)
    ctx_keys = [tuple(side.astype(_BF16) for side in sides) for sides in _split_heads(kc)]
    ctx_values = [tuple(jnp.concatenate([v_side, one_side], axis=-1).astype(_BF16)
                        for v_side, one_side in zip(v_sides, one_sides))
                  for v_sides, one_sides in zip(_split_heads(vc), _split_heads(ones))]

    def block(i, carry):
        q_rows = pl.ds(pl.multiple_of(i * ATT_BLOCK, ATT_BLOCK), ATT_BLOCK)
        k_rows = pl.ds(pl.multiple_of(i * ATT_BLOCK, ATT_BLOCK), span)
        q_pos = i * ATT_BLOCK + lax.broadcasted_iota(jnp.int32, (ATT_BLOCK, span), 0)
        k_pos = (i - 1) * ATT_BLOCK + lax.broadcasted_iota(jnp.int32, (ATT_BLOCK, span), 1)
        valid = (jnp.abs(k_pos - q_pos) <= WINDOW) & (k_pos >= 0) & (k_pos < T)
        for kv in range(ATT_KV_HEADS):
            keys = [(ks_ref[kv, 0, k_rows, :], ks_ref[kv, 1, k_rows, :]), ctx_keys[kv]]
            values = [(vs_ref[kv, 0, k_rows, :], vs_ref[kv, 1, k_rows, :]), ctx_values[kv]]
            for j in range(ATT_GROUP // 2):
                head = kv * ATT_GROUP + 2 * j
                cols = slice(head * HEAD_DIM, head * HEAD_DIM + pair)
                o = _pair_softmax_av(qr_ref[q_rows, cols], keys, values, [valid, None], sink_ref[layer, head],
                                     sink_ref[layer, head + 1])
                oa_ref[0, q_rows, cols] = o.astype(_BF16)
        return carry

    lax.fori_loop(0, T // ATT_BLOCK, block, 0, unroll=2)


def _gla_chunk(z_ref, la_ref, o_ref, st_ref, a_ref, direction, start):
    C = GLA_CHUNK
    rows = pl.ds(pl.multiple_of(start, C), C)
    q = z_ref[rows, C_QB:C_QB + GLA_QK] * (GLA_DK ** -0.5)
    k = z_ref[rows, C_KB:C_KB + GLA_QK]
    v = z_ref[rows, C_VB:C_VB + GLA_VW].astype(_BF16)
    la = la_ref[rows, direction * GLA_QK:(direction + 1) * GLA_QK]
    la_hi = la.astype(_BF16)
    la_lo = (la - la_hi.astype(_F32)).astype(_BF16)
    ti = lax.broadcasted_iota(jnp.int32, (C, C), 0)
    si = lax.broadcasted_iota(jnp.int32, (C, C), 1)
    causal = (si <= ti) if direction == 0 else (si >= ti)
    tri = jnp.where(causal, 1.0, 0.0).astype(_BF16)
    b = _dot(tri, la_hi) + _dot(tri, la_lo)
    end = C - 1 if direction == 0 else 0
    b_end = b[end:end + 1]
    q_in = (q * jnp.exp(b)).astype(_BF16)
    k_out = (k * jnp.exp(b_end - b)).astype(_BF16)
    e_end = jnp.exp(b_end)
    ones = jnp.ones((8, GLA_DK), _BF16)

    def row_group(g, carry):
        base = pl.multiple_of(g * 8, 8)
        b8 = a_ref[GLA_HEADS, pl.ds(base, 8), :]
        q8 = a_ref[GLA_HEADS + 1, pl.ds(base, 8), :]
        s_idx = lax.broadcasted_iota(jnp.int32, (C, 1), 0)
        rows_h = [[] for _ in range(GLA_HEADS)]
        for j in range(8):
            ok = (s_idx <= base + j) if direction == 0 else (s_idx >= base + j)
            decay = jnp.exp(jnp.where(ok, b8[j:j + 1] - b, NEG))
            p = (q8[j:j + 1] * k * decay).astype(_BF16)
            for h in range(GLA_HEADS):
                rows_h[h].append(_dot_nt(ones, p[:, h * GLA_DK:(h + 1) * GLA_DK])[0:1])
        for h in range(GLA_HEADS):
            a_ref[h, pl.ds(base, 8), 0:C] = jnp.concatenate(rows_h[h], axis=0)
        return carry

    a_ref[GLA_HEADS] = b
    a_ref[GLA_HEADS + 1] = q
    lax.fori_loop(0, C // 8, row_group, 0)
    outs = []
    for pair in range(GLA_HEADS // 2):
        s_pair = st_ref[direction, pair]
        s_next = []
        for h in (2 * pair, 2 * pair + 1):
            kc = slice(h * GLA_DK, (h + 1) * GLA_DK)
            vc = slice(h * GLA_DV, (h + 1) * GLA_DV)
            s_t = s_pair[:, (h % 2) * GLA_DK:(h % 2 + 1) * GLA_DK]
            outs.append(_dot_nt(q_in[:, kc], s_t.astype(_BF16)) + _dot(a_ref[h, :, 0:C].astype(_BF16), v[:, vc]))
            s_next.append(s_t * e_end[:, kc] + _dot_tn(v[:, vc], k_out[:, kc]))
        st_ref[direction, pair] = jnp.concatenate(s_next, axis=-1)
    o_ref[rows, :] += jnp.concatenate(outs, axis=-1)


def _gla_block(z_ref, la_ref, o_ref, st_ref, direction, start, use_state):
    NB = GLA_BLOCK
    fwd = direction == 0
    rows = slice(start, start + NB) if isinstance(start, int) else pl.ds(pl.multiple_of(start, NB), NB)
    q = z_ref[rows, C_QB:C_QB + GLA_QK] * (GLA_DK ** -0.5)
    k = z_ref[rows, C_KB:C_KB + GLA_QK]
    v = z_ref[rows, C_VB:C_VB + GLA_VW].astype(_BF16)
    b = la_ref[rows, direction * GLA_QK:(direction + 1) * GLA_QK]
    end, mid = (NB - 1, NB // 2 - 1) if fwd else (0, NB // 2)
    b_end = b[end:end + 1]
    c = b - b[mid:mid + 1]
    q_c = (q * jnp.exp(c)).astype(_BF16)
    k_c = (k * jnp.exp(-c)).astype(_BF16)
    k_fin = (k * jnp.exp(b_end - b)).astype(_BF16)
    if use_state:
        q_in = (q * jnp.exp(b)).astype(_BF16)
        e_all = jnp.exp(b_end)
    ti = lax.broadcasted_iota(jnp.int32, (NB, NB), 0)
    si = lax.broadcasted_iota(jnp.int32, (NB, NB), 1)
    causal = (si <= ti) if fwd else (si >= ti)
    tile = 2 * GLA_DK
    low = lax.broadcasted_iota(jnp.int32, (NB, tile), 1) < GLA_DK
    zero = jnp.zeros((NB, tile), _BF16)
    pick = lambda x, parity: jnp.where(low, x, zero) if parity == 0 else jnp.where(low, zero, x)
    outs = []
    for pair in range(GLA_HEADS // 2):
        lanes = slice(pair * tile, (pair + 1) * tile)
        if use_state:
            s_pair = st_ref[direction, pair]
            s_bf = s_pair.astype(_BF16)
        s_new = None
        for parity in range(2):
            h = 2 * pair + parity
            vc = slice(h * GLA_DV, (h + 1) * GLA_DV)
            a = jnp.where(causal, _dot_nt(q_c[:, lanes], pick(k_c[:, lanes], parity)), 0.0)
            o_h = _dot(a.astype(_BF16), v[:, vc])
            if use_state:
                o_h = o_h + _dot_nt(pick(q_in[:, lanes], parity), s_bf)
            outs.append(o_h)
            upd = _dot_tn(v[:, vc], pick(k_fin[:, lanes], parity))
            s_new = upd if s_new is None else s_new + upd
        if use_state:
            s_new = s_new + s_pair * e_all[:, lanes]
        st_ref[direction, pair] = s_new
    o_ref[rows, :] += jnp.concatenate(outs, axis=-1)


def _log_decay(z_ref, rows, wg2_ref, bg2_ref):
    x = z_ref[rows, C_GL:C_GL + GL_PAD]
    x_hi = x.astype(_BF16)
    x_lo = (x - x_hi.astype(_F32)).astype(_BF16)
    w_hi, w_lo = wg2_ref[0], wg2_ref[1]
    pre = _dot(x_hi, w_hi) + (_dot(x_lo, w_hi) + _dot(x_hi, w_lo)) + bg2_ref[...]
    return _log_sigmoid(pre) * (1.0 / GLA_TAU)


def _gla_prepare(T, z_ref, la_ref, o_ref, wg2_ref, bg2_ref):
    NB = GLA_BLOCK
    ti = lax.broadcasted_iota(jnp.int32, (NB, NB), 0)
    si = lax.broadcasted_iota(jnp.int32, (NB, NB), 1)
    tri = [jnp.where(si <= ti, 1.0, 0.0).astype(_BF16), jnp.where(si >= ti, 1.0, 0.0).astype(_BF16)]
    worst = None
    for r0 in range(0, T, NB):
        rows = slice(r0, r0 + NB)
        la = _log_decay(z_ref, rows, wg2_ref, bg2_ref)
        la_hi = la.astype(_BF16)
        la_lo = (la - la_hi.astype(_F32)).astype(_BF16)
        for d in range(2):
            cols = slice(d * GLA_QK, (d + 1) * GLA_QK)
            b = _dot(tri[d], la_hi[:, cols]) + _dot(tri[d], la_lo[:, cols])
            la_ref[rows, cols] = b
            first, mid, last = (0, NB // 2 - 1, NB - 1) if d == 0 else (NB - 1, NB // 2, 0)
            span = jnp.max(jnp.maximum(b[first:first + 1] - b[mid:mid + 1], b[mid:mid + 1] - b[last:last + 1]))
            worst = span if worst is None else jnp.maximum(worst, span)
    o_ref[...] = jnp.zeros(o_ref.shape, _F32)
    return worst


def _gla_run(T, seqs, worst, a_ref, wg2_ref, bg2_ref, use_state):
    NB = GLA_BLOCK
    n_chunks = T // GLA_CHUNK
    n_blocks = T // NB
    fast_ok = worst <= GLA_MAX_EXPONENT

    @pl.when(fast_ok)
    def _():
        if n_blocks == 1:
            for direction in range(2):
                for z_ref, la_ref, o_ref, st_ref in seqs:
                    _gla_block(z_ref, la_ref, o_ref, st_ref, direction, 0, use_state)
        else:
            def body(i, carry):
                for z_ref, la_ref, o_ref, st_ref in seqs:
                    _gla_block(z_ref, la_ref, o_ref, st_ref, 0, i * NB, True)
                    _gla_block(z_ref, la_ref, o_ref, st_ref, 1, (n_blocks - 1 - i) * NB, True)
                return carry
            lax.fori_loop(0, n_blocks, body, 0, unroll=2)

    @pl.when(jnp.logical_not(fast_ok))
    def _():
        for z_ref, la_ref, o_ref, st_ref in seqs:
            for r0 in range(0, T, NB):
                la_ref[r0:r0 + NB, :] = _log_decay(z_ref, slice(r0, r0 + NB), wg2_ref, bg2_ref)

            def body(i, carry):
                _gla_chunk(z_ref, la_ref, o_ref, st_ref, a_ref, 0, i * GLA_CHUNK)
                _gla_chunk(z_ref, la_ref, o_ref, st_ref, a_ref, 1, (n_chunks - 1 - i) * GLA_CHUNK)
                return carry
            lax.fori_loop(0, n_chunks, body, 0)


def _gla_finish(T, z_ref, o_ref, ggla_ref, vones_ref, ob_ref):
    for r0 in range(0, T, PROJ_TILE):
        rows = slice(r0, r0 + PROJ_TILE)
        o = o_ref[rows, :]
        y = o * _group_rms_scale(o, vones_ref[...], GLA_DV) * ggla_ref[...]
        ob_ref[0, rows, :] = (y * _silu(z_ref[rows, C_RB:C_RB + GLA_VW])).astype(_BF16)


def _pool_halo(T):
    return POOL_HALO if T > POOL_TILE else 0


def _pool(T, z_ref, upad_ref, wpool_ref, pscale_ref, oc_ref):
    halo = _pool_halo(T)
    if halo:
        zeros = jnp.zeros((halo, POOL_W), _BF16)
        upad_ref[0:halo, :] = zeros
        upad_ref[halo + T:halo + T + halo, :] = zeros
    upad_ref[halo:halo + T, :] = z_ref[:, C_UC:C_UC + POOL_W].astype(_BF16)
    span = POOL_TILE + 2 * halo
    r = lax.broadcasted_iota(jnp.int32, (POOL_TILE, span), 0)
    c = lax.broadcasted_iota(jnp.int32, (POOL_TILE, span), 1)
    off = c - halo - r
    for jb in range(T // POOL_TILE):
        t = jb * POOL_TILE + lax.broadcasted_iota(jnp.int32, (POOL_TILE, 1), 0)
        parts = []
        for g, w in enumerate(POOL_WINDOWS):
            cols = slice(g * POOL_GROUP_DIM, (g + 1) * POOL_GROUP_DIM)
            band = jnp.where((off >= -(w // 2)) & (off < w - w // 2), 1.0, 0.0).astype(_BF16)
            win = slice(jb * POOL_TILE, jb * POOL_TILE + span)
            total = _dot(band, upad_ref[win, cols])
            cnt = (jnp.minimum(t - w // 2 + w, T) - jnp.maximum(t - w // 2, 0)).astype(_F32)
            pooled = total / cnt - z_ref[jb * POOL_TILE:(jb + 1) * POOL_TILE, C_UC + g * POOL_GROUP_DIM:
                                         C_UC + (g + 1) * POOL_GROUP_DIM]
            parts.append(_dot(pooled.astype(_BF16), wpool_ref[g]))
        y = jnp.concatenate(parts, axis=-1) * pscale_ref[...]
        oc_ref[0, jb * POOL_TILE:(jb + 1) * POOL_TILE, :] = y.astype(_BF16)


N_MIX_PARAMS = 15
N_MIX_SCRATCH = 9


def _mix_body(latent, T, layer, params, latent_refs, out_refs, scratch):
    (x_ref, mod_ref, gn1_ref, wmain_ref, wtail_ref, gqn_ref, gkn_ref, sink_ref, wg2_ref, bg2_ref, ggla_ref,
     wpool_ref, pscale_ref, hones_ref, vones_ref) = params
    n_seq = x_ref.shape[0]
    one = lambda ref, s: ref.at[pl.ds(s, 1)]
    a_ref = scratch[N_MIX_SCRATCH - 2]
    per_seq = [tuple(ref.at[s] for ref in scratch[:N_MIX_SCRATCH - 2] + scratch[N_MIX_SCRATCH - 1:])
               for s in range(n_seq)]
    pad = ATT_BLOCK if latent else 0
    shift = mod_ref[0, :, 0:D_MODEL]
    scale = mod_ref[0, :, D_MODEL:2 * D_MODEL]

    for s, (z_ref, qr_ref, ks_ref, vs_ref, la_ref, o_ref, st_ref, upad_ref) in enumerate(per_seq):
        if latent:
            for ref in (ks_ref, vs_ref):
                zeros = jnp.zeros(ref.shape[:2] + (pad, ref.shape[3]), _BF16)
                ref[:, :, 0:pad, :] = zeros
                ref[:, :, pad + T:pad + T + pad, :] = zeros
        for r0 in range(0, T, PROJ_TILE):
            rows = slice(r0, r0 + PROJ_TILE)
            x = x_ref[s, rows, :]
            hn = (x * _rms_scale(x) * gn1_ref[...]) * (1.0 + scale) + shift
            hn = hn.astype(_BF16)
            z_ref[rows, 0:C_UC] = _dot(hn, wmain_ref[...])
            z_ref[rows, C_UC:MIX_W] = _dot(hn, wtail_ref[...])
            q = z_ref[rows, C_QA:C_QA + ATT_Q]
            k = z_ref[rows, C_KA:C_KA + ATT_KV]
            q = q * _group_rms_scale(q, hones_ref[...], HEAD_DIM) * gqn_ref[...]
            k = k * _group_rms_scale(k, hones_ref[0:ATT_KV, 0:ATT_KV], HEAD_DIM) * gkn_ref[...]
            v = z_ref[rows, C_VA:C_VA + ATT_KV]
            if latent:
                cos_ref, sin_ref = latent_refs[3], latent_refs[4]
                cos = jnp.concatenate([cos_ref[rows, :]] * (ATT_Q // ATT_KV), axis=-1)
                sin = jnp.concatenate([sin_ref[rows, :]] * (ATT_Q // ATT_KV), axis=-1)
                q = _rope(q, cos, sin)
                k = _rope(k, cos_ref[rows, :], sin_ref[rows, :])
            else:
                out_refs[3][s, rows, :] = k
                out_refs[4][s, rows, :] = v
            qr_ref[rows, :] = (q * (HEAD_DIM ** -0.5)).astype(_BF16)
            _store_split_kv(k, v, ks_ref, vs_ref, slice(pad + r0, pad + r0 + PROJ_TILE))

    for s, (z_ref, qr_ref, ks_ref, vs_ref, la_ref, o_ref, st_ref, upad_ref) in enumerate(per_seq):
        if latent:
            _attention_latent(T, layer, qr_ref, ks_ref, vs_ref, one(latent_refs[0], s), one(latent_refs[1], s),
                              sink_ref, one(out_refs[0], s))
        else:
            _attention_ctx(T, layer, qr_ref, ks_ref, vs_ref, sink_ref, one(out_refs[0], s))

    worst = None
    for s, (z_ref, qr_ref, ks_ref, vs_ref, la_ref, o_ref, st_ref, upad_ref) in enumerate(per_seq):
        if latent:
            st_ref[...] = latent_refs[2][s]
        else:
            st_ref[...] = jnp.zeros(st_ref.shape, _F32)
        span = _gla_prepare(T, z_ref, la_ref, o_ref, wg2_ref, bg2_ref)
        worst = span if worst is None else jnp.maximum(worst, span)
    _gla_run(T, [(z_ref, la_ref, o_ref, st_ref) for z_ref, _, _, _, la_ref, o_ref, st_ref, _ in per_seq],
             worst, a_ref, wg2_ref, bg2_ref, latent)

    for s, (z_ref, qr_ref, ks_ref, vs_ref, la_ref, o_ref, st_ref, upad_ref) in enumerate(per_seq):
        _gla_finish(T, z_ref, o_ref, ggla_ref, vones_ref, one(out_refs[1], s))
        if not latent:
            for d in range(2):
                for pair in range(GLA_HEADS // 2):
                    out_refs[5][s, d, pair] = st_ref[d, pair].T
        _pool(T, z_ref, upad_ref, wpool_ref, pscale_ref, one(out_refs[2], s))


def _merge(x, mod_ref, oa, ob, oc, gn1_ref, wgate_ref, wa_ref, wb_ref, wc_ref, wout_ref):
    mod = lambda i: mod_ref[0, :, i * D_MODEL:(i + 1) * D_MODEL]
    hn = (x * _rms_scale(x) * gn1_ref[...]) * (1.0 + mod(1)) + mod(0)
    gates = jax.nn.sigmoid(_dot(hn.astype(_BF16), wgate_ref[...]))
    mixed = (gates[:, 0:D_MODEL] * _dot(oa, wa_ref[...])
             + gates[:, D_MODEL:2 * D_MODEL] * _dot(ob, wb_ref[...])
             + gates[:, 2 * D_MODEL:3 * D_MODEL] * _dot(oc, wc_ref[...]))
    return x + mod(2) * _dot(mixed.astype(_BF16), wout_ref[...])


def _ffn(x, mod_ref, gn2_ref, wfg_ref, wfu_ref, wfd_ref):
    mod = lambda i: mod_ref[0, :, i * D_MODEL:(i + 1) * D_MODEL]
    hn = ((x * _rms_scale(x) * gn2_ref[...]) * (1.0 + mod(4)) + mod(3)).astype(_BF16)
    h = _silu(_dot(hn, wfg_ref[...])) * _dot(hn, wfu_ref[...])
    return x + mod(5) * _dot(h.astype(_BF16), wfd_ref[...])


def _mix_latent_kernel(T, layer_ref, *refs):
    params, refs = refs[:N_MIX_PARAMS], refs[N_MIX_PARAMS:]
    _mix_body(True, T, layer_ref[0], params, refs[:5], refs[5:8], refs[8:])


def _mix_ctx_kernel(T, layer_ref, *refs):
    params, refs = refs[:N_MIX_PARAMS], refs[N_MIX_PARAMS + 3:]
    _mix_body(False, T, layer_ref[0], params, None, refs[:6], refs[6:])


def _layer_spec(shape):
    zeros = (0,) * len(shape)
    return pl.BlockSpec((None,) + tuple(shape), lambda i, layer: (layer[0],) + zeros,
                        pipeline_mode=pl.Buffered(1))


def _const_spec(blk):
    return pl.BlockSpec(blk, lambda i, layer: (0,) * len(blk), pipeline_mode=pl.Buffered(1))


def _mix_params(x, x_spec, mod_spec, mod_all, pw):
    specs = [
        x_spec, mod_spec,
        _layer_spec((1, D_MODEL)),
        _layer_spec((D_MODEL, C_UC)),
        _layer_spec((D_MODEL, MIX_W - C_UC)),
        _layer_spec((1, ATT_Q)), _layer_spec((1, ATT_KV)),
        pl.BlockSpec(memory_space=pltpu.SMEM),
        _layer_spec((2, GL_PAD, 2 * GLA_QK)), _layer_spec((1, 2 * GLA_QK)), _layer_spec((1, GLA_VW)),
        _layer_spec((POOL_GROUPS, POOL_GROUP_DIM, POOL_GROUP_DIM)), _layer_spec((1, POOL_W)),
        _const_spec((ATT_Q, ATT_Q)), _const_spec((GLA_VW, GLA_VW)),
    ]
    args = [x, mod_all, pw["g_norm1"], pw["w_in"], pw["w_tail"], pw["g_qn"], pw["g_kn"], pw["att_sink"], pw["w_gate2"],
            pw["b_gate2"], pw["g_gla_out"], pw["w_pool"], pw["pool_scale"], pw["head_ones"], pw["gla_ones"]]
    assert len(specs) == len(args) == N_MIX_PARAMS
    return specs, args


def _mix_scratch(S, T, kv_rows):
    scratch = [
        pltpu.VMEM((S, T, MIX_W), _F32),
        pltpu.VMEM((S, T, ATT_Q), _BF16),
        pltpu.VMEM((S, ATT_KV_HEADS, 2, kv_rows, 2 * HEAD_DIM), _BF16),
        pltpu.VMEM((S, ATT_KV_HEADS, 2, kv_rows, 4 * HEAD_DIM), _BF16),
        pltpu.VMEM((S, T, 2 * GLA_QK), _F32),
        pltpu.VMEM((S, T, GLA_VW), _F32),
        pltpu.VMEM((S, 2, GLA_HEADS // 2, GLA_DV, 2 * GLA_DK), _F32),
        pltpu.VMEM((GLA_HEADS + 2, GLA_CHUNK, GLA_QK), _F32),
        pltpu.VMEM((S, T + 2 * _pool_halo(T), POOL_W), _BF16),
    ]
    assert len(scratch) == N_MIX_SCRATCH
    return scratch


def _mix_latent_call(layer, x, mod_all, pw, cache_k, cache_v, st0, cos, sin):
    B, T, _ = x.shape
    per_seq = lambda blk: pl.BlockSpec(blk, lambda b, layer: (b,) + (0,) * (len(blk) - 1))
    mod_spec = pl.BlockSpec((None, 1, 1, 6 * D_MODEL), lambda b, layer: (layer[0], b + 1, 0, 0))
    x_spec = pl.BlockSpec((1, T, D_MODEL), lambda b, layer: (b, 0, 0), pipeline_mode=pl.Buffered(1))
    in_specs, args = _mix_params(x, x_spec, mod_spec, mod_all, pw)
    P = cache_k.shape[2]
    cache_spec = pl.BlockSpec((1, 1, P, ATT_KV), lambda b, layer: (b, layer[0], 0, 0))
    in_specs += [cache_spec, cache_spec,
                 pl.BlockSpec((1, None, 2, GLA_HEADS // 2, GLA_DV, 2 * GLA_DK),
                              lambda b, layer: (b, layer[0], 0, 0, 0, 0)),
                 _const_spec((T, ATT_KV)), _const_spec((T, ATT_KV))]
    args += [cache_k, cache_v, st0, cos, sin]
    widths = (ATT_Q, GLA_VW, POOL_W)
    return pl.pallas_call(
        functools.partial(_mix_latent_kernel, T),
        grid_spec=pltpu.PrefetchScalarGridSpec(
            num_scalar_prefetch=1, grid=(B,), in_specs=in_specs,
            out_specs=[per_seq((1, T, w)) for w in widths],
            scratch_shapes=_mix_scratch(1, T, T + 2 * ATT_BLOCK)),
        out_shape=[jax.ShapeDtypeStruct((B, T, w), _BF16) for w in widths],
        compiler_params=pltpu.CompilerParams(dimension_semantics=("arbitrary",), vmem_limit_bytes=VMEM_LIMIT),
        name="mix_latent",
    )(layer, *args)


def _mix_ctx_call(layer, x, mod_all, pw, stacked):
    B, T, _ = x.shape
    S = CTX_SEQS_PER_STEP
    assert B % S == 0
    per_seq = lambda blk: pl.BlockSpec(blk, lambda b, layer: (b,) + (0,) * (len(blk) - 1))
    mod_spec = pl.BlockSpec((None, 1, 1, 6 * D_MODEL), lambda b, layer: (layer[0], 0, 0, 0))
    in_specs, args = _mix_params(x, per_seq((S, T, D_MODEL)), mod_spec, mod_all, pw)
    widths = (ATT_Q, GLA_VW, POOL_W)
    n_in = 1 + len(args)
    aliases = {n_in + j: len(widths) + j for j in range(len(stacked))}
    in_specs += [pl.BlockSpec(memory_space=pl.ANY)] * len(stacked)
    args += list(stacked)
    at_layer = lambda blk: pl.BlockSpec((S, None) + blk, lambda b, layer: (b, layer[0]) + (0,) * len(blk))
    out_specs = [per_seq((S, T, w)) for w in widths] + [
        at_layer((T, ATT_KV)), at_layer((T, ATT_KV)), at_layer((2, GLA_HEADS // 2, 2 * GLA_DK, GLA_DV))]
    out_shape = ([jax.ShapeDtypeStruct((B, T, w), _BF16) for w in widths]
                 + [jax.ShapeDtypeStruct(a.shape, a.dtype) for a in stacked])
    return pl.pallas_call(
        functools.partial(_mix_ctx_kernel, T),
        grid_spec=pltpu.PrefetchScalarGridSpec(
            num_scalar_prefetch=1, grid=(B // S,), in_specs=in_specs, out_specs=out_specs,
            scratch_shapes=_mix_scratch(S, T, T)),
        out_shape=out_shape,
        input_output_aliases=aliases,
        compiler_params=pltpu.CompilerParams(dimension_semantics=("arbitrary",), vmem_limit_bytes=VMEM_LIMIT),
        name="mix_ctx",
    )(layer, *args)


def _post_kernel(layer_ref, x_ref, mod_ref, oa_ref, ob_ref, oc_ref, gn1_ref, gn2_ref, wgate_ref, wa_ref, wb_ref,
                 wc_ref, wout_ref, wfg_ref, wfu_ref, wfd_ref, out_ref):
    x = _merge(x_ref[...], mod_ref, oa_ref[...], ob_ref[...], oc_ref[...], gn1_ref, wgate_ref, wa_ref, wb_ref,
               wc_ref, wout_ref)
    out_ref[...] = _ffn(x, mod_ref, gn2_ref, wfg_ref, wfu_ref, wfd_ref)


def _post_call(layer, x2d, mod_all, oa, ob, oc, pw, tiles_per_seq):
    row = lambda w: pl.BlockSpec((POST_TILE, w), lambda i, layer: (i, 0))
    if tiles_per_seq is None:
        mod_spec = pl.BlockSpec((None, 1, 1, 6 * D_MODEL), lambda i, layer: (layer[0], 0, 0, 0))
    else:
        mod_spec = pl.BlockSpec((None, 1, 1, 6 * D_MODEL),
                                lambda i, layer: (layer[0], 1 + i // tiles_per_seq, 0, 0))
    weights = [(pw["g_norm1"], (1, D_MODEL)), (pw["g_norm2"], (1, D_MODEL)), (pw["w_gates"], (D_MODEL, GATE_W)),
               (pw["w_br_a"], (ATT_Q, D_MODEL)), (pw["w_br_b"], (GLA_VW, D_MODEL)),
               (pw["w_br_c"], (POOL_W, D_MODEL)), (pw["w_out"], (D_MODEL, D_MODEL)),
               (pw["w_ff_gate"], (D_MODEL, D_FF)), (pw["w_ff_up"], (D_MODEL, D_FF)),
               (pw["w_ff_down"], (D_FF, D_MODEL))]
    in_specs = ([row(D_MODEL), mod_spec, row(ATT_Q), row(GLA_VW), row(POOL_W)]
                + [_layer_spec(shape) for _, shape in weights])
    return pl.pallas_call(
        _post_kernel,
        grid_spec=pltpu.PrefetchScalarGridSpec(
            num_scalar_prefetch=1, grid=(x2d.shape[0] // POST_TILE,), in_specs=in_specs, out_specs=row(D_MODEL)),
        out_shape=jax.ShapeDtypeStruct(x2d.shape, _F32),
        input_output_aliases={1: 0},
        compiler_params=pltpu.CompilerParams(dimension_semantics=("arbitrary",), vmem_limit_bytes=VMEM_LIMIT),
        name="post",
    )(layer, x2d, mod_all, oa, ob, oc, *[a for a, _ in weights])


def _rope_tables(T):
    quarter = HEAD_DIM // 4
    inv_freq = ROPE_BASE ** (-np.arange(quarter, dtype=np.float32) / quarter)
    pos = np.arange(T)
    ang_row = (pos // GRID_W).astype(np.float32)[:, None] * inv_freq[None, :]
    ang_col = (pos % GRID_W).astype(np.float32)[:, None] * inv_freq[None, :]
    cos = np.concatenate([np.cos(ang_row)] * 2 + [np.cos(ang_col)] * 2, axis=-1)
    sin = np.concatenate([-np.sin(ang_row), np.sin(ang_row), -np.sin(ang_col), np.sin(ang_col)], axis=-1)
    return (jnp.asarray(np.tile(cos, (1, ATT_KV_HEADS)), _F32), jnp.asarray(np.tile(sin, (1, ATT_KV_HEADS)), _F32))


def _prepare_weights(w_in, g_qn, g_kn, att_sink, w_gate2, b_gate2, g_gla_out, w_pool, pool_scale, w_br_a,
                     w_br_b, w_br_c, w_out, g_norm1, g_norm2, w_ff_gate, w_ff_up, w_ff_down):
    o_gl = ATT_Q + 2 * ATT_KV + 2 * GLA_QK + 2 * GLA_VW
    o_uc = o_gl + 2 * GLA_RANK
    o_gate = o_uc + POOL_W
    assert o_gl == C_UC
    w_in = w_in.astype(_BF16)
    w_tail = jnp.concatenate(
        [w_in[:, :, o_uc:o_gate], w_in[:, :, o_gl:o_uc],
         jnp.zeros((DEPTH, D_MODEL, GL_PAD - 2 * GLA_RANK), w_in.dtype)], axis=2)
    w_gates = w_in[:, :, o_gate:]
    wg2 = jnp.zeros((DEPTH, GL_PAD, 2 * GLA_QK), _F32)
    wg2 = wg2.at[:, 0:GLA_RANK, 0:GLA_QK].set(w_gate2[:, 0])
    wg2 = wg2.at[:, GLA_RANK:2 * GLA_RANK, GLA_QK:].set(w_gate2[:, 1])
    wg2_hi = wg2.astype(_BF16)
    wg2 = jnp.stack([wg2_hi, (wg2 - wg2_hi.astype(_F32)).astype(_BF16)], axis=1)
    vec = lambda a: a.reshape(DEPTH, 1, -1)
    group_ones = lambda n, width: jnp.asarray(
        (np.arange(n)[:, None] // width) == (np.arange(n)[None, :] // width), _BF16)
    return {
        "head_ones": group_ones(ATT_Q, HEAD_DIM),
        "gla_ones": group_ones(GLA_VW, GLA_DV),
        "w_in": w_in,
        "w_tail": w_tail,
        "w_gates": w_gates,
        "g_qn": vec(jnp.tile(g_qn, (1, ATT_HEADS))),
        "g_kn": vec(jnp.tile(g_kn, (1, ATT_KV_HEADS))),
        "att_sink": att_sink,
        "w_gate2": wg2,
        "b_gate2": vec(b_gate2),
        "g_gla_out": vec(jnp.tile(g_gla_out, (1, GLA_HEADS))),
        "w_pool": w_pool.astype(_BF16),
        "pool_scale": vec(pool_scale),
        "w_br_a": w_br_a.astype(_BF16),
        "w_br_b": w_br_b.astype(_BF16),
        "w_br_c": w_br_c.astype(_BF16),
        "w_out": w_out.astype(_BF16),
        "g_norm1": vec(g_norm1),
        "g_norm2": vec(g_norm2),
        "w_ff_gate": w_ff_gate.astype(_BF16),
        "w_ff_up": w_ff_up.astype(_BF16),
        "w_ff_down": w_ff_down.astype(_BF16),
    }


def kernel(x_prompt, x_sample, c, cache_k, cache_v, state_gla, c_ctx, w_in, g_qn, g_kn, att_sink, w_gate2,
           b_gate2, g_gla_out, w_pool, pool_scale, w_br_a, w_br_b, w_br_c, w_out, g_norm1, g_norm2, w_mod,
           b_mod, w_ff_gate, w_ff_up, w_ff_down):
    B, T, _ = x_prompt.shape
    BL, TL, _ = x_sample.shape
    assert (B * T) % POST_TILE == 0 and TL % POST_TILE == 0 and BL + 1 <= MOD_ROWS
    assert T % PROJ_TILE == 0 and TL % PROJ_TILE == 0
    cv = jnp.concatenate([c_ctx[None, :], c, jnp.zeros((MOD_ROWS - 1 - BL, D_MODEL), _F32)], axis=0)
    mod_all = _modulation(cv, w_mod, b_mod).reshape(DEPTH, MOD_ROWS, 1, 6 * D_MODEL)
    pw = _prepare_weights(w_in, g_qn, g_kn, att_sink, w_gate2, b_gate2, g_gla_out, w_pool, pool_scale, w_br_a,
                          w_br_b, w_br_c, w_out, g_norm1, g_norm2, w_ff_gate, w_ff_up, w_ff_down)
    cos, sin = _rope_tables(TL)
    P = cache_k.shape[2]
    latent_ctx = (cache_k.reshape(BL, DEPTH, P, ATT_KV), cache_v.reshape(BL, DEPTH, P, ATT_KV),
                  jnp.swapaxes(state_gla.reshape(BL, DEPTH, 2, GLA_HEADS // 2, 2 * GLA_DK, GLA_DV), -1, -2),
                  cos, sin)

    def layer_step(l, carry):
        yp, ys, new_k, new_v, new_st = carry
        layer = jnp.full((1,), l, jnp.int32)
        oa, ob, oc, new_k, new_v, new_st = _mix_ctx_call(layer, yp, mod_all, pw, (new_k, new_v, new_st))
        yp = _post_call(layer, yp.reshape(B * T, D_MODEL), mod_all, oa.reshape(B * T, -1), ob.reshape(B * T, -1),
                        oc.reshape(B * T, -1), pw, None).reshape(B, T, D_MODEL)
        oa, ob, oc = _mix_latent_call(layer, ys, mod_all, pw, *latent_ctx)
        ys = _post_call(layer, ys.reshape(BL * TL, D_MODEL), mod_all, oa.reshape(BL * TL, -1),
                        ob.reshape(BL * TL, -1), oc.reshape(BL * TL, -1), pw,
                        TL // POST_TILE).reshape(BL, TL, D_MODEL)
        return yp, ys, new_k, new_v, new_st

    init = (x_prompt, x_sample,
            jnp.zeros((B, DEPTH, T, ATT_KV), _F32), jnp.zeros((B, DEPTH, T, ATT_KV), _F32),
            jnp.zeros((B, DEPTH, 2, GLA_HEADS // 2, 2 * GLA_DK, GLA_DV), _F32))
    yp, ys, new_k, new_v, new_st = lax.fori_loop(0, DEPTH, layer_step, init)
    return (yp, ys, new_k.reshape(B, DEPTH, T, ATT_KV_HEADS, HEAD_DIM),
            new_v.reshape(B, DEPTH, T, ATT_KV_HEADS, HEAD_DIM),
            new_st.reshape(B, DEPTH, 2, GLA_HEADS, GLA_DK, GLA_DV))
```

```python
import functools

import jax
import jax.numpy as jnp
import numpy as np
from jax import lax
from jax.experimental import pallas as pl
from jax.experimental.pallas import tpu as pltpu

D_MODEL = 1024
DEPTH = 4
GRID_W = 64
ATT_HEADS = 8
ATT_KV_HEADS = 2
ATT_GROUP = ATT_HEADS // ATT_KV_HEADS
HEAD_DIM = 64
WINDOW = 128
ATT_BLOCK = 128
ROPE_BASE = 10000.0
GLA_HEADS = 4
GLA_DK = 64
GLA_DV = 128
GLA_RANK = 16
GLA_TAU = 16.0
GLA_CHUNK = 64
POOL_GROUPS = 4
POOL_GROUP_DIM = 128
POOL_WINDOWS = (2, 4, 8, 16)
D_FF = 2816
ATT_Q = ATT_HEADS * HEAD_DIM
ATT_KV = ATT_KV_HEADS * HEAD_DIM
GLA_QK = GLA_HEADS * GLA_DK
GLA_VW = GLA_HEADS * GLA_DV
POOL_W = POOL_GROUPS * POOL_GROUP_DIM
EPS = 1e-6
NEG = -1e30

C_QA = 0
C_KA = C_QA + ATT_Q
C_VA = C_KA + ATT_KV
C_QB = C_VA + ATT_KV
C_KB = C_QB + GLA_QK
C_VB = C_KB + GLA_QK
C_RB = C_VB + GLA_VW
C_UC = C_RB + GLA_VW
C_GL = C_UC + POOL_W
GL_PAD = 128
MIX_W = C_GL + GL_PAD
GATE_W = 3 * D_MODEL

POST_TILE = 256
PROJ_TILE = 256
CTX_SEQS_PER_STEP = 2
POOL_TILE = 256
POOL_HALO = 128
MOD_ROWS = 8
MOD_TILE = 1024
GLA_BLOCK = 256
GLA_MAX_EXPONENT = 80.0
VMEM_LIMIT = 56 * 1024 * 1024

_F32 = jnp.float32
_BF16 = jnp.bfloat16


def _dot(a, b):
    return jnp.dot(a, b, preferred_element_type=_F32)


def _dot_nt(a, b):
    return lax.dot_general(a, b, (((1,), (1,)), ((), ())), preferred_element_type=_F32)


def _dot_tn(a, b):
    return lax.dot_general(a, b, (((0,), (0,)), ((), ())), preferred_element_type=_F32)


def _rms_scale(x):
    return lax.rsqrt(jnp.mean(x * x, axis=-1, keepdims=True) + EPS)


def _group_rms_scale(x, group_ones, width):
    return lax.rsqrt(_dot((x * x).astype(_BF16), group_ones) * (1.0 / width) + EPS)


def _log_sigmoid(x):
    return jnp.minimum(x, 0.0) - jnp.log1p(jnp.exp(-jnp.abs(x)))


def _silu(x):
    return x * jax.nn.sigmoid(x)


def _rope(x, cos, sin_signed):
    n = x.shape[-1]
    lane = lax.broadcasted_iota(jnp.int32, x.shape, 1)
    up = pltpu.roll(x, n - HEAD_DIM // 4, axis=1)
    down = pltpu.roll(x, HEAD_DIM // 4, axis=1)
    partner = jnp.where((lane & (HEAD_DIM // 2 - 1)) < HEAD_DIM // 4, up, down)
    return x * cos + partner * sin_signed


def _mod_kernel(cv_ref, w_ref, b_ref, out_ref):
    s = _silu(cv_ref[...]).astype(_BF16)
    out_ref[0] = _dot(s, w_ref[0].astype(_BF16)) + b_ref[0]


def _modulation(cv, w_mod, b_mod):
    n_col = (6 * D_MODEL) // MOD_TILE
    return pl.pallas_call(
        _mod_kernel,
        grid=(DEPTH, n_col),
        in_specs=[
            pl.BlockSpec((MOD_ROWS, D_MODEL), lambda l, j: (0, 0)),
            pl.BlockSpec((1, D_MODEL, MOD_TILE), lambda l, j: (l, 0, j)),
            pl.BlockSpec((1, 1, MOD_TILE), lambda l, j: (l, 0, j)),
        ],
        out_specs=pl.BlockSpec((1, MOD_ROWS, MOD_TILE), lambda l, j: (l, 0, j)),
        out_shape=jax.ShapeDtypeStruct((DEPTH, MOD_ROWS, 6 * D_MODEL), _F32),
        name="modulation",
    )(cv, w_mod, b_mod.reshape(DEPTH, 1, 6 * D_MODEL))


def _split_heads(x):
    low = lax.broadcasted_iota(jnp.int32, x.shape, 1) < HEAD_DIM
    swapped = pltpu.roll(x, HEAD_DIM, axis=1)
    zero = jnp.zeros_like(x)
    return ((jnp.where(low, x, zero), jnp.where(low, zero, swapped)),
            (jnp.where(low, swapped, zero), jnp.where(low, zero, x)))


def _store_split_kv(k, v, ks_ref, vs_ref, rows):
    ones = jnp.ones_like(v)
    for kv, (k_sides, v_sides, one_sides) in enumerate(zip(_split_heads(k), _split_heads(v), _split_heads(ones))):
        for side in range(2):
            ks_ref[kv, side, rows, :] = k_sides[side].astype(_BF16)
            vs_ref[kv, side, rows, :] = jnp.concatenate([v_sides[side], one_sides[side]], axis=-1).astype(_BF16)


def _pair_softmax_av(qp, keys, values, masks, sink_even, sink_odd):
    m = qp.shape[0]
    scores = []
    for (k_left, k_right), mask in zip(keys, masks):
        s_even, s_odd = _dot_nt(qp, k_left), _dot_nt(qp, k_right)
        if mask is not None:
            s_even, s_odd = jnp.where(mask, s_even, NEG), jnp.where(mask, s_odd, NEG)
        scores.append((s_even, s_odd))
    m_even = jnp.full((m, 1), sink_even, _F32)
    m_odd = jnp.full((m, 1), sink_odd, _F32)
    for s_even, s_odd in scores:
        m_even = jnp.maximum(m_even, jnp.max(s_even, axis=-1, keepdims=True))
        m_odd = jnp.maximum(m_odd, jnp.max(s_odd, axis=-1, keepdims=True))
    res = None
    for (s_even, s_odd), (w_left, w_right) in zip(scores, values):
        r = (_dot(jnp.exp(s_even - m_even).astype(_BF16), w_left)
             + _dot(jnp.exp(s_odd - m_odd).astype(_BF16), w_right))
        res = r if res is None else res + r
    pair = 2 * HEAD_DIM
    low = lax.broadcasted_iota(jnp.int32, (m, pair), 1) < HEAD_DIM
    den = res[:, pair:] + jnp.where(low, jnp.exp(sink_even - m_even), jnp.exp(sink_odd - m_odd))
    return res[:, :pair] / den


def _attention_ctx(T, layer, qr_ref, ks_ref, vs_ref, sink_ref, oa_ref):
    pair = 2 * HEAD_DIM
    for kv in range(ATT_KV_HEADS):
        keys = [(ks_ref[kv, 0], ks_ref[kv, 1])]
        values = [(vs_ref[kv, 0], vs_ref[kv, 1])]
        for j in range(ATT_GROUP // 2):
            head = kv * ATT_GROUP + 2 * j
            cols = slice(head * HEAD_DIM, head * HEAD_DIM + pair)
            o = _pair_softmax_av(qr_ref[:, cols], keys, values, [None], sink_ref[layer, head],
                                 sink_ref[layer, head + 1])
            oa_ref[0, :, cols] = o.astype(_BF16)


def _attention_latent(T, layer, qr_ref, ks_ref, vs_ref, kc_ref, vc_ref, sink_ref, oa_ref):
    pair = 2 * HEAD_DIM
    span = 3 * ATT_BLOCK
    kc, vc = kc_ref[0, 0], vc_ref[0, 0]
    ones = jnp.ones_like(vc)
    ctx_keys = [tuple(side.astype(_BF16) for side in sides) for sides in _split_heads(kc)]
    ctx_values = [tuple(jnp.concatenate([v_side, one_side], axis=-1).astype(_BF16)
                        for v_side, one_side in zip(v_sides, one_sides))
                  for v_sides, one_sides in zip(_split_heads(vc), _split_heads(ones))]

    def block(i, carry):
        q_rows = pl.ds(pl.multiple_of(i * ATT_BLOCK, ATT_BLOCK), ATT_BLOCK)
        k_rows = pl.ds(pl.multiple_of(i * ATT_BLOCK, ATT_BLOCK), span)
        q_pos = i * ATT_BLOCK + lax.broadcasted_iota(jnp.int32, (ATT_BLOCK, span), 0)
        k_pos = (i - 1) * ATT_BLOCK + lax.broadcasted_iota(jnp.int32, (ATT_BLOCK, span), 1)
        valid = (jnp.abs(k_pos - q_pos) <= WINDOW) & (k_pos >= 0) & (k_pos < T)
        for kv in range(ATT_KV_HEADS):
            keys = [(ks_ref[kv, 0, k_rows, :], ks_ref[kv, 1, k_rows, :]), ctx_keys[kv]]
            values = [(vs_ref[kv, 0, k_rows, :], vs_ref[kv, 1, k_rows, :]), ctx_values[kv]]
            for j in range(ATT_GROUP // 2):
                head = kv * ATT_GROUP + 2 * j
                cols = slice(head * HEAD_DIM, head * HEAD_DIM + pair)
                o = _pair_softmax_av(qr_ref[q_rows, cols], keys, values, [valid, None], sink_ref[layer, head],
                                     sink_ref[layer, head + 1])
                oa_ref[0, q_rows, cols] = o.astype(_BF16)
        return carry

    lax.fori_loop(0, T // ATT_BLOCK, block, 0, unroll=2)


def _gla_chunk(z_ref, la_ref, o_ref, st_ref, a_ref, direction, start):
    C = GLA_CHUNK
    rows = pl.ds(pl.multiple_of(start, C), C)
    q = z_ref[rows, C_QB:C_QB + GLA_QK] * (GLA_DK ** -0.5)
    k = z_ref[rows, C_KB:C_KB + GLA_QK]
    v = z_ref[rows, C_VB:C_VB + GLA_VW].astype(_BF16)
    la = la_ref[rows, direction * GLA_QK:(direction + 1) * GLA_QK]
    la_hi = la.astype(_BF16)
    la_lo = (la - la_hi.astype(_F32)).astype(_BF16)
    ti = lax.broadcasted_iota(jnp.int32, (C, C), 0)
    si = lax.broadcasted_iota(jnp.int32, (C, C), 1)
    causal = (si <= ti) if direction == 0 else (si >= ti)
    tri = jnp.where(causal, 1.0, 0.0).astype(_BF16)
    b = _dot(tri, la_hi) + _dot(tri, la_lo)
    end = C - 1 if direction == 0 else 0
    b_end = b[end:end + 1]
    q_in = (q * jnp.exp(b)).astype(_BF16)
    k_out = (k * jnp.exp(b_end - b)).astype(_BF16)
    e_end = jnp.exp(b_end)
    ones = jnp.ones((8, GLA_DK), _BF16)

    def row_group(g, carry):
        base = pl.multiple_of(g * 8, 8)
        b8 = a_ref[GLA_HEADS, pl.ds(base, 8), :]
        q8 = a_ref[GLA_HEADS + 1, pl.ds(base, 8), :]
        s_idx = lax.broadcasted_iota(jnp.int32, (C, 1), 0)
        rows_h = [[] for _ in range(GLA_HEADS)]
        for j in range(8):
            ok = (s_idx <= base + j) if direction == 0 else (s_idx >= base + j)
            decay = jnp.exp(jnp.where(ok, b8[j:j + 1] - b, NEG))
            p = (q8[j:j + 1] * k * decay).astype(_BF16)
            for h in range(GLA_HEADS):
                rows_h[h].append(_dot_nt(ones, p[:, h * GLA_DK:(h + 1) * GLA_DK])[0:1])
        for h in range(GLA_HEADS):
            a_ref[h, pl.ds(base, 8), 0:C] = jnp.concatenate(rows_h[h], axis=0)
        return carry

    a_ref[GLA_HEADS] = b
    a_ref[GLA_HEADS + 1] = q
    lax.fori_loop(0, C // 8, row_group, 0)
    outs = []
    for pair in range(GLA_HEADS // 2):
        s_pair = st_ref[direction, pair]
        s_next = []
        for h in (2 * pair, 2 * pair + 1):
            kc = slice(h * GLA_DK, (h + 1) * GLA_DK)
            vc = slice(h * GLA_DV, (h + 1) * GLA_DV)
            s_t = s_pair[:, (h % 2) * GLA_DK:(h % 2 + 1) * GLA_DK]
            outs.append(_dot_nt(q_in[:, kc], s_t.astype(_BF16)) + _dot(a_ref[h, :, 0:C].astype(_BF16), v[:, vc]))
            s_next.append(s_t * e_end[:, kc] + _dot_tn(v[:, vc], k_out[:, kc]))
        st_ref[direction, pair] = jnp.concatenate(s_next, axis=-1)
    o_ref[rows, :] += jnp.concatenate(outs, axis=-1)


def _gla_block(z_ref, la_ref, o_ref, st_ref, direction, start, use_state):
    NB = GLA_BLOCK
    fwd = direction == 0
    rows = slice(start, start + NB) if isinstance(start, int) else pl.ds(pl.multiple_of(start, NB), NB)
    q = z_ref[rows, C_QB:C_QB + GLA_QK] * (GLA_DK ** -0.5)
    k = z_ref[rows, C_KB:C_KB + GLA_QK]
    v = z_ref[rows, C_VB:C_VB + GLA_VW].astype(_BF16)
    b = la_ref[rows, direction * GLA_QK:(direction + 1) * GLA_QK]
    end, mid = (NB - 1, NB // 2 - 1) if fwd else (0, NB // 2)
    b_end = b[end:end + 1]
    c = b - b[mid:mid + 1]
    q_c = (q * jnp.exp(c)).astype(_BF16)
    k_c = (k * jnp.exp(-c)).astype(_BF16)
    k_fin = (k * jnp.exp(b_end - b)).astype(_BF16)
    if use_state:
        q_in = (q * jnp.exp(b)).astype(_BF16)
        e_all = jnp.exp(b_end)
    ti = lax.broadcasted_iota(jnp.int32, (NB, NB), 0)
    si = lax.broadcasted_iota(jnp.int32, (NB, NB), 1)
    causal = (si <= ti) if fwd else (si >= ti)
    tile = 2 * GLA_DK
    low = lax.broadcasted_iota(jnp.int32, (NB, tile), 1) < GLA_DK
    zero = jnp.zeros((NB, tile), _BF16)
    pick = lambda x, parity: jnp.where(low, x, zero) if parity == 0 else jnp.where(low, zero, x)
    outs = []
    for pair in range(GLA_HEADS // 2):
        lanes = slice(pair * tile, (pair + 1) * tile)
        if use_state:
            s_pair = st_ref[direction, pair]
            s_bf = s_pair.astype(_BF16)
        s_new = None
        for parity in range(2):
            h = 2 * pair + parity
            vc = slice(h * GLA_DV, (h + 1) * GLA_DV)
            a = jnp.where(causal, _dot_nt(q_c[:, lanes], pick(k_c[:, lanes], parity)), 0.0)
            o_h = _dot(a.astype(_BF16), v[:, vc])
            if use_state:
                o_h = o_h + _dot_nt(pick(q_in[:, lanes], parity), s_bf)
            outs.append(o_h)
            upd = _dot_tn(v[:, vc], pick(k_fin[:, lanes], parity))
            s_new = upd if s_new is None else s_new + upd
        if use_state:
            s_new = s_new + s_pair * e_all[:, lanes]
        st_ref[direction, pair] = s_new
    o_ref[rows, :] += jnp.concatenate(outs, axis=-1)


def _log_decay(z_ref, rows, wg2_ref, bg2_ref):
    x = z_ref[rows, C_GL:C_GL + GL_PAD]
    x_hi = x.astype(_BF16)
    x_lo = (x - x_hi.astype(_F32)).astype(_BF16)
    w_hi, w_lo = wg2_ref[0], wg2_ref[1]
    pre = _dot(x_hi, w_hi) + (_dot(x_lo, w_hi) + _dot(x_hi, w_lo)) + bg2_ref[...]
    return _log_sigmoid(pre) * (1.0 / GLA_TAU)


def _gla_prepare(T, z_ref, la_ref, o_ref, wg2_ref, bg2_ref):
    NB = GLA_BLOCK
    ti = lax.broadcasted_iota(jnp.int32, (NB, NB), 0)
    si = lax.broadcasted_iota(jnp.int32, (NB, NB), 1)
    tri = [jnp.where(si <= ti, 1.0, 0.0).astype(_BF16), jnp.where(si >= ti, 1.0, 0.0).astype(_BF16)]
    worst = None
    for r0 in range(0, T, NB):
        rows = slice(r0, r0 + NB)
        la = _log_decay(z_ref, rows, wg2_ref, bg2_ref)
        la_hi = la.astype(_BF16)
        la_lo = (la - la_hi.astype(_F32)).astype(_BF16)
        for d in range(2):
            cols = slice(d * GLA_QK, (d + 1) * GLA_QK)
            b = _dot(tri[d], la_hi[:, cols]) + _dot(tri[d], la_lo[:, cols])
            la_ref[rows, cols] = b
            first, mid, last = (0, NB // 2 - 1, NB - 1) if d == 0 else (NB - 1, NB // 2, 0)
            span = jnp.max(jnp.maximum(b[first:first + 1] - b[mid:mid + 1], b[mid:mid + 1] - b[last:last + 1]))
            worst = span if worst is None else jnp.maximum(worst, span)
    o_ref[...] = jnp.zeros(o_ref.shape, _F32)
    return worst


def _gla_run(T, seqs, worst, a_ref, wg2_ref, bg2_ref, use_state):
    NB = GLA_BLOCK
    n_chunks = T // GLA_CHUNK
    n_blocks = T // NB
    fast_ok = worst <= GLA_MAX_EXPONENT

    @pl.when(fast_ok)
    def _():
        if n_blocks == 1:
            for direction in range(2):
                for z_ref, la_ref, o_ref, st_ref in seqs:
                    _gla_block(z_ref, la_ref, o_ref, st_ref, direction, 0, use_state)
        else:
            def body(i, carry):
                for z_ref, la_ref, o_ref, st_ref in seqs:
                    _gla_block(z_ref, la_ref, o_ref, st_ref, 0, i * NB, True)
                    _gla_block(z_ref, la_ref, o_ref, st_ref, 1, (n_blocks - 1 - i) * NB, True)
                return carry
            lax.fori_loop(0, n_blocks, body, 0, unroll=2)

    @pl.when(jnp.logical_not(fast_ok))
    def _():
        for z_ref, la_ref, o_ref, st_ref in seqs:
            for r0 in range(0, T, NB):
                la_ref[r0:r0 + NB, :] = _log_decay(z_ref, slice(r0, r0 + NB), wg2_ref, bg2_ref)

            def body(i, carry):
                _gla_chunk(z_ref, la_ref, o_ref, st_ref, a_ref, 0, i * GLA_CHUNK)
                _gla_chunk(z_ref, la_ref, o_ref, st_ref, a_ref, 1, (n_chunks - 1 - i) * GLA_CHUNK)
                return carry
            lax.fori_loop(0, n_chunks, body, 0)


def _gla_finish(T, z_ref, o_ref, ggla_ref, vones_ref, ob_ref):
    for r0 in range(0, T, PROJ_TILE):
        rows = slice(r0, r0 + PROJ_TILE)
        o = o_ref[rows, :]
        y = o * _group_rms_scale(o, vones_ref[...], GLA_DV) * ggla_ref[...]
        ob_ref[0, rows, :] = (y * _silu(z_ref[rows, C_RB:C_RB + GLA_VW])).astype(_BF16)


def _pool_halo(T):
    return POOL_HALO if T > POOL_TILE else 0


def _pool(T, z_ref, upad_ref, wpool_ref, pscale_ref, oc_ref):
    halo = _pool_halo(T)
    if halo:
        zeros = jnp.zeros((halo, POOL_W), _BF16)
        upad_ref[0:halo, :] = zeros
        upad_ref[halo + T:halo + T + halo, :] = zeros
    upad_ref[halo:halo + T, :] = z_ref[:, C_UC:C_UC + POOL_W].astype(_BF16)
    span = POOL_TILE + 2 * halo
    r = lax.broadcasted_iota(jnp.int32, (POOL_TILE, span), 0)
    c = lax.broadcasted_iota(jnp.int32, (POOL_TILE, span), 1)
    off = c - halo - r
    for jb in range(T // POOL_TILE):
        t = jb * POOL_TILE + lax.broadcasted_iota(jnp.int32, (POOL_TILE, 1), 0)
        parts = []
        for g, w in enumerate(POOL_WINDOWS):
            cols = slice(g * POOL_GROUP_DIM, (g + 1) * POOL_GROUP_DIM)
            band = jnp.where((off >= -(w // 2)) & (off < w - w // 2), 1.0, 0.0).astype(_BF16)
            win = slice(jb * POOL_TILE, jb * POOL_TILE + span)
            total = _dot(band, upad_ref[win, cols])
            cnt = (jnp.minimum(t - w // 2 + w, T) - jnp.maximum(t - w // 2, 0)).astype(_F32)
            pooled = total / cnt - z_ref[jb * POOL_TILE:(jb + 1) * POOL_TILE, C_UC + g * POOL_GROUP_DIM:
                                         C_UC + (g + 1) * POOL_GROUP_DIM]
            parts.append(_dot(pooled.astype(_BF16), wpool_ref[g]))
        y = jnp.concatenate(parts, axis=-1) * pscale_ref[...]
        oc_ref[0, jb * POOL_TILE:(jb + 1) * POOL_TILE, :] = y.astype(_BF16)


N_MIX_PARAMS = 15
N_MIX_SCRATCH = 9


def _mix_body(latent, T, layer, params, latent_refs, out_refs, scratch):
    (x_ref, mod_ref, gn1_ref, wmain_ref, wtail_ref, gqn_ref, gkn_ref, sink_ref, wg2_ref, bg2_ref, ggla_ref,
     wpool_ref, pscale_ref, hones_ref, vones_ref) = params
    n_seq = x_ref.shape[0]
    one = lambda ref, s: ref.at[pl.ds(s, 1)]
    a_ref = scratch[N_MIX_SCRATCH - 2]
    per_seq = [tuple(ref.at[s] for ref in scratch[:N_MIX_SCRATCH - 2] + scratch[N_MIX_SCRATCH - 1:])
               for s in range(n_seq)]
    pad = ATT_BLOCK if latent else 0
    shift = mod_ref[0, :, 0:D_MODEL]
    scale = mod_ref[0, :, D_MODEL:2 * D_MODEL]

    for s, (z_ref, qr_ref, ks_ref, vs_ref, la_ref, o_ref, st_ref, upad_ref) in enumerate(per_seq):
        if latent:
            for ref in (ks_ref, vs_ref):
                zeros = jnp.zeros(ref.shape[:2] + (pad, ref.shape[3]), _BF16)
                ref[:, :, 0:pad, :] = zeros
                ref[:, :, pad + T:pad + T + pad, :] = zeros
        for r0 in range(0, T, PROJ_TILE):
            rows = slice(r0, r0 + PROJ_TILE)
            x = x_ref[s, rows, :]
            hn = (x * _rms_scale(x) * gn1_ref[...]) * (1.0 + scale) + shift
            hn = hn.astype(_BF16)
            z_ref[rows, 0:C_UC] = _dot(hn, wmain_ref[...])
            z_ref[rows, C_UC:MIX_W] = _dot(hn, wtail_ref[...])
            q = z_ref[rows, C_QA:C_QA + ATT_Q]
            k = z_ref[rows, C_KA:C_KA + ATT_KV]
            q = q * _group_rms_scale(q, hones_ref[...], HEAD_DIM) * gqn_ref[...]
            k = k * _group_rms_scale(k, hones_ref[0:ATT_KV, 0:ATT_KV], HEAD_DIM) * gkn_ref[...]
            v = z_ref[rows, C_VA:C_VA + ATT_KV]
            if latent:
                cos_ref, sin_ref = latent_refs[3], latent_refs[4]
                cos = jnp.concatenate([cos_ref[rows, :]] * (ATT_Q // ATT_KV), axis=-1)
                sin = jnp.concatenate([sin_ref[rows, :]] * (ATT_Q // ATT_KV), axis=-1)
                q = _rope(q, cos, sin)
                k = _rope(k, cos_ref[rows, :], sin_ref[rows, :])
            else:
                out_refs[3][s, rows, :] = k
                out_refs[4][s, rows, :] = v
            qr_ref[rows, :] = (q * (HEAD_DIM ** -0.5)).astype(_BF16)
            _store_split_kv(k, v, ks_ref, vs_ref, slice(pad + r0, pad + r0 + PROJ_TILE))

    for s, (z_ref, qr_ref, ks_ref, vs_ref, la_ref, o_ref, st_ref, upad_ref) in enumerate(per_seq):
        if latent:
            _attention_latent(T, layer, qr_ref, ks_ref, vs_ref, one(latent_refs[0], s), one(latent_refs[1], s),
                              sink_ref, one(out_refs[0], s))
        else:
            _attention_ctx(T, layer, qr_ref, ks_ref, vs_ref, sink_ref, one(out_refs[0], s))

    worst = None
    for s, (z_ref, qr_ref, ks_ref, vs_ref, la_ref, o_ref, st_ref, upad_ref) in enumerate(per_seq):
        if latent:
            st_ref[...] = latent_refs[2][s]
        else:
            st_ref[...] = jnp.zeros(st_ref.shape, _F32)
        span = _gla_prepare(T, z_ref, la_ref, o_ref, wg2_ref, bg2_ref)
        worst = span if worst is None else jnp.maximum(worst, span)
    _gla_run(T, [(z_ref, la_ref, o_ref, st_ref) for z_ref, _, _, _, la_ref, o_ref, st_ref, _ in per_seq],
             worst, a_ref, wg2_ref, bg2_ref, latent)

    for s, (z_ref, qr_ref, ks_ref, vs_ref, la_ref, o_ref, st_ref, upad_ref) in enumerate(per_seq):
        _gla_finish(T, z_ref, o_ref, ggla_ref, vones_ref, one(out_refs[1], s))
        if not latent:
            for d in range(2):
                for pair in range(GLA_HEADS // 2):
                    out_refs[5][s, d, pair] = st_ref[d, pair].T
        _pool(T, z_ref, upad_ref, wpool_ref, pscale_ref, one(out_refs[2], s))


def _merge(x, mod_ref, oa, ob, oc, gn1_ref, wgate_ref, wa_ref, wb_ref, wc_ref, wout_ref):
    mod = lambda i: mod_ref[0, :, i * D_MODEL:(i + 1) * D_MODEL]
    hn = (x * _rms_scale(x) * gn1_ref[...]) * (1.0 + mod(1)) + mod(0)
    gates = jax.nn.sigmoid(_dot(hn.astype(_BF16), wgate_ref[...]))
    mixed = (gates[:, 0:D_MODEL] * _dot(oa, wa_ref[...])
             + gates[:, D_MODEL:2 * D_MODEL] * _dot(ob, wb_ref[...])
             + gates[:, 2 * D_MODEL:3 * D_MODEL] * _dot(oc, wc_ref[...]))
    return x + mod(2) * _dot(mixed.astype(_BF16), wout_ref[...])


def _ffn(x, mod_ref, gn2_ref, wfg_ref, wfu_ref, wfd_ref):
    mod = lambda i: mod_ref[0, :, i * D_MODEL:(i + 1) * D_MODEL]
    hn = ((x * _rms_scale(x) * gn2_ref[...]) * (1.0 + mod(4)) + mod(3)).astype(_BF16)
    h = _silu(_dot(hn, wfg_ref[...])) * _dot(hn, wfu_ref[...])
    return x + mod(5) * _dot(h.astype(_BF16), wfd_ref[...])


def _mix_latent_kernel(T, layer_ref, *refs):
    params, refs = refs[:N_MIX_PARAMS], refs[N_MIX_PARAMS:]
    _mix_body(True, T, layer_ref[0], params, refs[:5], refs[5:8], refs[8:])


def _mix_ctx_kernel(T, layer_ref, *refs):
    params, refs = refs[:N_MIX_PARAMS], refs[N_MIX_PARAMS + 3:]
    _mix_body(False, T, layer_ref[0], params, None, refs[:6], refs[6:])


def _layer_spec(shape):
    zeros = (0,) * len(shape)
    return pl.BlockSpec((None,) + tuple(shape), lambda i, layer: (layer[0],) + zeros,
                        pipeline_mode=pl.Buffered(1))


def _const_spec(blk):
    return pl.BlockSpec(blk, lambda i, layer: (0,) * len(blk), pipeline_mode=pl.Buffered(1))


def _mix_params(x, x_spec, mod_spec, mod_all, pw):
    specs = [
        x_spec, mod_spec,
        _layer_spec((1, D_MODEL)),
        _layer_spec((D_MODEL, C_UC)),
        _layer_spec((D_MODEL, MIX_W - C_UC)),
        _layer_spec((1, ATT_Q)), _layer_spec((1, ATT_KV)),
        pl.BlockSpec(memory_space=pltpu.SMEM),
        _layer_spec((2, GL_PAD, 2 * GLA_QK)), _layer_spec((1, 2 * GLA_QK)), _layer_spec((1, GLA_VW)),
        _layer_spec((POOL_GROUPS, POOL_GROUP_DIM, POOL_GROUP_DIM)), _layer_spec((1, POOL_W)),
        _const_spec((ATT_Q, ATT_Q)), _const_spec((GLA_VW, GLA_VW)),
    ]
    args = [x, mod_all, pw["g_norm1"], pw["w_in"], pw["w_tail"], pw["g_qn"], pw["g_kn"], pw["att_sink"], pw["w_gate2"],
            pw["b_gate2"], pw["g_gla_out"], pw["w_pool"], pw["pool_scale"], pw["head_ones"], pw["gla_ones"]]
    assert len(specs) == len(args) == N_MIX_PARAMS
    return specs, args


def _mix_scratch(S, T, kv_rows):
    scratch = [
        pltpu.VMEM((S, T, MIX_W), _F32),
        pltpu.VMEM((S, T, ATT_Q), _BF16),
        pltpu.VMEM((S, ATT_KV_HEADS, 2, kv_rows, 2 * HEAD_DIM), _BF16),
        pltpu.VMEM((S, ATT_KV_HEADS, 2, kv_rows, 4 * HEAD_DIM), _BF16),
        pltpu.VMEM((S, T, 2 * GLA_QK), _F32),
        pltpu.VMEM((S, T, GLA_VW), _F32),
        pltpu.VMEM((S, 2, GLA_HEADS // 2, GLA_DV, 2 * GLA_DK), _F32),
        pltpu.VMEM((GLA_HEADS + 2, GLA_CHUNK, GLA_QK), _F32),
        pltpu.VMEM((S, T + 2 * _pool_halo(T), POOL_W), _BF16),
    ]
    assert len(scratch) == N_MIX_SCRATCH
    return scratch


def _mix_latent_call(layer, x, mod_all, pw, cache_k, cache_v, st0, cos, sin):
    B, T, _ = x.shape
    per_seq = lambda blk: pl.BlockSpec(blk, lambda b, layer: (b,) + (0,) * (len(blk) - 1))
    mod_spec = pl.BlockSpec((None, 1, 1, 6 * D_MODEL), lambda b, layer: (layer[0], b + 1, 0, 0))
    x_spec = pl.BlockSpec((1, T, D_MODEL), lambda b, layer: (b, 0, 0), pipeline_mode=pl.Buffered(1))
    in_specs, args = _mix_params(x, x_spec, mod_spec, mod_all, pw)
    P = cache_k.shape[2]
    cache_spec = pl.BlockSpec((1, 1, P, ATT_KV), lambda b, layer: (b, layer[0], 0, 0))
    in_specs += [cache_spec, cache_spec,
                 pl.BlockSpec((1, None, 2, GLA_HEADS // 2, GLA_DV, 2 * GLA_DK),
                              lambda b, layer: (b, layer[0], 0, 0, 0, 0)),
                 _const_spec((T, ATT_KV)), _const_spec((T, ATT_KV))]
    args += [cache_k, cache_v, st0, cos, sin]
    widths = (ATT_Q, GLA_VW, POOL_W)
    return pl.pallas_call(
        functools.partial(_mix_latent_kernel, T),
        grid_spec=pltpu.PrefetchScalarGridSpec(
            num_scalar_prefetch=1, grid=(B,), in_specs=in_specs,
            out_specs=[per_seq((1, T, w)) for w in widths],
            scratch_shapes=_mix_scratch(1, T, T + 2 * ATT_BLOCK)),
        out_shape=[jax.ShapeDtypeStruct((B, T, w), _BF16) for w in widths],
        compiler_params=pltpu.CompilerParams(dimension_semantics=("arbitrary",), vmem_limit_bytes=VMEM_LIMIT),
        name="mix_latent",
    )(layer, *args)


def _mix_ctx_call(layer, x, mod_all, pw, stacked):
    B, T, _ = x.shape
    S = CTX_SEQS_PER_STEP
    assert B % S == 0
    per_seq = lambda blk: pl.BlockSpec(blk, lambda b, layer: (b,) + (0,) * (len(blk) - 1))
    mod_spec = pl.BlockSpec((None, 1, 1, 6 * D_MODEL), lambda b, layer: (layer[0], 0, 0, 0))
    in_specs, args = _mix_params(x, per_seq((S, T, D_MODEL)), mod_spec, mod_all, pw)
    widths = (ATT_Q, GLA_VW, POOL_W)
    n_in = 1 + len(args)
    aliases = {n_in + j: len(widths) + j for j in range(len(stacked))}
    in_specs += [pl.BlockSpec(memory_space=pl.ANY)] * len(stacked)
    args += list(stacked)
    at_layer = lambda blk: pl.BlockSpec((S, None) + blk, lambda b, layer: (b, layer[0]) + (0,) * len(blk))
    out_specs = [per_seq((S, T, w)) for w in widths] + [
        at_layer((T, ATT_KV)), at_layer((T, ATT_KV)), at_layer((2, GLA_HEADS // 2, 2 * GLA_DK, GLA_DV))]
    out_shape = ([jax.ShapeDtypeStruct((B, T, w), _BF16) for w in widths]
                 + [jax.ShapeDtypeStruct(a.shape, a.dtype) for a in stacked])
    return pl.pallas_call(
        functools.partial(_mix_ctx_kernel, T),
        grid_spec=pltpu.PrefetchScalarGridSpec(
            num_scalar_prefetch=1, grid=(B // S,), in_specs=in_specs, out_specs=out_specs,
            scratch_shapes=_mix_scratch(S, T, T)),
        out_shape=out_shape,
        input_output_aliases=aliases,
        compiler_params=pltpu.CompilerParams(dimension_semantics=("arbitrary",), vmem_limit_bytes=VMEM_LIMIT),
        name="mix_ctx",
    )(layer, *args)


STREAMED = ("w_br_a", "w_br_b", "w_br_c", "w_out", "w_ff_gate", "w_ff_up", "w_ff_down")


def _post_kernel(n_cast, layer_ref, x_ref, mod_ref, oa_ref, ob_ref, oc_ref, gn1_ref, gn2_ref, wgate_ref, wa_ref,
                 wb_ref, wc_ref, wout_ref, wfg_ref, wfu_ref, wfd_ref, *rest):
    src_refs, out_ref, dst_refs = rest[:n_cast], rest[n_cast], rest[n_cast + 1:]
    x = _merge(x_ref[...], mod_ref, oa_ref[...], ob_ref[...], oc_ref[...], gn1_ref, wgate_ref, wa_ref, wb_ref,
               wc_ref, wout_ref)
    out_ref[...] = _ffn(x, mod_ref, gn2_ref, wfg_ref, wfu_ref, wfd_ref)
    for src_ref, dst_ref in zip(src_refs, dst_refs):
        dst_ref[...] = src_ref[...].astype(_BF16)


def _cast_kernel(*refs):
    n = len(refs) // 2
    for src_ref, dst_ref in zip(refs[:n], refs[n:]):
        dst_ref[...] = src_ref[...].astype(_BF16)


def _cast_first_layer(weights, n_steps):
    chunk = lambda w: pl.BlockSpec((None, w.shape[1] // n_steps, w.shape[2]), lambda i: (0, i, 0))
    return pl.pallas_call(
        _cast_kernel,
        grid=(n_steps,),
        in_specs=[chunk(w) for w in weights],
        out_specs=[chunk(w) for w in weights],
        out_shape=[jax.ShapeDtypeStruct((DEPTH + 1,) + w.shape[1:], _BF16) for w in weights],
        name="cast_first_layer",
    )(*weights)


def _post_call(layer, x2d, mod_all, oa, ob, oc, pw, tiles_per_seq, next_f32=None):
    row = lambda w: pl.BlockSpec((POST_TILE, w), lambda i, layer: (i, 0))
    n_steps = x2d.shape[0] // POST_TILE
    if tiles_per_seq is None:
        mod_spec = pl.BlockSpec((None, 1, 1, 6 * D_MODEL), lambda i, layer: (layer[0], 0, 0, 0))
    else:
        mod_spec = pl.BlockSpec((None, 1, 1, 6 * D_MODEL),
                                lambda i, layer: (layer[0], 1 + i // tiles_per_seq, 0, 0))
    weights = [(pw["g_norm1"], (1, D_MODEL)), (pw["g_norm2"], (1, D_MODEL)), (pw["w_gates"], (D_MODEL, GATE_W)),
               (pw["w_br_a"], (ATT_Q, D_MODEL)), (pw["w_br_b"], (GLA_VW, D_MODEL)),
               (pw["w_br_c"], (POOL_W, D_MODEL)), (pw["w_out"], (D_MODEL, D_MODEL)),
               (pw["w_ff_gate"], (D_MODEL, D_FF)), (pw["w_ff_up"], (D_MODEL, D_FF)),
               (pw["w_ff_down"], (D_FF, D_MODEL))]
    in_specs = ([row(D_MODEL), mod_spec, row(ATT_Q), row(GLA_VW), row(POOL_W)]
                + [_layer_spec(shape) for _, shape in weights])
    args = [x2d, mod_all, oa, ob, oc] + [a for a, _ in weights]
    out_specs, out_shape = [row(D_MODEL)], [jax.ShapeDtypeStruct(x2d.shape, _F32)]
    aliases = {1: 0}
    if next_f32 is not None:
        first = 1 + len(args) - len(STREAMED)
        assert [id(a) for a in args[-len(STREAMED):]] == [id(pw[name]) for name in STREAMED]
        for j, w in enumerate(next_f32):
            blk = (None, w.shape[1] // n_steps, w.shape[2])
            in_specs.append(pl.BlockSpec(blk, lambda i, layer: (jnp.minimum(layer[0] + 1, DEPTH - 1), i, 0)))
            out_specs.append(pl.BlockSpec(blk, lambda i, layer: (layer[0] + 1, i, 0)))
            out_shape.append(jax.ShapeDtypeStruct(pw[STREAMED[j]].shape, _BF16))
            aliases[first + j] = 1 + j
        args += list(next_f32)
    outs = pl.pallas_call(
        functools.partial(_post_kernel, 0 if next_f32 is None else len(next_f32)),
        grid_spec=pltpu.PrefetchScalarGridSpec(
            num_scalar_prefetch=1, grid=(n_steps,), in_specs=in_specs, out_specs=out_specs),
        out_shape=out_shape,
        input_output_aliases=aliases,
        compiler_params=pltpu.CompilerParams(dimension_semantics=("arbitrary",), vmem_limit_bytes=VMEM_LIMIT),
        name="post",
    )(layer, *args)
    return outs[0] if next_f32 is None else outs


def _rope_tables(T):
    quarter = HEAD_DIM // 4
    inv_freq = ROPE_BASE ** (-np.arange(quarter, dtype=np.float32) / quarter)
    pos = np.arange(T)
    ang_row = (pos // GRID_W).astype(np.float32)[:, None] * inv_freq[None, :]
    ang_col = (pos % GRID_W).astype(np.float32)[:, None] * inv_freq[None, :]
    cos = np.concatenate([np.cos(ang_row)] * 2 + [np.cos(ang_col)] * 2, axis=-1)
    sin = np.concatenate([-np.sin(ang_row), np.sin(ang_row), -np.sin(ang_col), np.sin(ang_col)], axis=-1)
    return (jnp.asarray(np.tile(cos, (1, ATT_KV_HEADS)), _F32), jnp.asarray(np.tile(sin, (1, ATT_KV_HEADS)), _F32))


def _prepare_weights(w_in, g_qn, g_kn, att_sink, w_gate2, b_gate2, g_gla_out, w_pool, pool_scale, g_norm1, g_norm2):
    o_gl = ATT_Q + 2 * ATT_KV + 2 * GLA_QK + 2 * GLA_VW
    o_uc = o_gl + 2 * GLA_RANK
    o_gate = o_uc + POOL_W
    assert o_gl == C_UC
    w_in = w_in.astype(_BF16)
    w_tail = jnp.concatenate(
        [w_in[:, :, o_uc:o_gate], w_in[:, :, o_gl:o_uc],
         jnp.zeros((DEPTH, D_MODEL, GL_PAD - 2 * GLA_RANK), w_in.dtype)], axis=2)
    w_gates = w_in[:, :, o_gate:]
    wg2 = jnp.zeros((DEPTH, GL_PAD, 2 * GLA_QK), _F32)
    wg2 = wg2.at[:, 0:GLA_RANK, 0:GLA_QK].set(w_gate2[:, 0])
    wg2 = wg2.at[:, GLA_RANK:2 * GLA_RANK, GLA_QK:].set(w_gate2[:, 1])
    wg2_hi = wg2.astype(_BF16)
    wg2 = jnp.stack([wg2_hi, (wg2 - wg2_hi.astype(_F32)).astype(_BF16)], axis=1)
    vec = lambda a: a.reshape(DEPTH, 1, -1)
    group_ones = lambda n, width: jnp.asarray(
        (np.arange(n)[:, None] // width) == (np.arange(n)[None, :] // width), _BF16)
    return {
        "head_ones": group_ones(ATT_Q, HEAD_DIM),
        "gla_ones": group_ones(GLA_VW, GLA_DV),
        "w_in": w_in,
        "w_tail": w_tail,
        "w_gates": w_gates,
        "g_qn": vec(jnp.tile(g_qn, (1, ATT_HEADS))),
        "g_kn": vec(jnp.tile(g_kn, (1, ATT_KV_HEADS))),
        "att_sink": att_sink,
        "w_gate2": wg2,
        "b_gate2": vec(b_gate2),
        "g_gla_out": vec(jnp.tile(g_gla_out, (1, GLA_HEADS))),
        "w_pool": w_pool.astype(_BF16),
        "pool_scale": vec(pool_scale),
        "g_norm1": vec(g_norm1),
        "g_norm2": vec(g_norm2),
    }


def kernel(x_prompt, x_sample, c, cache_k, cache_v, state_gla, c_ctx, w_in, g_qn, g_kn, att_sink, w_gate2,
           b_gate2, g_gla_out, w_pool, pool_scale, w_br_a, w_br_b, w_br_c, w_out, g_norm1, g_norm2, w_mod,
           b_mod, w_ff_gate, w_ff_up, w_ff_down):
    B, T, _ = x_prompt.shape
    BL, TL, _ = x_sample.shape
    assert (B * T) % POST_TILE == 0 and TL % POST_TILE == 0 and BL + 1 <= MOD_ROWS
    assert T % PROJ_TILE == 0 and TL % PROJ_TILE == 0
    cv = jnp.concatenate([c_ctx[None, :], c, jnp.zeros((MOD_ROWS - 1 - BL, D_MODEL), _F32)], axis=0)
    mod_all = _modulation(cv, w_mod, b_mod).reshape(DEPTH, MOD_ROWS, 1, 6 * D_MODEL)
    pw = _prepare_weights(w_in, g_qn, g_kn, att_sink, w_gate2, b_gate2, g_gla_out, w_pool, pool_scale, g_norm1,
                          g_norm2)
    streamed_f32 = (w_br_a, w_br_b, w_br_c, w_out, w_ff_gate, w_ff_up, w_ff_down)
    streamed = tuple(_cast_first_layer(streamed_f32, (B * T) // POST_TILE))
    cos, sin = _rope_tables(TL)
    P = cache_k.shape[2]
    latent_ctx = (cache_k.reshape(BL, DEPTH, P, ATT_KV), cache_v.reshape(BL, DEPTH, P, ATT_KV),
                  jnp.swapaxes(state_gla.reshape(BL, DEPTH, 2, GLA_HEADS // 2, 2 * GLA_DK, GLA_DV), -1, -2),
                  cos, sin)

    def layer_step(l, carry):
        yp, ys, new_k, new_v, new_st, streamed = carry
        layer = jnp.full((1,), l, jnp.int32)
        oa, ob, oc, new_k, new_v, new_st = _mix_ctx_call(layer, yp, mod_all, pw, (new_k, new_v, new_st))
        yp, *streamed = _post_call(layer, yp.reshape(B * T, D_MODEL), mod_all, oa.reshape(B * T, -1),
                                   ob.reshape(B * T, -1), oc.reshape(B * T, -1),
                                   dict(pw, **dict(zip(STREAMED, streamed))), None, next_f32=streamed_f32)
        oa, ob, oc = _mix_latent_call(layer, ys, mod_all, pw, *latent_ctx)
        ys = _post_call(layer, ys.reshape(BL * TL, D_MODEL), mod_all, oa.reshape(BL * TL, -1),
                        ob.reshape(BL * TL, -1), oc.reshape(BL * TL, -1),
                        dict(pw, **dict(zip(STREAMED, streamed))), TL // POST_TILE)
        return yp.reshape(B, T, D_MODEL), ys.reshape(BL, TL, D_MODEL), new_k, new_v, new_st, tuple(streamed)

    init = (x_prompt, x_sample,
            jnp.zeros((B, DEPTH, T, ATT_KV), _F32), jnp.zeros((B, DEPTH, T, ATT_KV), _F32),
            jnp.zeros((B, DEPTH, 2, GLA_HEADS // 2, 2 * GLA_DK, GLA_DV), _F32), streamed)
    yp, ys, new_k, new_v, new_st, _ = lax.fori_loop(0, DEPTH, layer_step, init)
    return (yp, ys, new_k.reshape(B, DEPTH, T, ATT_KV_HEADS, HEAD_DIM),
            new_v.reshape(B, DEPTH, T, ATT_KV_HEADS, HEAD_DIM),
            new_st.reshape(B, DEPTH, 2, GLA_HEADS, GLA_DK, GLA_DV))
```

```python
import functools

import jax
import jax.numpy as jnp
import numpy as np
from jax import lax
from jax.experimental import pallas as pl
from jax.experimental.pallas import tpu as pltpu

D_MODEL = 1024
DEPTH = 4
GRID_W = 64
ATT_HEADS = 8
ATT_KV_HEADS = 2
ATT_GROUP = ATT_HEADS // ATT_KV_HEADS
HEAD_DIM = 64
WINDOW = 128
ATT_BLOCK = 128
ROPE_BASE = 10000.0
GLA_HEADS = 4
GLA_DK = 64
GLA_DV = 128
GLA_RANK = 16
GLA_TAU = 16.0
GLA_CHUNK = 64
POOL_GROUPS = 4
POOL_GROUP_DIM = 128
POOL_WINDOWS = (2, 4, 8, 16)
D_FF = 2816
ATT_Q = ATT_HEADS * HEAD_DIM
ATT_KV = ATT_KV_HEADS * HEAD_DIM
GLA_QK = GLA_HEADS * GLA_DK
GLA_VW = GLA_HEADS * GLA_DV
POOL_W = POOL_GROUPS * POOL_GROUP_DIM
EPS = 1e-6
NEG = -1e30

C_QA = 0
C_KA = C_QA + ATT_Q
C_VA = C_KA + ATT_KV
C_QB = C_VA + ATT_KV
C_KB = C_QB + GLA_QK
C_VB = C_KB + GLA_QK
C_RB = C_VB + GLA_VW
C_UC = C_RB + GLA_VW
C_GL = C_UC + POOL_W
GL_PAD = 128
MIX_W = C_GL + GL_PAD
GATE_W = 3 * D_MODEL

POST_TILE = 512
PROJ_TILE = 256
CTX_SEQS_PER_STEP = 2
POOL_TILE = 256
POOL_HALO = 128
MOD_ROWS = 8
MOD_TILE = 3072
GLA_BLOCK = 256
GLA_MAX_EXPONENT = 80.0
VMEM_LIMIT = 56 * 1024 * 1024

_F32 = jnp.float32
_BF16 = jnp.bfloat16


def _dot(a, b):
    return jnp.dot(a, b, preferred_element_type=_F32)


def _dot_nt(a, b):
    return lax.dot_general(a, b, (((1,), (1,)), ((), ())), preferred_element_type=_F32)


def _dot_tn(a, b):
    return lax.dot_general(a, b, (((0,), (0,)), ((), ())), preferred_element_type=_F32)


def _rms_scale(x):
    return lax.rsqrt(jnp.mean(x * x, axis=-1, keepdims=True) + EPS)


def _group_rms_scale(x, group_ones, width):
    return lax.rsqrt(_dot((x * x).astype(_BF16), group_ones) * (1.0 / width) + EPS)


def _log_sigmoid(x):
    return jnp.minimum(x, 0.0) - jnp.log1p(jnp.exp(-jnp.abs(x)))


def _silu(x):
    return x * jax.nn.sigmoid(x)


def _rope(x, cos, sin_signed):
    n = x.shape[-1]
    lane = lax.broadcasted_iota(jnp.int32, x.shape, 1)
    up = pltpu.roll(x, n - HEAD_DIM // 4, axis=1)
    down = pltpu.roll(x, HEAD_DIM // 4, axis=1)
    partner = jnp.where((lane & (HEAD_DIM // 2 - 1)) < HEAD_DIM // 4, up, down)
    return x * cos + partner * sin_signed


def _mod_kernel(cv_ref, w_ref, b_ref, out_ref):
    s = _silu(cv_ref[...]).astype(_BF16)
    out_ref[0] = _dot(s, w_ref[0].astype(_BF16)) + b_ref[0]


def _modulation(cv, w_mod, b_mod):
    n_col = (6 * D_MODEL) // MOD_TILE
    return pl.pallas_call(
        _mod_kernel,
        grid=(DEPTH, n_col),
        in_specs=[
            pl.BlockSpec((MOD_ROWS, D_MODEL), lambda l, j: (0, 0)),
            pl.BlockSpec((1, D_MODEL, MOD_TILE), lambda l, j: (l, 0, j)),
            pl.BlockSpec((1, 1, MOD_TILE), lambda l, j: (l, 0, j)),
        ],
        out_specs=pl.BlockSpec((1, MOD_ROWS, MOD_TILE), lambda l, j: (l, 0, j)),
        out_shape=jax.ShapeDtypeStruct((DEPTH, MOD_ROWS, 6 * D_MODEL), _F32),
        compiler_params=pltpu.CompilerParams(vmem_limit_bytes=VMEM_LIMIT),
        name="modulation",
    )(cv, w_mod, b_mod.reshape(DEPTH, 1, 6 * D_MODEL))


def _split_heads(x):
    low = lax.broadcasted_iota(jnp.int32, x.shape, 1) < HEAD_DIM
    swapped = pltpu.roll(x, HEAD_DIM, axis=1)
    zero = jnp.zeros_like(x)
    return ((jnp.where(low, x, zero), jnp.where(low, zero, swapped)),
            (jnp.where(low, swapped, zero), jnp.where(low, zero, x)))


def _store_split_kv(k, v, ks_ref, vs_ref, rows):
    ones = jnp.ones_like(v)
    for kv, (k_sides, v_sides, one_sides) in enumerate(zip(_split_heads(k), _split_heads(v), _split_heads(ones))):
        for side in range(2):
            ks_ref[kv, side, rows, :] = k_sides[side].astype(_BF16)
            vs_ref[kv, side, rows, :] = jnp.concatenate([v_sides[side], one_sides[side]], axis=-1).astype(_BF16)


def _pair_softmax_av(qp, keys, values, masks, sink_even, sink_odd):
    m = qp.shape[0]
    scores = []
    for (k_left, k_right), mask in zip(keys, masks):
        s_even, s_odd = _dot_nt(qp, k_left), _dot_nt(qp, k_right)
        if mask is not None:
            s_even, s_odd = jnp.where(mask, s_even, NEG), jnp.where(mask, s_odd, NEG)
        scores.append((s_even, s_odd))
    m_even = jnp.full((m, 1), sink_even, _F32)
    m_odd = jnp.full((m, 1), sink_odd, _F32)
    for s_even, s_odd in scores:
        m_even = jnp.maximum(m_even, jnp.max(s_even, axis=-1, keepdims=True))
        m_odd = jnp.maximum(m_odd, jnp.max(s_odd, axis=-1, keepdims=True))
    res = None
    for (s_even, s_odd), (w_left, w_right) in zip(scores, values):
        r = (_dot(jnp.exp(s_even - m_even).astype(_BF16), w_left)
             + _dot(jnp.exp(s_odd - m_odd).astype(_BF16), w_right))
        res = r if res is None else res + r
    pair = 2 * HEAD_DIM
    low = lax.broadcasted_iota(jnp.int32, (m, pair), 1) < HEAD_DIM
    den = res[:, pair:] + jnp.where(low, jnp.exp(sink_even - m_even), jnp.exp(sink_odd - m_odd))
    return res[:, :pair] / den


def _attention_ctx(T, layer, qr_ref, ks_ref, vs_ref, sink_ref, oa_ref):
    pair = 2 * HEAD_DIM
    for kv in range(ATT_KV_HEADS):
        keys = [(ks_ref[kv, 0], ks_ref[kv, 1])]
        values = [(vs_ref[kv, 0], vs_ref[kv, 1])]
        for j in range(ATT_GROUP // 2):
            head = kv * ATT_GROUP + 2 * j
            cols = slice(head * HEAD_DIM, head * HEAD_DIM + pair)
            o = _pair_softmax_av(qr_ref[:, cols], keys, values, [None], sink_ref[layer, head],
                                 sink_ref[layer, head + 1])
            oa_ref[0, :, cols] = o.astype(_BF16)


def _attention_latent(T, layer, qr_ref, ks_ref, vs_ref, kc_ref, vc_ref, sink_ref, oa_ref):
    pair = 2 * HEAD_DIM
    span = 3 * ATT_BLOCK
    kc, vc = kc_ref[0, 0], vc_ref[0, 0]
    ones = jnp.ones_like(vc)
    ctx_keys = [tuple(side.astype(_BF16) for side in sides) for sides in _split_heads(kc)]
    ctx_values = [tuple(jnp.concatenate([v_side, one_side], axis=-1).astype(_BF16)
                        for v_side, one_side in zip(v_sides, one_sides))
                  for v_sides, one_sides in zip(_split_heads(vc), _split_heads(ones))]

    def block(i, carry):
        q_rows = pl.ds(pl.multiple_of(i * ATT_BLOCK, ATT_BLOCK), ATT_BLOCK)
        k_rows = pl.ds(pl.multiple_of(i * ATT_BLOCK, ATT_BLOCK), span)
        q_pos = i * ATT_BLOCK + lax.broadcasted_iota(jnp.int32, (ATT_BLOCK, span), 0)
        k_pos = (i - 1) * ATT_BLOCK + lax.broadcasted_iota(jnp.int32, (ATT_BLOCK, span), 1)
        valid = (jnp.abs(k_pos - q_pos) <= WINDOW) & (k_pos >= 0) & (k_pos < T)
        for kv in range(ATT_KV_HEADS):
            keys = [(ks_ref[kv, 0, k_rows, :], ks_ref[kv, 1, k_rows, :]), ctx_keys[kv]]
            values = [(vs_ref[kv, 0, k_rows, :], vs_ref[kv, 1, k_rows, :]), ctx_values[kv]]
            for j in range(ATT_GROUP // 2):
                head = kv * ATT_GROUP + 2 * j
                cols = slice(head * HEAD_DIM, head * HEAD_DIM + pair)
                o = _pair_softmax_av(qr_ref[q_rows, cols], keys, values, [valid, None], sink_ref[layer, head],
                                     sink_ref[layer, head + 1])
                oa_ref[0, q_rows, cols] = o.astype(_BF16)
        return carry

    lax.fori_loop(0, T // ATT_BLOCK, block, 0, unroll=2)


def _gla_chunk(z_ref, la_ref, o_ref, st_ref, a_ref, direction, start):
    C = GLA_CHUNK
    rows = pl.ds(pl.multiple_of(start, C), C)
    q = z_ref[rows, C_QB:C_QB + GLA_QK] * (GLA_DK ** -0.5)
    k = z_ref[rows, C_KB:C_KB + GLA_QK]
    v = z_ref[rows, C_VB:C_VB + GLA_VW].astype(_BF16)
    la = la_ref[rows, direction * GLA_QK:(direction + 1) * GLA_QK]
    la_hi = la.astype(_BF16)
    la_lo = (la - la_hi.astype(_F32)).astype(_BF16)
    ti = lax.broadcasted_iota(jnp.int32, (C, C), 0)
    si = lax.broadcasted_iota(jnp.int32, (C, C), 1)
    causal = (si <= ti) if direction == 0 else (si >= ti)
    tri = jnp.where(causal, 1.0, 0.0).astype(_BF16)
    b = _dot(tri, la_hi) + _dot(tri, la_lo)
    end = C - 1 if direction == 0 else 0
    b_end = b[end:end + 1]
    q_in = (q * jnp.exp(b)).astype(_BF16)
    k_out = (k * jnp.exp(b_end - b)).astype(_BF16)
    e_end = jnp.exp(b_end)
    ones = jnp.ones((8, GLA_DK), _BF16)

    def row_group(g, carry):
        base = pl.multiple_of(g * 8, 8)
        b8 = a_ref[GLA_HEADS, pl.ds(base, 8), :]
        q8 = a_ref[GLA_HEADS + 1, pl.ds(base, 8), :]
        s_idx = lax.broadcasted_iota(jnp.int32, (C, 1), 0)
        rows_h = [[] for _ in range(GLA_HEADS)]
        for j in range(8):
            ok = (s_idx <= base + j) if direction == 0 else (s_idx >= base + j)
            decay = jnp.exp(jnp.where(ok, b8[j:j + 1] - b, NEG))
            p = (q8[j:j + 1] * k * decay).astype(_BF16)
            for h in range(GLA_HEADS):
                rows_h[h].append(_dot_nt(ones, p[:, h * GLA_DK:(h + 1) * GLA_DK])[0:1])
        for h in range(GLA_HEADS):
            a_ref[h, pl.ds(base, 8), 0:C] = jnp.concatenate(rows_h[h], axis=0)
        return carry

    a_ref[GLA_HEADS] = b
    a_ref[GLA_HEADS + 1] = q
    lax.fori_loop(0, C // 8, row_group, 0)
    outs = []
    for pair in range(GLA_HEADS // 2):
        s_pair = st_ref[direction, pair]
        s_next = []
        for h in (2 * pair, 2 * pair + 1):
            kc = slice(h * GLA_DK, (h + 1) * GLA_DK)
            vc = slice(h * GLA_DV, (h + 1) * GLA_DV)
            s_t = s_pair[:, (h % 2) * GLA_DK:(h % 2 + 1) * GLA_DK]
            outs.append(_dot_nt(q_in[:, kc], s_t.astype(_BF16)) + _dot(a_ref[h, :, 0:C].astype(_BF16), v[:, vc]))
            s_next.append(s_t * e_end[:, kc] + _dot_tn(v[:, vc], k_out[:, kc]))
        st_ref[direction, pair] = jnp.concatenate(s_next, axis=-1)
    o_ref[rows, :] += jnp.concatenate(outs, axis=-1)


def _gla_block(z_ref, la_ref, o_ref, st_ref, direction, start, use_state):
    NB = GLA_BLOCK
    fwd = direction == 0
    rows = slice(start, start + NB) if isinstance(start, int) else pl.ds(pl.multiple_of(start, NB), NB)
    q = z_ref[rows, C_QB:C_QB + GLA_QK] * (GLA_DK ** -0.5)
    k = z_ref[rows, C_KB:C_KB + GLA_QK]
    v = z_ref[rows, C_VB:C_VB + GLA_VW].astype(_BF16)
    b = la_ref[rows, direction * GLA_QK:(direction + 1) * GLA_QK]
    end, mid = (NB - 1, NB // 2 - 1) if fwd else (0, NB // 2)
    b_end = b[end:end + 1]
    c = b - b[mid:mid + 1]
    q_c = (q * jnp.exp(c)).astype(_BF16)
    k_c = (k * jnp.exp(-c)).astype(_BF16)
    k_fin = (k * jnp.exp(b_end - b)).astype(_BF16)
    if use_state:
        q_in = (q * jnp.exp(b)).astype(_BF16)
        e_all = jnp.exp(b_end)
    ti = lax.broadcasted_iota(jnp.int32, (NB, NB), 0)
    si = lax.broadcasted_iota(jnp.int32, (NB, NB), 1)
    causal = (si <= ti) if fwd else (si >= ti)
    tile = 2 * GLA_DK
    low = lax.broadcasted_iota(jnp.int32, (NB, tile), 1) < GLA_DK
    zero = jnp.zeros((NB, tile), _BF16)
    pick = lambda x, parity: jnp.where(low, x, zero) if parity == 0 else jnp.where(low, zero, x)
    outs = []
    for pair in range(GLA_HEADS // 2):
        lanes = slice(pair * tile, (pair + 1) * tile)
        if use_state:
            s_pair = st_ref[direction, pair]
            s_bf = s_pair.astype(_BF16)
        s_new = None
        for parity in range(2):
            h = 2 * pair + parity
            vc = slice(h * GLA_DV, (h + 1) * GLA_DV)
            a = jnp.where(causal, _dot_nt(q_c[:, lanes], pick(k_c[:, lanes], parity)), 0.0)
            o_h = _dot(a.astype(_BF16), v[:, vc])
            if use_state:
                o_h = o_h + _dot_nt(pick(q_in[:, lanes], parity), s_bf)
            outs.append(o_h)
            upd = _dot_tn(v[:, vc], pick(k_fin[:, lanes], parity))
            s_new = upd if s_new is None else s_new + upd
        if use_state:
            s_new = s_new + s_pair * e_all[:, lanes]
        st_ref[direction, pair] = s_new
    o_ref[rows, :] += jnp.concatenate(outs, axis=-1)


def _log_decay(z_ref, rows, wg2_ref, bg2_ref):
    x = z_ref[rows, C_GL:C_GL + GL_PAD]
    x_hi = x.astype(_BF16)
    x_lo = (x - x_hi.astype(_F32)).astype(_BF16)
    w_hi, w_lo = wg2_ref[0], wg2_ref[1]
    pre = _dot(x_hi, w_hi) + (_dot(x_lo, w_hi) + _dot(x_hi, w_lo)) + bg2_ref[...]
    return _log_sigmoid(pre) * (1.0 / GLA_TAU)


def _gla_prepare(T, z_ref, la_ref, o_ref, wg2_ref, bg2_ref):
    NB = GLA_BLOCK
    ti = lax.broadcasted_iota(jnp.int32, (NB, NB), 0)
    si = lax.broadcasted_iota(jnp.int32, (NB, NB), 1)
    tri = [jnp.where(si <= ti, 1.0, 0.0).astype(_BF16), jnp.where(si >= ti, 1.0, 0.0).astype(_BF16)]
    worst = None
    for r0 in range(0, T, NB):
        rows = slice(r0, r0 + NB)
        la = _log_decay(z_ref, rows, wg2_ref, bg2_ref)
        la_hi = la.astype(_BF16)
        la_lo = (la - la_hi.astype(_F32)).astype(_BF16)
        for d in range(2):
            cols = slice(d * GLA_QK, (d + 1) * GLA_QK)
            b = _dot(tri[d], la_hi[:, cols]) + _dot(tri[d], la_lo[:, cols])
            la_ref[rows, cols] = b
            first, mid, last = (0, NB // 2 - 1, NB - 1) if d == 0 else (NB - 1, NB // 2, 0)
            span = jnp.max(jnp.maximum(b[first:first + 1] - b[mid:mid + 1], b[mid:mid + 1] - b[last:last + 1]))
            worst = span if worst is None else jnp.maximum(worst, span)
    o_ref[...] = jnp.zeros(o_ref.shape, _F32)
    return worst


def _gla_run(T, seqs, worst, a_ref, wg2_ref, bg2_ref, use_state):
    NB = GLA_BLOCK
    n_chunks = T // GLA_CHUNK
    n_blocks = T // NB
    fast_ok = worst <= GLA_MAX_EXPONENT

    @pl.when(fast_ok)
    def _():
        if n_blocks == 1:
            for direction in range(2):
                for z_ref, la_ref, o_ref, st_ref in seqs:
                    _gla_block(z_ref, la_ref, o_ref, st_ref, direction, 0, use_state)
        else:
            def body(i, carry):
                for z_ref, la_ref, o_ref, st_ref in seqs:
                    _gla_block(z_ref, la_ref, o_ref, st_ref, 0, i * NB, True)
                    _gla_block(z_ref, la_ref, o_ref, st_ref, 1, (n_blocks - 1 - i) * NB, True)
                return carry
            lax.fori_loop(0, n_blocks, body, 0, unroll=2)

    @pl.when(jnp.logical_not(fast_ok))
    def _():
        for z_ref, la_ref, o_ref, st_ref in seqs:
            for r0 in range(0, T, NB):
                la_ref[r0:r0 + NB, :] = _log_decay(z_ref, slice(r0, r0 + NB), wg2_ref, bg2_ref)

            def body(i, carry):
                _gla_chunk(z_ref, la_ref, o_ref, st_ref, a_ref, 0, i * GLA_CHUNK)
                _gla_chunk(z_ref, la_ref, o_ref, st_ref, a_ref, 1, (n_chunks - 1 - i) * GLA_CHUNK)
                return carry
            lax.fori_loop(0, n_chunks, body, 0)


def _gla_finish(T, z_ref, o_ref, ggla_ref, vones_ref, ob_ref):
    for r0 in range(0, T, PROJ_TILE):
        rows = slice(r0, r0 + PROJ_TILE)
        o = o_ref[rows, :]
        y = o * _group_rms_scale(o, vones_ref[...], GLA_DV) * ggla_ref[...]
        ob_ref[0, rows, :] = (y * _silu(z_ref[rows, C_RB:C_RB + GLA_VW])).astype(_BF16)


def _pool_halo(T):
    return POOL_HALO if T > POOL_TILE else 0


def _pool(T, z_ref, upad_ref, wpool_ref, pscale_ref, oc_ref):
    halo = _pool_halo(T)
    if halo:
        zeros = jnp.zeros((halo, POOL_W), _BF16)
        upad_ref[0:halo, :] = zeros
        upad_ref[halo + T:halo + T + halo, :] = zeros
    upad_ref[halo:halo + T, :] = z_ref[:, C_UC:C_UC + POOL_W].astype(_BF16)
    span = POOL_TILE + 2 * halo
    r = lax.broadcasted_iota(jnp.int32, (POOL_TILE, span), 0)
    c = lax.broadcasted_iota(jnp.int32, (POOL_TILE, span), 1)
    off = c - halo - r
    for jb in range(T // POOL_TILE):
        t = jb * POOL_TILE + lax.broadcasted_iota(jnp.int32, (POOL_TILE, 1), 0)
        parts = []
        for g, w in enumerate(POOL_WINDOWS):
            cols = slice(g * POOL_GROUP_DIM, (g + 1) * POOL_GROUP_DIM)
            band = jnp.where((off >= -(w // 2)) & (off < w - w // 2), 1.0, 0.0).astype(_BF16)
            win = slice(jb * POOL_TILE, jb * POOL_TILE + span)
            total = _dot(band, upad_ref[win, cols])
            cnt = (jnp.minimum(t - w // 2 + w, T) - jnp.maximum(t - w // 2, 0)).astype(_F32)
            pooled = total / cnt - z_ref[jb * POOL_TILE:(jb + 1) * POOL_TILE, C_UC + g * POOL_GROUP_DIM:
                                         C_UC + (g + 1) * POOL_GROUP_DIM]
            parts.append(_dot(pooled.astype(_BF16), wpool_ref[g]))
        y = jnp.concatenate(parts, axis=-1) * pscale_ref[...]
        oc_ref[0, jb * POOL_TILE:(jb + 1) * POOL_TILE, :] = y.astype(_BF16)


N_MIX_PARAMS = 15
N_MIX_SCRATCH = 9


def _mix_body(latent, T, layer, params, latent_refs, out_refs, scratch):
    (x_ref, mod_ref, gn1_ref, wmain_ref, wtail_ref, gqn_ref, gkn_ref, sink_ref, wg2_ref, bg2_ref, ggla_ref,
     wpool_ref, pscale_ref, hones_ref, vones_ref) = params
    n_seq = x_ref.shape[0]
    one = lambda ref, s: ref.at[pl.ds(s, 1)]
    a_ref = scratch[N_MIX_SCRATCH - 2]
    per_seq = [tuple(ref.at[s] for ref in scratch[:N_MIX_SCRATCH - 2] + scratch[N_MIX_SCRATCH - 1:])
               for s in range(n_seq)]
    pad = ATT_BLOCK if latent else 0
    shift = mod_ref[0, :, 0:D_MODEL]
    scale = mod_ref[0, :, D_MODEL:2 * D_MODEL]

    for s, (z_ref, qr_ref, ks_ref, vs_ref, la_ref, o_ref, st_ref, upad_ref) in enumerate(per_seq):
        if latent:
            for ref in (ks_ref, vs_ref):
                zeros = jnp.zeros(ref.shape[:2] + (pad, ref.shape[3]), _BF16)
                ref[:, :, 0:pad, :] = zeros
                ref[:, :, pad + T:pad + T + pad, :] = zeros
        for r0 in range(0, T, PROJ_TILE):
            rows = slice(r0, r0 + PROJ_TILE)
            x = x_ref[s, rows, :]
            hn = (x * _rms_scale(x) * gn1_ref[...]) * (1.0 + scale) + shift
            hn = hn.astype(_BF16)
            z_ref[rows, 0:C_UC] = _dot(hn, wmain_ref[...])
            z_ref[rows, C_UC:MIX_W] = _dot(hn, wtail_ref[...])
            q = z_ref[rows, C_QA:C_QA + ATT_Q]
            k = z_ref[rows, C_KA:C_KA + ATT_KV]
            q = q * _group_rms_scale(q, hones_ref[...], HEAD_DIM) * gqn_ref[...]
            k = k * _group_rms_scale(k, hones_ref[0:ATT_KV, 0:ATT_KV], HEAD_DIM) * gkn_ref[...]
            v = z_ref[rows, C_VA:C_VA + ATT_KV]
            if latent:
                cos_ref, sin_ref = latent_refs[3], latent_refs[4]
                cos = jnp.concatenate([cos_ref[rows, :]] * (ATT_Q // ATT_KV), axis=-1)
                sin = jnp.concatenate([sin_ref[rows, :]] * (ATT_Q // ATT_KV), axis=-1)
                q = _rope(q, cos, sin)
                k = _rope(k, cos_ref[rows, :], sin_ref[rows, :])
            else:
                out_refs[3][s, rows, :] = k
                out_refs[4][s, rows, :] = v
            qr_ref[rows, :] = (q * (HEAD_DIM ** -0.5)).astype(_BF16)
            _store_split_kv(k, v, ks_ref, vs_ref, slice(pad + r0, pad + r0 + PROJ_TILE))

    for s, (z_ref, qr_ref, ks_ref, vs_ref, la_ref, o_ref, st_ref, upad_ref) in enumerate(per_seq):
        if latent:
            _attention_latent(T, layer, qr_ref, ks_ref, vs_ref, one(latent_refs[0], s), one(latent_refs[1], s),
                              sink_ref, one(out_refs[0], s))
        else:
            _attention_ctx(T, layer, qr_ref, ks_ref, vs_ref, sink_ref, one(out_refs[0], s))

    worst = None
    for s, (z_ref, qr_ref, ks_ref, vs_ref, la_ref, o_ref, st_ref, upad_ref) in enumerate(per_seq):
        if latent:
            st_ref[...] = latent_refs[2][s]
        else:
            st_ref[...] = jnp.zeros(st_ref.shape, _F32)
        span = _gla_prepare(T, z_ref, la_ref, o_ref, wg2_ref, bg2_ref)
        worst = span if worst is None else jnp.maximum(worst, span)
    _gla_run(T, [(z_ref, la_ref, o_ref, st_ref) for z_ref, _, _, _, la_ref, o_ref, st_ref, _ in per_seq],
             worst, a_ref, wg2_ref, bg2_ref, latent)

    for s, (z_ref, qr_ref, ks_ref, vs_ref, la_ref, o_ref, st_ref, upad_ref) in enumerate(per_seq):
        _gla_finish(T, z_ref, o_ref, ggla_ref, vones_ref, one(out_refs[1], s))
        if not latent:
            for d in range(2):
                for pair in range(GLA_HEADS // 2):
                    out_refs[5][s, d, pair] = st_ref[d, pair].T
        _pool(T, z_ref, upad_ref, wpool_ref, pscale_ref, one(out_refs[2], s))


def _merge(x, mod_ref, oa, ob, oc, gn1_ref, wgate_ref, wa_ref, wb_ref, wc_ref, wout_ref):
    mod = lambda i: mod_ref[0, :, i * D_MODEL:(i + 1) * D_MODEL]
    hn = (x * _rms_scale(x) * gn1_ref[...]) * (1.0 + mod(1)) + mod(0)
    gates = jax.nn.sigmoid(_dot(hn.astype(_BF16), wgate_ref[...]))
    mixed = (gates[:, 0:D_MODEL] * _dot(oa, wa_ref[...])
             + gates[:, D_MODEL:2 * D_MODEL] * _dot(ob, wb_ref[...])
             + gates[:, 2 * D_MODEL:3 * D_MODEL] * _dot(oc, wc_ref[...]))
    return x + mod(2) * _dot(mixed.astype(_BF16), wout_ref[...])


def _ffn(x, mod_ref, gn2_ref, wfg_ref, wfu_ref, wfd_ref):
    mod = lambda i: mod_ref[0, :, i * D_MODEL:(i + 1) * D_MODEL]
    hn = ((x * _rms_scale(x) * gn2_ref[...]) * (1.0 + mod(4)) + mod(3)).astype(_BF16)
    h = _silu(_dot(hn, wfg_ref[...])) * _dot(hn, wfu_ref[...])
    return x + mod(5) * _dot(h.astype(_BF16), wfd_ref[...])


def _mix_latent_kernel(T, layer_ref, *refs):
    params, refs = refs[:N_MIX_PARAMS], refs[N_MIX_PARAMS:]
    _mix_body(True, T, layer_ref[0], params, refs[:5], refs[5:8], refs[8:])


def _mix_ctx_kernel(T, layer_ref, *refs):
    params, refs = refs[:N_MIX_PARAMS], refs[N_MIX_PARAMS + 3:]
    _mix_body(False, T, layer_ref[0], params, None, refs[:6], refs[6:])


def _layer_spec(shape):
    zeros = (0,) * len(shape)
    return pl.BlockSpec((None,) + tuple(shape), lambda i, layer: (layer[0],) + zeros,
                        pipeline_mode=pl.Buffered(1))


def _const_spec(blk):
    return pl.BlockSpec(blk, lambda i, layer: (0,) * len(blk), pipeline_mode=pl.Buffered(1))


def _mix_params(x, x_spec, mod_spec, mod_all, pw):
    specs = [
        x_spec, mod_spec,
        _layer_spec((1, D_MODEL)),
        _layer_spec((D_MODEL, C_UC)),
        _layer_spec((D_MODEL, MIX_W - C_UC)),
        _layer_spec((1, ATT_Q)), _layer_spec((1, ATT_KV)),
        pl.BlockSpec(memory_space=pltpu.SMEM),
        _layer_spec((2, GL_PAD, 2 * GLA_QK)), _layer_spec((1, 2 * GLA_QK)), _layer_spec((1, GLA_VW)),
        _layer_spec((POOL_GROUPS, POOL_GROUP_DIM, POOL_GROUP_DIM)), _layer_spec((1, POOL_W)),
        _const_spec((ATT_Q, ATT_Q)), _const_spec((GLA_VW, GLA_VW)),
    ]
    args = [x, mod_all, pw["g_norm1"], pw["w_main"], pw["w_tail"], pw["g_qn"], pw["g_kn"], pw["att_sink"], pw["w_gate2"],
            pw["b_gate2"], pw["g_gla_out"], pw["w_pool"], pw["pool_scale"], pw["head_ones"], pw["gla_ones"]]
    assert len(specs) == len(args) == N_MIX_PARAMS
    return specs, args


def _mix_scratch(S, T, kv_rows):
    scratch = [
        pltpu.VMEM((S, T, MIX_W), _F32),
        pltpu.VMEM((S, T, ATT_Q), _BF16),
        pltpu.VMEM((S, ATT_KV_HEADS, 2, kv_rows, 2 * HEAD_DIM), _BF16),
        pltpu.VMEM((S, ATT_KV_HEADS, 2, kv_rows, 4 * HEAD_DIM), _BF16),
        pltpu.VMEM((S, T, 2 * GLA_QK), _F32),
        pltpu.VMEM((S, T, GLA_VW), _F32),
        pltpu.VMEM((S, 2, GLA_HEADS // 2, GLA_DV, 2 * GLA_DK), _F32),
        pltpu.VMEM((GLA_HEADS + 2, GLA_CHUNK, GLA_QK), _F32),
        pltpu.VMEM((S, T + 2 * _pool_halo(T), POOL_W), _BF16),
    ]
    assert len(scratch) == N_MIX_SCRATCH
    return scratch


def _mix_latent_call(layer, x, mod_all, pw, cache_k, cache_v, st0, cos, sin):
    B, T, _ = x.shape
    per_seq = lambda blk: pl.BlockSpec(blk, lambda b, layer: (b,) + (0,) * (len(blk) - 1))
    mod_spec = pl.BlockSpec((None, 1, 1, 6 * D_MODEL), lambda b, layer: (layer[0], b + 1, 0, 0))
    in_specs, args = _mix_params(x, per_seq((1, T, D_MODEL)), mod_spec, mod_all, pw)
    P = cache_k.shape[2]
    cache_spec = pl.BlockSpec((1, 1, P, ATT_KV), lambda b, layer: (b, layer[0], 0, 0))
    in_specs += [cache_spec, cache_spec,
                 pl.BlockSpec((1, None, 2, GLA_HEADS // 2, GLA_DV, 2 * GLA_DK),
                              lambda b, layer: (b, layer[0], 0, 0, 0, 0)),
                 _const_spec((T, ATT_KV)), _const_spec((T, ATT_KV))]
    args += [cache_k, cache_v, st0, cos, sin]
    widths = (ATT_Q, GLA_VW, POOL_W)
    return pl.pallas_call(
        functools.partial(_mix_latent_kernel, T),
        grid_spec=pltpu.PrefetchScalarGridSpec(
            num_scalar_prefetch=1, grid=(B,), in_specs=in_specs,
            out_specs=[per_seq((1, T, w)) for w in widths],
            scratch_shapes=_mix_scratch(1, T, T + 2 * ATT_BLOCK)),
        out_shape=[jax.ShapeDtypeStruct((B, T, w), _BF16) for w in widths],
        compiler_params=pltpu.CompilerParams(dimension_semantics=("arbitrary",), vmem_limit_bytes=VMEM_LIMIT),
        name="mix_latent",
    )(layer, *args)


def _mix_ctx_call(layer, x, mod_all, pw, stacked):
    B, T, _ = x.shape
    S = CTX_SEQS_PER_STEP
    assert B % S == 0
    per_seq = lambda blk: pl.BlockSpec(blk, lambda b, layer: (b,) + (0,) * (len(blk) - 1))
    mod_spec = pl.BlockSpec((None, 1, 1, 6 * D_MODEL), lambda b, layer: (layer[0], 0, 0, 0))
    in_specs, args = _mix_params(x, per_seq((S, T, D_MODEL)), mod_spec, mod_all, pw)
    widths = (ATT_Q, GLA_VW, POOL_W)
    n_in = 1 + len(args)
    aliases = {n_in + j: len(widths) + j for j in range(len(stacked))}
    in_specs += [pl.BlockSpec(memory_space=pl.ANY)] * len(stacked)
    args += list(stacked)
    at_layer = lambda blk: pl.BlockSpec((S, None) + blk, lambda b, layer: (b, layer[0]) + (0,) * len(blk))
    out_specs = [per_seq((S, T, w)) for w in widths] + [
        at_layer((T, ATT_KV)), at_layer((T, ATT_KV)), at_layer((2, GLA_HEADS // 2, 2 * GLA_DK, GLA_DV))]
    out_shape = ([jax.ShapeDtypeStruct((B, T, w), _BF16) for w in widths]
                 + [jax.ShapeDtypeStruct(a.shape, a.dtype) for a in stacked])
    return pl.pallas_call(
        functools.partial(_mix_ctx_kernel, T),
        grid_spec=pltpu.PrefetchScalarGridSpec(
            num_scalar_prefetch=1, grid=(B // S,), in_specs=in_specs, out_specs=out_specs,
            scratch_shapes=_mix_scratch(S, T, T)),
        out_shape=out_shape,
        input_output_aliases=aliases,
        compiler_params=pltpu.CompilerParams(dimension_semantics=("arbitrary",), vmem_limit_bytes=VMEM_LIMIT),
        name="mix_ctx",
    )(layer, *args)


def _post_kernel(layer_ref, x_ref, mod_ref, oa_ref, ob_ref, oc_ref, gn1_ref, gn2_ref, wgate_ref, wa_ref, wb_ref,
                 wc_ref, wout_ref, wfg_ref, wfu_ref, wfd_ref, out_ref):
    x = _merge(x_ref[...], mod_ref, oa_ref[...], ob_ref[...], oc_ref[...], gn1_ref, wgate_ref, wa_ref, wb_ref,
               wc_ref, wout_ref)
    out_ref[...] = _ffn(x, mod_ref, gn2_ref, wfg_ref, wfu_ref, wfd_ref)


def _post_call(layer, x2d, mod_all, oa, ob, oc, pw, tiles_per_seq):
    row = lambda w: pl.BlockSpec((POST_TILE, w), lambda i, layer: (i, 0))
    if tiles_per_seq is None:
        mod_spec = pl.BlockSpec((None, 1, 1, 6 * D_MODEL), lambda i, layer: (layer[0], 0, 0, 0))
    else:
        mod_spec = pl.BlockSpec((None, 1, 1, 6 * D_MODEL),
                                lambda i, layer: (layer[0], 1 + i // tiles_per_seq, 0, 0))
    weights = [(pw["g_norm1"], (1, D_MODEL)), (pw["g_norm2"], (1, D_MODEL)), (pw["w_gates"], (D_MODEL, GATE_W)),
               (pw["w_br_a"], (ATT_Q, D_MODEL)), (pw["w_br_b"], (GLA_VW, D_MODEL)),
               (pw["w_br_c"], (POOL_W, D_MODEL)), (pw["w_out"], (D_MODEL, D_MODEL)),
               (pw["w_ff_gate"], (D_MODEL, D_FF)), (pw["w_ff_up"], (D_MODEL, D_FF)),
               (pw["w_ff_down"], (D_FF, D_MODEL))]
    in_specs = ([row(D_MODEL), mod_spec, row(ATT_Q), row(GLA_VW), row(POOL_W)]
                + [_layer_spec(shape) for _, shape in weights])
    return pl.pallas_call(
        _post_kernel,
        grid_spec=pltpu.PrefetchScalarGridSpec(
            num_scalar_prefetch=1, grid=(x2d.shape[0] // POST_TILE,), in_specs=in_specs, out_specs=row(D_MODEL)),
        out_shape=jax.ShapeDtypeStruct(x2d.shape, _F32),
        input_output_aliases={1: 0},
        compiler_params=pltpu.CompilerParams(dimension_semantics=("arbitrary",), vmem_limit_bytes=VMEM_LIMIT),
        name="post",
    )(layer, x2d, mod_all, oa, ob, oc, *[a for a, _ in weights])


def _rope_tables(T):
    quarter = HEAD_DIM // 4
    inv_freq = ROPE_BASE ** (-np.arange(quarter, dtype=np.float32) / quarter)
    pos = np.arange(T)
    ang_row = (pos // GRID_W).astype(np.float32)[:, None] * inv_freq[None, :]
    ang_col = (pos % GRID_W).astype(np.float32)[:, None] * inv_freq[None, :]
    cos = np.concatenate([np.cos(ang_row)] * 2 + [np.cos(ang_col)] * 2, axis=-1)
    sin = np.concatenate([-np.sin(ang_row), np.sin(ang_row), -np.sin(ang_col), np.sin(ang_col)], axis=-1)
    return (jnp.asarray(np.tile(cos, (1, ATT_KV_HEADS)), _F32), jnp.asarray(np.tile(sin, (1, ATT_KV_HEADS)), _F32))


def _prepare_weights(w_in, g_qn, g_kn, att_sink, w_gate2, b_gate2, g_gla_out, w_pool, pool_scale, w_br_a,
                     w_br_b, w_br_c, w_out, g_norm1, g_norm2, w_ff_gate, w_ff_up, w_ff_down):
    o_gl = ATT_Q + 2 * ATT_KV + 2 * GLA_QK + 2 * GLA_VW
    o_uc = o_gl + 2 * GLA_RANK
    o_gate = o_uc + POOL_W
    assert o_gl == C_UC
    w_main = w_in[:, :, :o_gl].astype(_BF16)
    w_tail = jnp.concatenate(
        [w_in[:, :, o_uc:o_gate], w_in[:, :, o_gl:o_uc],
         jnp.zeros((DEPTH, D_MODEL, GL_PAD - 2 * GLA_RANK), w_in.dtype)], axis=2).astype(_BF16)
    w_gates = w_in[:, :, o_gate:].astype(_BF16)
    wg2 = jnp.zeros((DEPTH, GL_PAD, 2 * GLA_QK), _F32)
    wg2 = wg2.at[:, 0:GLA_RANK, 0:GLA_QK].set(w_gate2[:, 0])
    wg2 = wg2.at[:, GLA_RANK:2 * GLA_RANK, GLA_QK:].set(w_gate2[:, 1])
    wg2_hi = wg2.astype(_BF16)
    wg2 = jnp.stack([wg2_hi, (wg2 - wg2_hi.astype(_F32)).astype(_BF16)], axis=1)
    vec = lambda a: a.reshape(DEPTH, 1, -1)
    group_ones = lambda n, width: jnp.asarray(
        (np.arange(n)[:, None] // width) == (np.arange(n)[None, :] // width), _BF16)
    return {
        "head_ones": group_ones(ATT_Q, HEAD_DIM),
        "gla_ones": group_ones(GLA_VW, GLA_DV),
        "w_main": w_main,
        "w_tail": w_tail,
        "w_gates": w_gates,
        "g_qn": vec(jnp.tile(g_qn, (1, ATT_HEADS))),
        "g_kn": vec(jnp.tile(g_kn, (1, ATT_KV_HEADS))),
        "att_sink": att_sink,
        "w_gate2": wg2,
        "b_gate2": vec(b_gate2),
        "g_gla_out": vec(jnp.tile(g_gla_out, (1, GLA_HEADS))),
        "w_pool": w_pool.astype(_BF16),
        "pool_scale": vec(pool_scale),
        "w_br_a": w_br_a.astype(_BF16),
        "w_br_b": w_br_b.astype(_BF16),
        "w_br_c": w_br_c.astype(_BF16),
        "w_out": w_out.astype(_BF16),
        "g_norm1": vec(g_norm1),
        "g_norm2": vec(g_norm2),
        "w_ff_gate": w_ff_gate.astype(_BF16),
        "w_ff_up": w_ff_up.astype(_BF16),
        "w_ff_down": w_ff_down.astype(_BF16),
    }


def kernel(x_prompt, x_sample, c, cache_k, cache_v, state_gla, c_ctx, w_in, g_qn, g_kn, att_sink, w_gate2,
           b_gate2, g_gla_out, w_pool, pool_scale, w_br_a, w_br_b, w_br_c, w_out, g_norm1, g_norm2, w_mod,
           b_mod, w_ff_gate, w_ff_up, w_ff_down):
    B, T, _ = x_prompt.shape
    BL, TL, _ = x_sample.shape
    assert (B * T) % POST_TILE == 0 and TL % POST_TILE == 0 and BL + 1 <= MOD_ROWS
    assert T % PROJ_TILE == 0 and TL % PROJ_TILE == 0
    cv = jnp.concatenate([c_ctx[None, :], c, jnp.zeros((MOD_ROWS - 1 - BL, D_MODEL), _F32)], axis=0)
    mod_all = _modulation(cv, w_mod, b_mod).reshape(DEPTH, MOD_ROWS, 1, 6 * D_MODEL)
    pw = _prepare_weights(w_in, g_qn, g_kn, att_sink, w_gate2, b_gate2, g_gla_out, w_pool, pool_scale, w_br_a,
                          w_br_b, w_br_c, w_out, g_norm1, g_norm2, w_ff_gate, w_ff_up, w_ff_down)
    cos, sin = _rope_tables(TL)
    P = cache_k.shape[2]
    latent_ctx = (cache_k.reshape(BL, DEPTH, P, ATT_KV), cache_v.reshape(BL, DEPTH, P, ATT_KV),
                  jnp.swapaxes(state_gla.reshape(BL, DEPTH, 2, GLA_HEADS // 2, 2 * GLA_DK, GLA_DV), -1, -2),
                  cos, sin)

    def layer_step(l, carry):
        yp, ys, new_k, new_v, new_st = carry
        layer = jnp.full((1,), l, jnp.int32)
        oa, ob, oc, new_k, new_v, new_st = _mix_ctx_call(layer, yp, mod_all, pw, (new_k, new_v, new_st))
        yp = _post_call(layer, yp.reshape(B * T, D_MODEL), mod_all, oa.reshape(B * T, -1), ob.reshape(B * T, -1),
                        oc.reshape(B * T, -1), pw, None).reshape(B, T, D_MODEL)
        oa, ob, oc = _mix_latent_call(layer, ys, mod_all, pw, *latent_ctx)
        ys = _post_call(layer, ys.reshape(BL * TL, D_MODEL), mod_all, oa.reshape(BL * TL, -1),
                        ob.reshape(BL * TL, -1), oc.reshape(BL * TL, -1), pw,
                        TL // POST_TILE).reshape(BL, TL, D_MODEL)
        return yp, ys, new_k, new_v, new_st

    init = (x_prompt, x_sample,
            jnp.zeros((B, DEPTH, T, ATT_KV), _F32), jnp.zeros((B, DEPTH, T, ATT_KV), _F32),
            jnp.zeros((B, DEPTH, 2, GLA_HEADS // 2, 2 * GLA_DK, GLA_DV), _F32))
    yp, ys, new_k, new_v, new_st = lax.fori_loop(0, DEPTH, layer_step, init)
    return (yp, ys, new_k.reshape(B, DEPTH, T, ATT_KV_HEADS, HEAD_DIM),
            new_v.reshape(B, DEPTH, T, ATT_KV_HEADS, HEAD_DIM),
            new_st.reshape(B, DEPTH, 2, GLA_HEADS, GLA_DK, GLA_DV))
```

```python
import functools

import jax
import jax.numpy as jnp
import numpy as np
from jax import lax
from jax.experimental import pallas as pl
from jax.experimental.pallas import tpu as pltpu

D_MODEL = 1024
DEPTH = 4
GRID_W = 64
ATT_HEADS = 8
ATT_KV_HEADS = 2
ATT_GROUP = ATT_HEADS // ATT_KV_HEADS
HEAD_DIM = 64
WINDOW = 128
ATT_BLOCK = 128
ROPE_BASE = 10000.0
GLA_HEADS = 4
GLA_DK = 64
GLA_DV = 128
GLA_RANK = 16
GLA_TAU = 16.0
GLA_CHUNK = 64
POOL_GROUPS = 4
POOL_GROUP_DIM = 128
POOL_WINDOWS = (2, 4, 8, 16)
D_FF = 2816
ATT_Q = ATT_HEADS * HEAD_DIM
ATT_KV = ATT_KV_HEADS * HEAD_DIM
GLA_QK = GLA_HEADS * GLA_DK
GLA_VW = GLA_HEADS * GLA_DV
POOL_W = POOL_GROUPS * POOL_GROUP_DIM
EPS = 1e-6
NEG = -1e30

C_QA = 0
C_KA = C_QA + ATT_Q
C_VA = C_KA + ATT_KV
C_QB = C_VA + ATT_KV
C_KB = C_QB + GLA_QK
C_VB = C_KB + GLA_QK
C_RB = C_VB + GLA_VW
C_UC = C_RB + GLA_VW
C_GL = C_UC + POOL_W
GL_PAD = 128
MIX_W = C_GL + GL_PAD
GATE_W = 3 * D_MODEL

POST_TILE = 512
PROJ_TILE = 256
CTX_SEQS_PER_STEP = 2
POOL_TILE = 256
POOL_HALO = 128
MOD_ROWS = 8
MOD_TILE = 3072
GLA_BLOCK = 256
GLA_MAX_EXPONENT = 80.0
VMEM_LIMIT = 56 * 1024 * 1024

_F32 = jnp.float32
_BF16 = jnp.bfloat16


def _dot(a, b):
    return jnp.dot(a, b, preferred_element_type=_F32)


def _dot_nt(a, b):
    return lax.dot_general(a, b, (((1,), (1,)), ((), ())), preferred_element_type=_F32)


def _dot_tn(a, b):
    return lax.dot_general(a, b, (((0,), (0,)), ((), ())), preferred_element_type=_F32)


def _rms_scale(x):
    return lax.rsqrt(jnp.mean(x * x, axis=-1, keepdims=True) + EPS)


def _group_rms_scale(x, group_ones, width):
    return lax.rsqrt(_dot((x * x).astype(_BF16), group_ones) * (1.0 / width) + EPS)


def _log_sigmoid(x):
    return jnp.minimum(x, 0.0) - jnp.log1p(jnp.exp(-jnp.abs(x)))


def _silu(x):
    return x * jax.nn.sigmoid(x)


def _rope(x, cos, sin_signed):
    n = x.shape[-1]
    lane = lax.broadcasted_iota(jnp.int32, x.shape, 1)
    up = pltpu.roll(x, n - HEAD_DIM // 4, axis=1)
    down = pltpu.roll(x, HEAD_DIM // 4, axis=1)
    partner = jnp.where((lane & (HEAD_DIM // 2 - 1)) < HEAD_DIM // 4, up, down)
    return x * cos + partner * sin_signed


def _mod_kernel(cv_ref, w_ref, b_ref, out_ref):
    s = _silu(cv_ref[...]).astype(_BF16)
    out_ref[0] = _dot(s, w_ref[0].astype(_BF16)) + b_ref[0]


def _modulation(cv, w_mod, b_mod):
    n_col = (6 * D_MODEL) // MOD_TILE
    return pl.pallas_call(
        _mod_kernel,
        grid=(DEPTH, n_col),
        in_specs=[
            pl.BlockSpec((MOD_ROWS, D_MODEL), lambda l, j: (0, 0)),
            pl.BlockSpec((1, D_MODEL, MOD_TILE), lambda l, j: (l, 0, j)),
            pl.BlockSpec((1, 1, MOD_TILE), lambda l, j: (l, 0, j)),
        ],
        out_specs=pl.BlockSpec((1, MOD_ROWS, MOD_TILE), lambda l, j: (l, 0, j)),
        out_shape=jax.ShapeDtypeStruct((DEPTH, MOD_ROWS, 6 * D_MODEL), _F32),
        compiler_params=pltpu.CompilerParams(vmem_limit_bytes=VMEM_LIMIT),
        name="modulation",
    )(cv, w_mod, b_mod.reshape(DEPTH, 1, 6 * D_MODEL))


def _split_heads(x):
    low = lax.broadcasted_iota(jnp.int32, x.shape, 1) < HEAD_DIM
    swapped = pltpu.roll(x, HEAD_DIM, axis=1)
    zero = jnp.zeros_like(x)
    return ((jnp.where(low, x, zero), jnp.where(low, zero, swapped)),
            (jnp.where(low, swapped, zero), jnp.where(low, zero, x)))


def _store_split_kv(k, v, ks_ref, vs_ref, rows):
    ones = jnp.ones_like(v)
    for kv, (k_sides, v_sides, one_sides) in enumerate(zip(_split_heads(k), _split_heads(v), _split_heads(ones))):
        for side in range(2):
            ks_ref[kv, side, rows, :] = k_sides[side].astype(_BF16)
            vs_ref[kv, side, rows, :] = jnp.concatenate([v_sides[side], one_sides[side]], axis=-1).astype(_BF16)


def _pair_softmax_av(qp, keys, values, masks, sink_even, sink_odd):
    m = qp.shape[0]
    scores = []
    for (k_left, k_right), mask in zip(keys, masks):
        s_even, s_odd = _dot_nt(qp, k_left), _dot_nt(qp, k_right)
        if mask is not None:
            s_even, s_odd = jnp.where(mask, s_even, NEG), jnp.where(mask, s_odd, NEG)
        scores.append((s_even, s_odd))
    m_even = jnp.full((m, 1), sink_even, _F32)
    m_odd = jnp.full((m, 1), sink_odd, _F32)
    for s_even, s_odd in scores:
        m_even = jnp.maximum(m_even, jnp.max(s_even, axis=-1, keepdims=True))
        m_odd = jnp.maximum(m_odd, jnp.max(s_odd, axis=-1, keepdims=True))
    res = None
    for (s_even, s_odd), (w_left, w_right) in zip(scores, values):
        r = (_dot(jnp.exp(s_even - m_even).astype(_BF16), w_left)
             + _dot(jnp.exp(s_odd - m_odd).astype(_BF16), w_right))
        res = r if res is None else res + r
    pair = 2 * HEAD_DIM
    low = lax.broadcasted_iota(jnp.int32, (m, pair), 1) < HEAD_DIM
    den = res[:, pair:] + jnp.where(low, jnp.exp(sink_even - m_even), jnp.exp(sink_odd - m_odd))
    return res[:, :pair] / den


def _attention_ctx(T, layer, qr_ref, ks_ref, vs_ref, sink_ref, oa_ref):
    pair = 2 * HEAD_DIM
    for kv in range(ATT_KV_HEADS):
        keys = [(ks_ref[kv, 0], ks_ref[kv, 1])]
        values = [(vs_ref[kv, 0], vs_ref[kv, 1])]
        for j in range(ATT_GROUP // 2):
            head = kv * ATT_GROUP + 2 * j
            cols = slice(head * HEAD_DIM, head * HEAD_DIM + pair)
            o = _pair_softmax_av(qr_ref[:, cols], keys, values, [None], sink_ref[layer, head],
                                 sink_ref[layer, head + 1])
            oa_ref[0, :, cols] = o.astype(_BF16)


def _attention_latent(T, layer, qr_ref, ks_ref, vs_ref, kc_ref, vc_ref, sink_ref, oa_ref):
    pair = 2 * HEAD_DIM
    span = 3 * ATT_BLOCK
    kc, vc = kc_ref[0, 0], vc_ref[0, 0]
    ones = jnp.ones_like(vc)
    ctx_keys = [tuple(side.astype(_BF16) for side in sides) for sides in _split_heads(kc)]
    ctx_values = [tuple(jnp.concatenate([v_side, one_side], axis=-1).astype(_BF16)
                        for v_side, one_side in zip(v_sides, one_sides))
                  for v_sides, one_sides in zip(_split_heads(vc), _split_heads(ones))]

    def block(i, carry):
        q_rows = pl.ds(pl.multiple_of(i * ATT_BLOCK, ATT_BLOCK), ATT_BLOCK)
        k_rows = pl.ds(pl.multiple_of(i * ATT_BLOCK, ATT_BLOCK), span)
        q_pos = i * ATT_BLOCK + lax.broadcasted_iota(jnp.int32, (ATT_BLOCK, span), 0)
        k_pos = (i - 1) * ATT_BLOCK + lax.broadcasted_iota(jnp.int32, (ATT_BLOCK, span), 1)
        valid = (jnp.abs(k_pos - q_pos) <= WINDOW) & (k_pos >= 0) & (k_pos < T)
        for kv in range(ATT_KV_HEADS):
            keys = [(ks_ref[kv, 0, k_rows, :], ks_ref[kv, 1, k_rows, :]), ctx_keys[kv]]
            values = [(vs_ref[kv, 0, k_rows, :], vs_ref[kv, 1, k_rows, :]), ctx_values[kv]]
            for j in range(ATT_GROUP // 2):
                head = kv * ATT_GROUP + 2 * j
                cols = slice(head * HEAD_DIM, head * HEAD_DIM + pair)
                o = _pair_softmax_av(qr_ref[q_rows, cols], keys, values, [valid, None], sink_ref[layer, head],
                                     sink_ref[layer, head + 1])
                oa_ref[0, q_rows, cols] = o.astype(_BF16)
        return carry

    lax.fori_loop(0, T // ATT_BLOCK, block, 0, unroll=2)


def _gla_chunk(z_ref, la_ref, o_ref, st_ref, a_ref, direction, start):
    C = GLA_CHUNK
    rows = pl.ds(pl.multiple_of(start, C), C)
    q = z_ref[rows, C_QB:C_QB + GLA_QK] * (GLA_DK ** -0.5)
    k = z_ref[rows, C_KB:C_KB + GLA_QK]
    v = z_ref[rows, C_VB:C_VB + GLA_VW].astype(_BF16)
    la = la_ref[rows, direction * GLA_QK:(direction + 1) * GLA_QK]
    la_hi = la.astype(_BF16)
    la_lo = (la - la_hi.astype(_F32)).astype(_BF16)
    ti = lax.broadcasted_iota(jnp.int32, (C, C), 0)
    si = lax.broadcasted_iota(jnp.int32, (C, C), 1)
    causal = (si <= ti) if direction == 0 else (si >= ti)
    tri = jnp.where(causal, 1.0, 0.0).astype(_BF16)
    b = _dot(tri, la_hi) + _dot(tri, la_lo)
    end = C - 1 if direction == 0 else 0
    b_end = b[end:end + 1]
    q_in = (q * jnp.exp(b)).astype(_BF16)
    k_out = (k * jnp.exp(b_end - b)).astype(_BF16)
    e_end = jnp.exp(b_end)
    ones = jnp.ones((8, GLA_DK), _BF16)

    def row_group(g, carry):
        base = pl.multiple_of(g * 8, 8)
        b8 = a_ref[GLA_HEADS, pl.ds(base, 8), :]
        q8 = a_ref[GLA_HEADS + 1, pl.ds(base, 8), :]
        s_idx = lax.broadcasted_iota(jnp.int32, (C, 1), 0)
        rows_h = [[] for _ in range(GLA_HEADS)]
        for j in range(8):
            ok = (s_idx <= base + j) if direction == 0 else (s_idx >= base + j)
            decay = jnp.exp(jnp.where(ok, b8[j:j + 1] - b, NEG))
            p = (q8[j:j + 1] * k * decay).astype(_BF16)
            for h in range(GLA_HEADS):
                rows_h[h].append(_dot_nt(ones, p[:, h * GLA_DK:(h + 1) * GLA_DK])[0:1])
        for h in range(GLA_HEADS):
            a_ref[h, pl.ds(base, 8), 0:C] = jnp.concatenate(rows_h[h], axis=0)
        return carry

    a_ref[GLA_HEADS] = b
    a_ref[GLA_HEADS + 1] = q
    lax.fori_loop(0, C // 8, row_group, 0)
    outs = []
    for pair in range(GLA_HEADS // 2):
        s_pair = st_ref[direction, pair]
        s_next = []
        for h in (2 * pair, 2 * pair + 1):
            kc = slice(h * GLA_DK, (h + 1) * GLA_DK)
            vc = slice(h * GLA_DV, (h + 1) * GLA_DV)
            s_t = s_pair[:, (h % 2) * GLA_DK:(h % 2 + 1) * GLA_DK]
            outs.append(_dot_nt(q_in[:, kc], s_t.astype(_BF16)) + _dot(a_ref[h, :, 0:C].astype(_BF16), v[:, vc]))
            s_next.append(s_t * e_end[:, kc] + _dot_tn(v[:, vc], k_out[:, kc]))
        st_ref[direction, pair] = jnp.concatenate(s_next, axis=-1)
    o_ref[rows, :] += jnp.concatenate(outs, axis=-1)


def _gla_block(z_ref, la_ref, o_ref, st_ref, direction, start, use_state):
    NB = GLA_BLOCK
    fwd = direction == 0
    rows = slice(start, start + NB) if isinstance(start, int) else pl.ds(pl.multiple_of(start, NB), NB)
    q = z_ref[rows, C_QB:C_QB + GLA_QK] * (GLA_DK ** -0.5)
    k = z_ref[rows, C_KB:C_KB + GLA_QK]
    v = z_ref[rows, C_VB:C_VB + GLA_VW].astype(_BF16)
    b = la_ref[rows, direction * GLA_QK:(direction + 1) * GLA_QK]
    end, mid = (NB - 1, NB // 2 - 1) if fwd else (0, NB // 2)
    b_end = b[end:end + 1]
    c = b - b[mid:mid + 1]
    q_c = (q * jnp.exp(c)).astype(_BF16)
    k_c = (k * jnp.exp(-c)).astype(_BF16)
    k_fin = (k * jnp.exp(b_end - b)).astype(_BF16)
    if use_state:
        q_in = (q * jnp.exp(b)).astype(_BF16)
        e_all = jnp.exp(b_end)
    ti = lax.broadcasted_iota(jnp.int32, (NB, NB), 0)
    si = lax.broadcasted_iota(jnp.int32, (NB, NB), 1)
    causal = (si <= ti) if fwd else (si >= ti)
    tile = 2 * GLA_DK
    low = lax.broadcasted_iota(jnp.int32, (NB, tile), 1) < GLA_DK
    zero = jnp.zeros((NB, tile), _BF16)
    pick = lambda x, parity: jnp.where(low, x, zero) if parity == 0 else jnp.where(low, zero, x)
    outs = []
    for pair in range(GLA_HEADS // 2):
        lanes = slice(pair * tile, (pair + 1) * tile)
        if use_state:
            s_pair = st_ref[direction, pair]
            s_bf = s_pair.astype(_BF16)
        s_new = None
        for parity in range(2):
            h = 2 * pair + parity
            vc = slice(h * GLA_DV, (h + 1) * GLA_DV)
            a = jnp.where(causal, _dot_nt(q_c[:, lanes], pick(k_c[:, lanes], parity)), 0.0)
            o_h = _dot(a.astype(_BF16), v[:, vc])
            if use_state:
                o_h = o_h + _dot_nt(pick(q_in[:, lanes], parity), s_bf)
            outs.append(o_h)
            upd = _dot_tn(v[:, vc], pick(k_fin[:, lanes], parity))
            s_new = upd if s_new is None else s_new + upd
        if use_state:
            s_new = s_new + s_pair * e_all[:, lanes]
        st_ref[direction, pair] = s_new
    o_ref[rows, :] += jnp.concatenate(outs, axis=-1)


def _log_decay(z_ref, rows, wg2_ref, bg2_ref):
    x = z_ref[rows, C_GL:C_GL + GL_PAD]
    x_hi = x.astype(_BF16)
    x_lo = (x - x_hi.astype(_F32)).astype(_BF16)
    w_hi, w_lo = wg2_ref[0], wg2_ref[1]
    pre = _dot(x_hi, w_hi) + (_dot(x_lo, w_hi) + _dot(x_hi, w_lo)) + bg2_ref[...]
    return _log_sigmoid(pre) * (1.0 / GLA_TAU)


def _gla_prepare(T, z_ref, la_ref, o_ref, wg2_ref, bg2_ref):
    NB = GLA_BLOCK
    ti = lax.broadcasted_iota(jnp.int32, (NB, NB), 0)
    si = lax.broadcasted_iota(jnp.int32, (NB, NB), 1)
    tri = [jnp.where(si <= ti, 1.0, 0.0).astype(_BF16), jnp.where(si >= ti, 1.0, 0.0).astype(_BF16)]
    worst = None
    for r0 in range(0, T, NB):
        rows = slice(r0, r0 + NB)
        la = _log_decay(z_ref, rows, wg2_ref, bg2_ref)
        la_hi = la.astype(_BF16)
        la_lo = (la - la_hi.astype(_F32)).astype(_BF16)
        for d in range(2):
            cols = slice(d * GLA_QK, (d + 1) * GLA_QK)
            b = _dot(tri[d], la_hi[:, cols]) + _dot(tri[d], la_lo[:, cols])
            la_ref[rows, cols] = b
            first, mid, last = (0, NB // 2 - 1, NB - 1) if d == 0 else (NB - 1, NB // 2, 0)
            span = jnp.max(jnp.maximum(b[first:first + 1] - b[mid:mid + 1], b[mid:mid + 1] - b[last:last + 1]))
            worst = span if worst is None else jnp.maximum(worst, span)
    o_ref[...] = jnp.zeros(o_ref.shape, _F32)
    return worst


def _gla_run(T, seqs, worst, a_ref, wg2_ref, bg2_ref, use_state):
    NB = GLA_BLOCK
    n_chunks = T // GLA_CHUNK
    n_blocks = T // NB
    fast_ok = worst <= GLA_MAX_EXPONENT

    @pl.when(fast_ok)
    def _():
        if n_blocks == 1:
            for direction in range(2):
                for z_ref, la_ref, o_ref, st_ref in seqs:
                    _gla_block(z_ref, la_ref, o_ref, st_ref, direction, 0, use_state)
        else:
            def body(i, carry):
                for z_ref, la_ref, o_ref, st_ref in seqs:
                    _gla_block(z_ref, la_ref, o_ref, st_ref, 0, i * NB, True)
                    _gla_block(z_ref, la_ref, o_ref, st_ref, 1, (n_blocks - 1 - i) * NB, True)
                return carry
            lax.fori_loop(0, n_blocks, body, 0, unroll=2)

    @pl.when(jnp.logical_not(fast_ok))
    def _():
        for z_ref, la_ref, o_ref, st_ref in seqs:
            for r0 in range(0, T, NB):
                la_ref[r0:r0 + NB, :] = _log_decay(z_ref, slice(r0, r0 + NB), wg2_ref, bg2_ref)

            def body(i, carry):
                _gla_chunk(z_ref, la_ref, o_ref, st_ref, a_ref, 0, i * GLA_CHUNK)
                _gla_chunk(z_ref, la_ref, o_ref, st_ref, a_ref, 1, (n_chunks - 1 - i) * GLA_CHUNK)
                return carry
            lax.fori_loop(0, n_chunks, body, 0)


def _gla_finish(T, z_ref, o_ref, ggla_ref, vones_ref, ob_ref):
    for r0 in range(0, T, PROJ_TILE):
        rows = slice(r0, r0 + PROJ_TILE)
        o = o_ref[rows, :]
        y = o * _group_rms_scale(o, vones_ref[...], GLA_DV) * ggla_ref[...]
        ob_ref[0, rows, :] = (y * _silu(z_ref[rows, C_RB:C_RB + GLA_VW])).astype(_BF16)


def _pool_halo(T):
    return POOL_HALO if T > POOL_TILE else 0


def _pool(T, z_ref, upad_ref, wpool_ref, pscale_ref, oc_ref):
    halo = _pool_halo(T)
    if halo:
        zeros = jnp.zeros((halo, POOL_W), _BF16)
        upad_ref[0:halo, :] = zeros
        upad_ref[halo + T:halo + T + halo, :] = zeros
    upad_ref[halo:halo + T, :] = z_ref[:, C_UC:C_UC + POOL_W].astype(_BF16)
    span = POOL_TILE + 2 * halo
    r = lax.broadcasted_iota(jnp.int32, (POOL_TILE, span), 0)
    c = lax.broadcasted_iota(jnp.int32, (POOL_TILE, span), 1)
    off = c - halo - r
    for jb in range(T // POOL_TILE):
        t = jb * POOL_TILE + lax.broadcasted_iota(jnp.int32, (POOL_TILE, 1), 0)
        parts = []
        for g, w in enumerate(POOL_WINDOWS):
            cols = slice(g * POOL_GROUP_DIM, (g + 1) * POOL_GROUP_DIM)
            band = jnp.where((off >= -(w // 2)) & (off < w - w // 2), 1.0, 0.0).astype(_BF16)
            win = slice(jb * POOL_TILE, jb * POOL_TILE + span)
            total = _dot(band, upad_ref[win, cols])
            cnt = (jnp.minimum(t - w // 2 + w, T) - jnp.maximum(t - w // 2, 0)).astype(_F32)
            pooled = total / cnt - z_ref[jb * POOL_TILE:(jb + 1) * POOL_TILE, C_UC + g * POOL_GROUP_DIM:
                                         C_UC + (g + 1) * POOL_GROUP_DIM]
            parts.append(_dot(pooled.astype(_BF16), wpool_ref[g]))
        y = jnp.concatenate(parts, axis=-1) * pscale_ref[...]
        oc_ref[0, jb * POOL_TILE:(jb + 1) * POOL_TILE, :] = y.astype(_BF16)


N_MIX_PARAMS = 15
N_MIX_SCRATCH = 9


def _mix_body(latent, T, layer, params, latent_refs, out_refs, scratch):
    (x_ref, mod_ref, gn1_ref, wmain_ref, wtail_ref, gqn_ref, gkn_ref, sink_ref, wg2_ref, bg2_ref, ggla_ref,
     wpool_ref, pscale_ref, hones_ref, vones_ref) = params
    n_seq = x_ref.shape[0]
    one = lambda ref, s: ref.at[pl.ds(s, 1)]
    a_ref = scratch[N_MIX_SCRATCH - 2]
    per_seq = [tuple(ref.at[s] for ref in scratch[:N_MIX_SCRATCH - 2] + scratch[N_MIX_SCRATCH - 1:])
               for s in range(n_seq)]
    pad = ATT_BLOCK if latent else 0
    shift = mod_ref[0, :, 0:D_MODEL]
    scale = mod_ref[0, :, D_MODEL:2 * D_MODEL]

    for s, (z_ref, qr_ref, ks_ref, vs_ref, la_ref, o_ref, st_ref, upad_ref) in enumerate(per_seq):
        if latent:
            for ref in (ks_ref, vs_ref):
                zeros = jnp.zeros(ref.shape[:2] + (pad, ref.shape[3]), _BF16)
                ref[:, :, 0:pad, :] = zeros
                ref[:, :, pad + T:pad + T + pad, :] = zeros
        for r0 in range(0, T, PROJ_TILE):
            rows = slice(r0, r0 + PROJ_TILE)
            x = x_ref[s, rows, :]
            hn = (x * _rms_scale(x) * gn1_ref[...]) * (1.0 + scale) + shift
            hn = hn.astype(_BF16)
            z_ref[rows, 0:C_UC] = _dot(hn, wmain_ref[...])
            z_ref[rows, C_UC:MIX_W] = _dot(hn, wtail_ref[...])
            q = z_ref[rows, C_QA:C_QA + ATT_Q]
            k = z_ref[rows, C_KA:C_KA + ATT_KV]
            q = q * _group_rms_scale(q, hones_ref[...], HEAD_DIM) * gqn_ref[...]
            k = k * _group_rms_scale(k, hones_ref[0:ATT_KV, 0:ATT_KV], HEAD_DIM) * gkn_ref[...]
            v = z_ref[rows, C_VA:C_VA + ATT_KV]
            if latent:
                cos_ref, sin_ref = latent_refs[3], latent_refs[4]
                cos = jnp.concatenate([cos_ref[rows, :]] * (ATT_Q // ATT_KV), axis=-1)
                sin = jnp.concatenate([sin_ref[rows, :]] * (ATT_Q // ATT_KV), axis=-1)
                q = _rope(q, cos, sin)
                k = _rope(k, cos_ref[rows, :], sin_ref[rows, :])
            else:
                out_refs[3][s, rows, :] = k
                out_refs[4][s, rows, :] = v
            qr_ref[rows, :] = (q * (HEAD_DIM ** -0.5)).astype(_BF16)
            _store_split_kv(k, v, ks_ref, vs_ref, slice(pad + r0, pad + r0 + PROJ_TILE))

    for s, (z_ref, qr_ref, ks_ref, vs_ref, la_ref, o_ref, st_ref, upad_ref) in enumerate(per_seq):
        if latent:
            _attention_latent(T, layer, qr_ref, ks_ref, vs_ref, one(latent_refs[0], s), one(latent_refs[1], s),
                              sink_ref, one(out_refs[0], s))
        else:
            _attention_ctx(T, layer, qr_ref, ks_ref, vs_ref, sink_ref, one(out_refs[0], s))

    worst = None
    for s, (z_ref, qr_ref, ks_ref, vs_ref, la_ref, o_ref, st_ref, upad_ref) in enumerate(per_seq):
        if latent:
            st_ref[...] = latent_refs[2][s]
        else:
            st_ref[...] = jnp.zeros(st_ref.shape, _F32)
        span = _gla_prepare(T, z_ref, la_ref, o_ref, wg2_ref, bg2_ref)
        worst = span if worst is None else jnp.maximum(worst, span)
    _gla_run(T, [(z_ref, la_ref, o_ref, st_ref) for z_ref, _, _, _, la_ref, o_ref, st_ref, _ in per_seq],
             worst, a_ref, wg2_ref, bg2_ref, latent)

    for s, (z_ref, qr_ref, ks_ref, vs_ref, la_ref, o_ref, st_ref, upad_ref) in enumerate(per_seq):
        _gla_finish(T, z_ref, o_ref, ggla_ref, vones_ref, one(out_refs[1], s))
        if not latent:
            for d in range(2):
                for pair in range(GLA_HEADS // 2):
                    out_refs[5][s, d, pair] = st_ref[d, pair].T
        _pool(T, z_ref, upad_ref, wpool_ref, pscale_ref, one(out_refs[2], s))


def _merge(x, mod_ref, oa, ob, oc, gn1_ref, wgate_ref, wa_ref, wb_ref, wc_ref, wout_ref):
    mod = lambda i: mod_ref[0, :, i * D_MODEL:(i + 1) * D_MODEL]
    hn = (x * _rms_scale(x) * gn1_ref[...]) * (1.0 + mod(1)) + mod(0)
    gates = jax.nn.sigmoid(_dot(hn.astype(_BF16), wgate_ref[...]))
    mixed = (gates[:, 0:D_MODEL] * _dot(oa, wa_ref[...])
             + gates[:, D_MODEL:2 * D_MODEL] * _dot(ob, wb_ref[...])
             + gates[:, 2 * D_MODEL:3 * D_MODEL] * _dot(oc, wc_ref[...]))
    return x + mod(2) * _dot(mixed.astype(_BF16), wout_ref[...])


def _ffn(x, mod_ref, gn2_ref, wfg_ref, wfu_ref, wfd_ref):
    mod = lambda i: mod_ref[0, :, i * D_MODEL:(i + 1) * D_MODEL]
    hn = ((x * _rms_scale(x) * gn2_ref[...]) * (1.0 + mod(4)) + mod(3)).astype(_BF16)
    h = _silu(_dot(hn, wfg_ref[...])) * _dot(hn, wfu_ref[...])
    return x + mod(5) * _dot(h.astype(_BF16), wfd_ref[...])


def _mix_latent_kernel(T, layer_ref, *refs):
    params, refs = refs[:N_MIX_PARAMS], refs[N_MIX_PARAMS:]
    _mix_body(True, T, layer_ref[0], params, refs[:5], refs[5:8], refs[8:])


def _mix_ctx_kernel(T, layer_ref, *refs):
    params, refs = refs[:N_MIX_PARAMS], refs[N_MIX_PARAMS + 3:]
    _mix_body(False, T, layer_ref[0], params, None, refs[:6], refs[6:])


def _layer_spec(shape):
    zeros = (0,) * len(shape)
    return pl.BlockSpec((None,) + tuple(shape), lambda i, layer: (layer[0],) + zeros,
                        pipeline_mode=pl.Buffered(1))


def _const_spec(blk):
    return pl.BlockSpec(blk, lambda i, layer: (0,) * len(blk), pipeline_mode=pl.Buffered(1))


def _mix_params(x, x_spec, mod_spec, mod_all, pw):
    specs = [
        x_spec, mod_spec,
        _layer_spec((1, D_MODEL)),
        _layer_spec((D_MODEL, C_UC)),
        _layer_spec((D_MODEL, MIX_W - C_UC)),
        _layer_spec((1, ATT_Q)), _layer_spec((1, ATT_KV)),
        pl.BlockSpec(memory_space=pltpu.SMEM),
        _layer_spec((2, GL_PAD, 2 * GLA_QK)), _layer_spec((1, 2 * GLA_QK)), _layer_spec((1, GLA_VW)),
        _layer_spec((POOL_GROUPS, POOL_GROUP_DIM, POOL_GROUP_DIM)), _layer_spec((1, POOL_W)),
        _const_spec((ATT_Q, ATT_Q)), _const_spec((GLA_VW, GLA_VW)),
    ]
    args = [x, mod_all, pw["g_norm1"], pw["w_main"], pw["w_tail"], pw["g_qn"], pw["g_kn"], pw["att_sink"], pw["w_gate2"],
            pw["b_gate2"], pw["g_gla_out"], pw["w_pool"], pw["pool_scale"], pw["head_ones"], pw["gla_ones"]]
    assert len(specs) == len(args) == N_MIX_PARAMS
    return specs, args


def _mix_scratch(S, T, kv_rows):
    scratch = [
        pltpu.VMEM((S, T, MIX_W), _F32),
        pltpu.VMEM((S, T, ATT_Q), _BF16),
        pltpu.VMEM((S, ATT_KV_HEADS, 2, kv_rows, 2 * HEAD_DIM), _BF16),
        pltpu.VMEM((S, ATT_KV_HEADS, 2, kv_rows, 4 * HEAD_DIM), _BF16),
        pltpu.VMEM((S, T, 2 * GLA_QK), _F32),
        pltpu.VMEM((S, T, GLA_VW), _F32),
        pltpu.VMEM((S, 2, GLA_HEADS // 2, GLA_DV, 2 * GLA_DK), _F32),
        pltpu.VMEM((GLA_HEADS + 2, GLA_CHUNK, GLA_QK), _F32),
        pltpu.VMEM((S, T + 2 * _pool_halo(T), POOL_W), _BF16),
    ]
    assert len(scratch) == N_MIX_SCRATCH
    return scratch


def _mix_latent_call(layer, x, mod_all, pw, cache_k, cache_v, st0, cos, sin):
    B, T, _ = x.shape
    per_seq = lambda blk: pl.BlockSpec(blk, lambda b, layer: (b,) + (0,) * (len(blk) - 1))
    mod_spec = pl.BlockSpec((None, 1, 1, 6 * D_MODEL), lambda b, layer: (layer[0], b + 1, 0, 0))
    in_specs, args = _mix_params(x, per_seq((1, T, D_MODEL)), mod_spec, mod_all, pw)
    P = cache_k.shape[2]
    cache_spec = pl.BlockSpec((1, 1, P, ATT_KV), lambda b, layer: (b, layer[0], 0, 0))
    in_specs += [cache_spec, cache_spec,
                 pl.BlockSpec((1, None, 2, GLA_HEADS // 2, GLA_DV, 2 * GLA_DK),
                              lambda b, layer: (b, layer[0], 0, 0, 0, 0)),
                 _const_spec((T, ATT_KV)), _const_spec((T, ATT_KV))]
    args += [cache_k, cache_v, st0, cos, sin]
    widths = (ATT_Q, GLA_VW, POOL_W)
    return pl.pallas_call(
        functools.partial(_mix_latent_kernel, T),
        grid_spec=pltpu.PrefetchScalarGridSpec(
            num_scalar_prefetch=1, grid=(B,), in_specs=in_specs,
            out_specs=[per_seq((1, T, w)) for w in widths],
            scratch_shapes=_mix_scratch(1, T, T + 2 * ATT_BLOCK)),
        out_shape=[jax.ShapeDtypeStruct((B, T, w), _BF16) for w in widths],
        compiler_params=pltpu.CompilerParams(dimension_semantics=("arbitrary",), vmem_limit_bytes=VMEM_LIMIT),
        name="mix_latent",
    )(layer, *args)


def _mix_ctx_call(layer, x, mod_all, pw, stacked):
    B, T, _ = x.shape
    S = CTX_SEQS_PER_STEP
    assert B % S == 0
    per_seq = lambda blk: pl.BlockSpec(blk, lambda b, layer: (b,) + (0,) * (len(blk) - 1))
    mod_spec = pl.BlockSpec((None, 1, 1, 6 * D_MODEL), lambda b, layer: (layer[0], 0, 0, 0))
    in_specs, args = _mix_params(x, per_seq((S, T, D_MODEL)), mod_spec, mod_all, pw)
    widths = (ATT_Q, GLA_VW, POOL_W)
    n_in = 1 + len(args)
    aliases = {n_in + j: len(widths) + j for j in range(len(stacked))}
    in_specs += [pl.BlockSpec(memory_space=pl.ANY)] * len(stacked)
    args += list(stacked)
    at_layer = lambda blk: pl.BlockSpec((S, None) + blk, lambda b, layer: (b, layer[0]) + (0,) * len(blk))
    out_specs = [per_seq((S, T, w)) for w in widths] + [
        at_layer((T, ATT_KV)), at_layer((T, ATT_KV)), at_layer((2, GLA_HEADS // 2, 2 * GLA_DK, GLA_DV))]
    out_shape = ([jax.ShapeDtypeStruct((B, T, w), _BF16) for w in widths]
                 + [jax.ShapeDtypeStruct(a.shape, a.dtype) for a in stacked])
    return pl.pallas_call(
        functools.partial(_mix_ctx_kernel, T),
        grid_spec=pltpu.PrefetchScalarGridSpec(
            num_scalar_prefetch=1, grid=(B // S,), in_specs=in_specs, out_specs=out_specs,
            scratch_shapes=_mix_scratch(S, T, T)),
        out_shape=out_shape,
        input_output_aliases=aliases,
        compiler_params=pltpu.CompilerParams(dimension_semantics=("arbitrary",), vmem_limit_bytes=VMEM_LIMIT),
        name="mix_ctx",
    )(layer, *args)


def _post_kernel(n_first, layer_ref, mod_ref, *refs):
    first, second, refs = refs[0:4], refs[4:8], refs[8:]
    gn1_ref, gn2_ref, wgate_ref, wa_ref, wb_ref, wc_ref, wout_ref, wfg_ref, wfu_ref, wfd_ref = refs[:10]
    outs = refs[10:]

    def run(x_ref, oa_ref, ob_ref, oc_ref, out_ref):
        x = _merge(x_ref[...], mod_ref, oa_ref[...], ob_ref[...], oc_ref[...], gn1_ref, wgate_ref, wa_ref,
                   wb_ref, wc_ref, wout_ref)
        out_ref[...] = _ffn(x, mod_ref, gn2_ref, wfg_ref, wfu_ref, wfd_ref)

    on_first = pl.program_id(0) < n_first

    @pl.when(on_first)
    def _():
        run(*first, outs[0])

    @pl.when(jnp.logical_not(on_first))
    def _():
        run(*second, outs[1])


def _post_call(layer, mod_all, pw, first, second, tiles_per_seq):
    n0, n1 = first[0].shape[0] // POST_TILE, second[0].shape[0] // POST_TILE
    row0 = lambda w: pl.BlockSpec((POST_TILE, w), lambda i, layer: (jnp.minimum(i, n0 - 1), 0))
    row1 = lambda w: pl.BlockSpec((POST_TILE, w), lambda i, layer: (jnp.maximum(i - n0, 0), 0))
    mod_spec = pl.BlockSpec(
        (None, 1, 1, 6 * D_MODEL),
        lambda i, layer: (layer[0], jnp.where(i < n0, 0, 1 + jnp.maximum(i - n0, 0) // tiles_per_seq), 0, 0))
    weights = [(pw["g_norm1"], (1, D_MODEL)), (pw["g_norm2"], (1, D_MODEL)), (pw["w_gates"], (D_MODEL, GATE_W)),
               (pw["w_br_a"], (ATT_Q, D_MODEL)), (pw["w_br_b"], (GLA_VW, D_MODEL)),
               (pw["w_br_c"], (POOL_W, D_MODEL)), (pw["w_out"], (D_MODEL, D_MODEL)),
               (pw["w_ff_gate"], (D_MODEL, D_FF)), (pw["w_ff_up"], (D_MODEL, D_FF)),
               (pw["w_ff_down"], (D_FF, D_MODEL))]
    widths = (D_MODEL, ATT_Q, GLA_VW, POOL_W)
    in_specs = ([mod_spec] + [row0(w) for w in widths] + [row1(w) for w in widths]
                + [_layer_spec(shape) for _, shape in weights])
    return pl.pallas_call(
        functools.partial(_post_kernel, n0),
        grid_spec=pltpu.PrefetchScalarGridSpec(
            num_scalar_prefetch=1, grid=(n0 + n1,), in_specs=in_specs, out_specs=[row0(D_MODEL), row1(D_MODEL)]),
        out_shape=[jax.ShapeDtypeStruct(first[0].shape, _F32), jax.ShapeDtypeStruct(second[0].shape, _F32)],
        input_output_aliases={2: 0, 6: 1},
        compiler_params=pltpu.CompilerParams(dimension_semantics=("arbitrary",), vmem_limit_bytes=VMEM_LIMIT),
        name="post",
    )(layer, mod_all, *first, *second, *[a for a, _ in weights])


def _rope_tables(T):
    quarter = HEAD_DIM // 4
    inv_freq = ROPE_BASE ** (-np.arange(quarter, dtype=np.float32) / quarter)
    pos = np.arange(T)
    ang_row = (pos // GRID_W).astype(np.float32)[:, None] * inv_freq[None, :]
    ang_col = (pos % GRID_W).astype(np.float32)[:, None] * inv_freq[None, :]
    cos = np.concatenate([np.cos(ang_row)] * 2 + [np.cos(ang_col)] * 2, axis=-1)
    sin = np.concatenate([-np.sin(ang_row), np.sin(ang_row), -np.sin(ang_col), np.sin(ang_col)], axis=-1)
    return (jnp.asarray(np.tile(cos, (1, ATT_KV_HEADS)), _F32), jnp.asarray(np.tile(sin, (1, ATT_KV_HEADS)), _F32))


def _prepare_weights(w_in, g_qn, g_kn, att_sink, w_gate2, b_gate2, g_gla_out, w_pool, pool_scale, w_br_a,
                     w_br_b, w_br_c, w_out, g_norm1, g_norm2, w_ff_gate, w_ff_up, w_ff_down):
    o_gl = ATT_Q + 2 * ATT_KV + 2 * GLA_QK + 2 * GLA_VW
    o_uc = o_gl + 2 * GLA_RANK
    o_gate = o_uc + POOL_W
    assert o_gl == C_UC
    w_main = w_in[:, :, :o_gl].astype(_BF16)
    w_tail = jnp.concatenate(
        [w_in[:, :, o_uc:o_gate], w_in[:, :, o_gl:o_uc],
         jnp.zeros((DEPTH, D_MODEL, GL_PAD - 2 * GLA_RANK), w_in.dtype)], axis=2).astype(_BF16)
    w_gates = w_in[:, :, o_gate:].astype(_BF16)
    wg2 = jnp.zeros((DEPTH, GL_PAD, 2 * GLA_QK), _F32)
    wg2 = wg2.at[:, 0:GLA_RANK, 0:GLA_QK].set(w_gate2[:, 0])
    wg2 = wg2.at[:, GLA_RANK:2 * GLA_RANK, GLA_QK:].set(w_gate2[:, 1])
    wg2_hi = wg2.astype(_BF16)
    wg2 = jnp.stack([wg2_hi, (wg2 - wg2_hi.astype(_F32)).astype(_BF16)], axis=1)
    vec = lambda a: a.reshape(DEPTH, 1, -1)
    group_ones = lambda n, width: jnp.asarray(
        (np.arange(n)[:, None] // width) == (np.arange(n)[None, :] // width), _BF16)
    return {
        "head_ones": group_ones(ATT_Q, HEAD_DIM),
        "gla_ones": group_ones(GLA_VW, GLA_DV),
        "w_main": w_main,
        "w_tail": w_tail,
        "w_gates": w_gates,
        "g_qn": vec(jnp.tile(g_qn, (1, ATT_HEADS))),
        "g_kn": vec(jnp.tile(g_kn, (1, ATT_KV_HEADS))),
        "att_sink": att_sink,
        "w_gate2": wg2,
        "b_gate2": vec(b_gate2),
        "g_gla_out": vec(jnp.tile(g_gla_out, (1, GLA_HEADS))),
        "w_pool": w_pool.astype(_BF16),
        "pool_scale": vec(pool_scale),
        "w_br_a": w_br_a.astype(_BF16),
        "w_br_b": w_br_b.astype(_BF16),
        "w_br_c": w_br_c.astype(_BF16),
        "w_out": w_out.astype(_BF16),
        "g_norm1": vec(g_norm1),
        "g_norm2": vec(g_norm2),
        "w_ff_gate": w_ff_gate.astype(_BF16),
        "w_ff_up": w_ff_up.astype(_BF16),
        "w_ff_down": w_ff_down.astype(_BF16),
    }


def kernel(x_prompt, x_sample, c, cache_k, cache_v, state_gla, c_ctx, w_in, g_qn, g_kn, att_sink, w_gate2,
           b_gate2, g_gla_out, w_pool, pool_scale, w_br_a, w_br_b, w_br_c, w_out, g_norm1, g_norm2, w_mod,
           b_mod, w_ff_gate, w_ff_up, w_ff_down):
    B, T, _ = x_prompt.shape
    BL, TL, _ = x_sample.shape
    assert (B * T) % POST_TILE == 0 and TL % POST_TILE == 0 and BL + 1 <= MOD_ROWS
    assert T % PROJ_TILE == 0 and TL % PROJ_TILE == 0
    cv = jnp.concatenate([c_ctx[None, :], c, jnp.zeros((MOD_ROWS - 1 - BL, D_MODEL), _F32)], axis=0)
    mod_all = _modulation(cv, w_mod, b_mod).reshape(DEPTH, MOD_ROWS, 1, 6 * D_MODEL)
    pw = _prepare_weights(w_in, g_qn, g_kn, att_sink, w_gate2, b_gate2, g_gla_out, w_pool, pool_scale, w_br_a,
                          w_br_b, w_br_c, w_out, g_norm1, g_norm2, w_ff_gate, w_ff_up, w_ff_down)
    cos, sin = _rope_tables(TL)
    P = cache_k.shape[2]
    latent_ctx = (cache_k.reshape(BL, DEPTH, P, ATT_KV), cache_v.reshape(BL, DEPTH, P, ATT_KV),
                  jnp.swapaxes(state_gla.reshape(BL, DEPTH, 2, GLA_HEADS // 2, 2 * GLA_DK, GLA_DV), -1, -2),
                  cos, sin)

    def layer_step(l, carry):
        yp, ys, new_k, new_v, new_st = carry
        layer = jnp.full((1,), l, jnp.int32)
        oa, ob, oc, new_k, new_v, new_st = _mix_ctx_call(layer, yp, mod_all, pw, (new_k, new_v, new_st))
        la, lb, lc = _mix_latent_call(layer, ys, mod_all, pw, *latent_ctx)
        flat = lambda a: a.reshape(-1, a.shape[-1])
        yp, ys = _post_call(layer, mod_all, pw, (flat(yp), flat(oa), flat(ob), flat(oc)),
                            (flat(ys), flat(la), flat(lb), flat(lc)), TL // POST_TILE)
        yp, ys = yp.reshape(B, T, D_MODEL), ys.reshape(BL, TL, D_MODEL)
        return yp, ys, new_k, new_v, new_st

    init = (x_prompt, x_sample,
            jnp.zeros((B, DEPTH, T, ATT_KV), _F32), jnp.zeros((B, DEPTH, T, ATT_KV), _F32),
            jnp.zeros((B, DEPTH, 2, GLA_HEADS // 2, 2 * GLA_DK, GLA_DV), _F32))
    yp, ys, new_k, new_v, new_st = lax.fori_loop(0, DEPTH, layer_step, init)
    return (yp, ys, new_k.reshape(B, DEPTH, T, ATT_KV_HEADS, HEAD_DIM),
            new_v.reshape(B, DEPTH, T, ATT_KV_HEADS, HEAD_DIM),
            new_st.reshape(B, DEPTH, 2, GLA_HEADS, GLA_DK, GLA_DV))
```

```python
import functools

import jax
import jax.numpy as jnp
import numpy as np
from jax import lax
from jax.experimental import pallas as pl
from jax.experimental.pallas import tpu as pltpu

D_MODEL = 1024
DEPTH = 4
GRID_W = 64
ATT_HEADS = 8
ATT_KV_HEADS = 2
ATT_GROUP = ATT_HEADS // ATT_KV_HEADS
HEAD_DIM = 64
WINDOW = 128
ATT_BLOCK = 128
ROPE_BASE = 10000.0
GLA_HEADS = 4
GLA_DK = 64
GLA_DV = 128
GLA_RANK = 16
GLA_TAU = 16.0
GLA_CHUNK = 64
POOL_GROUPS = 4
POOL_GROUP_DIM = 128
POOL_WINDOWS = (2, 4, 8, 16)
D_FF = 2816
ATT_Q = ATT_HEADS * HEAD_DIM
ATT_KV = ATT_KV_HEADS * HEAD_DIM
GLA_QK = GLA_HEADS * GLA_DK
GLA_VW = GLA_HEADS * GLA_DV
POOL_W = POOL_GROUPS * POOL_GROUP_DIM
EPS = 1e-6
NEG = -1e30

C_QA = 0
C_KA = C_QA + ATT_Q
C_VA = C_KA + ATT_KV
C_QB = C_VA + ATT_KV
C_KB = C_QB + GLA_QK
C_VB = C_KB + GLA_QK
C_RB = C_VB + GLA_VW
C_UC = C_RB + GLA_VW
C_GL = C_UC + POOL_W
GL_PAD = 128
MIX_W = C_GL + GL_PAD
GATE_W = 3 * D_MODEL

POST_TILE = 512
PROJ_TILE = 256
CTX_SEQS_PER_STEP = 2
POOL_TILE = 256
POOL_HALO = 128
MOD_ROWS = 8
MOD_TILE = 3072
GLA_BLOCK = 256
GLA_MAX_EXPONENT = 80.0
VMEM_LIMIT = 56 * 1024 * 1024

_F32 = jnp.float32
_BF16 = jnp.bfloat16


def _dot(a, b):
    return jnp.dot(a, b, preferred_element_type=_F32)


def _dot_nt(a, b):
    return lax.dot_general(a, b, (((1,), (1,)), ((), ())), preferred_element_type=_F32)


def _dot_tn(a, b):
    return lax.dot_general(a, b, (((0,), (0,)), ((), ())), preferred_element_type=_F32)


def _rms_scale(x):
    return lax.rsqrt(jnp.mean(x * x, axis=-1, keepdims=True) + EPS)


def _group_rms_scale(x, group_ones, width):
    return lax.rsqrt(_dot((x * x).astype(_BF16), group_ones) * (1.0 / width) + EPS)


def _log_sigmoid(x):
    return jnp.minimum(x, 0.0) - jnp.log(1.0 + jnp.exp(-jnp.abs(x)))


def _silu(x):
    return x * jax.nn.sigmoid(x)


def _rope(x, cos, sin_signed):
    n = x.shape[-1]
    lane = lax.broadcasted_iota(jnp.int32, x.shape, 1)
    up = pltpu.roll(x, n - HEAD_DIM // 4, axis=1)
    down = pltpu.roll(x, HEAD_DIM // 4, axis=1)
    partner = jnp.where((lane & (HEAD_DIM // 2 - 1)) < HEAD_DIM // 4, up, down)
    return x * cos + partner * sin_signed


def _mod_kernel(cv_ref, w_ref, b_ref, out_ref):
    s = _silu(cv_ref[...]).astype(_BF16)
    out_ref[0] = _dot(s, w_ref[0].astype(_BF16)) + b_ref[0]


def _modulation(cv, w_mod, b_mod):
    n_col = (6 * D_MODEL) // MOD_TILE
    return pl.pallas_call(
        _mod_kernel,
        grid=(DEPTH, n_col),
        in_specs=[
            pl.BlockSpec((MOD_ROWS, D_MODEL), lambda l, j: (0, 0)),
            pl.BlockSpec((1, D_MODEL, MOD_TILE), lambda l, j: (l, 0, j)),
            pl.BlockSpec((1, 1, MOD_TILE), lambda l, j: (l, 0, j)),
        ],
        out_specs=pl.BlockSpec((1, MOD_ROWS, MOD_TILE), lambda l, j: (l, 0, j)),
        out_shape=jax.ShapeDtypeStruct((DEPTH, MOD_ROWS, 6 * D_MODEL), _F32),
        compiler_params=pltpu.CompilerParams(vmem_limit_bytes=VMEM_LIMIT),
        name="modulation",
    )(cv, w_mod, b_mod.reshape(DEPTH, 1, 6 * D_MODEL))


def _split_heads(x):
    low = lax.broadcasted_iota(jnp.int32, x.shape, 1) < HEAD_DIM
    swapped = pltpu.roll(x, HEAD_DIM, axis=1)
    zero = jnp.zeros_like(x)
    return ((jnp.where(low, x, zero), jnp.where(low, zero, swapped)),
            (jnp.where(low, swapped, zero), jnp.where(low, zero, x)))


def _store_split_kv(k, v, ks_ref, vs_ref, rows):
    ones = jnp.ones_like(v)
    for kv, (k_sides, v_sides, one_sides) in enumerate(zip(_split_heads(k), _split_heads(v), _split_heads(ones))):
        for side in range(2):
            ks_ref[kv, side, rows, :] = k_sides[side].astype(_BF16)
            vs_ref[kv, side, rows, :] = jnp.concatenate([v_sides[side], one_sides[side]], axis=-1).astype(_BF16)


def _pair_softmax_av(qp, keys, values, masks, sink_even, sink_odd):
    m = qp.shape[0]
    scores = []
    for (k_left, k_right), mask in zip(keys, masks):
        s_even, s_odd = _dot_nt(qp, k_left), _dot_nt(qp, k_right)
        if mask is not None:
            s_even, s_odd = jnp.where(mask, s_even, NEG), jnp.where(mask, s_odd, NEG)
        scores.append((s_even, s_odd))
    m_even = jnp.full((m, 1), sink_even, _F32)
    m_odd = jnp.full((m, 1), sink_odd, _F32)
    for s_even, s_odd in scores:
        m_even = jnp.maximum(m_even, jnp.max(s_even, axis=-1, keepdims=True))
        m_odd = jnp.maximum(m_odd, jnp.max(s_odd, axis=-1, keepdims=True))
    res = None
    for (s_even, s_odd), (w_left, w_right) in zip(scores, values):
        r = (_dot(jnp.exp(s_even - m_even).astype(_BF16), w_left)
             + _dot(jnp.exp(s_odd - m_odd).astype(_BF16), w_right))
        res = r if res is None else res + r
    pair = 2 * HEAD_DIM
    low = lax.broadcasted_iota(jnp.int32, (m, pair), 1) < HEAD_DIM
    den = res[:, pair:] + jnp.where(low, jnp.exp(sink_even - m_even), jnp.exp(sink_odd - m_odd))
    return res[:, :pair] / den


def _attention_ctx(T, layer, qr_ref, ks_ref, vs_ref, sink_ref, oa_ref):
    pair = 2 * HEAD_DIM
    for kv in range(ATT_KV_HEADS):
        keys = [(ks_ref[kv, 0], ks_ref[kv, 1])]
        values = [(vs_ref[kv, 0], vs_ref[kv, 1])]
        for j in range(ATT_GROUP // 2):
            head = kv * ATT_GROUP + 2 * j
            cols = slice(head * HEAD_DIM, head * HEAD_DIM + pair)
            o = _pair_softmax_av(qr_ref[:, cols], keys, values, [None], sink_ref[layer, head],
                                 sink_ref[layer, head + 1])
            oa_ref[0, :, cols] = o.astype(_BF16)


def _attention_latent(T, layer, qr_ref, ks_ref, vs_ref, kc_ref, vc_ref, sink_ref, oa_ref):
    pair = 2 * HEAD_DIM
    span = 3 * ATT_BLOCK
    kc, vc = kc_ref[0, 0], vc_ref[0, 0]
    ones = jnp.ones_like(vc)
    ctx_keys = [tuple(side.astype(_BF16) for side in sides) for sides in _split_heads(kc)]
    ctx_values = [tuple(jnp.concatenate([v_side, one_side], axis=-1).astype(_BF16)
                        for v_side, one_side in zip(v_sides, one_sides))
                  for v_sides, one_sides in zip(_split_heads(vc), _split_heads(ones))]

    def block(i, carry):
        q_rows = pl.ds(pl.multiple_of(i * ATT_BLOCK, ATT_BLOCK), ATT_BLOCK)
        k_rows = pl.ds(pl.multiple_of(i * ATT_BLOCK, ATT_BLOCK), span)
        q_pos = i * ATT_BLOCK + lax.broadcasted_iota(jnp.int32, (ATT_BLOCK, span), 0)
        k_pos = (i - 1) * ATT_BLOCK + lax.broadcasted_iota(jnp.int32, (ATT_BLOCK, span), 1)
        valid = (jnp.abs(k_pos - q_pos) <= WINDOW) & (k_pos >= 0) & (k_pos < T)
        for kv in range(ATT_KV_HEADS):
            keys = [(ks_ref[kv, 0, k_rows, :], ks_ref[kv, 1, k_rows, :]), ctx_keys[kv]]
            values = [(vs_ref[kv, 0, k_rows, :], vs_ref[kv, 1, k_rows, :]), ctx_values[kv]]
            for j in range(ATT_GROUP // 2):
                head = kv * ATT_GROUP + 2 * j
                cols = slice(head * HEAD_DIM, head * HEAD_DIM + pair)
                o = _pair_softmax_av(qr_ref[q_rows, cols], keys, values, [valid, None], sink_ref[layer, head],
                                     sink_ref[layer, head + 1])
                oa_ref[0, q_rows, cols] = o.astype(_BF16)
        return carry

    lax.fori_loop(0, T // ATT_BLOCK, block, 0, unroll=2)


def _gla_chunk(z_ref, la_ref, o_ref, st_ref, a_ref, direction, start):
    C = GLA_CHUNK
    rows = pl.ds(pl.multiple_of(start, C), C)
    q = z_ref[rows, C_QB:C_QB + GLA_QK] * (GLA_DK ** -0.5)
    k = z_ref[rows, C_KB:C_KB + GLA_QK]
    v = z_ref[rows, C_VB:C_VB + GLA_VW].astype(_BF16)
    la = la_ref[rows, direction * GLA_QK:(direction + 1) * GLA_QK]
    la_hi = la.astype(_BF16)
    la_lo = (la - la_hi.astype(_F32)).astype(_BF16)
    ti = lax.broadcasted_iota(jnp.int32, (C, C), 0)
    si = lax.broadcasted_iota(jnp.int32, (C, C), 1)
    causal = (si <= ti) if direction == 0 else (si >= ti)
    tri = jnp.where(causal, 1.0, 0.0).astype(_BF16)
    b = _dot(tri, la_hi) + _dot(tri, la_lo)
    end = C - 1 if direction == 0 else 0
    b_end = b[end:end + 1]
    q_in = (q * jnp.exp(b)).astype(_BF16)
    k_out = (k * jnp.exp(b_end - b)).astype(_BF16)
    e_end = jnp.exp(b_end)
    ones = jnp.ones((8, GLA_DK), _BF16)

    def row_group(g, carry):
        base = pl.multiple_of(g * 8, 8)
        b8 = a_ref[GLA_HEADS, pl.ds(base, 8), :]
        q8 = a_ref[GLA_HEADS + 1, pl.ds(base, 8), :]
        s_idx = lax.broadcasted_iota(jnp.int32, (C, 1), 0)
        rows_h = [[] for _ in range(GLA_HEADS)]
        for j in range(8):
            ok = (s_idx <= base + j) if direction == 0 else (s_idx >= base + j)
            decay = jnp.exp(jnp.where(ok, b8[j:j + 1] - b, NEG))
            p = (q8[j:j + 1] * k * decay).astype(_BF16)
            for h in range(GLA_HEADS):
                rows_h[h].append(_dot_nt(ones, p[:, h * GLA_DK:(h + 1) * GLA_DK])[0:1])
        for h in range(GLA_HEADS):
            a_ref[h, pl.ds(base, 8), 0:C] = jnp.concatenate(rows_h[h], axis=0)
        return carry

    a_ref[GLA_HEADS] = b
    a_ref[GLA_HEADS + 1] = q
    lax.fori_loop(0, C // 8, row_group, 0)
    outs = []
    for pair in range(GLA_HEADS // 2):
        s_pair = st_ref[direction, pair]
        s_next = []
        for h in (2 * pair, 2 * pair + 1):
            kc = slice(h * GLA_DK, (h + 1) * GLA_DK)
            vc = slice(h * GLA_DV, (h + 1) * GLA_DV)
            s_t = s_pair[:, (h % 2) * GLA_DK:(h % 2 + 1) * GLA_DK]
            outs.append(_dot_nt(q_in[:, kc], s_t.astype(_BF16)) + _dot(a_ref[h, :, 0:C].astype(_BF16), v[:, vc]))
            s_next.append(s_t * e_end[:, kc] + _dot_tn(v[:, vc], k_out[:, kc]))
        st_ref[direction, pair] = jnp.concatenate(s_next, axis=-1)
    o_ref[rows, :] += jnp.concatenate(outs, axis=-1)


def _gla_block(z_ref, la_ref, o_ref, st_ref, direction, start, use_state):
    NB = GLA_BLOCK
    fwd = direction == 0
    rows = slice(start, start + NB) if isinstance(start, int) else pl.ds(pl.multiple_of(start, NB), NB)
    q = z_ref[rows, C_QB:C_QB + GLA_QK] * (GLA_DK ** -0.5)
    k = z_ref[rows, C_KB:C_KB + GLA_QK]
    v = z_ref[rows, C_VB:C_VB + GLA_VW].astype(_BF16)
    b = la_ref[rows, direction * GLA_QK:(direction + 1) * GLA_QK]
    end, mid = (NB - 1, NB // 2 - 1) if fwd else (0, NB // 2)
    b_end = b[end:end + 1]
    c = b - b[mid:mid + 1]
    q_c = (q * jnp.exp(c)).astype(_BF16)
    k_c = (k * jnp.exp(-c)).astype(_BF16)
    k_fin = (k * jnp.exp(b_end - b)).astype(_BF16)
    if use_state:
        q_in = (q * jnp.exp(b)).astype(_BF16)
        e_all = jnp.exp(b_end)
    ti = lax.broadcasted_iota(jnp.int32, (NB, NB), 0)
    si = lax.broadcasted_iota(jnp.int32, (NB, NB), 1)
    causal = (si <= ti) if fwd else (si >= ti)
    tile = 2 * GLA_DK
    low = lax.broadcasted_iota(jnp.int32, (NB, tile), 1) < GLA_DK
    zero = jnp.zeros((NB, tile), _BF16)
    pick = lambda x, parity: jnp.where(low, x, zero) if parity == 0 else jnp.where(low, zero, x)
    outs = []
    for pair in range(GLA_HEADS // 2):
        lanes = slice(pair * tile, (pair + 1) * tile)
        if use_state:
            s_pair = st_ref[direction, pair]
            s_bf = s_pair.astype(_BF16)
        s_new = None
        for parity in range(2):
            h = 2 * pair + parity
            vc = slice(h * GLA_DV, (h + 1) * GLA_DV)
            a = jnp.where(causal, _dot_nt(q_c[:, lanes], pick(k_c[:, lanes], parity)), 0.0)
            o_h = _dot(a.astype(_BF16), v[:, vc])
            if use_state:
                o_h = o_h + _dot_nt(pick(q_in[:, lanes], parity), s_bf)
            outs.append(o_h)
            upd = _dot_tn(v[:, vc], pick(k_fin[:, lanes], parity))
            s_new = upd if s_new is None else s_new + upd
        if use_state:
            s_new = s_new + s_pair * e_all[:, lanes]
        st_ref[direction, pair] = s_new
    o_ref[rows, :] += jnp.concatenate(outs, axis=-1)


def _log_decay(z_ref, rows, wg2_ref, bg2_ref):
    x = z_ref[rows, C_GL:C_GL + GL_PAD]
    x_hi = x.astype(_BF16)
    x_lo = (x - x_hi.astype(_F32)).astype(_BF16)
    w_hi, w_lo = wg2_ref[0], wg2_ref[1]
    pre = _dot(x_hi, w_hi) + (_dot(x_lo, w_hi) + _dot(x_hi, w_lo)) + bg2_ref[...]
    return _log_sigmoid(pre) * (1.0 / GLA_TAU)


def _gla_prepare(T, z_ref, la_ref, o_ref, wg2_ref, bg2_ref, tri_ref):
    NB = GLA_BLOCK
    worst = None
    for r0 in range(0, T, NB):
        rows = slice(r0, r0 + NB)
        la = _log_decay(z_ref, rows, wg2_ref, bg2_ref)
        la_hi = la.astype(_BF16)
        la_lo = (la - la_hi.astype(_F32)).astype(_BF16)
        for d in range(2):
            cols = slice(d * GLA_QK, (d + 1) * GLA_QK)
            b = _dot(tri_ref[d], la_hi[:, cols]) + _dot(tri_ref[d], la_lo[:, cols])
            la_ref[rows, cols] = b
            first, mid, last = (0, NB // 2 - 1, NB - 1) if d == 0 else (NB - 1, NB // 2, 0)
            span = jnp.max(jnp.maximum(b[first:first + 1] - b[mid:mid + 1], b[mid:mid + 1] - b[last:last + 1]))
            worst = span if worst is None else jnp.maximum(worst, span)
    o_ref[...] = jnp.zeros(o_ref.shape, _F32)
    return worst


def _gla_run(T, seqs, worst, a_ref, wg2_ref, bg2_ref, use_state):
    NB = GLA_BLOCK
    n_chunks = T // GLA_CHUNK
    n_blocks = T // NB
    fast_ok = worst <= GLA_MAX_EXPONENT

    @pl.when(fast_ok)
    def _():
        if n_blocks == 1:
            for direction in range(2):
                for z_ref, la_ref, o_ref, st_ref in seqs:
                    _gla_block(z_ref, la_ref, o_ref, st_ref, direction, 0, use_state)
        else:
            def body(i, carry):
                for z_ref, la_ref, o_ref, st_ref in seqs:
                    _gla_block(z_ref, la_ref, o_ref, st_ref, 0, i * NB, True)
                    _gla_block(z_ref, la_ref, o_ref, st_ref, 1, (n_blocks - 1 - i) * NB, True)
                return carry
            lax.fori_loop(0, n_blocks, body, 0, unroll=2)

    @pl.when(jnp.logical_not(fast_ok))
    def _():
        for z_ref, la_ref, o_ref, st_ref in seqs:
            for r0 in range(0, T, NB):
                la_ref[r0:r0 + NB, :] = _log_decay(z_ref, slice(r0, r0 + NB), wg2_ref, bg2_ref)

            def body(i, carry):
                _gla_chunk(z_ref, la_ref, o_ref, st_ref, a_ref, 0, i * GLA_CHUNK)
                _gla_chunk(z_ref, la_ref, o_ref, st_ref, a_ref, 1, (n_chunks - 1 - i) * GLA_CHUNK)
                return carry
            lax.fori_loop(0, n_chunks, body, 0)


def _gla_finish(T, z_ref, o_ref, ggla_ref, vones_ref, ob_ref):
    for r0 in range(0, T, PROJ_TILE):
        rows = slice(r0, r0 + PROJ_TILE)
        o = o_ref[rows, :]
        y = o * _group_rms_scale(o, vones_ref[...], GLA_DV) * ggla_ref[...]
        ob_ref[0, rows, :] = (y * _silu(z_ref[rows, C_RB:C_RB + GLA_VW])).astype(_BF16)


def _pool_halo(T):
    return POOL_HALO if T > POOL_TILE else 0


def _pool_bands(T):
    halo = _pool_halo(T)
    off = np.arange(POOL_TILE + 2 * halo)[None, :] - halo - np.arange(POOL_TILE)[:, None]
    return jnp.asarray(np.stack([(off >= -(w // 2)) & (off < w - w // 2) for w in POOL_WINDOWS]), _BF16)


def _scan_masks():
    t = np.arange(GLA_BLOCK)
    return jnp.asarray(np.stack([t[None, :] <= t[:, None], t[None, :] >= t[:, None]]), _BF16)


def _pool(T, z_ref, upad_ref, wpool_ref, pscale_ref, bands_ref, oc_ref):
    halo = _pool_halo(T)
    if halo:
        zeros = jnp.zeros((halo, POOL_W), _BF16)
        upad_ref[0:halo, :] = zeros
        upad_ref[halo + T:halo + T + halo, :] = zeros
    upad_ref[halo:halo + T, :] = z_ref[:, C_UC:C_UC + POOL_W].astype(_BF16)
    span = POOL_TILE + 2 * halo
    for jb in range(T // POOL_TILE):
        t = jb * POOL_TILE + lax.broadcasted_iota(jnp.int32, (POOL_TILE, 1), 0)
        parts = []
        for g, w in enumerate(POOL_WINDOWS):
            cols = slice(g * POOL_GROUP_DIM, (g + 1) * POOL_GROUP_DIM)
            win = slice(jb * POOL_TILE, jb * POOL_TILE + span)
            total = _dot(bands_ref[g], upad_ref[win, cols])
            cnt = (jnp.minimum(t - w // 2 + w, T) - jnp.maximum(t - w // 2, 0)).astype(_F32)
            pooled = total / cnt - z_ref[jb * POOL_TILE:(jb + 1) * POOL_TILE, C_UC + g * POOL_GROUP_DIM:
                                         C_UC + (g + 1) * POOL_GROUP_DIM]
            parts.append(_dot(pooled.astype(_BF16), wpool_ref[g]))
        y = jnp.concatenate(parts, axis=-1) * pscale_ref[...]
        oc_ref[0, jb * POOL_TILE:(jb + 1) * POOL_TILE, :] = y.astype(_BF16)


N_MIX_PARAMS = 17
N_MIX_SCRATCH = 9


def _mix_body(latent, T, layer, params, latent_refs, out_refs, scratch):
    (x_ref, mod_ref, gn1_ref, wmain_ref, wtail_ref, gqn_ref, gkn_ref, sink_ref, wg2_ref, bg2_ref, ggla_ref,
     wpool_ref, pscale_ref, hones_ref, vones_ref, bands_ref, tri_ref) = params
    n_seq = x_ref.shape[0]
    one = lambda ref, s: ref.at[pl.ds(s, 1)]
    a_ref = scratch[N_MIX_SCRATCH - 2]
    per_seq = [tuple(ref.at[s] for ref in scratch[:N_MIX_SCRATCH - 2] + scratch[N_MIX_SCRATCH - 1:])
               for s in range(n_seq)]
    pad = ATT_BLOCK if latent else 0
    shift = mod_ref[0, :, 0:D_MODEL]
    scale = mod_ref[0, :, D_MODEL:2 * D_MODEL]

    for s, (z_ref, qr_ref, ks_ref, vs_ref, la_ref, o_ref, st_ref, upad_ref) in enumerate(per_seq):
        if latent:
            for ref in (ks_ref, vs_ref):
                zeros = jnp.zeros(ref.shape[:2] + (pad, ref.shape[3]), _BF16)
                ref[:, :, 0:pad, :] = zeros
                ref[:, :, pad + T:pad + T + pad, :] = zeros
        for r0 in range(0, T, PROJ_TILE):
            rows = slice(r0, r0 + PROJ_TILE)
            x = x_ref[s, rows, :]
            hn = (x * _rms_scale(x) * gn1_ref[...]) * (1.0 + scale) + shift
            hn = hn.astype(_BF16)
            z_ref[rows, 0:C_UC] = _dot(hn, wmain_ref[...])
            z_ref[rows, C_UC:MIX_W] = _dot(hn, wtail_ref[...])
            q = z_ref[rows, C_QA:C_QA + ATT_Q]
            k = z_ref[rows, C_KA:C_KA + ATT_KV]
            q = q * _group_rms_scale(q, hones_ref[...], HEAD_DIM) * gqn_ref[...]
            k = k * _group_rms_scale(k, hones_ref[0:ATT_KV, 0:ATT_KV], HEAD_DIM) * gkn_ref[...]
            v = z_ref[rows, C_VA:C_VA + ATT_KV]
            if latent:
                cos_ref, sin_ref = latent_refs[3], latent_refs[4]
                cos = jnp.concatenate([cos_ref[rows, :]] * (ATT_Q // ATT_KV), axis=-1)
                sin = jnp.concatenate([sin_ref[rows, :]] * (ATT_Q // ATT_KV), axis=-1)
                q = _rope(q, cos, sin)
                k = _rope(k, cos_ref[rows, :], sin_ref[rows, :])
            else:
                out_refs[3][s, rows, :] = k
                out_refs[4][s, rows, :] = v
            qr_ref[rows, :] = (q * (HEAD_DIM ** -0.5)).astype(_BF16)
            _store_split_kv(k, v, ks_ref, vs_ref, slice(pad + r0, pad + r0 + PROJ_TILE))

    for s, (z_ref, qr_ref, ks_ref, vs_ref, la_ref, o_ref, st_ref, upad_ref) in enumerate(per_seq):
        if latent:
            _attention_latent(T, layer, qr_ref, ks_ref, vs_ref, one(latent_refs[0], s), one(latent_refs[1], s),
                              sink_ref, one(out_refs[0], s))
        else:
            _attention_ctx(T, layer, qr_ref, ks_ref, vs_ref, sink_ref, one(out_refs[0], s))

    worst = None
    for s, (z_ref, qr_ref, ks_ref, vs_ref, la_ref, o_ref, st_ref, upad_ref) in enumerate(per_seq):
        if latent:
            st_ref[...] = latent_refs[2][s]
        else:
            st_ref[...] = jnp.zeros(st_ref.shape, _F32)
        span = _gla_prepare(T, z_ref, la_ref, o_ref, wg2_ref, bg2_ref, tri_ref)
        worst = span if worst is None else jnp.maximum(worst, span)
    _gla_run(T, [(z_ref, la_ref, o_ref, st_ref) for z_ref, _, _, _, la_ref, o_ref, st_ref, _ in per_seq],
             worst, a_ref, wg2_ref, bg2_ref, latent)

    for s, (z_ref, qr_ref, ks_ref, vs_ref, la_ref, o_ref, st_ref, upad_ref) in enumerate(per_seq):
        _gla_finish(T, z_ref, o_ref, ggla_ref, vones_ref, one(out_refs[1], s))
        if not latent:
            for d in range(2):
                for pair in range(GLA_HEADS // 2):
                    out_refs[5][s, d, pair] = st_ref[d, pair].T
        _pool(T, z_ref, upad_ref, wpool_ref, pscale_ref, bands_ref, one(out_refs[2], s))


def _merge(x, mod_ref, oa, ob, oc, gn1_ref, wgate_ref, wa_ref, wb_ref, wc_ref, wout_ref):
    mod = lambda i: mod_ref[0, :, i * D_MODEL:(i + 1) * D_MODEL]
    hn = (x * _rms_scale(x) * gn1_ref[...]) * (1.0 + mod(1)) + mod(0)
    gates = jax.nn.sigmoid(_dot(hn.astype(_BF16), wgate_ref[...]))
    mixed = (gates[:, 0:D_MODEL] * _dot(oa, wa_ref[...])
             + gates[:, D_MODEL:2 * D_MODEL] * _dot(ob, wb_ref[...])
             + gates[:, 2 * D_MODEL:3 * D_MODEL] * _dot(oc, wc_ref[...]))
    return x + mod(2) * _dot(mixed.astype(_BF16), wout_ref[...])


def _ffn(x, mod_ref, gn2_ref, wfg_ref, wfu_ref, wfd_ref):
    mod = lambda i: mod_ref[0, :, i * D_MODEL:(i + 1) * D_MODEL]
    hn = ((x * _rms_scale(x) * gn2_ref[...]) * (1.0 + mod(4)) + mod(3)).astype(_BF16)
    h = _silu(_dot(hn, wfg_ref[...])) * _dot(hn, wfu_ref[...])
    return x + mod(5) * _dot(h.astype(_BF16), wfd_ref[...])


def _mix_latent_kernel(T, layer_ref, *refs):
    params, refs = refs[:N_MIX_PARAMS], refs[N_MIX_PARAMS:]
    _mix_body(True, T, layer_ref[0], params, refs[:5], refs[5:8], refs[8:])


def _mix_ctx_kernel(T, layer_ref, *refs):
    params, refs = refs[:N_MIX_PARAMS], refs[N_MIX_PARAMS + 3:]
    _mix_body(False, T, layer_ref[0], params, None, refs[:6], refs[6:])


def _layer_spec(shape):
    zeros = (0,) * len(shape)
    return pl.BlockSpec((None,) + tuple(shape), lambda i, layer: (layer[0],) + zeros,
                        pipeline_mode=pl.Buffered(1))


def _const_spec(blk):
    return pl.BlockSpec(blk, lambda i, layer: (0,) * len(blk), pipeline_mode=pl.Buffered(1))


def _mix_params(x, x_spec, mod_spec, mod_all, pw):
    bands = _pool_bands(x.shape[1])
    specs = [
        x_spec, mod_spec,
        _layer_spec((1, D_MODEL)),
        _layer_spec((D_MODEL, C_UC)),
        _layer_spec((D_MODEL, MIX_W - C_UC)),
        _layer_spec((1, ATT_Q)), _layer_spec((1, ATT_KV)),
        pl.BlockSpec(memory_space=pltpu.SMEM),
        _layer_spec((2, GL_PAD, 2 * GLA_QK)), _layer_spec((1, 2 * GLA_QK)), _layer_spec((1, GLA_VW)),
        _layer_spec((POOL_GROUPS, POOL_GROUP_DIM, POOL_GROUP_DIM)), _layer_spec((1, POOL_W)),
        _const_spec((ATT_Q, ATT_Q)), _const_spec((GLA_VW, GLA_VW)),
        _const_spec(bands.shape), _const_spec((2, GLA_BLOCK, GLA_BLOCK)),
    ]
    args = [x, mod_all, pw["g_norm1"], pw["w_main"], pw["w_tail"], pw["g_qn"], pw["g_kn"], pw["att_sink"], pw["w_gate2"],
            pw["b_gate2"], pw["g_gla_out"], pw["w_pool"], pw["pool_scale"], pw["head_ones"], pw["gla_ones"], bands, _scan_masks()]
    assert len(specs) == len(args) == N_MIX_PARAMS
    return specs, args


def _mix_scratch(S, T, kv_rows):
    scratch = [
        pltpu.VMEM((S, T, MIX_W), _F32),
        pltpu.VMEM((S, T, ATT_Q), _BF16),
        pltpu.VMEM((S, ATT_KV_HEADS, 2, kv_rows, 2 * HEAD_DIM), _BF16),
        pltpu.VMEM((S, ATT_KV_HEADS, 2, kv_rows, 4 * HEAD_DIM), _BF16),
        pltpu.VMEM((S, T, 2 * GLA_QK), _F32),
        pltpu.VMEM((S, T, GLA_VW), _F32),
        pltpu.VMEM((S, 2, GLA_HEADS // 2, GLA_DV, 2 * GLA_DK), _F32),
        pltpu.VMEM((GLA_HEADS + 2, GLA_CHUNK, GLA_QK), _F32),
        pltpu.VMEM((S, T + 2 * _pool_halo(T), POOL_W), _BF16),
    ]
    assert len(scratch) == N_MIX_SCRATCH
    return scratch


def _mix_latent_call(layer, x, mod_all, pw, cache_k, cache_v, st0, cos, sin):
    B, T, _ = x.shape
    per_seq = lambda blk: pl.BlockSpec(blk, lambda b, layer: (b,) + (0,) * (len(blk) - 1))
    mod_spec = pl.BlockSpec((None, 1, 1, 6 * D_MODEL), lambda b, layer: (layer[0], b + 1, 0, 0))
    in_specs, args = _mix_params(x, per_seq((1, T, D_MODEL)), mod_spec, mod_all, pw)
    P = cache_k.shape[2]
    cache_spec = pl.BlockSpec((1, 1, P, ATT_KV), lambda b, layer: (b, layer[0], 0, 0))
    in_specs += [cache_spec, cache_spec,
                 pl.BlockSpec((1, None, 2, GLA_HEADS // 2, GLA_DV, 2 * GLA_DK),
                              lambda b, layer: (b, layer[0], 0, 0, 0, 0)),
                 _const_spec((T, ATT_KV)), _const_spec((T, ATT_KV))]
    args += [cache_k, cache_v, st0, cos, sin]
    widths = (ATT_Q, GLA_VW, POOL_W)
    return pl.pallas_call(
        functools.partial(_mix_latent_kernel, T),
        grid_spec=pltpu.PrefetchScalarGridSpec(
            num_scalar_prefetch=1, grid=(B,), in_specs=in_specs,
            out_specs=[per_seq((1, T, w)) for w in widths],
            scratch_shapes=_mix_scratch(1, T, T + 2 * ATT_BLOCK)),
        out_shape=[jax.ShapeDtypeStruct((B, T, w), _BF16) for w in widths],
        compiler_params=pltpu.CompilerParams(dimension_semantics=("arbitrary",), vmem_limit_bytes=VMEM_LIMIT),
        name="mix_latent",
    )(layer, *args)


def _mix_ctx_call(layer, x, mod_all, pw, stacked):
    B, T, _ = x.shape
    S = CTX_SEQS_PER_STEP
    assert B % S == 0
    per_seq = lambda blk: pl.BlockSpec(blk, lambda b, layer: (b,) + (0,) * (len(blk) - 1))
    mod_spec = pl.BlockSpec((None, 1, 1, 6 * D_MODEL), lambda b, layer: (layer[0], 0, 0, 0))
    in_specs, args = _mix_params(x, per_seq((S, T, D_MODEL)), mod_spec, mod_all, pw)
    widths = (ATT_Q, GLA_VW, POOL_W)
    n_in = 1 + len(args)
    aliases = {n_in + j: len(widths) + j for j in range(len(stacked))}
    in_specs += [pl.BlockSpec(memory_space=pl.ANY)] * len(stacked)
    args += list(stacked)
    at_layer = lambda blk: pl.BlockSpec((S, None) + blk, lambda b, layer: (b, layer[0]) + (0,) * len(blk))
    out_specs = [per_seq((S, T, w)) for w in widths] + [
        at_layer((T, ATT_KV)), at_layer((T, ATT_KV)), at_layer((2, GLA_HEADS // 2, 2 * GLA_DK, GLA_DV))]
    out_shape = ([jax.ShapeDtypeStruct((B, T, w), _BF16) for w in widths]
                 + [jax.ShapeDtypeStruct(a.shape, a.dtype) for a in stacked])
    return pl.pallas_call(
        functools.partial(_mix_ctx_kernel, T),
        grid_spec=pltpu.PrefetchScalarGridSpec(
            num_scalar_prefetch=1, grid=(B // S,), in_specs=in_specs, out_specs=out_specs,
            scratch_shapes=_mix_scratch(S, T, T)),
        out_shape=out_shape,
        input_output_aliases=aliases,
        compiler_params=pltpu.CompilerParams(dimension_semantics=("arbitrary",), vmem_limit_bytes=VMEM_LIMIT),
        name="mix_ctx",
    )(layer, *args)


def _post_kernel(layer_ref, x_ref, mod_ref, oa_ref, ob_ref, oc_ref, gn1_ref, gn2_ref, wgate_ref, wa_ref, wb_ref,
                 wc_ref, wout_ref, wfg_ref, wfu_ref, wfd_ref, out_ref):
    x = _merge(x_ref[...], mod_ref, oa_ref[...], ob_ref[...], oc_ref[...], gn1_ref, wgate_ref, wa_ref, wb_ref,
               wc_ref, wout_ref)
    out_ref[...] = _ffn(x, mod_ref, gn2_ref, wfg_ref, wfu_ref, wfd_ref)


def _post_call(layer, x2d, mod_all, oa, ob, oc, pw, tiles_per_seq):
    row = lambda w: pl.BlockSpec((POST_TILE, w), lambda i, layer: (i, 0))
    if tiles_per_seq is None:
        mod_spec = pl.BlockSpec((None, 1, 1, 6 * D_MODEL), lambda i, layer: (layer[0], 0, 0, 0))
    else:
        mod_spec = pl.BlockSpec((None, 1, 1, 6 * D_MODEL),
                                lambda i, layer: (layer[0], 1 + i // tiles_per_seq, 0, 0))
    weights = [(pw["g_norm1"], (1, D_MODEL)), (pw["g_norm2"], (1, D_MODEL)), (pw["w_gates"], (D_MODEL, GATE_W)),
               (pw["w_br_a"], (ATT_Q, D_MODEL)), (pw["w_br_b"], (GLA_VW, D_MODEL)),
               (pw["w_br_c"], (POOL_W, D_MODEL)), (pw["w_out"], (D_MODEL, D_MODEL)),
               (pw["w_ff_gate"], (D_MODEL, D_FF)), (pw["w_ff_up"], (D_MODEL, D_FF)),
               (pw["w_ff_down"], (D_FF, D_MODEL))]
    in_specs = ([row(D_MODEL), mod_spec, row(ATT_Q), row(GLA_VW), row(POOL_W)]
                + [_layer_spec(shape) for _, shape in weights])
    return pl.pallas_call(
        _post_kernel,
        grid_spec=pltpu.PrefetchScalarGridSpec(
            num_scalar_prefetch=1, grid=(x2d.shape[0] // POST_TILE,), in_specs=in_specs, out_specs=row(D_MODEL)),
        out_shape=jax.ShapeDtypeStruct(x2d.shape, _F32),
        input_output_aliases={1: 0},
        compiler_params=pltpu.CompilerParams(dimension_semantics=("arbitrary",), vmem_limit_bytes=VMEM_LIMIT),
        name="post",
    )(layer, x2d, mod_all, oa, ob, oc, *[a for a, _ in weights])


def _rope_tables(T):
    quarter = HEAD_DIM // 4
    inv_freq = ROPE_BASE ** (-np.arange(quarter, dtype=np.float32) / quarter)
    pos = np.arange(T)
    ang_row = (pos // GRID_W).astype(np.float32)[:, None] * inv_freq[None, :]
    ang_col = (pos % GRID_W).astype(np.float32)[:, None] * inv_freq[None, :]
    cos = np.concatenate([np.cos(ang_row)] * 2 + [np.cos(ang_col)] * 2, axis=-1)
    sin = np.concatenate([-np.sin(ang_row), np.sin(ang_row), -np.sin(ang_col), np.sin(ang_col)], axis=-1)
    return (jnp.asarray(np.tile(cos, (1, ATT_KV_HEADS)), _F32), jnp.asarray(np.tile(sin, (1, ATT_KV_HEADS)), _F32))


def _prepare_weights(w_in, g_qn, g_kn, att_sink, w_gate2, b_gate2, g_gla_out, w_pool, pool_scale, w_br_a,
                     w_br_b, w_br_c, w_out, g_norm1, g_norm2, w_ff_gate, w_ff_up, w_ff_down):
    o_gl = ATT_Q + 2 * ATT_KV + 2 * GLA_QK + 2 * GLA_VW
    o_uc = o_gl + 2 * GLA_RANK
    o_gate = o_uc + POOL_W
    assert o_gl == C_UC
    w_main = w_in[:, :, :o_gl].astype(_BF16)
    w_tail = jnp.concatenate(
        [w_in[:, :, o_uc:o_gate], w_in[:, :, o_gl:o_uc],
         jnp.zeros((DEPTH, D_MODEL, GL_PAD - 2 * GLA_RANK), w_in.dtype)], axis=2).astype(_BF16)
    w_gates = w_in[:, :, o_gate:].astype(_BF16)
    wg2 = jnp.zeros((DEPTH, GL_PAD, 2 * GLA_QK), _F32)
    wg2 = wg2.at[:, 0:GLA_RANK, 0:GLA_QK].set(w_gate2[:, 0])
    wg2 = wg2.at[:, GLA_RANK:2 * GLA_RANK, GLA_QK:].set(w_gate2[:, 1])
    wg2_hi = wg2.astype(_BF16)
    wg2 = jnp.stack([wg2_hi, (wg2 - wg2_hi.astype(_F32)).astype(_BF16)], axis=1)
    vec = lambda a: a.reshape(DEPTH, 1, -1)
    group_ones = lambda n, width: jnp.asarray(
        (np.arange(n)[:, None] // width) == (np.arange(n)[None, :] // width), _BF16)
    return {
        "head_ones": group_ones(ATT_Q, HEAD_DIM),
        "gla_ones": group_ones(GLA_VW, GLA_DV),
        "w_main": w_main,
        "w_tail": w_tail,
        "w_gates": w_gates,
        "g_qn": vec(jnp.tile(g_qn, (1, ATT_HEADS))),
        "g_kn": vec(jnp.tile(g_kn, (1, ATT_KV_HEADS))),
        "att_sink": att_sink,
        "w_gate2": wg2,
        "b_gate2": vec(b_gate2),
        "g_gla_out": vec(jnp.tile(g_gla_out, (1, GLA_HEADS))),
        "w_pool": w_pool.astype(_BF16),
        "pool_scale": vec(pool_scale),
        "w_br_a": w_br_a.astype(_BF16),
        "w_br_b": w_br_b.astype(_BF16),
        "w_br_c": w_br_c.astype(_BF16),
        "w_out": w_out.astype(_BF16),
        "g_norm1": vec(g_norm1),
        "g_norm2": vec(g_norm2),
        "w_ff_gate": w_ff_gate.astype(_BF16),
        "w_ff_up": w_ff_up.astype(_BF16),
        "w_ff_down": w_ff_down.astype(_BF16),
    }


def kernel(x_prompt, x_sample, c, cache_k, cache_v, state_gla, c_ctx, w_in, g_qn, g_kn, att_sink, w_gate2,
           b_gate2, g_gla_out, w_pool, pool_scale, w_br_a, w_br_b, w_br_c, w_out, g_norm1, g_norm2, w_mod,
           b_mod, w_ff_gate, w_ff_up, w_ff_down):
    B, T, _ = x_prompt.shape
    BL, TL, _ = x_sample.shape
    assert (B * T) % POST_TILE == 0 and TL % POST_TILE == 0 and BL + 1 <= MOD_ROWS
    assert T % PROJ_TILE == 0 and TL % PROJ_TILE == 0
    cv = jnp.concatenate([c_ctx[None, :], c, jnp.zeros((MOD_ROWS - 1 - BL, D_MODEL), _F32)], axis=0)
    mod_all = _modulation(cv, w_mod, b_mod).reshape(DEPTH, MOD_ROWS, 1, 6 * D_MODEL)
    pw = _prepare_weights(w_in, g_qn, g_kn, att_sink, w_gate2, b_gate2, g_gla_out, w_pool, pool_scale, w_br_a,
                          w_br_b, w_br_c, w_out, g_norm1, g_norm2, w_ff_gate, w_ff_up, w_ff_down)
    cos, sin = _rope_tables(TL)
    P = cache_k.shape[2]
    latent_ctx = (cache_k.reshape(BL, DEPTH, P, ATT_KV), cache_v.reshape(BL, DEPTH, P, ATT_KV),
                  jnp.swapaxes(state_gla.reshape(BL, DEPTH, 2, GLA_HEADS // 2, 2 * GLA_DK, GLA_DV), -1, -2),
                  cos, sin)

    def layer_step(l, carry):
        yp, ys, new_k, new_v, new_st = carry
        layer = jnp.full((1,), l, jnp.int32)
        oa, ob, oc, new_k, new_v, new_st = _mix_ctx_call(layer, yp, mod_all, pw, (new_k, new_v, new_st))
        yp = _post_call(layer, yp.reshape(B * T, D_MODEL), mod_all, oa.reshape(B * T, -1), ob.reshape(B * T, -1),
                        oc.reshape(B * T, -1), pw, None).reshape(B, T, D_MODEL)
        oa, ob, oc = _mix_latent_call(layer, ys, mod_all, pw, *latent_ctx)
        ys = _post_call(layer, ys.reshape(BL * TL, D_MODEL), mod_all, oa.reshape(BL * TL, -1),
                        ob.reshape(BL * TL, -1), oc.reshape(BL * TL, -1), pw,
                        TL // POST_TILE).reshape(BL, TL, D_MODEL)
        return yp, ys, new_k, new_v, new_st

    init = (x_prompt, x_sample,
            jnp.zeros((B, DEPTH, T, ATT_KV), _F32), jnp.zeros((B, DEPTH, T, ATT_KV), _F32),
            jnp.zeros((B, DEPTH, 2, GLA_HEADS // 2, 2 * GLA_DK, GLA_DV), _F32))
    yp, ys, new_k, new_v, new_st = lax.fori_loop(0, DEPTH, layer_step, init)
    return (yp, ys, new_k.reshape(B, DEPTH, T, ATT_KV_HEADS, HEAD_DIM),
            new_v.reshape(B, DEPTH, T, ATT_KV_HEADS, HEAD_DIM),
            new_st.reshape(B, DEPTH, 2, GLA_HEADS, GLA_DK, GLA_DV))
```

```python
import functools

import jax
import jax.numpy as jnp
import numpy as np
from jax import lax
from jax.experimental import pallas as pl
from jax.experimental.pallas import tpu as pltpu

D_MODEL = 1024
DEPTH = 4
GRID_W = 64
ATT_HEADS = 8
ATT_KV_HEADS = 2
ATT_GROUP = ATT_HEADS // ATT_KV_HEADS
HEAD_DIM = 64
WINDOW = 128
ATT_BLOCK = 128
ROPE_BASE = 10000.0
GLA_HEADS = 4
GLA_DK = 64
GLA_DV = 128
GLA_RANK = 16
GLA_TAU = 16.0
GLA_CHUNK = 64
POOL_GROUPS = 4
POOL_GROUP_DIM = 128
POOL_WINDOWS = (2, 4, 8, 16)
D_FF = 2816
ATT_Q = ATT_HEADS * HEAD_DIM
ATT_KV = ATT_KV_HEADS * HEAD_DIM
GLA_QK = GLA_HEADS * GLA_DK
GLA_VW = GLA_HEADS * GLA_DV
POOL_W = POOL_GROUPS * POOL_GROUP_DIM
EPS = 1e-6
NEG = -1e30

C_QA = 0
C_KA = C_QA + ATT_Q
C_VA = C_KA + ATT_KV
C_QB = C_VA + ATT_KV
C_KB = C_QB + GLA_QK
C_VB = C_KB + GLA_QK
C_RB = C_VB + GLA_VW
C_UC = C_RB + GLA_VW
C_GL = C_UC + POOL_W
LANES = 128
SUBLANES = 8
GL_PAD = LANES
MIX_W = C_GL + GL_PAD
GATE_W = 3 * D_MODEL

POST_TILE = 512
PROJ_TILE = 256
CTX_SEQS_PER_STEP = 2
POOL_TILE = 256
POOL_HALO = LANES
MOD_ROWS = SUBLANES
MOD_TILE = 3072
GLA_BLOCK = 256
GLA_MAX_EXPONENT = 80.0
VMEM_LIMIT = 56 * 1024 * 1024

_F32 = jnp.float32
_BF16 = jnp.bfloat16


def _dot(a, b):
    return jnp.dot(a, b, preferred_element_type=_F32)


def _dot_nt(a, b):
    return lax.dot_general(a, b, (((1,), (1,)), ((), ())), preferred_element_type=_F32)


def _dot_tn(a, b):
    return lax.dot_general(a, b, (((0,), (0,)), ((), ())), preferred_element_type=_F32)


def _rms_scale(x):
    return lax.rsqrt(jnp.mean(x * x, axis=-1, keepdims=True) + EPS)


def _group_rms_scale(x, group_ones, width):
    return lax.rsqrt(_dot((x * x).astype(_BF16), group_ones) * (1.0 / width) + EPS)


def _log_sigmoid(x):
    return jnp.minimum(x, 0.0) - jnp.log(1.0 + jnp.exp(-jnp.abs(x)))


def _silu(x):
    return x * jax.nn.sigmoid(x)


def _rope(x, cos, sin_signed):
    n = x.shape[-1]
    lane = lax.broadcasted_iota(jnp.int32, x.shape, 1)
    up = pltpu.roll(x, n - HEAD_DIM // 4, axis=1)
    down = pltpu.roll(x, HEAD_DIM // 4, axis=1)
    partner = jnp.where((lane & (HEAD_DIM // 2 - 1)) < HEAD_DIM // 4, up, down)
    return x * cos + partner * sin_signed


def _mod_kernel(cv_ref, w_ref, b_ref, out_ref):
    s = _silu(cv_ref[...]).astype(_BF16)
    out_ref[0] = _dot(s, w_ref[0].astype(_BF16)) + b_ref[0]


def _modulation(cv, w_mod, b_mod):
    n_col = (6 * D_MODEL) // MOD_TILE
    return pl.pallas_call(
        _mod_kernel,
        grid=(DEPTH, n_col),
        in_specs=[
            pl.BlockSpec((MOD_ROWS, D_MODEL), lambda l, j: (0, 0)),
            pl.BlockSpec((1, D_MODEL, MOD_TILE), lambda l, j: (l, 0, j)),
            pl.BlockSpec((1, 1, MOD_TILE), lambda l, j: (l, 0, j)),
        ],
        out_specs=pl.BlockSpec((1, MOD_ROWS, MOD_TILE), lambda l, j: (l, 0, j)),
        out_shape=jax.ShapeDtypeStruct((DEPTH, MOD_ROWS, 6 * D_MODEL), _F32),
        compiler_params=pltpu.CompilerParams(vmem_limit_bytes=VMEM_LIMIT),
        name="modulation",
    )(cv, w_mod, b_mod.reshape(DEPTH, 1, 6 * D_MODEL))


def _split_heads(x):
    low = lax.broadcasted_iota(jnp.int32, x.shape, 1) < HEAD_DIM
    swapped = pltpu.roll(x, HEAD_DIM, axis=1)
    zero = jnp.zeros_like(x)
    return ((jnp.where(low, x, zero), jnp.where(low, zero, swapped)),
            (jnp.where(low, swapped, zero), jnp.where(low, zero, x)))


def _store_split_kv(k, v, ks_ref, vs_ref, rows):
    ones = jnp.ones_like(v)
    for kv, (k_sides, v_sides, one_sides) in enumerate(zip(_split_heads(k), _split_heads(v), _split_heads(ones))):
        for side in range(2):
            ks_ref[kv, side, rows, :] = k_sides[side].astype(_BF16)
            vs_ref[kv, side, rows, :] = jnp.concatenate([v_sides[side], one_sides[side]], axis=-1).astype(_BF16)


def _pair_softmax_av(qp, keys, values, masks, sink_even, sink_odd):
    m = qp.shape[0]
    scores = []
    for (k_left, k_right), mask in zip(keys, masks):
        s_even, s_odd = _dot_nt(qp, k_left), _dot_nt(qp, k_right)
        if mask is not None:
            s_even, s_odd = jnp.where(mask, s_even, NEG), jnp.where(mask, s_odd, NEG)
        scores.append((s_even, s_odd))
    m_even = jnp.full((m, 1), sink_even, _F32)
    m_odd = jnp.full((m, 1), sink_odd, _F32)
    for s_even, s_odd in scores:
        m_even = jnp.maximum(m_even, jnp.max(s_even, axis=-1, keepdims=True))
        m_odd = jnp.maximum(m_odd, jnp.max(s_odd, axis=-1, keepdims=True))
    res = None
    for (s_even, s_odd), (w_left, w_right) in zip(scores, values):
        r = (_dot(jnp.exp(s_even - m_even).astype(_BF16), w_left)
             + _dot(jnp.exp(s_odd - m_odd).astype(_BF16), w_right))
        res = r if res is None else res + r
    pair = 2 * HEAD_DIM
    low = lax.broadcasted_iota(jnp.int32, (m, pair), 1) < HEAD_DIM
    den = res[:, pair:] + jnp.where(low, jnp.exp(sink_even - m_even), jnp.exp(sink_odd - m_odd))
    return res[:, :pair] / den


def _attention_ctx(T, layer, qr_ref, ks_ref, vs_ref, sink_ref, oa_ref):
    pair = 2 * HEAD_DIM
    for kv in range(ATT_KV_HEADS):
        keys = [(ks_ref[kv, 0], ks_ref[kv, 1])]
        values = [(vs_ref[kv, 0], vs_ref[kv, 1])]
        for j in range(ATT_GROUP // 2):
            head = kv * ATT_GROUP + 2 * j
            cols = slice(head * HEAD_DIM, head * HEAD_DIM + pair)
            o = _pair_softmax_av(qr_ref[:, cols], keys, values, [None], sink_ref[layer, head],
                                 sink_ref[layer, head + 1])
            oa_ref[0, :, cols] = o.astype(_BF16)


def _attention_latent(T, layer, qr_ref, ks_ref, vs_ref, kc_ref, vc_ref, sink_ref, oa_ref):
    pair = 2 * HEAD_DIM
    span = 3 * ATT_BLOCK
    kc, vc = kc_ref[0, 0], vc_ref[0, 0]
    ones = jnp.ones_like(vc)
    ctx_keys = [tuple(side.astype(_BF16) for side in sides) for sides in _split_heads(kc)]
    ctx_values = [tuple(jnp.concatenate([v_side, one_side], axis=-1).astype(_BF16)
                        for v_side, one_side in zip(v_sides, one_sides))
                  for v_sides, one_sides in zip(_split_heads(vc), _split_heads(ones))]

    def block(i, carry):
        q_rows = pl.ds(pl.multiple_of(i * ATT_BLOCK, ATT_BLOCK), ATT_BLOCK)
        k_rows = pl.ds(pl.multiple_of(i * ATT_BLOCK, ATT_BLOCK), span)
        q_pos = i * ATT_BLOCK + lax.broadcasted_iota(jnp.int32, (ATT_BLOCK, span), 0)
        k_pos = (i - 1) * ATT_BLOCK + lax.broadcasted_iota(jnp.int32, (ATT_BLOCK, span), 1)
        valid = (jnp.abs(k_pos - q_pos) <= WINDOW) & (k_pos >= 0) & (k_pos < T)
        for kv in range(ATT_KV_HEADS):
            keys = [(ks_ref[kv, 0, k_rows, :], ks_ref[kv, 1, k_rows, :]), ctx_keys[kv]]
            values = [(vs_ref[kv, 0, k_rows, :], vs_ref[kv, 1, k_rows, :]), ctx_values[kv]]
            for j in range(ATT_GROUP // 2):
                head = kv * ATT_GROUP + 2 * j
                cols = slice(head * HEAD_DIM, head * HEAD_DIM + pair)
                o = _pair_softmax_av(qr_ref[q_rows, cols], keys, values, [valid, None], sink_ref[layer, head],
                                     sink_ref[layer, head + 1])
                oa_ref[0, q_rows, cols] = o.astype(_BF16)
        return carry

    lax.fori_loop(0, T // ATT_BLOCK, block, 0, unroll=2)


def _gla_chunk(z_ref, la_ref, o_ref, st_ref, a_ref, direction, start):
    C = GLA_CHUNK
    rows = pl.ds(pl.multiple_of(start, C), C)
    q = z_ref[rows, C_QB:C_QB + GLA_QK] * (GLA_DK ** -0.5)
    k = z_ref[rows, C_KB:C_KB + GLA_QK]
    v = z_ref[rows, C_VB:C_VB + GLA_VW].astype(_BF16)
    la = la_ref[rows, direction * GLA_QK:(direction + 1) * GLA_QK]
    la_hi = la.astype(_BF16)
    la_lo = (la - la_hi.astype(_F32)).astype(_BF16)
    ti = lax.broadcasted_iota(jnp.int32, (C, C), 0)
    si = lax.broadcasted_iota(jnp.int32, (C, C), 1)
    causal = (si <= ti) if direction == 0 else (si >= ti)
    tri = jnp.where(causal, 1.0, 0.0).astype(_BF16)
    b = _dot(tri, la_hi) + _dot(tri, la_lo)
    end = C - 1 if direction == 0 else 0
    b_end = b[end:end + 1]
    q_in = (q * jnp.exp(b)).astype(_BF16)
    k_out = (k * jnp.exp(b_end - b)).astype(_BF16)
    e_end = jnp.exp(b_end)
    ones = jnp.ones((8, GLA_DK), _BF16)

    def row_group(g, carry):
        base = pl.multiple_of(g * 8, 8)
        b8 = a_ref[GLA_HEADS, pl.ds(base, 8), :]
        q8 = a_ref[GLA_HEADS + 1, pl.ds(base, 8), :]
        s_idx = lax.broadcasted_iota(jnp.int32, (C, 1), 0)
        rows_h = [[] for _ in range(GLA_HEADS)]
        for j in range(8):
            ok = (s_idx <= base + j) if direction == 0 else (s_idx >= base + j)
            decay = jnp.exp(jnp.where(ok, b8[j:j + 1] - b, NEG))
            p = (q8[j:j + 1] * k * decay).astype(_BF16)
            for h in range(GLA_HEADS):
                rows_h[h].append(_dot_nt(ones, p[:, h * GLA_DK:(h + 1) * GLA_DK])[0:1])
        for h in range(GLA_HEADS):
            a_ref[h, pl.ds(base, 8), 0:C] = jnp.concatenate(rows_h[h], axis=0)
        return carry

    a_ref[GLA_HEADS] = b
    a_ref[GLA_HEADS + 1] = q
    lax.fori_loop(0, C // 8, row_group, 0)
    outs = []
    for pair in range(GLA_HEADS // 2):
        s_pair = st_ref[direction, pair]
        s_next = []
        for h in (2 * pair, 2 * pair + 1):
            kc = slice(h * GLA_DK, (h + 1) * GLA_DK)
            vc = slice(h * GLA_DV, (h + 1) * GLA_DV)
            s_t = s_pair[:, (h % 2) * GLA_DK:(h % 2 + 1) * GLA_DK]
            outs.append(_dot_nt(q_in[:, kc], s_t.astype(_BF16)) + _dot(a_ref[h, :, 0:C].astype(_BF16), v[:, vc]))
            s_next.append(s_t * e_end[:, kc] + _dot_tn(v[:, vc], k_out[:, kc]))
        st_ref[direction, pair] = jnp.concatenate(s_next, axis=-1)
    o_ref[rows, :] += jnp.concatenate(outs, axis=-1)


def _gla_block(z_ref, la_ref, o_ref, st_ref, direction, start, use_state):
    NB = GLA_BLOCK
    fwd = direction == 0
    rows = slice(start, start + NB) if isinstance(start, int) else pl.ds(pl.multiple_of(start, NB), NB)
    q = z_ref[rows, C_QB:C_QB + GLA_QK] * (GLA_DK ** -0.5)
    k = z_ref[rows, C_KB:C_KB + GLA_QK]
    v = z_ref[rows, C_VB:C_VB + GLA_VW].astype(_BF16)
    b = la_ref[rows, direction * GLA_QK:(direction + 1) * GLA_QK]
    end, mid = (NB - 1, NB // 2 - 1) if fwd else (0, NB // 2)
    b_end = b[end:end + 1]
    c = b - b[mid:mid + 1]
    q_c = (q * jnp.exp(c)).astype(_BF16)
    k_c = (k * jnp.exp(-c)).astype(_BF16)
    k_fin = (k * jnp.exp(b_end - b)).astype(_BF16)
    if use_state:
        q_in = (q * jnp.exp(b)).astype(_BF16)
        e_all = jnp.exp(b_end)
    ti = lax.broadcasted_iota(jnp.int32, (NB, NB), 0)
    si = lax.broadcasted_iota(jnp.int32, (NB, NB), 1)
    causal = (si <= ti) if fwd else (si >= ti)
    tile = 2 * GLA_DK
    low = lax.broadcasted_iota(jnp.int32, (NB, tile), 1) < GLA_DK
    zero = jnp.zeros((NB, tile), _BF16)
    pick = lambda x, parity: jnp.where(low, x, zero) if parity == 0 else jnp.where(low, zero, x)
    outs = []
    for pair in range(GLA_HEADS // 2):
        lanes = slice(pair * tile, (pair + 1) * tile)
        if use_state:
            s_pair = st_ref[direction, pair]
            s_bf = s_pair.astype(_BF16)
        s_new = None
        for parity in range(2):
            h = 2 * pair + parity
            vc = slice(h * GLA_DV, (h + 1) * GLA_DV)
            a = jnp.where(causal, _dot_nt(q_c[:, lanes], pick(k_c[:, lanes], parity)), 0.0)
            o_h = _dot(a.astype(_BF16), v[:, vc])
            if use_state:
                o_h = o_h + _dot_nt(pick(q_in[:, lanes], parity), s_bf)
            outs.append(o_h)
            upd = _dot_tn(v[:, vc], pick(k_fin[:, lanes], parity))
            s_new = upd if s_new is None else s_new + upd
        if use_state:
            s_new = s_new + s_pair * e_all[:, lanes]
        st_ref[direction, pair] = s_new
    o_ref[rows, :] += jnp.concatenate(outs, axis=-1)


def _log_decay(z_ref, rows, wg2_ref, bg2_ref):
    x = z_ref[rows, C_GL:C_GL + GL_PAD]
    x_hi = x.astype(_BF16)
    x_lo = (x - x_hi.astype(_F32)).astype(_BF16)
    w_hi, w_lo = wg2_ref[0], wg2_ref[1]
    pre = _dot(x_hi, w_hi) + (_dot(x_lo, w_hi) + _dot(x_hi, w_lo)) + bg2_ref[...]
    return _log_sigmoid(pre) * (1.0 / GLA_TAU)


def _gla_prepare(T, z_ref, la_ref, o_ref, wg2_ref, bg2_ref, tri_ref):
    NB = GLA_BLOCK
    worst = None
    for r0 in range(0, T, NB):
        rows = slice(r0, r0 + NB)
        la = _log_decay(z_ref, rows, wg2_ref, bg2_ref)
        la_hi = la.astype(_BF16)
        la_lo = (la - la_hi.astype(_F32)).astype(_BF16)
        for d in range(2):
            cols = slice(d * GLA_QK, (d + 1) * GLA_QK)
            b = _dot(tri_ref[d], la_hi[:, cols]) + _dot(tri_ref[d], la_lo[:, cols])
            la_ref[rows, cols] = b
            first, mid, last = (0, NB // 2 - 1, NB - 1) if d == 0 else (NB - 1, NB // 2, 0)
            span = jnp.max(jnp.maximum(b[first:first + 1] - b[mid:mid + 1], b[mid:mid + 1] - b[last:last + 1]))
            worst = span if worst is None else jnp.maximum(worst, span)
    o_ref[...] = jnp.zeros(o_ref.shape, _F32)
    return worst


def _gla_run(T, seqs, worst, a_ref, wg2_ref, bg2_ref, use_state):
    NB = GLA_BLOCK
    n_chunks = T // GLA_CHUNK
    n_blocks = T // NB
    fast_ok = worst <= GLA_MAX_EXPONENT

    @pl.when(fast_ok)
    def _():
        if n_blocks == 1:
            for direction in range(2):
                for z_ref, la_ref, o_ref, st_ref in seqs:
                    _gla_block(z_ref, la_ref, o_ref, st_ref, direction, 0, use_state)
        else:
            def body(i, carry):
                for z_ref, la_ref, o_ref, st_ref in seqs:
                    _gla_block(z_ref, la_ref, o_ref, st_ref, 0, i * NB, True)
                    _gla_block(z_ref, la_ref, o_ref, st_ref, 1, (n_blocks - 1 - i) * NB, True)
                return carry
            lax.fori_loop(0, n_blocks, body, 0, unroll=2)

    @pl.when(jnp.logical_not(fast_ok))
    def _():
        for z_ref, la_ref, o_ref, st_ref in seqs:
            for r0 in range(0, T, NB):
                la_ref[r0:r0 + NB, :] = _log_decay(z_ref, slice(r0, r0 + NB), wg2_ref, bg2_ref)

            def body(i, carry):
                _gla_chunk(z_ref, la_ref, o_ref, st_ref, a_ref, 0, i * GLA_CHUNK)
                _gla_chunk(z_ref, la_ref, o_ref, st_ref, a_ref, 1, (n_chunks - 1 - i) * GLA_CHUNK)
                return carry
            lax.fori_loop(0, n_chunks, body, 0)


def _gla_finish(T, z_ref, o_ref, ggla_ref, vones_ref, ob_ref):
    for r0 in range(0, T, PROJ_TILE):
        rows = slice(r0, r0 + PROJ_TILE)
        o = o_ref[rows, :]
        y = o * _group_rms_scale(o, vones_ref[...], GLA_DV) * ggla_ref[...]
        ob_ref[0, rows, :] = (y * _silu(z_ref[rows, C_RB:C_RB + GLA_VW])).astype(_BF16)


def _pool_halo(T):
    return POOL_HALO if T > POOL_TILE else 0


def _pool_bands(T):
    halo = _pool_halo(T)
    off = np.arange(POOL_TILE + 2 * halo)[None, :] - halo - np.arange(POOL_TILE)[:, None]
    return jnp.asarray(np.stack([(off >= -(w // 2)) & (off < w - w // 2) for w in POOL_WINDOWS]), _BF16)


def _scan_masks():
    t = np.arange(GLA_BLOCK)
    return jnp.asarray(np.stack([t[None, :] <= t[:, None], t[None, :] >= t[:, None]]), _BF16)


def _pool_inverse_counts(T):
    t = np.arange(T)
    cols = [np.repeat((1.0 / (np.minimum(t - w // 2 + w, T) - np.maximum(t - w // 2, 0)))[:, None],
                      POOL_GROUP_DIM, axis=1) for w in POOL_WINDOWS]
    return jnp.asarray(np.concatenate(cols, axis=1), _F32)


def _pool(T, z_ref, upad_ref, wpool_ref, pscale_ref, bands_ref, icnt_ref, oc_ref):
    halo = _pool_halo(T)
    if halo:
        zeros = jnp.zeros((halo, POOL_W), _BF16)
        upad_ref[0:halo, :] = zeros
        upad_ref[halo + T:halo + T + halo, :] = zeros
    upad_ref[halo:halo + T, :] = z_ref[:, C_UC:C_UC + POOL_W].astype(_BF16)
    span = POOL_TILE + 2 * halo
    group = lambda g: slice(g * POOL_GROUP_DIM, (g + 1) * POOL_GROUP_DIM)
    for jb in range(T // POOL_TILE):
        rows = slice(jb * POOL_TILE, (jb + 1) * POOL_TILE)
        win = slice(jb * POOL_TILE, jb * POOL_TILE + span)
        total = jnp.concatenate([_dot(bands_ref[g], upad_ref[win, group(g)]) for g in range(POOL_GROUPS)], axis=-1)
        pooled = (total * icnt_ref[rows, :] - z_ref[rows, C_UC:C_UC + POOL_W]).astype(_BF16)
        y = jnp.concatenate([_dot(pooled[:, group(g)], wpool_ref[g]) for g in range(POOL_GROUPS)], axis=-1)
        oc_ref[0, rows, :] = (y * pscale_ref[...]).astype(_BF16)


N_MIX_PARAMS = 18
N_MIX_SCRATCH = 9


def _mix_body(latent, T, layer, params, latent_refs, out_refs, scratch):
    (x_ref, mod_ref, gn1_ref, wmain_ref, wtail_ref, gqn_ref, gkn_ref, sink_ref, wg2_ref, bg2_ref, ggla_ref,
     wpool_ref, pscale_ref, hones_ref, vones_ref, bands_ref, icnt_ref, tri_ref) = params
    n_seq = x_ref.shape[0]
    one = lambda ref, s: ref.at[pl.ds(s, 1)]
    a_ref = scratch[N_MIX_SCRATCH - 2]
    per_seq = [tuple(ref.at[s] for ref in scratch[:N_MIX_SCRATCH - 2] + scratch[N_MIX_SCRATCH - 1:])
               for s in range(n_seq)]
    pad = ATT_BLOCK if latent else 0
    shift = mod_ref[0, :, 0:D_MODEL]
    scale = mod_ref[0, :, D_MODEL:2 * D_MODEL]

    for s, (z_ref, qr_ref, ks_ref, vs_ref, la_ref, o_ref, st_ref, upad_ref) in enumerate(per_seq):
        if latent:
            for ref in (ks_ref, vs_ref):
                zeros = jnp.zeros(ref.shape[:2] + (pad, ref.shape[3]), _BF16)
                ref[:, :, 0:pad, :] = zeros
                ref[:, :, pad + T:pad + T + pad, :] = zeros
        for r0 in range(0, T, PROJ_TILE):
            rows = slice(r0, r0 + PROJ_TILE)
            x = x_ref[s, rows, :]
            hn = (x * _rms_scale(x) * gn1_ref[...]) * (1.0 + scale) + shift
            hn = hn.astype(_BF16)
            z_ref[rows, 0:C_UC] = _dot(hn, wmain_ref[...])
            z_ref[rows, C_UC:MIX_W] = _dot(hn, wtail_ref[...])
            q = z_ref[rows, C_QA:C_QA + ATT_Q]
            k = z_ref[rows, C_KA:C_KA + ATT_KV]
            q = q * _group_rms_scale(q, hones_ref[...], HEAD_DIM) * gqn_ref[...]
            k = k * _group_rms_scale(k, hones_ref[0:ATT_KV, 0:ATT_KV], HEAD_DIM) * gkn_ref[...]
            v = z_ref[rows, C_VA:C_VA + ATT_KV]
            if latent:
                cos_ref, sin_ref = latent_refs[3], latent_refs[4]
                cos = jnp.concatenate([cos_ref[rows, :]] * (ATT_Q // ATT_KV), axis=-1)
                sin = jnp.concatenate([sin_ref[rows, :]] * (ATT_Q // ATT_KV), axis=-1)
                q = _rope(q, cos, sin)
                k = _rope(k, cos_ref[rows, :], sin_ref[rows, :])
            else:
                out_refs[3][s, rows, :] = k
                out_refs[4][s, rows, :] = v
            qr_ref[rows, :] = (q * (HEAD_DIM ** -0.5)).astype(_BF16)
            _store_split_kv(k, v, ks_ref, vs_ref, slice(pad + r0, pad + r0 + PROJ_TILE))

    for s, (z_ref, qr_ref, ks_ref, vs_ref, la_ref, o_ref, st_ref, upad_ref) in enumerate(per_seq):
        if latent:
            _attention_latent(T, layer, qr_ref, ks_ref, vs_ref, one(latent_refs[0], s), one(latent_refs[1], s),
                              sink_ref, one(out_refs[0], s))
        else:
            _attention_ctx(T, layer, qr_ref, ks_ref, vs_ref, sink_ref, one(out_refs[0], s))

    worst = None
    for s, (z_ref, qr_ref, ks_ref, vs_ref, la_ref, o_ref, st_ref, upad_ref) in enumerate(per_seq):
        if latent:
            st_ref[...] = latent_refs[2][s]
        else:
            st_ref[...] = jnp.zeros(st_ref.shape, _F32)
        span = _gla_prepare(T, z_ref, la_ref, o_ref, wg2_ref, bg2_ref, tri_ref)
        worst = span if worst is None else jnp.maximum(worst, span)
    _gla_run(T, [(z_ref, la_ref, o_ref, st_ref) for z_ref, _, _, _, la_ref, o_ref, st_ref, _ in per_seq],
             worst, a_ref, wg2_ref, bg2_ref, latent)

    for s, (z_ref, qr_ref, ks_ref, vs_ref, la_ref, o_ref, st_ref, upad_ref) in enumerate(per_seq):
        _gla_finish(T, z_ref, o_ref, ggla_ref, vones_ref, one(out_refs[1], s))
        if not latent:
            for d in range(2):
                for pair in range(GLA_HEADS // 2):
                    out_refs[5][s, d, pair] = st_ref[d, pair].T
        _pool(T, z_ref, upad_ref, wpool_ref, pscale_ref, bands_ref, icnt_ref, one(out_refs[2], s))


def _merge(x, mod_ref, oa, ob, oc, gn1_ref, wgate_ref, wa_ref, wb_ref, wc_ref, wout_ref):
    mod = lambda i: mod_ref[0, :, i * D_MODEL:(i + 1) * D_MODEL]
    hn = (x * _rms_scale(x) * gn1_ref[...]) * (1.0 + mod(1)) + mod(0)
    gates = jax.nn.sigmoid(_dot(hn.astype(_BF16), wgate_ref[...]))
    mixed = (gates[:, 0:D_MODEL] * _dot(oa, wa_ref[...])
             + gates[:, D_MODEL:2 * D_MODEL] * _dot(ob, wb_ref[...])
             + gates[:, 2 * D_MODEL:3 * D_MODEL] * _dot(oc, wc_ref[...]))
    return x + mod(2) * _dot(mixed.astype(_BF16), wout_ref[...])


def _ffn(x, mod_ref, gn2_ref, wfg_ref, wfu_ref, wfd_ref):
    mod = lambda i: mod_ref[0, :, i * D_MODEL:(i + 1) * D_MODEL]
    hn = ((x * _rms_scale(x) * gn2_ref[...]) * (1.0 + mod(4)) + mod(3)).astype(_BF16)
    h = _silu(_dot(hn, wfg_ref[...])) * _dot(hn, wfu_ref[...])
    return x + mod(5) * _dot(h.astype(_BF16), wfd_ref[...])


def _mix_latent_kernel(T, layer_ref, *refs):
    params, refs = refs[:N_MIX_PARAMS], refs[N_MIX_PARAMS:]
    _mix_body(True, T, layer_ref[0], params, refs[:5], refs[5:8], refs[8:])


def _mix_ctx_kernel(T, layer_ref, *refs):
    params, refs = refs[:N_MIX_PARAMS], refs[N_MIX_PARAMS + 3:]
    _mix_body(False, T, layer_ref[0], params, None, refs[:6], refs[6:])


def _layer_spec(shape):
    zeros = (0,) * len(shape)
    return pl.BlockSpec((None,) + tuple(shape), lambda i, layer: (layer[0],) + zeros,
                        pipeline_mode=pl.Buffered(1))


def _const_spec(blk):
    return pl.BlockSpec(blk, lambda i, layer: (0,) * len(blk), pipeline_mode=pl.Buffered(1))


def _mix_params(x, x_spec, mod_spec, mod_all, pw):
    bands = _pool_bands(x.shape[1])
    specs = [
        x_spec, mod_spec,
        _layer_spec((1, D_MODEL)),
        _layer_spec((D_MODEL, C_UC)),
        _layer_spec((D_MODEL, MIX_W - C_UC)),
        _layer_spec((1, ATT_Q)), _layer_spec((1, ATT_KV)),
        pl.BlockSpec(memory_space=pltpu.SMEM),
        _layer_spec((2, GL_PAD, 2 * GLA_QK)), _layer_spec((1, 2 * GLA_QK)), _layer_spec((1, GLA_VW)),
        _layer_spec((POOL_GROUPS, POOL_GROUP_DIM, POOL_GROUP_DIM)), _layer_spec((1, POOL_W)),
        _const_spec((ATT_Q, ATT_Q)), _const_spec((GLA_VW, GLA_VW)),
        _const_spec(bands.shape), _const_spec((x.shape[1], POOL_W)), _const_spec((2, GLA_BLOCK, GLA_BLOCK)),
    ]
    args = [x, mod_all, pw["g_norm1"], pw["w_main"], pw["w_tail"], pw["g_qn"], pw["g_kn"], pw["att_sink"], pw["w_gate2"],
            pw["b_gate2"], pw["g_gla_out"], pw["w_pool"], pw["pool_scale"], pw["head_ones"], pw["gla_ones"], bands,
            _pool_inverse_counts(x.shape[1]), _scan_masks()]
    assert len(specs) == len(args) == N_MIX_PARAMS
    return specs, args


def _mix_scratch(S, T, kv_rows):
    scratch = [
        pltpu.VMEM((S, T, MIX_W), _F32),
        pltpu.VMEM((S, T, ATT_Q), _BF16),
        pltpu.VMEM((S, ATT_KV_HEADS, 2, kv_rows, 2 * HEAD_DIM), _BF16),
        pltpu.VMEM((S, ATT_KV_HEADS, 2, kv_rows, 4 * HEAD_DIM), _BF16),
        pltpu.VMEM((S, T, 2 * GLA_QK), _F32),
        pltpu.VMEM((S, T, GLA_VW), _F32),
        pltpu.VMEM((S, 2, GLA_HEADS // 2, GLA_DV, 2 * GLA_DK), _F32),
        pltpu.VMEM((GLA_HEADS + 2, GLA_CHUNK, GLA_QK), _F32),
        pltpu.VMEM((S, T + 2 * _pool_halo(T), POOL_W), _BF16),
    ]
    assert len(scratch) == N_MIX_SCRATCH
    return scratch


def _mix_latent_call(layer, x, mod_all, pw, cache_k, cache_v, st0, cos, sin):
    B, T, _ = x.shape
    per_seq = lambda blk: pl.BlockSpec(blk, lambda b, layer: (b,) + (0,) * (len(blk) - 1))
    mod_spec = pl.BlockSpec((None, 1, 1, 6 * D_MODEL), lambda b, layer: (layer[0], b + 1, 0, 0))
    in_specs, args = _mix_params(x, per_seq((1, T, D_MODEL)), mod_spec, mod_all, pw)
    P = cache_k.shape[2]
    cache_spec = pl.BlockSpec((1, 1, P, ATT_KV), lambda b, layer: (b, layer[0], 0, 0))
    in_specs += [cache_spec, cache_spec,
                 pl.BlockSpec((1, None, 2, GLA_HEADS // 2, GLA_DV, 2 * GLA_DK),
                              lambda b, layer: (b, layer[0], 0, 0, 0, 0)),
                 _const_spec((T, ATT_KV)), _const_spec((T, ATT_KV))]
    args += [cache_k, cache_v, st0, cos, sin]
    widths = (ATT_Q, GLA_VW, POOL_W)
    return pl.pallas_call(
        functools.partial(_mix_latent_kernel, T),
        grid_spec=pltpu.PrefetchScalarGridSpec(
            num_scalar_prefetch=1, grid=(B,), in_specs=in_specs,
            out_specs=[per_seq((1, T, w)) for w in widths],
            scratch_shapes=_mix_scratch(1, T, T + 2 * ATT_BLOCK)),
        out_shape=[jax.ShapeDtypeStruct((B, T, w), _BF16) for w in widths],
        compiler_params=pltpu.CompilerParams(dimension_semantics=("arbitrary",), vmem_limit_bytes=VMEM_LIMIT),
        name="mix_latent",
    )(layer, *args)


def _mix_ctx_call(layer, x, mod_all, pw, stacked):
    B, T, _ = x.shape
    S = CTX_SEQS_PER_STEP
    assert B % S == 0
    per_seq = lambda blk: pl.BlockSpec(blk, lambda b, layer: (b,) + (0,) * (len(blk) - 1))
    mod_spec = pl.BlockSpec((None, 1, 1, 6 * D_MODEL), lambda b, layer: (layer[0], 0, 0, 0))
    in_specs, args = _mix_params(x, per_seq((S, T, D_MODEL)), mod_spec, mod_all, pw)
    widths = (ATT_Q, GLA_VW, POOL_W)
    n_in = 1 + len(args)
    aliases = {n_in + j: len(widths) + j for j in range(len(stacked))}
    in_specs += [pl.BlockSpec(memory_space=pl.ANY)] * len(stacked)
    args += list(stacked)
    at_layer = lambda blk: pl.BlockSpec((S, None) + blk, lambda b, layer: (b, layer[0]) + (0,) * len(blk))
    out_specs = [per_seq((S, T, w)) for w in widths] + [
        at_layer((T, ATT_KV)), at_layer((T, ATT_KV)), at_layer((2, GLA_HEADS // 2, 2 * GLA_DK, GLA_DV))]
    out_shape = ([jax.ShapeDtypeStruct((B, T, w), _BF16) for w in widths]
                 + [jax.ShapeDtypeStruct(a.shape, a.dtype) for a in stacked])
    return pl.pallas_call(
        functools.partial(_mix_ctx_kernel, T),
        grid_spec=pltpu.PrefetchScalarGridSpec(
            num_scalar_prefetch=1, grid=(B // S,), in_specs=in_specs, out_specs=out_specs,
            scratch_shapes=_mix_scratch(S, T, T)),
        out_shape=out_shape,
        input_output_aliases=aliases,
        compiler_params=pltpu.CompilerParams(dimension_semantics=("arbitrary",), vmem_limit_bytes=VMEM_LIMIT),
        name="mix_ctx",
    )(layer, *args)


def _post_kernel(layer_ref, x_ref, mod_ref, oa_ref, ob_ref, oc_ref, gn1_ref, gn2_ref, wgate_ref, wa_ref, wb_ref,
                 wc_ref, wout_ref, wfg_ref, wfu_ref, wfd_ref, out_ref):
    x = _merge(x_ref[...], mod_ref, oa_ref[...], ob_ref[...], oc_ref[...], gn1_ref, wgate_ref, wa_ref, wb_ref,
               wc_ref, wout_ref)
    out_ref[...] = _ffn(x, mod_ref, gn2_ref, wfg_ref, wfu_ref, wfd_ref)


def _post_call(layer, x2d, mod_all, oa, ob, oc, pw, tiles_per_seq):
    row = lambda w: pl.BlockSpec((POST_TILE, w), lambda i, layer: (i, 0))
    if tiles_per_seq is None:
        mod_spec = pl.BlockSpec((None, 1, 1, 6 * D_MODEL), lambda i, layer: (layer[0], 0, 0, 0))
    else:
        mod_spec = pl.BlockSpec((None, 1, 1, 6 * D_MODEL),
                                lambda i, layer: (layer[0], 1 + i // tiles_per_seq, 0, 0))
    weights = [(pw["g_norm1"], (1, D_MODEL)), (pw["g_norm2"], (1, D_MODEL)), (pw["w_gates"], (D_MODEL, GATE_W)),
               (pw["w_br_a"], (ATT_Q, D_MODEL)), (pw["w_br_b"], (GLA_VW, D_MODEL)),
               (pw["w_br_c"], (POOL_W, D_MODEL)), (pw["w_out"], (D_MODEL, D_MODEL)),
               (pw["w_ff_gate"], (D_MODEL, D_FF)), (pw["w_ff_up"], (D_MODEL, D_FF)),
               (pw["w_ff_down"], (D_FF, D_MODEL))]
    in_specs = ([row(D_MODEL), mod_spec, row(ATT_Q), row(GLA_VW), row(POOL_W)]
                + [_layer_spec(shape) for _, shape in weights])
    return pl.pallas_call(
        _post_kernel,
        grid_spec=pltpu.PrefetchScalarGridSpec(
            num_scalar_prefetch=1, grid=(x2d.shape[0] // POST_TILE,), in_specs=in_specs, out_specs=row(D_MODEL)),
        out_shape=jax.ShapeDtypeStruct(x2d.shape, _F32),
        input_output_aliases={1: 0},
        compiler_params=pltpu.CompilerParams(dimension_semantics=("arbitrary",), vmem_limit_bytes=VMEM_LIMIT),
        name="post",
    )(layer, x2d, mod_all, oa, ob, oc, *[a for a, _ in weights])


def _rope_tables(T):
    quarter = HEAD_DIM // 4
    inv_freq = ROPE_BASE ** (-np.arange(quarter, dtype=np.float32) / quarter)
    pos = np.arange(T)
    ang_row = (pos // GRID_W).astype(np.float32)[:, None] * inv_freq[None, :]
    ang_col = (pos % GRID_W).astype(np.float32)[:, None] * inv_freq[None, :]
    cos = np.concatenate([np.cos(ang_row)] * 2 + [np.cos(ang_col)] * 2, axis=-1)
    sin = np.concatenate([-np.sin(ang_row), np.sin(ang_row), -np.sin(ang_col), np.sin(ang_col)], axis=-1)
    return (jnp.asarray(np.tile(cos, (1, ATT_KV_HEADS)), _F32), jnp.asarray(np.tile(sin, (1, ATT_KV_HEADS)), _F32))


def _prepare_weights(w_in, g_qn, g_kn, att_sink, w_gate2, b_gate2, g_gla_out, w_pool, pool_scale, w_br_a,
                     w_br_b, w_br_c, w_out, g_norm1, g_norm2, w_ff_gate, w_ff_up, w_ff_down):
    o_gl = ATT_Q + 2 * ATT_KV + 2 * GLA_QK + 2 * GLA_VW
    o_uc = o_gl + 2 * GLA_RANK
    o_gate = o_uc + POOL_W
    assert o_gl == C_UC
    w_main = w_in[:, :, :o_gl].astype(_BF16)
    w_tail = jnp.concatenate(
        [w_in[:, :, o_uc:o_gate], w_in[:, :, o_gl:o_uc],
         jnp.zeros((DEPTH, D_MODEL, GL_PAD - 2 * GLA_RANK), w_in.dtype)], axis=2).astype(_BF16)
    w_gates = w_in[:, :, o_gate:].astype(_BF16)
    wg2 = jnp.zeros((DEPTH, GL_PAD, 2 * GLA_QK), _F32)
    wg2 = wg2.at[:, 0:GLA_RANK, 0:GLA_QK].set(w_gate2[:, 0])
    wg2 = wg2.at[:, GLA_RANK:2 * GLA_RANK, GLA_QK:].set(w_gate2[:, 1])
    wg2_hi = wg2.astype(_BF16)
    wg2 = jnp.stack([wg2_hi, (wg2 - wg2_hi.astype(_F32)).astype(_BF16)], axis=1)
    vec = lambda a: a.reshape(DEPTH, 1, -1)
    group_ones = lambda n, width: jnp.asarray(
        (np.arange(n)[:, None] // width) == (np.arange(n)[None, :] // width), _BF16)
    return {
        "head_ones": group_ones(ATT_Q, HEAD_DIM),
        "gla_ones": group_ones(GLA_VW, GLA_DV),
        "w_main": w_main,
        "w_tail": w_tail,
        "w_gates": w_gates,
        "g_qn": vec(jnp.tile(g_qn, (1, ATT_HEADS))),
        "g_kn": vec(jnp.tile(g_kn, (1, ATT_KV_HEADS))),
        "att_sink": att_sink,
        "w_gate2": wg2,
        "b_gate2": vec(b_gate2),
        "g_gla_out": vec(jnp.tile(g_gla_out, (1, GLA_HEADS))),
        "w_pool": w_pool.astype(_BF16),
        "pool_scale": vec(pool_scale),
        "w_br_a": w_br_a.astype(_BF16),
        "w_br_b": w_br_b.astype(_BF16),
        "w_br_c": w_br_c.astype(_BF16),
        "w_out": w_out.astype(_BF16),
        "g_norm1": vec(g_norm1),
        "g_norm2": vec(g_norm2),
        "w_ff_gate": w_ff_gate.astype(_BF16),
        "w_ff_up": w_ff_up.astype(_BF16),
        "w_ff_down": w_ff_down.astype(_BF16),
    }


def kernel(x_prompt, x_sample, c, cache_k, cache_v, state_gla, c_ctx, w_in, g_qn, g_kn, att_sink, w_gate2,
           b_gate2, g_gla_out, w_pool, pool_scale, w_br_a, w_br_b, w_br_c, w_out, g_norm1, g_norm2, w_mod,
           b_mod, w_ff_gate, w_ff_up, w_ff_down):
    B, T, _ = x_prompt.shape
    BL, TL, _ = x_sample.shape
    assert (B * T) % POST_TILE == 0 and TL % POST_TILE == 0 and BL + 1 <= MOD_ROWS
    assert T % PROJ_TILE == 0 and TL % PROJ_TILE == 0
    cv = jnp.concatenate([c_ctx[None, :], c, jnp.zeros((MOD_ROWS - 1 - BL, D_MODEL), _F32)], axis=0)
    mod_all = _modulation(cv, w_mod, b_mod).reshape(DEPTH, MOD_ROWS, 1, 6 * D_MODEL)
    pw = _prepare_weights(w_in, g_qn, g_kn, att_sink, w_gate2, b_gate2, g_gla_out, w_pool, pool_scale, w_br_a,
                          w_br_b, w_br_c, w_out, g_norm1, g_norm2, w_ff_gate, w_ff_up, w_ff_down)
    cos, sin = _rope_tables(TL)
    P = cache_k.shape[2]
    latent_ctx = (cache_k.reshape(BL, DEPTH, P, ATT_KV), cache_v.reshape(BL, DEPTH, P, ATT_KV),
                  jnp.swapaxes(state_gla.reshape(BL, DEPTH, 2, GLA_HEADS // 2, 2 * GLA_DK, GLA_DV), -1, -2),
                  cos, sin)

    def layer_step(l, carry):
        yp, ys, new_k, new_v, new_st = carry
        layer = jnp.full((1,), l, jnp.int32)
        oa, ob, oc, new_k, new_v, new_st = _mix_ctx_call(layer, yp, mod_all, pw, (new_k, new_v, new_st))
        yp = _post_call(layer, yp.reshape(B * T, D_MODEL), mod_all, oa.reshape(B * T, -1), ob.reshape(B * T, -1),
                        oc.reshape(B * T, -1), pw, None).reshape(B, T, D_MODEL)
        oa, ob, oc = _mix_latent_call(layer, ys, mod_all, pw, *latent_ctx)
        ys = _post_call(layer, ys.reshape(BL * TL, D_MODEL), mod_all, oa.reshape(BL * TL, -1),
                        ob.reshape(BL * TL, -1), oc.reshape(BL * TL, -1), pw,
                        TL // POST_TILE).reshape(BL, TL, D_MODEL)
        return yp, ys, new_k, new_v, new_st

    init = (x_prompt, x_sample,
            jnp.zeros((B, DEPTH, T, ATT_KV), _F32), jnp.zeros((B, DEPTH, T, ATT_KV), _F32),
            jnp.zeros((B, DEPTH, 2, GLA_HEADS // 2, 2 * GLA_DK, GLA_DV), _F32))
    yp, ys, new_k, new_v, new_st = lax.fori_loop(0, DEPTH, layer_step, init)
    return (yp, ys, new_k.reshape(B, DEPTH, T, ATT_KV_HEADS, HEAD_DIM),
            new_v.reshape(B, DEPTH, T, ATT_KV_HEADS, HEAD_DIM),
            new_st.reshape(B, DEPTH, 2, GLA_HEADS, GLA_DK, GLA_DV))
```

```python
import functools

import jax
import jax.numpy as jnp
import numpy as np
from jax import lax
from jax.experimental import pallas as pl
from jax.experimental.pallas import tpu as pltpu

D_MODEL = 1024
DEPTH = 4
GRID_W = 64
ATT_HEADS = 8
ATT_KV_HEADS = 2
ATT_GROUP = ATT_HEADS // ATT_KV_HEADS
HEAD_DIM = 64
WINDOW = 128
ATT_BLOCK = 128
ROPE_BASE = 10000.0
GLA_HEADS = 4
GLA_DK = 64
GLA_DV = 128
GLA_RANK = 16
GLA_TAU = 16.0
GLA_CHUNK = 64
POOL_GROUPS = 4
POOL_GROUP_DIM = 128
POOL_WINDOWS = (2, 4, 8, 16)
D_FF = 2816
ATT_Q = ATT_HEADS * HEAD_DIM
ATT_KV = ATT_KV_HEADS * HEAD_DIM
GLA_QK = GLA_HEADS * GLA_DK
GLA_VW = GLA_HEADS * GLA_DV
POOL_W = POOL_GROUPS * POOL_GROUP_DIM
EPS = 1e-6
NEG = -1e30

C_QA = 0
C_KA = C_QA + ATT_Q
C_VA = C_KA + ATT_KV
C_QB = C_VA + ATT_KV
C_KB = C_QB + GLA_QK
C_VB = C_KB + GLA_QK
C_RB = C_VB + GLA_VW
C_UC = C_RB + GLA_VW
C_GL = C_UC + POOL_W
LANES = 128
SUBLANES = 8
MXU_TILE = 256
GL_PAD = LANES
MIX_W = C_GL + GL_PAD
GATE_W = 3 * D_MODEL

POST_TILE = 512
PROJ_TILE = 256
CTX_SEQS_PER_STEP = 2
POOL_TILE = 256
POOL_HALO = LANES
MOD_ROWS = SUBLANES
MOD_TILE = 3072
GLA_BLOCK = 256
GLA_MAX_EXPONENT = 80.0
VMEM_LIMIT = 56 * 1024 * 1024

_F32 = jnp.float32
_BF16 = jnp.bfloat16


def _dot(a, b):
    return jnp.dot(a, b, preferred_element_type=_F32)


def _dot_nt(a, b):
    return lax.dot_general(a, b, (((1,), (1,)), ((), ())), preferred_element_type=_F32)


def _dot_tn(a, b):
    return lax.dot_general(a, b, (((0,), (0,)), ((), ())), preferred_element_type=_F32)


def _rms_scale(x):
    return lax.rsqrt(jnp.mean(x * x, axis=-1, keepdims=True) + EPS)


def _group_rms_scale(x, group_ones, width):
    sq = (x * x).astype(_BF16)
    n, blk = x.shape[-1], group_ones.shape[0]
    if n <= blk:
        sums = _dot(sq, group_ones[0:n, 0:n])
    else:
        sums = jnp.concatenate([_dot(sq[:, i:i + blk], group_ones) for i in range(0, n, blk)], axis=-1)
    return lax.rsqrt(sums * (1.0 / width) + EPS)


def _log_sigmoid(x):
    return jnp.minimum(x, 0.0) - jnp.log(1.0 + jnp.exp(-jnp.abs(x)))


def _silu(x):
    return x * jax.nn.sigmoid(x)


def _rope(x, cos, sin_signed):
    n = x.shape[-1]
    lane = lax.broadcasted_iota(jnp.int32, x.shape, 1)
    up = pltpu.roll(x, n - HEAD_DIM // 4, axis=1)
    down = pltpu.roll(x, HEAD_DIM // 4, axis=1)
    partner = jnp.where((lane & (HEAD_DIM // 2 - 1)) < HEAD_DIM // 4, up, down)
    return x * cos + partner * sin_signed


def _mod_kernel(cv_ref, w_ref, b_ref, out_ref):
    s = _silu(cv_ref[...]).astype(_BF16)
    out_ref[0] = _dot(s, w_ref[0].astype(_BF16)) + b_ref[0]


def _modulation(cv, w_mod, b_mod):
    n_col = (6 * D_MODEL) // MOD_TILE
    return pl.pallas_call(
        _mod_kernel,
        grid=(DEPTH, n_col),
        in_specs=[
            pl.BlockSpec((MOD_ROWS, D_MODEL), lambda l, j: (0, 0)),
            pl.BlockSpec((1, D_MODEL, MOD_TILE), lambda l, j: (l, 0, j)),
            pl.BlockSpec((1, 1, MOD_TILE), lambda l, j: (l, 0, j)),
        ],
        out_specs=pl.BlockSpec((1, MOD_ROWS, MOD_TILE), lambda l, j: (l, 0, j)),
        out_shape=jax.ShapeDtypeStruct((DEPTH, MOD_ROWS, 6 * D_MODEL), _F32),
        compiler_params=pltpu.CompilerParams(vmem_limit_bytes=VMEM_LIMIT),
        name="modulation",
    )(cv, w_mod, b_mod.reshape(DEPTH, 1, 6 * D_MODEL))


def _split_heads(x):
    low = lax.broadcasted_iota(jnp.int32, x.shape, 1) < HEAD_DIM
    swapped = pltpu.roll(x, HEAD_DIM, axis=1)
    zero = jnp.zeros_like(x)
    return ((jnp.where(low, x, zero), jnp.where(low, zero, swapped)),
            (jnp.where(low, swapped, zero), jnp.where(low, zero, x)))


def _store_split_kv(k, v, ks_ref, vs_ref, rows):
    ones = jnp.ones_like(v)
    for kv, (k_sides, v_sides, one_sides) in enumerate(zip(_split_heads(k), _split_heads(v), _split_heads(ones))):
        for side in range(2):
            ks_ref[kv, side, rows, :] = k_sides[side].astype(_BF16)
            vs_ref[kv, side, rows, :] = jnp.concatenate([v_sides[side], one_sides[side]], axis=-1).astype(_BF16)


def _pair_softmax_av(qp, keys, values, masks, sink_even, sink_odd):
    m = qp.shape[0]
    scores = []
    for (k_left, k_right), mask in zip(keys, masks):
        s_even, s_odd = _dot_nt(qp, k_left), _dot_nt(qp, k_right)
        if mask is not None:
            s_even, s_odd = jnp.where(mask, s_even, NEG), jnp.where(mask, s_odd, NEG)
        scores.append((s_even, s_odd))
    m_even = jnp.full((m, 1), sink_even, _F32)
    m_odd = jnp.full((m, 1), sink_odd, _F32)
    for s_even, s_odd in scores:
        m_even = jnp.maximum(m_even, jnp.max(s_even, axis=-1, keepdims=True))
        m_odd = jnp.maximum(m_odd, jnp.max(s_odd, axis=-1, keepdims=True))
    res = None
    for (s_even, s_odd), (w_left, w_right) in zip(scores, values):
        r = (_dot(jnp.exp(s_even - m_even).astype(_BF16), w_left)
             + _dot(jnp.exp(s_odd - m_odd).astype(_BF16), w_right))
        res = r if res is None else res + r
    pair = 2 * HEAD_DIM
    low = lax.broadcasted_iota(jnp.int32, (m, pair), 1) < HEAD_DIM
    den = res[:, pair:] + jnp.where(low, jnp.exp(sink_even - m_even), jnp.exp(sink_odd - m_odd))
    return res[:, :pair] / den


def _attention_ctx(T, layer, qr_ref, ks_ref, vs_ref, sink_ref, oa_ref):
    pair = 2 * HEAD_DIM
    for kv in range(ATT_KV_HEADS):
        keys = [(ks_ref[kv, 0], ks_ref[kv, 1])]
        values = [(vs_ref[kv, 0], vs_ref[kv, 1])]
        for j in range(ATT_GROUP // 2):
            head = kv * ATT_GROUP + 2 * j
            cols = slice(head * HEAD_DIM, head * HEAD_DIM + pair)
            o = _pair_softmax_av(qr_ref[:, cols], keys, values, [None], sink_ref[layer, head],
                                 sink_ref[layer, head + 1])
            oa_ref[0, :, cols] = o.astype(_BF16)


def _attention_latent(T, layer, qr_ref, ks_ref, vs_ref, kc_ref, vc_ref, sink_ref, oa_ref):
    pair = 2 * HEAD_DIM
    span = 3 * ATT_BLOCK
    kc, vc = kc_ref[0, 0], vc_ref[0, 0]
    ones = jnp.ones_like(vc)
    ctx_keys = [tuple(side.astype(_BF16) for side in sides) for sides in _split_heads(kc)]
    ctx_values = [tuple(jnp.concatenate([v_side, one_side], axis=-1).astype(_BF16)
                        for v_side, one_side in zip(v_sides, one_sides))
                  for v_sides, one_sides in zip(_split_heads(vc), _split_heads(ones))]

    def block(i, carry):
        q_rows = pl.ds(pl.multiple_of(i * ATT_BLOCK, ATT_BLOCK), ATT_BLOCK)
        k_rows = pl.ds(pl.multiple_of(i * ATT_BLOCK, ATT_BLOCK), span)
        q_pos = i * ATT_BLOCK + lax.broadcasted_iota(jnp.int32, (ATT_BLOCK, span), 0)
        k_pos = (i - 1) * ATT_BLOCK + lax.broadcasted_iota(jnp.int32, (ATT_BLOCK, span), 1)
        valid = (jnp.abs(k_pos - q_pos) <= WINDOW) & (k_pos >= 0) & (k_pos < T)
        for kv in range(ATT_KV_HEADS):
            keys = [(ks_ref[kv, 0, k_rows, :], ks_ref[kv, 1, k_rows, :]), ctx_keys[kv]]
            values = [(vs_ref[kv, 0, k_rows, :], vs_ref[kv, 1, k_rows, :]), ctx_values[kv]]
            for j in range(ATT_GROUP // 2):
                head = kv * ATT_GROUP + 2 * j
                cols = slice(head * HEAD_DIM, head * HEAD_DIM + pair)
                o = _pair_softmax_av(qr_ref[q_rows, cols], keys, values, [valid, None], sink_ref[layer, head],
                                     sink_ref[layer, head + 1])
                oa_ref[0, q_rows, cols] = o.astype(_BF16)
        return carry

    lax.fori_loop(0, T // ATT_BLOCK, block, 0, unroll=2)


def _gla_chunk(z_ref, la_ref, o_ref, st_ref, a_ref, direction, start):
    C = GLA_CHUNK
    rows = pl.ds(pl.multiple_of(start, C), C)
    q = z_ref[rows, C_QB:C_QB + GLA_QK] * (GLA_DK ** -0.5)
    k = z_ref[rows, C_KB:C_KB + GLA_QK]
    v = z_ref[rows, C_VB:C_VB + GLA_VW].astype(_BF16)
    la = la_ref[rows, direction * GLA_QK:(direction + 1) * GLA_QK]
    la_hi = la.astype(_BF16)
    la_lo = (la - la_hi.astype(_F32)).astype(_BF16)
    ti = lax.broadcasted_iota(jnp.int32, (C, C), 0)
    si = lax.broadcasted_iota(jnp.int32, (C, C), 1)
    causal = (si <= ti) if direction == 0 else (si >= ti)
    tri = jnp.where(causal, 1.0, 0.0).astype(_BF16)
    b = _dot(tri, la_hi) + _dot(tri, la_lo)
    end = C - 1 if direction == 0 else 0
    b_end = b[end:end + 1]
    q_in = (q * jnp.exp(b)).astype(_BF16)
    k_out = (k * jnp.exp(b_end - b)).astype(_BF16)
    e_end = jnp.exp(b_end)
    ones = jnp.ones((8, GLA_DK), _BF16)

    def row_group(g, carry):
        base = pl.multiple_of(g * 8, 8)
        b8 = a_ref[GLA_HEADS, pl.ds(base, 8), :]
        q8 = a_ref[GLA_HEADS + 1, pl.ds(base, 8), :]
        s_idx = lax.broadcasted_iota(jnp.int32, (C, 1), 0)
        rows_h = [[] for _ in range(GLA_HEADS)]
        for j in range(8):
            ok = (s_idx <= base + j) if direction == 0 else (s_idx >= base + j)
            decay = jnp.exp(jnp.where(ok, b8[j:j + 1] - b, NEG))
            p = (q8[j:j + 1] * k * decay).astype(_BF16)
            for h in range(GLA_HEADS):
                rows_h[h].append(_dot_nt(ones, p[:, h * GLA_DK:(h + 1) * GLA_DK])[0:1])
        for h in range(GLA_HEADS):
            a_ref[h, pl.ds(base, 8), 0:C] = jnp.concatenate(rows_h[h], axis=0)
        return carry

    a_ref[GLA_HEADS] = b
    a_ref[GLA_HEADS + 1] = q
    lax.fori_loop(0, C // 8, row_group, 0)
    outs = []
    for pair in range(GLA_HEADS // 2):
        s_pair = st_ref[direction, pair]
        s_next = []
        for h in (2 * pair, 2 * pair + 1):
            kc = slice(h * GLA_DK, (h + 1) * GLA_DK)
            vc = slice(h * GLA_DV, (h + 1) * GLA_DV)
            s_t = s_pair[:, (h % 2) * GLA_DK:(h % 2 + 1) * GLA_DK]
            outs.append(_dot_nt(q_in[:, kc], s_t.astype(_BF16)) + _dot(a_ref[h, :, 0:C].astype(_BF16), v[:, vc]))
            s_next.append(s_t * e_end[:, kc] + _dot_tn(v[:, vc], k_out[:, kc]))
        st_ref[direction, pair] = jnp.concatenate(s_next, axis=-1)
    o_ref[rows, :] += jnp.concatenate(outs, axis=-1)


def _gla_block(z_ref, la_ref, o_ref, st_ref, direction, start, use_state):
    NB = GLA_BLOCK
    fwd = direction == 0
    rows = slice(start, start + NB) if isinstance(start, int) else pl.ds(pl.multiple_of(start, NB), NB)
    q = z_ref[rows, C_QB:C_QB + GLA_QK] * (GLA_DK ** -0.5)
    k = z_ref[rows, C_KB:C_KB + GLA_QK]
    v = z_ref[rows, C_VB:C_VB + GLA_VW].astype(_BF16)
    b = la_ref[rows, direction * GLA_QK:(direction + 1) * GLA_QK]
    end, mid = (NB - 1, NB // 2 - 1) if fwd else (0, NB // 2)
    b_end = b[end:end + 1]
    c = b - b[mid:mid + 1]
    q_c = (q * jnp.exp(c)).astype(_BF16)
    k_c = (k * jnp.exp(-c)).astype(_BF16)
    k_fin = (k * jnp.exp(b_end - b)).astype(_BF16)
    if use_state:
        q_in = (q * jnp.exp(b)).astype(_BF16)
        e_all = jnp.exp(b_end)
    ti = lax.broadcasted_iota(jnp.int32, (NB, NB), 0)
    si = lax.broadcasted_iota(jnp.int32, (NB, NB), 1)
    causal = (si <= ti) if fwd else (si >= ti)
    tile = 2 * GLA_DK
    low = lax.broadcasted_iota(jnp.int32, (NB, tile), 1) < GLA_DK
    zero = jnp.zeros((NB, tile), _BF16)
    pick = lambda x, parity: jnp.where(low, x, zero) if parity == 0 else jnp.where(low, zero, x)
    outs = []
    for pair in range(GLA_HEADS // 2):
        lanes = slice(pair * tile, (pair + 1) * tile)
        if use_state:
            s_pair = st_ref[direction, pair]
            s_bf = s_pair.astype(_BF16)
        s_new = None
        for parity in range(2):
            h = 2 * pair + parity
            vc = slice(h * GLA_DV, (h + 1) * GLA_DV)
            a = jnp.where(causal, _dot_nt(q_c[:, lanes], pick(k_c[:, lanes], parity)), 0.0)
            o_h = _dot(a.astype(_BF16), v[:, vc])
            if use_state:
                o_h = o_h + _dot_nt(pick(q_in[:, lanes], parity), s_bf)
            outs.append(o_h)
            upd = _dot_tn(v[:, vc], pick(k_fin[:, lanes], parity))
            s_new = upd if s_new is None else s_new + upd
        if use_state:
            s_new = s_new + s_pair * e_all[:, lanes]
        st_ref[direction, pair] = s_new
    o_ref[rows, :] += jnp.concatenate(outs, axis=-1)


def _log_decay(z_ref, rows, wg2_ref, bg2_ref):
    pre = _dot(z_ref[rows, C_GL:C_GL + GL_PAD].astype(_BF16), wg2_ref[...]) + bg2_ref[...]
    return _log_sigmoid(pre) * (1.0 / GLA_TAU)


def _gla_prepare(T, z_ref, la_ref, o_ref, wg2_ref, bg2_ref, tri_ref):
    NB = GLA_BLOCK
    worst = None
    for r0 in range(0, T, NB):
        rows = slice(r0, r0 + NB)
        la = _log_decay(z_ref, rows, wg2_ref, bg2_ref)
        la_hi = la.astype(_BF16)
        la_lo = (la - la_hi.astype(_F32)).astype(_BF16)
        for d in range(2):
            cols = slice(d * GLA_QK, (d + 1) * GLA_QK)
            b = _dot(tri_ref[d], la_hi[:, cols]) + _dot(tri_ref[d], la_lo[:, cols])
            la_ref[rows, cols] = b
            first, mid, last = (0, NB // 2 - 1, NB - 1) if d == 0 else (NB - 1, NB // 2, 0)
            span = jnp.max(jnp.maximum(b[first:first + 1] - b[mid:mid + 1], b[mid:mid + 1] - b[last:last + 1]))
            worst = span if worst is None else jnp.maximum(worst, span)
    o_ref[...] = jnp.zeros(o_ref.shape, _F32)
    return worst


def _gla_run(T, seqs, worst, a_ref, wg2_ref, bg2_ref, use_state):
    NB = GLA_BLOCK
    n_chunks = T // GLA_CHUNK
    n_blocks = T // NB
    fast_ok = worst <= GLA_MAX_EXPONENT

    @pl.when(fast_ok)
    def _():
        if n_blocks == 1:
            for direction in range(2):
                for z_ref, la_ref, o_ref, st_ref in seqs:
                    _gla_block(z_ref, la_ref, o_ref, st_ref, direction, 0, use_state)
        else:
            def body(i, carry):
                for z_ref, la_ref, o_ref, st_ref in seqs:
                    _gla_block(z_ref, la_ref, o_ref, st_ref, 0, i * NB, True)
                    _gla_block(z_ref, la_ref, o_ref, st_ref, 1, (n_blocks - 1 - i) * NB, True)
                return carry
            lax.fori_loop(0, n_blocks, body, 0, unroll=2)

    @pl.when(jnp.logical_not(fast_ok))
    def _():
        for z_ref, la_ref, o_ref, st_ref in seqs:
            for r0 in range(0, T, NB):
                la_ref[r0:r0 + NB, :] = _log_decay(z_ref, slice(r0, r0 + NB), wg2_ref, bg2_ref)

            def body(i, carry):
                _gla_chunk(z_ref, la_ref, o_ref, st_ref, a_ref, 0, i * GLA_CHUNK)
                _gla_chunk(z_ref, la_ref, o_ref, st_ref, a_ref, 1, (n_chunks - 1 - i) * GLA_CHUNK)
                return carry
            lax.fori_loop(0, n_chunks, body, 0)


def _gla_finish(T, z_ref, o_ref, ggla_ref, vones_ref, ob_ref):
    for r0 in range(0, T, PROJ_TILE):
        rows = slice(r0, r0 + PROJ_TILE)
        o = o_ref[rows, :]
        y = o * _group_rms_scale(o, vones_ref[...], GLA_DV) * ggla_ref[...]
        ob_ref[0, rows, :] = (y * _silu(z_ref[rows, C_RB:C_RB + GLA_VW])).astype(_BF16)


def _pool_halo(T):
    return POOL_HALO if T > POOL_TILE else 0


def _pool_bands(T):
    halo = _pool_halo(T)
    off = np.arange(POOL_TILE + 2 * halo)[None, :] - halo - np.arange(POOL_TILE)[:, None]
    return jnp.asarray(np.stack([(off >= -(w // 2)) & (off < w - w // 2) for w in POOL_WINDOWS]), _BF16)


def _scan_masks():
    t = np.arange(GLA_BLOCK)
    return jnp.asarray(np.stack([t[None, :] <= t[:, None], t[None, :] >= t[:, None]]), _BF16)


def _pool_inverse_counts(T):
    t = np.arange(T)
    cols = [np.repeat((1.0 / (np.minimum(t - w // 2 + w, T) - np.maximum(t - w // 2, 0)))[:, None],
                      POOL_GROUP_DIM, axis=1) for w in POOL_WINDOWS]
    return jnp.asarray(np.concatenate(cols, axis=1), _F32)


def _pool(T, z_ref, upad_ref, wpool_ref, pscale_ref, bands_ref, icnt_ref, oc_ref):
    halo = _pool_halo(T)
    if halo:
        zeros = jnp.zeros((halo, POOL_W), _BF16)
        upad_ref[0:halo, :] = zeros
        upad_ref[halo + T:halo + T + halo, :] = zeros
    upad_ref[halo:halo + T, :] = z_ref[:, C_UC:C_UC + POOL_W].astype(_BF16)
    span = POOL_TILE + 2 * halo
    group = lambda g: slice(g * POOL_GROUP_DIM, (g + 1) * POOL_GROUP_DIM)
    for jb in range(T // POOL_TILE):
        rows = slice(jb * POOL_TILE, (jb + 1) * POOL_TILE)
        win = slice(jb * POOL_TILE, jb * POOL_TILE + span)
        total = jnp.concatenate([_dot(bands_ref[g], upad_ref[win, group(g)]) for g in range(POOL_GROUPS)], axis=-1)
        pooled = (total * icnt_ref[rows, :] - z_ref[rows, C_UC:C_UC + POOL_W]).astype(_BF16)
        pair = 2 * POOL_GROUP_DIM
        y = jnp.concatenate([_dot(pooled[:, p * pair:(p + 1) * pair], wpool_ref[p])
                             for p in range(POOL_GROUPS // 2)], axis=-1)
        oc_ref[0, rows, :] = (y * pscale_ref[...]).astype(_BF16)


N_MIX_PARAMS = 18
N_MIX_SCRATCH = 9


def _mix_body(latent, T, layer, params, latent_refs, out_refs, scratch):
    (x_ref, mod_ref, gn1_ref, wmain_ref, wtail_ref, gqn_ref, gkn_ref, sink_ref, wg2_ref, bg2_ref, ggla_ref,
     wpool_ref, pscale_ref, hones_ref, vones_ref, bands_ref, icnt_ref, tri_ref) = params
    n_seq = x_ref.shape[0]
    one = lambda ref, s: ref.at[pl.ds(s, 1)]
    a_ref = scratch[N_MIX_SCRATCH - 2]
    per_seq = [tuple(ref.at[s] for ref in scratch[:N_MIX_SCRATCH - 2] + scratch[N_MIX_SCRATCH - 1:])
               for s in range(n_seq)]
    pad = ATT_BLOCK if latent else 0
    shift = mod_ref[0, :, 0:D_MODEL]
    scale = mod_ref[0, :, D_MODEL:2 * D_MODEL]

    for s, (z_ref, qr_ref, ks_ref, vs_ref, la_ref, o_ref, st_ref, upad_ref) in enumerate(per_seq):
        if latent:
            for ref in (ks_ref, vs_ref):
                zeros = jnp.zeros(ref.shape[:2] + (pad, ref.shape[3]), _BF16)
                ref[:, :, 0:pad, :] = zeros
                ref[:, :, pad + T:pad + T + pad, :] = zeros
        for r0 in range(0, T, PROJ_TILE):
            rows = slice(r0, r0 + PROJ_TILE)
            x = x_ref[s, rows, :]
            hn = (x * _rms_scale(x) * gn1_ref[...]) * (1.0 + scale) + shift
            hn = hn.astype(_BF16)
            z_ref[rows, 0:C_UC] = _dot(hn, wmain_ref[...])
            z_ref[rows, C_UC:MIX_W] = _dot(hn, wtail_ref[...])
            q = z_ref[rows, C_QA:C_QA + ATT_Q]
            k = z_ref[rows, C_KA:C_KA + ATT_KV]
            q = q * _group_rms_scale(q, hones_ref[...], HEAD_DIM) * gqn_ref[...]
            k = k * _group_rms_scale(k, hones_ref[...], HEAD_DIM) * gkn_ref[...]
            v = z_ref[rows, C_VA:C_VA + ATT_KV]
            if latent:
                cos_ref, sin_ref = latent_refs[3], latent_refs[4]
                cos = jnp.concatenate([cos_ref[rows, :]] * (ATT_Q // ATT_KV), axis=-1)
                sin = jnp.concatenate([sin_ref[rows, :]] * (ATT_Q // ATT_KV), axis=-1)
                q = _rope(q, cos, sin)
                k = _rope(k, cos_ref[rows, :], sin_ref[rows, :])
            else:
                out_refs[3][s, rows, :] = k
                out_refs[4][s, rows, :] = v
            qr_ref[rows, :] = (q * (HEAD_DIM ** -0.5)).astype(_BF16)
            _store_split_kv(k, v, ks_ref, vs_ref, slice(pad + r0, pad + r0 + PROJ_TILE))

    for s, (z_ref, qr_ref, ks_ref, vs_ref, la_ref, o_ref, st_ref, upad_ref) in enumerate(per_seq):
        if latent:
            _attention_latent(T, layer, qr_ref, ks_ref, vs_ref, one(latent_refs[0], s), one(latent_refs[1], s),
                              sink_ref, one(out_refs[0], s))
        else:
            _attention_ctx(T, layer, qr_ref, ks_ref, vs_ref, sink_ref, one(out_refs[0], s))

    worst = None
    for s, (z_ref, qr_ref, ks_ref, vs_ref, la_ref, o_ref, st_ref, upad_ref) in enumerate(per_seq):
        if latent:
            st_ref[...] = latent_refs[2][s]
        else:
            st_ref[...] = jnp.zeros(st_ref.shape, _F32)
        span = _gla_prepare(T, z_ref, la_ref, o_ref, wg2_ref, bg2_ref, tri_ref)
        worst = span if worst is None else jnp.maximum(worst, span)
    _gla_run(T, [(z_ref, la_ref, o_ref, st_ref) for z_ref, _, _, _, la_ref, o_ref, st_ref, _ in per_seq],
             worst, a_ref, wg2_ref, bg2_ref, latent)

    for s, (z_ref, qr_ref, ks_ref, vs_ref, la_ref, o_ref, st_ref, upad_ref) in enumerate(per_seq):
        _gla_finish(T, z_ref, o_ref, ggla_ref, vones_ref, one(out_refs[1], s))
        if not latent:
            for d in range(2):
                for pair in range(GLA_HEADS // 2):
                    out_refs[5][s, d, pair] = st_ref[d, pair].T
        _pool(T, z_ref, upad_ref, wpool_ref, pscale_ref, bands_ref, icnt_ref, one(out_refs[2], s))


def _merge(x, mod_ref, oa, ob, oc, gn1_ref, wgate_ref, wa_ref, wb_ref, wc_ref, wout_ref):
    mod = lambda i: mod_ref[0, :, i * D_MODEL:(i + 1) * D_MODEL]
    hn = (x * _rms_scale(x) * gn1_ref[...]) * (1.0 + mod(1)) + mod(0)
    gates = jax.nn.sigmoid(_dot(hn.astype(_BF16), wgate_ref[...]))
    mixed = (gates[:, 0:D_MODEL] * _dot(oa, wa_ref[...])
             + gates[:, D_MODEL:2 * D_MODEL] * _dot(ob, wb_ref[...])
             + gates[:, 2 * D_MODEL:3 * D_MODEL] * _dot(oc, wc_ref[...]))
    return x + mod(2) * _dot(mixed.astype(_BF16), wout_ref[...])


def _ffn(x, mod_ref, gn2_ref, wfg_ref, wfu_ref, wfd_ref):
    mod = lambda i: mod_ref[0, :, i * D_MODEL:(i + 1) * D_MODEL]
    hn = ((x * _rms_scale(x) * gn2_ref[...]) * (1.0 + mod(4)) + mod(3)).astype(_BF16)
    h = _silu(_dot(hn, wfg_ref[...])) * _dot(hn, wfu_ref[...])
    return x + mod(5) * _dot(h.astype(_BF16), wfd_ref[...])


def _mix_latent_kernel(T, layer_ref, *refs):
    params, refs = refs[:N_MIX_PARAMS], refs[N_MIX_PARAMS:]
    _mix_body(True, T, layer_ref[0], params, refs[:5], refs[5:8], refs[8:])


def _mix_ctx_kernel(T, layer_ref, *refs):
    params, refs = refs[:N_MIX_PARAMS], refs[N_MIX_PARAMS + 3:]
    _mix_body(False, T, layer_ref[0], params, None, refs[:6], refs[6:])


def _layer_spec(shape):
    zeros = (0,) * len(shape)
    return pl.BlockSpec((None,) + tuple(shape), lambda i, layer: (layer[0],) + zeros,
                        pipeline_mode=pl.Buffered(1))


def _const_spec(blk):
    return pl.BlockSpec(blk, lambda i, layer: (0,) * len(blk), pipeline_mode=pl.Buffered(1))


def _mix_params(x, x_spec, mod_spec, mod_all, pw):
    bands = _pool_bands(x.shape[1])
    specs = [
        x_spec, mod_spec,
        _layer_spec((1, D_MODEL)),
        _layer_spec((D_MODEL, C_UC)),
        _layer_spec((D_MODEL, MIX_W - C_UC)),
        _layer_spec((1, ATT_Q)), _layer_spec((1, ATT_KV)),
        pl.BlockSpec(memory_space=pltpu.SMEM),
        _layer_spec((GL_PAD, 2 * GLA_QK)), _layer_spec((1, 2 * GLA_QK)), _layer_spec((1, GLA_VW)),
        _layer_spec((POOL_GROUPS // 2, 2 * POOL_GROUP_DIM, 2 * POOL_GROUP_DIM)), _layer_spec((1, POOL_W)),
        _const_spec((MXU_TILE, MXU_TILE)), _const_spec((MXU_TILE, MXU_TILE)),
        _const_spec(bands.shape), _const_spec((x.shape[1], POOL_W)), _const_spec((2, GLA_BLOCK, GLA_BLOCK)),
    ]
    args = [x, mod_all, pw["g_norm1"], pw["w_main"], pw["w_tail"], pw["g_qn"], pw["g_kn"], pw["att_sink"], pw["w_gate2"],
            pw["b_gate2"], pw["g_gla_out"], pw["w_pool"], pw["pool_scale"], pw["head_ones"], pw["gla_ones"], bands,
            _pool_inverse_counts(x.shape[1]), _scan_masks()]
    assert len(specs) == len(args) == N_MIX_PARAMS
    return specs, args


def _mix_scratch(S, T, kv_rows):
    scratch = [
        pltpu.VMEM((S, T, MIX_W), _F32),
        pltpu.VMEM((S, T, ATT_Q), _BF16),
        pltpu.VMEM((S, ATT_KV_HEADS, 2, kv_rows, 2 * HEAD_DIM), _BF16),
        pltpu.VMEM((S, ATT_KV_HEADS, 2, kv_rows, 4 * HEAD_DIM), _BF16),
        pltpu.VMEM((S, T, 2 * GLA_QK), _F32),
        pltpu.VMEM((S, T, GLA_VW), _F32),
        pltpu.VMEM((S, 2, GLA_HEADS // 2, GLA_DV, 2 * GLA_DK), _F32),
        pltpu.VMEM((GLA_HEADS + 2, GLA_CHUNK, GLA_QK), _F32),
        pltpu.VMEM((S, T + 2 * _pool_halo(T), POOL_W), _BF16),
    ]
    assert len(scratch) == N_MIX_SCRATCH
    return scratch


def _mix_latent_call(layer, x, mod_all, pw, cache_k, cache_v, st0, cos, sin):
    B, T, _ = x.shape
    per_seq = lambda blk: pl.BlockSpec(blk, lambda b, layer: (b,) + (0,) * (len(blk) - 1))
    mod_spec = pl.BlockSpec((None, 1, 1, 6 * D_MODEL), lambda b, layer: (layer[0], b + 1, 0, 0))
    in_specs, args = _mix_params(x, per_seq((1, T, D_MODEL)), mod_spec, mod_all, pw)
    P = cache_k.shape[2]
    cache_spec = pl.BlockSpec((1, 1, P, ATT_KV), lambda b, layer: (b, layer[0], 0, 0))
    in_specs += [cache_spec, cache_spec,
                 pl.BlockSpec((1, None, 2, GLA_HEADS // 2, GLA_DV, 2 * GLA_DK),
                              lambda b, layer: (b, layer[0], 0, 0, 0, 0)),
                 _const_spec((T, ATT_KV)), _const_spec((T, ATT_KV))]
    args += [cache_k, cache_v, st0, cos, sin]
    widths = (ATT_Q, GLA_VW, POOL_W)
    return pl.pallas_call(
        functools.partial(_mix_latent_kernel, T),
        grid_spec=pltpu.PrefetchScalarGridSpec(
            num_scalar_prefetch=1, grid=(B,), in_specs=in_specs,
            out_specs=[per_seq((1, T, w)) for w in widths],
            scratch_shapes=_mix_scratch(1, T, T + 2 * ATT_BLOCK)),
        out_shape=[jax.ShapeDtypeStruct((B, T, w), _BF16) for w in widths],
        compiler_params=pltpu.CompilerParams(dimension_semantics=("arbitrary",), vmem_limit_bytes=VMEM_LIMIT),
        name="mix_latent",
    )(layer, *args)


def _mix_ctx_call(layer, x, mod_all, pw, stacked):
    B, T, _ = x.shape
    S = CTX_SEQS_PER_STEP
    assert B % S == 0
    per_seq = lambda blk: pl.BlockSpec(blk, lambda b, layer: (b,) + (0,) * (len(blk) - 1))
    mod_spec = pl.BlockSpec((None, 1, 1, 6 * D_MODEL), lambda b, layer: (layer[0], 0, 0, 0))
    in_specs, args = _mix_params(x, per_seq((S, T, D_MODEL)), mod_spec, mod_all, pw)
    widths = (ATT_Q, GLA_VW, POOL_W)
    n_in = 1 + len(args)
    aliases = {n_in + j: len(widths) + j for j in range(len(stacked))}
    in_specs += [pl.BlockSpec(memory_space=pl.ANY)] * len(stacked)
    args += list(stacked)
    at_layer = lambda blk: pl.BlockSpec((S, None) + blk, lambda b, layer: (b, layer[0]) + (0,) * len(blk))
    out_specs = [per_seq((S, T, w)) for w in widths] + [
        at_layer((T, ATT_KV)), at_layer((T, ATT_KV)), at_layer((2, GLA_HEADS // 2, 2 * GLA_DK, GLA_DV))]
    out_shape = ([jax.ShapeDtypeStruct((B, T, w), _BF16) for w in widths]
                 + [jax.ShapeDtypeStruct(a.shape, a.dtype) for a in stacked])
    return pl.pallas_call(
        functools.partial(_mix_ctx_kernel, T),
        grid_spec=pltpu.PrefetchScalarGridSpec(
            num_scalar_prefetch=1, grid=(B // S,), in_specs=in_specs, out_specs=out_specs,
            scratch_shapes=_mix_scratch(S, T, T)),
        out_shape=out_shape,
        input_output_aliases=aliases,
        compiler_params=pltpu.CompilerParams(dimension_semantics=("arbitrary",), vmem_limit_bytes=VMEM_LIMIT),
        name="mix_ctx",
    )(layer, *args)


def _post_kernel(layer_ref, x_ref, mod_ref, oa_ref, ob_ref, oc_ref, gn1_ref, gn2_ref, wgate_ref, wa_ref, wb_ref,
                 wc_ref, wout_ref, wfg_ref, wfu_ref, wfd_ref, out_ref):
    x = _merge(x_ref[...], mod_ref, oa_ref[...], ob_ref[...], oc_ref[...], gn1_ref, wgate_ref, wa_ref, wb_ref,
               wc_ref, wout_ref)
    out_ref[...] = _ffn(x, mod_ref, gn2_ref, wfg_ref, wfu_ref, wfd_ref)


def _post_call(layer, x2d, mod_all, oa, ob, oc, pw, tiles_per_seq):
    row = lambda w: pl.BlockSpec((POST_TILE, w), lambda i, layer: (i, 0))
    if tiles_per_seq is None:
        mod_spec = pl.BlockSpec((None, 1, 1, 6 * D_MODEL), lambda i, layer: (layer[0], 0, 0, 0))
    else:
        mod_spec = pl.BlockSpec((None, 1, 1, 6 * D_MODEL),
                                lambda i, layer: (layer[0], 1 + i // tiles_per_seq, 0, 0))
    weights = [(pw["g_norm1"], (1, D_MODEL)), (pw["g_norm2"], (1, D_MODEL)), (pw["w_gates"], (D_MODEL, GATE_W)),
               (pw["w_br_a"], (ATT_Q, D_MODEL)), (pw["w_br_b"], (GLA_VW, D_MODEL)),
               (pw["w_br_c"], (POOL_W, D_MODEL)), (pw["w_out"], (D_MODEL, D_MODEL)),
               (pw["w_ff_gate"], (D_MODEL, D_FF)), (pw["w_ff_up"], (D_MODEL, D_FF)),
               (pw["w_ff_down"], (D_FF, D_MODEL))]
    in_specs = ([row(D_MODEL), mod_spec, row(ATT_Q), row(GLA_VW), row(POOL_W)]
                + [_layer_spec(shape) for _, shape in weights])
    return pl.pallas_call(
        _post_kernel,
        grid_spec=pltpu.PrefetchScalarGridSpec(
            num_scalar_prefetch=1, grid=(x2d.shape[0] // POST_TILE,), in_specs=in_specs, out_specs=row(D_MODEL)),
        out_shape=jax.ShapeDtypeStruct(x2d.shape, _F32),
        input_output_aliases={1: 0},
        compiler_params=pltpu.CompilerParams(dimension_semantics=("arbitrary",), vmem_limit_bytes=VMEM_LIMIT),
        name="post",
    )(layer, x2d, mod_all, oa, ob, oc, *[a for a, _ in weights])


def _rope_tables(T):
    quarter = HEAD_DIM // 4
    inv_freq = ROPE_BASE ** (-np.arange(quarter, dtype=np.float32) / quarter)
    pos = np.arange(T)
    ang_row = (pos // GRID_W).astype(np.float32)[:, None] * inv_freq[None, :]
    ang_col = (pos % GRID_W).astype(np.float32)[:, None] * inv_freq[None, :]
    cos = np.concatenate([np.cos(ang_row)] * 2 + [np.cos(ang_col)] * 2, axis=-1)
    sin = np.concatenate([-np.sin(ang_row), np.sin(ang_row), -np.sin(ang_col), np.sin(ang_col)], axis=-1)
    return (jnp.asarray(np.tile(cos, (1, ATT_KV_HEADS)), _F32), jnp.asarray(np.tile(sin, (1, ATT_KV_HEADS)), _F32))


def _prepare_weights(w_in, g_qn, g_kn, att_sink, w_gate2, b_gate2, g_gla_out, w_pool, pool_scale, w_br_a,
                     w_br_b, w_br_c, w_out, g_norm1, g_norm2, w_ff_gate, w_ff_up, w_ff_down):
    o_gl = ATT_Q + 2 * ATT_KV + 2 * GLA_QK + 2 * GLA_VW
    o_uc = o_gl + 2 * GLA_RANK
    o_gate = o_uc + POOL_W
    assert o_gl == C_UC
    w_main = w_in[:, :, :o_gl].astype(_BF16)
    w_tail = jnp.concatenate(
        [w_in[:, :, o_uc:o_gate], w_in[:, :, o_gl:o_uc],
         jnp.zeros((DEPTH, D_MODEL, GL_PAD - 2 * GLA_RANK), w_in.dtype)], axis=2).astype(_BF16)
    w_gates = w_in[:, :, o_gate:].astype(_BF16)
    wg2 = jnp.zeros((DEPTH, GL_PAD, 2 * GLA_QK), _F32)
    wg2 = wg2.at[:, 0:GLA_RANK, 0:GLA_QK].set(w_gate2[:, 0])
    wg2 = wg2.at[:, GLA_RANK:2 * GLA_RANK, GLA_QK:].set(w_gate2[:, 1])
    w_pool2 = jnp.zeros((DEPTH, POOL_GROUPS // 2, 2 * POOL_GROUP_DIM, 2 * POOL_GROUP_DIM), _F32)
    w_pool2 = w_pool2.at[:, :, :POOL_GROUP_DIM, :POOL_GROUP_DIM].set(w_pool[:, 0::2])
    w_pool2 = w_pool2.at[:, :, POOL_GROUP_DIM:, POOL_GROUP_DIM:].set(w_pool[:, 1::2])
    vec = lambda a: a.reshape(DEPTH, 1, -1)
    group_ones = lambda n, width: jnp.asarray(
        (np.arange(n)[:, None] // width) == (np.arange(n)[None, :] // width), _BF16)
    return {
        "head_ones": group_ones(MXU_TILE, HEAD_DIM),
        "gla_ones": group_ones(MXU_TILE, GLA_DV),
        "w_main": w_main,
        "w_tail": w_tail,
        "w_gates": w_gates,
        "g_qn": vec(jnp.tile(g_qn, (1, ATT_HEADS))),
        "g_kn": vec(jnp.tile(g_kn, (1, ATT_KV_HEADS))),
        "att_sink": att_sink,
        "w_gate2": wg2.astype(_BF16),
        "b_gate2": vec(b_gate2),
        "g_gla_out": vec(jnp.tile(g_gla_out, (1, GLA_HEADS))),
        "w_pool": w_pool2.astype(_BF16),
        "pool_scale": vec(pool_scale),
        "w_br_a": w_br_a.astype(_BF16),
        "w_br_b": w_br_b.astype(_BF16),
        "w_br_c": w_br_c.astype(_BF16),
        "w_out": w_out.astype(_BF16),
        "g_norm1": vec(g_norm1),
        "g_norm2": vec(g_norm2),
        "w_ff_gate": w_ff_gate.astype(_BF16),
        "w_ff_up": w_ff_up.astype(_BF16),
        "w_ff_down": w_ff_down.astype(_BF16),
    }


def kernel(x_prompt, x_sample, c, cache_k, cache_v, state_gla, c_ctx, w_in, g_qn, g_kn, att_sink, w_gate2,
           b_gate2, g_gla_out, w_pool, pool_scale, w_br_a, w_br_b, w_br_c, w_out, g_norm1, g_norm2, w_mod,
           b_mod, w_ff_gate, w_ff_up, w_ff_down):
    B, T, _ = x_prompt.shape
    BL, TL, _ = x_sample.shape
    assert (B * T) % POST_TILE == 0 and TL % POST_TILE == 0 and BL + 1 <= MOD_ROWS
    assert T % PROJ_TILE == 0 and TL % PROJ_TILE == 0
    cv = jnp.concatenate([c_ctx[None, :], c, jnp.zeros((MOD_ROWS - 1 - BL, D_MODEL), _F32)], axis=0)
    mod_all = _modulation(cv, w_mod, b_mod).reshape(DEPTH, MOD_ROWS, 1, 6 * D_MODEL)
    pw = _prepare_weights(w_in, g_qn, g_kn, att_sink, w_gate2, b_gate2, g_gla_out, w_pool, pool_scale, w_br_a,
                          w_br_b, w_br_c, w_out, g_norm1, g_norm2, w_ff_gate, w_ff_up, w_ff_down)
    cos, sin = _rope_tables(TL)
    P = cache_k.shape[2]
    latent_ctx = (cache_k.reshape(BL, DEPTH, P, ATT_KV), cache_v.reshape(BL, DEPTH, P, ATT_KV),
                  jnp.swapaxes(state_gla.reshape(BL, DEPTH, 2, GLA_HEADS // 2, 2 * GLA_DK, GLA_DV), -1, -2),
                  cos, sin)

    def layer_step(l, carry):
        yp, ys, new_k, new_v, new_st = carry
        layer = jnp.full((1,), l, jnp.int32)
        oa, ob, oc, new_k, new_v, new_st = _mix_ctx_call(layer, yp, mod_all, pw, (new_k, new_v, new_st))
        yp = _post_call(layer, yp.reshape(B * T, D_MODEL), mod_all, oa.reshape(B * T, -1), ob.reshape(B * T, -1),
                        oc.reshape(B * T, -1), pw, None).reshape(B, T, D_MODEL)
        oa, ob, oc = _mix_latent_call(layer, ys, mod_all, pw, *latent_ctx)
        ys = _post_call(layer, ys.reshape(BL * TL, D_MODEL), mod_all, oa.reshape(BL * TL, -1),
                        ob.reshape(BL * TL, -1), oc.reshape(BL * TL, -1), pw,
                        TL // POST_TILE).reshape(BL, TL, D_MODEL)
        return yp, ys, new_k, new_v, new_st

    init = (x_prompt, x_sample,
            jnp.zeros((B, DEPTH, T, ATT_KV), _F32), jnp.zeros((B, DEPTH, T, ATT_KV), _F32),
            jnp.zeros((B, DEPTH, 2, GLA_HEADS // 2, 2 * GLA_DK, GLA_DV), _F32))
    yp, ys, new_k, new_v, new_st = lax.fori_loop(0, DEPTH, layer_step, init)
    return (yp, ys, new_k.reshape(B, DEPTH, T, ATT_KV_HEADS, HEAD_DIM),
            new_v.reshape(B, DEPTH, T, ATT_KV_HEADS, HEAD_DIM),
            new_st.reshape(B, DEPTH, 2, GLA_HEADS, GLA_DK, GLA_DV))
```

```python
import functools

import jax
import jax.numpy as jnp
import numpy as np
from jax import lax
from jax.experimental import pallas as pl
from jax.experimental.pallas import tpu as pltpu

D_MODEL = 1024
DEPTH = 4
GRID_W = 64
ATT_HEADS = 8
ATT_KV_HEADS = 2
ATT_GROUP = ATT_HEADS // ATT_KV_HEADS
HEAD_DIM = 64
WINDOW = 128
ATT_BLOCK = 128
ROPE_BASE = 10000.0
GLA_HEADS = 4
GLA_DK = 64
GLA_DV = 128
GLA_RANK = 16
GLA_TAU = 16.0
GLA_CHUNK = 64
POOL_GROUPS = 4
POOL_GROUP_DIM = 128
POOL_WINDOWS = (2, 4, 8, 16)
D_FF = 2816
ATT_Q = ATT_HEADS * HEAD_DIM
ATT_KV = ATT_KV_HEADS * HEAD_DIM
GLA_QK = GLA_HEADS * GLA_DK
GLA_VW = GLA_HEADS * GLA_DV
POOL_W = POOL_GROUPS * POOL_GROUP_DIM
EPS = 1e-6
NEG = -1e30

C_QA = 0
C_KA = C_QA + ATT_Q
C_VA = C_KA + ATT_KV
C_QB = C_VA + ATT_KV
C_KB = C_QB + GLA_QK
C_VB = C_KB + GLA_QK
C_RB = C_VB + GLA_VW
C_UC = C_RB + GLA_VW
C_GL = C_UC + POOL_W
LANES = 128
SUBLANES = 8
MXU_TILE = 256
GL_PAD = LANES
MIX_W = C_GL + GL_PAD
GATE_W = 3 * D_MODEL

POST_TILE = 512
PROJ_TILE = 256
CTX_SEQS_PER_STEP = 2
POOL_TILE = 256
POOL_HALO = LANES
MOD_ROWS = SUBLANES
MOD_TILE = 3072
GLA_BLOCK = 256
GLA_MAX_EXPONENT = 80.0
VMEM_LIMIT = 56 * 1024 * 1024

_F32 = jnp.float32
_BF16 = jnp.bfloat16


def _dot(a, b):
    return jnp.dot(a, b, preferred_element_type=_F32)


def _dot_nt(a, b):
    return lax.dot_general(a, b, (((1,), (1,)), ((), ())), preferred_element_type=_F32)


def _dot_tn(a, b):
    return lax.dot_general(a, b, (((0,), (0,)), ((), ())), preferred_element_type=_F32)


def _rms_scale(x):
    return lax.rsqrt(jnp.mean(x * x, axis=-1, keepdims=True) + EPS)


def _group_rms_scale(x, group_ones, width):
    sq = (x * x).astype(_BF16)
    n, blk = x.shape[-1], group_ones.shape[0]
    if n <= blk:
        sums = _dot(sq, group_ones[0:n, 0:n])
    else:
        sums = jnp.concatenate([_dot(sq[:, i:i + blk], group_ones) for i in range(0, n, blk)], axis=-1)
    return lax.rsqrt(sums * (1.0 / width) + EPS)


def _log_sigmoid(x):
    return jnp.minimum(x, 0.0) - jnp.log(1.0 + jnp.exp(-jnp.abs(x)))


def _silu(x):
    return x * jax.nn.sigmoid(x)


def _rope(x, cos, sin_signed):
    n = x.shape[-1]
    lane = lax.broadcasted_iota(jnp.int32, x.shape, 1)
    up = pltpu.roll(x, n - HEAD_DIM // 4, axis=1)
    down = pltpu.roll(x, HEAD_DIM // 4, axis=1)
    partner = jnp.where((lane & (HEAD_DIM // 2 - 1)) < HEAD_DIM // 4, up, down)
    return x * cos + partner * sin_signed


def _mod_kernel(cv_ref, w_ref, b_ref, out_ref):
    s = _silu(cv_ref[...]).astype(_BF16)
    out_ref[0] = _dot(s, w_ref[0].astype(_BF16)) + b_ref[0]


def _modulation(cv, w_mod, b_mod):
    n_col = (6 * D_MODEL) // MOD_TILE
    return pl.pallas_call(
        _mod_kernel,
        grid=(DEPTH, n_col),
        in_specs=[
            pl.BlockSpec((MOD_ROWS, D_MODEL), lambda l, j: (0, 0)),
            pl.BlockSpec((1, D_MODEL, MOD_TILE), lambda l, j: (l, 0, j)),
            pl.BlockSpec((1, 1, MOD_TILE), lambda l, j: (l, 0, j)),
        ],
        out_specs=pl.BlockSpec((1, MOD_ROWS, MOD_TILE), lambda l, j: (l, 0, j)),
        out_shape=jax.ShapeDtypeStruct((DEPTH, MOD_ROWS, 6 * D_MODEL), _F32),
        compiler_params=pltpu.CompilerParams(vmem_limit_bytes=VMEM_LIMIT),
        name="modulation",
    )(cv, w_mod, b_mod.reshape(DEPTH, 1, 6 * D_MODEL))


def _split_heads(x):
    low = lax.broadcasted_iota(jnp.int32, x.shape, 1) < HEAD_DIM
    swapped = pltpu.roll(x, HEAD_DIM, axis=1)
    zero = jnp.zeros_like(x)
    return ((jnp.where(low, x, zero), jnp.where(low, zero, swapped)),
            (jnp.where(low, swapped, zero), jnp.where(low, zero, x)))


def _store_split_kv(k, v, ks_ref, vs_ref, rows):
    ones = jnp.ones_like(v)
    for kv, (k_sides, v_sides, one_sides) in enumerate(zip(_split_heads(k), _split_heads(v), _split_heads(ones))):
        for side in range(2):
            ks_ref[kv, side, rows, :] = k_sides[side].astype(_BF16)
            vs_ref[kv, side, rows, :] = jnp.concatenate([v_sides[side], one_sides[side]], axis=-1).astype(_BF16)


def _pair_softmax_av(qp, keys, values, masks, sink_even, sink_odd):
    m = qp.shape[0]
    scores = []
    for (k_left, k_right), mask in zip(keys, masks):
        s_even, s_odd = _dot_nt(qp, k_left), _dot_nt(qp, k_right)
        if mask is not None:
            s_even, s_odd = jnp.where(mask, s_even, NEG), jnp.where(mask, s_odd, NEG)
        scores.append((s_even, s_odd))
    m_even = jnp.full((m, 1), sink_even, _F32)
    m_odd = jnp.full((m, 1), sink_odd, _F32)
    for s_even, s_odd in scores:
        m_even = jnp.maximum(m_even, jnp.max(s_even, axis=-1, keepdims=True))
        m_odd = jnp.maximum(m_odd, jnp.max(s_odd, axis=-1, keepdims=True))
    res = None
    for (s_even, s_odd), (w_left, w_right) in zip(scores, values):
        r = (_dot(jnp.exp(s_even - m_even).astype(_BF16), w_left)
             + _dot(jnp.exp(s_odd - m_odd).astype(_BF16), w_right))
        res = r if res is None else res + r
    pair = 2 * HEAD_DIM
    low = lax.broadcasted_iota(jnp.int32, (m, pair), 1) < HEAD_DIM
    den = res[:, pair:] + jnp.where(low, jnp.exp(sink_even - m_even), jnp.exp(sink_odd - m_odd))
    return res[:, :pair] / den


def _attention_ctx(T, layer, qr_ref, ks_ref, vs_ref, sink_ref, oa_ref):
    pair = 2 * HEAD_DIM
    for kv in range(ATT_KV_HEADS):
        keys = [(ks_ref[kv, 0], ks_ref[kv, 1])]
        values = [(vs_ref[kv, 0], vs_ref[kv, 1])]
        for j in range(ATT_GROUP // 2):
            head = kv * ATT_GROUP + 2 * j
            cols = slice(head * HEAD_DIM, head * HEAD_DIM + pair)
            o = _pair_softmax_av(qr_ref[:, cols], keys, values, [None], sink_ref[layer, head],
                                 sink_ref[layer, head + 1])
            oa_ref[0, :, cols] = o.astype(_BF16)


def _attention_latent(T, layer, qr_ref, ks_ref, vs_ref, kc_ref, vc_ref, sink_ref, oa_ref):
    pair = 2 * HEAD_DIM
    span = 3 * ATT_BLOCK
    kc, vc = kc_ref[0, 0], vc_ref[0, 0]
    ones = jnp.ones_like(vc)
    ctx_keys = [tuple(side.astype(_BF16) for side in sides) for sides in _split_heads(kc)]
    ctx_values = [tuple(jnp.concatenate([v_side, one_side], axis=-1).astype(_BF16)
                        for v_side, one_side in zip(v_sides, one_sides))
                  for v_sides, one_sides in zip(_split_heads(vc), _split_heads(ones))]

    def block(i, carry):
        q_rows = pl.ds(pl.multiple_of(i * ATT_BLOCK, ATT_BLOCK), ATT_BLOCK)
        k_rows = pl.ds(pl.multiple_of(i * ATT_BLOCK, ATT_BLOCK), span)
        q_pos = i * ATT_BLOCK + lax.broadcasted_iota(jnp.int32, (ATT_BLOCK, span), 0)
        k_pos = (i - 1) * ATT_BLOCK + lax.broadcasted_iota(jnp.int32, (ATT_BLOCK, span), 1)
        valid = (jnp.abs(k_pos - q_pos) <= WINDOW) & (k_pos >= 0) & (k_pos < T)
        for kv in range(ATT_KV_HEADS):
            keys = [(ks_ref[kv, 0, k_rows, :], ks_ref[kv, 1, k_rows, :]), ctx_keys[kv]]
            values = [(vs_ref[kv, 0, k_rows, :], vs_ref[kv, 1, k_rows, :]), ctx_values[kv]]
            for j in range(ATT_GROUP // 2):
                head = kv * ATT_GROUP + 2 * j
                cols = slice(head * HEAD_DIM, head * HEAD_DIM + pair)
                o = _pair_softmax_av(qr_ref[q_rows, cols], keys, values, [valid, None], sink_ref[layer, head],
                                     sink_ref[layer, head + 1])
                oa_ref[0, q_rows, cols] = o.astype(_BF16)
        return carry

    lax.fori_loop(0, T // ATT_BLOCK, block, 0, unroll=4)


def _gla_chunk(z_ref, la_ref, o_ref, st_ref, a_ref, direction, start):
    C = GLA_CHUNK
    rows = pl.ds(pl.multiple_of(start, C), C)
    q = z_ref[rows, C_QB:C_QB + GLA_QK] * (GLA_DK ** -0.5)
    k = z_ref[rows, C_KB:C_KB + GLA_QK]
    v = z_ref[rows, C_VB:C_VB + GLA_VW].astype(_BF16)
    la = la_ref[rows, direction * GLA_QK:(direction + 1) * GLA_QK]
    la_hi = la.astype(_BF16)
    la_lo = (la - la_hi.astype(_F32)).astype(_BF16)
    ti = lax.broadcasted_iota(jnp.int32, (C, C), 0)
    si = lax.broadcasted_iota(jnp.int32, (C, C), 1)
    causal = (si <= ti) if direction == 0 else (si >= ti)
    tri = jnp.where(causal, 1.0, 0.0).astype(_BF16)
    b = _dot(tri, la_hi) + _dot(tri, la_lo)
    end = C - 1 if direction == 0 else 0
    b_end = b[end:end + 1]
    q_in = (q * jnp.exp(b)).astype(_BF16)
    k_out = (k * jnp.exp(b_end - b)).astype(_BF16)
    e_end = jnp.exp(b_end)
    ones = jnp.ones((8, GLA_DK), _BF16)

    def row_group(g, carry):
        base = pl.multiple_of(g * 8, 8)
        b8 = a_ref[GLA_HEADS, pl.ds(base, 8), :]
        q8 = a_ref[GLA_HEADS + 1, pl.ds(base, 8), :]
        s_idx = lax.broadcasted_iota(jnp.int32, (C, 1), 0)
        rows_h = [[] for _ in range(GLA_HEADS)]
        for j in range(8):
            ok = (s_idx <= base + j) if direction == 0 else (s_idx >= base + j)
            decay = jnp.exp(jnp.where(ok, b8[j:j + 1] - b, NEG))
            p = (q8[j:j + 1] * k * decay).astype(_BF16)
            for h in range(GLA_HEADS):
                rows_h[h].append(_dot_nt(ones, p[:, h * GLA_DK:(h + 1) * GLA_DK])[0:1])
        for h in range(GLA_HEADS):
            a_ref[h, pl.ds(base, 8), 0:C] = jnp.concatenate(rows_h[h], axis=0)
        return carry

    a_ref[GLA_HEADS] = b
    a_ref[GLA_HEADS + 1] = q
    lax.fori_loop(0, C // 8, row_group, 0)
    outs = []
    for pair in range(GLA_HEADS // 2):
        s_pair = st_ref[direction, pair]
        s_next = []
        for h in (2 * pair, 2 * pair + 1):
            kc = slice(h * GLA_DK, (h + 1) * GLA_DK)
            vc = slice(h * GLA_DV, (h + 1) * GLA_DV)
            s_t = s_pair[:, (h % 2) * GLA_DK:(h % 2 + 1) * GLA_DK]
            outs.append(_dot_nt(q_in[:, kc], s_t.astype(_BF16)) + _dot(a_ref[h, :, 0:C].astype(_BF16), v[:, vc]))
            s_next.append(s_t * e_end[:, kc] + _dot_tn(v[:, vc], k_out[:, kc]))
        st_ref[direction, pair] = jnp.concatenate(s_next, axis=-1)
    o_ref[rows, :] += jnp.concatenate(outs, axis=-1)


def _gla_block(z_ref, la_ref, o_ref, st_ref, direction, start, use_state):
    NB = GLA_BLOCK
    fwd = direction == 0
    rows = slice(start, start + NB) if isinstance(start, int) else pl.ds(pl.multiple_of(start, NB), NB)
    q = z_ref[rows, C_QB:C_QB + GLA_QK] * (GLA_DK ** -0.5)
    k = z_ref[rows, C_KB:C_KB + GLA_QK]
    v = z_ref[rows, C_VB:C_VB + GLA_VW].astype(_BF16)
    b = la_ref[rows, direction * GLA_QK:(direction + 1) * GLA_QK]
    end, mid = (NB - 1, NB // 2 - 1) if fwd else (0, NB // 2)
    b_end = b[end:end + 1]
    c = b - b[mid:mid + 1]
    q_c = (q * jnp.exp(c)).astype(_BF16)
    k_c = (k * jnp.exp(-c)).astype(_BF16)
    k_fin = (k * jnp.exp(b_end - b)).astype(_BF16)
    if use_state:
        q_in = (q * jnp.exp(b)).astype(_BF16)
        e_all = jnp.exp(b_end)
    ti = lax.broadcasted_iota(jnp.int32, (NB, NB), 0)
    si = lax.broadcasted_iota(jnp.int32, (NB, NB), 1)
    causal = (si <= ti) if fwd else (si >= ti)
    tile = 2 * GLA_DK
    low = lax.broadcasted_iota(jnp.int32, (NB, tile), 1) < GLA_DK
    zero = jnp.zeros((NB, tile), _BF16)
    pick = lambda x, parity: jnp.where(low, x, zero) if parity == 0 else jnp.where(low, zero, x)
    outs = []
    for pair in range(GLA_HEADS // 2):
        lanes = slice(pair * tile, (pair + 1) * tile)
        if use_state:
            s_pair = st_ref[direction, pair]
            s_bf = s_pair.astype(_BF16)
        s_new = None
        for parity in range(2):
            h = 2 * pair + parity
            vc = slice(h * GLA_DV, (h + 1) * GLA_DV)
            a = jnp.where(causal, _dot_nt(q_c[:, lanes], pick(k_c[:, lanes], parity)), 0.0)
            o_h = _dot(a.astype(_BF16), v[:, vc])
            if use_state:
                o_h = o_h + _dot_nt(pick(q_in[:, lanes], parity), s_bf)
            outs.append(o_h)
            upd = _dot_tn(v[:, vc], pick(k_fin[:, lanes], parity))
            s_new = upd if s_new is None else s_new + upd
        if use_state:
            s_new = s_new + s_pair * e_all[:, lanes]
        st_ref[direction, pair] = s_new
    o_ref[rows, :] += jnp.concatenate(outs, axis=-1)


def _log_decay(z_ref, rows, wg2_ref, bg2_ref):
    pre = _dot(z_ref[rows, C_GL:C_GL + GL_PAD].astype(_BF16), wg2_ref[...]) + bg2_ref[...]
    return _log_sigmoid(pre) * (1.0 / GLA_TAU)


def _gla_prepare(T, z_ref, la_ref, o_ref, wg2_ref, bg2_ref, tri_ref):
    NB = GLA_BLOCK
    worst = None
    for r0 in range(0, T, NB):
        rows = slice(r0, r0 + NB)
        la = _log_decay(z_ref, rows, wg2_ref, bg2_ref)
        la_hi = la.astype(_BF16)
        la_lo = (la - la_hi.astype(_F32)).astype(_BF16)
        for d in range(2):
            cols = slice(d * GLA_QK, (d + 1) * GLA_QK)
            b = _dot(tri_ref[d], la_hi[:, cols]) + _dot(tri_ref[d], la_lo[:, cols])
            la_ref[rows, cols] = b
            first, mid, last = (0, NB // 2 - 1, NB - 1) if d == 0 else (NB - 1, NB // 2, 0)
            span = jnp.max(jnp.maximum(b[first:first + 1] - b[mid:mid + 1], b[mid:mid + 1] - b[last:last + 1]))
            worst = span if worst is None else jnp.maximum(worst, span)
    o_ref[...] = jnp.zeros(o_ref.shape, _F32)
    return worst


def _gla_run(T, seqs, worst, a_ref, wg2_ref, bg2_ref, use_state):
    NB = GLA_BLOCK
    n_chunks = T // GLA_CHUNK
    n_blocks = T // NB
    fast_ok = worst <= GLA_MAX_EXPONENT

    @pl.when(fast_ok)
    def _():
        if n_blocks == 1:
            for direction in range(2):
                for z_ref, la_ref, o_ref, st_ref in seqs:
                    _gla_block(z_ref, la_ref, o_ref, st_ref, direction, 0, use_state)
        else:
            def body(i, carry):
                for z_ref, la_ref, o_ref, st_ref in seqs:
                    _gla_block(z_ref, la_ref, o_ref, st_ref, 0, i * NB, True)
                    _gla_block(z_ref, la_ref, o_ref, st_ref, 1, (n_blocks - 1 - i) * NB, True)
                return carry
            lax.fori_loop(0, n_blocks, body, 0, unroll=2)

    @pl.when(jnp.logical_not(fast_ok))
    def _():
        for z_ref, la_ref, o_ref, st_ref in seqs:
            for r0 in range(0, T, NB):
                la_ref[r0:r0 + NB, :] = _log_decay(z_ref, slice(r0, r0 + NB), wg2_ref, bg2_ref)

            def body(i, carry):
                _gla_chunk(z_ref, la_ref, o_ref, st_ref, a_ref, 0, i * GLA_CHUNK)
                _gla_chunk(z_ref, la_ref, o_ref, st_ref, a_ref, 1, (n_chunks - 1 - i) * GLA_CHUNK)
                return carry
            lax.fori_loop(0, n_chunks, body, 0)


def _gla_finish(T, z_ref, o_ref, ggla_ref, vones_ref, ob_ref):
    for r0 in range(0, T, PROJ_TILE):
        rows = slice(r0, r0 + PROJ_TILE)
        o = o_ref[rows, :]
        y = o * _group_rms_scale(o, vones_ref[...], GLA_DV) * ggla_ref[...]
        ob_ref[0, rows, :] = (y * _silu(z_ref[rows, C_RB:C_RB + GLA_VW])).astype(_BF16)


def _pool_halo(T):
    return POOL_HALO if T > POOL_TILE else 0


def _pool_bands(T):
    halo = _pool_halo(T)
    off = np.arange(POOL_TILE + 2 * halo)[None, :] - halo - np.arange(POOL_TILE)[:, None]
    return jnp.asarray(np.stack([(off >= -(w // 2)) & (off < w - w // 2) for w in POOL_WINDOWS]), _BF16)


def _scan_masks():
    t = np.arange(GLA_BLOCK)
    return jnp.asarray(np.stack([t[None, :] <= t[:, None], t[None, :] >= t[:, None]]), _BF16)


def _pool_inverse_counts(T):
    t = np.arange(T)
    cols = [np.repeat((1.0 / (np.minimum(t - w // 2 + w, T) - np.maximum(t - w // 2, 0)))[:, None],
                      POOL_GROUP_DIM, axis=1) for w in POOL_WINDOWS]
    return jnp.asarray(np.concatenate(cols, axis=1), _F32)


def _pool(T, z_ref, upad_ref, wpool_ref, pscale_ref, bands_ref, icnt_ref, oc_ref):
    halo = _pool_halo(T)
    if halo:
        zeros = jnp.zeros((halo, POOL_W), _BF16)
        upad_ref[0:halo, :] = zeros
        upad_ref[halo + T:halo + T + halo, :] = zeros
    upad_ref[halo:halo + T, :] = z_ref[:, C_UC:C_UC + POOL_W].astype(_BF16)
    span = POOL_TILE + 2 * halo
    group = lambda g: slice(g * POOL_GROUP_DIM, (g + 1) * POOL_GROUP_DIM)
    for jb in range(T // POOL_TILE):
        rows = slice(jb * POOL_TILE, (jb + 1) * POOL_TILE)
        win = slice(jb * POOL_TILE, jb * POOL_TILE + span)
        total = jnp.concatenate([_dot(bands_ref[g], upad_ref[win, group(g)]) for g in range(POOL_GROUPS)], axis=-1)
        pooled = (total * icnt_ref[rows, :] - z_ref[rows, C_UC:C_UC + POOL_W]).astype(_BF16)
        pair = 2 * POOL_GROUP_DIM
        y = jnp.concatenate([_dot(pooled[:, p * pair:(p + 1) * pair], wpool_ref[p])
                             for p in range(POOL_GROUPS // 2)], axis=-1)
        oc_ref[0, rows, :] = (y * pscale_ref[...]).astype(_BF16)


N_MIX_PARAMS = 18
N_MIX_SCRATCH = 9


def _mix_body(latent, T, layer, params, latent_refs, out_refs, scratch):
    (x_ref, mod_ref, gn1_ref, wmain_ref, wtail_ref, gqn_ref, gkn_ref, sink_ref, wg2_ref, bg2_ref, ggla_ref,
     wpool_ref, pscale_ref, hones_ref, vones_ref, bands_ref, icnt_ref, tri_ref) = params
    n_seq = x_ref.shape[0]
    one = lambda ref, s: ref.at[pl.ds(s, 1)]
    a_ref = scratch[N_MIX_SCRATCH - 2]
    per_seq = [tuple(ref.at[s] for ref in scratch[:N_MIX_SCRATCH - 2] + scratch[N_MIX_SCRATCH - 1:])
               for s in range(n_seq)]
    pad = ATT_BLOCK if latent else 0
    shift = mod_ref[0, :, 0:D_MODEL]
    scale = mod_ref[0, :, D_MODEL:2 * D_MODEL]

    for s, (z_ref, qr_ref, ks_ref, vs_ref, la_ref, o_ref, st_ref, upad_ref) in enumerate(per_seq):
        if latent:
            for ref in (ks_ref, vs_ref):
                zeros = jnp.zeros(ref.shape[:2] + (pad, ref.shape[3]), _BF16)
                ref[:, :, 0:pad, :] = zeros
                ref[:, :, pad + T:pad + T + pad, :] = zeros
        for r0 in range(0, T, PROJ_TILE):
            rows = slice(r0, r0 + PROJ_TILE)
            x = x_ref[s, rows, :]
            hn = (x * _rms_scale(x) * gn1_ref[...]) * (1.0 + scale) + shift
            hn = hn.astype(_BF16)
            z_ref[rows, 0:C_UC] = _dot(hn, wmain_ref[...])
            z_ref[rows, C_UC:MIX_W] = _dot(hn, wtail_ref[...])
            q = z_ref[rows, C_QA:C_QA + ATT_Q]
            k = z_ref[rows, C_KA:C_KA + ATT_KV]
            q = q * _group_rms_scale(q, hones_ref[...], HEAD_DIM) * gqn_ref[...]
            k = k * _group_rms_scale(k, hones_ref[...], HEAD_DIM) * gkn_ref[...]
            v = z_ref[rows, C_VA:C_VA + ATT_KV]
            if latent:
                cos_ref, sin_ref = latent_refs[3], latent_refs[4]
                cos = jnp.concatenate([cos_ref[rows, :]] * (ATT_Q // ATT_KV), axis=-1)
                sin = jnp.concatenate([sin_ref[rows, :]] * (ATT_Q // ATT_KV), axis=-1)
                q = _rope(q, cos, sin)
                k = _rope(k, cos_ref[rows, :], sin_ref[rows, :])
            else:
                out_refs[3][s, rows, :] = k
                out_refs[4][s, rows, :] = v
            qr_ref[rows, :] = (q * (HEAD_DIM ** -0.5)).astype(_BF16)
            _store_split_kv(k, v, ks_ref, vs_ref, slice(pad + r0, pad + r0 + PROJ_TILE))

    for s, (z_ref, qr_ref, ks_ref, vs_ref, la_ref, o_ref, st_ref, upad_ref) in enumerate(per_seq):
        if latent:
            _attention_latent(T, layer, qr_ref, ks_ref, vs_ref, one(latent_refs[0], s), one(latent_refs[1], s),
                              sink_ref, one(out_refs[0], s))
        else:
            _attention_ctx(T, layer, qr_ref, ks_ref, vs_ref, sink_ref, one(out_refs[0], s))

    worst = None
    for s, (z_ref, qr_ref, ks_ref, vs_ref, la_ref, o_ref, st_ref, upad_ref) in enumerate(per_seq):
        if latent:
            st_ref[...] = latent_refs[2][s]
        else:
            st_ref[...] = jnp.zeros(st_ref.shape, _F32)
        span = _gla_prepare(T, z_ref, la_ref, o_ref, wg2_ref, bg2_ref, tri_ref)
        worst = span if worst is None else jnp.maximum(worst, span)
    _gla_run(T, [(z_ref, la_ref, o_ref, st_ref) for z_ref, _, _, _, la_ref, o_ref, st_ref, _ in per_seq],
             worst, a_ref, wg2_ref, bg2_ref, latent)

    for s, (z_ref, qr_ref, ks_ref, vs_ref, la_ref, o_ref, st_ref, upad_ref) in enumerate(per_seq):
        _gla_finish(T, z_ref, o_ref, ggla_ref, vones_ref, one(out_refs[1], s))
        if not latent:
            for d in range(2):
                for pair in range(GLA_HEADS // 2):
                    out_refs[5][s, d, pair] = st_ref[d, pair].T
        _pool(T, z_ref, upad_ref, wpool_ref, pscale_ref, bands_ref, icnt_ref, one(out_refs[2], s))


def _merge(x, mod_ref, oa, ob, oc, gn1_ref, wgate_ref, wa_ref, wb_ref, wc_ref, wout_ref):
    mod = lambda i: mod_ref[0, :, i * D_MODEL:(i + 1) * D_MODEL]
    hn = (x * _rms_scale(x) * gn1_ref[...]) * (1.0 + mod(1)) + mod(0)
    gates = jax.nn.sigmoid(_dot(hn.astype(_BF16), wgate_ref[...]))
    mixed = (gates[:, 0:D_MODEL] * _dot(oa, wa_ref[...])
             + gates[:, D_MODEL:2 * D_MODEL] * _dot(ob, wb_ref[...])
             + gates[:, 2 * D_MODEL:3 * D_MODEL] * _dot(oc, wc_ref[...]))
    return x + mod(2) * _dot(mixed.astype(_BF16), wout_ref[...])


def _ffn(x, mod_ref, gn2_ref, wfg_ref, wfu_ref, wfd_ref):
    mod = lambda i: mod_ref[0, :, i * D_MODEL:(i + 1) * D_MODEL]
    hn = ((x * _rms_scale(x) * gn2_ref[...]) * (1.0 + mod(4)) + mod(3)).astype(_BF16)
    h = _silu(_dot(hn, wfg_ref[...])) * _dot(hn, wfu_ref[...])
    return x + mod(5) * _dot(h.astype(_BF16), wfd_ref[...])


def _mix_latent_kernel(T, layer_ref, *refs):
    params, refs = refs[:N_MIX_PARAMS], refs[N_MIX_PARAMS:]
    _mix_body(True, T, layer_ref[0], params, refs[:5], refs[5:8], refs[8:])


def _mix_ctx_kernel(T, layer_ref, *refs):
    params, refs = refs[:N_MIX_PARAMS], refs[N_MIX_PARAMS + 3:]
    _mix_body(False, T, layer_ref[0], params, None, refs[:6], refs[6:])


def _layer_spec(shape):
    zeros = (0,) * len(shape)
    return pl.BlockSpec((None,) + tuple(shape), lambda i, layer: (layer[0],) + zeros,
                        pipeline_mode=pl.Buffered(1))


def _const_spec(blk):
    return pl.BlockSpec(blk, lambda i, layer: (0,) * len(blk), pipeline_mode=pl.Buffered(1))


def _mix_params(x, x_spec, mod_spec, mod_all, pw):
    bands = _pool_bands(x.shape[1])
    specs = [
        x_spec, mod_spec,
        _layer_spec((1, D_MODEL)),
        _layer_spec((D_MODEL, C_UC)),
        _layer_spec((D_MODEL, MIX_W - C_UC)),
        _layer_spec((1, ATT_Q)), _layer_spec((1, ATT_KV)),
        pl.BlockSpec(memory_space=pltpu.SMEM),
        _layer_spec((GL_PAD, 2 * GLA_QK)), _layer_spec((1, 2 * GLA_QK)), _layer_spec((1, GLA_VW)),
        _layer_spec((POOL_GROUPS // 2, 2 * POOL_GROUP_DIM, 2 * POOL_GROUP_DIM)), _layer_spec((1, POOL_W)),
        _const_spec((MXU_TILE, MXU_TILE)), _const_spec((MXU_TILE, MXU_TILE)),
        _const_spec(bands.shape), _const_spec((x.shape[1], POOL_W)), _const_spec((2, GLA_BLOCK, GLA_BLOCK)),
    ]
    args = [x, mod_all, pw["g_norm1"], pw["w_main"], pw["w_tail"], pw["g_qn"], pw["g_kn"], pw["att_sink"], pw["w_gate2"],
            pw["b_gate2"], pw["g_gla_out"], pw["w_pool"], pw["pool_scale"], pw["head_ones"], pw["gla_ones"], bands,
            _pool_inverse_counts(x.shape[1]), _scan_masks()]
    assert len(specs) == len(args) == N_MIX_PARAMS
    return specs, args


def _mix_scratch(S, T, kv_rows):
    scratch = [
        pltpu.VMEM((S, T, MIX_W), _F32),
        pltpu.VMEM((S, T, ATT_Q), _BF16),
        pltpu.VMEM((S, ATT_KV_HEADS, 2, kv_rows, 2 * HEAD_DIM), _BF16),
        pltpu.VMEM((S, ATT_KV_HEADS, 2, kv_rows, 4 * HEAD_DIM), _BF16),
        pltpu.VMEM((S, T, 2 * GLA_QK), _F32),
        pltpu.VMEM((S, T, GLA_VW), _F32),
        pltpu.VMEM((S, 2, GLA_HEADS // 2, GLA_DV, 2 * GLA_DK), _F32),
        pltpu.VMEM((GLA_HEADS + 2, GLA_CHUNK, GLA_QK), _F32),
        pltpu.VMEM((S, T + 2 * _pool_halo(T), POOL_W), _BF16),
    ]
    assert len(scratch) == N_MIX_SCRATCH
    return scratch


def _mix_latent_call(layer, x, mod_all, pw, cache_k, cache_v, st0, cos, sin):
    B, T, _ = x.shape
    per_seq = lambda blk: pl.BlockSpec(blk, lambda b, layer: (b,) + (0,) * (len(blk) - 1))
    mod_spec = pl.BlockSpec((None, 1, 1, 6 * D_MODEL), lambda b, layer: (layer[0], b + 1, 0, 0))
    in_specs, args = _mix_params(x, per_seq((1, T, D_MODEL)), mod_spec, mod_all, pw)
    P = cache_k.shape[2]
    cache_spec = pl.BlockSpec((1, 1, P, ATT_KV), lambda b, layer: (b, layer[0], 0, 0))
    in_specs += [cache_spec, cache_spec,
                 pl.BlockSpec((1, None, 2, GLA_HEADS // 2, GLA_DV, 2 * GLA_DK),
                              lambda b, layer: (b, layer[0], 0, 0, 0, 0)),
                 _const_spec((T, ATT_KV)), _const_spec((T, ATT_KV))]
    args += [cache_k, cache_v, st0, cos, sin]
    widths = (ATT_Q, GLA_VW, POOL_W)
    return pl.pallas_call(
        functools.partial(_mix_latent_kernel, T),
        grid_spec=pltpu.PrefetchScalarGridSpec(
            num_scalar_prefetch=1, grid=(B,), in_specs=in_specs,
            out_specs=[per_seq((1, T, w)) for w in widths],
            scratch_shapes=_mix_scratch(1, T, T + 2 * ATT_BLOCK)),
        out_shape=[jax.ShapeDtypeStruct((B, T, w), _BF16) for w in widths],
        compiler_params=pltpu.CompilerParams(dimension_semantics=("arbitrary",), vmem_limit_bytes=VMEM_LIMIT),
        name="mix_latent",
    )(layer, *args)


def _mix_ctx_call(layer, x, mod_all, pw, stacked):
    B, T, _ = x.shape
    S = CTX_SEQS_PER_STEP
    assert B % S == 0
    per_seq = lambda blk: pl.BlockSpec(blk, lambda b, layer: (b,) + (0,) * (len(blk) - 1))
    mod_spec = pl.BlockSpec((None, 1, 1, 6 * D_MODEL), lambda b, layer: (layer[0], 0, 0, 0))
    in_specs, args = _mix_params(x, per_seq((S, T, D_MODEL)), mod_spec, mod_all, pw)
    widths = (ATT_Q, GLA_VW, POOL_W)
    n_in = 1 + len(args)
    aliases = {n_in + j: len(widths) + j for j in range(len(stacked))}
    in_specs += [pl.BlockSpec(memory_space=pl.ANY)] * len(stacked)
    args += list(stacked)
    at_layer = lambda blk: pl.BlockSpec((S, None) + blk, lambda b, layer: (b, layer[0]) + (0,) * len(blk))
    out_specs = [per_seq((S, T, w)) for w in widths] + [
        at_layer((T, ATT_KV)), at_layer((T, ATT_KV)), at_layer((2, GLA_HEADS // 2, 2 * GLA_DK, GLA_DV))]
    out_shape = ([jax.ShapeDtypeStruct((B, T, w), _BF16) for w in widths]
                 + [jax.ShapeDtypeStruct(a.shape, a.dtype) for a in stacked])
    return pl.pallas_call(
        functools.partial(_mix_ctx_kernel, T),
        grid_spec=pltpu.PrefetchScalarGridSpec(
            num_scalar_prefetch=1, grid=(B // S,), in_specs=in_specs, out_specs=out_specs,
            scratch_shapes=_mix_scratch(S, T, T)),
        out_shape=out_shape,
        input_output_aliases=aliases,
        compiler_params=pltpu.CompilerParams(dimension_semantics=("arbitrary",), vmem_limit_bytes=VMEM_LIMIT),
        name="mix_ctx",
    )(layer, *args)


def _post_kernel(layer_ref, x_ref, mod_ref, oa_ref, ob_ref, oc_ref, gn1_ref, gn2_ref, wgate_ref, wa_ref, wb_ref,
                 wc_ref, wout_ref, wfg_ref, wfu_ref, wfd_ref, out_ref):
    x = _merge(x_ref[...], mod_ref, oa_ref[...], ob_ref[...], oc_ref[...], gn1_ref, wgate_ref, wa_ref, wb_ref,
               wc_ref, wout_ref)
    out_ref[...] = _ffn(x, mod_ref, gn2_ref, wfg_ref, wfu_ref, wfd_ref)


def _post_call(layer, x2d, mod_all, oa, ob, oc, pw, tiles_per_seq):
    row = lambda w: pl.BlockSpec((POST_TILE, w), lambda i, layer: (i, 0))
    if tiles_per_seq is None:
        mod_spec = pl.BlockSpec((None, 1, 1, 6 * D_MODEL), lambda i, layer: (layer[0], 0, 0, 0))
    else:
        mod_spec = pl.BlockSpec((None, 1, 1, 6 * D_MODEL),
                                lambda i, layer: (layer[0], 1 + i // tiles_per_seq, 0, 0))
    weights = [(pw["g_norm1"], (1, D_MODEL)), (pw["g_norm2"], (1, D_MODEL)), (pw["w_gates"], (D_MODEL, GATE_W)),
               (pw["w_br_a"], (ATT_Q, D_MODEL)), (pw["w_br_b"], (GLA_VW, D_MODEL)),
               (pw["w_br_c"], (POOL_W, D_MODEL)), (pw["w_out"], (D_MODEL, D_MODEL)),
               (pw["w_ff_gate"], (D_MODEL, D_FF)), (pw["w_ff_up"], (D_MODEL, D_FF)),
               (pw["w_ff_down"], (D_FF, D_MODEL))]
    in_specs = ([row(D_MODEL), mod_spec, row(ATT_Q), row(GLA_VW), row(POOL_W)]
                + [_layer_spec(shape) for _, shape in weights])
    return pl.pallas_call(
        _post_kernel,
        grid_spec=pltpu.PrefetchScalarGridSpec(
            num_scalar_prefetch=1, grid=(x2d.shape[0] // POST_TILE,), in_specs=in_specs, out_specs=row(D_MODEL)),
        out_shape=jax.ShapeDtypeStruct(x2d.shape, _F32),
        input_output_aliases={1: 0},
        compiler_params=pltpu.CompilerParams(dimension_semantics=("arbitrary",), vmem_limit_bytes=VMEM_LIMIT),
        name="post",
    )(layer, x2d, mod_all, oa, ob, oc, *[a for a, _ in weights])


def _rope_tables(T):
    quarter = HEAD_DIM // 4
    inv_freq = ROPE_BASE ** (-np.arange(quarter, dtype=np.float32) / quarter)
    pos = np.arange(T)
    ang_row = (pos // GRID_W).astype(np.float32)[:, None] * inv_freq[None, :]
    ang_col = (pos % GRID_W).astype(np.float32)[:, None] * inv_freq[None, :]
    cos = np.concatenate([np.cos(ang_row)] * 2 + [np.cos(ang_col)] * 2, axis=-1)
    sin = np.concatenate([-np.sin(ang_row), np.sin(ang_row), -np.sin(ang_col), np.sin(ang_col)], axis=-1)
    return (jnp.asarray(np.tile(cos, (1, ATT_KV_HEADS)), _F32), jnp.asarray(np.tile(sin, (1, ATT_KV_HEADS)), _F32))


def _prepare_weights(w_in, g_qn, g_kn, att_sink, w_gate2, b_gate2, g_gla_out, w_pool, pool_scale, w_br_a,
                     w_br_b, w_br_c, w_out, g_norm1, g_norm2, w_ff_gate, w_ff_up, w_ff_down):
    o_gl = ATT_Q + 2 * ATT_KV + 2 * GLA_QK + 2 * GLA_VW
    o_uc = o_gl + 2 * GLA_RANK
    o_gate = o_uc + POOL_W
    assert o_gl == C_UC
    w_main = w_in[:, :, :o_gl].astype(_BF16)
    w_tail = jnp.concatenate(
        [w_in[:, :, o_uc:o_gate], w_in[:, :, o_gl:o_uc],
         jnp.zeros((DEPTH, D_MODEL, GL_PAD - 2 * GLA_RANK), w_in.dtype)], axis=2).astype(_BF16)
    w_gates = w_in[:, :, o_gate:].astype(_BF16)
    wg2 = jnp.zeros((DEPTH, GL_PAD, 2 * GLA_QK), _F32)
    wg2 = wg2.at[:, 0:GLA_RANK, 0:GLA_QK].set(w_gate2[:, 0])
    wg2 = wg2.at[:, GLA_RANK:2 * GLA_RANK, GLA_QK:].set(w_gate2[:, 1])
    w_pool2 = jnp.zeros((DEPTH, POOL_GROUPS // 2, 2 * POOL_GROUP_DIM, 2 * POOL_GROUP_DIM), _F32)
    w_pool2 = w_pool2.at[:, :, :POOL_GROUP_DIM, :POOL_GROUP_DIM].set(w_pool[:, 0::2])
    w_pool2 = w_pool2.at[:, :, POOL_GROUP_DIM:, POOL_GROUP_DIM:].set(w_pool[:, 1::2])
    vec = lambda a: a.reshape(DEPTH, 1, -1)
    group_ones = lambda n, width: jnp.asarray(
        (np.arange(n)[:, None] // width) == (np.arange(n)[None, :] // width), _BF16)
    return {
        "head_ones": group_ones(MXU_TILE, HEAD_DIM),
        "gla_ones": group_ones(MXU_TILE, GLA_DV),
        "w_main": w_main,
        "w_tail": w_tail,
        "w_gates": w_gates,
        "g_qn": vec(jnp.tile(g_qn, (1, ATT_HEADS))),
        "g_kn": vec(jnp.tile(g_kn, (1, ATT_KV_HEADS))),
        "att_sink": att_sink,
        "w_gate2": wg2.astype(_BF16),
        "b_gate2": vec(b_gate2),
        "g_gla_out": vec(jnp.tile(g_gla_out, (1, GLA_HEADS))),
        "w_pool": w_pool2.astype(_BF16),
        "pool_scale": vec(pool_scale),
        "w_br_a": w_br_a.astype(_BF16),
        "w_br_b": w_br_b.astype(_BF16),
        "w_br_c": w_br_c.astype(_BF16),
        "w_out": w_out.astype(_BF16),
        "g_norm1": vec(g_norm1),
        "g_norm2": vec(g_norm2),
        "w_ff_gate": w_ff_gate.astype(_BF16),
        "w_ff_up": w_ff_up.astype(_BF16),
        "w_ff_down": w_ff_down.astype(_BF16),
    }


def kernel(x_prompt, x_sample, c, cache_k, cache_v, state_gla, c_ctx, w_in, g_qn, g_kn, att_sink, w_gate2,
           b_gate2, g_gla_out, w_pool, pool_scale, w_br_a, w_br_b, w_br_c, w_out, g_norm1, g_norm2, w_mod,
           b_mod, w_ff_gate, w_ff_up, w_ff_down):
    B, T, _ = x_prompt.shape
    BL, TL, _ = x_sample.shape
    assert (B * T) % POST_TILE == 0 and TL % POST_TILE == 0 and BL + 1 <= MOD_ROWS
    assert T % PROJ_TILE == 0 and TL % PROJ_TILE == 0
    cv = jnp.concatenate([c_ctx[None, :], c, jnp.zeros((MOD_ROWS - 1 - BL, D_MODEL), _F32)], axis=0)
    mod_all = _modulation(cv, w_mod, b_mod).reshape(DEPTH, MOD_ROWS, 1, 6 * D_MODEL)
    pw = _prepare_weights(w_in, g_qn, g_kn, att_sink, w_gate2, b_gate2, g_gla_out, w_pool, pool_scale, w_br_a,
                          w_br_b, w_br_c, w_out, g_norm1, g_norm2, w_ff_gate, w_ff_up, w_ff_down)
    cos, sin = _rope_tables(TL)
    P = cache_k.shape[2]
    latent_ctx = (cache_k.reshape(BL, DEPTH, P, ATT_KV), cache_v.reshape(BL, DEPTH, P, ATT_KV),
                  jnp.swapaxes(state_gla.reshape(BL, DEPTH, 2, GLA_HEADS // 2, 2 * GLA_DK, GLA_DV), -1, -2),
                  cos, sin)

    def layer_step(l, carry):
        yp, ys, new_k, new_v, new_st = carry
        layer = jnp.full((1,), l, jnp.int32)
        oa, ob, oc, new_k, new_v, new_st = _mix_ctx_call(layer, yp, mod_all, pw, (new_k, new_v, new_st))
        yp = _post_call(layer, yp.reshape(B * T, D_MODEL), mod_all, oa.reshape(B * T, -1), ob.reshape(B * T, -1),
                        oc.reshape(B * T, -1), pw, None).reshape(B, T, D_MODEL)
        oa, ob, oc = _mix_latent_call(layer, ys, mod_all, pw, *latent_ctx)
        ys = _post_call(layer, ys.reshape(BL * TL, D_MODEL), mod_all, oa.reshape(BL * TL, -1),
                        ob.reshape(BL * TL, -1), oc.reshape(BL * TL, -1), pw,
                        TL // POST_TILE).reshape(BL, TL, D_MODEL)
        return yp, ys, new_k, new_v, new_st

    init = (x_prompt, x_sample,
            jnp.zeros((B, DEPTH, T, ATT_KV), _F32), jnp.zeros((B, DEPTH, T, ATT_KV), _F32),
            jnp.zeros((B, DEPTH, 2, GLA_HEADS // 2, 2 * GLA_DK, GLA_DV), _F32))
    yp, ys, new_k, new_v, new_st = lax.fori_loop(0, DEPTH, layer_step, init)
    return (yp, ys, new_k.reshape(B, DEPTH, T, ATT_KV_HEADS, HEAD_DIM),
            new_v.reshape(B, DEPTH, T, ATT_KV_HEADS, HEAD_DIM),
            new_st.reshape(B, DEPTH, 2, GLA_HEADS, GLA_DK, GLA_DV))
```

```python
import functools

import jax
import jax.numpy as jnp
import numpy as np
from jax import lax
from jax.experimental import pallas as pl
from jax.experimental.pallas import tpu as pltpu

D_MODEL = 1024
DEPTH = 4
GRID_W = 64
ATT_HEADS = 8
ATT_KV_HEADS = 2
ATT_GROUP = ATT_HEADS // ATT_KV_HEADS
HEAD_DIM = 64
WINDOW = 128
ATT_BLOCK = 128
ROPE_BASE = 10000.0
GLA_HEADS = 4
GLA_DK = 64
GLA_DV = 128
GLA_RANK = 16
GLA_TAU = 16.0
GLA_CHUNK = 64
POOL_GROUPS = 4
POOL_GROUP_DIM = 128
POOL_WINDOWS = (2, 4, 8, 16)
D_FF = 2816
ATT_Q = ATT_HEADS * HEAD_DIM
ATT_KV = ATT_KV_HEADS * HEAD_DIM
GLA_QK = GLA_HEADS * GLA_DK
GLA_VW = GLA_HEADS * GLA_DV
POOL_W = POOL_GROUPS * POOL_GROUP_DIM
EPS = 1e-6
NEG = -1e30

C_QA = 0
C_KA = C_QA + ATT_Q
C_VA = C_KA + ATT_KV
C_QB = C_VA + ATT_KV
C_KB = C_QB + GLA_QK
C_VB = C_KB + GLA_QK
C_RB = C_VB + GLA_VW
C_UC = C_RB + GLA_VW
C_GL = C_UC + POOL_W
LANES = 128
SUBLANES = 8
MXU_TILE = 256
GL_PAD = LANES
MIX_W = C_GL + GL_PAD
GATE_W = 3 * D_MODEL

POST_TILE = 512
PROJ_TILE = 256
CTX_SEQS_PER_STEP = 2
POOL_TILE = 256
POOL_HALO = LANES
MOD_ROWS = SUBLANES
MOD_TILE = 3072
GLA_BLOCK = 256
GLA_MAX_EXPONENT = 80.0
VMEM_LIMIT = 56 * 1024 * 1024

_F32 = jnp.float32
_BF16 = jnp.bfloat16


def _dot(a, b):
    return jnp.dot(a, b, preferred_element_type=_F32)


def _dot_nt(a, b):
    return lax.dot_general(a, b, (((1,), (1,)), ((), ())), preferred_element_type=_F32)


def _dot_tn(a, b):
    return lax.dot_general(a, b, (((0,), (0,)), ((), ())), preferred_element_type=_F32)


def _rms_scale(x):
    return lax.rsqrt(jnp.mean(x * x, axis=-1, keepdims=True) + EPS)


def _group_rms_scale(x, group_ones, width):
    sq = (x * x).astype(_BF16)
    n, blk = x.shape[-1], group_ones.shape[0]
    if n <= blk:
        sums = _dot(sq, group_ones[0:n, 0:n])
    else:
        sums = jnp.concatenate([_dot(sq[:, i:i + blk], group_ones) for i in range(0, n, blk)], axis=-1)
    return lax.rsqrt(sums * (1.0 / width) + EPS)


def _log_sigmoid(x):
    return jnp.minimum(x, 0.0) - jnp.log(1.0 + jnp.exp(-jnp.abs(x)))


def _silu(x):
    return x * jax.nn.sigmoid(x)


def _rope(x, cos, sin_signed):
    n = x.shape[-1]
    lane = lax.broadcasted_iota(jnp.int32, x.shape, 1)
    up = pltpu.roll(x, n - HEAD_DIM // 4, axis=1)
    down = pltpu.roll(x, HEAD_DIM // 4, axis=1)
    partner = jnp.where((lane & (HEAD_DIM // 2 - 1)) < HEAD_DIM // 4, up, down)
    return x * cos + partner * sin_signed


def _mod_kernel(cv_ref, w_ref, b_ref, out_ref):
    s = _silu(cv_ref[...]).astype(_BF16)
    out_ref[0] = _dot(s, w_ref[0].astype(_BF16)) + b_ref[0]


def _modulation(cv, w_mod, b_mod):
    n_col = (6 * D_MODEL) // MOD_TILE
    return pl.pallas_call(
        _mod_kernel,
        grid=(DEPTH, n_col),
        in_specs=[
            pl.BlockSpec((MOD_ROWS, D_MODEL), lambda l, j: (0, 0)),
            pl.BlockSpec((1, D_MODEL, MOD_TILE), lambda l, j: (l, 0, j)),
            pl.BlockSpec((1, 1, MOD_TILE), lambda l, j: (l, 0, j)),
        ],
        out_specs=pl.BlockSpec((1, MOD_ROWS, MOD_TILE), lambda l, j: (l, 0, j)),
        out_shape=jax.ShapeDtypeStruct((DEPTH, MOD_ROWS, 6 * D_MODEL), _F32),
        compiler_params=pltpu.CompilerParams(vmem_limit_bytes=VMEM_LIMIT),
        name="modulation",
    )(cv, w_mod, b_mod.reshape(DEPTH, 1, 6 * D_MODEL))


def _split_heads(x):
    low = lax.broadcasted_iota(jnp.int32, x.shape, 1) < HEAD_DIM
    swapped = pltpu.roll(x, HEAD_DIM, axis=1)
    zero = jnp.zeros_like(x)
    return ((jnp.where(low, x, zero), jnp.where(low, zero, swapped)),
            (jnp.where(low, swapped, zero), jnp.where(low, zero, x)))


def _store_split_kv(k, v, ks_ref, vs_ref, rows):
    ones = jnp.ones_like(v)
    for kv, (k_sides, v_sides, one_sides) in enumerate(zip(_split_heads(k), _split_heads(v), _split_heads(ones))):
        for side in range(2):
            ks_ref[kv, side, rows, :] = k_sides[side].astype(_BF16)
            vs_ref[kv, side, rows, :] = jnp.concatenate([v_sides[side], one_sides[side]], axis=-1).astype(_BF16)


def _pair_softmax_av(qp, keys, values, masks, sink_even, sink_odd):
    m = qp.shape[0]
    scores = []
    for (k_left, k_right), mask in zip(keys, masks):
        s_even, s_odd = _dot_nt(qp, k_left), _dot_nt(qp, k_right)
        if mask is not None:
            s_even, s_odd = jnp.where(mask, s_even, NEG), jnp.where(mask, s_odd, NEG)
        scores.append((s_even, s_odd))
    m_even = jnp.full((m, 1), sink_even, _F32)
    m_odd = jnp.full((m, 1), sink_odd, _F32)
    for s_even, s_odd in scores:
        m_even = jnp.maximum(m_even, jnp.max(s_even, axis=-1, keepdims=True))
        m_odd = jnp.maximum(m_odd, jnp.max(s_odd, axis=-1, keepdims=True))
    res = None
    for (s_even, s_odd), (w_left, w_right) in zip(scores, values):
        r = (_dot(jnp.exp(s_even - m_even).astype(_BF16), w_left)
             + _dot(jnp.exp(s_odd - m_odd).astype(_BF16), w_right))
        res = r if res is None else res + r
    pair = 2 * HEAD_DIM
    low = lax.broadcasted_iota(jnp.int32, (m, pair), 1) < HEAD_DIM
    den = res[:, pair:] + jnp.where(low, jnp.exp(sink_even - m_even), jnp.exp(sink_odd - m_odd))
    return res[:, :pair] / den


def _attention_ctx(T, layer, qr_ref, ks_ref, vs_ref, sink_ref, oa_ref):
    pair = 2 * HEAD_DIM
    for kv in range(ATT_KV_HEADS):
        keys = [(ks_ref[kv, 0], ks_ref[kv, 1])]
        values = [(vs_ref[kv, 0], vs_ref[kv, 1])]
        for j in range(ATT_GROUP // 2):
            head = kv * ATT_GROUP + 2 * j
            cols = slice(head * HEAD_DIM, head * HEAD_DIM + pair)
            o = _pair_softmax_av(qr_ref[:, cols], keys, values, [None], sink_ref[layer, head],
                                 sink_ref[layer, head + 1])
            oa_ref[0, :, cols] = o.astype(_BF16)


def _attention_latent(T, layer, qr_ref, ks_ref, vs_ref, kc_ref, vc_ref, sink_ref, oa_ref):
    pair = 2 * HEAD_DIM
    span = 3 * ATT_BLOCK
    kc, vc = kc_ref[0, 0], vc_ref[0, 0]
    ones = jnp.ones_like(vc)
    ctx_keys = [tuple(side.astype(_BF16) for side in sides) for sides in _split_heads(kc)]
    ctx_values = [tuple(jnp.concatenate([v_side, one_side], axis=-1).astype(_BF16)
                        for v_side, one_side in zip(v_sides, one_sides))
                  for v_sides, one_sides in zip(_split_heads(vc), _split_heads(ones))]

    def block(i, carry):
        q_rows = pl.ds(pl.multiple_of(i * ATT_BLOCK, ATT_BLOCK), ATT_BLOCK)
        k_rows = pl.ds(pl.multiple_of(i * ATT_BLOCK, ATT_BLOCK), span)
        q_pos = i * ATT_BLOCK + lax.broadcasted_iota(jnp.int32, (ATT_BLOCK, span), 0)
        k_pos = (i - 1) * ATT_BLOCK + lax.broadcasted_iota(jnp.int32, (ATT_BLOCK, span), 1)
        valid = (jnp.abs(k_pos - q_pos) <= WINDOW) & (k_pos >= 0) & (k_pos < T)
        for kv in range(ATT_KV_HEADS):
            keys = [(ks_ref[kv, 0, k_rows, :], ks_ref[kv, 1, k_rows, :]), ctx_keys[kv]]
            values = [(vs_ref[kv, 0, k_rows, :], vs_ref[kv, 1, k_rows, :]), ctx_values[kv]]
            for j in range(ATT_GROUP // 2):
                head = kv * ATT_GROUP + 2 * j
                cols = slice(head * HEAD_DIM, head * HEAD_DIM + pair)
                o = _pair_softmax_av(qr_ref[q_rows, cols], keys, values, [valid, None], sink_ref[layer, head],
                                     sink_ref[layer, head + 1])
                oa_ref[0, q_rows, cols] = o.astype(_BF16)
        return carry

    lax.fori_loop(0, T // ATT_BLOCK, block, 0, unroll=4)


def _gla_chunk(z_ref, la_ref, o_ref, st_ref, a_ref, direction, start):
    C = GLA_CHUNK
    rows = pl.ds(pl.multiple_of(start, C), C)
    q = z_ref[rows, C_QB:C_QB + GLA_QK] * (GLA_DK ** -0.5)
    k = z_ref[rows, C_KB:C_KB + GLA_QK]
    v = z_ref[rows, C_VB:C_VB + GLA_VW].astype(_BF16)
    la = la_ref[rows, direction * GLA_QK:(direction + 1) * GLA_QK]
    la_hi = la.astype(_BF16)
    la_lo = (la - la_hi.astype(_F32)).astype(_BF16)
    ti = lax.broadcasted_iota(jnp.int32, (C, C), 0)
    si = lax.broadcasted_iota(jnp.int32, (C, C), 1)
    causal = (si <= ti) if direction == 0 else (si >= ti)
    tri = jnp.where(causal, 1.0, 0.0).astype(_BF16)
    b = _dot(tri, la_hi) + _dot(tri, la_lo)
    end = C - 1 if direction == 0 else 0
    b_end = b[end:end + 1]
    q_in = (q * jnp.exp(b)).astype(_BF16)
    k_out = (k * jnp.exp(b_end - b)).astype(_BF16)
    e_end = jnp.exp(b_end)
    ones = jnp.ones((8, GLA_DK), _BF16)

    def row_group(g, carry):
        base = pl.multiple_of(g * 8, 8)
        b8 = a_ref[GLA_HEADS, pl.ds(base, 8), :]
        q8 = a_ref[GLA_HEADS + 1, pl.ds(base, 8), :]
        s_idx = lax.broadcasted_iota(jnp.int32, (C, 1), 0)
        rows_h = [[] for _ in range(GLA_HEADS)]
        for j in range(8):
            ok = (s_idx <= base + j) if direction == 0 else (s_idx >= base + j)
            decay = jnp.exp(jnp.where(ok, b8[j:j + 1] - b, NEG))
            p = (q8[j:j + 1] * k * decay).astype(_BF16)
            for h in range(GLA_HEADS):
                rows_h[h].append(_dot_nt(ones, p[:, h * GLA_DK:(h + 1) * GLA_DK])[0:1])
        for h in range(GLA_HEADS):
            a_ref[h, pl.ds(base, 8), 0:C] = jnp.concatenate(rows_h[h], axis=0)
        return carry

    a_ref[GLA_HEADS] = b
    a_ref[GLA_HEADS + 1] = q
    lax.fori_loop(0, C // 8, row_group, 0)
    outs = []
    for pair in range(GLA_HEADS // 2):
        s_pair = st_ref[direction, pair]
        s_next = []
        for h in (2 * pair, 2 * pair + 1):
            kc = slice(h * GLA_DK, (h + 1) * GLA_DK)
            vc = slice(h * GLA_DV, (h + 1) * GLA_DV)
            s_t = s_pair[:, (h % 2) * GLA_DK:(h % 2 + 1) * GLA_DK]
            outs.append(_dot_nt(q_in[:, kc], s_t.astype(_BF16)) + _dot(a_ref[h, :, 0:C].astype(_BF16), v[:, vc]))
            s_next.append(s_t * e_end[:, kc] + _dot_tn(v[:, vc], k_out[:, kc]))
        st_ref[direction, pair] = jnp.concatenate(s_next, axis=-1)
    o_ref[rows, :] += jnp.concatenate(outs, axis=-1)


def _gla_block(z_ref, la_ref, o_ref, st_ref, direction, start, use_state):
    NB = GLA_BLOCK
    fwd = direction == 0
    rows = slice(start, start + NB) if isinstance(start, int) else pl.ds(pl.multiple_of(start, NB), NB)
    q = z_ref[rows, C_QB:C_QB + GLA_QK] * (GLA_DK ** -0.5)
    k = z_ref[rows, C_KB:C_KB + GLA_QK]
    v = z_ref[rows, C_VB:C_VB + GLA_VW].astype(_BF16)
    b = la_ref[rows, direction * GLA_QK:(direction + 1) * GLA_QK]
    end, mid = (NB - 1, NB // 2 - 1) if fwd else (0, NB // 2)
    b_end = b[end:end + 1]
    c = b - b[mid:mid + 1]
    q_c = (q * jnp.exp(c)).astype(_BF16)
    k_c = (k * jnp.exp(-c)).astype(_BF16)
    k_fin = (k * jnp.exp(b_end - b)).astype(_BF16)
    if use_state:
        q_in = (q * jnp.exp(b)).astype(_BF16)
        e_all = jnp.exp(b_end)
    ti = lax.broadcasted_iota(jnp.int32, (NB, NB), 0)
    si = lax.broadcasted_iota(jnp.int32, (NB, NB), 1)
    causal = (si <= ti) if fwd else (si >= ti)
    tile = 2 * GLA_DK
    low = lax.broadcasted_iota(jnp.int32, (NB, tile), 1) < GLA_DK
    zero = jnp.zeros((NB, tile), _BF16)
    pick = lambda x, parity: jnp.where(low, x, zero) if parity == 0 else jnp.where(low, zero, x)
    outs = []
    for pair in range(GLA_HEADS // 2):
        lanes = slice(pair * tile, (pair + 1) * tile)
        if use_state:
            s_pair = st_ref[direction, pair]
            s_bf = s_pair.astype(_BF16)
        s_new = None
        for parity in range(2):
            h = 2 * pair + parity
            vc = slice(h * GLA_DV, (h + 1) * GLA_DV)
            a = jnp.where(causal, _dot_nt(q_c[:, lanes], pick(k_c[:, lanes], parity)), 0.0)
            o_h = _dot(a.astype(_BF16), v[:, vc])
            if use_state:
                o_h = o_h + _dot_nt(pick(q_in[:, lanes], parity), s_bf)
            outs.append(o_h)
            upd = _dot_tn(v[:, vc], pick(k_fin[:, lanes], parity))
            s_new = upd if s_new is None else s_new + upd
        if use_state:
            s_new = s_new + s_pair * e_all[:, lanes]
        st_ref[direction, pair] = s_new
    o_ref[rows, :] += jnp.concatenate(outs, axis=-1)


def _log_decay(z_ref, rows, wg2_ref, bg2_ref):
    pre = _dot(z_ref[rows, C_GL:C_GL + GL_PAD].astype(_BF16), wg2_ref[...]) + bg2_ref[...]
    return _log_sigmoid(pre) * (1.0 / GLA_TAU)


def _gla_prepare(T, z_ref, la_ref, o_ref, wg2_ref, bg2_ref, tri_ref):
    NB = GLA_BLOCK
    worst = None
    for r0 in range(0, T, NB):
        rows = slice(r0, r0 + NB)
        la = _log_decay(z_ref, rows, wg2_ref, bg2_ref)
        la_hi = la.astype(_BF16)
        la_lo = (la - la_hi.astype(_F32)).astype(_BF16)
        for d in range(2):
            cols = slice(d * GLA_QK, (d + 1) * GLA_QK)
            b = _dot(tri_ref[d], la_hi[:, cols]) + _dot(tri_ref[d], la_lo[:, cols])
            la_ref[rows, cols] = b
            first, mid, last = (0, NB // 2 - 1, NB - 1) if d == 0 else (NB - 1, NB // 2, 0)
            span = jnp.max(jnp.maximum(b[first:first + 1] - b[mid:mid + 1], b[mid:mid + 1] - b[last:last + 1]))
            worst = span if worst is None else jnp.maximum(worst, span)
    o_ref[...] = jnp.zeros(o_ref.shape, _F32)
    return worst


def _gla_run(T, seqs, worst, a_ref, wg2_ref, bg2_ref, use_state):
    NB = GLA_BLOCK
    n_chunks = T // GLA_CHUNK
    n_blocks = T // NB
    fast_ok = worst <= GLA_MAX_EXPONENT

    @pl.when(fast_ok)
    def _():
        if n_blocks == 1:
            for direction in range(2):
                for z_ref, la_ref, o_ref, st_ref in seqs:
                    _gla_block(z_ref, la_ref, o_ref, st_ref, direction, 0, use_state)
        else:
            def body(i, carry):
                for z_ref, la_ref, o_ref, st_ref in seqs:
                    _gla_block(z_ref, la_ref, o_ref, st_ref, 0, i * NB, True)
                    _gla_block(z_ref, la_ref, o_ref, st_ref, 1, (n_blocks - 1 - i) * NB, True)
                return carry
            lax.fori_loop(0, n_blocks, body, 0, unroll=2)

    @pl.when(jnp.logical_not(fast_ok))
    def _():
        for z_ref, la_ref, o_ref, st_ref in seqs:
            for r0 in range(0, T, NB):
                la_ref[r0:r0 + NB, :] = _log_decay(z_ref, slice(r0, r0 + NB), wg2_ref, bg2_ref)

            def body(i, carry):
                _gla_chunk(z_ref, la_ref, o_ref, st_ref, a_ref, 0, i * GLA_CHUNK)
                _gla_chunk(z_ref, la_ref, o_ref, st_ref, a_ref, 1, (n_chunks - 1 - i) * GLA_CHUNK)
                return carry
            lax.fori_loop(0, n_chunks, body, 0)


def _gla_finish(T, z_ref, o_ref, ggla_ref, vones_ref, ob_ref):
    for r0 in range(0, T, PROJ_TILE):
        rows = slice(r0, r0 + PROJ_TILE)
        o = o_ref[rows, :]
        y = o * _group_rms_scale(o, vones_ref[...], GLA_DV) * ggla_ref[...]
        ob_ref[0, rows, :] = (y * _silu(z_ref[rows, C_RB:C_RB + GLA_VW])).astype(_BF16)


def _pool_halo(T):
    return POOL_HALO if T > POOL_TILE else 0


def _pool_bands(T):
    halo = _pool_halo(T)
    off = np.arange(POOL_TILE + 2 * halo)[None, :] - halo - np.arange(POOL_TILE)[:, None]
    return jnp.asarray(np.stack([(off >= -(w // 2)) & (off < w - w // 2) for w in POOL_WINDOWS]), _BF16)


def _scan_masks():
    t = np.arange(GLA_BLOCK)
    return jnp.asarray(np.stack([t[None, :] <= t[:, None], t[None, :] >= t[:, None]]), _BF16)


def _pool_inverse_counts(T):
    t = np.arange(T)
    cols = [np.repeat((1.0 / (np.minimum(t - w // 2 + w, T) - np.maximum(t - w // 2, 0)))[:, None],
                      POOL_GROUP_DIM, axis=1) for w in POOL_WINDOWS]
    return jnp.asarray(np.concatenate(cols, axis=1), _F32)


def _pool(T, z_ref, upad_ref, wpool_ref, pscale_ref, bands_ref, icnt_ref, oc_ref):
    halo = _pool_halo(T)
    if halo:
        zeros = jnp.zeros((halo, POOL_W), _BF16)
        upad_ref[0:halo, :] = zeros
        upad_ref[halo + T:halo + T + halo, :] = zeros
    upad_ref[halo:halo + T, :] = z_ref[:, C_UC:C_UC + POOL_W].astype(_BF16)
    span = POOL_TILE + 2 * halo
    group = lambda g: slice(g * POOL_GROUP_DIM, (g + 1) * POOL_GROUP_DIM)
    for jb in range(T // POOL_TILE):
        rows = slice(jb * POOL_TILE, (jb + 1) * POOL_TILE)
        win = slice(jb * POOL_TILE, jb * POOL_TILE + span)
        total = jnp.concatenate([_dot(bands_ref[g], upad_ref[win, group(g)]) for g in range(POOL_GROUPS)], axis=-1)
        pooled = (total * icnt_ref[rows, :] - z_ref[rows, C_UC:C_UC + POOL_W]).astype(_BF16)
        pair = 2 * POOL_GROUP_DIM
        y = jnp.concatenate([_dot(pooled[:, p * pair:(p + 1) * pair], wpool_ref[p])
                             for p in range(POOL_GROUPS // 2)], axis=-1)
        oc_ref[0, rows, :] = (y * pscale_ref[...]).astype(_BF16)


N_MIX_PARAMS = 18
N_MIX_SCRATCH = 9


def _mix_body(latent, T, layer, params, latent_refs, out_refs, scratch):
    (x_ref, mod_ref, gn1_ref, wmain_ref, wtail_ref, gqn_ref, gkn_ref, sink_ref, wg2_ref, bg2_ref, ggla_ref,
     wpool_ref, pscale_ref, hones_ref, vones_ref, bands_ref, icnt_ref, tri_ref) = params
    n_seq = x_ref.shape[0]
    one = lambda ref, s: ref.at[pl.ds(s, 1)]
    a_ref = scratch[N_MIX_SCRATCH - 2]
    per_seq = [tuple(ref.at[s] for ref in scratch[:N_MIX_SCRATCH - 2] + scratch[N_MIX_SCRATCH - 1:])
               for s in range(n_seq)]
    pad = ATT_BLOCK if latent else 0
    shift = mod_ref[0, :, 0:D_MODEL]
    scale = mod_ref[0, :, D_MODEL:2 * D_MODEL]

    for s, (z_ref, qr_ref, ks_ref, vs_ref, la_ref, o_ref, st_ref, upad_ref) in enumerate(per_seq):
        if latent:
            for ref in (ks_ref, vs_ref):
                zeros = jnp.zeros(ref.shape[:2] + (pad, ref.shape[3]), _BF16)
                ref[:, :, 0:pad, :] = zeros
                ref[:, :, pad + T:pad + T + pad, :] = zeros
        for r0 in range(0, T, PROJ_TILE):
            rows = slice(r0, r0 + PROJ_TILE)
            x = x_ref[s, rows, :]
            hn = (x * _rms_scale(x) * gn1_ref[...]) * (1.0 + scale) + shift
            hn = hn.astype(_BF16)
            z_ref[rows, 0:C_UC] = _dot(hn, wmain_ref[...])
            z_ref[rows, C_UC:MIX_W] = _dot(hn, wtail_ref[...])
            q = z_ref[rows, C_QA:C_QA + ATT_Q]
            k = z_ref[rows, C_KA:C_KA + ATT_KV]
            q = q * _group_rms_scale(q, hones_ref[...], HEAD_DIM) * gqn_ref[...]
            k = k * _group_rms_scale(k, hones_ref[...], HEAD_DIM) * gkn_ref[...]
            v = z_ref[rows, C_VA:C_VA + ATT_KV]
            if latent:
                cos_ref, sin_ref = latent_refs[3], latent_refs[4]
                cos = jnp.concatenate([cos_ref[rows, :]] * (ATT_Q // ATT_KV), axis=-1)
                sin = jnp.concatenate([sin_ref[rows, :]] * (ATT_Q // ATT_KV), axis=-1)
                q = _rope(q, cos, sin)
                k = _rope(k, cos_ref[rows, :], sin_ref[rows, :])
            else:
                out_refs[3][s, rows, :] = k
                out_refs[4][s, rows, :] = v
            qr_ref[rows, :] = (q * (HEAD_DIM ** -0.5)).astype(_BF16)
            _store_split_kv(k, v, ks_ref, vs_ref, slice(pad + r0, pad + r0 + PROJ_TILE))

    for s, (z_ref, qr_ref, ks_ref, vs_ref, la_ref, o_ref, st_ref, upad_ref) in enumerate(per_seq):
        if latent:
            _attention_latent(T, layer, qr_ref, ks_ref, vs_ref, one(latent_refs[0], s), one(latent_refs[1], s),
                              sink_ref, one(out_refs[0], s))
        else:
            _attention_ctx(T, layer, qr_ref, ks_ref, vs_ref, sink_ref, one(out_refs[0], s))

    worst = None
    for s, (z_ref, qr_ref, ks_ref, vs_ref, la_ref, o_ref, st_ref, upad_ref) in enumerate(per_seq):
        if latent:
            st_ref[...] = latent_refs[2][s]
        else:
            st_ref[...] = jnp.zeros(st_ref.shape, _F32)
        span = _gla_prepare(T, z_ref, la_ref, o_ref, wg2_ref, bg2_ref, tri_ref)
        worst = span if worst is None else jnp.maximum(worst, span)
    _gla_run(T, [(z_ref, la_ref, o_ref, st_ref) for z_ref, _, _, _, la_ref, o_ref, st_ref, _ in per_seq],
             worst, a_ref, wg2_ref, bg2_ref, latent)

    for s, (z_ref, qr_ref, ks_ref, vs_ref, la_ref, o_ref, st_ref, upad_ref) in enumerate(per_seq):
        _gla_finish(T, z_ref, o_ref, ggla_ref, vones_ref, one(out_refs[1], s))
        if not latent:
            for d in range(2):
                for pair in range(GLA_HEADS // 2):
                    out_refs[5][s, d, pair] = st_ref[d, pair].T
        _pool(T, z_ref, upad_ref, wpool_ref, pscale_ref, bands_ref, icnt_ref, one(out_refs[2], s))


def _merge(x, mod_ref, oa, ob, oc, gn1_ref, wgate_ref, wa_ref, wb_ref, wc_ref, wout_ref):
    mod = lambda i: mod_ref[0, :, i * D_MODEL:(i + 1) * D_MODEL]
    hn = (x * _rms_scale(x) * gn1_ref[...]) * (1.0 + mod(1)) + mod(0)
    gates = jax.nn.sigmoid(_dot(hn.astype(_BF16), wgate_ref[...]))
    mixed = (gates[:, 0:D_MODEL] * _dot(oa, wa_ref[...])
             + gates[:, D_MODEL:2 * D_MODEL] * _dot(ob, wb_ref[...])
             + gates[:, 2 * D_MODEL:3 * D_MODEL] * _dot(oc, wc_ref[...]))
    return x + mod(2) * _dot(mixed.astype(_BF16), wout_ref[...])


def _ffn(x, mod_ref, gn2_ref, wfg_ref, wfu_ref, wfd_ref):
    mod = lambda i: mod_ref[0, :, i * D_MODEL:(i + 1) * D_MODEL]
    hn = ((x * _rms_scale(x) * gn2_ref[...]) * (1.0 + mod(4)) + mod(3)).astype(_BF16)
    h = _silu(_dot(hn, wfg_ref[...])) * _dot(hn, wfu_ref[...])
    return x + mod(5) * _dot(h.astype(_BF16), wfd_ref[...])


def _mix_latent_kernel(T, layer_ref, *refs):
    params, refs = refs[:N_MIX_PARAMS], refs[N_MIX_PARAMS:]
    _mix_body(True, T, layer_ref[0], params, refs[:5], refs[5:8], refs[8:])


def _mix_ctx_kernel(T, layer_ref, *refs):
    params, refs = refs[:N_MIX_PARAMS], refs[N_MIX_PARAMS + 3:]
    _mix_body(False, T, layer_ref[0], params, None, refs[:6], refs[6:])


def _layer_spec(shape):
    zeros = (0,) * len(shape)
    return pl.BlockSpec((None,) + tuple(shape), lambda i, layer: (layer[0],) + zeros,
                        pipeline_mode=pl.Buffered(1))


def _const_spec(blk):
    return pl.BlockSpec(blk, lambda i, layer: (0,) * len(blk), pipeline_mode=pl.Buffered(1))


def _mix_params(x, x_spec, mod_spec, mod_all, pw):
    bands = _pool_bands(x.shape[1])
    specs = [
        x_spec, mod_spec,
        _layer_spec((1, D_MODEL)),
        _layer_spec((D_MODEL, C_UC)),
        _layer_spec((D_MODEL, MIX_W - C_UC)),
        _layer_spec((1, ATT_Q)), _layer_spec((1, ATT_KV)),
        pl.BlockSpec(memory_space=pltpu.SMEM),
        _layer_spec((GL_PAD, 2 * GLA_QK)), _layer_spec((1, 2 * GLA_QK)), _layer_spec((1, GLA_VW)),
        _layer_spec((POOL_GROUPS // 2, 2 * POOL_GROUP_DIM, 2 * POOL_GROUP_DIM)), _layer_spec((1, POOL_W)),
        _const_spec((MXU_TILE, MXU_TILE)), _const_spec((MXU_TILE, MXU_TILE)),
        _const_spec(bands.shape), _const_spec((x.shape[1], POOL_W)), _const_spec((2, GLA_BLOCK, GLA_BLOCK)),
    ]
    args = [x, mod_all, pw["g_norm1"], pw["w_main"], pw["w_tail"], pw["g_qn"], pw["g_kn"], pw["att_sink"], pw["w_gate2"],
            pw["b_gate2"], pw["g_gla_out"], pw["w_pool"], pw["pool_scale"], pw["head_ones"], pw["gla_ones"], bands,
            _pool_inverse_counts(x.shape[1]), _scan_masks()]
    assert len(specs) == len(args) == N_MIX_PARAMS
    return specs, args


def _mix_scratch(S, T, kv_rows):
    scratch = [
        pltpu.VMEM((S, T, MIX_W), _F32),
        pltpu.VMEM((S, T, ATT_Q), _BF16),
        pltpu.VMEM((S, ATT_KV_HEADS, 2, kv_rows, 2 * HEAD_DIM), _BF16),
        pltpu.VMEM((S, ATT_KV_HEADS, 2, kv_rows, 4 * HEAD_DIM), _BF16),
        pltpu.VMEM((S, T, 2 * GLA_QK), _F32),
        pltpu.VMEM((S, T, GLA_VW), _F32),
        pltpu.VMEM((S, 2, GLA_HEADS // 2, GLA_DV, 2 * GLA_DK), _F32),
        pltpu.VMEM((GLA_HEADS + 2, GLA_CHUNK, GLA_QK), _F32),
        pltpu.VMEM((S, T + 2 * _pool_halo(T), POOL_W), _BF16),
    ]
    assert len(scratch) == N_MIX_SCRATCH
    return scratch


def _mix_latent_call(layer, x, mod_all, pw, cache_k, cache_v, st0, cos, sin):
    B, T, _ = x.shape
    per_seq = lambda blk: pl.BlockSpec(blk, lambda b, layer: (b,) + (0,) * (len(blk) - 1))
    mod_spec = pl.BlockSpec((None, 1, 1, 6 * D_MODEL), lambda b, layer: (layer[0], b + 1, 0, 0))
    in_specs, args = _mix_params(x, per_seq((1, T, D_MODEL)), mod_spec, mod_all, pw)
    P = cache_k.shape[2]
    cache_spec = pl.BlockSpec((1, 1, P, ATT_KV), lambda b, layer: (b, layer[0], 0, 0))
    in_specs += [cache_spec, cache_spec,
                 pl.BlockSpec((1, None, 2, GLA_HEADS // 2, GLA_DV, 2 * GLA_DK),
                              lambda b, layer: (b, layer[0], 0, 0, 0, 0)),
                 _const_spec((T, ATT_KV)), _const_spec((T, ATT_KV))]
    args += [cache_k, cache_v, st0, cos, sin]
    widths = (ATT_Q, GLA_VW, POOL_W)
    return pl.pallas_call(
        functools.partial(_mix_latent_kernel, T),
        grid_spec=pltpu.PrefetchScalarGridSpec(
            num_scalar_prefetch=1, grid=(B,), in_specs=in_specs,
            out_specs=[per_seq((1, T, w)) for w in widths],
            scratch_shapes=_mix_scratch(1, T, T + 2 * ATT_BLOCK)),
        out_shape=[jax.ShapeDtypeStruct((B, T, w), _BF16) for w in widths],
        compiler_params=pltpu.CompilerParams(dimension_semantics=("arbitrary",), vmem_limit_bytes=VMEM_LIMIT),
        name="mix_latent",
    )(layer, *args)


def _mix_ctx_call(layer, x, mod_all, pw, stacked):
    B, T, _ = x.shape
    S = CTX_SEQS_PER_STEP
    assert B % S == 0
    per_seq = lambda blk: pl.BlockSpec(blk, lambda b, layer: (b,) + (0,) * (len(blk) - 1))
    mod_spec = pl.BlockSpec((None, 1, 1, 6 * D_MODEL), lambda b, layer: (layer[0], 0, 0, 0))
    in_specs, args = _mix_params(x, per_seq((S, T, D_MODEL)), mod_spec, mod_all, pw)
    widths = (ATT_Q, GLA_VW, POOL_W)
    n_in = 1 + len(args)
    aliases = {n_in + j: len(widths) + j for j in range(len(stacked))}
    in_specs += [pl.BlockSpec(memory_space=pl.ANY)] * len(stacked)
    args += list(stacked)
    at_layer = lambda blk: pl.BlockSpec((S, None) + blk, lambda b, layer: (b, layer[0]) + (0,) * len(blk))
    out_specs = [per_seq((S, T, w)) for w in widths] + [
        at_layer((T, ATT_KV)), at_layer((T, ATT_KV)), at_layer((2, GLA_HEADS // 2, 2 * GLA_DK, GLA_DV))]
    out_shape = ([jax.ShapeDtypeStruct((B, T, w), _BF16) for w in widths]
                 + [jax.ShapeDtypeStruct(a.shape, a.dtype) for a in stacked])
    return pl.pallas_call(
        functools.partial(_mix_ctx_kernel, T),
        grid_spec=pltpu.PrefetchScalarGridSpec(
            num_scalar_prefetch=1, grid=(B // S,), in_specs=in_specs, out_specs=out_specs,
            scratch_shapes=_mix_scratch(S, T, T)),
        out_shape=out_shape,
        input_output_aliases=aliases,
        compiler_params=pltpu.CompilerParams(dimension_semantics=("arbitrary",), vmem_limit_bytes=VMEM_LIMIT),
        name="mix_ctx",
    )(layer, *args)


def _post_kernel(layer_ref, x_ref, mod_ref, oa_ref, ob_ref, oc_ref, gn1_ref, gn2_ref, wgate_ref, wa_ref, wb_ref,
                 wc_ref, wout_ref, wfg_hbm, wfu_hbm, wfd_hbm, out_ref, wfg_ref, wfu_ref, wfd_ref, sems):
    layer = layer_ref[0]
    first = pl.program_id(0) == 0
    copies = [pltpu.make_async_copy(hbm.at[layer], buf, sems.at[j])
              for j, (hbm, buf) in enumerate(((wfg_hbm, wfg_ref), (wfu_hbm, wfu_ref), (wfd_hbm, wfd_ref)))]

    @pl.when(first)
    def _():
        for copy in copies:
            copy.start()

    x = _merge(x_ref[...], mod_ref, oa_ref[...], ob_ref[...], oc_ref[...], gn1_ref, wgate_ref, wa_ref, wb_ref,
               wc_ref, wout_ref)

    @pl.when(first)
    def _():
        for copy in copies:
            copy.wait()

    out_ref[...] = _ffn(x, mod_ref, gn2_ref, wfg_ref, wfu_ref, wfd_ref)


def _post_call(layer, x2d, mod_all, oa, ob, oc, pw, tiles_per_seq):
    row = lambda w: pl.BlockSpec((POST_TILE, w), lambda i, layer: (i, 0))
    if tiles_per_seq is None:
        mod_spec = pl.BlockSpec((None, 1, 1, 6 * D_MODEL), lambda i, layer: (layer[0], 0, 0, 0))
    else:
        mod_spec = pl.BlockSpec((None, 1, 1, 6 * D_MODEL),
                                lambda i, layer: (layer[0], 1 + i // tiles_per_seq, 0, 0))
    weights = [(pw["g_norm1"], (1, D_MODEL)), (pw["g_norm2"], (1, D_MODEL)), (pw["w_gates"], (D_MODEL, GATE_W)),
               (pw["w_br_a"], (ATT_Q, D_MODEL)), (pw["w_br_b"], (GLA_VW, D_MODEL)),
               (pw["w_br_c"], (POOL_W, D_MODEL)), (pw["w_out"], (D_MODEL, D_MODEL)),
               (pw["w_ff_gate"], (D_MODEL, D_FF)), (pw["w_ff_up"], (D_MODEL, D_FF)),
               (pw["w_ff_down"], (D_FF, D_MODEL))]
    n_manual = 3
    in_specs = ([row(D_MODEL), mod_spec, row(ATT_Q), row(GLA_VW), row(POOL_W)]
                + [_layer_spec(shape) for _, shape in weights[:-n_manual]]
                + [pl.BlockSpec(memory_space=pl.ANY)] * n_manual)
    scratch = [pltpu.VMEM(shape, _BF16) for _, shape in weights[-n_manual:]] + [pltpu.SemaphoreType.DMA((n_manual,))]
    return pl.pallas_call(
        _post_kernel,
        grid_spec=pltpu.PrefetchScalarGridSpec(
            num_scalar_prefetch=1, grid=(x2d.shape[0] // POST_TILE,), in_specs=in_specs, out_specs=row(D_MODEL),
            scratch_shapes=scratch),
        out_shape=jax.ShapeDtypeStruct(x2d.shape, _F32),
        input_output_aliases={1: 0},
        compiler_params=pltpu.CompilerParams(dimension_semantics=("arbitrary",), vmem_limit_bytes=VMEM_LIMIT),
        name="post",
    )(layer, x2d, mod_all, oa, ob, oc, *[a for a, _ in weights])


def _rope_tables(T):
    quarter = HEAD_DIM // 4
    inv_freq = ROPE_BASE ** (-np.arange(quarter, dtype=np.float32) / quarter)
    pos = np.arange(T)
    ang_row = (pos // GRID_W).astype(np.float32)[:, None] * inv_freq[None, :]
    ang_col = (pos % GRID_W).astype(np.float32)[:, None] * inv_freq[None, :]
    cos = np.concatenate([np.cos(ang_row)] * 2 + [np.cos(ang_col)] * 2, axis=-1)
    sin = np.concatenate([-np.sin(ang_row), np.sin(ang_row), -np.sin(ang_col), np.sin(ang_col)], axis=-1)
    return (jnp.asarray(np.tile(cos, (1, ATT_KV_HEADS)), _F32), jnp.asarray(np.tile(sin, (1, ATT_KV_HEADS)), _F32))


def _prepare_weights(w_in, g_qn, g_kn, att_sink, w_gate2, b_gate2, g_gla_out, w_pool, pool_scale, w_br_a,
                     w_br_b, w_br_c, w_out, g_norm1, g_norm2, w_ff_gate, w_ff_up, w_ff_down):
    o_gl = ATT_Q + 2 * ATT_KV + 2 * GLA_QK + 2 * GLA_VW
    o_uc = o_gl + 2 * GLA_RANK
    o_gate = o_uc + POOL_W
    assert o_gl == C_UC
    w_main = w_in[:, :, :o_gl].astype(_BF16)
    w_tail = jnp.concatenate(
        [w_in[:, :, o_uc:o_gate], w_in[:, :, o_gl:o_uc],
         jnp.zeros((DEPTH, D_MODEL, GL_PAD - 2 * GLA_RANK), w_in.dtype)], axis=2).astype(_BF16)
    w_gates = w_in[:, :, o_gate:].astype(_BF16)
    wg2 = jnp.zeros((DEPTH, GL_PAD, 2 * GLA_QK), _F32)
    wg2 = wg2.at[:, 0:GLA_RANK, 0:GLA_QK].set(w_gate2[:, 0])
    wg2 = wg2.at[:, GLA_RANK:2 * GLA_RANK, GLA_QK:].set(w_gate2[:, 1])
    w_pool2 = jnp.zeros((DEPTH, POOL_GROUPS // 2, 2 * POOL_GROUP_DIM, 2 * POOL_GROUP_DIM), _F32)
    w_pool2 = w_pool2.at[:, :, :POOL_GROUP_DIM, :POOL_GROUP_DIM].set(w_pool[:, 0::2])
    w_pool2 = w_pool2.at[:, :, POOL_GROUP_DIM:, POOL_GROUP_DIM:].set(w_pool[:, 1::2])
    vec = lambda a: a.reshape(DEPTH, 1, -1)
    group_ones = lambda n, width: jnp.asarray(
        (np.arange(n)[:, None] // width) == (np.arange(n)[None, :] // width), _BF16)
    return {
        "head_ones": group_ones(MXU_TILE, HEAD_DIM),
        "gla_ones": group_ones(MXU_TILE, GLA_DV),
        "w_main": w_main,
        "w_tail": w_tail,
        "w_gates": w_gates,
        "g_qn": vec(jnp.tile(g_qn, (1, ATT_HEADS))),
        "g_kn": vec(jnp.tile(g_kn, (1, ATT_KV_HEADS))),
        "att_sink": att_sink,
        "w_gate2": wg2.astype(_BF16),
        "b_gate2": vec(b_gate2),
        "g_gla_out": vec(jnp.tile(g_gla_out, (1, GLA_HEADS))),
        "w_pool": w_pool2.astype(_BF16),
        "pool_scale": vec(pool_scale),
        "w_br_a": w_br_a.astype(_BF16),
        "w_br_b": w_br_b.astype(_BF16),
        "w_br_c": w_br_c.astype(_BF16),
        "w_out": w_out.astype(_BF16),
        "g_norm1": vec(g_norm1),
        "g_norm2": vec(g_norm2),
        "w_ff_gate": w_ff_gate.astype(_BF16),
        "w_ff_up": w_ff_up.astype(_BF16),
        "w_ff_down": w_ff_down.astype(_BF16),
    }


def kernel(x_prompt, x_sample, c, cache_k, cache_v, state_gla, c_ctx, w_in, g_qn, g_kn, att_sink, w_gate2,
           b_gate2, g_gla_out, w_pool, pool_scale, w_br_a, w_br_b, w_br_c, w_out, g_norm1, g_norm2, w_mod,
           b_mod, w_ff_gate, w_ff_up, w_ff_down):
    B, T, _ = x_prompt.shape
    BL, TL, _ = x_sample.shape
    assert (B * T) % POST_TILE == 0 and TL % POST_TILE == 0 and BL + 1 <= MOD_ROWS
    assert T % PROJ_TILE == 0 and TL % PROJ_TILE == 0
    cv = jnp.concatenate([c_ctx[None, :], c, jnp.zeros((MOD_ROWS - 1 - BL, D_MODEL), _F32)], axis=0)
    mod_all = _modulation(cv, w_mod, b_mod).reshape(DEPTH, MOD_ROWS, 1, 6 * D_MODEL)
    pw = _prepare_weights(w_in, g_qn, g_kn, att_sink, w_gate2, b_gate2, g_gla_out, w_pool, pool_scale, w_br_a,
                          w_br_b, w_br_c, w_out, g_norm1, g_norm2, w_ff_gate, w_ff_up, w_ff_down)
    cos, sin = _rope_tables(TL)
    P = cache_k.shape[2]
    latent_ctx = (cache_k.reshape(BL, DEPTH, P, ATT_KV), cache_v.reshape(BL, DEPTH, P, ATT_KV),
                  jnp.swapaxes(state_gla.reshape(BL, DEPTH, 2, GLA_HEADS // 2, 2 * GLA_DK, GLA_DV), -1, -2),
                  cos, sin)

    def layer_step(l, carry):
        yp, ys, new_k, new_v, new_st = carry
        layer = jnp.full((1,), l, jnp.int32)
        oa, ob, oc, new_k, new_v, new_st = _mix_ctx_call(layer, yp, mod_all, pw, (new_k, new_v, new_st))
        yp = _post_call(layer, yp.reshape(B * T, D_MODEL), mod_all, oa.reshape(B * T, -1), ob.reshape(B * T, -1),
                        oc.reshape(B * T, -1), pw, None).reshape(B, T, D_MODEL)
        oa, ob, oc = _mix_latent_call(layer, ys, mod_all, pw, *latent_ctx)
        ys = _post_call(layer, ys.reshape(BL * TL, D_MODEL), mod_all, oa.reshape(BL * TL, -1),
                        ob.reshape(BL * TL, -1), oc.reshape(BL * TL, -1), pw,
                        TL // POST_TILE).reshape(BL, TL, D_MODEL)
        return yp, ys, new_k, new_v, new_st

    init = (x_prompt, x_sample,
            jnp.zeros((B, DEPTH, T, ATT_KV), _F32), jnp.zeros((B, DEPTH, T, ATT_KV), _F32),
            jnp.zeros((B, DEPTH, 2, GLA_HEADS // 2, 2 * GLA_DK, GLA_DV), _F32))
    yp, ys, new_k, new_v, new_st = lax.fori_loop(0, DEPTH, layer_step, init)
    return (yp, ys, new_k.reshape(B, DEPTH, T, ATT_KV_HEADS, HEAD_DIM),
            new_v.reshape(B, DEPTH, T, ATT_KV_HEADS, HEAD_DIM),
            new_st.reshape(B, DEPTH, 2, GLA_HEADS, GLA_DK, GLA_DV))
```

```python
import functools

import jax
import jax.numpy as jnp
import numpy as np
from jax import lax
from jax.experimental import pallas as pl
from jax.experimental.pallas import tpu as pltpu

D_MODEL = 1024
DEPTH = 4
GRID_W = 64
ATT_HEADS = 8
ATT_KV_HEADS = 2
ATT_GROUP = ATT_HEADS // ATT_KV_HEADS
HEAD_DIM = 64
WINDOW = 128
ATT_BLOCK = 128
ROPE_BASE = 10000.0
GLA_HEADS = 4
GLA_DK = 64
GLA_DV = 128
GLA_RANK = 16
GLA_TAU = 16.0
GLA_CHUNK = 64
POOL_GROUPS = 4
POOL_GROUP_DIM = 128
POOL_WINDOWS = (2, 4, 8, 16)
D_FF = 2816
ATT_Q = ATT_HEADS * HEAD_DIM
ATT_KV = ATT_KV_HEADS * HEAD_DIM
GLA_QK = GLA_HEADS * GLA_DK
GLA_VW = GLA_HEADS * GLA_DV
POOL_W = POOL_GROUPS * POOL_GROUP_DIM
EPS = 1e-6
NEG = -1e30

C_QA = 0
C_KA = C_QA + ATT_Q
C_VA = C_KA + ATT_KV
C_QB = C_VA + ATT_KV
C_KB = C_QB + GLA_QK
C_VB = C_KB + GLA_QK
C_RB = C_VB + GLA_VW
C_UC = C_RB + GLA_VW
C_GL = C_UC + POOL_W
LANES = 128
SUBLANES = 8
MXU_TILE = 256
GL_PAD = LANES
MIX_W = C_GL + GL_PAD
GATE_W = 3 * D_MODEL

POST_TILE = 512
PROJ_TILE = 256
CTX_SEQS_PER_STEP = 2
POOL_TILE = 256
POOL_HALO = LANES
MOD_ROWS = SUBLANES
MOD_TILE = 3072
GLA_BLOCK = 256
GLA_MAX_EXPONENT = 80.0
VMEM_LIMIT = 56 * 1024 * 1024

_F32 = jnp.float32
_BF16 = jnp.bfloat16


def _dot(a, b):
    return jnp.dot(a, b, preferred_element_type=_F32)


def _dot_nt(a, b):
    return lax.dot_general(a, b, (((1,), (1,)), ((), ())), preferred_element_type=_F32)


def _dot_tn(a, b):
    return lax.dot_general(a, b, (((0,), (0,)), ((), ())), preferred_element_type=_F32)


def _rms_scale(x):
    return lax.rsqrt(jnp.mean(x * x, axis=-1, keepdims=True) + EPS)


def _group_rms_scale(x, group_ones, width):
    sq = (x * x).astype(_BF16)
    n, blk = x.shape[-1], group_ones.shape[0]
    if n <= blk:
        sums = _dot(sq, group_ones[0:n, 0:n])
    else:
        sums = jnp.concatenate([_dot(sq[:, i:i + blk], group_ones) for i in range(0, n, blk)], axis=-1)
    return lax.rsqrt(sums * (1.0 / width) + EPS)


def _log_sigmoid(x):
    return jnp.minimum(x, 0.0) - jnp.log(1.0 + jnp.exp(-jnp.abs(x)))


def _silu(x):
    return x * jax.nn.sigmoid(x)


def _rope(x, cos, sin_signed):
    n = x.shape[-1]
    lane = lax.broadcasted_iota(jnp.int32, x.shape, 1)
    up = pltpu.roll(x, n - HEAD_DIM // 4, axis=1)
    down = pltpu.roll(x, HEAD_DIM // 4, axis=1)
    partner = jnp.where((lane & (HEAD_DIM // 2 - 1)) < HEAD_DIM // 4, up, down)
    return x * cos + partner * sin_signed


def _mod_kernel(cv_ref, w_ref, b_ref, out_ref):
    s = _silu(cv_ref[...]).astype(_BF16)
    out_ref[0] = _dot(s, w_ref[0].astype(_BF16)) + b_ref[0]


def _modulation(cv, w_mod, b_mod):
    n_col = (6 * D_MODEL) // MOD_TILE
    return pl.pallas_call(
        _mod_kernel,
        grid=(DEPTH, n_col),
        in_specs=[
            pl.BlockSpec((MOD_ROWS, D_MODEL), lambda l, j: (0, 0)),
            pl.BlockSpec((1, D_MODEL, MOD_TILE), lambda l, j: (l, 0, j)),
            pl.BlockSpec((1, 1, MOD_TILE), lambda l, j: (l, 0, j)),
        ],
        out_specs=pl.BlockSpec((1, MOD_ROWS, MOD_TILE), lambda l, j: (l, 0, j)),
        out_shape=jax.ShapeDtypeStruct((DEPTH, MOD_ROWS, 6 * D_MODEL), _F32),
        compiler_params=pltpu.CompilerParams(vmem_limit_bytes=VMEM_LIMIT),
        name="modulation",
    )(cv, w_mod, b_mod.reshape(DEPTH, 1, 6 * D_MODEL))


def _split_heads(x):
    low = lax.broadcasted_iota(jnp.int32, x.shape, 1) < HEAD_DIM
    swapped = pltpu.roll(x, HEAD_DIM, axis=1)
    zero = jnp.zeros_like(x)
    return ((jnp.where(low, x, zero), jnp.where(low, zero, swapped)),
            (jnp.where(low, swapped, zero), jnp.where(low, zero, x)))


def _store_split_kv(k, v, ks_ref, vs_ref, rows):
    ones = jnp.ones_like(v)
    for kv, (k_sides, v_sides, one_sides) in enumerate(zip(_split_heads(k), _split_heads(v), _split_heads(ones))):
        for side in range(2):
            ks_ref[kv, side, rows, :] = k_sides[side].astype(_BF16)
            vs_ref[kv, side, rows, :] = jnp.concatenate([v_sides[side], one_sides[side]], axis=-1).astype(_BF16)


def _pair_softmax_av(qp, keys, values, masks, sink_even, sink_odd):
    m = qp.shape[0]
    scores = []
    for (k_left, k_right), mask in zip(keys, masks):
        s_even, s_odd = _dot_nt(qp, k_left), _dot_nt(qp, k_right)
        if mask is not None:
            s_even, s_odd = jnp.where(mask, s_even, NEG), jnp.where(mask, s_odd, NEG)
        scores.append((s_even, s_odd))
    m_even = jnp.full((m, 1), sink_even, _F32)
    m_odd = jnp.full((m, 1), sink_odd, _F32)
    for s_even, s_odd in scores:
        m_even = jnp.maximum(m_even, jnp.max(s_even, axis=-1, keepdims=True))
        m_odd = jnp.maximum(m_odd, jnp.max(s_odd, axis=-1, keepdims=True))
    res = None
    for (s_even, s_odd), (w_left, w_right) in zip(scores, values):
        r = (_dot(jnp.exp(s_even - m_even).astype(_BF16), w_left)
             + _dot(jnp.exp(s_odd - m_odd).astype(_BF16), w_right))
        res = r if res is None else res + r
    pair = 2 * HEAD_DIM
    low = lax.broadcasted_iota(jnp.int32, (m, pair), 1) < HEAD_DIM
    den = res[:, pair:] + jnp.where(low, jnp.exp(sink_even - m_even), jnp.exp(sink_odd - m_odd))
    return res[:, :pair] / den


def _attention_ctx(T, layer, qr_ref, ks_ref, vs_ref, sink_ref, oa_ref):
    pair = 2 * HEAD_DIM
    for kv in range(ATT_KV_HEADS):
        keys = [(ks_ref[kv, 0], ks_ref[kv, 1])]
        values = [(vs_ref[kv, 0], vs_ref[kv, 1])]
        for j in range(ATT_GROUP // 2):
            head = kv * ATT_GROUP + 2 * j
            cols = slice(head * HEAD_DIM, head * HEAD_DIM + pair)
            o = _pair_softmax_av(qr_ref[:, cols], keys, values, [None], sink_ref[layer, head],
                                 sink_ref[layer, head + 1])
            oa_ref[0, :, cols] = o.astype(_BF16)


def _attention_latent(T, layer, qr_ref, ks_ref, vs_ref, kc_ref, vc_ref, sink_ref, oa_ref):
    pair = 2 * HEAD_DIM
    span = 3 * ATT_BLOCK
    kc, vc = kc_ref[0, 0], vc_ref[0, 0]
    ones = jnp.ones_like(vc)
    ctx_keys = [tuple(side.astype(_BF16) for side in sides) for sides in _split_heads(kc)]
    ctx_values = [tuple(jnp.concatenate([v_side, one_side], axis=-1).astype(_BF16)
                        for v_side, one_side in zip(v_sides, one_sides))
                  for v_sides, one_sides in zip(_split_heads(vc), _split_heads(ones))]

    def block(i, carry):
        q_rows = pl.ds(pl.multiple_of(i * ATT_BLOCK, ATT_BLOCK), ATT_BLOCK)
        k_rows = pl.ds(pl.multiple_of(i * ATT_BLOCK, ATT_BLOCK), span)
        q_pos = i * ATT_BLOCK + lax.broadcasted_iota(jnp.int32, (ATT_BLOCK, span), 0)
        k_pos = (i - 1) * ATT_BLOCK + lax.broadcasted_iota(jnp.int32, (ATT_BLOCK, span), 1)
        valid = (jnp.abs(k_pos - q_pos) <= WINDOW) & (k_pos >= 0) & (k_pos < T)
        for kv in range(ATT_KV_HEADS):
            keys = [(ks_ref[kv, 0, k_rows, :], ks_ref[kv, 1, k_rows, :]), ctx_keys[kv]]
            values = [(vs_ref[kv, 0, k_rows, :], vs_ref[kv, 1, k_rows, :]), ctx_values[kv]]
            for j in range(ATT_GROUP // 2):
                head = kv * ATT_GROUP + 2 * j
                cols = slice(head * HEAD_DIM, head * HEAD_DIM + pair)
                o = _pair_softmax_av(qr_ref[q_rows, cols], keys, values, [valid, None], sink_ref[layer, head],
                                     sink_ref[layer, head + 1])
                oa_ref[0, q_rows, cols] = o.astype(_BF16)
        return carry

    lax.fori_loop(0, T // ATT_BLOCK, block, 0, unroll=4)


def _gla_chunk(z_ref, la_ref, o_ref, st_ref, a_ref, direction, start):
    C = GLA_CHUNK
    rows = pl.ds(pl.multiple_of(start, C), C)
    q = z_ref[rows, C_QB:C_QB + GLA_QK] * (GLA_DK ** -0.5)
    k = z_ref[rows, C_KB:C_KB + GLA_QK]
    v = z_ref[rows, C_VB:C_VB + GLA_VW].astype(_BF16)
    la = la_ref[rows, direction * GLA_QK:(direction + 1) * GLA_QK]
    la_hi = la.astype(_BF16)
    la_lo = (la - la_hi.astype(_F32)).astype(_BF16)
    ti = lax.broadcasted_iota(jnp.int32, (C, C), 0)
    si = lax.broadcasted_iota(jnp.int32, (C, C), 1)
    causal = (si <= ti) if direction == 0 else (si >= ti)
    tri = jnp.where(causal, 1.0, 0.0).astype(_BF16)
    b = _dot(tri, la_hi) + _dot(tri, la_lo)
    end = C - 1 if direction == 0 else 0
    b_end = b[end:end + 1]
    q_in = (q * jnp.exp(b)).astype(_BF16)
    k_out = (k * jnp.exp(b_end - b)).astype(_BF16)
    e_end = jnp.exp(b_end)
    ones = jnp.ones((8, GLA_DK), _BF16)

    def row_group(g, carry):
        base = pl.multiple_of(g * 8, 8)
        b8 = a_ref[GLA_HEADS, pl.ds(base, 8), :]
        q8 = a_ref[GLA_HEADS + 1, pl.ds(base, 8), :]
        s_idx = lax.broadcasted_iota(jnp.int32, (C, 1), 0)
        rows_h = [[] for _ in range(GLA_HEADS)]
        for j in range(8):
            ok = (s_idx <= base + j) if direction == 0 else (s_idx >= base + j)
            decay = jnp.exp(jnp.where(ok, b8[j:j + 1] - b, NEG))
            p = (q8[j:j + 1] * k * decay).astype(_BF16)
            for h in range(GLA_HEADS):
                rows_h[h].append(_dot_nt(ones, p[:, h * GLA_DK:(h + 1) * GLA_DK])[0:1])
        for h in range(GLA_HEADS):
            a_ref[h, pl.ds(base, 8), 0:C] = jnp.concatenate(rows_h[h], axis=0)
        return carry

    a_ref[GLA_HEADS] = b
    a_ref[GLA_HEADS + 1] = q
    lax.fori_loop(0, C // 8, row_group, 0)
    outs = []
    for pair in range(GLA_HEADS // 2):
        s_pair = st_ref[direction, pair]
        s_next = []
        for h in (2 * pair, 2 * pair + 1):
            kc = slice(h * GLA_DK, (h + 1) * GLA_DK)
            vc = slice(h * GLA_DV, (h + 1) * GLA_DV)
            s_t = s_pair[:, (h % 2) * GLA_DK:(h % 2 + 1) * GLA_DK]
            outs.append(_dot_nt(q_in[:, kc], s_t.astype(_BF16)) + _dot(a_ref[h, :, 0:C].astype(_BF16), v[:, vc]))
            s_next.append(s_t * e_end[:, kc] + _dot_tn(v[:, vc], k_out[:, kc]))
        st_ref[direction, pair] = jnp.concatenate(s_next, axis=-1)
    o_ref[rows, :] += jnp.concatenate(outs, axis=-1)


def _gla_block(z_ref, la_ref, o_ref, st_ref, direction, start, use_state):
    NB = GLA_BLOCK
    fwd = direction == 0
    rows = slice(start, start + NB) if isinstance(start, int) else pl.ds(pl.multiple_of(start, NB), NB)
    q = z_ref[rows, C_QB:C_QB + GLA_QK] * (GLA_DK ** -0.5)
    k = z_ref[rows, C_KB:C_KB + GLA_QK]
    v = z_ref[rows, C_VB:C_VB + GLA_VW].astype(_BF16)
    b = la_ref[rows, direction * GLA_QK:(direction + 1) * GLA_QK]
    end, mid = (NB - 1, NB // 2 - 1) if fwd else (0, NB // 2)
    b_end = b[end:end + 1]
    c = b - b[mid:mid + 1]
    q_c = (q * jnp.exp(c)).astype(_BF16)
    k_c = (k * jnp.exp(-c)).astype(_BF16)
    k_fin = (k * jnp.exp(b_end - b)).astype(_BF16)
    if use_state:
        q_in = (q * jnp.exp(b)).astype(_BF16)
        e_all = jnp.exp(b_end)
    ti = lax.broadcasted_iota(jnp.int32, (NB, NB), 0)
    si = lax.broadcasted_iota(jnp.int32, (NB, NB), 1)
    causal = (si <= ti) if fwd else (si >= ti)
    tile = 2 * GLA_DK
    low = lax.broadcasted_iota(jnp.int32, (NB, tile), 1) < GLA_DK
    zero = jnp.zeros((NB, tile), _BF16)
    pick = lambda x, parity: jnp.where(low, x, zero) if parity == 0 else jnp.where(low, zero, x)
    outs = []
    for pair in range(GLA_HEADS // 2):
        lanes = slice(pair * tile, (pair + 1) * tile)
        if use_state:
            s_pair = st_ref[direction, pair]
            s_bf = s_pair.astype(_BF16)
        s_new = None
        for parity in range(2):
            h = 2 * pair + parity
            vc = slice(h * GLA_DV, (h + 1) * GLA_DV)
            a = jnp.where(causal, _dot_nt(q_c[:, lanes], pick(k_c[:, lanes], parity)), 0.0)
            o_h = _dot(a.astype(_BF16), v[:, vc])
            if use_state:
                o_h = o_h + _dot_nt(pick(q_in[:, lanes], parity), s_bf)
            outs.append(o_h)
            upd = _dot_tn(v[:, vc], pick(k_fin[:, lanes], parity))
            s_new = upd if s_new is None else s_new + upd
        if use_state:
            s_new = s_new + s_pair * e_all[:, lanes]
        st_ref[direction, pair] = s_new
    o_ref[rows, :] += jnp.concatenate(outs, axis=-1)


def _log_decay(z_ref, rows, wg2_ref, bg2_ref):
    pre = _dot(z_ref[rows, C_GL:C_GL + GL_PAD].astype(_BF16), wg2_ref[...]) + bg2_ref[...]
    return _log_sigmoid(pre) * (1.0 / GLA_TAU)


def _gla_prepare(T, z_ref, la_ref, o_ref, wg2_ref, bg2_ref, tri_ref):
    NB = GLA_BLOCK
    worst = None
    for r0 in range(0, T, NB):
        rows = slice(r0, r0 + NB)
        la = _log_decay(z_ref, rows, wg2_ref, bg2_ref)
        la_hi = la.astype(_BF16)
        la_lo = (la - la_hi.astype(_F32)).astype(_BF16)
        for d in range(2):
            cols = slice(d * GLA_QK, (d + 1) * GLA_QK)
            b = _dot(tri_ref[d], la_hi[:, cols]) + _dot(tri_ref[d], la_lo[:, cols])
            la_ref[rows, cols] = b
            first, mid, last = (0, NB // 2 - 1, NB - 1) if d == 0 else (NB - 1, NB // 2, 0)
            span = jnp.max(jnp.maximum(b[first:first + 1] - b[mid:mid + 1], b[mid:mid + 1] - b[last:last + 1]))
            worst = span if worst is None else jnp.maximum(worst, span)
    o_ref[...] = jnp.zeros(o_ref.shape, _F32)
    return worst


def _gla_run(T, seqs, worst, a_ref, wg2_ref, bg2_ref, use_state):
    NB = GLA_BLOCK
    n_chunks = T // GLA_CHUNK
    n_blocks = T // NB
    fast_ok = worst <= GLA_MAX_EXPONENT

    @pl.when(fast_ok)
    def _():
        if n_blocks == 1:
            for direction in range(2):
                for z_ref, la_ref, o_ref, st_ref in seqs:
                    _gla_block(z_ref, la_ref, o_ref, st_ref, direction, 0, use_state)
        else:
            def body(i, carry):
                for z_ref, la_ref, o_ref, st_ref in seqs:
                    _gla_block(z_ref, la_ref, o_ref, st_ref, 0, i * NB, True)
                    _gla_block(z_ref, la_ref, o_ref, st_ref, 1, (n_blocks - 1 - i) * NB, True)
                return carry
            lax.fori_loop(0, n_blocks, body, 0, unroll=2)

    @pl.when(jnp.logical_not(fast_ok))
    def _():
        for z_ref, la_ref, o_ref, st_ref in seqs:
            for r0 in range(0, T, NB):
                la_ref[r0:r0 + NB, :] = _log_decay(z_ref, slice(r0, r0 + NB), wg2_ref, bg2_ref)

            def body(i, carry):
                _gla_chunk(z_ref, la_ref, o_ref, st_ref, a_ref, 0, i * GLA_CHUNK)
                _gla_chunk(z_ref, la_ref, o_ref, st_ref, a_ref, 1, (n_chunks - 1 - i) * GLA_CHUNK)
                return carry
            lax.fori_loop(0, n_chunks, body, 0)


def _gla_finish(T, z_ref, o_ref, ggla_ref, vones_ref, ob_ref):
    for r0 in range(0, T, PROJ_TILE):
        rows = slice(r0, r0 + PROJ_TILE)
        o = o_ref[rows, :]
        y = o * _group_rms_scale(o, vones_ref[...], GLA_DV) * ggla_ref[...]
        ob_ref[0, rows, :] = (y * _silu(z_ref[rows, C_RB:C_RB + GLA_VW])).astype(_BF16)


def _pool_halo(T):
    return POOL_HALO if T > POOL_TILE else 0


def _pool_bands(T):
    halo = _pool_halo(T)
    off = np.arange(POOL_TILE + 2 * halo)[None, :] - halo - np.arange(POOL_TILE)[:, None]
    return jnp.asarray(np.stack([(off >= -(w // 2)) & (off < w - w // 2) for w in POOL_WINDOWS]), _BF16)


def _scan_masks():
    t = np.arange(GLA_BLOCK)
    return jnp.asarray(np.stack([t[None, :] <= t[:, None], t[None, :] >= t[:, None]]), _BF16)


def _pool_inverse_counts(T):
    t = np.arange(T)
    cols = [np.repeat((1.0 / (np.minimum(t - w // 2 + w, T) - np.maximum(t - w // 2, 0)))[:, None],
                      POOL_GROUP_DIM, axis=1) for w in POOL_WINDOWS]
    return jnp.asarray(np.concatenate(cols, axis=1), _F32)


def _pool(T, z_ref, upad_ref, wpool_ref, pscale_ref, bands_ref, icnt_ref, oc_ref):
    halo = _pool_halo(T)
    if halo:
        zeros = jnp.zeros((halo, POOL_W), _BF16)
        upad_ref[0:halo, :] = zeros
        upad_ref[halo + T:halo + T + halo, :] = zeros
    upad_ref[halo:halo + T, :] = z_ref[:, C_UC:C_UC + POOL_W].astype(_BF16)
    span = POOL_TILE + 2 * halo
    group = lambda g: slice(g * POOL_GROUP_DIM, (g + 1) * POOL_GROUP_DIM)
    for jb in range(T // POOL_TILE):
        rows = slice(jb * POOL_TILE, (jb + 1) * POOL_TILE)
        win = slice(jb * POOL_TILE, jb * POOL_TILE + span)
        total = jnp.concatenate([_dot(bands_ref[g], upad_ref[win, group(g)]) for g in range(POOL_GROUPS)], axis=-1)
        pooled = (total * icnt_ref[rows, :] - z_ref[rows, C_UC:C_UC + POOL_W]).astype(_BF16)
        pair = 2 * POOL_GROUP_DIM
        y = jnp.concatenate([_dot(pooled[:, p * pair:(p + 1) * pair], wpool_ref[p])
                             for p in range(POOL_GROUPS // 2)], axis=-1)
        oc_ref[0, rows, :] = (y * pscale_ref[...]).astype(_BF16)


N_MIX_PARAMS = 18
N_MIX_SCRATCH = 9


def _mix_body(latent, T, layer, params, latent_refs, out_refs, scratch):
    (x_ref, mod_ref, gn1_ref, wmain_ref, wtail_ref, gqn_ref, gkn_ref, sink_ref, wg2_ref, bg2_ref, ggla_ref,
     wpool_ref, pscale_ref, hones_ref, vones_ref, bands_ref, icnt_ref, tri_ref) = params
    n_seq = x_ref.shape[0]
    one = lambda ref, s: ref.at[pl.ds(s, 1)]
    a_ref = scratch[N_MIX_SCRATCH - 2]
    per_seq = [tuple(ref.at[s] for ref in scratch[:N_MIX_SCRATCH - 2] + scratch[N_MIX_SCRATCH - 1:])
               for s in range(n_seq)]
    pad = ATT_BLOCK if latent else 0
    shift = mod_ref[0, :, 0:D_MODEL]
    scale = mod_ref[0, :, D_MODEL:2 * D_MODEL]

    for s, (z_ref, qr_ref, ks_ref, vs_ref, la_ref, o_ref, st_ref, upad_ref) in enumerate(per_seq):
        if latent:
            for ref in (ks_ref, vs_ref):
                zeros = jnp.zeros(ref.shape[:2] + (pad, ref.shape[3]), _BF16)
                ref[:, :, 0:pad, :] = zeros
                ref[:, :, pad + T:pad + T + pad, :] = zeros
        for r0 in range(0, T, PROJ_TILE):
            rows = slice(r0, r0 + PROJ_TILE)
            x = x_ref[s, rows, :]
            hn = (x * _rms_scale(x) * gn1_ref[...]) * (1.0 + scale) + shift
            hn = hn.astype(_BF16)
            z_ref[rows, 0:C_UC] = _dot(hn, wmain_ref[...])
            z_ref[rows, C_UC:MIX_W] = _dot(hn, wtail_ref[...])
            q = z_ref[rows, C_QA:C_QA + ATT_Q]
            k = z_ref[rows, C_KA:C_KA + ATT_KV]
            q = q * _group_rms_scale(q, hones_ref[...], HEAD_DIM) * gqn_ref[...]
            k = k * _group_rms_scale(k, hones_ref[...], HEAD_DIM) * gkn_ref[...]
            v = z_ref[rows, C_VA:C_VA + ATT_KV]
            if latent:
                cos_ref, sin_ref = latent_refs[3], latent_refs[4]
                cos = jnp.concatenate([cos_ref[rows, :]] * (ATT_Q // ATT_KV), axis=-1)
                sin = jnp.concatenate([sin_ref[rows, :]] * (ATT_Q // ATT_KV), axis=-1)
                q = _rope(q, cos, sin)
                k = _rope(k, cos_ref[rows, :], sin_ref[rows, :])
            else:
                out_refs[3][s, rows, :] = k
                out_refs[4][s, rows, :] = v
            qr_ref[rows, :] = (q * (HEAD_DIM ** -0.5)).astype(_BF16)
            _store_split_kv(k, v, ks_ref, vs_ref, slice(pad + r0, pad + r0 + PROJ_TILE))

    for s, (z_ref, qr_ref, ks_ref, vs_ref, la_ref, o_ref, st_ref, upad_ref) in enumerate(per_seq):
        if latent:
            _attention_latent(T, layer, qr_ref, ks_ref, vs_ref, one(latent_refs[0], s), one(latent_refs[1], s),
                              sink_ref, one(out_refs[0], s))
        else:
            _attention_ctx(T, layer, qr_ref, ks_ref, vs_ref, sink_ref, one(out_refs[0], s))

    worst = None
    for s, (z_ref, qr_ref, ks_ref, vs_ref, la_ref, o_ref, st_ref, upad_ref) in enumerate(per_seq):
        if latent:
            st_ref[...] = latent_refs[2][s]
        else:
            st_ref[...] = jnp.zeros(st_ref.shape, _F32)
        span = _gla_prepare(T, z_ref, la_ref, o_ref, wg2_ref, bg2_ref, tri_ref)
        worst = span if worst is None else jnp.maximum(worst, span)
    _gla_run(T, [(z_ref, la_ref, o_ref, st_ref) for z_ref, _, _, _, la_ref, o_ref, st_ref, _ in per_seq],
             worst, a_ref, wg2_ref, bg2_ref, latent)

    for s, (z_ref, qr_ref, ks_ref, vs_ref, la_ref, o_ref, st_ref, upad_ref) in enumerate(per_seq):
        _gla_finish(T, z_ref, o_ref, ggla_ref, vones_ref, one(out_refs[1], s))
        if not latent:
            for d in range(2):
                for pair in range(GLA_HEADS // 2):
                    out_refs[5][s, d, pair] = st_ref[d, pair].T
        _pool(T, z_ref, upad_ref, wpool_ref, pscale_ref, bands_ref, icnt_ref, one(out_refs[2], s))


def _merge(x, mod_ref, oa, ob, oc, gn1_ref, wgate_ref, wa_ref, wb_ref, wc_ref, wout_ref, before_branches=None):
    mod = lambda i: mod_ref[0, :, i * D_MODEL:(i + 1) * D_MODEL]
    hn = (x * _rms_scale(x) * gn1_ref[...]) * (1.0 + mod(1)) + mod(0)
    gates = jax.nn.sigmoid(_dot(hn.astype(_BF16), wgate_ref[...]))
    if before_branches is not None:
        before_branches()
    mixed = (gates[:, 0:D_MODEL] * _dot(oa, wa_ref[...])
             + gates[:, D_MODEL:2 * D_MODEL] * _dot(ob, wb_ref[...])
             + gates[:, 2 * D_MODEL:3 * D_MODEL] * _dot(oc, wc_ref[...]))
    return x + mod(2) * _dot(mixed.astype(_BF16), wout_ref[...])


def _ffn(x, mod_ref, gn2_ref, wfg_ref, wfu_ref, wfd_ref):
    mod = lambda i: mod_ref[0, :, i * D_MODEL:(i + 1) * D_MODEL]
    hn = ((x * _rms_scale(x) * gn2_ref[...]) * (1.0 + mod(4)) + mod(3)).astype(_BF16)
    h = _silu(_dot(hn, wfg_ref[...])) * _dot(hn, wfu_ref[...])
    return x + mod(5) * _dot(h.astype(_BF16), wfd_ref[...])


def _mix_latent_kernel(T, layer_ref, *refs):
    params, refs = refs[:N_MIX_PARAMS], refs[N_MIX_PARAMS:]
    _mix_body(True, T, layer_ref[0], params, refs[:5], refs[5:8], refs[8:])


def _mix_ctx_kernel(T, layer_ref, *refs):
    params, refs = refs[:N_MIX_PARAMS], refs[N_MIX_PARAMS + 3:]
    _mix_body(False, T, layer_ref[0], params, None, refs[:6], refs[6:])


def _layer_spec(shape):
    zeros = (0,) * len(shape)
    return pl.BlockSpec((None,) + tuple(shape), lambda i, layer: (layer[0],) + zeros,
                        pipeline_mode=pl.Buffered(1))


def _const_spec(blk):
    return pl.BlockSpec(blk, lambda i, layer: (0,) * len(blk), pipeline_mode=pl.Buffered(1))


def _mix_params(x, x_spec, mod_spec, mod_all, pw):
    bands = _pool_bands(x.shape[1])
    specs = [
        x_spec, mod_spec,
        _layer_spec((1, D_MODEL)),
        _layer_spec((D_MODEL, C_UC)),
        _layer_spec((D_MODEL, MIX_W - C_UC)),
        _layer_spec((1, ATT_Q)), _layer_spec((1, ATT_KV)),
        pl.BlockSpec(memory_space=pltpu.SMEM),
        _layer_spec((GL_PAD, 2 * GLA_QK)), _layer_spec((1, 2 * GLA_QK)), _layer_spec((1, GLA_VW)),
        _layer_spec((POOL_GROUPS // 2, 2 * POOL_GROUP_DIM, 2 * POOL_GROUP_DIM)), _layer_spec((1, POOL_W)),
        _const_spec((MXU_TILE, MXU_TILE)), _const_spec((MXU_TILE, MXU_TILE)),
        _const_spec(bands.shape), _const_spec((x.shape[1], POOL_W)), _const_spec((2, GLA_BLOCK, GLA_BLOCK)),
    ]
    args = [x, mod_all, pw["g_norm1"], pw["w_main"], pw["w_tail"], pw["g_qn"], pw["g_kn"], pw["att_sink"], pw["w_gate2"],
            pw["b_gate2"], pw["g_gla_out"], pw["w_pool"], pw["pool_scale"], pw["head_ones"], pw["gla_ones"], bands,
            _pool_inverse_counts(x.shape[1]), _scan_masks()]
    assert len(specs) == len(args) == N_MIX_PARAMS
    return specs, args


def _mix_scratch(S, T, kv_rows):
    scratch = [
        pltpu.VMEM((S, T, MIX_W), _F32),
        pltpu.VMEM((S, T, ATT_Q), _BF16),
        pltpu.VMEM((S, ATT_KV_HEADS, 2, kv_rows, 2 * HEAD_DIM), _BF16),
        pltpu.VMEM((S, ATT_KV_HEADS, 2, kv_rows, 4 * HEAD_DIM), _BF16),
        pltpu.VMEM((S, T, 2 * GLA_QK), _F32),
        pltpu.VMEM((S, T, GLA_VW), _F32),
        pltpu.VMEM((S, 2, GLA_HEADS // 2, GLA_DV, 2 * GLA_DK), _F32),
        pltpu.VMEM((GLA_HEADS + 2, GLA_CHUNK, GLA_QK), _F32),
        pltpu.VMEM((S, T + 2 * _pool_halo(T), POOL_W), _BF16),
    ]
    assert len(scratch) == N_MIX_SCRATCH
    return scratch


def _mix_latent_call(layer, x, mod_all, pw, cache_k, cache_v, st0, cos, sin):
    B, T, _ = x.shape
    per_seq = lambda blk: pl.BlockSpec(blk, lambda b, layer: (b,) + (0,) * (len(blk) - 1))
    mod_spec = pl.BlockSpec((None, 1, 1, 6 * D_MODEL), lambda b, layer: (layer[0], b + 1, 0, 0))
    in_specs, args = _mix_params(x, per_seq((1, T, D_MODEL)), mod_spec, mod_all, pw)
    P = cache_k.shape[2]
    cache_spec = pl.BlockSpec((1, 1, P, ATT_KV), lambda b, layer: (b, layer[0], 0, 0))
    in_specs += [cache_spec, cache_spec,
                 pl.BlockSpec((1, None, 2, GLA_HEADS // 2, GLA_DV, 2 * GLA_DK),
                              lambda b, layer: (b, layer[0], 0, 0, 0, 0)),
                 _const_spec((T, ATT_KV)), _const_spec((T, ATT_KV))]
    args += [cache_k, cache_v, st0, cos, sin]
    widths = (ATT_Q, GLA_VW, POOL_W)
    return pl.pallas_call(
        functools.partial(_mix_latent_kernel, T),
        grid_spec=pltpu.PrefetchScalarGridSpec(
            num_scalar_prefetch=1, grid=(B,), in_specs=in_specs,
            out_specs=[per_seq((1, T, w)) for w in widths],
            scratch_shapes=_mix_scratch(1, T, T + 2 * ATT_BLOCK)),
        out_shape=[jax.ShapeDtypeStruct((B, T, w), _BF16) for w in widths],
        compiler_params=pltpu.CompilerParams(dimension_semantics=("arbitrary",), vmem_limit_bytes=VMEM_LIMIT),
        name="mix_latent",
    )(layer, *args)


def _mix_ctx_call(layer, x, mod_all, pw, stacked):
    B, T, _ = x.shape
    S = CTX_SEQS_PER_STEP
    assert B % S == 0
    per_seq = lambda blk: pl.BlockSpec(blk, lambda b, layer: (b,) + (0,) * (len(blk) - 1))
    mod_spec = pl.BlockSpec((None, 1, 1, 6 * D_MODEL), lambda b, layer: (layer[0], 0, 0, 0))
    in_specs, args = _mix_params(x, per_seq((S, T, D_MODEL)), mod_spec, mod_all, pw)
    widths = (ATT_Q, GLA_VW, POOL_W)
    n_in = 1 + len(args)
    aliases = {n_in + j: len(widths) + j for j in range(len(stacked))}
    in_specs += [pl.BlockSpec(memory_space=pl.ANY)] * len(stacked)
    args += list(stacked)
    at_layer = lambda blk: pl.BlockSpec((S, None) + blk, lambda b, layer: (b, layer[0]) + (0,) * len(blk))
    out_specs = [per_seq((S, T, w)) for w in widths] + [
        at_layer((T, ATT_KV)), at_layer((T, ATT_KV)), at_layer((2, GLA_HEADS // 2, 2 * GLA_DK, GLA_DV))]
    out_shape = ([jax.ShapeDtypeStruct((B, T, w), _BF16) for w in widths]
                 + [jax.ShapeDtypeStruct(a.shape, a.dtype) for a in stacked])
    return pl.pallas_call(
        functools.partial(_mix_ctx_kernel, T),
        grid_spec=pltpu.PrefetchScalarGridSpec(
            num_scalar_prefetch=1, grid=(B // S,), in_specs=in_specs, out_specs=out_specs,
            scratch_shapes=_mix_scratch(S, T, T)),
        out_shape=out_shape,
        input_output_aliases=aliases,
        compiler_params=pltpu.CompilerParams(dimension_semantics=("arbitrary",), vmem_limit_bytes=VMEM_LIMIT),
        name="mix_ctx",
    )(layer, *args)


def _post_kernel(layer_ref, x_ref, mod_ref, oa_ref, ob_ref, oc_ref, gn1_ref, gn2_ref, wgate_ref, wa_hbm, wb_hbm,
                 wc_hbm, wout_hbm, wfg_hbm, wfu_hbm, wfd_hbm, out_ref, wa_ref, wb_ref, wc_ref, wout_ref, wfg_ref,
                 wfu_ref, wfd_ref, sems):
    layer = layer_ref[0]
    first = pl.program_id(0) == 0
    pairs = ((wa_hbm, wa_ref), (wb_hbm, wb_ref), (wc_hbm, wc_ref), (wout_hbm, wout_ref),
             (wfg_hbm, wfg_ref), (wfu_hbm, wfu_ref), (wfd_hbm, wfd_ref))
    copies = [pltpu.make_async_copy(hbm.at[layer], buf, sems.at[j]) for j, (hbm, buf) in enumerate(pairs)]
    n_merge = 4

    @pl.when(first)
    def _():
        for copy in copies:
            copy.start()

    def wait_merge_weights():
        @pl.when(first)
        def _():
            for copy in copies[:n_merge]:
                copy.wait()

    x = _merge(x_ref[...], mod_ref, oa_ref[...], ob_ref[...], oc_ref[...], gn1_ref, wgate_ref, wa_ref, wb_ref,
               wc_ref, wout_ref, before_branches=wait_merge_weights)

    @pl.when(first)
    def _():
        for copy in copies[n_merge:]:
            copy.wait()

    out_ref[...] = _ffn(x, mod_ref, gn2_ref, wfg_ref, wfu_ref, wfd_ref)


def _post_call(layer, x2d, mod_all, oa, ob, oc, pw, tiles_per_seq):
    row = lambda w: pl.BlockSpec((POST_TILE, w), lambda i, layer: (i, 0))
    if tiles_per_seq is None:
        mod_spec = pl.BlockSpec((None, 1, 1, 6 * D_MODEL), lambda i, layer: (layer[0], 0, 0, 0))
    else:
        mod_spec = pl.BlockSpec((None, 1, 1, 6 * D_MODEL),
                                lambda i, layer: (layer[0], 1 + i // tiles_per_seq, 0, 0))
    weights = [(pw["g_norm1"], (1, D_MODEL)), (pw["g_norm2"], (1, D_MODEL)), (pw["w_gates"], (D_MODEL, GATE_W)),
               (pw["w_br_a"], (ATT_Q, D_MODEL)), (pw["w_br_b"], (GLA_VW, D_MODEL)),
               (pw["w_br_c"], (POOL_W, D_MODEL)), (pw["w_out"], (D_MODEL, D_MODEL)),
               (pw["w_ff_gate"], (D_MODEL, D_FF)), (pw["w_ff_up"], (D_MODEL, D_FF)),
               (pw["w_ff_down"], (D_FF, D_MODEL))]
    n_manual = 7
    in_specs = ([row(D_MODEL), mod_spec, row(ATT_Q), row(GLA_VW), row(POOL_W)]
                + [_layer_spec(shape) for _, shape in weights[:-n_manual]]
                + [pl.BlockSpec(memory_space=pl.ANY)] * n_manual)
    scratch = [pltpu.VMEM(shape, _BF16) for _, shape in weights[-n_manual:]] + [pltpu.SemaphoreType.DMA((n_manual,))]
    return pl.pallas_call(
        _post_kernel,
        grid_spec=pltpu.PrefetchScalarGridSpec(
            num_scalar_prefetch=1, grid=(x2d.shape[0] // POST_TILE,), in_specs=in_specs, out_specs=row(D_MODEL),
            scratch_shapes=scratch),
        out_shape=jax.ShapeDtypeStruct(x2d.shape, _F32),
        input_output_aliases={1: 0},
        compiler_params=pltpu.CompilerParams(dimension_semantics=("arbitrary",), vmem_limit_bytes=VMEM_LIMIT),
        name="post",
    )(layer, x2d, mod_all, oa, ob, oc, *[a for a, _ in weights])


def _rope_tables(T):
    quarter = HEAD_DIM // 4
    inv_freq = ROPE_BASE ** (-np.arange(quarter, dtype=np.float32) / quarter)
    pos = np.arange(T)
    ang_row = (pos // GRID_W).astype(np.float32)[:, None] * inv_freq[None, :]
    ang_col = (pos % GRID_W).astype(np.float32)[:, None] * inv_freq[None, :]
    cos = np.concatenate([np.cos(ang_row)] * 2 + [np.cos(ang_col)] * 2, axis=-1)
    sin = np.concatenate([-np.sin(ang_row), np.sin(ang_row), -np.sin(ang_col), np.sin(ang_col)], axis=-1)
    return (jnp.asarray(np.tile(cos, (1, ATT_KV_HEADS)), _F32), jnp.asarray(np.tile(sin, (1, ATT_KV_HEADS)), _F32))


def _prepare_weights(w_in, g_qn, g_kn, att_sink, w_gate2, b_gate2, g_gla_out, w_pool, pool_scale, w_br_a,
                     w_br_b, w_br_c, w_out, g_norm1, g_norm2, w_ff_gate, w_ff_up, w_ff_down):
    o_gl = ATT_Q + 2 * ATT_KV + 2 * GLA_QK + 2 * GLA_VW
    o_uc = o_gl + 2 * GLA_RANK
    o_gate = o_uc + POOL_W
    assert o_gl == C_UC
    w_main = w_in[:, :, :o_gl].astype(_BF16)
    w_tail = jnp.concatenate(
        [w_in[:, :, o_uc:o_gate], w_in[:, :, o_gl:o_uc],
         jnp.zeros((DEPTH, D_MODEL, GL_PAD - 2 * GLA_RANK), w_in.dtype)], axis=2).astype(_BF16)
    w_gates = w_in[:, :, o_gate:].astype(_BF16)
    wg2 = jnp.zeros((DEPTH, GL_PAD, 2 * GLA_QK), _F32)
    wg2 = wg2.at[:, 0:GLA_RANK, 0:GLA_QK].set(w_gate2[:, 0])
    wg2 = wg2.at[:, GLA_RANK:2 * GLA_RANK, GLA_QK:].set(w_gate2[:, 1])
    w_pool2 = jnp.zeros((DEPTH, POOL_GROUPS // 2, 2 * POOL_GROUP_DIM, 2 * POOL_GROUP_DIM), _F32)
    w_pool2 = w_pool2.at[:, :, :POOL_GROUP_DIM, :POOL_GROUP_DIM].set(w_pool[:, 0::2])
    w_pool2 = w_pool2.at[:, :, POOL_GROUP_DIM:, POOL_GROUP_DIM:].set(w_pool[:, 1::2])
    vec = lambda a: a.reshape(DEPTH, 1, -1)
    group_ones = lambda n, width: jnp.asarray(
        (np.arange(n)[:, None] // width) == (np.arange(n)[None, :] // width), _BF16)
    return {
        "head_ones": group_ones(MXU_TILE, HEAD_DIM),
        "gla_ones": group_ones(MXU_TILE, GLA_DV),
        "w_main": w_main,
        "w_tail": w_tail,
        "w_gates": w_gates,
        "g_qn": vec(jnp.tile(g_qn, (1, ATT_HEADS))),
        "g_kn": vec(jnp.tile(g_kn, (1, ATT_KV_HEADS))),
        "att_sink": att_sink,
        "w_gate2": wg2.astype(_BF16),
        "b_gate2": vec(b_gate2),
        "g_gla_out": vec(jnp.tile(g_gla_out, (1, GLA_HEADS))),
        "w_pool": w_pool2.astype(_BF16),
        "pool_scale": vec(pool_scale),
        "w_br_a": w_br_a.astype(_BF16),
        "w_br_b": w_br_b.astype(_BF16),
        "w_br_c": w_br_c.astype(_BF16),
        "w_out": w_out.astype(_BF16),
        "g_norm1": vec(g_norm1),
        "g_norm2": vec(g_norm2),
        "w_ff_gate": w_ff_gate.astype(_BF16),
        "w_ff_up": w_ff_up.astype(_BF16),
        "w_ff_down": w_ff_down.astype(_BF16),
    }


def kernel(x_prompt, x_sample, c, cache_k, cache_v, state_gla, c_ctx, w_in, g_qn, g_kn, att_sink, w_gate2,
           b_gate2, g_gla_out, w_pool, pool_scale, w_br_a, w_br_b, w_br_c, w_out, g_norm1, g_norm2, w_mod,
           b_mod, w_ff_gate, w_ff_up, w_ff_down):
    B, T, _ = x_prompt.shape
    BL, TL, _ = x_sample.shape
    assert (B * T) % POST_TILE == 0 and TL % POST_TILE == 0 and BL + 1 <= MOD_ROWS
    assert T % PROJ_TILE == 0 and TL % PROJ_TILE == 0
    cv = jnp.concatenate([c_ctx[None, :], c, jnp.zeros((MOD_ROWS - 1 - BL, D_MODEL), _F32)], axis=0)
    mod_all = _modulation(cv, w_mod, b_mod).reshape(DEPTH, MOD_ROWS, 1, 6 * D_MODEL)
    pw = _prepare_weights(w_in, g_qn, g_kn, att_sink, w_gate2, b_gate2, g_gla_out, w_pool, pool_scale, w_br_a,
                          w_br_b, w_br_c, w_out, g_norm1, g_norm2, w_ff_gate, w_ff_up, w_ff_down)
    cos, sin = _rope_tables(TL)
    P = cache_k.shape[2]
    latent_ctx = (cache_k.reshape(BL, DEPTH, P, ATT_KV), cache_v.reshape(BL, DEPTH, P, ATT_KV),
                  jnp.swapaxes(state_gla.reshape(BL, DEPTH, 2, GLA_HEADS // 2, 2 * GLA_DK, GLA_DV), -1, -2),
                  cos, sin)

    def layer_step(l, carry):
        yp, ys, new_k, new_v, new_st = carry
        layer = jnp.full((1,), l, jnp.int32)
        oa, ob, oc, new_k, new_v, new_st = _mix_ctx_call(layer, yp, mod_all, pw, (new_k, new_v, new_st))
        yp = _post_call(layer, yp.reshape(B * T, D_MODEL), mod_all, oa.reshape(B * T, -1), ob.reshape(B * T, -1),
                        oc.reshape(B * T, -1), pw, None).reshape(B, T, D_MODEL)
        oa, ob, oc = _mix_latent_call(layer, ys, mod_all, pw, *latent_ctx)
        ys = _post_call(layer, ys.reshape(BL * TL, D_MODEL), mod_all, oa.reshape(BL * TL, -1),
                        ob.reshape(BL * TL, -1), oc.reshape(BL * TL, -1), pw,
                        TL // POST_TILE).reshape(BL, TL, D_MODEL)
        return yp, ys, new_k, new_v, new_st

    init = (x_prompt, x_sample,
            jnp.zeros((B, DEPTH, T, ATT_KV), _F32), jnp.zeros((B, DEPTH, T, ATT_KV), _F32),
            jnp.zeros((B, DEPTH, 2, GLA_HEADS // 2, 2 * GLA_DK, GLA_DV), _F32))
    yp, ys, new_k, new_v, new_st = lax.fori_loop(0, DEPTH, layer_step, init)
    return (yp, ys, new_k.reshape(B, DEPTH, T, ATT_KV_HEADS, HEAD_DIM),
            new_v.reshape(B, DEPTH, T, ATT_KV_HEADS, HEAD_DIM),
            new_st.reshape(B, DEPTH, 2, GLA_HEADS, GLA_DK, GLA_DV))
```
